```python
import math
import numpy as np
import jax
import jax.numpy as jnp
from jax import lax

D_MODEL = 2048
BATCH = 8
SEQ = 2048
DEPTH = 1
DEC_BATCH = 32
DEC_SEQ = 4
PAST_LEN = 8192
PAGE_SIZE = 128

HEAD_DIM = 128
N_ATTN_HEADS = 8
N_KV_HEADS = 2
N_DELTA_HEADS = 8
ATTN_WIDTH = N_ATTN_HEADS * HEAD_DIM
KV_WIDTH = N_KV_HEADS * HEAD_DIM
DELTA_WIDTH = N_DELTA_HEADS * HEAD_DIM
MIX_WIDTH = ATTN_WIDTH + DELTA_WIDTH
N_IDX_HEADS = 16
IDX_DIM = 64
TOPK_MAX = 256
Q_BLOCK = 128
ROPE_THETA = 500000.0
ROPE_FRACTION = 4
CONV_WIDTH = 4
CONV_CHANNELS = 3 * DELTA_WIDTH
DELTA_CHUNK = 64
N_GROUPS = 8
EXPERTS_PER_GROUP = 8
N_EXPERTS = N_GROUPS * EXPERTS_PER_GROUP
TOP_K_IN_GROUP = 2
D_EXPERT = 512
MOE_BLOCK = 128
N_MOD = 6
EPS = 1e-6
PROJ_SIZES = (ATTN_WIDTH, KV_WIDTH, KV_WIDTH, N_IDX_HEADS * IDX_DIM, IDX_DIM, N_IDX_HEADS,
              DELTA_WIDTH, DELTA_WIDTH, DELTA_WIDTH, DELTA_WIDTH, N_DELTA_HEADS, N_DELTA_HEADS)
PROJ_WIDTH = sum(PROJ_SIZES)

kernel_name = "hymba_dsa_gdn_hmoe_step"

F32 = jnp.float32


def rms_norm(x, w):
    xf = x.astype(F32)
    y = xf * lax.rsqrt(jnp.mean(xf * xf, axis=-1, keepdims=True) + EPS)
    return (y * w.astype(F32)).astype(x.dtype)


def l2_norm(x):
    xf = x.astype(F32)
    return (xf * lax.rsqrt(jnp.sum(xf * xf, axis=-1, keepdims=True) + EPS)).astype(x.dtype)


def partial_rope(x, pos):
    d_rot = x.shape[-1] // ROPE_FRACTION
    half = d_rot // 2
    inv_freq = jnp.power(ROPE_THETA, -(jnp.arange(half, dtype=F32) * 2.0 / d_rot))
    ang = pos.astype(F32)[:, None] * inv_freq[None, :]
    cos = jnp.cos(ang)[None, :, None, :]
    sin = jnp.sin(ang)[None, :, None, :]
    xf = x.astype(F32)
    x1 = xf[..., :half]
    x2 = xf[..., half:d_rot]
    out = jnp.concatenate([x1 * cos - x2 * sin, x2 * cos + x1 * sin, xf[..., d_rot:]], axis=-1)
    return out.astype(x.dtype)


def split_projection(proj):
    bounds = np.cumsum(PROJ_SIZES)[:-1].tolist()
    return jnp.split(proj, bounds, axis=-1)


def ada_modulation(c, w_ada, b_ada):
    m = jax.nn.silu(c) @ w_ada + b_ada
    return jnp.split(m[:, None, :], N_MOD, axis=-1)


def modulate(x, norm_w, shift, scale):
    return rms_norm(x, norm_w) * (1.0 + scale) + shift


def gather_pages(cache, page_table):
    g = cache[page_table]
    return g.reshape(page_table.shape[0], -1, *cache.shape[2:])


def dsa_attend(q, q_idx, w_idx, k_all, v_all, kidx_all, q_pos, n_sel):
    b, tq = q.shape[:2]
    l = k_all.shape[1]
    s_idx = jnp.einsum('bqhd,bsd->bqhs', q_idx, kidx_all).astype(F32)
    scores = jnp.einsum('bqhs,bqh->bqs', jax.nn.relu(s_idx), w_idx.astype(F32))
    admissible = jnp.arange(l)[None, :] <= q_pos[:, None]
    scores = jnp.where(admissible[None], scores, -jnp.inf)
    _, sel = lax.top_k(scores, n_sel)
    sel_ok = sel <= q_pos[None, :, None]
    take = jax.vmap(lambda a, i: a[i])
    k_sel = take(k_all, sel)
    v_sel = take(v_all, sel)
    qg = q.reshape(b, tq, N_KV_HEADS, N_ATTN_HEADS // N_KV_HEADS, HEAD_DIM)
    logits = jnp.einsum('bqgrd,bqkgd->bqgrk', qg, k_sel).astype(F32) * (HEAD_DIM ** -0.5)
    logits = jnp.where(sel_ok[:, :, None, None, :], logits, -jnp.inf)
    p = jax.nn.softmax(logits, axis=-1).astype(v_all.dtype)
    o = jnp.einsum('bqgrk,bqkgd->bqgrd', p, v_sel)
    return o.reshape(b, tq, ATTN_WIDTH)


def dsa_sweep(q, q_idx, w_idx, k_all, v_all, kidx_all, pos, n_sel):
    b, t = q.shape[:2]
    qb = Q_BLOCK if t % Q_BLOCK == 0 else t

    def block(i):
        s = i * qb
        sl = lambda a: lax.dynamic_slice_in_dim(a, s, qb, axis=1)
        return dsa_attend(sl(q), sl(q_idx), sl(w_idx), k_all, v_all, kidx_all,
                          lax.dynamic_slice_in_dim(pos, s, qb), n_sel)

    o = lax.map(block, jnp.arange(t // qb))
    return jnp.swapaxes(o, 0, 1).reshape(b, t, ATTN_WIDTH)


def causal_conv_silu(x, buf, w):
    t = x.shape[1]
    xp = jnp.concatenate([buf, x], axis=1)
    y = xp[:, 0:t] * w[0]
    for j in range(1, CONV_WIDTH):
        y = y + xp[:, j:j + t] * w[j]
    return jax.nn.silu(y), xp[:, xp.shape[1] - (CONV_WIDTH - 1):]


def gated_delta_rule(q, k, v, g, beta, state0):
    b, t, h, dk = q.shape
    dv = v.shape[-1]
    c = DELTA_CHUNK if t % DELTA_CHUNK == 0 else t
    n = t // c

    def chunks(a):
        a = a.astype(F32).reshape(b, n, c, h, *a.shape[3:])
        return jnp.moveaxis(a, (1, 3), (0, 2))

    qc = chunks(q) * (dk ** -0.5)
    kc = chunks(k)
    vc = chunks(v)
    bc = chunks(beta)
    gc = jnp.cumsum(chunks(g), axis=-1)
    causal = jnp.tril(jnp.ones((c, c), bool))
    strict = jnp.tril(jnp.ones((c, c), bool), -1)
    decay = jnp.exp(jnp.where(causal, gc[..., :, None] - gc[..., None, :], -jnp.inf))
    kb = kc * bc[..., None]
    a_mat = jnp.where(strict, jnp.einsum('nbhid,nbhjd->nbhij', kb, kc) * decay, 0.0)
    m = a_mat + jnp.eye(c, dtype=F32)
    rhs = jnp.concatenate([vc * bc[..., None], kb * jnp.exp(gc)[..., None]], axis=-1)
    sol = lax.linalg.triangular_solve(m, rhs, left_side=True, lower=True, unit_diagonal=True)
    u, w = sol[..., :dv], sol[..., dv:]
    intra = jnp.where(causal, jnp.einsum('nbhid,nbhjd->nbhij', qc, kc) * decay, 0.0)

    def step(s, xs):
        q_c, k_c, u_c, w_c, g_c, a_c = xs
        v_new = u_c - jnp.einsum('bhck,bhkv->bhcv', w_c, s)
        o = (jnp.einsum('bhck,bhkv->bhcv', q_c * jnp.exp(g_c)[..., None], s)
             + jnp.einsum('bhij,bhjv->bhiv', a_c, v_new))
        g_last = g_c[..., -1:]
        s = (s * jnp.exp(g_last)[..., None]
             + jnp.einsum('bhck,bhcv->bhkv', k_c * jnp.exp(g_last - g_c)[..., None], v_new))
        return s, o

    s, o = lax.scan(step, state0.astype(F32), (qc, kc, u, w, gc, intra))
    o = jnp.moveaxis(o, (0, 2), (1, 3)).reshape(b, t, h, dv)
    return o.astype(v.dtype), s.astype(state0.dtype)


def mixer(h, pos0, past_k, past_v, past_idx_k, conv_buf, ssm_state,
          w_in, w_out, conv_w, a_log, dt_bias, q_norm_w, k_norm_w, idx_k_norm_w, o_norm_w):
    b, t, _ = h.shape
    pos = pos0 + jnp.arange(t)
    qa, ka, va, qi, ki, wi, qd, kd, vd, zd, bd, ad = split_projection(h @ w_in)
    q = partial_rope(rms_norm(qa.reshape(b, t, N_ATTN_HEADS, HEAD_DIM), q_norm_w), pos)
    k = partial_rope(rms_norm(ka.reshape(b, t, N_KV_HEADS, HEAD_DIM), k_norm_w), pos)
    v = va.reshape(b, t, N_KV_HEADS, HEAD_DIM)
    q_idx = partial_rope(qi.reshape(b, t, N_IDX_HEADS, IDX_DIM), pos)
    k_idx = partial_rope(rms_norm(ki, idx_k_norm_w)[:, :, None, :], pos)[:, :, 0, :]
    k_all = jnp.concatenate([past_k.astype(k.dtype), k], axis=1)
    v_all = jnp.concatenate([past_v.astype(v.dtype), v], axis=1)
    kidx_all = jnp.concatenate([past_idx_k.astype(k_idx.dtype), k_idx], axis=1)
    n_sel = min(TOPK_MAX, k_all.shape[1] // 4)
    o_attn = dsa_sweep(q, q_idx, wi, k_all, v_all, kidx_all, pos, n_sel)
    qkv, new_conv = causal_conv_silu(jnp.concatenate([qd, kd, vd], axis=-1), conv_buf.astype(h.dtype), conv_w)
    qc, kc, vc = jnp.split(qkv, 3, axis=-1)
    qc = l2_norm(qc.reshape(b, t, N_DELTA_HEADS, HEAD_DIM))
    kc = l2_norm(kc.reshape(b, t, N_DELTA_HEADS, HEAD_DIM))
    vc = vc.reshape(b, t, N_DELTA_HEADS, HEAD_DIM)
    beta = jax.nn.sigmoid(bd.astype(F32))
    g = -jnp.exp(a_log.astype(F32)) * jax.nn.softplus(ad.astype(F32) + dt_bias.astype(F32))
    o_d, new_ssm = gated_delta_rule(qc, kc, vc, g, beta, ssm_state)
    o_d = rms_norm(o_d, o_norm_w) * jax.nn.silu(zd.reshape(b, t, N_DELTA_HEADS, HEAD_DIM))
    out = jnp.concatenate([o_attn, o_d.reshape(b, t, DELTA_WIDTH)], axis=-1) @ w_out
    return out, k, v, k_idx, new_ssm, new_conv


def routed_experts(x, expert, gate, w_gate, w_up, w_down):
    n, d = x.shape
    kk = expert.shape[1]
    nk = n * kk
    flat_e = expert.reshape(-1)
    order = jnp.argsort(flat_e)
    sorted_e = flat_e[order]
    counts = jnp.zeros((N_EXPERTS,), jnp.int32).at[flat_e].add(1)
    padded = (counts + MOE_BLOCK - 1) // MOE_BLOCK * MOE_BLOCK
    pad_end = jnp.cumsum(padded)
    pad_start = pad_end - padded
    start = jnp.cumsum(counts) - counts
    dest = pad_start[sorted_e] + jnp.arange(nk) - start[sorted_e]
    n_blocks = -(-nk // MOE_BLOCK) + N_EXPERTS
    slot_tok = jnp.full((n_blocks * MOE_BLOCK,), n, jnp.int32).at[dest].set((order // kk).astype(jnp.int32))
    block_exp = jnp.minimum(jnp.searchsorted(pad_end, jnp.arange(n_blocks) * MOE_BLOCK, side='right'),
                            N_EXPERTS - 1)
    x_ext = jnp.concatenate([x, jnp.zeros((1, d), x.dtype)], axis=0)

    def run_block(args):
        toks, e = args
        xb = x_ext[toks]
        hid = jax.nn.silu(xb @ w_gate[e]) * (xb @ w_up[e])
        return hid @ w_down[e]

    yb = lax.map(run_block, (slot_tok.reshape(n_blocks, MOE_BLOCK), block_exp))
    y_sorted = yb.reshape(n_blocks * MOE_BLOCK, d)[dest]
    y_slots = jnp.zeros((nk, d), yb.dtype).at[order].set(y_sorted)
    return jnp.einsum('nkd,nk->nd', y_slots.reshape(n, kk, d), gate)


def hier_moe(h, w_group, b_group, w_router, b_router, w_gate, w_up, w_down):
    b, t, d = h.shape
    x = h.reshape(b * t, d)
    rows = jnp.arange(b * t)
    p_group = jax.nn.softmax((x @ w_group).astype(F32) + b_group.astype(F32), axis=-1)
    grp = jnp.argmax(p_group, axis=-1)
    logits = ((x @ w_router).astype(F32) + b_router.astype(F32)).reshape(b * t, N_GROUPS, EXPERTS_PER_GROUP)
    logits = logits[rows, grp]
    top_val, top_idx = lax.top_k(logits, TOP_K_IN_GROUP)
    gate = jax.nn.softmax(top_val, axis=-1) * p_group[rows, grp][:, None]
    expert = (grp[:, None] * EXPERTS_PER_GROUP + top_idx).astype(jnp.int32)
    y = routed_experts(x, expert, gate.astype(x.dtype), w_gate, w_up, w_down)
    return y.reshape(b, t, d)


def setup_inputs(seed: int = 0) -> dict:
    key = jax.random.key(seed)
    ks = jax.random.split(key, 32)
    n_pages = PAST_LEN // PAGE_SIZE
    n_used = DEC_BATCH * n_pages
    n_pool = n_used + max(1, n_used // 4)

    def normal(k, shape, scale):
        return jax.random.normal(k, shape, F32) * scale

    def gain(k, shape):
        return 1.0 + 0.02 * jax.random.normal(k, shape, F32)

    page_table = jax.random.permutation(ks[0], n_pool)[:n_used].reshape(DEC_BATCH, n_pages).astype(jnp.int32)
    dt = jnp.exp(jax.random.uniform(ks[1], (DEPTH, N_DELTA_HEADS), F32, math.log(1e-3), math.log(1e-1)))
    dt_bias = dt + jnp.log(-jnp.expm1(-dt))
    a_log = jnp.log(jax.random.uniform(ks[2], (DEPTH, N_DELTA_HEADS), F32, 1.0, 16.0))
    return {
        "x_prompt": normal(ks[3], (BATCH, SEQ, D_MODEL), 1.0),
        "x_sample": normal(ks[4], (DEC_BATCH, DEC_SEQ, D_MODEL), 1.0),
        "cache_k": normal(ks[5], (DEPTH, n_pool, PAGE_SIZE, N_KV_HEADS, HEAD_DIM), 1.0),
        "cache_v": normal(ks[6], (DEPTH, n_pool, PAGE_SIZE, N_KV_HEADS, HEAD_DIM), 1.0),
        "cache_idx_k": normal(ks[7], (DEPTH, n_pool, PAGE_SIZE, IDX_DIM), 1.0),
        "state_ssm": normal(ks[8], (DEPTH, DEC_BATCH, N_DELTA_HEADS, HEAD_DIM, HEAD_DIM), HEAD_DIM ** -0.5),
        "state_conv": normal(ks[9], (DEPTH, DEC_BATCH, CONV_WIDTH - 1, CONV_CHANNELS), 1.0),
        "page_table": page_table,
        "c_prompt": normal(ks[10], (BATCH, D_MODEL), 1.0),
        "c_sample": normal(ks[11], (DEC_BATCH, D_MODEL), 1.0),
        "w_in": normal(ks[12], (DEPTH, D_MODEL, PROJ_WIDTH), D_MODEL ** -0.5),
        "w_out": normal(ks[13], (DEPTH, MIX_WIDTH, D_MODEL), MIX_WIDTH ** -0.5),
        "conv_w": normal(ks[14], (DEPTH, CONV_WIDTH, CONV_CHANNELS), CONV_WIDTH ** -0.5),
        "a_log": a_log,
        "dt_bias": dt_bias,
        "q_norm_w": gain(ks[15], (DEPTH, HEAD_DIM)),
        "k_norm_w": gain(ks[16], (DEPTH, HEAD_DIM)),
        "idx_k_norm_w": gain(ks[17], (DEPTH, IDX_DIM)),
        "o_norm_w": gain(ks[18], (DEPTH, HEAD_DIM)),
        "norm1_w": gain(ks[19], (DEPTH, D_MODEL)),
        "norm2_w": gain(ks[20], (DEPTH, D_MODEL)),
        "w_ada": normal(ks[21], (DEPTH, D_MODEL, N_MOD * D_MODEL), 0.5 * D_MODEL ** -0.5),
        "b_ada": normal(ks[22], (DEPTH, N_MOD * D_MODEL), 0.02),
        "w_group": normal(ks[23], (DEPTH, D_MODEL, N_GROUPS), D_MODEL ** -0.5),
        "b_group": normal(ks[24], (DEPTH, N_GROUPS), 0.01),
        "w_router": normal(ks[25], (DEPTH, D_MODEL, N_EXPERTS), D_MODEL ** -0.5),
        "b_router": normal(ks[26], (DEPTH, N_EXPERTS), 0.01),
        "w_gate": normal(ks[27], (DEPTH, N_EXPERTS, D_MODEL, D_EXPERT), D_MODEL ** -0.5),
        "w_up": normal(ks[28], (DEPTH, N_EXPERTS, D_MODEL, D_EXPERT), D_MODEL ** -0.5),
        "w_down": normal(ks[29], (DEPTH, N_EXPERTS, D_EXPERT, D_MODEL), D_EXPERT ** -0.5),
    }


def reference(x_prompt, x_sample, cache_k, cache_v, cache_idx_k, state_ssm, state_conv, page_table,
              c_prompt, c_sample, w_in, w_out, conv_w, a_log, dt_bias, q_norm_w, k_norm_w,
              idx_k_norm_w, o_norm_w, norm1_w, norm2_w, w_ada, b_ada, w_group, b_group,
              w_router, b_router, w_gate, w_up, w_down):
    yp, ys = x_prompt, x_sample
    bp = x_prompt.shape[0]
    past_len = page_table.shape[1] * PAGE_SIZE
    kp_l, vp_l, ip_l, sp_l, cp_l = [], [], [], [], []
    ks_l, vs_l, is_l, ss_l, cs_l = [], [], [], [], []
    for l in range(DEPTH):
        mp = ada_modulation(c_prompt, w_ada[l], b_ada[l])
        ms = ada_modulation(c_sample, w_ada[l], b_ada[l])
        mix_p, k_p, v_p, i_p, s_p, c_p = mixer(
            modulate(yp, norm1_w[l], mp[0], mp[1]), 0,
            jnp.zeros((bp, 0, N_KV_HEADS, HEAD_DIM), yp.dtype),
            jnp.zeros((bp, 0, N_KV_HEADS, HEAD_DIM), yp.dtype),
            jnp.zeros((bp, 0, IDX_DIM), yp.dtype),
            jnp.zeros((bp, CONV_WIDTH - 1, CONV_CHANNELS), yp.dtype),
            jnp.zeros((bp, N_DELTA_HEADS, HEAD_DIM, HEAD_DIM), yp.dtype),
            w_in[l], w_out[l], conv_w[l], a_log[l], dt_bias[l], q_norm_w[l], k_norm_w[l],
            idx_k_norm_w[l], o_norm_w[l])
        mix_s, k_s, v_s, i_s, s_s, c_s = mixer(
            modulate(ys, norm1_w[l], ms[0], ms[1]), past_len,
            gather_pages(cache_k[l], page_table),
            gather_pages(cache_v[l], page_table),
            gather_pages(cache_idx_k[l], page_table),
            state_conv[l], state_ssm[l],
            w_in[l], w_out[l], conv_w[l], a_log[l], dt_bias[l], q_norm_w[l], k_norm_w[l],
            idx_k_norm_w[l], o_norm_w[l])
        yp = yp + mp[2] * mix_p
        ys = ys + ms[2] * mix_s
        yp = yp + mp[5] * hier_moe(modulate(yp, norm2_w[l], mp[3], mp[4]), w_group[l], b_group[l],
                                   w_router[l], b_router[l], w_gate[l], w_up[l], w_down[l])
        ys = ys + ms[5] * hier_moe(modulate(ys, norm2_w[l], ms[3], ms[4]), w_group[l], b_group[l],
                                   w_router[l], b_router[l], w_gate[l], w_up[l], w_down[l])
        kp_l.append(k_p); vp_l.append(v_p); ip_l.append(i_p); sp_l.append(s_p); cp_l.append(c_p)
        ks_l.append(k_s); vs_l.append(v_s); is_l.append(i_s); ss_l.append(s_s); cs_l.append(c_s)
    return (yp, ys, jnp.stack(kp_l), jnp.stack(vp_l), jnp.stack(ip_l), jnp.stack(sp_l), jnp.stack(cp_l),
            jnp.stack(ks_l), jnp.stack(vs_l), jnp.stack(is_l), jnp.stack(ss_l), jnp.stack(cs_l))
```

```python
import functools

import jax
import jax.numpy as jnp
import numpy as np
from jax import lax
from jax.experimental import pallas as pl
from jax.experimental.pallas import tpu as pltpu

F32 = jnp.float32
BF16 = jnp.bfloat16
I32 = jnp.int32

HEAD_DIM = 128
N_ATTN_HEADS = 8
N_KV_HEADS = 2
KV_GROUP = N_ATTN_HEADS // N_KV_HEADS
N_DELTA_HEADS = 8
N_IDX_HEADS = 16
IDX_DIM = 64
ATTN_WIDTH = N_ATTN_HEADS * HEAD_DIM
KV_WIDTH = N_KV_HEADS * HEAD_DIM
DELTA_WIDTH = N_DELTA_HEADS * HEAD_DIM
IDX_WIDTH = N_IDX_HEADS * IDX_DIM
CONV_CHANNELS = 3 * DELTA_WIDTH
TOPK_MAX = 256
ROPE_THETA = 500000.0
ROPE_FRACTION = 4
CONV_WIDTH = 4
DELTA_CHUNK = 64
N_GROUPS = 8
EXPERTS_PER_GROUP = 8
N_EXPERTS = N_GROUPS * EXPERTS_PER_GROUP
N_MOD = 6
EPS = 1e-6
PAGE_SIZE = 128
PROJ_SIZES = (ATTN_WIDTH, KV_WIDTH, KV_WIDTH, IDX_WIDTH, IDX_DIM, N_IDX_HEADS,
              DELTA_WIDTH, DELTA_WIDTH, DELTA_WIDTH, DELTA_WIDTH, N_DELTA_HEADS, N_DELTA_HEADS)

LANES = 128
SUBLANES = 8
VMEM_LIMIT = 56 * 1024 * 1024

C_QA = 0
C_QI = 1024
C_ZD = 2048
C_CONV = 3072
C_KA = 6144
C_VA = 6400
C_MISC = 6656
PROJ_PACKED = 6912
M_KI = 0
M_WI = 64
M_BD = 80
M_AD = 88

Q_TILE = 128
KEY_CHUNK = 256
SAMPLE_ROWS = 16
NEG_BIG = -1e30
INT_MIN = -2147483648
INT_MAX = 2147483647


def _cparams(sem):
    return pltpu.CompilerParams(dimension_semantics=sem, vmem_limit_bytes=VMEM_LIMIT)


def _dot(a, b):
    return jnp.dot(a, b, preferred_element_type=F32)


def _dot_nt(a, b):
    return lax.dot_general(a, b, (((1,), (1,)), ((), ())), preferred_element_type=F32)


def _dot_tn(a, b):
    return lax.dot_general(a, b, (((0,), (0,)), ((), ())), preferred_element_type=F32)


def _sigmoid(x):
    return 1.0 / (1.0 + jnp.exp(-x))


def _silu(x):
    return x * _sigmoid(x)


def _softplus(x):
    return jnp.maximum(x, 0.0) + jnp.log(1.0 + jnp.exp(-jnp.abs(x)))


def _ada_kernel(c_ref, w_ref, b_ref, o_ref):
    s = _silu(c_ref[...]).astype(BF16)
    o_ref[...] = _dot(s, w_ref[...].astype(BF16)) + b_ref[...]


def _ada_modulation(c, w_ada, b_ada):
    r, d = c.shape
    n = w_ada.shape[1]
    tn = 1024 if n % 1024 == 0 else n
    return pl.pallas_call(
        _ada_kernel,
        grid=(n // tn,),
        in_specs=[pl.BlockSpec((r, d), lambda j: (0, 0)),
                  pl.BlockSpec((d, tn), lambda j: (0, j)),
                  pl.BlockSpec((1, tn), lambda j: (0, j))],
        out_specs=pl.BlockSpec((r, tn), lambda j: (0, j)),
        out_shape=jax.ShapeDtypeStruct((r, n), F32),
        compiler_params=_cparams(("parallel",)),
    )(c, w_ada, b_ada.reshape(1, n))


def _inproj_kernel(x_ref, sh_ref, sc_ref, nw_ref, w_ref, o_ref, h_scr):
    @pl.when(pl.program_id(1) == 0)
    def _():
        x = x_ref[...]
        y = x * lax.rsqrt(jnp.mean(x * x, axis=-1, keepdims=True) + EPS) * nw_ref[...]
        h_scr[...] = (y * (1.0 + sc_ref[0]) + sh_ref[0]).astype(BF16)

    o_ref[...] = _dot(h_scr[...], w_ref[...])


def _in_projection(x2d, shift, scale, norm_w, w_packed, tm, rows_per_mod_block):
    n, d = x2d.shape
    np_ = w_packed.shape[1]
    tn = 1152
    r = shift.shape[1]
    tiles_per_mod = rows_per_mod_block // tm
    mod_spec = pl.BlockSpec((1, r, d), lambda i, j: (i // tiles_per_mod, 0, 0))
    return pl.pallas_call(
        _inproj_kernel,
        grid=(n // tm, np_ // tn),
        in_specs=[pl.BlockSpec((tm, d), lambda i, j: (i, 0)),
                  mod_spec, mod_spec,
                  pl.BlockSpec((1, d), lambda i, j: (0, 0)),
                  pl.BlockSpec((d, tn), lambda i, j: (0, j))],
        out_specs=pl.BlockSpec((tm, tn), lambda i, j: (i, j)),
        out_shape=jax.ShapeDtypeStruct((n, np_), F32),
        scratch_shapes=[pltpu.VMEM((tm, d), BF16)],
        compiler_params=_cparams(("parallel", "arbitrary")),
    )(x2d, shift, scale, norm_w.reshape(1, d), w_packed)


def _rope(x, tab, rot):
    c = tab[:, 0:LANES]
    s1 = tab[:, LANES:2 * LANES]
    s2 = tab[:, 2 * LANES:3 * LANES]
    return x * c + pltpu.roll(x, LANES - rot, 1) * s1 + pltpu.roll(x, rot, 1) * s2


def _rms_head(x, w):
    return x * lax.rsqrt(jnp.mean(x * x, axis=-1, keepdims=True) + EPS) * w


def _prep_kernel(qa_ref, qi_ref, ka_ref, va_ref, misc_ref, tabm_ref, tabi_ref, qw_ref, kw_ref, iw_ref,
                 q_ref, kf_ref, kb_ref, vb_ref, qib_ref, kif_ref, kib_ref):
    tabm = tabm_ref[0]
    tabi = tabi_ref[0]
    half_main = HEAD_DIM // ROPE_FRACTION // 2
    half_idx = IDX_DIM // ROPE_FRACTION // 2
    for h in range(N_ATTN_HEADS):
        sl = slice(h * HEAD_DIM, (h + 1) * HEAD_DIM)
        y = _rope(_rms_head(qa_ref[:, sl], qw_ref[...]), tabm, half_main)
        q_ref[:, sl] = (y * (HEAD_DIM ** -0.5)).astype(BF16)
    for h in range(N_KV_HEADS):
        sl = slice(h * HEAD_DIM, (h + 1) * HEAD_DIM)
        y = _rope(_rms_head(ka_ref[:, sl], kw_ref[...]), tabm, half_main)
        kf_ref[:, sl] = y
        kb_ref[:, sl] = y.astype(BF16)
    vb_ref[...] = va_ref[...].astype(BF16)
    for p in range(IDX_WIDTH // LANES):
        sl = slice(p * LANES, (p + 1) * LANES)
        qib_ref[:, sl] = _rope(qi_ref[:, sl], tabi, half_idx).astype(BF16)
    m = misc_ref[...]
    lane = lax.broadcasted_iota(I32, m.shape, 1)
    ki = jnp.where(lane < IDX_DIM, m, 0.0)
    ms = jnp.sum(ki * ki, axis=-1, keepdims=True) * (1.0 / IDX_DIM)
    y = _rope(ki * lax.rsqrt(ms + EPS) * iw_ref[...], tabi, half_idx)
    kif_ref[...] = y[:, 0:IDX_DIM]
    kib_ref[...] = (y + pltpu.roll(y, IDX_DIM, 1)).astype(BF16)


def _rope_tables(pos, head_dim, group):
    d_rot = head_dim // ROPE_FRACTION
    half = d_rot // 2
    inv_freq = jnp.power(ROPE_THETA, -(jnp.arange(half, dtype=F32) * 2.0 / d_rot))
    ang = pos.astype(F32)[:, None] * inv_freq[None, :]
    cos = jnp.cos(ang)
    sin = jnp.sin(ang)
    t = pos.shape[0]
    z = jnp.zeros((t, group - d_rot), F32)
    c = jnp.concatenate([cos, cos, jnp.ones((t, group - d_rot), F32)], axis=1)
    s1 = jnp.concatenate([-sin, jnp.zeros((t, half), F32), z], axis=1)
    s2 = jnp.concatenate([jnp.zeros((t, half), F32), sin, z], axis=1)
    rep = LANES // group
    return jnp.concatenate([jnp.tile(c, (1, rep)), jnp.tile(s1, (1, rep)), jnp.tile(s2, (1, rep))], axis=1)


def _attention_prep(proj, pos, tq, q_norm_w, k_norm_w, idx_k_norm_w):
    n = proj.shape[0]
    p = pos.shape[0]
    g = p // tq
    tabm = _rope_tables(pos, HEAD_DIM, LANES).reshape(g, tq, 3 * LANES)
    tabi = _rope_tables(pos, IDX_DIM, IDX_DIM).reshape(g, tq, 3 * LANES)
    iw = jnp.concatenate([idx_k_norm_w, jnp.zeros((LANES - IDX_DIM,), F32)]).reshape(1, LANES)
    row = lambda w, c: pl.BlockSpec((tq, w), lambda i: (i, c // w))
    tab_spec = pl.BlockSpec((1, tq, 3 * LANES), lambda i: (i % g, 0, 0))
    vec_spec = pl.BlockSpec((1, LANES), lambda i: (0, 0))
    out_row = lambda w: pl.BlockSpec((tq, w), lambda i: (i, 0))
    return pl.pallas_call(
        _prep_kernel,
        grid=(n // tq,),
        in_specs=[row(ATTN_WIDTH, C_QA), row(IDX_WIDTH, C_QI), row(KV_WIDTH, C_KA), row(KV_WIDTH, C_VA),
                  row(LANES, C_MISC), tab_spec, tab_spec, vec_spec, vec_spec, vec_spec],
        out_specs=[out_row(ATTN_WIDTH), out_row(KV_WIDTH), out_row(KV_WIDTH), out_row(KV_WIDTH),
                   out_row(IDX_WIDTH), out_row(IDX_DIM), out_row(LANES)],
        out_shape=[jax.ShapeDtypeStruct((n, ATTN_WIDTH), BF16),
                   jax.ShapeDtypeStruct((n, KV_WIDTH), F32),
                   jax.ShapeDtypeStruct((n, KV_WIDTH), BF16),
                   jax.ShapeDtypeStruct((n, KV_WIDTH), BF16),
                   jax.ShapeDtypeStruct((n, IDX_WIDTH), BF16),
                   jax.ShapeDtypeStruct((n, IDX_DIM), F32),
                   jax.ShapeDtypeStruct((n, LANES), BF16)],
        compiler_params=_cparams(("parallel",)),
    )(proj, proj, proj, proj, proj, tabm, tabi,
      q_norm_w.reshape(1, LANES), k_norm_w.reshape(1, LANES), iw)


def _sort_key(x):
    b = pltpu.bitcast(x + 0.0, I32)
    return b ^ ((b >> 31) & INT_MAX)


def _kth_largest_key(count_ge, k, rows):
    def body(it, ans_u):
        bit = jnp.left_shift(jnp.int32(1), 31 - it)
        cand_u = ans_u | bit
        cnt = count_ge(cand_u ^ INT_MIN)
        return jnp.where(cnt >= k, cand_u, ans_u)

    ans_u = lax.fori_loop(0, 32, body, jnp.zeros((rows, 1), I32))
    return ans_u ^ INT_MIN


def _tie_index_limit(count_eq_le, need, n_keys, rows):
    nbits = max(1, int(n_keys - 1).bit_length())

    def body(it, lo):
        bit = jnp.left_shift(jnp.int32(1), nbits - 1 - it)
        cand = lo | bit
        cnt = count_eq_le(cand - 1)
        return jnp.where(cnt >= need, lo, cand)

    return lax.fori_loop(0, nbits, body, jnp.zeros((rows, 1), I32))


def _dsa_prompt_kernel(q_ref, qi_ref, misc_ref, k_ref, v_ref, kx_ref, o_ref,
                       key_scr, thr_scr, lim_scr, m_scr, l_scr, acc_scr, *, n_sel):
    i = pl.program_id(1)
    tq = Q_TILE
    ck = KEY_CHUNK
    n_ch = (i * tq + tq + ck - 1) // ck
    row_t = i * tq + lax.broadcasted_iota(I32, (tq, 1), 0)
    lane_k = lax.broadcasted_iota(I32, (1, ck), 1)
    lo_half = lax.broadcasted_iota(I32, (tq, LANES), 1) < IDX_DIM

    qi = qi_ref[...]
    w = misc_ref[:, M_WI:M_WI + N_IDX_HEADS]
    zero = jnp.zeros((), BF16)

    def score_chunk(c, carry):
        off = pl.multiple_of(c * ck, ck)
        ks = kx_ref[pl.ds(off, ck), :]
        acc = jnp.zeros((tq, ck), F32)
        for p in range(IDX_WIDTH // LANES):
            slab = qi[:, p * LANES:(p + 1) * LANES]
            for half in range(2):
                lhs = jnp.where(lo_half if half == 0 else jnp.logical_not(lo_half), slab, zero)
                s = _dot_nt(lhs, ks)
                h = 2 * p + half
                acc = acc + w[:, h:h + 1] * jnp.maximum(s, 0.0)
        acc = jnp.where(off + lane_k <= row_t, acc, -jnp.inf)
        key_scr[:, pl.ds(off, ck)] = _sort_key(acc)
        return carry

    lax.fori_loop(0, n_ch, score_chunk, 0)

    def count_where(pred):
        def body(c, cnt):
            off = pl.multiple_of(c * ck, ck)
            kk = key_scr[:, pl.ds(off, ck)]
            return cnt + jnp.sum(pred(kk, off + lane_k).astype(F32), axis=1, keepdims=True)
        return lax.fori_loop(0, n_ch, body, jnp.zeros((tq, 1), F32))

    thr_scr[...] = jnp.full((tq, 1), INT_MIN, I32)
    lim_scr[...] = jnp.full((tq, 1), INT_MAX, I32)

    @pl.when((i + 1) * tq > n_sel)
    def _():
        t = _kth_largest_key(lambda cand: count_where(lambda kk, pos: kk >= cand), float(n_sel), tq)
        thr_scr[...] = t
        n_gt = count_where(lambda kk, pos: kk > t)
        n_ge = count_where(lambda kk, pos: kk >= t)

        @pl.when(jnp.max(n_ge) > float(n_sel))
        def _():
            lim_scr[...] = _tie_index_limit(
                lambda idx: count_where(lambda kk, pos: jnp.logical_and(kk == t, pos <= idx)),
                float(n_sel) - n_gt, k_ref.shape[0], tq)

    thr = thr_scr[...]
    lim = lim_scr[...]

    for g in range(N_KV_HEADS):
        qg = jnp.concatenate(
            [q_ref[:, (g * KV_GROUP + r) * HEAD_DIM:(g * KV_GROUP + r + 1) * HEAD_DIM] for r in range(KV_GROUP)],
            axis=0)
        m_scr[...] = jnp.full(m_scr.shape, NEG_BIG, F32)
        l_scr[...] = jnp.zeros(l_scr.shape, F32)
        acc_scr[...] = jnp.zeros(acc_scr.shape, F32)

        def attend_chunk(c, carry):
            off = pl.multiple_of(c * ck, ck)
            kc = k_ref[pl.ds(off, ck), g * HEAD_DIM:(g + 1) * HEAD_DIM]
            vc = v_ref[pl.ds(off, ck), g * HEAD_DIM:(g + 1) * HEAD_DIM]
            kk = key_scr[:, pl.ds(off, ck)]
            pos = off + lane_k
            sel = jnp.logical_or(kk > thr, jnp.logical_and(kk == thr, pos <= lim))
            sel = jnp.logical_and(sel, pos <= row_t)
            sel = jnp.concatenate([sel] * KV_GROUP, axis=0)
            s = jnp.where(sel, _dot_nt(qg, kc), NEG_BIG)
            m_old = m_scr[...]
            m_new = jnp.maximum(m_old, jnp.max(s, axis=1, keepdims=True))
            p = jnp.where(sel, jnp.exp(s - m_new), 0.0)
            alpha = jnp.exp(m_old - m_new)
            l_scr[...] = alpha * l_scr[...] + jnp.sum(p, axis=1, keepdims=True)
            acc_scr[...] = alpha * acc_scr[...] + _dot(p.astype(BF16), vc)
            m_scr[...] = m_new
            return carry

        lax.fori_loop(0, n_ch, attend_chunk, 0)
        o = acc_scr[...] / l_scr[...]
        for r in range(KV_GROUP):
            h = g * KV_GROUP + r
            o_ref[:, h * HEAD_DIM:(h + 1) * HEAD_DIM] = o[r * tq:(r + 1) * tq].astype(BF16)


def _dsa_prompt(q_bf, qi_bf, proj, k_bf, v_bf, kx_bf, b, t):
    n = b * t
    nq = t // Q_TILE
    n_sel = min(TOPK_MAX, t // 4)
    t_pad = -(-t // KEY_CHUNK) * KEY_CHUNK
    assert t_pad == t
    qrow = lambda w: pl.BlockSpec((Q_TILE, w), lambda bb, i: (bb * nq + i, 0))
    seq = lambda w: pl.BlockSpec((t, w), lambda bb, i: (bb, 0))
    return pl.pallas_call(
        functools.partial(_dsa_prompt_kernel, n_sel=n_sel),
        grid=(b, nq),
        in_specs=[qrow(ATTN_WIDTH), qrow(IDX_WIDTH),
                  pl.BlockSpec((Q_TILE, LANES), lambda bb, i: (bb * nq + i, C_MISC // LANES)),
                  seq(KV_WIDTH), seq(KV_WIDTH), seq(LANES)],
        out_specs=qrow(ATTN_WIDTH),
        out_shape=jax.ShapeDtypeStruct((n, ATTN_WIDTH), BF16),
        scratch_shapes=[pltpu.VMEM((Q_TILE, t), I32),
                        pltpu.VMEM((Q_TILE, 1), I32),
                        pltpu.VMEM((Q_TILE, 1), I32),
                        pltpu.VMEM((KV_GROUP * Q_TILE, 1), F32),
                        pltpu.VMEM((KV_GROUP * Q_TILE, 1), F32),
                        pltpu.VMEM((KV_GROUP * Q_TILE, HEAD_DIM), F32)],
        compiler_params=_cparams(("parallel", "arbitrary")),
    )(q_bf, qi_bf, proj, k_bf, v_bf, kx_bf)


def _sample_score_kernel(pt_ref, q_ref, w_ref, kn_ref, *refs, pages, t_valid):
    page_refs = refs[:pages]
    past_ref, new_ref = refs[pages], refs[pages + 1]
    rows = SAMPLE_ROWS
    q = q_ref[0]
    w = w_ref[0]

    def head_sum(s):
        s = w * jnp.maximum(s, 0.0)
        acc = s[0:rows]
        for h in range(1, N_IDX_HEADS):
            acc = acc + s[h * rows:(h + 1) * rows]
        return acc

    for j in range(pages):
        kp = page_refs[j][0].astype(BF16)
        past_ref[0, :, j * PAGE_SIZE:(j + 1) * PAGE_SIZE] = _sort_key(head_sum(_dot_nt(q, kp)))

    @pl.when(pl.program_id(1) == 0)
    def _():
        kn = jnp.concatenate([kn_ref[...], jnp.zeros((LANES - rows, IDX_DIM), F32)], axis=0).astype(BF16)
        sc = head_sum(_dot_nt(q, kn))
        t = lax.broadcasted_iota(I32, sc.shape, 0)
        s = lax.broadcasted_iota(I32, sc.shape, 1)
        ok = jnp.logical_and(s <= t, s < t_valid)
        new_ref[0] = _sort_key(jnp.where(ok, sc, -jnp.inf))


def _sample_scores(page_table, q_t, w_col, kif, cache_idx, pages, t_valid):
    bs, n_pages = page_table.shape
    past = n_pages * PAGE_SIZE
    hr = N_IDX_HEADS * SAMPLE_ROWS
    page_spec = lambda j: pl.BlockSpec((1, PAGE_SIZE, IDX_DIM), lambda b, c, pt: (pt[b, c * pages + j], 0, 0))
    grid_spec = pltpu.PrefetchScalarGridSpec(
        num_scalar_prefetch=1,
        grid=(bs, n_pages // pages),
        in_specs=[pl.BlockSpec((1, hr, IDX_DIM), lambda b, c, pt: (b, 0, 0)),
                  pl.BlockSpec((1, hr, 1), lambda b, c, pt: (b, 0, 0)),
                  pl.BlockSpec((SAMPLE_ROWS, IDX_DIM), lambda b, c, pt: (b, 0))]
                 + [page_spec(j) for j in range(pages)],
        out_specs=[pl.BlockSpec((1, SAMPLE_ROWS, pages * PAGE_SIZE), lambda b, c, pt: (b, 0, c)),
                   pl.BlockSpec((1, SAMPLE_ROWS, LANES), lambda b, c, pt: (b, 0, 0))],
    )
    return pl.pallas_call(
        functools.partial(_sample_score_kernel, pages=pages, t_valid=t_valid),
        grid_spec=grid_spec,
        out_shape=[jax.ShapeDtypeStruct((bs, SAMPLE_ROWS, past), I32),
                   jax.ShapeDtypeStruct((bs, SAMPLE_ROWS, LANES), I32)],
        compiler_params=_cparams(("parallel", "arbitrary")),
    )(page_table, q_t, w_col, kif, *([cache_idx] * pages))


def _sample_attend_kernel(pt_ref, kp_ref, kn_ref, q_ref, knew_ref, vnew_ref, *refs, pages, n_sel, past):
    k_pages = refs[:pages]
    v_pages = refs[pages:2 * pages]
    o_ref = refs[2 * pages]
    thr_scr, lim_scr, m_scr, l_scr, acc_scr = refs[2 * pages + 1:]
    c = pl.program_id(1)
    rows = SAMPLE_ROWS
    span = pages * PAGE_SIZE

    @pl.when(c == 0)
    def _():
        m_scr[...] = jnp.full(m_scr.shape, NEG_BIG, F32)
        l_scr[...] = jnp.zeros(l_scr.shape, F32)
        acc_scr[...] = jnp.zeros(acc_scr.shape, F32)
        kp = kp_ref[0]
        kn = kn_ref[0]
        pos_p = lax.broadcasted_iota(I32, kp.shape, 1)
        pos_n = past + lax.broadcasted_iota(I32, kn.shape, 1)

        def count_where(pred):
            return (jnp.sum(pred(kp, pos_p).astype(F32), axis=1, keepdims=True)
                    + jnp.sum(pred(kn, pos_n).astype(F32), axis=1, keepdims=True))

        t = _kth_largest_key(lambda cand: count_where(lambda kk, pos: kk >= cand), float(n_sel), rows)
        thr_scr[...] = t
        lim_scr[...] = jnp.full((rows, 1), INT_MAX, I32)
        n_gt = count_where(lambda kk, pos: kk > t)
        n_ge = count_where(lambda kk, pos: kk >= t)

        @pl.when(jnp.max(n_ge) > float(n_sel))
        def _():
            lim_scr[...] = _tie_index_limit(
                lambda idx: count_where(lambda kk, pos: jnp.logical_and(kk == t, pos <= idx)),
                float(n_sel) - n_gt, past + LANES, rows)

    thr = thr_scr[...]
    lim = lim_scr[...]

    def update(g, qg, kc, vc, sel):
        sel = jnp.concatenate([sel] * KV_GROUP, axis=0)
        s = jnp.where(sel, _dot_nt(qg, kc), NEG_BIG)
        m_old = m_scr[g]
        m_new = jnp.maximum(m_old, jnp.max(s, axis=1, keepdims=True))
        p = jnp.where(sel, jnp.exp(s - m_new), 0.0)
        alpha = jnp.exp(m_old - m_new)
        l_scr[g] = alpha * l_scr[g] + jnp.sum(p, axis=1, keepdims=True)
        acc_scr[g] = alpha * acc_scr[g] + _dot(p.astype(BF16), vc)
        m_scr[g] = m_new

    def select(kk, pos):
        return jnp.logical_or(kk > thr, jnp.logical_and(kk == thr, pos <= lim))

    k_cat = jnp.concatenate([r[0] for r in k_pages], axis=0).astype(BF16)
    v_cat = jnp.concatenate([r[0] for r in v_pages], axis=0).astype(BF16)
    off = pl.multiple_of(c * span, span)
    kk = kp_ref[0, :, pl.ds(off, span)]
    sel_past = select(kk, off + lax.broadcasted_iota(I32, kk.shape, 1))
    q_groups = []
    for g in range(N_KV_HEADS):
        qg = jnp.concatenate(
            [q_ref[:, (g * KV_GROUP + r) * HEAD_DIM:(g * KV_GROUP + r + 1) * HEAD_DIM] for r in range(KV_GROUP)],
            axis=0)
        q_groups.append(qg)
        update(g, qg, k_cat[:, g * HEAD_DIM:(g + 1) * HEAD_DIM], v_cat[:, g * HEAD_DIM:(g + 1) * HEAD_DIM], sel_past)

    @pl.when(c == pl.num_programs(1) - 1)
    def _():
        kn = kn_ref[0]
        lane = lax.broadcasted_iota(I32, kn.shape, 1)
        sel_new = jnp.logical_and(select(kn, past + lane), lane < rows)
        pad = jnp.zeros((LANES - rows, KV_WIDTH), BF16)
        k_new = jnp.concatenate([knew_ref[...], pad], axis=0)
        v_new = jnp.concatenate([vnew_ref[...], pad], axis=0)
        for g in range(N_KV_HEADS):
            sl = slice(g * HEAD_DIM, (g + 1) * HEAD_DIM)
            update(g, q_groups[g], k_new[:, sl], v_new[:, sl], sel_new)
            o = acc_scr[g] / l_scr[g]
            for r in range(KV_GROUP):
                h = g * KV_GROUP + r
                o_ref[:, h * HEAD_DIM:(h + 1) * HEAD_DIM] = o[r * rows:(r + 1) * rows].astype(BF16)


def _sample_attend(page_table, keys_past, keys_new, q_bf, k_bf, v_bf, cache_k, cache_v, pages, n_sel):
    bs, n_pages = page_table.shape
    past = n_pages * PAGE_SIZE
    page_spec = lambda j: pl.BlockSpec((1, PAGE_SIZE, KV_WIDTH), lambda b, c, pt: (pt[b, c * pages + j], 0, 0))
    row = lambda w: pl.BlockSpec((SAMPLE_ROWS, w), lambda b, c, pt: (b, 0))
    grid_spec = pltpu.PrefetchScalarGridSpec(
        num_scalar_prefetch=1,
        grid=(bs, n_pages // pages),
        in_specs=[pl.BlockSpec((1, SAMPLE_ROWS, past), lambda b, c, pt: (b, 0, 0)),
                  pl.BlockSpec((1, SAMPLE_ROWS, LANES), lambda b, c, pt: (b, 0, 0)),
                  row(ATTN_WIDTH), row(KV_WIDTH), row(KV_WIDTH)]
                 + [page_spec(j) for j in range(pages)] * 2,
        out_specs=row(ATTN_WIDTH),
        scratch_shapes=[pltpu.VMEM((SAMPLE_ROWS, 1), I32),
                        pltpu.VMEM((SAMPLE_ROWS, 1), I32),
                        pltpu.VMEM((N_KV_HEADS, KV_GROUP * SAMPLE_ROWS, 1), F32),
                        pltpu.VMEM((N_KV_HEADS, KV_GROUP * SAMPLE_ROWS, 1), F32),
                        pltpu.VMEM((N_KV_HEADS, KV_GROUP * SAMPLE_ROWS, HEAD_DIM), F32)],
    )
    return pl.pallas_call(
        functools.partial(_sample_attend_kernel, pages=pages, n_sel=n_sel, past=past),
        grid_spec=grid_spec,
        out_shape=jax.ShapeDtypeStruct((bs * SAMPLE_ROWS, ATTN_WIDTH), BF16),
        compiler_params=_cparams(("parallel", "arbitrary")),
    )(page_table, keys_past, keys_new, q_bf, k_bf, v_bf, *([cache_k] * pages), *([cache_v] * pages))


def _delta_prep_kernel(x_ref, halo_ref, prev_ref, misc_ref, cw_ref, al_ref, dt_ref,
                       qn_ref, kn_ref, vv_ref, bg_ref, xp_scr, *, tiles_per_seq, t_valid, tt):
    i = pl.program_id(0)
    tile_in_seq = i % tiles_per_seq
    halo = jnp.where(tile_in_seq == 0, prev_ref[0], halo_ref[...])
    xp_scr[0:SUBLANES, :] = halo
    xp_scr[SUBLANES:SUBLANES + tt, :] = x_ref[...]
    base = SUBLANES - (CONV_WIDTH - 1)
    outs = (qn_ref, kn_ref, vv_ref)
    for sec in range(3):
        for h in range(N_DELTA_HEADS):
            col = sec * DELTA_WIDTH + h * HEAD_DIM
            sl = slice(col, col + HEAD_DIM)
            y = xp_scr[base:base + tt, sl] * cw_ref[0:1, sl]
            for j in range(1, CONV_WIDTH):
                y = y + xp_scr[base + j:base + j + tt, sl] * cw_ref[j:j + 1, sl]
            y = _silu(y)
            if sec < 2:
                y = y * lax.rsqrt(jnp.sum(y * y, axis=-1, keepdims=True) + EPS)
            if sec == 0:
                y = y * (HEAD_DIM ** -0.5)
            outs[sec][:, h * HEAD_DIM:(h + 1) * HEAD_DIM] = y
    m = misc_ref[...]
    lane = lax.broadcasted_iota(I32, m.shape, 1)
    row = tile_in_seq * tt + lax.broadcasted_iota(I32, m.shape, 0)
    beta = _sigmoid(m)
    g = -jnp.exp(al_ref[...]) * _softplus(m + dt_ref[...])
    is_b = jnp.logical_and(lane >= M_BD, lane < M_BD + N_DELTA_HEADS)
    is_g = jnp.logical_and(lane >= M_AD, lane < M_AD + N_DELTA_HEADS)
    comb = jnp.where(is_b, beta, jnp.where(is_g, g, 0.0))
    comb = jnp.where(row < t_valid, comb, 0.0)
    bg_ref[...] = pltpu.roll(comb, LANES - M_BD, 1)


def _delta_prep(proj, prev8, conv_w, a_log, dt_bias, b, t, tt, t_valid):
    n = proj.shape[0]
    tiles_per_seq = t // tt
    pad_vec = lambda v: jnp.zeros((1, LANES), F32).at[0, M_AD:M_AD + N_DELTA_HEADS].set(v)
    halo_blocks = tt // SUBLANES
    return pl.pallas_call(
        functools.partial(_delta_prep_kernel, tiles_per_seq=tiles_per_seq, t_valid=t_valid, tt=tt),
        grid=(n // tt,),
        in_specs=[pl.BlockSpec((tt, CONV_CHANNELS), lambda i: (i, C_CONV // CONV_CHANNELS)),
                  pl.BlockSpec((SUBLANES, CONV_CHANNELS),
                               lambda i: (jnp.maximum(i * halo_blocks - 1, 0), C_CONV // CONV_CHANNELS)),
                  pl.BlockSpec((1, SUBLANES, CONV_CHANNELS), lambda i: (i // tiles_per_seq, 0, 0)),
                  pl.BlockSpec((tt, LANES), lambda i: (i, C_MISC // LANES)),
                  pl.BlockSpec((CONV_WIDTH, CONV_CHANNELS), lambda i: (0, 0)),
                  pl.BlockSpec((1, LANES), lambda i: (0, 0)),
                  pl.BlockSpec((1, LANES), lambda i: (0, 0))],
        out_specs=[pl.BlockSpec((tt, DELTA_WIDTH), lambda i: (i, 0))] * 3
                  + [pl.BlockSpec((tt, LANES), lambda i: (i, 0))],
        out_shape=[jax.ShapeDtypeStruct((n, DELTA_WIDTH), F32)] * 3 + [jax.ShapeDtypeStruct((n, LANES), F32)],
        scratch_shapes=[pltpu.VMEM((SUBLANES + tt, CONV_CHANNELS), F32)],
        compiler_params=_cparams(("parallel",)),
    )(proj, proj, prev8, proj, conv_w, pad_vec(a_log), pad_vec(dt_bias))


def _split_bf16(a):
    hi = a.astype(BF16)
    return hi, (a - hi.astype(F32)).astype(BF16)


def _mm(a, b, passes):
    if passes == 1:
        return _dot(a.astype(BF16), b.astype(BF16))
    a_hi, a_lo = _split_bf16(a)
    b_hi, b_lo = _split_bf16(b)
    return _dot(a_hi, b_hi) + _dot(a_lo, b_hi) + _dot(a_hi, b_lo)


def _mm_nt(a, b, passes):
    if passes == 1:
        return _dot_nt(a.astype(BF16), b.astype(BF16))
    a_hi, a_lo = _split_bf16(a)
    b_hi, b_lo = _split_bf16(b)
    return _dot_nt(a_hi, b_hi) + _dot_nt(a_lo, b_hi) + _dot_nt(a_hi, b_lo)


DELTA_PASSES = {"a": 1, "inv": 1, "sol": 1, "ws": 1, "iv": 1, "kv": 1}
DELTA_INV_BLOCK = 16


def _delta_chunk_kernel(qn_ref, kn_ref, vv_ref, bg_ref, z_ref, s0_ref, ow_ref, od_ref, so_ref, s_scr):
    c = pl.program_id(1)
    cs = DELTA_CHUNK

    @pl.when(c == 0)
    def _():
        s_scr[...] = s0_ref[0]

    bg = bg_ref[...]
    ri = lax.broadcasted_iota(I32, (cs, cs), 0)
    ci = lax.broadcasted_iota(I32, (cs, cs), 1)
    causal = ri >= ci
    strict = ri > ci
    eye = (ri == ci).astype(F32)
    ltri = causal.astype(BF16)
    g1 = bg.astype(BF16)
    r1 = bg - g1.astype(F32)
    g2 = r1.astype(BF16)
    g3 = (r1 - g2.astype(F32)).astype(BF16)
    gc = _dot(ltri, g1) + _dot(ltri, g2) + _dot(ltri, g3)
    gct = gc.T
    for h in range(N_DELTA_HEADS):
        sl = slice(h * HEAD_DIM, (h + 1) * HEAD_DIM)
        gcc = gc[:, N_DELTA_HEADS + h:N_DELTA_HEADS + h + 1]
        gcr = gct[N_DELTA_HEADS + h:N_DELTA_HEADS + h + 1, :]
        bcol = bg[:, h:h + 1]
        decay = jnp.exp(jnp.where(causal, gcc - gcr, -jnp.inf))
        k = kn_ref[:, sl]
        q = qn_ref[:, sl]
        v = vv_ref[:, sl]
        kb = k * bcol
        eg = jnp.exp(gcc)
        kq = _mm_nt(jnp.concatenate([kb, q], axis=0), k, DELTA_PASSES["a"])
        a = jnp.where(strict, kq[0:cs] * decay, 0.0)
        intra = jnp.where(causal, kq[cs:2 * cs] * decay, 0.0)
        x = -a
        nb = DELTA_INV_BLOCK
        y = jnp.where((ri // nb) == (ci // nb), x, 0.0)
        p = eye + y
        y = _mm(y, y, DELTA_PASSES["inv"])
        n_sq = max(1, int(nb - 1).bit_length())
        for lvl in range(1, n_sq):
            if lvl < n_sq - 1:
                py = _mm(jnp.concatenate([p, y], axis=0), y, DELTA_PASSES["inv"])
                p = p + py[0:cs]
                y = py[cs:2 * cs]
            else:
                p = p + _mm(p, y, DELTA_PASSES["inv"])
        size = 2 * nb
        while size <= cs:
            off = jnp.where(jnp.logical_and((ri // size) == (ci // size), (ri // (size // 2)) != (ci // (size // 2))),
                            x, 0.0)
            p = p + _mm(_mm(p, off, DELTA_PASSES["inv"]), p, DELTA_PASSES["inv"])
            size *= 2
        rhs = jnp.concatenate([v * bcol, kb * eg], axis=1)
        sol = _mm(p, rhs, DELTA_PASSES["sol"])
        u = sol[:, 0:HEAD_DIM]
        w = sol[:, HEAD_DIM:2 * HEAD_DIM]
        s_h = s_scr[h]
        ws = _mm(jnp.concatenate([w, q * eg], axis=0), s_h, DELTA_PASSES["ws"])
        v_new = u - ws[0:cs]
        o = ws[cs:2 * cs] + _mm(intra, v_new, DELTA_PASSES["iv"])
        g_last = gcr[:, cs - 1:cs]
        kg_t = (k * jnp.exp(g_last - gcc)).T
        s_scr[h] = s_h * jnp.exp(g_last) + _mm(kg_t, v_new, DELTA_PASSES["kv"])
        on = o * lax.rsqrt(jnp.mean(o * o, axis=-1, keepdims=True) + EPS) * ow_ref[...]
        od_ref[:, sl] = (on * _silu(z_ref[:, sl])).astype(BF16)

    @pl.when(c == pl.num_programs(1) - 1)
    def _():
        so_ref[0] = s_scr[...]


def _delta_chunks(qn, kn, vv, bg, zsrc, z_col_block, state0, o_norm_w, b, t):
    n = b * t
    nc = t // DELTA_CHUNK
    row = lambda w, cb=0: pl.BlockSpec((DELTA_CHUNK, w), lambda bb, c: (bb * nc + c, cb))
    st = pl.BlockSpec((1, N_DELTA_HEADS, HEAD_DIM, HEAD_DIM), lambda bb, c: (bb, 0, 0, 0))
    return pl.pallas_call(
        _delta_chunk_kernel,
        grid=(b, nc),
        in_specs=[row(DELTA_WIDTH), row(DELTA_WIDTH), row(DELTA_WIDTH), row(LANES),
                  row(DELTA_WIDTH, z_col_block), st, pl.BlockSpec((1, LANES), lambda bb, c: (0, 0))],
        out_specs=[row(DELTA_WIDTH), st],
        out_shape=[jax.ShapeDtypeStruct((n, DELTA_WIDTH), BF16),
                   jax.ShapeDtypeStruct(state0.shape, F32)],
        scratch_shapes=[pltpu.VMEM((N_DELTA_HEADS, HEAD_DIM, HEAD_DIM), F32)],
        compiler_params=_cparams(("parallel", "arbitrary")),
    )(qn, kn, vv, bg, zsrc, state0, o_norm_w.reshape(1, LANES))


def _outproj_kernel(oa_ref, od_ref, x_ref, g1_ref, sh_ref, sc_ref, nw_ref, wo_ref, wrh_ref, wrl_ref, br_ref,
                    x1_ref, h2_ref, lg_ref):
    mix = _dot(oa_ref[...], wo_ref[0:ATTN_WIDTH, :]) + _dot(od_ref[...], wo_ref[ATTN_WIDTH:ATTN_WIDTH + DELTA_WIDTH, :])
    x1 = x_ref[...] + g1_ref[0] * mix
    x1_ref[...] = x1
    y = x1 * lax.rsqrt(jnp.mean(x1 * x1, axis=-1, keepdims=True) + EPS) * nw_ref[...]
    h2 = y * (1.0 + sc_ref[0]) + sh_ref[0]
    hb = h2.astype(BF16)
    h2_ref[...] = hb
    lo = (h2 - hb.astype(F32)).astype(BF16)
    lg_ref[...] = _dot(hb, wrh_ref[...]) + _dot(lo, wrh_ref[...]) + _dot(hb, wrl_ref[...]) + br_ref[...]


def _out_projection(o_attn, o_delta, x2d, gate1, shift2, scale2, norm2_w, w_out_bf, wr_hi, wr_lo, b_rt,
                    tm, rows_per_mod_block):
    n, d = x2d.shape
    r = gate1.shape[1]
    tiles_per_mod = rows_per_mod_block // tm
    mod_spec = pl.BlockSpec((1, r, d), lambda i: (i // tiles_per_mod, 0, 0))
    row = lambda w: pl.BlockSpec((tm, w), lambda i: (i, 0))
    full = lambda a: pl.BlockSpec(a.shape, lambda i: (0, 0))
    return pl.pallas_call(
        _outproj_kernel,
        grid=(n // tm,),
        in_specs=[row(ATTN_WIDTH), row(DELTA_WIDTH), row(d), mod_spec, mod_spec, mod_spec,
                  pl.BlockSpec((1, d), lambda i: (0, 0)), full(w_out_bf), full(wr_hi), full(wr_lo), full(b_rt)],
        out_specs=[row(d), row(d), row(LANES)],
        out_shape=[jax.ShapeDtypeStruct((n, d), F32), jax.ShapeDtypeStruct((n, d), BF16),
                   jax.ShapeDtypeStruct((n, LANES), F32)],
        compiler_params=_cparams(("parallel",)),
    )(o_attn, o_delta, x2d, gate1, shift2, scale2, norm2_w.reshape(1, d), w_out_bf, wr_hi, wr_lo, b_rt)


def _route_kernel(lg_ref, eid_ref, gate_ref):
    x = lg_ref[...]
    lane = lax.broadcasted_iota(I32, x.shape, 1)
    gl = jnp.where(lane < N_GROUPS, x, -jnp.inf)
    ge = jnp.exp(gl - jnp.max(gl, axis=1, keepdims=True))
    p = ge / jnp.sum(ge, axis=1, keepdims=True)
    p_max = jnp.max(p, axis=1, keepdims=True)
    grp = jnp.min(jnp.where(p == p_max, lane, LANES), axis=1, keepdims=True)
    e_lane = lane - N_GROUPS
    in_grp = jnp.logical_and(jnp.logical_and(e_lane >= 0, e_lane < N_EXPERTS),
                             (e_lane >> 3) == grp)
    rl = jnp.where(in_grp, x, -jnp.inf)
    v1 = jnp.max(rl, axis=1, keepdims=True)
    i1 = jnp.min(jnp.where(rl == v1, lane, LANES), axis=1, keepdims=True)
    rl2 = jnp.where(lane == i1, -jnp.inf, rl)
    v2 = jnp.max(rl2, axis=1, keepdims=True)
    i2 = jnp.min(jnp.where(rl2 == v2, lane, LANES), axis=1, keepdims=True)
    t = jnp.exp(v2 - v1)
    den = 1.0 + t
    eid_ref[...] = jnp.where(lane == 0, i1 - N_GROUPS, jnp.where(lane == 1, i2 - N_GROUPS, 0))
    gate_ref[...] = jnp.where(lane == 0, (1.0 / den) * p_max, jnp.where(lane == 1, (t / den) * p_max, 0.0))


def _route(logits, tm):
    n = logits.shape[0]
    spec = pl.BlockSpec((tm, LANES), lambda i: (i, 0))
    return pl.pallas_call(
        _route_kernel,
        grid=(n // tm,),
        in_specs=[spec],
        out_specs=[spec, spec],
        out_shape=[jax.ShapeDtypeStruct((n, LANES), I32), jax.ShapeDtypeStruct((n, LANES), F32)],
        compiler_params=_cparams(("parallel",)),
    )(logits)


def _moe_kernel(be_ref, na_ref, x_ref, wg_ref, wu_ref, wd_ref, o_ref, wg_scr, wu_scr, wd_scr):
    i = pl.program_id(0)

    @pl.when(i < na_ref[0])
    def _():
        changed = jnp.logical_or(i == 0, be_ref[i] != be_ref[jnp.maximum(i - 1, 0)])

        @pl.when(changed)
        def _():
            wg_scr[...] = wg_ref[0].astype(BF16)
            wu_scr[...] = wu_ref[0].astype(BF16)
            wd_scr[...] = wd_ref[0].astype(BF16)

        x = x_ref[...]
        hid = _silu(_dot(x, wg_scr[...])) * _dot(x, wu_scr[...])
        o_ref[...] = _dot(hid.astype(BF16), wd_scr[...])

    @pl.when(i >= na_ref[0])
    def _():
        o_ref[...] = jnp.zeros(o_ref.shape, F32)


def _moe_experts(block_exp, n_active, x_sorted, w_gate, w_up, w_down, bm):
    ns, d = x_sorted.shape
    f = w_gate.shape[2]
    grid_spec = pltpu.PrefetchScalarGridSpec(
        num_scalar_prefetch=2,
        grid=(ns // bm,),
        in_specs=[pl.BlockSpec((bm, d), lambda i, be, na: (i, 0)),
                  pl.BlockSpec((1, d, f), lambda i, be, na: (be[i], 0, 0)),
                  pl.BlockSpec((1, d, f), lambda i, be, na: (be[i], 0, 0)),
                  pl.BlockSpec((1, f, d), lambda i, be, na: (be[i], 0, 0))],
        out_specs=pl.BlockSpec((bm, d), lambda i, be, na: (i, 0)),
        scratch_shapes=[pltpu.VMEM((d, f), BF16), pltpu.VMEM((d, f), BF16), pltpu.VMEM((f, d), BF16)],
    )
    return pl.pallas_call(
        _moe_kernel,
        grid_spec=grid_spec,
        out_shape=jax.ShapeDtypeStruct((ns, d), F32),
        compiler_params=_cparams(("arbitrary",)),
    )(block_exp, n_active, x_sorted, w_gate, w_up, w_down)


def _combine_kernel(x1_ref, y0_ref, y1_ref, gt_ref, g2_ref, o_ref):
    gt = gt_ref[...]
    y = y0_ref[...] * gt[:, 0:1] + y1_ref[...] * gt[:, 1:2]
    o_ref[...] = x1_ref[...] + g2_ref[0] * y


def _combine(x1, y0, y1, gates, gate2, tm, rows_per_mod_block):
    n, d = x1.shape
    r = gate2.shape[1]
    tiles_per_mod = rows_per_mod_block // tm
    row = lambda w: pl.BlockSpec((tm, w), lambda i: (i, 0))
    return pl.pallas_call(
        _combine_kernel,
        grid=(n // tm,),
        in_specs=[row(d), row(d), row(d), row(LANES),
                  pl.BlockSpec((1, r, d), lambda i: (i // tiles_per_mod, 0, 0))],
        out_specs=row(d),
        out_shape=jax.ShapeDtypeStruct((n, d), F32),
        compiler_params=_cparams(("parallel",)),
    )(x1, y0, y1, gates, gate2)


def _pick_tile(n, pref, mult=16):
    t = min(pref, n)
    while n % t or t % mult:
        t -= 1
    return t


def _pack_w_in(w_in):
    d = w_in.shape[0]
    bounds = np.cumsum(PROJ_SIZES)[:-1].tolist()
    qa, ka, va, qi, ki, wi, qd, kd, vd, zd, bd, ad = jnp.split(w_in, bounds, axis=1)
    used = IDX_DIM + N_IDX_HEADS + 2 * N_DELTA_HEADS
    misc = jnp.concatenate([ki, wi, bd, ad, jnp.zeros((d, LANES - used), w_in.dtype)], axis=1)
    cols = [qa, qi, zd, qd, kd, vd, ka, va, misc]
    width = sum(c.shape[1] for c in cols)
    cols.append(jnp.zeros((d, PROJ_PACKED - width), w_in.dtype))
    return jnp.concatenate(cols, axis=1).astype(BF16)


def _route_and_sort(eid, bm):
    n = eid.shape[0]
    nk = 2 * n
    flat_e = eid.reshape(-1)
    order = jnp.argsort(flat_e, stable=True).astype(I32)
    sorted_e = flat_e[order]
    counts = jnp.zeros((N_EXPERTS,), I32).at[flat_e].add(1)
    padded = (counts + bm - 1) // bm * bm
    pad_end = jnp.cumsum(padded)
    pad_start = pad_end - padded
    start = jnp.cumsum(counts) - counts
    dest_sorted = (pad_start[sorted_e] + jnp.arange(nk, dtype=I32) - start[sorted_e]).astype(I32)
    n_blocks = -(-nk // bm) + N_EXPERTS
    slot_tok = jnp.zeros((n_blocks * bm,), I32).at[dest_sorted].set(order // 2)
    dest = jnp.zeros((nk,), I32).at[order].set(dest_sorted)
    block_exp = jnp.minimum(jnp.searchsorted(pad_end, jnp.arange(n_blocks, dtype=I32) * bm, side='right'),
                            N_EXPERTS - 1).astype(I32)
    n_active = (pad_end[-1] // bm).astype(I32).reshape(1)
    return slot_tok, dest.reshape(n, 2), block_exp, n_active


def _layer(yp, ys, cache_k, cache_v, cache_idx, state_ssm, state_conv, page_table, c_prompt, c_sample,
           w_in, w_out, conv_w, a_log, dt_bias, q_norm_w, k_norm_w, idx_k_norm_w, o_norm_w, norm1_w, norm2_w,
           w_ada, b_ada, w_group, b_group, w_router, b_router, w_gate, w_up, w_down):
    bp, tp, d = yp.shape
    bs, ts, _ = ys.shape
    past = page_table.shape[1] * PAGE_SIZE
    rows = SAMPLE_ROWS
    assert CONV_WIDTH - 1 <= ts <= rows and tp % KEY_CHUNK == 0 and tp % DELTA_CHUNK == 0

    n_c = bp + bs
    n_c_pad = -(-n_c // SUBLANES) * SUBLANES
    c_all = jnp.concatenate([c_prompt, c_sample, jnp.zeros((n_c_pad - n_c, d), F32)], axis=0)
    mod = _ada_modulation(c_all, w_ada, b_ada)
    mods = jnp.split(mod, N_MOD, axis=1)
    mp = [m[:bp].reshape(bp, 1, d) for m in mods]
    ms = [jnp.repeat(m[bp:bp + bs], rows, axis=0).reshape(1, bs * rows, d) for m in mods]

    w_packed = _pack_w_in(w_in)
    w_out_bf = w_out.astype(BF16)
    w_rt = jnp.concatenate([w_group, w_router, jnp.zeros((d, LANES - N_GROUPS - N_EXPERTS), F32)], axis=1)
    wr_hi = w_rt.astype(BF16)
    wr_lo = (w_rt - wr_hi.astype(F32)).astype(BF16)
    b_rt = jnp.concatenate([b_group, b_router, jnp.zeros((LANES - N_GROUPS - N_EXPERTS,), F32)]).reshape(1, LANES)

    np_ = bp * tp
    xp2 = yp.reshape(np_, d)
    tm_p = _pick_tile(tp, 512)
    proj_p = _in_projection(xp2, mp[0], mp[1], norm1_w, w_packed, tm_p, tp)
    tq_p = _pick_tile(tp, 256)
    q_p, kf_p, kb_p, vb_p, qi_p, kif_p, kx_p = _attention_prep(
        proj_p, jnp.arange(tp), tq_p, q_norm_w, k_norm_w, idx_k_norm_w)
    oa_p = _dsa_prompt(q_p, qi_p, proj_p, kb_p, vb_p, kx_p, bp, tp)
    tt_p = _pick_tile(tp, 256)
    qn_p, kn_p, vv_p, bg_p = _delta_prep(proj_p, jnp.zeros((bp, SUBLANES, CONV_CHANNELS), F32), conv_w,
                                         a_log, dt_bias, bp, tp, tt_p, tp)
    od_p, ssm_p = _delta_chunks(qn_p, kn_p, vv_p, bg_p, proj_p, C_ZD // DELTA_WIDTH,
                                jnp.zeros((bp, N_DELTA_HEADS, HEAD_DIM, HEAD_DIM), F32), o_norm_w, bp, tp)
    tm_o = _pick_tile(tp, 256)
    x1_p, h2_p, lg_p = _out_projection(oa_p, od_p, xp2, mp[2], mp[3], mp[4], norm2_w, w_out_bf, wr_hi, wr_lo, b_rt,
                                       tm_o, tp)

    ns_ = bs * rows
    xs2 = jnp.pad(ys, ((0, 0), (0, rows - ts), (0, 0))).reshape(ns_, d)
    proj_s = _in_projection(xs2, ms[0], ms[1], norm1_w, w_packed, ns_, ns_)
    q_s, kf_s, kb_s, vb_s, qi_s, kif_s, _ = _attention_prep(
        proj_s, past + jnp.arange(rows), rows, q_norm_w, k_norm_w, idx_k_norm_w)
    q_t = qi_s.reshape(bs, rows, N_IDX_HEADS, IDX_DIM).transpose(0, 2, 1, 3).reshape(bs, N_IDX_HEADS * rows, IDX_DIM)
    w_col = proj_s[:, C_MISC + M_WI:C_MISC + M_WI + N_IDX_HEADS].reshape(bs, rows, N_IDX_HEADS)
    w_col = w_col.transpose(0, 2, 1).reshape(bs, N_IDX_HEADS * rows, 1)
    pages = _pick_tile(page_table.shape[1], 8, 1)
    keys_past, keys_new = _sample_scores(page_table, q_t, w_col, kif_s, cache_idx, pages, ts)
    n_sel_s = min(TOPK_MAX, (past + ts) // 4)
    oa_s = _sample_attend(page_table, keys_past, keys_new, q_s, kb_s, vb_s,
                          cache_k.reshape(-1, PAGE_SIZE, KV_WIDTH), cache_v.reshape(-1, PAGE_SIZE, KV_WIDTH),
                          pages, n_sel_s)
    prev8 = jnp.pad(state_conv, ((0, 0), (SUBLANES - (CONV_WIDTH - 1), 0), (0, 0)))
    qn_s, kn_s, vv_s, bg_s = _delta_prep(proj_s, prev8, conv_w, a_log, dt_bias, bs, rows, rows, ts)
    to_chunk = lambda a: jnp.pad(a.reshape(bs, rows, -1), ((0, 0), (0, DELTA_CHUNK - rows), (0, 0))).reshape(
        bs * DELTA_CHUNK, -1)
    z_s = proj_s[:, C_ZD:C_ZD + DELTA_WIDTH]
    od_s, ssm_s = _delta_chunks(to_chunk(qn_s), to_chunk(kn_s), to_chunk(vv_s), to_chunk(bg_s), to_chunk(z_s), 0,
                                state_ssm, o_norm_w, bs, DELTA_CHUNK)
    od_s = od_s.reshape(bs, DELTA_CHUNK, DELTA_WIDTH)[:, :rows].reshape(ns_, DELTA_WIDTH)
    x1_s, h2_s, lg_s = _out_projection(oa_s, od_s, xs2, ms[2], ms[3], ms[4], norm2_w, w_out_bf, wr_hi, wr_lo, b_rt,
                                       ns_, ns_)

    n_all = np_ + ns_
    h2_all = jnp.concatenate([h2_p, h2_s], axis=0)
    lg_all = jnp.concatenate([lg_p, lg_s], axis=0)
    eid, gates = _route(lg_all, _pick_tile(n_all, 512, SUBLANES))
    bm = 256
    slot_tok, dest, block_exp, n_active = _route_and_sort(eid[:, 0:2], bm)
    x_sorted = h2_all[slot_tok]
    yb = _moe_experts(block_exp, n_active, x_sorted, w_gate, w_up, w_down, bm)
    y0 = yb[dest[:, 0]]
    y1 = yb[dest[:, 1]]
    out_p = _combine(x1_p, y0[:np_], y1[:np_], gates[:np_], mp[5], tm_o, tp)
    out_s = _combine(x1_s, y0[np_:], y1[np_:], gates[np_:], ms[5], ns_, ns_)

    valid = lambda a: a.reshape(bs, rows, -1)[:, :ts]
    conv_p = proj_p[:, C_CONV:C_CONV + CONV_CHANNELS].reshape(bp, tp, CONV_CHANNELS)[:, tp - (CONV_WIDTH - 1):]
    conv_s = valid(proj_s[:, C_CONV:C_CONV + CONV_CHANNELS])[:, ts - (CONV_WIDTH - 1):]
    return (out_p.reshape(bp, tp, d), valid(out_s),
            kf_p.reshape(bp, tp, N_KV_HEADS, HEAD_DIM),
            proj_p[:, C_VA:C_VA + KV_WIDTH].reshape(bp, tp, N_KV_HEADS, HEAD_DIM),
            kif_p.reshape(bp, tp, IDX_DIM), ssm_p, conv_p,
            valid(kf_s).reshape(bs, ts, N_KV_HEADS, HEAD_DIM),
            valid(proj_s[:, C_VA:C_VA + KV_WIDTH]).reshape(bs, ts, N_KV_HEADS, HEAD_DIM),
            valid(kif_s), ssm_s, conv_s)


def kernel(x_prompt, x_sample, cache_k, cache_v, cache_idx_k, state_ssm, state_conv, page_table, c_prompt, c_sample,
           w_in, w_out, conv_w, a_log, dt_bias, q_norm_w, k_norm_w, idx_k_norm_w, o_norm_w, norm1_w, norm2_w,
           w_ada, b_ada, w_group, b_group, w_router, b_router, w_gate, w_up, w_down):
    depth = w_in.shape[0]
    yp, ys = x_prompt, x_sample
    per_layer = []
    for l in range(depth):
        res = _layer(yp, ys, cache_k[l], cache_v[l], cache_idx_k[l], state_ssm[l], state_conv[l], page_table,
                     c_prompt, c_sample, w_in[l], w_out[l], conv_w[l], a_log[l], dt_bias[l], q_norm_w[l],
                     k_norm_w[l], idx_k_norm_w[l], o_norm_w[l], norm1_w[l], norm2_w[l], w_ada[l], b_ada[l],
                     w_group[l], b_group[l], w_router[l], b_router[l], w_gate[l], w_up[l], w_down[l])
        yp, ys = res[0], res[1]
        per_layer.append(res[2:])
    stacked = tuple(jnp.stack([pl_[j] for pl_ in per_layer]) for j in range(10))
    return (yp, ys) + stacked
```

```python
import functools

import jax
import jax.numpy as jnp
import numpy as np
from jax import lax
from jax.experimental import pallas as pl
from jax.experimental.pallas import tpu as pltpu

F32 = jnp.float32
BF16 = jnp.bfloat16
I32 = jnp.int32

HEAD_DIM = 128
N_ATTN_HEADS = 8
N_KV_HEADS = 2
KV_GROUP = N_ATTN_HEADS // N_KV_HEADS
N_DELTA_HEADS = 8
N_IDX_HEADS = 16
IDX_DIM = 64
ATTN_WIDTH = N_ATTN_HEADS * HEAD_DIM
KV_WIDTH = N_KV_HEADS * HEAD_DIM
DELTA_WIDTH = N_DELTA_HEADS * HEAD_DIM
IDX_WIDTH = N_IDX_HEADS * IDX_DIM
CONV_CHANNELS = 3 * DELTA_WIDTH
TOPK_MAX = 256
ROPE_THETA = 500000.0
ROPE_FRACTION = 4
CONV_WIDTH = 4
DELTA_CHUNK = 64
N_GROUPS = 8
EXPERTS_PER_GROUP = 8
N_EXPERTS = N_GROUPS * EXPERTS_PER_GROUP
N_MOD = 6
EPS = 1e-6
PAGE_SIZE = 128
PROJ_SIZES = (ATTN_WIDTH, KV_WIDTH, KV_WIDTH, IDX_WIDTH, IDX_DIM, N_IDX_HEADS,
              DELTA_WIDTH, DELTA_WIDTH, DELTA_WIDTH, DELTA_WIDTH, N_DELTA_HEADS, N_DELTA_HEADS)

LANES = 128
SUBLANES = 8
VMEM_LIMIT = 56 * 1024 * 1024

C_QA = 0
C_QI = 1024
C_ZD = 2048
C_CONV = 3072
C_KA = 6144
C_VA = 6400
C_MISC = 6656
PROJ_PACKED = 6912
M_KI = 0
M_WI = 64
M_BD = 80
M_AD = 88

Q_TILE = 128
KEY_CHUNK = 256
SAMPLE_ROWS = 16
NEG_BIG = -1e30
INT_MIN = -2147483648
INT_MAX = 2147483647


def _cparams(sem):
    return pltpu.CompilerParams(dimension_semantics=sem, vmem_limit_bytes=VMEM_LIMIT)


def _dot(a, b):
    return jnp.dot(a, b, preferred_element_type=F32)


def _dot_nt(a, b):
    return lax.dot_general(a, b, (((1,), (1,)), ((), ())), preferred_element_type=F32)


def _dot_tn(a, b):
    return lax.dot_general(a, b, (((0,), (0,)), ((), ())), preferred_element_type=F32)


def _sigmoid(x):
    return 1.0 / (1.0 + jnp.exp(-x))


def _silu(x):
    return x * _sigmoid(x)


def _softplus(x):
    return jnp.maximum(x, 0.0) + jnp.log(1.0 + jnp.exp(-jnp.abs(x)))


def _ada_kernel(c_ref, w_ref, b_ref, o_ref):
    s = _silu(c_ref[...]).astype(BF16)
    o_ref[...] = _dot(s, w_ref[...].astype(BF16)) + b_ref[...]


def _ada_modulation(c, w_ada, b_ada):
    r, d = c.shape
    n = w_ada.shape[1]
    tn = 1024 if n % 1024 == 0 else n
    return pl.pallas_call(
        _ada_kernel,
        name="ada_mod",
        grid=(n // tn,),
        in_specs=[pl.BlockSpec((r, d), lambda j: (0, 0)),
                  pl.BlockSpec((d, tn), lambda j: (0, j)),
                  pl.BlockSpec((1, tn), lambda j: (0, j))],
        out_specs=pl.BlockSpec((r, tn), lambda j: (0, j)),
        out_shape=jax.ShapeDtypeStruct((r, n), F32),
        compiler_params=_cparams(("parallel",)),
    )(c, w_ada, b_ada.reshape(1, n))


def _inproj_kernel(x_ref, sh_ref, sc_ref, nw_ref, w_ref, o_ref, h_scr):
    @pl.when(pl.program_id(1) == 0)
    def _():
        x = x_ref[...]
        y = x * lax.rsqrt(jnp.mean(x * x, axis=-1, keepdims=True) + EPS) * nw_ref[...]
        h_scr[...] = (y * (1.0 + sc_ref[0]) + sh_ref[0]).astype(BF16)

    o_ref[...] = _dot(h_scr[...], w_ref[...])


def _in_projection(x2d, shift, scale, norm_w, w_packed, tm, rows_per_mod_block):
    n, d = x2d.shape
    np_ = w_packed.shape[1]
    tn = 1152
    r = shift.shape[1]
    tiles_per_mod = rows_per_mod_block // tm
    mod_spec = pl.BlockSpec((1, r, d), lambda i, j: (i // tiles_per_mod, 0, 0))
    return pl.pallas_call(
        _inproj_kernel,
        name="in_proj",
        grid=(n // tm, np_ // tn),
        in_specs=[pl.BlockSpec((tm, d), lambda i, j: (i, 0)),
                  mod_spec, mod_spec,
                  pl.BlockSpec((1, d), lambda i, j: (0, 0)),
                  pl.BlockSpec((d, tn), lambda i, j: (0, j))],
        out_specs=pl.BlockSpec((tm, tn), lambda i, j: (i, j)),
        out_shape=jax.ShapeDtypeStruct((n, np_), F32),
        scratch_shapes=[pltpu.VMEM((tm, d), BF16)],
        compiler_params=_cparams(("parallel", "arbitrary")),
    )(x2d, shift, scale, norm_w.reshape(1, d), w_packed)


def _rope(x, tab, rot):
    c = tab[:, 0:LANES]
    s1 = tab[:, LANES:2 * LANES]
    s2 = tab[:, 2 * LANES:3 * LANES]
    return x * c + pltpu.roll(x, LANES - rot, 1) * s1 + pltpu.roll(x, rot, 1) * s2


def _rms_head(x, w):
    return x * lax.rsqrt(jnp.mean(x * x, axis=-1, keepdims=True) + EPS) * w


def _prep_kernel(qa_ref, qi_ref, ka_ref, va_ref, misc_ref, tabm_ref, tabi_ref, qw_ref, kw_ref, iw_ref,
                 q_ref, kf_ref, kb_ref, vb_ref, qib_ref, kif_ref, kib_ref, *, transpose_v):
    tabm = tabm_ref[0]
    tabi = tabi_ref[0]
    half_main = HEAD_DIM // ROPE_FRACTION // 2
    half_idx = IDX_DIM // ROPE_FRACTION // 2
    for h in range(N_ATTN_HEADS):
        sl = slice(h * HEAD_DIM, (h + 1) * HEAD_DIM)
        y = _rope(_rms_head(qa_ref[:, sl], qw_ref[...]), tabm, half_main)
        q_ref[:, sl] = (y * (HEAD_DIM ** -0.5)).astype(BF16)
    for h in range(N_KV_HEADS):
        sl = slice(h * HEAD_DIM, (h + 1) * HEAD_DIM)
        y = _rope(_rms_head(ka_ref[:, sl], kw_ref[...]), tabm, half_main)
        kf_ref[:, sl] = y
        kb_ref[:, sl] = y.astype(BF16)
    if transpose_v:
        vb_ref[...] = va_ref[...].T.astype(BF16)
    else:
        vb_ref[...] = va_ref[...].astype(BF16)
    for p in range(IDX_WIDTH // LANES):
        sl = slice(p * LANES, (p + 1) * LANES)
        qib_ref[:, sl] = _rope(qi_ref[:, sl], tabi, half_idx).astype(BF16)
    m = misc_ref[...]
    lane = lax.broadcasted_iota(I32, m.shape, 1)
    ki = jnp.where(lane < IDX_DIM, m, 0.0)
    ms = jnp.sum(ki * ki, axis=-1, keepdims=True) * (1.0 / IDX_DIM)
    y = _rope(ki * lax.rsqrt(ms + EPS) * iw_ref[...], tabi, half_idx)
    kif_ref[...] = y[:, 0:IDX_DIM]
    kib_ref[...] = (y + pltpu.roll(y, IDX_DIM, 1)).astype(BF16)


def _rope_tables(pos, head_dim, group):
    d_rot = head_dim // ROPE_FRACTION
    half = d_rot // 2
    inv_freq = jnp.power(ROPE_THETA, -(jnp.arange(half, dtype=F32) * 2.0 / d_rot))
    ang = pos.astype(F32)[:, None] * inv_freq[None, :]
    cos = jnp.cos(ang)
    sin = jnp.sin(ang)
    t = pos.shape[0]
    z = jnp.zeros((t, group - d_rot), F32)
    c = jnp.concatenate([cos, cos, jnp.ones((t, group - d_rot), F32)], axis=1)
    s1 = jnp.concatenate([-sin, jnp.zeros((t, half), F32), z], axis=1)
    s2 = jnp.concatenate([jnp.zeros((t, half), F32), sin, z], axis=1)
    rep = LANES // group
    return jnp.concatenate([jnp.tile(c, (1, rep)), jnp.tile(s1, (1, rep)), jnp.tile(s2, (1, rep))], axis=1)


def _attention_prep(proj, pos, tq, q_norm_w, k_norm_w, idx_k_norm_w, transpose_v):
    n = proj.shape[0]
    p = pos.shape[0]
    g = p // tq
    tabm = _rope_tables(pos, HEAD_DIM, LANES).reshape(g, tq, 3 * LANES)
    tabi = _rope_tables(pos, IDX_DIM, IDX_DIM).reshape(g, tq, 3 * LANES)
    iw = jnp.concatenate([idx_k_norm_w, jnp.zeros((LANES - IDX_DIM,), F32)]).reshape(1, LANES)
    row = lambda w, c: pl.BlockSpec((tq, w), lambda i: (i, c // w))
    tab_spec = pl.BlockSpec((1, tq, 3 * LANES), lambda i: (i % g, 0, 0))
    vec_spec = pl.BlockSpec((1, LANES), lambda i: (0, 0))
    out_row = lambda w: pl.BlockSpec((tq, w), lambda i: (i, 0))
    v_spec = pl.BlockSpec((KV_WIDTH, tq), lambda i: (0, i)) if transpose_v else out_row(KV_WIDTH)
    v_shape = (KV_WIDTH, n) if transpose_v else (n, KV_WIDTH)
    return pl.pallas_call(
        functools.partial(_prep_kernel, transpose_v=transpose_v),
        name="attn_prep",
        grid=(n // tq,),
        in_specs=[row(ATTN_WIDTH, C_QA), row(IDX_WIDTH, C_QI), row(KV_WIDTH, C_KA), row(KV_WIDTH, C_VA),
                  row(LANES, C_MISC), tab_spec, tab_spec, vec_spec, vec_spec, vec_spec],
        out_specs=[out_row(ATTN_WIDTH), out_row(KV_WIDTH), out_row(KV_WIDTH), v_spec,
                   out_row(IDX_WIDTH), out_row(IDX_DIM), out_row(LANES)],
        out_shape=[jax.ShapeDtypeStruct((n, ATTN_WIDTH), BF16),
                   jax.ShapeDtypeStruct((n, KV_WIDTH), F32),
                   jax.ShapeDtypeStruct((n, KV_WIDTH), BF16),
                   jax.ShapeDtypeStruct(v_shape, BF16),
                   jax.ShapeDtypeStruct((n, IDX_WIDTH), BF16),
                   jax.ShapeDtypeStruct((n, IDX_DIM), F32),
                   jax.ShapeDtypeStruct((n, LANES), BF16)],
        compiler_params=_cparams(("parallel",)),
    )(proj, proj, proj, proj, proj, tabm, tabi,
      q_norm_w.reshape(1, LANES), k_norm_w.reshape(1, LANES), iw)


def _sort_key(x):
    b = pltpu.bitcast(x + 0.0, I32)
    return b ^ ((b >> 31) & INT_MAX)


def _kth_largest_key(count_ge, k, shape):
    def body(it, ans_u):
        bit = jnp.left_shift(jnp.int32(1), 31 - it)
        cand_u = ans_u | bit
        cnt = count_ge(cand_u ^ INT_MIN)
        return jnp.where(cnt >= k, cand_u, ans_u)

    ans_u = lax.fori_loop(0, 32, body, jnp.zeros(shape, I32))
    return ans_u ^ INT_MIN


def _tie_index_limit(count_eq_le, need, n_keys, shape):
    nbits = max(1, int(n_keys - 1).bit_length())

    def body(it, lo):
        bit = jnp.left_shift(jnp.int32(1), nbits - 1 - it)
        cand = lo | bit
        cnt = count_eq_le(cand - 1)
        return jnp.where(cnt >= need, lo, cand)

    return lax.fori_loop(0, nbits, body, jnp.zeros(shape, I32))


def _dsa_prompt_kernel(q_ref, qi_ref, misc_ref, k_ref, vt_ref, kx_ref, o_ref,
                       key_scr, qsel_scr, thr_scr, lim_scr, m_scr, l_scr, acc_scr, *, n_sel):
    i = pl.program_id(1)
    tq = Q_TILE
    ck = KEY_CHUNK
    n_ch = (i * tq + tq + ck - 1) // ck
    q_pos = i * tq + lax.broadcasted_iota(I32, (1, tq), 1)
    row_k = lax.broadcasted_iota(I32, (ck, 1), 0)

    lo_half = lax.broadcasted_iota(I32, (tq, LANES), 1) < IDX_DIM
    zero = jnp.zeros((), BF16)
    for p in range(IDX_WIDTH // LANES):
        slab = qi_ref[:, p * LANES:(p + 1) * LANES]
        qsel_scr[(2 * p) * tq:(2 * p + 1) * tq, :] = jnp.where(lo_half, slab, zero)
        qsel_scr[(2 * p + 1) * tq:(2 * p + 2) * tq, :] = jnp.where(lo_half, zero, slab)
    w_t = misc_ref[...].T

    def score_chunk(c, carry):
        off = pl.multiple_of(c * ck, ck)
        s = _dot_nt(kx_ref[pl.ds(off, ck), :], qsel_scr[...])
        acc = jnp.zeros((ck, tq), F32)
        for h in range(N_IDX_HEADS):
            acc = acc + w_t[M_WI + h:M_WI + h + 1, :] * jnp.maximum(s[:, h * tq:(h + 1) * tq], 0.0)
        acc = jnp.where(off + row_k <= q_pos, acc, -jnp.inf)
        key_scr[pl.ds(off, ck), :] = _sort_key(acc)
        return carry

    lax.fori_loop(0, n_ch, score_chunk, 0)

    def count_where(pred):
        def body(c, cnt):
            off = pl.multiple_of(c * ck, ck)
            hit = pred(key_scr[pl.ds(off, ck), :], off + row_k).astype(F32)
            return cnt + jnp.sum(hit.reshape(ck // SUBLANES, SUBLANES, tq), axis=0)
        cnt = lax.fori_loop(0, n_ch, body, jnp.zeros((SUBLANES, tq), F32))
        return jnp.sum(cnt, axis=0, keepdims=True)

    thr_scr[...] = jnp.full((1, tq), INT_MIN, I32)
    lim_scr[...] = jnp.full((1, tq), INT_MAX, I32)

    @pl.when((i + 1) * tq > n_sel)
    def _():
        t = _kth_largest_key(lambda cand: count_where(lambda kk, pos: kk >= cand), float(n_sel), (1, tq))
        thr_scr[...] = t
        n_gt = count_where(lambda kk, pos: kk > t)
        n_ge = count_where(lambda kk, pos: kk >= t)

        @pl.when(jnp.max(n_ge) > float(n_sel))
        def _():
            lim_scr[...] = _tie_index_limit(
                lambda idx: count_where(lambda kk, pos: jnp.logical_and(kk == t, pos <= idx)),
                float(n_sel) - n_gt, k_ref.shape[0], (1, tq))

    thr = thr_scr[...]
    lim = lim_scr[...]

    for g in range(N_KV_HEADS):
        qg = jnp.concatenate(
            [q_ref[:, (g * KV_GROUP + r) * HEAD_DIM:(g * KV_GROUP + r + 1) * HEAD_DIM] for r in range(KV_GROUP)],
            axis=0)
        m_scr[...] = jnp.full(m_scr.shape, NEG_BIG, F32)
        l_scr[...] = jnp.zeros(l_scr.shape, F32)
        acc_scr[...] = jnp.zeros(acc_scr.shape, F32)

        def attend_chunk(c, carry):
            off = pl.multiple_of(c * ck, ck)
            kc = k_ref[pl.ds(off, ck), g * HEAD_DIM:(g + 1) * HEAD_DIM]
            vt = vt_ref[g * HEAD_DIM:(g + 1) * HEAD_DIM, pl.ds(off, ck)]
            kk = key_scr[pl.ds(off, ck), :]
            pos = off + row_k
            sel = jnp.logical_or(kk > thr, jnp.logical_and(kk == thr, pos <= lim))
            sel = jnp.logical_and(sel, pos <= q_pos)
            sel = jnp.concatenate([sel] * KV_GROUP, axis=1)
            s = jnp.where(sel, _dot_nt(kc, qg), NEG_BIG)
            m_old = m_scr[...]
            m_new = jnp.maximum(m_old, jnp.max(s, axis=0, keepdims=True))
            p = jnp.where(sel, jnp.exp(s - m_new), 0.0)
            alpha = jnp.exp(m_old - m_new)
            l_scr[...] = alpha * l_scr[...] + jnp.sum(p, axis=0, keepdims=True)
            acc_scr[...] = alpha * acc_scr[...] + _dot(vt, p.astype(BF16))
            m_scr[...] = m_new
            return carry

        lax.fori_loop(0, n_ch, attend_chunk, 0)
        o_t = acc_scr[...] / l_scr[...]
        for r in range(KV_GROUP):
            h = g * KV_GROUP + r
            o_ref[:, h * HEAD_DIM:(h + 1) * HEAD_DIM] = o_t[:, r * tq:(r + 1) * tq].T.astype(BF16)


def _dsa_prompt(q_bf, qi_bf, proj, k_bf, vt_bf, kx_bf, b, t):
    n = b * t
    nq = t // Q_TILE
    n_sel = min(TOPK_MAX, t // 4)
    qrow = lambda w: pl.BlockSpec((Q_TILE, w), lambda bb, i: (bb * nq + i, 0))
    seq = lambda w: pl.BlockSpec((t, w), lambda bb, i: (bb, 0))
    return pl.pallas_call(
        functools.partial(_dsa_prompt_kernel, n_sel=n_sel),
        name="dsa_prompt",
        grid=(b, nq),
        in_specs=[qrow(ATTN_WIDTH), qrow(IDX_WIDTH),
                  pl.BlockSpec((Q_TILE, LANES), lambda bb, i: (bb * nq + i, C_MISC // LANES)),
                  seq(KV_WIDTH), pl.BlockSpec((KV_WIDTH, t), lambda bb, i: (0, bb)), seq(LANES)],
        out_specs=qrow(ATTN_WIDTH),
        out_shape=jax.ShapeDtypeStruct((n, ATTN_WIDTH), BF16),
        scratch_shapes=[pltpu.VMEM((t, Q_TILE), I32),
                        pltpu.VMEM((N_IDX_HEADS * Q_TILE, LANES), BF16),
                        pltpu.VMEM((1, Q_TILE), I32),
                        pltpu.VMEM((1, Q_TILE), I32),
                        pltpu.VMEM((1, KV_GROUP * Q_TILE), F32),
                        pltpu.VMEM((1, KV_GROUP * Q_TILE), F32),
                        pltpu.VMEM((HEAD_DIM, KV_GROUP * Q_TILE), F32)],
        compiler_params=_cparams(("parallel", "arbitrary")),
    )(q_bf, qi_bf, proj, k_bf, vt_bf, kx_bf)


def _sample_score_kernel(pt_ref, q_ref, w_ref, kn_ref, *refs, pages, t_valid):
    page_refs = refs[:pages]
    past_ref, new_ref = refs[pages], refs[pages + 1]
    rows = SAMPLE_ROWS
    q = q_ref[0]
    w = w_ref[0]

    def head_sum(s):
        s = w * jnp.maximum(s, 0.0)
        acc = s[0:rows]
        for h in range(1, N_IDX_HEADS):
            acc = acc + s[h * rows:(h + 1) * rows]
        return acc

    for j in range(pages):
        kp = page_refs[j][0].astype(BF16)
        past_ref[0, :, j * PAGE_SIZE:(j + 1) * PAGE_SIZE] = _sort_key(head_sum(_dot_nt(q, kp)))

    @pl.when(pl.program_id(1) == 0)
    def _():
        kn = jnp.concatenate([kn_ref[...], jnp.zeros((LANES - rows, IDX_DIM), F32)], axis=0).astype(BF16)
        sc = head_sum(_dot_nt(q, kn))
        t = lax.broadcasted_iota(I32, sc.shape, 0)
        s = lax.broadcasted_iota(I32, sc.shape, 1)
        ok = jnp.logical_and(s <= t, s < t_valid)
        new_ref[0] = _sort_key(jnp.where(ok, sc, -jnp.inf))


def _sample_scores(page_table, q_t, w_col, kif, cache_idx, pages, t_valid):
    bs, n_pages = page_table.shape
    past = n_pages * PAGE_SIZE
    hr = N_IDX_HEADS * SAMPLE_ROWS
    page_spec = lambda j: pl.BlockSpec((1, PAGE_SIZE, IDX_DIM), lambda b, c, pt: (pt[b, c * pages + j], 0, 0))
    grid_spec = pltpu.PrefetchScalarGridSpec(
        num_scalar_prefetch=1,
        grid=(bs, n_pages // pages),
        in_specs=[pl.BlockSpec((1, hr, IDX_DIM), lambda b, c, pt: (b, 0, 0)),
                  pl.BlockSpec((1, hr, 1), lambda b, c, pt: (b, 0, 0)),
                  pl.BlockSpec((SAMPLE_ROWS, IDX_DIM), lambda b, c, pt: (b, 0))]
                 + [page_spec(j) for j in range(pages)],
        out_specs=[pl.BlockSpec((1, SAMPLE_ROWS, pages * PAGE_SIZE), lambda b, c, pt: (b, 0, c)),
                   pl.BlockSpec((1, SAMPLE_ROWS, LANES), lambda b, c, pt: (b, 0, 0))],
    )
    return pl.pallas_call(
        functools.partial(_sample_score_kernel, pages=pages, t_valid=t_valid),
        name="sample_scores",
        grid_spec=grid_spec,
        out_shape=[jax.ShapeDtypeStruct((bs, SAMPLE_ROWS, past), I32),
                   jax.ShapeDtypeStruct((bs, SAMPLE_ROWS, LANES), I32)],
        compiler_params=_cparams(("parallel", "arbitrary")),
    )(page_table, q_t, w_col, kif, *([cache_idx] * pages))


def _sample_attend_kernel(pt_ref, kp_ref, kn_ref, q_ref, knew_ref, vnew_ref, *refs, pages, n_sel, past):
    k_pages = refs[:pages]
    v_pages = refs[pages:2 * pages]
    o_ref = refs[2 * pages]
    thr_scr, lim_scr, m_scr, l_scr, acc_scr = refs[2 * pages + 1:]
    c = pl.program_id(1)
    rows = SAMPLE_ROWS
    span = pages * PAGE_SIZE

    @pl.when(c == 0)
    def _():
        m_scr[...] = jnp.full(m_scr.shape, NEG_BIG, F32)
        l_scr[...] = jnp.zeros(l_scr.shape, F32)
        acc_scr[...] = jnp.zeros(acc_scr.shape, F32)
        kp = kp_ref[0]
        kn = kn_ref[0]
        pos_p = lax.broadcasted_iota(I32, kp.shape, 1)
        pos_n = past + lax.broadcasted_iota(I32, kn.shape, 1)

        def count_where(pred):
            return (jnp.sum(pred(kp, pos_p).astype(F32), axis=1, keepdims=True)
                    + jnp.sum(pred(kn, pos_n).astype(F32), axis=1, keepdims=True))

        t = _kth_largest_key(lambda cand: count_where(lambda kk, pos: kk >= cand), float(n_sel), (rows, 1))
        thr_scr[...] = t
        lim_scr[...] = jnp.full((rows, 1), INT_MAX, I32)
        n_gt = count_where(lambda kk, pos: kk > t)
        n_ge = count_where(lambda kk, pos: kk >= t)

        @pl.when(jnp.max(n_ge) > float(n_sel))
        def _():
            lim_scr[...] = _tie_index_limit(
                lambda idx: count_where(lambda kk, pos: jnp.logical_and(kk == t, pos <= idx)),
                float(n_sel) - n_gt, past + LANES, (rows, 1))

    thr = thr_scr[...]
    lim = lim_scr[...]

    def update(g, qg, kc, vc, sel):
        sel = jnp.concatenate([sel] * KV_GROUP, axis=0)
        s = jnp.where(sel, _dot_nt(qg, kc), NEG_BIG)
        m_old = m_scr[g]
        m_new = jnp.maximum(m_old, jnp.max(s, axis=1, keepdims=True))
        p = jnp.where(sel, jnp.exp(s - m_new), 0.0)
        alpha = jnp.exp(m_old - m_new)
        l_scr[g] = alpha * l_scr[g] + jnp.sum(p, axis=1, keepdims=True)
        acc_scr[g] = alpha * acc_scr[g] + _dot(p.astype(BF16), vc)
        m_scr[g] = m_new

    def select(kk, pos):
        return jnp.logical_or(kk > thr, jnp.logical_and(kk == thr, pos <= lim))

    k_cat = jnp.concatenate([r[0] for r in k_pages], axis=0).astype(BF16)
    v_cat = jnp.concatenate([r[0] for r in v_pages], axis=0).astype(BF16)
    off = pl.multiple_of(c * span, span)
    kk = kp_ref[0, :, pl.ds(off, span)]
    sel_past = select(kk, off + lax.broadcasted_iota(I32, kk.shape, 1))
    q_groups = []
    for g in range(N_KV_HEADS):
        qg = jnp.concatenate(
            [q_ref[:, (g * KV_GROUP + r) * HEAD_DIM:(g * KV_GROUP + r + 1) * HEAD_DIM] for r in range(KV_GROUP)],
            axis=0)
        q_groups.append(qg)
        update(g, qg, k_cat[:, g * HEAD_DIM:(g + 1) * HEAD_DIM], v_cat[:, g * HEAD_DIM:(g + 1) * HEAD_DIM], sel_past)

    @pl.when(c == pl.num_programs(1) - 1)
    def _():
        kn = kn_ref[0]
        lane = lax.broadcasted_iota(I32, kn.shape, 1)
        sel_new = jnp.logical_and(select(kn, past + lane), lane < rows)
        pad = jnp.zeros((LANES - rows, KV_WIDTH), BF16)
        k_new = jnp.concatenate([knew_ref[...], pad], axis=0)
        v_new = jnp.concatenate([vnew_ref[...], pad], axis=0)
        for g in range(N_KV_HEADS):
            sl = slice(g * HEAD_DIM, (g + 1) * HEAD_DIM)
            update(g, q_groups[g], k_new[:, sl], v_new[:, sl], sel_new)
            o = acc_scr[g] / l_scr[g]
            for r in range(KV_GROUP):
                h = g * KV_GROUP + r
                o_ref[:, h * HEAD_DIM:(h + 1) * HEAD_DIM] = o[r * rows:(r + 1) * rows].astype(BF16)


def _sample_attend(page_table, keys_past, keys_new, q_bf, k_bf, v_bf, cache_k, cache_v, pages, n_sel):
    bs, n_pages = page_table.shape
    past = n_pages * PAGE_SIZE
    page_spec = lambda j: pl.BlockSpec((1, PAGE_SIZE, KV_WIDTH), lambda b, c, pt: (pt[b, c * pages + j], 0, 0))
    row = lambda w: pl.BlockSpec((SAMPLE_ROWS, w), lambda b, c, pt: (b, 0))
    grid_spec = pltpu.PrefetchScalarGridSpec(
        num_scalar_prefetch=1,
        grid=(bs, n_pages // pages),
        in_specs=[pl.BlockSpec((1, SAMPLE_ROWS, past), lambda b, c, pt: (b, 0, 0)),
                  pl.BlockSpec((1, SAMPLE_ROWS, LANES), lambda b, c, pt: (b, 0, 0)),
                  row(ATTN_WIDTH), row(KV_WIDTH), row(KV_WIDTH)]
                 + [page_spec(j) for j in range(pages)] * 2,
        out_specs=row(ATTN_WIDTH),
        scratch_shapes=[pltpu.VMEM((SAMPLE_ROWS, 1), I32),
                        pltpu.VMEM((SAMPLE_ROWS, 1), I32),
                        pltpu.VMEM((N_KV_HEADS, KV_GROUP * SAMPLE_ROWS, 1), F32),
                        pltpu.VMEM((N_KV_HEADS, KV_GROUP * SAMPLE_ROWS, 1), F32),
                        pltpu.VMEM((N_KV_HEADS, KV_GROUP * SAMPLE_ROWS, HEAD_DIM), F32)],
    )
    return pl.pallas_call(
        functools.partial(_sample_attend_kernel, pages=pages, n_sel=n_sel, past=past),
        name="sample_attend",
        grid_spec=grid_spec,
        out_shape=jax.ShapeDtypeStruct((bs * SAMPLE_ROWS, ATTN_WIDTH), BF16),
        compiler_params=_cparams(("parallel", "arbitrary")),
    )(page_table, keys_past, keys_new, q_bf, k_bf, v_bf, *([cache_k] * pages), *([cache_v] * pages))


def _delta_prep_kernel(x_ref, halo_ref, prev_ref, misc_ref, cw_ref, al_ref, dt_ref,
                       qn_ref, kn_ref, vv_ref, bg_ref, xp_scr, *, tiles_per_seq, t_valid, tt):
    i = pl.program_id(0)
    tile_in_seq = i % tiles_per_seq
    halo = jnp.where(tile_in_seq == 0, prev_ref[0], halo_ref[...])
    xp_scr[0:SUBLANES, :] = halo
    xp_scr[SUBLANES:SUBLANES + tt, :] = x_ref[...]
    base = SUBLANES - (CONV_WIDTH - 1)
    outs = (qn_ref, kn_ref, vv_ref)
    for sec in range(3):
        for h in range(N_DELTA_HEADS):
            col = sec * DELTA_WIDTH + h * HEAD_DIM
            sl = slice(col, col + HEAD_DIM)
            y = xp_scr[base:base + tt, sl] * cw_ref[0:1, sl]
            for j in range(1, CONV_WIDTH):
                y = y + xp_scr[base + j:base + j + tt, sl] * cw_ref[j:j + 1, sl]
            y = _silu(y)
            if sec < 2:
                y = y * lax.rsqrt(jnp.sum(y * y, axis=-1, keepdims=True) + EPS)
            if sec == 0:
                y = y * (HEAD_DIM ** -0.5)
            outs[sec][:, h * HEAD_DIM:(h + 1) * HEAD_DIM] = y
    m = misc_ref[...]
    lane = lax.broadcasted_iota(I32, m.shape, 1)
    row = tile_in_seq * tt + lax.broadcasted_iota(I32, m.shape, 0)
    beta = _sigmoid(m)
    g = -jnp.exp(al_ref[...]) * _softplus(m + dt_ref[...])
    is_b = jnp.logical_and(lane >= M_BD, lane < M_BD + N_DELTA_HEADS)
    is_g = jnp.logical_and(lane >= M_AD, lane < M_AD + N_DELTA_HEADS)
    comb = jnp.where(is_b, beta, jnp.where(is_g, g, 0.0))
    comb = jnp.where(row < t_valid, comb, 0.0)
    bg_ref[...] = pltpu.roll(comb, LANES - M_BD, 1)


def _delta_prep(proj, prev8, conv_w, a_log, dt_bias, b, t, tt, t_valid):
    n = proj.shape[0]
    tiles_per_seq = t // tt
    pad_vec = lambda v: jnp.zeros((1, LANES), F32).at[0, M_AD:M_AD + N_DELTA_HEADS].set(v)
    halo_blocks = tt // SUBLANES
    return pl.pallas_call(
        functools.partial(_delta_prep_kernel, tiles_per_seq=tiles_per_seq, t_valid=t_valid, tt=tt),
        name="delta_prep",
        grid=(n // tt,),
        in_specs=[pl.BlockSpec((tt, CONV_CHANNELS), lambda i: (i, C_CONV // CONV_CHANNELS)),
                  pl.BlockSpec((SUBLANES, CONV_CHANNELS),
                               lambda i: (jnp.maximum(i * halo_blocks - 1, 0), C_CONV // CONV_CHANNELS)),
                  pl.BlockSpec((1, SUBLANES, CONV_CHANNELS), lambda i: (i // tiles_per_seq, 0, 0)),
                  pl.BlockSpec((tt, LANES), lambda i: (i, C_MISC // LANES)),
                  pl.BlockSpec((CONV_WIDTH, CONV_CHANNELS), lambda i: (0, 0)),
                  pl.BlockSpec((1, LANES), lambda i: (0, 0)),
                  pl.BlockSpec((1, LANES), lambda i: (0, 0))],
        out_specs=[pl.BlockSpec((tt, DELTA_WIDTH), lambda i: (i, 0))] * 3
                  + [pl.BlockSpec((tt, LANES), lambda i: (i, 0))],
        out_shape=[jax.ShapeDtypeStruct((n, DELTA_WIDTH), F32)] * 3 + [jax.ShapeDtypeStruct((n, LANES), F32)],
        scratch_shapes=[pltpu.VMEM((SUBLANES + tt, CONV_CHANNELS), F32)],
        compiler_params=_cparams(("parallel",)),
    )(proj, proj, prev8, proj, conv_w, pad_vec(a_log), pad_vec(dt_bias))


def _mm(a, b):
    return _dot(a.astype(BF16), b.astype(BF16))


def _mm_nt(a, b):
    return _dot_nt(a.astype(BF16), b.astype(BF16))


DELTA_INV_BLOCK = 16
DELTA_STACK = 4


def _delta_chunk_kernel(qn_ref, kn_ref, vv_ref, bg_ref, z_ref, s0_ref, ow_ref, od_ref, so_ref, s_scr):
    c = pl.program_id(1)
    cs = DELTA_CHUNK

    @pl.when(c == 0)
    def _():
        s_scr[...] = s0_ref[0]

    bg = bg_ref[...]
    ltri = (lax.broadcasted_iota(I32, (cs, cs), 0) >= lax.broadcasted_iota(I32, (cs, cs), 1)).astype(BF16)
    g1 = bg.astype(BF16)
    r1 = bg - g1.astype(F32)
    g2 = r1.astype(BF16)
    g3 = (r1 - g2.astype(F32)).astype(BF16)
    gc = _dot(ltri, g1) + _dot(ltri, g2) + _dot(ltri, g3)
    gct = gc.T

    gh = DELTA_STACK
    rows = gh * cs
    rr = lax.broadcasted_iota(I32, (rows, rows), 0)
    cc = lax.broadcasted_iota(I32, (rows, rows), 1)
    same = (rr // cs) == (cc // cs)
    causal = jnp.logical_and(same, rr >= cc)
    strict = jnp.logical_and(same, rr > cc)
    eye = (rr == cc).astype(F32)
    row_head = lax.broadcasted_iota(I32, (rows, 1), 0) // cs
    for grp in range(N_DELTA_HEADS // gh):
        heads = [grp * gh + j for j in range(gh)]
        stack = lambda ref: jnp.concatenate([ref[:, h * HEAD_DIM:(h + 1) * HEAD_DIM] for h in heads], axis=0)
        col = lambda a, lane0: jnp.concatenate([a[:, lane0 + h:lane0 + h + 1] for h in heads], axis=0)
        k = stack(kn_ref)
        q = stack(qn_ref)
        v = stack(vv_ref)
        bcol = col(bg, 0)
        gcc = col(gc, N_DELTA_HEADS)
        gcr = jnp.concatenate([gct[N_DELTA_HEADS + h:N_DELTA_HEADS + h + 1, :] for h in heads], axis=1)
        g_last = [gc[cs - 1:cs, N_DELTA_HEADS + h:N_DELTA_HEADS + h + 1] for h in heads]
        glc = jnp.concatenate([jnp.broadcast_to(gl, (cs, 1)) for gl in g_last], axis=0)
        decay = jnp.exp(jnp.where(causal, gcc - gcr, -jnp.inf))
        kb = k * bcol
        eg = jnp.exp(gcc)
        kq = _mm_nt(jnp.concatenate([kb, q], axis=0), k)
        a = jnp.where(strict, kq[0:rows] * decay, 0.0)
        intra = jnp.where(causal, kq[rows:2 * rows] * decay, 0.0)
        x = -a
        nb = DELTA_INV_BLOCK
        y = jnp.where((rr // nb) == (cc // nb), x, 0.0)
        p = eye + y
        y = _mm(y, y)
        n_sq = max(1, int(nb - 1).bit_length())
        for lvl in range(1, n_sq):
            if lvl < n_sq - 1:
                py = _mm(jnp.concatenate([p, y], axis=0), y)
                p = p + py[0:rows]
                y = py[rows:2 * rows]
            else:
                p = p + _mm(p, y)
        size = 2 * nb
        while size <= cs:
            off = jnp.where(jnp.logical_and((rr // size) == (cc // size), (rr // (size // 2)) != (cc // (size // 2))),
                            x, 0.0)
            p = p + _mm(_mm(p, off), p)
            size *= 2
        sol = _mm(p, jnp.concatenate([v * bcol, kb * eg], axis=1))
        u = sol[:, 0:HEAD_DIM]
        w = sol[:, HEAD_DIM:2 * HEAD_DIM]
        lanes_g = slice(grp * gh * HEAD_DIM, (grp + 1) * gh * HEAD_DIM)
        s_g = s_scr[:, lanes_g]
        wq_s = _mm(jnp.concatenate([w, q * eg], axis=0), s_g)
        own = lambda m, r0: jnp.concatenate(
            [m[r0 + j * cs:r0 + (j + 1) * cs, j * HEAD_DIM:(j + 1) * HEAD_DIM] for j in range(gh)], axis=0)
        v_new = u - own(wq_s, 0)
        o = own(wq_s, rows) + _mm(intra, v_new)
        kg_t = (k * jnp.exp(glc - gcc)).T
        vn_blocks = jnp.concatenate([jnp.where(row_head == j, v_new, 0.0) for j in range(gh)], axis=1)
        s_decay = jnp.concatenate([jnp.broadcast_to(jnp.exp(gl), (1, HEAD_DIM)) for gl in g_last], axis=1)
        s_scr[:, lanes_g] = s_g * s_decay + _mm(kg_t, vn_blocks)
        on = o * lax.rsqrt(jnp.mean(o * o, axis=-1, keepdims=True) + EPS) * ow_ref[...]
        for j, h in enumerate(heads):
            sl = slice(h * HEAD_DIM, (h + 1) * HEAD_DIM)
            od_ref[:, sl] = (on[j * cs:(j + 1) * cs] * _silu(z_ref[:, sl])).astype(BF16)

    @pl.when(c == pl.num_programs(1) - 1)
    def _():
        so_ref[0] = s_scr[...]


def _delta_chunks(qn, kn, vv, bg, zsrc, z_col_block, state0, o_norm_w, b, t):
    n = b * t
    nc = t // DELTA_CHUNK
    sw = N_DELTA_HEADS * HEAD_DIM
    row = lambda w, cb=0: pl.BlockSpec((DELTA_CHUNK, w), lambda bb, c: (bb * nc + c, cb))
    st = pl.BlockSpec((1, HEAD_DIM, sw), lambda bb, c: (bb, 0, 0))
    s_in = state0.transpose(0, 2, 1, 3).reshape(b, HEAD_DIM, sw)
    od, s_out = pl.pallas_call(
        _delta_chunk_kernel,
        name="delta_chunks",
        grid=(b, nc),
        in_specs=[row(DELTA_WIDTH), row(DELTA_WIDTH), row(DELTA_WIDTH), row(LANES),
                  row(DELTA_WIDTH, z_col_block), st, pl.BlockSpec((1, LANES), lambda bb, c: (0, 0))],
        out_specs=[row(DELTA_WIDTH), st],
        out_shape=[jax.ShapeDtypeStruct((n, DELTA_WIDTH), BF16),
                   jax.ShapeDtypeStruct((b, HEAD_DIM, sw), F32)],
        scratch_shapes=[pltpu.VMEM((HEAD_DIM, sw), F32)],
        compiler_params=_cparams(("parallel", "arbitrary")),
    )(qn, kn, vv, bg, zsrc, s_in, o_norm_w.reshape(1, LANES))
    return od, s_out.reshape(b, HEAD_DIM, N_DELTA_HEADS, HEAD_DIM).transpose(0, 2, 1, 3)


def _outproj_kernel(oa_ref, od_ref, x_ref, g1_ref, sh_ref, sc_ref, nw_ref, wo_ref, wrh_ref, wrl_ref, br_ref,
                    x1_ref, h2_ref, lg_ref):
    mix = _dot(oa_ref[...], wo_ref[0:ATTN_WIDTH, :]) + _dot(od_ref[...], wo_ref[ATTN_WIDTH:ATTN_WIDTH + DELTA_WIDTH, :])
    x1 = x_ref[...] + g1_ref[0] * mix
    x1_ref[...] = x1
    y = x1 * lax.rsqrt(jnp.mean(x1 * x1, axis=-1, keepdims=True) + EPS) * nw_ref[...]
    h2 = y * (1.0 + sc_ref[0]) + sh_ref[0]
    hb = h2.astype(BF16)
    h2_ref[...] = hb
    lo = (h2 - hb.astype(F32)).astype(BF16)
    lg_ref[...] = _dot(hb, wrh_ref[...]) + _dot(lo, wrh_ref[...]) + _dot(hb, wrl_ref[...]) + br_ref[...]


def _out_projection(o_attn, o_delta, x2d, gate1, shift2, scale2, norm2_w, w_out_bf, wr_hi, wr_lo, b_rt,
                    tm, rows_per_mod_block):
    n, d = x2d.shape
    r = gate1.shape[1]
    tiles_per_mod = rows_per_mod_block // tm
    mod_spec = pl.BlockSpec((1, r, d), lambda i: (i // tiles_per_mod, 0, 0))
    row = lambda w: pl.BlockSpec((tm, w), lambda i: (i, 0))
    full = lambda a: pl.BlockSpec(a.shape, lambda i: (0, 0))
    return pl.pallas_call(
        _outproj_kernel,
        name="out_proj",
        grid=(n // tm,),
        in_specs=[row(ATTN_WIDTH), row(DELTA_WIDTH), row(d), mod_spec, mod_spec, mod_spec,
                  pl.BlockSpec((1, d), lambda i: (0, 0)), full(w_out_bf), full(wr_hi), full(wr_lo), full(b_rt)],
        out_specs=[row(d), row(d), row(LANES)],
        out_shape=[jax.ShapeDtypeStruct((n, d), F32), jax.ShapeDtypeStruct((n, d), BF16),
                   jax.ShapeDtypeStruct((n, LANES), F32)],
        compiler_params=_cparams(("parallel",)),
    )(o_attn, o_delta, x2d, gate1, shift2, scale2, norm2_w.reshape(1, d), w_out_bf, wr_hi, wr_lo, b_rt)


def _route_kernel(lg_ref, eid_ref, gate_ref):
    x = lg_ref[...]
    lane = lax.broadcasted_iota(I32, x.shape, 1)
    gl = jnp.where(lane < N_GROUPS, x, -jnp.inf)
    ge = jnp.exp(gl - jnp.max(gl, axis=1, keepdims=True))
    p = ge / jnp.sum(ge, axis=1, keepdims=True)
    p_max = jnp.max(p, axis=1, keepdims=True)
    grp = jnp.min(jnp.where(p == p_max, lane, LANES), axis=1, keepdims=True)
    e_lane = lane - N_GROUPS
    in_grp = jnp.logical_and(jnp.logical_and(e_lane >= 0, e_lane < N_EXPERTS),
                             (e_lane >> 3) == grp)
    rl = jnp.where(in_grp, x, -jnp.inf)
    v1 = jnp.max(rl, axis=1, keepdims=True)
    i1 = jnp.min(jnp.where(rl == v1, lane, LANES), axis=1, keepdims=True)
    rl2 = jnp.where(lane == i1, -jnp.inf, rl)
    v2 = jnp.max(rl2, axis=1, keepdims=True)
    i2 = jnp.min(jnp.where(rl2 == v2, lane, LANES), axis=1, keepdims=True)
    t = jnp.exp(v2 - v1)
    den = 1.0 + t
    eid_ref[...] = jnp.where(lane == 0, i1 - N_GROUPS, jnp.where(lane == 1, i2 - N_GROUPS, 0))
    gate_ref[...] = jnp.where(lane == 0, (1.0 / den) * p_max, jnp.where(lane == 1, (t / den) * p_max, 0.0))


def _route(logits, tm):
    n = logits.shape[0]
    spec = pl.BlockSpec((tm, LANES), lambda i: (i, 0))
    return pl.pallas_call(
        _route_kernel,
        name="route",
        grid=(n // tm,),
        in_specs=[spec],
        out_specs=[spec, spec],
        out_shape=[jax.ShapeDtypeStruct((n, LANES), I32), jax.ShapeDtypeStruct((n, LANES), F32)],
        compiler_params=_cparams(("parallel",)),
    )(logits)


def _moe_kernel(be_ref, na_ref, x_ref, wg_ref, wu_ref, wd_ref, o_ref, wg_scr, wu_scr, wd_scr):
    i = pl.program_id(0)

    @pl.when(i < na_ref[0])
    def _():
        changed = jnp.logical_or(i == 0, be_ref[i] != be_ref[jnp.maximum(i - 1, 0)])

        @pl.when(changed)
        def _():
            wg_scr[...] = wg_ref[0].astype(BF16)
            wu_scr[...] = wu_ref[0].astype(BF16)
            wd_scr[...] = wd_ref[0].astype(BF16)

        x = x_ref[...]
        hid = _silu(_dot(x, wg_scr[...])) * _dot(x, wu_scr[...])
        o_ref[...] = _dot(hid.astype(BF16), wd_scr[...])

    @pl.when(i >= na_ref[0])
    def _():
        o_ref[...] = jnp.zeros(o_ref.shape, F32)


def _moe_experts(block_exp, n_active, x_sorted, w_gate, w_up, w_down, bm):
    ns, d = x_sorted.shape
    f = w_gate.shape[2]
    grid_spec = pltpu.PrefetchScalarGridSpec(
        num_scalar_prefetch=2,
        grid=(ns // bm,),
        in_specs=[pl.BlockSpec((bm, d), lambda i, be, na: (i, 0)),
                  pl.BlockSpec((1, d, f), lambda i, be, na: (be[i], 0, 0)),
                  pl.BlockSpec((1, d, f), lambda i, be, na: (be[i], 0, 0)),
                  pl.BlockSpec((1, f, d), lambda i, be, na: (be[i], 0, 0))],
        out_specs=pl.BlockSpec((bm, d), lambda i, be, na: (i, 0)),
        scratch_shapes=[pltpu.VMEM((d, f), BF16), pltpu.VMEM((d, f), BF16), pltpu.VMEM((f, d), BF16)],
    )
    return pl.pallas_call(
        _moe_kernel,
        name="moe_experts",
        grid_spec=grid_spec,
        out_shape=jax.ShapeDtypeStruct((ns, d), F32),
        compiler_params=_cparams(("arbitrary",)),
    )(block_exp, n_active, x_sorted, w_gate, w_up, w_down)


def _combine_kernel(x1_ref, y0_ref, y1_ref, gt_ref, g2_ref, o_ref):
    gt = gt_ref[...]
    y = y0_ref[...] * gt[:, 0:1] + y1_ref[...] * gt[:, 1:2]
    o_ref[...] = x1_ref[...] + g2_ref[0] * y


def _combine(x1, y0, y1, gates, gate2, tm, rows_per_mod_block):
    n, d = x1.shape
    r = gate2.shape[1]
    tiles_per_mod = rows_per_mod_block // tm
    row = lambda w: pl.BlockSpec((tm, w), lambda i: (i, 0))
    return pl.pallas_call(
        _combine_kernel,
        name="moe_combine",
        grid=(n // tm,),
        in_specs=[row(d), row(d), row(d), row(LANES),
                  pl.BlockSpec((1, r, d), lambda i: (i // tiles_per_mod, 0, 0))],
        out_specs=row(d),
        out_shape=jax.ShapeDtypeStruct((n, d), F32),
        compiler_params=_cparams(("parallel",)),
    )(x1, y0, y1, gates, gate2)


def _pick_tile(n, pref, mult=16):
    t = min(pref, n)
    while n % t or t % mult:
        t -= 1
    return t


def _pack_w_in(w_in):
    d = w_in.shape[0]
    bounds = np.cumsum(PROJ_SIZES)[:-1].tolist()
    qa, ka, va, qi, ki, wi, qd, kd, vd, zd, bd, ad = jnp.split(w_in, bounds, axis=1)
    used = IDX_DIM + N_IDX_HEADS + 2 * N_DELTA_HEADS
    misc = jnp.concatenate([ki, wi, bd, ad, jnp.zeros((d, LANES - used), w_in.dtype)], axis=1)
    cols = [qa, qi, zd, qd, kd, vd, ka, va, misc]
    width = sum(c.shape[1] for c in cols)
    cols.append(jnp.zeros((d, PROJ_PACKED - width), w_in.dtype))
    return jnp.concatenate(cols, axis=1).astype(BF16)


def _route_and_sort(eid, bm):
    n = eid.shape[0]
    nk = 2 * n
    flat_e = eid.reshape(-1)
    order = jnp.argsort(flat_e, stable=True).astype(I32)
    sorted_e = flat_e[order]
    counts = jnp.zeros((N_EXPERTS,), I32).at[flat_e].add(1)
    padded = (counts + bm - 1) // bm * bm
    pad_end = jnp.cumsum(padded)
    pad_start = pad_end - padded
    start = jnp.cumsum(counts) - counts
    dest_sorted = (pad_start[sorted_e] + jnp.arange(nk, dtype=I32) - start[sorted_e]).astype(I32)
    n_blocks = -(-nk // bm) + N_EXPERTS
    slot_tok = jnp.zeros((n_blocks * bm,), I32).at[dest_sorted].set(order // 2)
    dest = jnp.zeros((nk,), I32).at[order].set(dest_sorted)
    block_exp = jnp.minimum(jnp.searchsorted(pad_end, jnp.arange(n_blocks, dtype=I32) * bm, side='right'),
                            N_EXPERTS - 1).astype(I32)
    n_active = (pad_end[-1] // bm).astype(I32).reshape(1)
    return slot_tok, dest.reshape(n, 2), block_exp, n_active


def _layer(yp, ys, cache_k, cache_v, cache_idx, state_ssm, state_conv, page_table, c_prompt, c_sample,
           w_in, w_out, conv_w, a_log, dt_bias, q_norm_w, k_norm_w, idx_k_norm_w, o_norm_w, norm1_w, norm2_w,
           w_ada, b_ada, w_group, b_group, w_router, b_router, w_gate, w_up, w_down):
    bp, tp, d = yp.shape
    bs, ts, _ = ys.shape
    past = page_table.shape[1] * PAGE_SIZE
    rows = SAMPLE_ROWS
    assert CONV_WIDTH - 1 <= ts <= rows and tp % KEY_CHUNK == 0 and tp % DELTA_CHUNK == 0

    n_c = bp + bs
    n_c_pad = -(-n_c // SUBLANES) * SUBLANES
    c_all = jnp.concatenate([c_prompt, c_sample, jnp.zeros((n_c_pad - n_c, d), F32)], axis=0)
    mod = _ada_modulation(c_all, w_ada, b_ada)
    mods = jnp.split(mod, N_MOD, axis=1)
    mp = [m[:bp].reshape(bp, 1, d) for m in mods]
    ms = [jnp.repeat(m[bp:bp + bs], rows, axis=0).reshape(1, bs * rows, d) for m in mods]

    w_packed = _pack_w_in(w_in)
    w_out_bf = w_out.astype(BF16)
    w_rt = jnp.concatenate([w_group, w_router, jnp.zeros((d, LANES - N_GROUPS - N_EXPERTS), F32)], axis=1)
    wr_hi = w_rt.astype(BF16)
    wr_lo = (w_rt - wr_hi.astype(F32)).astype(BF16)
    b_rt = jnp.concatenate([b_group, b_router, jnp.zeros((LANES - N_GROUPS - N_EXPERTS,), F32)]).reshape(1, LANES)

    np_ = bp * tp
    xp2 = yp.reshape(np_, d)
    tm_p = _pick_tile(tp, 512)
    proj_p = _in_projection(xp2, mp[0], mp[1], norm1_w, w_packed, tm_p, tp)
    tq_p = _pick_tile(tp, 256)
    q_p, kf_p, kb_p, vt_p, qi_p, kif_p, kx_p = _attention_prep(
        proj_p, jnp.arange(tp), tq_p, q_norm_w, k_norm_w, idx_k_norm_w, True)
    oa_p = _dsa_prompt(q_p, qi_p, proj_p, kb_p, vt_p, kx_p, bp, tp)
    tt_p = _pick_tile(tp, 256)
    qn_p, kn_p, vv_p, bg_p = _delta_prep(proj_p, jnp.zeros((bp, SUBLANES, CONV_CHANNELS), F32), conv_w,
                                         a_log, dt_bias, bp, tp, tt_p, tp)
    od_p, ssm_p = _delta_chunks(qn_p, kn_p, vv_p, bg_p, proj_p, C_ZD // DELTA_WIDTH,
                                jnp.zeros((bp, N_DELTA_HEADS, HEAD_DIM, HEAD_DIM), F32), o_norm_w, bp, tp)
    tm_o = _pick_tile(tp, 256)
    x1_p, h2_p, lg_p = _out_projection(oa_p, od_p, xp2, mp[2], mp[3], mp[4], norm2_w, w_out_bf, wr_hi, wr_lo, b_rt,
                                       tm_o, tp)

    ns_ = bs * rows
    xs2 = jnp.pad(ys, ((0, 0), (0, rows - ts), (0, 0))).reshape(ns_, d)
    proj_s = _in_projection(xs2, ms[0], ms[1], norm1_w, w_packed, ns_, ns_)
    q_s, kf_s, kb_s, vb_s, qi_s, kif_s, _ = _attention_prep(
        proj_s, past + jnp.arange(rows), rows, q_norm_w, k_norm_w, idx_k_norm_w, False)
    q_t = qi_s.reshape(bs, rows, N_IDX_HEADS, IDX_DIM).transpose(0, 2, 1, 3).reshape(bs, N_IDX_HEADS * rows, IDX_DIM)
    w_col = proj_s[:, C_MISC + M_WI:C_MISC + M_WI + N_IDX_HEADS].reshape(bs, rows, N_IDX_HEADS)
    w_col = w_col.transpose(0, 2, 1).reshape(bs, N_IDX_HEADS * rows, 1)
    pages = _pick_tile(page_table.shape[1], 8, 1)
    keys_past, keys_new = _sample_scores(page_table, q_t, w_col, kif_s, cache_idx, pages, ts)
    n_sel_s = min(TOPK_MAX, (past + ts) // 4)
    oa_s = _sample_attend(page_table, keys_past, keys_new, q_s, kb_s, vb_s,
                          cache_k.reshape(-1, PAGE_SIZE, KV_WIDTH), cache_v.reshape(-1, PAGE_SIZE, KV_WIDTH),
                          pages, n_sel_s)
    prev8 = jnp.pad(state_conv, ((0, 0), (SUBLANES - (CONV_WIDTH - 1), 0), (0, 0)))
    qn_s, kn_s, vv_s, bg_s = _delta_prep(proj_s, prev8, conv_w, a_log, dt_bias, bs, rows, rows, ts)
    to_chunk = lambda a: jnp.pad(a.reshape(bs, rows, -1), ((0, 0), (0, DELTA_CHUNK - rows), (0, 0))).reshape(
        bs * DELTA_CHUNK, -1)
    z_s = proj_s[:, C_ZD:C_ZD + DELTA_WIDTH]
    od_s, ssm_s = _delta_chunks(to_chunk(qn_s), to_chunk(kn_s), to_chunk(vv_s), to_chunk(bg_s), to_chunk(z_s), 0,
                                state_ssm, o_norm_w, bs, DELTA_CHUNK)
    od_s = od_s.reshape(bs, DELTA_CHUNK, DELTA_WIDTH)[:, :rows].reshape(ns_, DELTA_WIDTH)
    x1_s, h2_s, lg_s = _out_projection(oa_s, od_s, xs2, ms[2], ms[3], ms[4], norm2_w, w_out_bf, wr_hi, wr_lo, b_rt,
                                       ns_, ns_)

    n_all = np_ + ns_
    h2_all = jnp.concatenate([h2_p, h2_s], axis=0)
    lg_all = jnp.concatenate([lg_p, lg_s], axis=0)
    eid, gates = _route(lg_all, _pick_tile(n_all, 512, SUBLANES))
    bm = 256
    slot_tok, dest, block_exp, n_active = _route_and_sort(eid[:, 0:2], bm)
    x_sorted = h2_all[slot_tok]
    yb = _moe_experts(block_exp, n_active, x_sorted, w_gate, w_up, w_down, bm)
    y0 = yb[dest[:, 0]]
    y1 = yb[dest[:, 1]]
    out_p = _combine(x1_p, y0[:np_], y1[:np_], gates[:np_], mp[5], tm_o, tp)
    out_s = _combine(x1_s, y0[np_:], y1[np_:], gates[np_:], ms[5], ns_, ns_)

    valid = lambda a: a.reshape(bs, rows, -1)[:, :ts]
    conv_p = proj_p[:, C_CONV:C_CONV + CONV_CHANNELS].reshape(bp, tp, CONV_CHANNELS)[:, tp - (CONV_WIDTH - 1):]
    conv_s = valid(proj_s[:, C_CONV:C_CONV + CONV_CHANNELS])[:, ts - (CONV_WIDTH - 1):]
    return (out_p.reshape(bp, tp, d), valid(out_s),
            kf_p.reshape(bp, tp, N_KV_HEADS, HEAD_DIM),
            proj_p[:, C_VA:C_VA + KV_WIDTH].reshape(bp, tp, N_KV_HEADS, HEAD_DIM),
            kif_p.reshape(bp, tp, IDX_DIM), ssm_p, conv_p,
            valid(kf_s).reshape(bs, ts, N_KV_HEADS, HEAD_DIM),
            valid(proj_s[:, C_VA:C_VA + KV_WIDTH]).reshape(bs, ts, N_KV_HEADS, HEAD_DIM),
            valid(kif_s), ssm_s, conv_s)


def kernel(x_prompt, x_sample, cache_k, cache_v, cache_idx_k, state_ssm, state_conv, page_table, c_prompt, c_sample,
           w_in, w_out, conv_w, a_log, dt_bias, q_norm_w, k_norm_w, idx_k_norm_w, o_norm_w, norm1_w, norm2_w,
           w_ada, b_ada, w_group, b_group, w_router, b_router, w_gate, w_up, w_down):
    depth = w_in.shape[0]
    yp, ys = x_prompt, x_sample
    per_layer = []
    for l in range(depth):
        res = _layer(yp, ys, cache_k[l], cache_v[l], cache_idx_k[l], state_ssm[l], state_conv[l], page_table,
                     c_prompt, c_sample, w_in[l], w_out[l], conv_w[l], a_log[l], dt_bias[l], q_norm_w[l],
                     k_norm_w[l], idx_k_norm_w[l], o_norm_w[l], norm1_w[l], norm2_w[l], w_ada[l], b_ada[l],
                     w_group[l], b_group[l], w_router[l], b_router[l], w_gate[l], w_up[l], w_down[l])
        yp, ys = res[0], res[1]
        per_layer.append(res[2:])
    stacked = tuple(jnp.stack([pl_[j] for pl_ in per_layer]) for j in range(10))
    return (yp, ys) + stacked
```

```python
import functools

import jax
import jax.numpy as jnp
import numpy as np
from jax import lax
from jax.experimental import pallas as pl
from jax.experimental.pallas import tpu as pltpu

F32 = jnp.float32
BF16 = jnp.bfloat16
I32 = jnp.int32

HEAD_DIM = 128
N_ATTN_HEADS = 8
N_KV_HEADS = 2
KV_GROUP = N_ATTN_HEADS // N_KV_HEADS
N_DELTA_HEADS = 8
N_IDX_HEADS = 16
IDX_DIM = 64
ATTN_WIDTH = N_ATTN_HEADS * HEAD_DIM
KV_WIDTH = N_KV_HEADS * HEAD_DIM
DELTA_WIDTH = N_DELTA_HEADS * HEAD_DIM
IDX_WIDTH = N_IDX_HEADS * IDX_DIM
CONV_CHANNELS = 3 * DELTA_WIDTH
TOPK_MAX = 256
ROPE_THETA = 500000.0
ROPE_FRACTION = 4
CONV_WIDTH = 4
DELTA_CHUNK = 64
N_GROUPS = 8
EXPERTS_PER_GROUP = 8
N_EXPERTS = N_GROUPS * EXPERTS_PER_GROUP
N_MOD = 6
EPS = 1e-6
PAGE_SIZE = 128
PROJ_SIZES = (ATTN_WIDTH, KV_WIDTH, KV_WIDTH, IDX_WIDTH, IDX_DIM, N_IDX_HEADS,
              DELTA_WIDTH, DELTA_WIDTH, DELTA_WIDTH, DELTA_WIDTH, N_DELTA_HEADS, N_DELTA_HEADS)

LANES = 128
SUBLANES = 8
VMEM_LIMIT = 56 * 1024 * 1024

C_QA = 0
C_QI = 1024
C_ZD = 2048
C_CONV = 3072
C_KA = 6144
C_VA = 6400
C_MISC = 6656
PROJ_PACKED = 6912
M_KI = 0
M_WI = 64
M_BD = 80
M_AD = 88

Q_TILE = 128
KEY_CHUNK = 256
SAMPLE_ROWS = 16
NEG_BIG = -1e30
INT_MIN = -2147483648
INT_MAX = 2147483647


def _cparams(sem):
    return pltpu.CompilerParams(dimension_semantics=sem, vmem_limit_bytes=VMEM_LIMIT)


def _dot(a, b):
    return jnp.dot(a, b, preferred_element_type=F32)


def _dot_nt(a, b):
    return lax.dot_general(a, b, (((1,), (1,)), ((), ())), preferred_element_type=F32)


def _dot_tn(a, b):
    return lax.dot_general(a, b, (((0,), (0,)), ((), ())), preferred_element_type=F32)


def _sigmoid(x):
    return 1.0 / (1.0 + jnp.exp(-x))


def _silu(x):
    return x * _sigmoid(x)


def _softplus(x):
    return jnp.maximum(x, 0.0) + jnp.log(1.0 + jnp.exp(-jnp.abs(x)))


def _ada_kernel(c_ref, w_ref, b_ref, o_ref):
    s = _silu(c_ref[...]).astype(BF16)
    o_ref[...] = _dot(s, w_ref[...].astype(BF16)) + b_ref[...]


def _ada_modulation(c, w_ada, b_ada):
    r, d = c.shape
    n = w_ada.shape[1]
    tn = 1024 if n % 1024 == 0 else n
    return pl.pallas_call(
        _ada_kernel,
        name="ada_mod",
        grid=(n // tn,),
        in_specs=[pl.BlockSpec((r, d), lambda j: (0, 0)),
                  pl.BlockSpec((d, tn), lambda j: (0, j)),
                  pl.BlockSpec((1, tn), lambda j: (0, j))],
        out_specs=pl.BlockSpec((r, tn), lambda j: (0, j)),
        out_shape=jax.ShapeDtypeStruct((r, n), F32),
        compiler_params=_cparams(("parallel",)),
    )(c, w_ada, b_ada.reshape(1, n))


def _inproj_kernel(x_ref, sh_ref, sc_ref, nw_ref, w_ref, o_ref, h_scr):
    @pl.when(pl.program_id(1) == 0)
    def _():
        x = x_ref[...]
        y = x * lax.rsqrt(jnp.mean(x * x, axis=-1, keepdims=True) + EPS) * nw_ref[...]
        h_scr[...] = (y * (1.0 + sc_ref[0]) + sh_ref[0]).astype(BF16)

    o_ref[...] = _dot(h_scr[...], w_ref[...])


def _in_projection(x2d, shift, scale, norm_w, w_packed, tm, rows_per_mod_block):
    n, d = x2d.shape
    np_ = w_packed.shape[1]
    tn = 1152
    r = shift.shape[1]
    tiles_per_mod = rows_per_mod_block // tm
    mod_spec = pl.BlockSpec((1, r, d), lambda i, j: (i // tiles_per_mod, 0, 0))
    return pl.pallas_call(
        _inproj_kernel,
        name="in_proj",
        grid=(n // tm, np_ // tn),
        in_specs=[pl.BlockSpec((tm, d), lambda i, j: (i, 0)),
                  mod_spec, mod_spec,
                  pl.BlockSpec((1, d), lambda i, j: (0, 0)),
                  pl.BlockSpec((d, tn), lambda i, j: (0, j))],
        out_specs=pl.BlockSpec((tm, tn), lambda i, j: (i, j)),
        out_shape=jax.ShapeDtypeStruct((n, np_), F32),
        scratch_shapes=[pltpu.VMEM((tm, d), BF16)],
        compiler_params=_cparams(("parallel", "arbitrary")),
    )(x2d, shift, scale, norm_w.reshape(1, d), w_packed)


def _rope(x, tab, rot):
    c = tab[:, 0:LANES]
    s1 = tab[:, LANES:2 * LANES]
    s2 = tab[:, 2 * LANES:3 * LANES]
    return x * c + pltpu.roll(x, LANES - rot, 1) * s1 + pltpu.roll(x, rot, 1) * s2


def _rms_head(x, w):
    return x * lax.rsqrt(jnp.mean(x * x, axis=-1, keepdims=True) + EPS) * w


def _prep_kernel(qa_ref, qi_ref, ka_ref, va_ref, misc_ref, tabm_ref, tabi_ref, qw_ref, kw_ref, iw_ref,
                 q_ref, kf_ref, kb_ref, vb_ref, qib_ref, kif_ref, kib_ref, *, transpose_v):
    tabm = tabm_ref[0]
    tabi = tabi_ref[0]
    half_main = HEAD_DIM // ROPE_FRACTION // 2
    half_idx = IDX_DIM // ROPE_FRACTION // 2
    for h in range(N_ATTN_HEADS):
        sl = slice(h * HEAD_DIM, (h + 1) * HEAD_DIM)
        y = _rope(_rms_head(qa_ref[:, sl], qw_ref[...]), tabm, half_main)
        q_ref[:, sl] = (y * (HEAD_DIM ** -0.5)).astype(BF16)
    for h in range(N_KV_HEADS):
        sl = slice(h * HEAD_DIM, (h + 1) * HEAD_DIM)
        y = _rope(_rms_head(ka_ref[:, sl], kw_ref[...]), tabm, half_main)
        kf_ref[:, sl] = y
        kb_ref[:, sl] = y.astype(BF16)
    if transpose_v:
        vb_ref[...] = va_ref[...].T.astype(BF16)
    else:
        vb_ref[...] = va_ref[...].astype(BF16)
    for p in range(IDX_WIDTH // LANES):
        sl = slice(p * LANES, (p + 1) * LANES)
        qib_ref[:, sl] = _rope(qi_ref[:, sl], tabi, half_idx).astype(BF16)
    m = misc_ref[...]
    lane = lax.broadcasted_iota(I32, m.shape, 1)
    ki = jnp.where(lane < IDX_DIM, m, 0.0)
    ms = jnp.sum(ki * ki, axis=-1, keepdims=True) * (1.0 / IDX_DIM)
    y = _rope(ki * lax.rsqrt(ms + EPS) * iw_ref[...], tabi, half_idx)
    kif_ref[...] = y[:, 0:IDX_DIM]
    kib_ref[...] = (y + pltpu.roll(y, IDX_DIM, 1)).astype(BF16)


def _rope_tables(pos, head_dim, group):
    d_rot = head_dim // ROPE_FRACTION
    half = d_rot // 2
    inv_freq = jnp.power(ROPE_THETA, -(jnp.arange(half, dtype=F32) * 2.0 / d_rot))
    ang = pos.astype(F32)[:, None] * inv_freq[None, :]
    cos = jnp.cos(ang)
    sin = jnp.sin(ang)
    t = pos.shape[0]
    z = jnp.zeros((t, group - d_rot), F32)
    c = jnp.concatenate([cos, cos, jnp.ones((t, group - d_rot), F32)], axis=1)
    s1 = jnp.concatenate([-sin, jnp.zeros((t, half), F32), z], axis=1)
    s2 = jnp.concatenate([jnp.zeros((t, half), F32), sin, z], axis=1)
    rep = LANES // group
    return jnp.concatenate([jnp.tile(c, (1, rep)), jnp.tile(s1, (1, rep)), jnp.tile(s2, (1, rep))], axis=1)


def _attention_prep(proj, pos, tq, q_norm_w, k_norm_w, idx_k_norm_w, transpose_v):
    n = proj.shape[0]
    p = pos.shape[0]
    g = p // tq
    tabm = _rope_tables(pos, HEAD_DIM, LANES).reshape(g, tq, 3 * LANES)
    tabi = _rope_tables(pos, IDX_DIM, IDX_DIM).reshape(g, tq, 3 * LANES)
    iw = jnp.concatenate([idx_k_norm_w, jnp.zeros((LANES - IDX_DIM,), F32)]).reshape(1, LANES)
    row = lambda w, c: pl.BlockSpec((tq, w), lambda i: (i, c // w))
    tab_spec = pl.BlockSpec((1, tq, 3 * LANES), lambda i: (i % g, 0, 0))
    vec_spec = pl.BlockSpec((1, LANES), lambda i: (0, 0))
    out_row = lambda w: pl.BlockSpec((tq, w), lambda i: (i, 0))
    v_spec = pl.BlockSpec((KV_WIDTH, tq), lambda i: (0, i)) if transpose_v else out_row(KV_WIDTH)
    v_shape = (KV_WIDTH, n) if transpose_v else (n, KV_WIDTH)
    return pl.pallas_call(
        functools.partial(_prep_kernel, transpose_v=transpose_v),
        name="attn_prep",
        grid=(n // tq,),
        in_specs=[row(ATTN_WIDTH, C_QA), row(IDX_WIDTH, C_QI), row(KV_WIDTH, C_KA), row(KV_WIDTH, C_VA),
                  row(LANES, C_MISC), tab_spec, tab_spec, vec_spec, vec_spec, vec_spec],
        out_specs=[out_row(ATTN_WIDTH), out_row(KV_WIDTH), out_row(KV_WIDTH), v_spec,
                   out_row(IDX_WIDTH), out_row(IDX_DIM), out_row(LANES)],
        out_shape=[jax.ShapeDtypeStruct((n, ATTN_WIDTH), BF16),
                   jax.ShapeDtypeStruct((n, KV_WIDTH), F32),
                   jax.ShapeDtypeStruct((n, KV_WIDTH), BF16),
                   jax.ShapeDtypeStruct(v_shape, BF16),
                   jax.ShapeDtypeStruct((n, IDX_WIDTH), BF16),
                   jax.ShapeDtypeStruct((n, IDX_DIM), F32),
                   jax.ShapeDtypeStruct((n, LANES), BF16)],
        compiler_params=_cparams(("parallel",)),
    )(proj, proj, proj, proj, proj, tabm, tabi,
      q_norm_w.reshape(1, LANES), k_norm_w.reshape(1, LANES), iw)


def _sort_key(x):
    b = pltpu.bitcast(x + 0.0, I32)
    return b ^ ((b >> 31) & INT_MAX)


def _kth_largest_key(count_ge, k, shape):
    def body(it, ans_u):
        bit = jnp.left_shift(jnp.int32(1), 31 - it)
        cand_u = ans_u | bit
        cnt = count_ge(cand_u ^ INT_MIN)
        return jnp.where(cnt >= k, cand_u, ans_u)

    ans_u = lax.fori_loop(0, 32, body, jnp.zeros(shape, I32))
    return ans_u ^ INT_MIN


def _tie_index_limit(count_eq_le, need, n_keys, shape):
    nbits = max(1, int(n_keys - 1).bit_length())

    def body(it, lo):
        bit = jnp.left_shift(jnp.int32(1), nbits - 1 - it)
        cand = lo | bit
        cnt = count_eq_le(cand - 1)
        return jnp.where(cnt >= need, lo, cand)

    return lax.fori_loop(0, nbits, body, jnp.zeros(shape, I32))


def _dsa_prompt_kernel(q_ref, qi_ref, misc_ref, k_ref, vt_ref, kx_ref, o_ref,
                       key_scr, qsel_scr, thr_scr, lim_scr, m_scr, l_scr, acc_scr, *, n_sel):
    i = pl.program_id(1)
    tq = Q_TILE
    ck = KEY_CHUNK
    n_ch = (i * tq + tq + ck - 1) // ck
    q_pos = i * tq + lax.broadcasted_iota(I32, (1, tq), 1)
    row_k = lax.broadcasted_iota(I32, (ck, 1), 0)

    lo_half = lax.broadcasted_iota(I32, (tq, LANES), 1) < IDX_DIM
    zero = jnp.zeros((), BF16)
    for p in range(IDX_WIDTH // LANES):
        slab = qi_ref[:, p * LANES:(p + 1) * LANES]
        qsel_scr[(2 * p) * tq:(2 * p + 1) * tq, :] = jnp.where(lo_half, slab, zero)
        qsel_scr[(2 * p + 1) * tq:(2 * p + 2) * tq, :] = jnp.where(lo_half, zero, slab)
    w_t = misc_ref[...].T

    def score_chunk(c, carry):
        off = pl.multiple_of(c * ck, ck)
        s = _dot_nt(kx_ref[pl.ds(off, ck), :], qsel_scr[...])
        acc = jnp.zeros((ck, tq), F32)
        for h in range(N_IDX_HEADS):
            acc = acc + w_t[M_WI + h:M_WI + h + 1, :] * jnp.maximum(s[:, h * tq:(h + 1) * tq], 0.0)
        acc = jnp.where(off + row_k <= q_pos, acc, -jnp.inf)
        key_scr[pl.ds(off, ck), :] = _sort_key(acc)
        return carry

    lax.fori_loop(0, n_ch, score_chunk, 0)

    def count_where(pred):
        def body(c, cnt):
            off = pl.multiple_of(c * ck, ck)
            hit = pred(key_scr[pl.ds(off, ck), :], off + row_k).astype(F32)
            return cnt + jnp.sum(hit.reshape(ck // SUBLANES, SUBLANES, tq), axis=0)
        cnt = lax.fori_loop(0, n_ch, body, jnp.zeros((SUBLANES, tq), F32))
        return jnp.sum(cnt, axis=0, keepdims=True)

    thr_scr[...] = jnp.full((1, tq), INT_MIN, I32)
    lim_scr[...] = jnp.full((1, tq), INT_MAX, I32)

    @pl.when((i + 1) * tq > n_sel)
    def _():
        t = _kth_largest_key(lambda cand: count_where(lambda kk, pos: kk >= cand), float(n_sel), (1, tq))
        thr_scr[...] = t
        n_gt = count_where(lambda kk, pos: kk > t)
        n_ge = count_where(lambda kk, pos: kk >= t)

        @pl.when(jnp.max(n_ge) > float(n_sel))
        def _():
            lim_scr[...] = _tie_index_limit(
                lambda idx: count_where(lambda kk, pos: jnp.logical_and(kk == t, pos <= idx)),
                float(n_sel) - n_gt, k_ref.shape[0], (1, tq))

    thr = thr_scr[...]
    lim = lim_scr[...]

    for g in range(N_KV_HEADS):
        qg = jnp.concatenate(
            [q_ref[:, (g * KV_GROUP + r) * HEAD_DIM:(g * KV_GROUP + r + 1) * HEAD_DIM] for r in range(KV_GROUP)],
            axis=0)
        m_scr[...] = jnp.full(m_scr.shape, NEG_BIG, F32)
        l_scr[...] = jnp.zeros(l_scr.shape, F32)
        acc_scr[...] = jnp.zeros(acc_scr.shape, F32)

        def attend_chunk(c, carry):
            off = pl.multiple_of(c * ck, ck)
            kc = k_ref[pl.ds(off, ck), g * HEAD_DIM:(g + 1) * HEAD_DIM]
            vt = vt_ref[g * HEAD_DIM:(g + 1) * HEAD_DIM, pl.ds(off, ck)]
            kk = key_scr[pl.ds(off, ck), :]
            pos = off + row_k
            sel = jnp.logical_or(kk > thr, jnp.logical_and(kk == thr, pos <= lim))
            sel = jnp.logical_and(sel, pos <= q_pos)
            sel = jnp.concatenate([sel] * KV_GROUP, axis=1)
            s = jnp.where(sel, _dot_nt(kc, qg), NEG_BIG)
            m_old = m_scr[...]
            m_new = jnp.maximum(m_old, jnp.max(s, axis=0, keepdims=True))
            p = jnp.where(sel, jnp.exp(s - m_new), 0.0)
            alpha = jnp.exp(m_old - m_new)
            l_scr[...] = alpha * l_scr[...] + jnp.sum(p, axis=0, keepdims=True)
            acc_scr[...] = alpha * acc_scr[...] + _dot(vt, p.astype(BF16))
            m_scr[...] = m_new
            return carry

        lax.fori_loop(0, n_ch, attend_chunk, 0)
        o_t = acc_scr[...] / l_scr[...]
        for r in range(KV_GROUP):
            h = g * KV_GROUP + r
            o_ref[:, h * HEAD_DIM:(h + 1) * HEAD_DIM] = o_t[:, r * tq:(r + 1) * tq].T.astype(BF16)


def _dsa_prompt(q_bf, qi_bf, proj, k_bf, vt_bf, kx_bf, b, t):
    n = b * t
    nq = t // Q_TILE
    n_sel = min(TOPK_MAX, t // 4)
    qrow = lambda w: pl.BlockSpec((Q_TILE, w), lambda bb, i: (bb * nq + i, 0))
    seq = lambda w: pl.BlockSpec((t, w), lambda bb, i: (bb, 0))
    return pl.pallas_call(
        functools.partial(_dsa_prompt_kernel, n_sel=n_sel),
        name="dsa_prompt",
        grid=(b, nq),
        in_specs=[qrow(ATTN_WIDTH), qrow(IDX_WIDTH),
                  pl.BlockSpec((Q_TILE, LANES), lambda bb, i: (bb * nq + i, C_MISC // LANES)),
                  seq(KV_WIDTH), pl.BlockSpec((KV_WIDTH, t), lambda bb, i: (0, bb)), seq(LANES)],
        out_specs=qrow(ATTN_WIDTH),
        out_shape=jax.ShapeDtypeStruct((n, ATTN_WIDTH), BF16),
        scratch_shapes=[pltpu.VMEM((t, Q_TILE), I32),
                        pltpu.VMEM((N_IDX_HEADS * Q_TILE, LANES), BF16),
                        pltpu.VMEM((1, Q_TILE), I32),
                        pltpu.VMEM((1, Q_TILE), I32),
                        pltpu.VMEM((1, KV_GROUP * Q_TILE), F32),
                        pltpu.VMEM((1, KV_GROUP * Q_TILE), F32),
                        pltpu.VMEM((HEAD_DIM, KV_GROUP * Q_TILE), F32)],
        compiler_params=_cparams(("parallel", "arbitrary")),
    )(q_bf, qi_bf, proj, k_bf, vt_bf, kx_bf)


def _sample_score_kernel(pt_ref, q_ref, w_ref, kn_ref, *refs, pages, t_valid):
    page_refs = refs[:pages]
    past_ref, new_ref = refs[pages], refs[pages + 1]
    rows = SAMPLE_ROWS
    q = q_ref[0]
    w = w_ref[0]

    def head_sum(s):
        s = w * jnp.maximum(s, 0.0)
        acc = s[0:rows]
        for h in range(1, N_IDX_HEADS):
            acc = acc + s[h * rows:(h + 1) * rows]
        return acc

    for j in range(pages):
        kp = page_refs[j][0].astype(BF16)
        past_ref[0, :, j * PAGE_SIZE:(j + 1) * PAGE_SIZE] = _sort_key(head_sum(_dot_nt(q, kp)))

    @pl.when(pl.program_id(1) == 0)
    def _():
        kn = jnp.concatenate([kn_ref[...], jnp.zeros((LANES - rows, IDX_DIM), F32)], axis=0).astype(BF16)
        sc = head_sum(_dot_nt(q, kn))
        t = lax.broadcasted_iota(I32, sc.shape, 0)
        s = lax.broadcasted_iota(I32, sc.shape, 1)
        ok = jnp.logical_and(s <= t, s < t_valid)
        new_ref[0] = _sort_key(jnp.where(ok, sc, -jnp.inf))


def _sample_scores(page_table, q_t, w_col, kif, cache_idx, pages, t_valid):
    bs, n_pages = page_table.shape
    past = n_pages * PAGE_SIZE
    hr = N_IDX_HEADS * SAMPLE_ROWS
    page_spec = lambda j: pl.BlockSpec((1, PAGE_SIZE, IDX_DIM), lambda b, c, pt: (pt[b, c * pages + j], 0, 0))
    grid_spec = pltpu.PrefetchScalarGridSpec(
        num_scalar_prefetch=1,
        grid=(bs, n_pages // pages),
        in_specs=[pl.BlockSpec((1, hr, IDX_DIM), lambda b, c, pt: (b, 0, 0)),
                  pl.BlockSpec((1, hr, 1), lambda b, c, pt: (b, 0, 0)),
                  pl.BlockSpec((SAMPLE_ROWS, IDX_DIM), lambda b, c, pt: (b, 0))]
                 + [page_spec(j) for j in range(pages)],
        out_specs=[pl.BlockSpec((1, SAMPLE_ROWS, pages * PAGE_SIZE), lambda b, c, pt: (b, 0, c)),
                   pl.BlockSpec((1, SAMPLE_ROWS, LANES), lambda b, c, pt: (b, 0, 0))],
    )
    return pl.pallas_call(
        functools.partial(_sample_score_kernel, pages=pages, t_valid=t_valid),
        name="sample_scores",
        grid_spec=grid_spec,
        out_shape=[jax.ShapeDtypeStruct((bs, SAMPLE_ROWS, past), I32),
                   jax.ShapeDtypeStruct((bs, SAMPLE_ROWS, LANES), I32)],
        compiler_params=_cparams(("parallel", "arbitrary")),
    )(page_table, q_t, w_col, kif, *([cache_idx] * pages))


def _sample_attend_kernel(pt_ref, kp_ref, kn_ref, q_ref, knew_ref, vnew_ref, *refs, pages, n_sel, past):
    k_pages = refs[:pages]
    v_pages = refs[pages:2 * pages]
    o_ref = refs[2 * pages]
    thr_scr, lim_scr, m_scr, l_scr, acc_scr = refs[2 * pages + 1:]
    c = pl.program_id(1)
    rows = SAMPLE_ROWS
    span = pages * PAGE_SIZE

    @pl.when(c == 0)
    def _():
        m_scr[...] = jnp.full(m_scr.shape, NEG_BIG, F32)
        l_scr[...] = jnp.zeros(l_scr.shape, F32)
        acc_scr[...] = jnp.zeros(acc_scr.shape, F32)
        kp = kp_ref[0]
        kn = kn_ref[0]
        pos_p = lax.broadcasted_iota(I32, kp.shape, 1)
        pos_n = past + lax.broadcasted_iota(I32, kn.shape, 1)

        def count_where(pred):
            return (jnp.sum(pred(kp, pos_p).astype(F32), axis=1, keepdims=True)
                    + jnp.sum(pred(kn, pos_n).astype(F32), axis=1, keepdims=True))

        t = _kth_largest_key(lambda cand: count_where(lambda kk, pos: kk >= cand), float(n_sel), (rows, 1))
        thr_scr[...] = t
        lim_scr[...] = jnp.full((rows, 1), INT_MAX, I32)
        n_gt = count_where(lambda kk, pos: kk > t)
        n_ge = count_where(lambda kk, pos: kk >= t)

        @pl.when(jnp.max(n_ge) > float(n_sel))
        def _():
            lim_scr[...] = _tie_index_limit(
                lambda idx: count_where(lambda kk, pos: jnp.logical_and(kk == t, pos <= idx)),
                float(n_sel) - n_gt, past + LANES, (rows, 1))

    thr = thr_scr[...]
    lim = lim_scr[...]

    def update(g, qg, kc, vc, sel):
        sel = jnp.concatenate([sel] * KV_GROUP, axis=0)
        s = jnp.where(sel, _dot_nt(qg, kc), NEG_BIG)
        m_old = m_scr[g]
        m_new = jnp.maximum(m_old, jnp.max(s, axis=1, keepdims=True))
        p = jnp.where(sel, jnp.exp(s - m_new), 0.0)
        alpha = jnp.exp(m_old - m_new)
        l_scr[g] = alpha * l_scr[g] + jnp.sum(p, axis=1, keepdims=True)
        acc_scr[g] = alpha * acc_scr[g] + _dot(p.astype(BF16), vc)
        m_scr[g] = m_new

    def select(kk, pos):
        return jnp.logical_or(kk > thr, jnp.logical_and(kk == thr, pos <= lim))

    k_cat = jnp.concatenate([r[0] for r in k_pages], axis=0).astype(BF16)
    v_cat = jnp.concatenate([r[0] for r in v_pages], axis=0).astype(BF16)
    off = pl.multiple_of(c * span, span)
    kk = kp_ref[0, :, pl.ds(off, span)]
    sel_past = select(kk, off + lax.broadcasted_iota(I32, kk.shape, 1))
    q_groups = []
    for g in range(N_KV_HEADS):
        qg = jnp.concatenate(
            [q_ref[:, (g * KV_GROUP + r) * HEAD_DIM:(g * KV_GROUP + r + 1) * HEAD_DIM] for r in range(KV_GROUP)],
            axis=0)
        q_groups.append(qg)
        update(g, qg, k_cat[:, g * HEAD_DIM:(g + 1) * HEAD_DIM], v_cat[:, g * HEAD_DIM:(g + 1) * HEAD_DIM], sel_past)

    @pl.when(c == pl.num_programs(1) - 1)
    def _():
        kn = kn_ref[0]
        lane = lax.broadcasted_iota(I32, kn.shape, 1)
        sel_new = jnp.logical_and(select(kn, past + lane), lane < rows)
        pad = jnp.zeros((LANES - rows, KV_WIDTH), BF16)
        k_new = jnp.concatenate([knew_ref[...], pad], axis=0)
        v_new = jnp.concatenate([vnew_ref[...], pad], axis=0)
        for g in range(N_KV_HEADS):
            sl = slice(g * HEAD_DIM, (g + 1) * HEAD_DIM)
            update(g, q_groups[g], k_new[:, sl], v_new[:, sl], sel_new)
            o = acc_scr[g] / l_scr[g]
            for r in range(KV_GROUP):
                h = g * KV_GROUP + r
                o_ref[:, h * HEAD_DIM:(h + 1) * HEAD_DIM] = o[r * rows:(r + 1) * rows].astype(BF16)


def _sample_attend(page_table, keys_past, keys_new, q_bf, k_bf, v_bf, cache_k, cache_v, pages, n_sel):
    bs, n_pages = page_table.shape
    past = n_pages * PAGE_SIZE
    page_spec = lambda j: pl.BlockSpec((1, PAGE_SIZE, KV_WIDTH), lambda b, c, pt: (pt[b, c * pages + j], 0, 0))
    row = lambda w: pl.BlockSpec((SAMPLE_ROWS, w), lambda b, c, pt: (b, 0))
    grid_spec = pltpu.PrefetchScalarGridSpec(
        num_scalar_prefetch=1,
        grid=(bs, n_pages // pages),
        in_specs=[pl.BlockSpec((1, SAMPLE_ROWS, past), lambda b, c, pt: (b, 0, 0)),
                  pl.BlockSpec((1, SAMPLE_ROWS, LANES), lambda b, c, pt: (b, 0, 0)),
                  row(ATTN_WIDTH), row(KV_WIDTH), row(KV_WIDTH)]
                 + [page_spec(j) for j in range(pages)] * 2,
        out_specs=row(ATTN_WIDTH),
        scratch_shapes=[pltpu.VMEM((SAMPLE_ROWS, 1), I32),
                        pltpu.VMEM((SAMPLE_ROWS, 1), I32),
                        pltpu.VMEM((N_KV_HEADS, KV_GROUP * SAMPLE_ROWS, 1), F32),
                        pltpu.VMEM((N_KV_HEADS, KV_GROUP * SAMPLE_ROWS, 1), F32),
                        pltpu.VMEM((N_KV_HEADS, KV_GROUP * SAMPLE_ROWS, HEAD_DIM), F32)],
    )
    return pl.pallas_call(
        functools.partial(_sample_attend_kernel, pages=pages, n_sel=n_sel, past=past),
        name="sample_attend",
        grid_spec=grid_spec,
        out_shape=jax.ShapeDtypeStruct((bs * SAMPLE_ROWS, ATTN_WIDTH), BF16),
        compiler_params=_cparams(("parallel", "arbitrary")),
    )(page_table, keys_past, keys_new, q_bf, k_bf, v_bf, *([cache_k] * pages), *([cache_v] * pages))


def _delta_prep_kernel(x_ref, halo_ref, prev_ref, misc_ref, cw_ref, al_ref, dt_ref,
                       qn_ref, kn_ref, vv_ref, bg_ref, xp_scr, *, tiles_per_seq, t_valid, tt):
    i = pl.program_id(0)
    tile_in_seq = i % tiles_per_seq
    halo = jnp.where(tile_in_seq == 0, prev_ref[0], halo_ref[...])
    xp_scr[0:SUBLANES, :] = halo
    xp_scr[SUBLANES:SUBLANES + tt, :] = x_ref[...]
    base = SUBLANES - (CONV_WIDTH - 1)
    outs = (qn_ref, kn_ref, vv_ref)
    for sec in range(3):
        for h in range(N_DELTA_HEADS):
            col = sec * DELTA_WIDTH + h * HEAD_DIM
            sl = slice(col, col + HEAD_DIM)
            y = xp_scr[base:base + tt, sl] * cw_ref[0:1, sl]
            for j in range(1, CONV_WIDTH):
                y = y + xp_scr[base + j:base + j + tt, sl] * cw_ref[j:j + 1, sl]
            y = _silu(y)
            if sec < 2:
                y = y * lax.rsqrt(jnp.sum(y * y, axis=-1, keepdims=True) + EPS)
            if sec == 0:
                y = y * (HEAD_DIM ** -0.5)
            outs[sec][:, h * HEAD_DIM:(h + 1) * HEAD_DIM] = y
    m = misc_ref[...]
    lane = lax.broadcasted_iota(I32, m.shape, 1)
    row = tile_in_seq * tt + lax.broadcasted_iota(I32, m.shape, 0)
    beta = _sigmoid(m)
    g = -jnp.exp(al_ref[...]) * _softplus(m + dt_ref[...])
    is_b = jnp.logical_and(lane >= M_BD, lane < M_BD + N_DELTA_HEADS)
    is_g = jnp.logical_and(lane >= M_AD, lane < M_AD + N_DELTA_HEADS)
    comb = jnp.where(is_b, beta, jnp.where(is_g, g, 0.0))
    comb = jnp.where(row < t_valid, comb, 0.0)
    bg_ref[...] = pltpu.roll(comb, LANES - M_BD, 1)


def _delta_prep(proj, prev8, conv_w, a_log, dt_bias, b, t, tt, t_valid):
    n = proj.shape[0]
    tiles_per_seq = t // tt
    pad_vec = lambda v: jnp.zeros((1, LANES), F32).at[0, M_AD:M_AD + N_DELTA_HEADS].set(v)
    halo_blocks = tt // SUBLANES
    return pl.pallas_call(
        functools.partial(_delta_prep_kernel, tiles_per_seq=tiles_per_seq, t_valid=t_valid, tt=tt),
        name="delta_prep",
        grid=(n // tt,),
        in_specs=[pl.BlockSpec((tt, CONV_CHANNELS), lambda i: (i, C_CONV // CONV_CHANNELS)),
                  pl.BlockSpec((SUBLANES, CONV_CHANNELS),
                               lambda i: (jnp.maximum(i * halo_blocks - 1, 0), C_CONV // CONV_CHANNELS)),
                  pl.BlockSpec((1, SUBLANES, CONV_CHANNELS), lambda i: (i // tiles_per_seq, 0, 0)),
                  pl.BlockSpec((tt, LANES), lambda i: (i, C_MISC // LANES)),
                  pl.BlockSpec((CONV_WIDTH, CONV_CHANNELS), lambda i: (0, 0)),
                  pl.BlockSpec((1, LANES), lambda i: (0, 0)),
                  pl.BlockSpec((1, LANES), lambda i: (0, 0))],
        out_specs=[pl.BlockSpec((tt, DELTA_WIDTH), lambda i: (i, 0))] * 3
                  + [pl.BlockSpec((tt, LANES), lambda i: (i, 0))],
        out_shape=[jax.ShapeDtypeStruct((n, DELTA_WIDTH), F32)] * 3 + [jax.ShapeDtypeStruct((n, LANES), F32)],
        scratch_shapes=[pltpu.VMEM((SUBLANES + tt, CONV_CHANNELS), F32)],
        compiler_params=_cparams(("parallel",)),
    )(proj, proj, prev8, proj, conv_w, pad_vec(a_log), pad_vec(dt_bias))


def _mm(a, b):
    return _dot(a.astype(BF16), b.astype(BF16))


def _mm_nt(a, b):
    return _dot_nt(a.astype(BF16), b.astype(BF16))


DELTA_INV_BLOCK = 16
DELTA_STACK = 4


def _delta_chunk_kernel(qn_ref, kn_ref, vv_ref, bg_ref, z_ref, s0_ref, ow_ref, od_ref, so_ref, s_scr):
    c = pl.program_id(1)
    cs = DELTA_CHUNK

    @pl.when(c == 0)
    def _():
        s_scr[...] = s0_ref[0]

    bg = bg_ref[...]
    ltri = (lax.broadcasted_iota(I32, (cs, cs), 0) >= lax.broadcasted_iota(I32, (cs, cs), 1)).astype(BF16)
    g1 = bg.astype(BF16)
    r1 = bg - g1.astype(F32)
    g2 = r1.astype(BF16)
    g3 = (r1 - g2.astype(F32)).astype(BF16)
    gc = _dot(ltri, g1) + _dot(ltri, g2) + _dot(ltri, g3)
    gct = gc.T

    gh = DELTA_STACK
    rows = gh * cs
    rr = lax.broadcasted_iota(I32, (rows, rows), 0)
    cc = lax.broadcasted_iota(I32, (rows, rows), 1)
    same = (rr // cs) == (cc // cs)
    causal = jnp.logical_and(same, rr >= cc)
    strict = jnp.logical_and(same, rr > cc)
    eye = (rr == cc).astype(F32)
    row_head = lax.broadcasted_iota(I32, (rows, 1), 0) // cs
    for grp in range(N_DELTA_HEADS // gh):
        heads = [grp * gh + j for j in range(gh)]
        stack = lambda ref: jnp.concatenate([ref[:, h * HEAD_DIM:(h + 1) * HEAD_DIM] for h in heads], axis=0)
        col = lambda a, lane0: jnp.concatenate([a[:, lane0 + h:lane0 + h + 1] for h in heads], axis=0)
        k = stack(kn_ref)
        q = stack(qn_ref)
        v = stack(vv_ref)
        bcol = col(bg, 0)
        gcc = col(gc, N_DELTA_HEADS)
        gcr = jnp.concatenate([gct[N_DELTA_HEADS + h:N_DELTA_HEADS + h + 1, :] for h in heads], axis=1)
        g_last = [gc[cs - 1:cs, N_DELTA_HEADS + h:N_DELTA_HEADS + h + 1] for h in heads]
        glc = jnp.concatenate([jnp.broadcast_to(gl, (cs, 1)) for gl in g_last], axis=0)
        decay = jnp.exp(jnp.where(causal, gcc - gcr, -jnp.inf))
        kb = k * bcol
        eg = jnp.exp(gcc)
        kq = _mm_nt(jnp.concatenate([kb, q], axis=0), k)
        a = jnp.where(strict, kq[0:rows] * decay, 0.0)
        intra = jnp.where(causal, kq[rows:2 * rows] * decay, 0.0)
        x = -a
        nb = DELTA_INV_BLOCK
        y = jnp.where((rr // nb) == (cc // nb), x, 0.0)
        p = eye + y
        y = _mm(y, y)
        n_sq = max(1, int(nb - 1).bit_length())
        for lvl in range(1, n_sq):
            if lvl < n_sq - 1:
                py = _mm(jnp.concatenate([p, y], axis=0), y)
                p = p + py[0:rows]
                y = py[rows:2 * rows]
            else:
                p = p + _mm(p, y)
        size = 2 * nb
        while size <= cs:
            off = jnp.where(jnp.logical_and((rr // size) == (cc // size), (rr // (size // 2)) != (cc // (size // 2))),
                            x, 0.0)
            p = p + _mm(_mm(p, off), p)
            size *= 2
        sol = _mm(p, jnp.concatenate([v * bcol, kb * eg], axis=1))
        u = sol[:, 0:HEAD_DIM]
        w = sol[:, HEAD_DIM:2 * HEAD_DIM]
        lanes_g = slice(grp * gh * HEAD_DIM, (grp + 1) * gh * HEAD_DIM)
        s_g = s_scr[:, lanes_g]
        wq_s = _mm(jnp.concatenate([w, q * eg], axis=0), s_g)
        own = lambda m, r0: jnp.concatenate(
            [m[r0 + j * cs:r0 + (j + 1) * cs, j * HEAD_DIM:(j + 1) * HEAD_DIM] for j in range(gh)], axis=0)
        v_new = u - own(wq_s, 0)
        o = own(wq_s, rows) + _mm(intra, v_new)
        kg_t = (k * jnp.exp(glc - gcc)).T
        vn_blocks = jnp.concatenate([jnp.where(row_head == j, v_new, 0.0) for j in range(gh)], axis=1)
        s_decay = jnp.concatenate([jnp.broadcast_to(jnp.exp(gl), (1, HEAD_DIM)) for gl in g_last], axis=1)
        s_scr[:, lanes_g] = s_g * s_decay + _mm(kg_t, vn_blocks)
        on = o * lax.rsqrt(jnp.mean(o * o, axis=-1, keepdims=True) + EPS) * ow_ref[...]
        for j, h in enumerate(heads):
            sl = slice(h * HEAD_DIM, (h + 1) * HEAD_DIM)
            od_ref[:, sl] = (on[j * cs:(j + 1) * cs] * _silu(z_ref[:, sl])).astype(BF16)

    so_ref[0] = s_scr[...]


def _delta_chunks(qn, kn, vv, bg, zsrc, z_col_block, state0, o_norm_w, b, t):
    n = b * t
    nc = t // DELTA_CHUNK
    sw = N_DELTA_HEADS * HEAD_DIM
    row = lambda w, cb=0: pl.BlockSpec((DELTA_CHUNK, w), lambda bb, c: (bb * nc + c, cb))
    st = pl.BlockSpec((1, HEAD_DIM, sw), lambda bb, c: (bb, 0, 0))
    s_in = state0.transpose(0, 2, 1, 3).reshape(b, HEAD_DIM, sw)
    od, s_out = pl.pallas_call(
        _delta_chunk_kernel,
        name="delta_chunks",
        grid=(b, nc),
        in_specs=[row(DELTA_WIDTH), row(DELTA_WIDTH), row(DELTA_WIDTH), row(LANES),
                  row(DELTA_WIDTH, z_col_block), st, pl.BlockSpec((1, LANES), lambda bb, c: (0, 0))],
        out_specs=[row(DELTA_WIDTH), st],
        out_shape=[jax.ShapeDtypeStruct((n, DELTA_WIDTH), BF16),
                   jax.ShapeDtypeStruct((b, HEAD_DIM, sw), F32)],
        scratch_shapes=[pltpu.VMEM((HEAD_DIM, sw), F32)],
        compiler_params=_cparams(("parallel", "arbitrary")),
    )(qn, kn, vv, bg, zsrc, s_in, o_norm_w.reshape(1, LANES))
    return od, s_out.reshape(b, HEAD_DIM, N_DELTA_HEADS, HEAD_DIM).transpose(0, 2, 1, 3)


def _outproj_kernel(oa_ref, od_ref, x_ref, g1_ref, sh_ref, sc_ref, nw_ref, wo_ref, wrh_ref, wrl_ref, br_ref,
                    x1_ref, h2_ref, lg_ref):
    mix = _dot(oa_ref[...], wo_ref[0:ATTN_WIDTH, :]) + _dot(od_ref[...], wo_ref[ATTN_WIDTH:ATTN_WIDTH + DELTA_WIDTH, :])
    x1 = x_ref[...] + g1_ref[0] * mix
    x1_ref[...] = x1
    y = x1 * lax.rsqrt(jnp.mean(x1 * x1, axis=-1, keepdims=True) + EPS) * nw_ref[...]
    h2 = y * (1.0 + sc_ref[0]) + sh_ref[0]
    h2_ref[...] = h2
    hb = h2.astype(BF16)
    lo = (h2 - hb.astype(F32)).astype(BF16)
    lg_ref[...] = _dot(hb, wrh_ref[...]) + _dot(lo, wrh_ref[...]) + _dot(hb, wrl_ref[...]) + br_ref[...]


def _out_projection(o_attn, o_delta, x2d, gate1, shift2, scale2, norm2_w, w_out_bf, wr_hi, wr_lo, b_rt,
                    tm, rows_per_mod_block):
    n, d = x2d.shape
    r = gate1.shape[1]
    tiles_per_mod = rows_per_mod_block // tm
    mod_spec = pl.BlockSpec((1, r, d), lambda i: (i // tiles_per_mod, 0, 0))
    row = lambda w: pl.BlockSpec((tm, w), lambda i: (i, 0))
    full = lambda a: pl.BlockSpec(a.shape, lambda i: (0, 0))
    return pl.pallas_call(
        _outproj_kernel,
        name="out_proj",
        grid=(n // tm,),
        in_specs=[row(ATTN_WIDTH), row(DELTA_WIDTH), row(d), mod_spec, mod_spec, mod_spec,
                  pl.BlockSpec((1, d), lambda i: (0, 0)), full(w_out_bf), full(wr_hi), full(wr_lo), full(b_rt)],
        out_specs=[row(d), row(d), row(LANES)],
        out_shape=[jax.ShapeDtypeStruct((n, d), F32), jax.ShapeDtypeStruct((n, d), F32),
                   jax.ShapeDtypeStruct((n, LANES), F32)],
        compiler_params=_cparams(("parallel",)),
    )(o_attn, o_delta, x2d, gate1, shift2, scale2, norm2_w.reshape(1, d), w_out_bf, wr_hi, wr_lo, b_rt)


def _route_kernel(lg_ref, eid_ref, gate_ref):
    x = lg_ref[...]
    lane = lax.broadcasted_iota(I32, x.shape, 1)
    gl = jnp.where(lane < N_GROUPS, x, -jnp.inf)
    ge = jnp.exp(gl - jnp.max(gl, axis=1, keepdims=True))
    p = ge / jnp.sum(ge, axis=1, keepdims=True)
    p_max = jnp.max(p, axis=1, keepdims=True)
    grp = jnp.min(jnp.where(p == p_max, lane, LANES), axis=1, keepdims=True)
    e_lane = lane - N_GROUPS
    in_grp = jnp.logical_and(jnp.logical_and(e_lane >= 0, e_lane < N_EXPERTS),
                             (e_lane >> 3) == grp)
    rl = jnp.where(in_grp, x, -jnp.inf)
    v1 = jnp.max(rl, axis=1, keepdims=True)
    i1 = jnp.min(jnp.where(rl == v1, lane, LANES), axis=1, keepdims=True)
    rl2 = jnp.where(lane == i1, -jnp.inf, rl)
    v2 = jnp.max(rl2, axis=1, keepdims=True)
    i2 = jnp.min(jnp.where(rl2 == v2, lane, LANES), axis=1, keepdims=True)
    t = jnp.exp(v2 - v1)
    den = 1.0 + t
    eid_ref[...] = jnp.where(lane == 0, i1 - N_GROUPS, jnp.where(lane == 1, i2 - N_GROUPS, 0))
    gate_ref[...] = jnp.where(lane == 0, (1.0 / den) * p_max, jnp.where(lane == 1, (t / den) * p_max, 0.0))


def _route(logits, tm):
    n = logits.shape[0]
    spec = pl.BlockSpec((tm, LANES), lambda i: (i, 0))
    return pl.pallas_call(
        _route_kernel,
        name="route",
        grid=(n // tm,),
        in_specs=[spec],
        out_specs=[spec, spec],
        out_shape=[jax.ShapeDtypeStruct((n, LANES), I32), jax.ShapeDtypeStruct((n, LANES), F32)],
        compiler_params=_cparams(("parallel",)),
    )(logits)


def _row_gather(idx_ref, base, n_rows, src_hbm, dst, sem):
    def body(r, carry):
        pltpu.make_async_copy(src_hbm.at[pl.ds(idx_ref[base + r], 1), :], dst.at[pl.ds(r, 1), :], sem).start()
        return carry

    lax.fori_loop(0, n_rows, body, 0, unroll=8)


def _row_gather_wait(n_rows, src_hbm, dst, sem):
    pltpu.make_async_copy(src_hbm.at[pl.ds(0, n_rows), :], dst, sem).wait()


def _moe_kernel(tok_ref, be_ref, na_ref, h_hbm, wg_ref, wu_ref, wd_ref, o_ref,
                x_buf, sem, wg_scr, wu_scr, wd_scr, *, bm):
    i = pl.program_id(0)
    n_act = na_ref[0]

    @pl.when(jnp.logical_and(i == 0, n_act > 0))
    def _():
        _row_gather(tok_ref, 0, bm, h_hbm, x_buf.at[0], sem.at[0])

    @pl.when(i < n_act)
    def _():
        slot = i % 2

        @pl.when(i + 1 < n_act)
        def _():
            _row_gather(tok_ref, (i + 1) * bm, bm, h_hbm, x_buf.at[1 - slot], sem.at[1 - slot])

        changed = jnp.logical_or(i == 0, be_ref[i] != be_ref[jnp.maximum(i - 1, 0)])

        @pl.when(changed)
        def _():
            wg_scr[...] = wg_ref[0].astype(BF16)
            wu_scr[...] = wu_ref[0].astype(BF16)
            wd_scr[...] = wd_ref[0].astype(BF16)

        _row_gather_wait(bm, h_hbm, x_buf.at[slot], sem.at[slot])
        x = x_buf[slot].astype(BF16)
        hid = _silu(_dot(x, wg_scr[...])) * _dot(x, wu_scr[...])
        o_ref[...] = _dot(hid.astype(BF16), wd_scr[...])

    @pl.when(i >= n_act)
    def _():
        o_ref[...] = jnp.zeros(o_ref.shape, F32)


def _moe_experts(slot_tok, block_exp, n_active, h2, w_gate, w_up, w_down, bm):
    ns = slot_tok.shape[0]
    d = h2.shape[1]
    f = w_gate.shape[2]
    grid_spec = pltpu.PrefetchScalarGridSpec(
        num_scalar_prefetch=3,
        grid=(ns // bm,),
        in_specs=[pl.BlockSpec(memory_space=pl.ANY),
                  pl.BlockSpec((1, d, f), lambda i, tok, be, na: (be[i], 0, 0)),
                  pl.BlockSpec((1, d, f), lambda i, tok, be, na: (be[i], 0, 0)),
                  pl.BlockSpec((1, f, d), lambda i, tok, be, na: (be[i], 0, 0))],
        out_specs=pl.BlockSpec((bm, d), lambda i, tok, be, na: (i, 0)),
        scratch_shapes=[pltpu.VMEM((2, bm, d), F32), pltpu.SemaphoreType.DMA((2,)),
                        pltpu.VMEM((d, f), BF16), pltpu.VMEM((d, f), BF16), pltpu.VMEM((f, d), BF16)],
    )
    return pl.pallas_call(
        functools.partial(_moe_kernel, bm=bm),
        name="moe_experts",
        grid_spec=grid_spec,
        out_shape=jax.ShapeDtypeStruct((ns, d), F32),
        compiler_params=_cparams(("arbitrary",)),
    )(slot_tok, block_exp, n_active, h2, w_gate, w_up, w_down)


def _combine_kernel(dest_ref, x1_ref, gt_ref, g2_ref, y_hbm, o_ref, y_buf, sem, *, tm):
    i = pl.program_id(0)
    n = pl.num_programs(0)
    slot = i % 2

    @pl.when(i == 0)
    def _():
        _row_gather(dest_ref, 0, 2 * tm, y_hbm, y_buf.at[0], sem.at[0])

    @pl.when(i + 1 < n)
    def _():
        _row_gather(dest_ref, (i + 1) * 2 * tm, 2 * tm, y_hbm, y_buf.at[1 - slot], sem.at[1 - slot])

    _row_gather_wait(2 * tm, y_hbm, y_buf.at[slot], sem.at[slot])
    gt = gt_ref[...]
    y = y_buf[slot, 0:tm, :] * gt[:, 0:1] + y_buf[slot, tm:2 * tm, :] * gt[:, 1:2]
    o_ref[...] = x1_ref[...] + g2_ref[0] * y


def _combine(x1, dest, y_rows, gates, gate2, tm, rows_per_mod_block):
    n, d = x1.shape
    r = gate2.shape[1]
    tiles_per_mod = rows_per_mod_block // tm
    dest_tiles = dest.reshape(n // tm, tm, 2).transpose(0, 2, 1).reshape(-1)
    row = lambda w: pl.BlockSpec((tm, w), lambda i, dst: (i, 0))
    grid_spec = pltpu.PrefetchScalarGridSpec(
        num_scalar_prefetch=1,
        grid=(n // tm,),
        in_specs=[row(d), row(LANES),
                  pl.BlockSpec((1, r, d), lambda i, dst: (i // tiles_per_mod, 0, 0)),
                  pl.BlockSpec(memory_space=pl.ANY)],
        out_specs=row(d),
        scratch_shapes=[pltpu.VMEM((2, 2 * tm, d), F32), pltpu.SemaphoreType.DMA((2,))],
    )
    return pl.pallas_call(
        functools.partial(_combine_kernel, tm=tm),
        name="moe_combine",
        grid_spec=grid_spec,
        out_shape=jax.ShapeDtypeStruct((n, d), F32),
        compiler_params=_cparams(("arbitrary",)),
    )(dest_tiles, x1, gates, gate2, y_rows)


def _pick_tile(n, pref, mult=16):
    t = min(pref, n)
    while n % t or t % mult:
        t -= 1
    return t


def _pack_w_in(w_in):
    d = w_in.shape[0]
    bounds = np.cumsum(PROJ_SIZES)[:-1].tolist()
    qa, ka, va, qi, ki, wi, qd, kd, vd, zd, bd, ad = jnp.split(w_in, bounds, axis=1)
    used = IDX_DIM + N_IDX_HEADS + 2 * N_DELTA_HEADS
    misc = jnp.concatenate([ki, wi, bd, ad, jnp.zeros((d, LANES - used), w_in.dtype)], axis=1)
    cols = [qa, qi, zd, qd, kd, vd, ka, va, misc]
    width = sum(c.shape[1] for c in cols)
    cols.append(jnp.zeros((d, PROJ_PACKED - width), w_in.dtype))
    return jnp.concatenate(cols, axis=1).astype(BF16)


def _route_and_sort(eid, bm):
    n = eid.shape[0]
    nk = 2 * n
    flat_e = eid.reshape(-1)
    order = jnp.argsort(flat_e, stable=True).astype(I32)
    sorted_e = flat_e[order]
    counts = jnp.zeros((N_EXPERTS,), I32).at[flat_e].add(1)
    padded = (counts + bm - 1) // bm * bm
    pad_end = jnp.cumsum(padded)
    pad_start = pad_end - padded
    start = jnp.cumsum(counts) - counts
    dest_sorted = (pad_start[sorted_e] + jnp.arange(nk, dtype=I32) - start[sorted_e]).astype(I32)
    n_blocks = -(-nk // bm) + N_EXPERTS
    slot_tok = jnp.zeros((n_blocks * bm,), I32).at[dest_sorted].set(order // 2)
    dest = jnp.zeros((nk,), I32).at[order].set(dest_sorted)
    block_exp = jnp.minimum(jnp.searchsorted(pad_end, jnp.arange(n_blocks, dtype=I32) * bm, side='right'),
                            N_EXPERTS - 1).astype(I32)
    n_active = (pad_end[-1] // bm).astype(I32).reshape(1)
    return slot_tok, dest.reshape(n, 2), block_exp, n_active


def _layer(yp, ys, cache_k, cache_v, cache_idx, state_ssm, state_conv, page_table, c_prompt, c_sample,
           w_in, w_out, conv_w, a_log, dt_bias, q_norm_w, k_norm_w, idx_k_norm_w, o_norm_w, norm1_w, norm2_w,
           w_ada, b_ada, w_group, b_group, w_router, b_router, w_gate, w_up, w_down):
    bp, tp, d = yp.shape
    bs, ts, _ = ys.shape
    past = page_table.shape[1] * PAGE_SIZE
    rows = SAMPLE_ROWS
    assert CONV_WIDTH - 1 <= ts <= rows and tp % KEY_CHUNK == 0 and tp % DELTA_CHUNK == 0

    n_c = bp + bs
    n_c_pad = -(-n_c // SUBLANES) * SUBLANES
    c_all = jnp.concatenate([c_prompt, c_sample, jnp.zeros((n_c_pad - n_c, d), F32)], axis=0)
    mod = _ada_modulation(c_all, w_ada, b_ada)
    mods = jnp.split(mod, N_MOD, axis=1)
    mp = [m[:bp].reshape(bp, 1, d) for m in mods]
    ms = [jnp.repeat(m[bp:bp + bs], rows, axis=0).reshape(1, bs * rows, d) for m in mods]

    w_packed = _pack_w_in(w_in)
    w_out_bf = w_out.astype(BF16)
    w_rt = jnp.concatenate([w_group, w_router, jnp.zeros((d, LANES - N_GROUPS - N_EXPERTS), F32)], axis=1)
    wr_hi = w_rt.astype(BF16)
    wr_lo = (w_rt - wr_hi.astype(F32)).astype(BF16)
    b_rt = jnp.concatenate([b_group, b_router, jnp.zeros((LANES - N_GROUPS - N_EXPERTS,), F32)]).reshape(1, LANES)

    np_ = bp * tp
    xp2 = yp.reshape(np_, d)
    tm_p = _pick_tile(tp, 512)
    proj_p = _in_projection(xp2, mp[0], mp[1], norm1_w, w_packed, tm_p, tp)
    tq_p = _pick_tile(tp, 256)
    q_p, kf_p, kb_p, vt_p, qi_p, kif_p, kx_p = _attention_prep(
        proj_p, jnp.arange(tp), tq_p, q_norm_w, k_norm_w, idx_k_norm_w, True)
    oa_p = _dsa_prompt(q_p, qi_p, proj_p, kb_p, vt_p, kx_p, bp, tp)
    tt_p = _pick_tile(tp, 256)
    qn_p, kn_p, vv_p, bg_p = _delta_prep(proj_p, jnp.zeros((bp, SUBLANES, CONV_CHANNELS), F32), conv_w,
                                         a_log, dt_bias, bp, tp, tt_p, tp)
    od_p, ssm_p = _delta_chunks(qn_p, kn_p, vv_p, bg_p, proj_p, C_ZD // DELTA_WIDTH,
                                jnp.zeros((bp, N_DELTA_HEADS, HEAD_DIM, HEAD_DIM), F32), o_norm_w, bp, tp)
    tm_o = _pick_tile(tp, 256)
    x1_p, h2_p, lg_p = _out_projection(oa_p, od_p, xp2, mp[2], mp[3], mp[4], norm2_w, w_out_bf, wr_hi, wr_lo, b_rt,
                                       tm_o, tp)

    ns_ = bs * rows
    xs2 = jnp.pad(ys, ((0, 0), (0, rows - ts), (0, 0))).reshape(ns_, d)
    proj_s = _in_projection(xs2, ms[0], ms[1], norm1_w, w_packed, ns_, ns_)
    q_s, kf_s, kb_s, vb_s, qi_s, kif_s, _ = _attention_prep(
        proj_s, past + jnp.arange(rows), rows, q_norm_w, k_norm_w, idx_k_norm_w, False)
    q_t = qi_s.reshape(bs, rows, N_IDX_HEADS, IDX_DIM).transpose(0, 2, 1, 3).reshape(bs, N_IDX_HEADS * rows, IDX_DIM)
    w_col = proj_s[:, C_MISC + M_WI:C_MISC + M_WI + N_IDX_HEADS].reshape(bs, rows, N_IDX_HEADS)
    w_col = w_col.transpose(0, 2, 1).reshape(bs, N_IDX_HEADS * rows, 1)
    pages = _pick_tile(page_table.shape[1], 8, 1)
    keys_past, keys_new = _sample_scores(page_table, q_t, w_col, kif_s, cache_idx, pages, ts)
    n_sel_s = min(TOPK_MAX, (past + ts) // 4)
    oa_s = _sample_attend(page_table, keys_past, keys_new, q_s, kb_s, vb_s,
                          cache_k.reshape(-1, PAGE_SIZE, KV_WIDTH), cache_v.reshape(-1, PAGE_SIZE, KV_WIDTH),
                          pages, n_sel_s)
    prev8 = jnp.pad(state_conv, ((0, 0), (SUBLANES - (CONV_WIDTH - 1), 0), (0, 0)))
    qn_s, kn_s, vv_s, bg_s = _delta_prep(proj_s, prev8, conv_w, a_log, dt_bias, bs, rows, rows, ts)
    to_chunk = lambda a: jnp.pad(a.reshape(bs, rows, -1), ((0, 0), (0, DELTA_CHUNK - rows), (0, 0))).reshape(
        bs * DELTA_CHUNK, -1)
    z_s = proj_s[:, C_ZD:C_ZD + DELTA_WIDTH]
    od_s, ssm_s = _delta_chunks(to_chunk(qn_s), to_chunk(kn_s), to_chunk(vv_s), to_chunk(bg_s), to_chunk(z_s), 0,
                                state_ssm, o_norm_w, bs, DELTA_CHUNK)
    od_s = od_s.reshape(bs, DELTA_CHUNK, DELTA_WIDTH)[:, :rows].reshape(ns_, DELTA_WIDTH)
    x1_s, h2_s, lg_s = _out_projection(oa_s, od_s, xs2, ms[2], ms[3], ms[4], norm2_w, w_out_bf, wr_hi, wr_lo, b_rt,
                                       ns_, ns_)

    n_all = np_ + ns_
    h2_all = jnp.concatenate([h2_p, h2_s], axis=0)
    lg_all = jnp.concatenate([lg_p, lg_s], axis=0)
    eid, gates = _route(lg_all, _pick_tile(n_all, 512, SUBLANES))
    bm = 256
    slot_tok, dest, block_exp, n_active = _route_and_sort(eid[:, 0:2], bm)
    yb = _moe_experts(slot_tok, block_exp, n_active, h2_all, w_gate, w_up, w_down, bm)
    out_p = _combine(x1_p, dest[:np_], yb, gates[:np_], mp[5], tm_o, tp)
    out_s = _combine(x1_s, dest[np_:], yb, gates[np_:], ms[5], ns_, ns_)

    valid = lambda a: a.reshape(bs, rows, -1)[:, :ts]
    conv_p = proj_p.reshape(bp, tp, PROJ_PACKED)[:, tp - (CONV_WIDTH - 1):, C_CONV:C_CONV + CONV_CHANNELS]
    conv_s = proj_s.reshape(bs, rows, PROJ_PACKED)[:, ts - (CONV_WIDTH - 1):ts, C_CONV:C_CONV + CONV_CHANNELS]
    return (out_p.reshape(bp, tp, d), valid(out_s),
            kf_p.reshape(bp, tp, N_KV_HEADS, HEAD_DIM),
            proj_p[:, C_VA:C_VA + KV_WIDTH].reshape(bp, tp, N_KV_HEADS, HEAD_DIM),
            kif_p.reshape(bp, tp, IDX_DIM), ssm_p, conv_p,
            valid(kf_s).reshape(bs, ts, N_KV_HEADS, HEAD_DIM),
            valid(proj_s[:, C_VA:C_VA + KV_WIDTH]).reshape(bs, ts, N_KV_HEADS, HEAD_DIM),
            valid(kif_s), ssm_s, conv_s)


def kernel(x_prompt, x_sample, cache_k, cache_v, cache_idx_k, state_ssm, state_conv, page_table, c_prompt, c_sample,
           w_in, w_out, conv_w, a_log, dt_bias, q_norm_w, k_norm_w, idx_k_norm_w, o_norm_w, norm1_w, norm2_w,
           w_ada, b_ada, w_group, b_group, w_router, b_router, w_gate, w_up, w_down):
    depth = w_in.shape[0]
    yp, ys = x_prompt, x_sample
    per_layer = []
    for l in range(depth):
        res = _layer(yp, ys, cache_k[l], cache_v[l], cache_idx_k[l], state_ssm[l], state_conv[l], page_table,
                     c_prompt, c_sample, w_in[l], w_out[l], conv_w[l], a_log[l], dt_bias[l], q_norm_w[l],
                     k_norm_w[l], idx_k_norm_w[l], o_norm_w[l], norm1_w[l], norm2_w[l], w_ada[l], b_ada[l],
                     w_group[l], b_group[l], w_router[l], b_router[l], w_gate[l], w_up[l], w_down[l])
        yp, ys = res[0], res[1]
        per_layer.append(res[2:])
    stacked = tuple(jnp.stack([pl_[j] for pl_ in per_layer]) for j in range(10))
    return (yp, ys) + stacked
```

```python
import functools

import jax
import jax.numpy as jnp
import numpy as np
from jax import lax
from jax.experimental import pallas as pl
from jax.experimental.pallas import tpu as pltpu

F32 = jnp.float32
BF16 = jnp.bfloat16
I32 = jnp.int32

HEAD_DIM = 128
N_ATTN_HEADS = 8
N_KV_HEADS = 2
KV_GROUP = N_ATTN_HEADS // N_KV_HEADS
N_DELTA_HEADS = 8
N_IDX_HEADS = 16
IDX_DIM = 64
ATTN_WIDTH = N_ATTN_HEADS * HEAD_DIM
KV_WIDTH = N_KV_HEADS * HEAD_DIM
DELTA_WIDTH = N_DELTA_HEADS * HEAD_DIM
IDX_WIDTH = N_IDX_HEADS * IDX_DIM
CONV_CHANNELS = 3 * DELTA_WIDTH
TOPK_MAX = 256
ROPE_THETA = 500000.0
ROPE_FRACTION = 4
CONV_WIDTH = 4
DELTA_CHUNK = 64
N_GROUPS = 8
EXPERTS_PER_GROUP = 8
N_EXPERTS = N_GROUPS * EXPERTS_PER_GROUP
N_MOD = 6
EPS = 1e-6
PAGE_SIZE = 128
PROJ_SIZES = (ATTN_WIDTH, KV_WIDTH, KV_WIDTH, IDX_WIDTH, IDX_DIM, N_IDX_HEADS,
              DELTA_WIDTH, DELTA_WIDTH, DELTA_WIDTH, DELTA_WIDTH, N_DELTA_HEADS, N_DELTA_HEADS)

LANES = 128
SUBLANES = 8
VMEM_LIMIT = 56 * 1024 * 1024

C_QA = 0
C_QI = 1024
C_ZD = 2048
C_CONV = 3072
C_KA = 6144
C_VA = 6400
C_MISC = 6656
PROJ_PACKED = 6912
M_KI = 0
M_WI = 64
M_BD = 80
M_AD = 88

Q_TILE = 128
KEY_CHUNK = 256
SAMPLE_ROWS = 16
NEG_BIG = -1e30
INT_MIN = -2147483648
INT_MAX = 2147483647


def _cparams(sem):
    return pltpu.CompilerParams(dimension_semantics=sem, vmem_limit_bytes=VMEM_LIMIT)


def _dot(a, b):
    return jnp.dot(a, b, preferred_element_type=F32)


def _dot_nt(a, b):
    return lax.dot_general(a, b, (((1,), (1,)), ((), ())), preferred_element_type=F32)


def _dot_tn(a, b):
    return lax.dot_general(a, b, (((0,), (0,)), ((), ())), preferred_element_type=F32)


def _sigmoid(x):
    return 1.0 / (1.0 + jnp.exp(-x))


def _silu(x):
    return x * _sigmoid(x)


def _softplus(x):
    return jnp.maximum(x, 0.0) + jnp.log(1.0 + jnp.exp(-jnp.abs(x)))


def _ada_kernel(c_ref, w_ref, b_ref, o_ref):
    s = _silu(c_ref[...]).astype(BF16)
    o_ref[...] = _dot(s, w_ref[...].astype(BF16)) + b_ref[...]


def _ada_modulation(c, w_ada, b_ada):
    r, d = c.shape
    n = w_ada.shape[1]
    tn = 1024 if n % 1024 == 0 else n
    return pl.pallas_call(
        _ada_kernel,
        name="ada_mod",
        grid=(n // tn,),
        in_specs=[pl.BlockSpec((r, d), lambda j: (0, 0)),
                  pl.BlockSpec((d, tn), lambda j: (0, j)),
                  pl.BlockSpec((1, tn), lambda j: (0, j))],
        out_specs=pl.BlockSpec((r, tn), lambda j: (0, j)),
        out_shape=jax.ShapeDtypeStruct((r, n), F32),
        compiler_params=_cparams(("parallel",)),
    )(c, w_ada, b_ada.reshape(1, n))


def _inproj_kernel(x_ref, sh_ref, sc_ref, nw_ref, w_ref, o_ref, h_scr):
    @pl.when(pl.program_id(1) == 0)
    def _():
        x = x_ref[...]
        y = x * lax.rsqrt(jnp.mean(x * x, axis=-1, keepdims=True) + EPS) * nw_ref[...]
        h_scr[...] = (y * (1.0 + sc_ref[0]) + sh_ref[0]).astype(BF16)

    o_ref[...] = _dot(h_scr[...], w_ref[...])


def _in_projection(x2d, shift, scale, norm_w, w_packed, tm, rows_per_mod_block):
    n, d = x2d.shape
    np_ = w_packed.shape[1]
    tn = 1152
    r = shift.shape[1]
    tiles_per_mod = rows_per_mod_block // tm
    mod_spec = pl.BlockSpec((1, r, d), lambda i, j: (i // tiles_per_mod, 0, 0))
    return pl.pallas_call(
        _inproj_kernel,
        name="in_proj",
        grid=(n // tm, np_ // tn),
        in_specs=[pl.BlockSpec((tm, d), lambda i, j: (i, 0)),
                  mod_spec, mod_spec,
                  pl.BlockSpec((1, d), lambda i, j: (0, 0)),
                  pl.BlockSpec((d, tn), lambda i, j: (0, j))],
        out_specs=pl.BlockSpec((tm, tn), lambda i, j: (i, j)),
        out_shape=jax.ShapeDtypeStruct((n, np_), F32),
        scratch_shapes=[pltpu.VMEM((tm, d), BF16)],
        compiler_params=_cparams(("parallel", "arbitrary")),
    )(x2d, shift, scale, norm_w.reshape(1, d), w_packed)


def _rope(x, tab, rot):
    c = tab[:, 0:LANES]
    s1 = tab[:, LANES:2 * LANES]
    s2 = tab[:, 2 * LANES:3 * LANES]
    return x * c + pltpu.roll(x, LANES - rot, 1) * s1 + pltpu.roll(x, rot, 1) * s2


def _rms_head(x, w):
    return x * lax.rsqrt(jnp.mean(x * x, axis=-1, keepdims=True) + EPS) * w


def _prep_kernel(qa_ref, qi_ref, ka_ref, va_ref, misc_ref, tabm_ref, tabi_ref, qw_ref, kw_ref, iw_ref,
                 q_ref, kf_ref, kb_ref, vb_ref, qib_ref, kif_ref, kib_ref, *, transpose_v):
    tabm = tabm_ref[0]
    tabi = tabi_ref[0]
    half_main = HEAD_DIM // ROPE_FRACTION // 2
    half_idx = IDX_DIM // ROPE_FRACTION // 2
    for h in range(N_ATTN_HEADS):
        sl = slice(h * HEAD_DIM, (h + 1) * HEAD_DIM)
        y = _rope(_rms_head(qa_ref[:, sl], qw_ref[...]), tabm, half_main)
        q_ref[:, sl] = (y * (HEAD_DIM ** -0.5)).astype(BF16)
    for h in range(N_KV_HEADS):
        sl = slice(h * HEAD_DIM, (h + 1) * HEAD_DIM)
        y = _rope(_rms_head(ka_ref[:, sl], kw_ref[...]), tabm, half_main)
        kf_ref[:, sl] = y
        kb_ref[:, sl] = y.astype(BF16)
    if transpose_v:
        vb_ref[...] = va_ref[...].T.astype(BF16)
    else:
        vb_ref[...] = va_ref[...].astype(BF16)
    for p in range(IDX_WIDTH // LANES):
        sl = slice(p * LANES, (p + 1) * LANES)
        qib_ref[:, sl] = _rope(qi_ref[:, sl], tabi, half_idx).astype(BF16)
    m = misc_ref[...]
    lane = lax.broadcasted_iota(I32, m.shape, 1)
    ki = jnp.where(lane < IDX_DIM, m, 0.0)
    ms = jnp.sum(ki * ki, axis=-1, keepdims=True) * (1.0 / IDX_DIM)
    y = _rope(ki * lax.rsqrt(ms + EPS) * iw_ref[...], tabi, half_idx)
    kif_ref[...] = y[:, 0:IDX_DIM]
    kib_ref[...] = (y + pltpu.roll(y, IDX_DIM, 1)).astype(BF16)


def _rope_tables(pos, head_dim, group):
    d_rot = head_dim // ROPE_FRACTION
    half = d_rot // 2
    inv_freq = jnp.power(ROPE_THETA, -(jnp.arange(half, dtype=F32) * 2.0 / d_rot))
    ang = pos.astype(F32)[:, None] * inv_freq[None, :]
    cos = jnp.cos(ang)
    sin = jnp.sin(ang)
    t = pos.shape[0]
    z = jnp.zeros((t, group - d_rot), F32)
    c = jnp.concatenate([cos, cos, jnp.ones((t, group - d_rot), F32)], axis=1)
    s1 = jnp.concatenate([-sin, jnp.zeros((t, half), F32), z], axis=1)
    s2 = jnp.concatenate([jnp.zeros((t, half), F32), sin, z], axis=1)
    rep = LANES // group
    return jnp.concatenate([jnp.tile(c, (1, rep)), jnp.tile(s1, (1, rep)), jnp.tile(s2, (1, rep))], axis=1)


def _attention_prep(proj, pos, tq, q_norm_w, k_norm_w, idx_k_norm_w, transpose_v):
    n = proj.shape[0]
    p = pos.shape[0]
    g = p // tq
    tabm = _rope_tables(pos, HEAD_DIM, LANES).reshape(g, tq, 3 * LANES)
    tabi = _rope_tables(pos, IDX_DIM, IDX_DIM).reshape(g, tq, 3 * LANES)
    iw = jnp.concatenate([idx_k_norm_w, jnp.zeros((LANES - IDX_DIM,), F32)]).reshape(1, LANES)
    row = lambda w, c: pl.BlockSpec((tq, w), lambda i: (i, c // w))
    tab_spec = pl.BlockSpec((1, tq, 3 * LANES), lambda i: (i % g, 0, 0))
    vec_spec = pl.BlockSpec((1, LANES), lambda i: (0, 0))
    out_row = lambda w: pl.BlockSpec((tq, w), lambda i: (i, 0))
    v_spec = pl.BlockSpec((KV_WIDTH, tq), lambda i: (0, i)) if transpose_v else out_row(KV_WIDTH)
    v_shape = (KV_WIDTH, n) if transpose_v else (n, KV_WIDTH)
    return pl.pallas_call(
        functools.partial(_prep_kernel, transpose_v=transpose_v),
        name="attn_prep",
        grid=(n // tq,),
        in_specs=[row(ATTN_WIDTH, C_QA), row(IDX_WIDTH, C_QI), row(KV_WIDTH, C_KA), row(KV_WIDTH, C_VA),
                  row(LANES, C_MISC), tab_spec, tab_spec, vec_spec, vec_spec, vec_spec],
        out_specs=[out_row(ATTN_WIDTH), out_row(KV_WIDTH), out_row(KV_WIDTH), v_spec,
                   out_row(IDX_WIDTH), out_row(IDX_DIM), out_row(LANES)],
        out_shape=[jax.ShapeDtypeStruct((n, ATTN_WIDTH), BF16),
                   jax.ShapeDtypeStruct((n, KV_WIDTH), F32),
                   jax.ShapeDtypeStruct((n, KV_WIDTH), BF16),
                   jax.ShapeDtypeStruct(v_shape, BF16),
                   jax.ShapeDtypeStruct((n, IDX_WIDTH), BF16),
                   jax.ShapeDtypeStruct((n, IDX_DIM), F32),
                   jax.ShapeDtypeStruct((n, LANES), BF16)],
        compiler_params=_cparams(("parallel",)),
    )(proj, proj, proj, proj, proj, tabm, tabi,
      q_norm_w.reshape(1, LANES), k_norm_w.reshape(1, LANES), iw)


def _sort_key(x):
    b = pltpu.bitcast(x + 0.0, I32)
    return b ^ ((b >> 31) & INT_MAX)


def _kth_largest_key(count_ge, k, shape):
    def body(it, ans_u):
        bit = jnp.left_shift(jnp.int32(1), 31 - it)
        cand_u = ans_u | bit
        cnt = count_ge(cand_u ^ INT_MIN)
        return jnp.where(cnt >= k, cand_u, ans_u)

    ans_u = lax.fori_loop(0, 32, body, jnp.zeros(shape, I32))
    return ans_u ^ INT_MIN


def _tie_index_limit(count_eq_le, need, n_keys, shape):
    nbits = max(1, int(n_keys - 1).bit_length())

    def body(it, lo):
        bit = jnp.left_shift(jnp.int32(1), nbits - 1 - it)
        cand = lo | bit
        cnt = count_eq_le(cand - 1)
        return jnp.where(cnt >= need, lo, cand)

    return lax.fori_loop(0, nbits, body, jnp.zeros(shape, I32))


def _dsa_prompt_kernel(q_ref, qi_ref, misc_ref, k_ref, vt_ref, kx_ref, o_ref,
                       key_scr, qsel_scr, thr_scr, lim_scr, m_scr, l_scr, acc_scr, *, n_sel):
    i = pl.program_id(1)
    tq = Q_TILE
    ck = KEY_CHUNK
    n_ch = (i * tq + tq + ck - 1) // ck
    q_pos = i * tq + lax.broadcasted_iota(I32, (1, tq), 1)
    row_k = lax.broadcasted_iota(I32, (ck, 1), 0)

    lo_half = lax.broadcasted_iota(I32, (tq, LANES), 1) < IDX_DIM
    zero = jnp.zeros((), BF16)
    for p in range(IDX_WIDTH // LANES):
        slab = qi_ref[:, p * LANES:(p + 1) * LANES]
        qsel_scr[(2 * p) * tq:(2 * p + 1) * tq, :] = jnp.where(lo_half, slab, zero)
        qsel_scr[(2 * p + 1) * tq:(2 * p + 2) * tq, :] = jnp.where(lo_half, zero, slab)
    w_t = misc_ref[...].T

    def score_chunk(c, carry):
        off = pl.multiple_of(c * ck, ck)
        s = _dot_nt(kx_ref[pl.ds(off, ck), :], qsel_scr[...])
        acc = jnp.zeros((ck, tq), F32)
        for h in range(N_IDX_HEADS):
            acc = acc + w_t[M_WI + h:M_WI + h + 1, :] * jnp.maximum(s[:, h * tq:(h + 1) * tq], 0.0)
        acc = jnp.where(off + row_k <= q_pos, acc, -jnp.inf)
        key_scr[pl.ds(off, ck), :] = _sort_key(acc)
        return carry

    lax.fori_loop(0, n_ch, score_chunk, 0)

    def count_where(pred):
        def body(c, cnt):
            off = pl.multiple_of(c * ck, ck)
            hit = pred(key_scr[pl.ds(off, ck), :], off + row_k).astype(F32)
            return cnt + jnp.sum(hit.reshape(ck // SUBLANES, SUBLANES, tq), axis=0)
        cnt = lax.fori_loop(0, n_ch, body, jnp.zeros((SUBLANES, tq), F32))
        return jnp.sum(cnt, axis=0, keepdims=True)

    thr_scr[...] = jnp.full((1, tq), INT_MIN, I32)
    lim_scr[...] = jnp.full((1, tq), INT_MAX, I32)

    @pl.when((i + 1) * tq > n_sel)
    def _():
        t = _kth_largest_key(lambda cand: count_where(lambda kk, pos: kk >= cand), float(n_sel), (1, tq))
        thr_scr[...] = t
        n_gt = count_where(lambda kk, pos: kk > t)
        n_ge = count_where(lambda kk, pos: kk >= t)

        @pl.when(jnp.max(n_ge) > float(n_sel))
        def _():
            lim_scr[...] = _tie_index_limit(
                lambda idx: count_where(lambda kk, pos: jnp.logical_and(kk == t, pos <= idx)),
                float(n_sel) - n_gt, k_ref.shape[0], (1, tq))

    thr = thr_scr[...]
    lim = lim_scr[...]

    for g in range(N_KV_HEADS):
        qg = jnp.concatenate(
            [q_ref[:, (g * KV_GROUP + r) * HEAD_DIM:(g * KV_GROUP + r + 1) * HEAD_DIM] for r in range(KV_GROUP)],
            axis=0)
        m_scr[...] = jnp.full(m_scr.shape, NEG_BIG, F32)
        l_scr[...] = jnp.zeros(l_scr.shape, F32)
        acc_scr[...] = jnp.zeros(acc_scr.shape, F32)

        def attend_chunk(c, carry):
            off = pl.multiple_of(c * ck, ck)
            kc = k_ref[pl.ds(off, ck), g * HEAD_DIM:(g + 1) * HEAD_DIM]
            vt = vt_ref[g * HEAD_DIM:(g + 1) * HEAD_DIM, pl.ds(off, ck)]
            kk = key_scr[pl.ds(off, ck), :]
            pos = off + row_k
            sel = jnp.logical_or(kk > thr, jnp.logical_and(kk == thr, pos <= lim))
            sel = jnp.logical_and(sel, pos <= q_pos)
            sel = jnp.concatenate([sel] * KV_GROUP, axis=1)
            s = jnp.where(sel, _dot_nt(kc, qg), NEG_BIG)
            m_old = m_scr[...]
            m_new = jnp.maximum(m_old, jnp.max(s, axis=0, keepdims=True))
            p = jnp.where(sel, jnp.exp(s - m_new), 0.0)
            alpha = jnp.exp(m_old - m_new)
            l_scr[...] = alpha * l_scr[...] + jnp.sum(p, axis=0, keepdims=True)
            acc_scr[...] = alpha * acc_scr[...] + _dot(vt, p.astype(BF16))
            m_scr[...] = m_new
            return carry

        lax.fori_loop(0, n_ch, attend_chunk, 0)
        o_t = acc_scr[...] / l_scr[...]
        for r in range(KV_GROUP):
            h = g * KV_GROUP + r
            o_ref[:, h * HEAD_DIM:(h + 1) * HEAD_DIM] = o_t[:, r * tq:(r + 1) * tq].T.astype(BF16)


def _dsa_prompt(q_bf, qi_bf, proj, k_bf, vt_bf, kx_bf, b, t):
    n = b * t
    nq = t // Q_TILE
    n_sel = min(TOPK_MAX, t // 4)
    qrow = lambda w: pl.BlockSpec((Q_TILE, w), lambda bb, i: (bb * nq + i, 0))
    seq = lambda w: pl.BlockSpec((t, w), lambda bb, i: (bb, 0))
    return pl.pallas_call(
        functools.partial(_dsa_prompt_kernel, n_sel=n_sel),
        name="dsa_prompt",
        grid=(b, nq),
        in_specs=[qrow(ATTN_WIDTH), qrow(IDX_WIDTH),
                  pl.BlockSpec((Q_TILE, LANES), lambda bb, i: (bb * nq + i, C_MISC // LANES)),
                  seq(KV_WIDTH), pl.BlockSpec((KV_WIDTH, t), lambda bb, i: (0, bb)), seq(LANES)],
        out_specs=qrow(ATTN_WIDTH),
        out_shape=jax.ShapeDtypeStruct((n, ATTN_WIDTH), BF16),
        scratch_shapes=[pltpu.VMEM((t, Q_TILE), I32),
                        pltpu.VMEM((N_IDX_HEADS * Q_TILE, LANES), BF16),
                        pltpu.VMEM((1, Q_TILE), I32),
                        pltpu.VMEM((1, Q_TILE), I32),
                        pltpu.VMEM((1, KV_GROUP * Q_TILE), F32),
                        pltpu.VMEM((1, KV_GROUP * Q_TILE), F32),
                        pltpu.VMEM((HEAD_DIM, KV_GROUP * Q_TILE), F32)],
        compiler_params=_cparams(("parallel", "arbitrary")),
    )(q_bf, qi_bf, proj, k_bf, vt_bf, kx_bf)


def _sample_score_kernel(pt_ref, q_ref, w_ref, kn_ref, *refs, pages, t_valid):
    page_refs = refs[:pages]
    past_ref, new_ref = refs[pages], refs[pages + 1]
    rows = SAMPLE_ROWS
    q = q_ref[0]
    w = w_ref[0]

    def head_sum(s):
        s = w * jnp.maximum(s, 0.0)
        acc = s[0:rows]
        for h in range(1, N_IDX_HEADS):
            acc = acc + s[h * rows:(h + 1) * rows]
        return acc

    for j in range(pages):
        kp = page_refs[j][0, 0].astype(BF16)
        past_ref[0, :, j * PAGE_SIZE:(j + 1) * PAGE_SIZE] = _sort_key(head_sum(_dot_nt(q, kp)))

    @pl.when(pl.program_id(1) == 0)
    def _():
        kn = jnp.concatenate([kn_ref[...], jnp.zeros((LANES - rows, IDX_DIM), F32)], axis=0).astype(BF16)
        sc = head_sum(_dot_nt(q, kn))
        t = lax.broadcasted_iota(I32, sc.shape, 0)
        s = lax.broadcasted_iota(I32, sc.shape, 1)
        ok = jnp.logical_and(s <= t, s < t_valid)
        new_ref[0] = _sort_key(jnp.where(ok, sc, -jnp.inf))


def _sample_scores(page_table, q_t, w_col, kif, cache_idx, layer, pages, t_valid):
    bs, n_pages = page_table.shape
    past = n_pages * PAGE_SIZE
    hr = N_IDX_HEADS * SAMPLE_ROWS
    page_spec = lambda j: pl.BlockSpec((1, 1, PAGE_SIZE, IDX_DIM),
                                       lambda b, c, pt: (layer, pt[b, c * pages + j], 0, 0))
    grid_spec = pltpu.PrefetchScalarGridSpec(
        num_scalar_prefetch=1,
        grid=(bs, n_pages // pages),
        in_specs=[pl.BlockSpec((1, hr, IDX_DIM), lambda b, c, pt: (b, 0, 0)),
                  pl.BlockSpec((1, hr, 1), lambda b, c, pt: (b, 0, 0)),
                  pl.BlockSpec((SAMPLE_ROWS, IDX_DIM), lambda b, c, pt: (b, 0))]
                 + [page_spec(j) for j in range(pages)],
        out_specs=[pl.BlockSpec((1, SAMPLE_ROWS, pages * PAGE_SIZE), lambda b, c, pt: (b, 0, c)),
                   pl.BlockSpec((1, SAMPLE_ROWS, LANES), lambda b, c, pt: (b, 0, 0))],
    )
    return pl.pallas_call(
        functools.partial(_sample_score_kernel, pages=pages, t_valid=t_valid),
        name="sample_scores",
        grid_spec=grid_spec,
        out_shape=[jax.ShapeDtypeStruct((bs, SAMPLE_ROWS, past), I32),
                   jax.ShapeDtypeStruct((bs, SAMPLE_ROWS, LANES), I32)],
        compiler_params=_cparams(("parallel", "arbitrary")),
    )(page_table, q_t, w_col, kif, *([cache_idx] * pages))


def _sample_attend_kernel(pt_ref, kp_ref, kn_ref, q_ref, knew_ref, vnew_ref, *refs, pages, n_sel, past):
    k_pages = refs[:pages]
    v_pages = refs[pages:2 * pages]
    o_ref = refs[2 * pages]
    thr_scr, lim_scr, m_scr, l_scr, acc_scr = refs[2 * pages + 1:]
    c = pl.program_id(1)
    rows = SAMPLE_ROWS
    span = pages * PAGE_SIZE

    @pl.when(c == 0)
    def _():
        m_scr[...] = jnp.full(m_scr.shape, NEG_BIG, F32)
        l_scr[...] = jnp.zeros(l_scr.shape, F32)
        acc_scr[...] = jnp.zeros(acc_scr.shape, F32)
        kp = kp_ref[0]
        kn = kn_ref[0]
        pos_p = lax.broadcasted_iota(I32, kp.shape, 1)
        pos_n = past + lax.broadcasted_iota(I32, kn.shape, 1)

        def count_where(pred):
            return (jnp.sum(pred(kp, pos_p).astype(F32), axis=1, keepdims=True)
                    + jnp.sum(pred(kn, pos_n).astype(F32), axis=1, keepdims=True))

        t = _kth_largest_key(lambda cand: count_where(lambda kk, pos: kk >= cand), float(n_sel), (rows, 1))
        thr_scr[...] = t
        lim_scr[...] = jnp.full((rows, 1), INT_MAX, I32)
        n_gt = count_where(lambda kk, pos: kk > t)
        n_ge = count_where(lambda kk, pos: kk >= t)

        @pl.when(jnp.max(n_ge) > float(n_sel))
        def _():
            lim_scr[...] = _tie_index_limit(
                lambda idx: count_where(lambda kk, pos: jnp.logical_and(kk == t, pos <= idx)),
                float(n_sel) - n_gt, past + LANES, (rows, 1))

    thr = thr_scr[...]
    lim = lim_scr[...]

    def update(g, qg, kc, vc, sel):
        sel = jnp.concatenate([sel] * KV_GROUP, axis=0)
        s = jnp.where(sel, _dot_nt(qg, kc), NEG_BIG)
        m_old = m_scr[g]
        m_new = jnp.maximum(m_old, jnp.max(s, axis=1, keepdims=True))
        p = jnp.where(sel, jnp.exp(s - m_new), 0.0)
        alpha = jnp.exp(m_old - m_new)
        l_scr[g] = alpha * l_scr[g] + jnp.sum(p, axis=1, keepdims=True)
        acc_scr[g] = alpha * acc_scr[g] + _dot(p.astype(BF16), vc)
        m_scr[g] = m_new

    def select(kk, pos):
        return jnp.logical_or(kk > thr, jnp.logical_and(kk == thr, pos <= lim))

    def page_cat(page_refs, g):
        return jnp.concatenate([r[0, 0, :, g, :] for r in page_refs], axis=0).astype(BF16)

    off = pl.multiple_of(c * span, span)
    kk = kp_ref[0, :, pl.ds(off, span)]
    sel_past = select(kk, off + lax.broadcasted_iota(I32, kk.shape, 1))
    q_groups = []
    for g in range(N_KV_HEADS):
        qg = jnp.concatenate(
            [q_ref[:, (g * KV_GROUP + r) * HEAD_DIM:(g * KV_GROUP + r + 1) * HEAD_DIM] for r in range(KV_GROUP)],
            axis=0)
        q_groups.append(qg)
        update(g, qg, page_cat(k_pages, g), page_cat(v_pages, g), sel_past)

    @pl.when(c == pl.num_programs(1) - 1)
    def _():
        kn = kn_ref[0]
        lane = lax.broadcasted_iota(I32, kn.shape, 1)
        sel_new = jnp.logical_and(select(kn, past + lane), lane < rows)
        pad = jnp.zeros((LANES - rows, KV_WIDTH), BF16)
        k_new = jnp.concatenate([knew_ref[...], pad], axis=0)
        v_new = jnp.concatenate([vnew_ref[...], pad], axis=0)
        for g in range(N_KV_HEADS):
            sl = slice(g * HEAD_DIM, (g + 1) * HEAD_DIM)
            update(g, q_groups[g], k_new[:, sl], v_new[:, sl], sel_new)
            o = acc_scr[g] / l_scr[g]
            for r in range(KV_GROUP):
                h = g * KV_GROUP + r
                o_ref[:, h * HEAD_DIM:(h + 1) * HEAD_DIM] = o[r * rows:(r + 1) * rows].astype(BF16)


def _sample_attend(page_table, keys_past, keys_new, q_bf, k_bf, v_bf, cache_k, cache_v, layer, pages, n_sel):
    bs, n_pages = page_table.shape
    past = n_pages * PAGE_SIZE
    page_spec = lambda j: pl.BlockSpec((1, 1, PAGE_SIZE, N_KV_HEADS, HEAD_DIM),
                                       lambda b, c, pt: (layer, pt[b, c * pages + j], 0, 0, 0))
    row = lambda w: pl.BlockSpec((SAMPLE_ROWS, w), lambda b, c, pt: (b, 0))
    grid_spec = pltpu.PrefetchScalarGridSpec(
        num_scalar_prefetch=1,
        grid=(bs, n_pages // pages),
        in_specs=[pl.BlockSpec((1, SAMPLE_ROWS, past), lambda b, c, pt: (b, 0, 0)),
                  pl.BlockSpec((1, SAMPLE_ROWS, LANES), lambda b, c, pt: (b, 0, 0)),
                  row(ATTN_WIDTH), row(KV_WIDTH), row(KV_WIDTH)]
                 + [page_spec(j) for j in range(pages)] * 2,
        out_specs=row(ATTN_WIDTH),
        scratch_shapes=[pltpu.VMEM((SAMPLE_ROWS, 1), I32),
                        pltpu.VMEM((SAMPLE_ROWS, 1), I32),
                        pltpu.VMEM((N_KV_HEADS, KV_GROUP * SAMPLE_ROWS, 1), F32),
                        pltpu.VMEM((N_KV_HEADS, KV_GROUP * SAMPLE_ROWS, 1), F32),
                        pltpu.VMEM((N_KV_HEADS, KV_GROUP * SAMPLE_ROWS, HEAD_DIM), F32)],
    )
    return pl.pallas_call(
        functools.partial(_sample_attend_kernel, pages=pages, n_sel=n_sel, past=past),
        name="sample_attend",
        grid_spec=grid_spec,
        out_shape=jax.ShapeDtypeStruct((bs * SAMPLE_ROWS, ATTN_WIDTH), BF16),
        compiler_params=_cparams(("parallel", "arbitrary")),
    )(page_table, keys_past, keys_new, q_bf, k_bf, v_bf, *([cache_k] * pages), *([cache_v] * pages))


def _delta_prep_kernel(x_ref, halo_ref, prev_ref, misc_ref, cw_ref, al_ref, dt_ref,
                       qn_ref, kn_ref, vv_ref, bg_ref, xp_scr, *, tiles_per_seq, t_valid, tt):
    i = pl.program_id(0)
    tile_in_seq = i % tiles_per_seq
    halo = jnp.where(tile_in_seq == 0, prev_ref[0], halo_ref[...])
    xp_scr[0:SUBLANES, :] = halo
    xp_scr[SUBLANES:SUBLANES + tt, :] = x_ref[...]
    base = SUBLANES - (CONV_WIDTH - 1)
    outs = (qn_ref, kn_ref, vv_ref)
    for sec in range(3):
        for h in range(N_DELTA_HEADS):
            col = sec * DELTA_WIDTH + h * HEAD_DIM
            sl = slice(col, col + HEAD_DIM)
            y = xp_scr[base:base + tt, sl] * cw_ref[0:1, sl]
            for j in range(1, CONV_WIDTH):
                y = y + xp_scr[base + j:base + j + tt, sl] * cw_ref[j:j + 1, sl]
            y = _silu(y)
            if sec < 2:
                y = y * lax.rsqrt(jnp.sum(y * y, axis=-1, keepdims=True) + EPS)
            if sec == 0:
                y = y * (HEAD_DIM ** -0.5)
            outs[sec][:, h * HEAD_DIM:(h + 1) * HEAD_DIM] = y
    m = misc_ref[...]
    lane = lax.broadcasted_iota(I32, m.shape, 1)
    row = tile_in_seq * tt + lax.broadcasted_iota(I32, m.shape, 0)
    beta = _sigmoid(m)
    g = -jnp.exp(al_ref[...]) * _softplus(m + dt_ref[...])
    is_b = jnp.logical_and(lane >= M_BD, lane < M_BD + N_DELTA_HEADS)
    is_g = jnp.logical_and(lane >= M_AD, lane < M_AD + N_DELTA_HEADS)
    comb = jnp.where(is_b, beta, jnp.where(is_g, g, 0.0))
    comb = jnp.where(row < t_valid, comb, 0.0)
    bg_ref[...] = pltpu.roll(comb, LANES - M_BD, 1)


def _delta_prep(proj, prev8, conv_w, a_log, dt_bias, b, t, tt, t_valid):
    n = proj.shape[0]
    tiles_per_seq = t // tt
    pad_vec = lambda v: jnp.zeros((1, LANES), F32).at[0, M_AD:M_AD + N_DELTA_HEADS].set(v)
    halo_blocks = tt // SUBLANES
    return pl.pallas_call(
        functools.partial(_delta_prep_kernel, tiles_per_seq=tiles_per_seq, t_valid=t_valid, tt=tt),
        name="delta_prep",
        grid=(n // tt,),
        in_specs=[pl.BlockSpec((tt, CONV_CHANNELS), lambda i: (i, C_CONV // CONV_CHANNELS)),
                  pl.BlockSpec((SUBLANES, CONV_CHANNELS),
                               lambda i: (jnp.maximum(i * halo_blocks - 1, 0), C_CONV // CONV_CHANNELS)),
                  pl.BlockSpec((1, SUBLANES, CONV_CHANNELS), lambda i: (i // tiles_per_seq, 0, 0)),
                  pl.BlockSpec((tt, LANES), lambda i: (i, C_MISC // LANES)),
                  pl.BlockSpec((CONV_WIDTH, CONV_CHANNELS), lambda i: (0, 0)),
                  pl.BlockSpec((1, LANES), lambda i: (0, 0)),
                  pl.BlockSpec((1, LANES), lambda i: (0, 0))],
        out_specs=[pl.BlockSpec((tt, DELTA_WIDTH), lambda i: (i, 0))] * 3
                  + [pl.BlockSpec((tt, LANES), lambda i: (i, 0))],
        out_shape=[jax.ShapeDtypeStruct((n, DELTA_WIDTH), F32)] * 3 + [jax.ShapeDtypeStruct((n, LANES), F32)],
        scratch_shapes=[pltpu.VMEM((SUBLANES + tt, CONV_CHANNELS), F32)],
        compiler_params=_cparams(("parallel",)),
    )(proj, proj, prev8, proj, conv_w, pad_vec(a_log), pad_vec(dt_bias))


def _mm(a, b):
    return _dot(a.astype(BF16), b.astype(BF16))


def _mm_nt(a, b):
    return _dot_nt(a.astype(BF16), b.astype(BF16))


DELTA_INV_BLOCK = 16
DELTA_STACK = 4


def _delta_chunk_kernel(qn_ref, kn_ref, vv_ref, bg_ref, z_ref, s0_ref, ow_ref, od_ref, so_ref, s_scr):
    c = pl.program_id(1)
    cs = DELTA_CHUNK

    @pl.when(c == 0)
    def _():
        s_scr[...] = s0_ref[0]

    bg = bg_ref[...]
    ltri = (lax.broadcasted_iota(I32, (cs, cs), 0) >= lax.broadcasted_iota(I32, (cs, cs), 1)).astype(BF16)
    g1 = bg.astype(BF16)
    r1 = bg - g1.astype(F32)
    g2 = r1.astype(BF16)
    g3 = (r1 - g2.astype(F32)).astype(BF16)
    gc = _dot(ltri, g1) + _dot(ltri, g2) + _dot(ltri, g3)
    gct = gc.T

    gh = DELTA_STACK
    rows = gh * cs
    rr = lax.broadcasted_iota(I32, (rows, rows), 0)
    cc = lax.broadcasted_iota(I32, (rows, rows), 1)
    same = (rr // cs) == (cc // cs)
    causal = jnp.logical_and(same, rr >= cc)
    strict = jnp.logical_and(same, rr > cc)
    eye = (rr == cc).astype(F32)
    row_head = lax.broadcasted_iota(I32, (rows, 1), 0) // cs
    for grp in range(N_DELTA_HEADS // gh):
        heads = [grp * gh + j for j in range(gh)]
        stack = lambda ref: jnp.concatenate([ref[:, h * HEAD_DIM:(h + 1) * HEAD_DIM] for h in heads], axis=0)
        col = lambda a, lane0: jnp.concatenate([a[:, lane0 + h:lane0 + h + 1] for h in heads], axis=0)
        k = stack(kn_ref)
        q = stack(qn_ref)
        v = stack(vv_ref)
        bcol = col(bg, 0)
        gcc = col(gc, N_DELTA_HEADS)
        gcr = jnp.concatenate([gct[N_DELTA_HEADS + h:N_DELTA_HEADS + h + 1, :] for h in heads], axis=1)
        g_last = [gc[cs - 1:cs, N_DELTA_HEADS + h:N_DELTA_HEADS + h + 1] for h in heads]
        glc = jnp.concatenate([jnp.broadcast_to(gl, (cs, 1)) for gl in g_last], axis=0)
        decay = jnp.exp(jnp.where(causal, gcc - gcr, -jnp.inf))
        kb = k * bcol
        eg = jnp.exp(gcc)
        kq = _mm_nt(jnp.concatenate([kb, q], axis=0), k)
        a = jnp.where(strict, kq[0:rows] * decay, 0.0)
        intra = jnp.where(causal, kq[rows:2 * rows] * decay, 0.0)
        x = -a
        nb = DELTA_INV_BLOCK
        y = jnp.where((rr // nb) == (cc // nb), x, 0.0)
        p = eye + y
        y = _mm(y, y)
        n_sq = max(1, int(nb - 1).bit_length())
        for lvl in range(1, n_sq):
            if lvl < n_sq - 1:
                py = _mm(jnp.concatenate([p, y], axis=0), y)
                p = p + py[0:rows]
                y = py[rows:2 * rows]
            else:
                p = p + _mm(p, y)
        size = 2 * nb
        while size <= cs:
            off = jnp.where(jnp.logical_and((rr // size) == (cc // size), (rr // (size // 2)) != (cc // (size // 2))),
                            x, 0.0)
            p = p + _mm(_mm(p, off), p)
            size *= 2
        sol = _mm(p, jnp.concatenate([v * bcol, kb * eg], axis=1))
        u = sol[:, 0:HEAD_DIM]
        w = sol[:, HEAD_DIM:2 * HEAD_DIM]
        lanes_g = slice(grp * gh * HEAD_DIM, (grp + 1) * gh * HEAD_DIM)
        s_g = s_scr[:, lanes_g]
        wq_s = _mm(jnp.concatenate([w, q * eg], axis=0), s_g)
        own = lambda m, r0: jnp.concatenate(
            [m[r0 + j * cs:r0 + (j + 1) * cs, j * HEAD_DIM:(j + 1) * HEAD_DIM] for j in range(gh)], axis=0)
        v_new = u - own(wq_s, 0)
        o = own(wq_s, rows) + _mm(intra, v_new)
        kg_t = (k * jnp.exp(glc - gcc)).T
        vn_blocks = jnp.concatenate([jnp.where(row_head == j, v_new, 0.0) for j in range(gh)], axis=1)
        s_decay = jnp.concatenate([jnp.broadcast_to(jnp.exp(gl), (1, HEAD_DIM)) for gl in g_last], axis=1)
        s_scr[:, lanes_g] = s_g * s_decay + _mm(kg_t, vn_blocks)
        on = o * lax.rsqrt(jnp.mean(o * o, axis=-1, keepdims=True) + EPS) * ow_ref[...]
        for j, h in enumerate(heads):
            sl = slice(h * HEAD_DIM, (h + 1) * HEAD_DIM)
            od_ref[:, sl] = (on[j * cs:(j + 1) * cs] * _silu(z_ref[:, sl])).astype(BF16)

    so_ref[0] = s_scr[...]


def _delta_chunks(qn, kn, vv, bg, zsrc, z_col_block, state0, o_norm_w, b, t):
    n = b * t
    nc = t // DELTA_CHUNK
    sw = N_DELTA_HEADS * HEAD_DIM
    row = lambda w, cb=0: pl.BlockSpec((DELTA_CHUNK, w), lambda bb, c: (bb * nc + c, cb))
    st = pl.BlockSpec((1, HEAD_DIM, sw), lambda bb, c: (bb, 0, 0))
    s_in = state0.transpose(0, 2, 1, 3).reshape(b, HEAD_DIM, sw)
    od, s_out = pl.pallas_call(
        _delta_chunk_kernel,
        name="delta_chunks",
        grid=(b, nc),
        in_specs=[row(DELTA_WIDTH), row(DELTA_WIDTH), row(DELTA_WIDTH), row(LANES),
                  row(DELTA_WIDTH, z_col_block), st, pl.BlockSpec((1, LANES), lambda bb, c: (0, 0))],
        out_specs=[row(DELTA_WIDTH), st],
        out_shape=[jax.ShapeDtypeStruct((n, DELTA_WIDTH), BF16),
                   jax.ShapeDtypeStruct((b, HEAD_DIM, sw), F32)],
        scratch_shapes=[pltpu.VMEM((HEAD_DIM, sw), F32)],
        compiler_params=_cparams(("parallel", "arbitrary")),
    )(qn, kn, vv, bg, zsrc, s_in, o_norm_w.reshape(1, LANES))
    return od, s_out.reshape(b, HEAD_DIM, N_DELTA_HEADS, HEAD_DIM).transpose(0, 2, 1, 3)


def _outproj_kernel(oa_ref, od_ref, x_ref, g1_ref, sh_ref, sc_ref, nw_ref, wo_ref, wrh_ref, wrl_ref, br_ref,
                    x1_ref, h2_ref, lg_ref):
    mix = _dot(oa_ref[...], wo_ref[0:ATTN_WIDTH, :]) + _dot(od_ref[...], wo_ref[ATTN_WIDTH:ATTN_WIDTH + DELTA_WIDTH, :])
    x1 = x_ref[...] + g1_ref[0] * mix
    x1_ref[...] = x1
    y = x1 * lax.rsqrt(jnp.mean(x1 * x1, axis=-1, keepdims=True) + EPS) * nw_ref[...]
    h2 = y * (1.0 + sc_ref[0]) + sh_ref[0]
    h2_ref[...] = h2
    hb = h2.astype(BF16)
    lo = (h2 - hb.astype(F32)).astype(BF16)
    lg_ref[...] = _dot(hb, wrh_ref[...]) + _dot(lo, wrh_ref[...]) + _dot(hb, wrl_ref[...]) + br_ref[...]


def _out_projection(o_attn, o_delta, x2d, gate1, shift2, scale2, norm2_w, w_out_bf, wr_hi, wr_lo, b_rt,
                    tm, rows_per_mod_block):
    n, d = x2d.shape
    r = gate1.shape[1]
    tiles_per_mod = rows_per_mod_block // tm
    mod_spec = pl.BlockSpec((1, r, d), lambda i: (i // tiles_per_mod, 0, 0))
    row = lambda w: pl.BlockSpec((tm, w), lambda i: (i, 0))
    full = lambda a: pl.BlockSpec(a.shape, lambda i: (0, 0))
    return pl.pallas_call(
        _outproj_kernel,
        name="out_proj",
        grid=(n // tm,),
        in_specs=[row(ATTN_WIDTH), row(DELTA_WIDTH), row(d), mod_spec, mod_spec, mod_spec,
                  pl.BlockSpec((1, d), lambda i: (0, 0)), full(w_out_bf), full(wr_hi), full(wr_lo), full(b_rt)],
        out_specs=[row(d), row(d), row(LANES)],
        out_shape=[jax.ShapeDtypeStruct((n, d), F32), jax.ShapeDtypeStruct((n, d), F32),
                   jax.ShapeDtypeStruct((n, LANES), F32)],
        compiler_params=_cparams(("parallel",)),
    )(o_attn, o_delta, x2d, gate1, shift2, scale2, norm2_w.reshape(1, d), w_out_bf, wr_hi, wr_lo, b_rt)


def _route_kernel(lg_ref, eid_ref, gate_ref):
    x = lg_ref[...]
    lane = lax.broadcasted_iota(I32, x.shape, 1)
    gl = jnp.where(lane < N_GROUPS, x, -jnp.inf)
    ge = jnp.exp(gl - jnp.max(gl, axis=1, keepdims=True))
    p = ge / jnp.sum(ge, axis=1, keepdims=True)
    p_max = jnp.max(p, axis=1, keepdims=True)
    grp = jnp.min(jnp.where(p == p_max, lane, LANES), axis=1, keepdims=True)
    e_lane = lane - N_GROUPS
    in_grp = jnp.logical_and(jnp.logical_and(e_lane >= 0, e_lane < N_EXPERTS),
                             (e_lane >> 3) == grp)
    rl = jnp.where(in_grp, x, -jnp.inf)
    v1 = jnp.max(rl, axis=1, keepdims=True)
    i1 = jnp.min(jnp.where(rl == v1, lane, LANES), axis=1, keepdims=True)
    rl2 = jnp.where(lane == i1, -jnp.inf, rl)
    v2 = jnp.max(rl2, axis=1, keepdims=True)
    i2 = jnp.min(jnp.where(rl2 == v2, lane, LANES), axis=1, keepdims=True)
    t = jnp.exp(v2 - v1)
    den = 1.0 + t
    eid_ref[...] = jnp.where(lane == 0, i1 - N_GROUPS, jnp.where(lane == 1, i2 - N_GROUPS, 0))
    gate_ref[...] = jnp.where(lane == 0, (1.0 / den) * p_max, jnp.where(lane == 1, (t / den) * p_max, 0.0))


def _route(logits, tm):
    n = logits.shape[0]
    spec = pl.BlockSpec((tm, LANES), lambda i: (i, 0))
    return pl.pallas_call(
        _route_kernel,
        name="route",
        grid=(n // tm,),
        in_specs=[spec],
        out_specs=[spec, spec],
        out_shape=[jax.ShapeDtypeStruct((n, LANES), I32), jax.ShapeDtypeStruct((n, LANES), F32)],
        compiler_params=_cparams(("parallel",)),
    )(logits)


def _row_gather(idx_ref, base, n_rows, src_hbm, dst, sem):
    def body(r, carry):
        pltpu.make_async_copy(src_hbm.at[pl.ds(idx_ref[base + r], 1), :], dst.at[pl.ds(r, 1), :], sem).start()
        return carry

    lax.fori_loop(0, n_rows, body, 0, unroll=8)


def _row_gather_wait(n_rows, src_hbm, dst, sem):
    pltpu.make_async_copy(src_hbm.at[pl.ds(0, n_rows), :], dst, sem).wait()


def _moe_kernel(tok_ref, j0_ref, be_ref, na_ref, h_hbm, wg_ref, wu_ref, wd_ref, o_ref,
                x_buf, sem, wg_scr, wu_scr, wd_scr, *, bm):
    i = pl.program_id(0)
    n_act = na_ref[0]

    @pl.when(jnp.logical_and(i == 0, n_act > 0))
    def _():
        _row_gather(tok_ref, j0_ref[0], bm, h_hbm, x_buf.at[0], sem.at[0])

    @pl.when(i < n_act)
    def _():
        slot = i % 2

        @pl.when(i + 1 < n_act)
        def _():
            _row_gather(tok_ref, j0_ref[i + 1], bm, h_hbm, x_buf.at[1 - slot], sem.at[1 - slot])

        changed = jnp.logical_or(i == 0, be_ref[i] != be_ref[jnp.maximum(i - 1, 0)])

        @pl.when(changed)
        def _():
            wg_scr[...] = wg_ref[0].astype(BF16)
            wu_scr[...] = wu_ref[0].astype(BF16)
            wd_scr[...] = wd_ref[0].astype(BF16)

        _row_gather_wait(bm, h_hbm, x_buf.at[slot], sem.at[slot])
        x = x_buf[slot].astype(BF16)
        hid = _silu(_dot(x, wg_scr[...])) * _dot(x, wu_scr[...])
        o_ref[...] = _dot(hid.astype(BF16), wd_scr[...])

    @pl.when(i >= n_act)
    def _():
        o_ref[...] = jnp.zeros(o_ref.shape, F32)


def _moe_experts(tok_sorted, block_j0, block_exp, n_active, h2, w_gate, w_up, w_down, bm):
    ns = block_exp.shape[0] * bm
    d = h2.shape[1]
    f = w_gate.shape[2]
    grid_spec = pltpu.PrefetchScalarGridSpec(
        num_scalar_prefetch=4,
        grid=(ns // bm,),
        in_specs=[pl.BlockSpec(memory_space=pl.ANY),
                  pl.BlockSpec((1, d, f), lambda i, tok, j0, be, na: (be[i], 0, 0)),
                  pl.BlockSpec((1, d, f), lambda i, tok, j0, be, na: (be[i], 0, 0)),
                  pl.BlockSpec((1, f, d), lambda i, tok, j0, be, na: (be[i], 0, 0))],
        out_specs=pl.BlockSpec((bm, d), lambda i, tok, j0, be, na: (i, 0)),
        scratch_shapes=[pltpu.VMEM((2, bm, d), F32), pltpu.SemaphoreType.DMA((2,)),
                        pltpu.VMEM((d, f), BF16), pltpu.VMEM((d, f), BF16), pltpu.VMEM((f, d), BF16)],
    )
    return pl.pallas_call(
        functools.partial(_moe_kernel, bm=bm),
        name="moe_experts",
        grid_spec=grid_spec,
        out_shape=jax.ShapeDtypeStruct((ns, d), F32),
        compiler_params=_cparams(("arbitrary",)),
    )(tok_sorted, block_j0, block_exp, n_active, h2, w_gate, w_up, w_down)


def _combine_kernel(dest_ref, x1_ref, gt_ref, g2_ref, y_hbm, o_ref, y_buf, sem, *, tm):
    i = pl.program_id(0)
    n = pl.num_programs(0)
    slot = i % 2

    @pl.when(i == 0)
    def _():
        _row_gather(dest_ref, 0, 2 * tm, y_hbm, y_buf.at[0], sem.at[0])

    @pl.when(i + 1 < n)
    def _():
        _row_gather(dest_ref, (i + 1) * 2 * tm, 2 * tm, y_hbm, y_buf.at[1 - slot], sem.at[1 - slot])

    _row_gather_wait(2 * tm, y_hbm, y_buf.at[slot], sem.at[slot])
    gt = gt_ref[...]
    y = y_buf[slot, 0:tm, :] * gt[:, 0:1] + y_buf[slot, tm:2 * tm, :] * gt[:, 1:2]
    o_ref[...] = x1_ref[...] + g2_ref[0] * y


def _combine(x1, dest, y_rows, gates, gate2, tm, rows_per_mod_block):
    n, d = x1.shape
    r = gate2.shape[1]
    tiles_per_mod = rows_per_mod_block // tm
    dest_tiles = dest.reshape(n // tm, tm, 2).transpose(0, 2, 1).reshape(-1)
    row = lambda w: pl.BlockSpec((tm, w), lambda i, dst: (i, 0))
    grid_spec = pltpu.PrefetchScalarGridSpec(
        num_scalar_prefetch=1,
        grid=(n // tm,),
        in_specs=[row(d), row(LANES),
                  pl.BlockSpec((1, r, d), lambda i, dst: (i // tiles_per_mod, 0, 0)),
                  pl.BlockSpec(memory_space=pl.ANY)],
        out_specs=row(d),
        scratch_shapes=[pltpu.VMEM((2, 2 * tm, d), F32), pltpu.SemaphoreType.DMA((2,))],
    )
    return pl.pallas_call(
        functools.partial(_combine_kernel, tm=tm),
        name="moe_combine",
        grid_spec=grid_spec,
        out_shape=jax.ShapeDtypeStruct((n, d), F32),
        compiler_params=_cparams(("arbitrary",)),
    )(dest_tiles, x1, gates, gate2, y_rows)


def _pick_tile(n, pref, mult=16):
    t = min(pref, n)
    while n % t or t % mult:
        t -= 1
    return t


def _pack_w_in(w_in):
    d = w_in.shape[0]
    bounds = np.cumsum(PROJ_SIZES)[:-1].tolist()
    qa, ka, va, qi, ki, wi, qd, kd, vd, zd, bd, ad = jnp.split(w_in, bounds, axis=1)
    used = IDX_DIM + N_IDX_HEADS + 2 * N_DELTA_HEADS
    misc = jnp.concatenate([ki, wi, bd, ad, jnp.zeros((d, LANES - used), w_in.dtype)], axis=1)
    cols = [qa, qi, zd, qd, kd, vd, ka, va, misc]
    width = sum(c.shape[1] for c in cols)
    cols.append(jnp.zeros((d, PROJ_PACKED - width), w_in.dtype))
    return jnp.concatenate(cols, axis=1).astype(BF16)


def _route_and_sort(eid, bm):
    n = eid.shape[0]
    nk = 2 * n
    flat_e = eid.reshape(-1)
    order = jnp.argsort(flat_e, stable=True).astype(I32)
    inv = jnp.argsort(order).astype(I32)
    onehot = flat_e[:, None] == jnp.arange(N_EXPERTS, dtype=I32)[None, :]
    counts = jnp.sum(onehot.astype(I32), axis=0)
    padded = (counts + bm - 1) // bm * bm
    pad_end = jnp.cumsum(padded)
    shift = (pad_end - padded) - (jnp.cumsum(counts) - counts)
    dest = inv + jnp.sum(jnp.where(onehot, shift[None, :], 0), axis=1)
    n_blocks = -(-nk // bm) + N_EXPERTS
    block_exp = jnp.minimum(jnp.searchsorted(pad_end, jnp.arange(n_blocks, dtype=I32) * bm, side='right'),
                            N_EXPERTS - 1).astype(I32)
    n_active = (pad_end[-1] // bm).astype(I32).reshape(1)
    block_j0 = jnp.clip(jnp.arange(n_blocks, dtype=I32) * bm - shift[block_exp], 0, nk)
    tok_sorted = jnp.concatenate([order // 2, jnp.zeros((bm,), I32)])
    return tok_sorted, block_j0, dest.astype(I32).reshape(n, 2), block_exp, n_active


def _layer(layer, yp, ys, cache_k, cache_v, cache_idx, state_ssm, state_conv, page_table, c_prompt, c_sample,
           w_in, w_out, conv_w, a_log, dt_bias, q_norm_w, k_norm_w, idx_k_norm_w, o_norm_w, norm1_w, norm2_w,
           w_ada, b_ada, w_group, b_group, w_router, b_router, w_gate, w_up, w_down):
    bp, tp, d = yp.shape
    bs, ts, _ = ys.shape
    past = page_table.shape[1] * PAGE_SIZE
    rows = SAMPLE_ROWS
    assert CONV_WIDTH - 1 <= ts <= rows and tp % KEY_CHUNK == 0 and tp % DELTA_CHUNK == 0

    n_c = bp + bs
    n_c_pad = -(-n_c // SUBLANES) * SUBLANES
    c_all = jnp.concatenate([c_prompt, c_sample, jnp.zeros((n_c_pad - n_c, d), F32)], axis=0)
    mod = _ada_modulation(c_all, w_ada, b_ada)
    mods = jnp.split(mod, N_MOD, axis=1)
    mp = [m[:bp].reshape(bp, 1, d) for m in mods]
    ms = [jnp.repeat(m[bp:bp + bs], rows, axis=0).reshape(1, bs * rows, d) for m in mods]

    w_packed = _pack_w_in(w_in)
    w_out_bf = w_out.astype(BF16)
    w_rt = jnp.concatenate([w_group, w_router, jnp.zeros((d, LANES - N_GROUPS - N_EXPERTS), F32)], axis=1)
    wr_hi = w_rt.astype(BF16)
    wr_lo = (w_rt - wr_hi.astype(F32)).astype(BF16)
    b_rt = jnp.concatenate([b_group, b_router, jnp.zeros((LANES - N_GROUPS - N_EXPERTS,), F32)]).reshape(1, LANES)

    np_ = bp * tp
    xp2 = yp.reshape(np_, d)
    tm_p = _pick_tile(tp, 512)
    proj_p = _in_projection(xp2, mp[0], mp[1], norm1_w, w_packed, tm_p, tp)
    tq_p = _pick_tile(tp, 256)
    q_p, kf_p, kb_p, vt_p, qi_p, kif_p, kx_p = _attention_prep(
        proj_p, jnp.arange(tp), tq_p, q_norm_w, k_norm_w, idx_k_norm_w, True)
    oa_p = _dsa_prompt(q_p, qi_p, proj_p, kb_p, vt_p, kx_p, bp, tp)
    tt_p = _pick_tile(tp, 256)
    qn_p, kn_p, vv_p, bg_p = _delta_prep(proj_p, jnp.zeros((bp, SUBLANES, CONV_CHANNELS), F32), conv_w,
                                         a_log, dt_bias, bp, tp, tt_p, tp)
    od_p, ssm_p = _delta_chunks(qn_p, kn_p, vv_p, bg_p, proj_p, C_ZD // DELTA_WIDTH,
                                jnp.zeros((bp, N_DELTA_HEADS, HEAD_DIM, HEAD_DIM), F32), o_norm_w, bp, tp)
    tm_o = _pick_tile(tp, 256)
    x1_p, h2_p, lg_p = _out_projection(oa_p, od_p, xp2, mp[2], mp[3], mp[4], norm2_w, w_out_bf, wr_hi, wr_lo, b_rt,
                                       tm_o, tp)

    ns_ = bs * rows
    xs2 = jnp.pad(ys, ((0, 0), (0, rows - ts), (0, 0))).reshape(ns_, d)
    proj_s = _in_projection(xs2, ms[0], ms[1], norm1_w, w_packed, ns_, ns_)
    q_s, kf_s, kb_s, vb_s, qi_s, kif_s, _ = _attention_prep(
        proj_s, past + jnp.arange(rows), rows, q_norm_w, k_norm_w, idx_k_norm_w, False)
    q_t = qi_s.reshape(bs, rows, N_IDX_HEADS, IDX_DIM).transpose(0, 2, 1, 3).reshape(bs, N_IDX_HEADS * rows, IDX_DIM)
    w_col = proj_s[:, C_MISC + M_WI:C_MISC + M_WI + N_IDX_HEADS].reshape(bs, rows, N_IDX_HEADS)
    w_col = w_col.transpose(0, 2, 1).reshape(bs, N_IDX_HEADS * rows, 1)
    pages = _pick_tile(page_table.shape[1], 8, 1)
    keys_past, keys_new = _sample_scores(page_table, q_t, w_col, kif_s, cache_idx, layer, pages, ts)
    n_sel_s = min(TOPK_MAX, (past + ts) // 4)
    oa_s = _sample_attend(page_table, keys_past, keys_new, q_s, kb_s, vb_s, cache_k, cache_v, layer, pages, n_sel_s)
    prev8 = jnp.pad(state_conv, ((0, 0), (SUBLANES - (CONV_WIDTH - 1), 0), (0, 0)))
    qn_s, kn_s, vv_s, bg_s = _delta_prep(proj_s, prev8, conv_w, a_log, dt_bias, bs, rows, rows, ts)
    to_chunk = lambda a: jnp.pad(a.reshape(bs, rows, -1), ((0, 0), (0, DELTA_CHUNK - rows), (0, 0))).reshape(
        bs * DELTA_CHUNK, -1)
    z_s = proj_s[:, C_ZD:C_ZD + DELTA_WIDTH]
    od_s, ssm_s = _delta_chunks(to_chunk(qn_s), to_chunk(kn_s), to_chunk(vv_s), to_chunk(bg_s), to_chunk(z_s), 0,
                                state_ssm, o_norm_w, bs, DELTA_CHUNK)
    od_s = od_s.reshape(bs, DELTA_CHUNK, DELTA_WIDTH)[:, :rows].reshape(ns_, DELTA_WIDTH)
    x1_s, h2_s, lg_s = _out_projection(oa_s, od_s, xs2, ms[2], ms[3], ms[4], norm2_w, w_out_bf, wr_hi, wr_lo, b_rt,
                                       ns_, ns_)

    n_all = np_ + ns_
    h2_all = jnp.concatenate([h2_p, h2_s], axis=0)
    lg_all = jnp.concatenate([lg_p, lg_s], axis=0)
    eid, gates = _route(lg_all, _pick_tile(n_all, 512, SUBLANES))
    bm = 256
    tok_sorted, block_j0, dest, block_exp, n_active = _route_and_sort(eid[:, 0:2], bm)
    yb = _moe_experts(tok_sorted, block_j0, block_exp, n_active, h2_all, w_gate, w_up, w_down, bm)
    out_p = _combine(x1_p, dest[:np_], yb, gates[:np_], mp[5], tm_o, tp)
    out_s = _combine(x1_s, dest[np_:], yb, gates[np_:], ms[5], ns_, ns_)

    valid = lambda a: a.reshape(bs, rows, -1)[:, :ts]
    conv_p = proj_p.reshape(bp, tp, PROJ_PACKED)[:, tp - (CONV_WIDTH - 1):, C_CONV:C_CONV + CONV_CHANNELS]
    conv_s = proj_s.reshape(bs, rows, PROJ_PACKED)[:, ts - (CONV_WIDTH - 1):ts, C_CONV:C_CONV + CONV_CHANNELS]
    return (out_p.reshape(bp, tp, d), valid(out_s),
            kf_p.reshape(bp, tp, N_KV_HEADS, HEAD_DIM),
            proj_p[:, C_VA:C_VA + KV_WIDTH].reshape(bp, tp, N_KV_HEADS, HEAD_DIM),
            kif_p.reshape(bp, tp, IDX_DIM), ssm_p, conv_p,
            valid(kf_s).reshape(bs, ts, N_KV_HEADS, HEAD_DIM),
            valid(proj_s[:, C_VA:C_VA + KV_WIDTH]).reshape(bs, ts, N_KV_HEADS, HEAD_DIM),
            valid(kif_s), ssm_s, conv_s)


def kernel(x_prompt, x_sample, cache_k, cache_v, cache_idx_k, state_ssm, state_conv, page_table, c_prompt, c_sample,
           w_in, w_out, conv_w, a_log, dt_bias, q_norm_w, k_norm_w, idx_k_norm_w, o_norm_w, norm1_w, norm2_w,
           w_ada, b_ada, w_group, b_group, w_router, b_router, w_gate, w_up, w_down):
    depth = w_in.shape[0]
    yp, ys = x_prompt, x_sample
    per_layer = []
    for l in range(depth):
        res = _layer(l, yp, ys, cache_k, cache_v, cache_idx_k, state_ssm[l], state_conv[l], page_table,
                     c_prompt, c_sample, w_in[l], w_out[l], conv_w[l], a_log[l], dt_bias[l], q_norm_w[l],
                     k_norm_w[l], idx_k_norm_w[l], o_norm_w[l], norm1_w[l], norm2_w[l], w_ada[l], b_ada[l],
                     w_group[l], b_group[l], w_router[l], b_router[l], w_gate[l], w_up[l], w_down[l])
        yp, ys = res[0], res[1]
        per_layer.append(res[2:])
    stacked = tuple(jnp.stack([pl_[j] for pl_ in per_layer]) for j in range(10))
    return (yp, ys) + stacked
```

```python
import functools

import jax
import jax.numpy as jnp
import numpy as np
from jax import lax
from jax.experimental import pallas as pl
from jax.experimental.pallas import tpu as pltpu

F32 = jnp.float32
BF16 = jnp.bfloat16
I32 = jnp.int32

HEAD_DIM = 128
N_ATTN_HEADS = 8
N_KV_HEADS = 2
KV_GROUP = N_ATTN_HEADS // N_KV_HEADS
N_DELTA_HEADS = 8
N_IDX_HEADS = 16
IDX_DIM = 64
ATTN_WIDTH = N_ATTN_HEADS * HEAD_DIM
KV_WIDTH = N_KV_HEADS * HEAD_DIM
DELTA_WIDTH = N_DELTA_HEADS * HEAD_DIM
IDX_WIDTH = N_IDX_HEADS * IDX_DIM
CONV_CHANNELS = 3 * DELTA_WIDTH
TOPK_MAX = 256
ROPE_THETA = 500000.0
ROPE_FRACTION = 4
CONV_WIDTH = 4
DELTA_CHUNK = 64
N_GROUPS = 8
EXPERTS_PER_GROUP = 8
N_EXPERTS = N_GROUPS * EXPERTS_PER_GROUP
N_MOD = 6
EPS = 1e-6
PAGE_SIZE = 128
PROJ_SIZES = (ATTN_WIDTH, KV_WIDTH, KV_WIDTH, IDX_WIDTH, IDX_DIM, N_IDX_HEADS,
              DELTA_WIDTH, DELTA_WIDTH, DELTA_WIDTH, DELTA_WIDTH, N_DELTA_HEADS, N_DELTA_HEADS)

LANES = 128
SUBLANES = 8
VMEM_LIMIT = 56 * 1024 * 1024

C_QA = 0
C_QI = 1024
C_ZD = 2048
C_CONV = 3072
C_KA = 6144
C_VA = 6400
C_MISC = 6656
PROJ_PACKED = 6912
M_KI = 0
M_WI = 64
M_BD = 80
M_AD = 88

Q_TILE = 128
KEY_CHUNK = 256
SEL_SPAN = 512
NEG_INF_KEY = -2139095041
SAMPLE_ROWS = 16
NEG_BIG = -1e30
INT_MIN = -2147483648
INT_MAX = 2147483647


def _cparams(sem):
    return pltpu.CompilerParams(dimension_semantics=sem, vmem_limit_bytes=VMEM_LIMIT)


def _dot(a, b):
    return jnp.dot(a, b, preferred_element_type=F32)


def _dot_nt(a, b):
    return lax.dot_general(a, b, (((1,), (1,)), ((), ())), preferred_element_type=F32)


def _dot_tn(a, b):
    return lax.dot_general(a, b, (((0,), (0,)), ((), ())), preferred_element_type=F32)


def _sigmoid(x):
    return 1.0 / (1.0 + jnp.exp(-x))


def _silu(x):
    return x * _sigmoid(x)


def _softplus(x):
    return jnp.maximum(x, 0.0) + jnp.log(1.0 + jnp.exp(-jnp.abs(x)))


def _ada_kernel(c_ref, w_ref, b_ref, o_ref):
    s = _silu(c_ref[...]).astype(BF16)
    o_ref[...] = _dot(s, w_ref[...].astype(BF16)) + b_ref[...]


def _ada_modulation(c, w_ada, b_ada):
    r, d = c.shape
    n = w_ada.shape[1]
    tn = 1024 if n % 1024 == 0 else n
    return pl.pallas_call(
        _ada_kernel,
        name="ada_mod",
        grid=(n // tn,),
        in_specs=[pl.BlockSpec((r, d), lambda j: (0, 0)),
                  pl.BlockSpec((d, tn), lambda j: (0, j)),
                  pl.BlockSpec((1, tn), lambda j: (0, j))],
        out_specs=pl.BlockSpec((r, tn), lambda j: (0, j)),
        out_shape=jax.ShapeDtypeStruct((r, n), F32),
        compiler_params=_cparams(("parallel",)),
    )(c, w_ada, b_ada.reshape(1, n))


INPROJ_COLS = 1152


def _inproj_kernel(x_ref, sh_ref, sc_ref, nw_ref, w_ref, o_ref):
    x = x_ref[...]
    y = x * lax.rsqrt(jnp.mean(x * x, axis=-1, keepdims=True) + EPS) * nw_ref[...]
    h = (y * (1.0 + sc_ref[0]) + sh_ref[0]).astype(BF16)
    for c0 in range(0, o_ref.shape[1], INPROJ_COLS):
        o_ref[:, c0:c0 + INPROJ_COLS] = _dot(h, w_ref[:, c0:c0 + INPROJ_COLS])


def _in_projection(x2d, shift, scale, norm_w, w_packed, tm, rows_per_mod_block):
    n, d = x2d.shape
    np_ = w_packed.shape[1]
    r = shift.shape[1]
    tiles_per_mod = rows_per_mod_block // tm
    mod_spec = pl.BlockSpec((1, r, d), lambda i: (i // tiles_per_mod, 0, 0))
    return pl.pallas_call(
        _inproj_kernel,
        name="in_proj",
        grid=(n // tm,),
        in_specs=[pl.BlockSpec((tm, d), lambda i: (i, 0)),
                  mod_spec, mod_spec,
                  pl.BlockSpec((1, d), lambda i: (0, 0)),
                  pl.BlockSpec((d, np_), lambda i: (0, 0), pipeline_mode=pl.Buffered(1))],
        out_specs=pl.BlockSpec((tm, np_), lambda i: (i, 0)),
        out_shape=jax.ShapeDtypeStruct((n, np_), F32),
        compiler_params=_cparams(("parallel",)),
    )(x2d, shift, scale, norm_w.reshape(1, d), w_packed)


def _rope(x, tab, rot):
    c = tab[:, 0:LANES]
    s1 = tab[:, LANES:2 * LANES]
    s2 = tab[:, 2 * LANES:3 * LANES]
    return x * c + pltpu.roll(x, LANES - rot, 1) * s1 + pltpu.roll(x, rot, 1) * s2


def _rms_head(x, w):
    return x * lax.rsqrt(jnp.mean(x * x, axis=-1, keepdims=True) + EPS) * w


def _prep_kernel(qa_ref, qi_ref, ka_ref, va_ref, misc_ref, tabm_ref, tabi_ref, qw_ref, kw_ref, iw_ref,
                 q_ref, kf_ref, kb_ref, vb_ref, qib_ref, kif_ref, kib_ref, *, transpose_v):
    tabm = tabm_ref[0]
    tabi = tabi_ref[0]
    half_main = HEAD_DIM // ROPE_FRACTION // 2
    half_idx = IDX_DIM // ROPE_FRACTION // 2
    for h in range(N_ATTN_HEADS):
        sl = slice(h * HEAD_DIM, (h + 1) * HEAD_DIM)
        y = _rope(_rms_head(qa_ref[:, sl], qw_ref[...]), tabm, half_main)
        q_ref[:, sl] = (y * (HEAD_DIM ** -0.5)).astype(BF16)
    for h in range(N_KV_HEADS):
        sl = slice(h * HEAD_DIM, (h + 1) * HEAD_DIM)
        y = _rope(_rms_head(ka_ref[:, sl], kw_ref[...]), tabm, half_main)
        kf_ref[:, sl] = y
        kb_ref[:, sl] = y.astype(BF16)
    if transpose_v:
        vb_ref[...] = va_ref[...].T.astype(BF16)
    else:
        vb_ref[...] = va_ref[...].astype(BF16)
    for p in range(IDX_WIDTH // LANES):
        sl = slice(p * LANES, (p + 1) * LANES)
        qib_ref[:, sl] = _rope(qi_ref[:, sl], tabi, half_idx).astype(BF16)
    m = misc_ref[...]
    lane = lax.broadcasted_iota(I32, m.shape, 1)
    ki = jnp.where(lane < IDX_DIM, m, 0.0)
    ms = jnp.sum(ki * ki, axis=-1, keepdims=True) * (1.0 / IDX_DIM)
    y = _rope(ki * lax.rsqrt(ms + EPS) * iw_ref[...], tabi, half_idx)
    kif_ref[...] = y[:, 0:IDX_DIM]
    kib_ref[...] = (y + pltpu.roll(y, IDX_DIM, 1)).astype(BF16)


def _rope_tables(pos, head_dim, group):
    d_rot = head_dim // ROPE_FRACTION
    half = d_rot // 2
    inv_freq = jnp.power(ROPE_THETA, -(jnp.arange(half, dtype=F32) * 2.0 / d_rot))
    ang = pos.astype(F32)[:, None] * inv_freq[None, :]
    cos = jnp.cos(ang)
    sin = jnp.sin(ang)
    t = pos.shape[0]
    z = jnp.zeros((t, group - d_rot), F32)
    c = jnp.concatenate([cos, cos, jnp.ones((t, group - d_rot), F32)], axis=1)
    s1 = jnp.concatenate([-sin, jnp.zeros((t, half), F32), z], axis=1)
    s2 = jnp.concatenate([jnp.zeros((t, half), F32), sin, z], axis=1)
    rep = LANES // group
    return jnp.concatenate([jnp.tile(c, (1, rep)), jnp.tile(s1, (1, rep)), jnp.tile(s2, (1, rep))], axis=1)


def _attention_prep(proj, pos, tq, q_norm_w, k_norm_w, idx_k_norm_w, transpose_v):
    n = proj.shape[0]
    p = pos.shape[0]
    g = p // tq
    tabm = _rope_tables(pos, HEAD_DIM, LANES).reshape(g, tq, 3 * LANES)
    tabi = _rope_tables(pos, IDX_DIM, IDX_DIM).reshape(g, tq, 3 * LANES)
    iw = jnp.concatenate([idx_k_norm_w, jnp.zeros((LANES - IDX_DIM,), F32)]).reshape(1, LANES)
    row = lambda w, c: pl.BlockSpec((tq, w), lambda i: (i, c // w))
    tab_spec = pl.BlockSpec((1, tq, 3 * LANES), lambda i: (i % g, 0, 0))
    vec_spec = pl.BlockSpec((1, LANES), lambda i: (0, 0))
    out_row = lambda w: pl.BlockSpec((tq, w), lambda i: (i, 0))
    v_spec = pl.BlockSpec((KV_WIDTH, tq), lambda i: (0, i)) if transpose_v else out_row(KV_WIDTH)
    v_shape = (KV_WIDTH, n) if transpose_v else (n, KV_WIDTH)
    return pl.pallas_call(
        functools.partial(_prep_kernel, transpose_v=transpose_v),
        name="attn_prep",
        grid=(n // tq,),
        in_specs=[row(ATTN_WIDTH, C_QA), row(IDX_WIDTH, C_QI), row(KV_WIDTH, C_KA), row(KV_WIDTH, C_VA),
                  row(LANES, C_MISC), tab_spec, tab_spec, vec_spec, vec_spec, vec_spec],
        out_specs=[out_row(ATTN_WIDTH), out_row(KV_WIDTH), out_row(KV_WIDTH), v_spec,
                   out_row(IDX_WIDTH), out_row(IDX_DIM), out_row(LANES)],
        out_shape=[jax.ShapeDtypeStruct((n, ATTN_WIDTH), BF16),
                   jax.ShapeDtypeStruct((n, KV_WIDTH), F32),
                   jax.ShapeDtypeStruct((n, KV_WIDTH), BF16),
                   jax.ShapeDtypeStruct(v_shape, BF16),
                   jax.ShapeDtypeStruct((n, IDX_WIDTH), BF16),
                   jax.ShapeDtypeStruct((n, IDX_DIM), F32),
                   jax.ShapeDtypeStruct((n, LANES), BF16)],
        compiler_params=_cparams(("parallel",)),
    )(proj, proj, proj, proj, proj, tabm, tabi,
      q_norm_w.reshape(1, LANES), k_norm_w.reshape(1, LANES), iw)


def _sort_key(x):
    b = pltpu.bitcast(x + 0.0, I32)
    return b ^ ((b >> 31) & INT_MAX)


def _kth_largest_key(count_ge, k, shape):
    def body(it, ans_u):
        bit = jnp.left_shift(jnp.int32(1), 31 - it)
        cand_u = ans_u | bit
        cnt = count_ge(cand_u ^ INT_MIN)
        return jnp.where(cnt >= k, cand_u, ans_u)

    ans_u = lax.fori_loop(0, 32, body, jnp.zeros(shape, I32))
    return ans_u ^ INT_MIN


def _tie_index_limit(count_eq_le, need, n_keys, shape):
    nbits = max(1, int(n_keys - 1).bit_length())

    def body(it, lo):
        bit = jnp.left_shift(jnp.int32(1), nbits - 1 - it)
        cand = lo | bit
        cnt = count_eq_le(cand - 1)
        return jnp.where(cnt >= need, lo, cand)

    return lax.fori_loop(0, nbits, body, jnp.zeros(shape, I32))


def _dsa_prompt_kernel(q_ref, qi_ref, misc_ref, k_ref, vt_ref, kx_ref, o_ref,
                       key_scr, qsel_scr, qg_scr, thr_scr, lim_scr, m_scr, l_scr, acc_scr, *, n_sel):
    i = pl.program_id(1)
    tq = Q_TILE
    ck = KEY_CHUNK
    n_ch = (i * tq + tq + ck - 1) // ck
    q_pos = i * tq + lax.broadcasted_iota(I32, (1, tq), 1)
    row_k = lax.broadcasted_iota(I32, (ck, 1), 0)

    lo_half = lax.broadcasted_iota(I32, (tq, LANES), 1) < IDX_DIM
    zero = jnp.zeros((), BF16)
    for p in range(IDX_WIDTH // LANES):
        slab = qi_ref[:, p * LANES:(p + 1) * LANES]
        qsel_scr[(2 * p) * tq:(2 * p + 1) * tq, :] = jnp.where(lo_half, slab, zero)
        qsel_scr[(2 * p + 1) * tq:(2 * p + 2) * tq, :] = jnp.where(lo_half, zero, slab)
    w_t = misc_ref[...].T

    def score_chunk(c, carry):
        off = pl.multiple_of(c * ck, ck)
        s = _dot_nt(kx_ref[pl.ds(off, ck), :], qsel_scr[...])
        acc = jnp.zeros((ck, tq), F32)
        for h in range(N_IDX_HEADS):
            acc = acc + w_t[M_WI + h:M_WI + h + 1, :] * jnp.maximum(s[:, h * tq:(h + 1) * tq], 0.0)
        acc = jnp.where(off + row_k <= q_pos, acc, -jnp.inf)
        key_scr[pl.ds(off, ck), :] = _sort_key(acc)
        return carry

    lax.fori_loop(0, n_ch, score_chunk, 0)

    spc = SEL_SPAN // ck
    n_span = (n_ch + spc - 1) // spc
    neg_key = jnp.full((ck, tq), NEG_INF_KEY, I32)

    def pad_chunk(c, carry):
        key_scr[pl.ds(pl.multiple_of(c * ck, ck), ck), :] = neg_key
        return carry

    lax.fori_loop(n_ch, n_span * spc, pad_chunk, 0)

    thr_scr[...] = jnp.full((1, tq), INT_MIN, I32)
    lim_scr[...] = jnp.full((1, tq), INT_MAX, I32)

    def select_threshold(n_keys):
        def count_where(pred):
            tot = jnp.zeros((SUBLANES, tq), F32)
            for c0 in range(0, n_keys, ck):
                hit = pred(key_scr[c0:c0 + ck, :], c0 + row_k).astype(F32)
                tot = tot + jnp.sum(hit.reshape(ck // SUBLANES, SUBLANES, tq), axis=0)
            return jnp.sum(tot, axis=0, keepdims=True)

        t = _kth_largest_key(lambda cand: count_where(lambda kk, pos: kk >= cand), float(n_sel), (1, tq))
        thr_scr[...] = t
        n_gt = count_where(lambda kk, pos: kk > t)
        n_ge = count_where(lambda kk, pos: kk >= t)

        @pl.when(jnp.max(n_ge) > float(n_sel))
        def _():
            lim_scr[...] = _tie_index_limit(
                lambda idx: count_where(lambda kk, pos: jnp.logical_and(kk == t, pos <= idx)),
                float(n_sel) - n_gt, k_ref.shape[0], (1, tq))

    for spans in range(1, k_ref.shape[0] // SEL_SPAN + 1):
        if spans * SEL_SPAN > n_sel:
            pl.when(jnp.logical_and(n_span == spans, (i + 1) * tq > n_sel))(
                functools.partial(select_threshold, spans * SEL_SPAN))

    thr = thr_scr[...]
    lim = lim_scr[...]

    for g in range(N_KV_HEADS):
        for r in range(KV_GROUP):
            h = g * KV_GROUP + r
            qg_scr[g, r * tq:(r + 1) * tq, :] = q_ref[:, h * HEAD_DIM:(h + 1) * HEAD_DIM]
    m_scr[...] = jnp.full(m_scr.shape, NEG_BIG, F32)
    l_scr[...] = jnp.zeros(l_scr.shape, F32)
    acc_scr[...] = jnp.zeros(acc_scr.shape, F32)

    def attend_chunk(c, carry):
        off = pl.multiple_of(c * ck, ck)
        kk = key_scr[pl.ds(off, ck), :]
        pos = off + row_k
        sel = jnp.logical_or(kk > thr, jnp.logical_and(kk == thr, pos <= lim))
        sel = jnp.logical_and(sel, pos <= q_pos)
        for g in range(N_KV_HEADS):
            kc = k_ref[pl.ds(off, ck), g * HEAD_DIM:(g + 1) * HEAD_DIM]
            vt = vt_ref[g * HEAD_DIM:(g + 1) * HEAD_DIM, pl.ds(off, ck)]
            qk = _dot_nt(kc, qg_scr[g])
            s = jnp.concatenate([jnp.where(sel, qk[:, r * tq:(r + 1) * tq], NEG_BIG) for r in range(KV_GROUP)],
                                axis=1)
            m_old = m_scr[g]
            m_new = jnp.maximum(m_old, jnp.max(s, axis=0, keepdims=True))
            p = jnp.exp(s - m_new)
            alpha = jnp.exp(m_old - m_new)
            l_scr[g] = alpha * l_scr[g] + jnp.sum(p, axis=0, keepdims=True)
            acc_scr[g] = alpha * acc_scr[g] + _dot(vt, p.astype(BF16))
            m_scr[g] = m_new
        return carry

    lax.fori_loop(0, n_ch, attend_chunk, 0)
    for g in range(N_KV_HEADS):
        o_t = acc_scr[g] / l_scr[g]
        for r in range(KV_GROUP):
            h = g * KV_GROUP + r
            o_ref[:, h * HEAD_DIM:(h + 1) * HEAD_DIM] = o_t[:, r * tq:(r + 1) * tq].T.astype(BF16)


def _dsa_prompt(q_bf, qi_bf, proj, k_bf, vt_bf, kx_bf, b, t):
    n = b * t
    nq = t // Q_TILE
    n_sel = min(TOPK_MAX, t // 4)
    qrow = lambda w: pl.BlockSpec((Q_TILE, w), lambda bb, i: (bb * nq + i, 0))
    seq = lambda w: pl.BlockSpec((t, w), lambda bb, i: (bb, 0))
    return pl.pallas_call(
        functools.partial(_dsa_prompt_kernel, n_sel=n_sel),
        name="dsa_prompt",
        grid=(b, nq),
        in_specs=[qrow(ATTN_WIDTH), qrow(IDX_WIDTH),
                  pl.BlockSpec((Q_TILE, LANES), lambda bb, i: (bb * nq + i, C_MISC // LANES)),
                  seq(KV_WIDTH), pl.BlockSpec((KV_WIDTH, t), lambda bb, i: (0, bb)), seq(LANES)],
        out_specs=qrow(ATTN_WIDTH),
        out_shape=jax.ShapeDtypeStruct((n, ATTN_WIDTH), BF16),
        scratch_shapes=[pltpu.VMEM((t, Q_TILE), I32),
                        pltpu.VMEM((N_IDX_HEADS * Q_TILE, LANES), BF16),
                        pltpu.VMEM((N_KV_HEADS, KV_GROUP * Q_TILE, HEAD_DIM), BF16),
                        pltpu.VMEM((1, Q_TILE), I32),
                        pltpu.VMEM((1, Q_TILE), I32),
                        pltpu.VMEM((N_KV_HEADS, 1, KV_GROUP * Q_TILE), F32),
                        pltpu.VMEM((N_KV_HEADS, 1, KV_GROUP * Q_TILE), F32),
                        pltpu.VMEM((N_KV_HEADS, HEAD_DIM, KV_GROUP * Q_TILE), F32)],
        compiler_params=_cparams(("parallel", "arbitrary")),
    )(q_bf, qi_bf, proj, k_bf, vt_bf, kx_bf)


def _sample_score_kernel(pt_ref, q_ref, w_ref, kn_ref, *refs, pages, t_valid):
    page_refs = refs[:pages]
    past_ref, new_ref = refs[pages], refs[pages + 1]
    rows = SAMPLE_ROWS
    q = q_ref[0]
    w = w_ref[0]

    def head_sum(s):
        s = w * jnp.maximum(s, 0.0)
        acc = s[0:rows]
        for h in range(1, N_IDX_HEADS):
            acc = acc + s[h * rows:(h + 1) * rows]
        return acc

    for j in range(pages):
        kp = page_refs[j][0, 0].astype(BF16)
        past_ref[0, :, j * PAGE_SIZE:(j + 1) * PAGE_SIZE] = _sort_key(head_sum(_dot_nt(q, kp)))

    @pl.when(pl.program_id(1) == 0)
    def _():
        kn = jnp.concatenate([kn_ref[...], jnp.zeros((LANES - rows, IDX_DIM), F32)], axis=0).astype(BF16)
        sc = head_sum(_dot_nt(q, kn))
        t = lax.broadcasted_iota(I32, sc.shape, 0)
        s = lax.broadcasted_iota(I32, sc.shape, 1)
        ok = jnp.logical_and(s <= t, s < t_valid)
        new_ref[0] = _sort_key(jnp.where(ok, sc, -jnp.inf))


def _sample_scores(page_table, q_t, w_col, kif, cache_idx, layer, pages, t_valid):
    bs, n_pages = page_table.shape
    past = n_pages * PAGE_SIZE
    hr = N_IDX_HEADS * SAMPLE_ROWS
    page_spec = lambda j: pl.BlockSpec((1, 1, PAGE_SIZE, IDX_DIM),
                                       lambda b, c, pt: (layer, pt[b, c * pages + j], 0, 0))
    grid_spec = pltpu.PrefetchScalarGridSpec(
        num_scalar_prefetch=1,
        grid=(bs, n_pages // pages),
        in_specs=[pl.BlockSpec((1, hr, IDX_DIM), lambda b, c, pt: (b, 0, 0)),
                  pl.BlockSpec((1, hr, 1), lambda b, c, pt: (b, 0, 0)),
                  pl.BlockSpec((SAMPLE_ROWS, IDX_DIM), lambda b, c, pt: (b, 0))]
                 + [page_spec(j) for j in range(pages)],
        out_specs=[pl.BlockSpec((1, SAMPLE_ROWS, pages * PAGE_SIZE), lambda b, c, pt: (b, 0, c)),
                   pl.BlockSpec((1, SAMPLE_ROWS, LANES), lambda b, c, pt: (b, 0, 0))],
    )
    return pl.pallas_call(
        functools.partial(_sample_score_kernel, pages=pages, t_valid=t_valid),
        name="sample_scores",
        grid_spec=grid_spec,
        out_shape=[jax.ShapeDtypeStruct((bs, SAMPLE_ROWS, past), I32),
                   jax.ShapeDtypeStruct((bs, SAMPLE_ROWS, LANES), I32)],
        compiler_params=_cparams(("parallel", "arbitrary")),
    )(page_table, q_t, w_col, kif, *([cache_idx] * pages))


def _sample_attend_kernel(pt_ref, kp_ref, kn_ref, q_ref, knew_ref, vnew_ref, *refs, pages, n_sel, past):
    k_pages = refs[:pages]
    v_pages = refs[pages:2 * pages]
    o_ref = refs[2 * pages]
    thr_scr, lim_scr, m_scr, l_scr, acc_scr = refs[2 * pages + 1:]
    c = pl.program_id(1)
    rows = SAMPLE_ROWS
    span = pages * PAGE_SIZE

    @pl.when(c == 0)
    def _():
        m_scr[...] = jnp.full(m_scr.shape, NEG_BIG, F32)
        l_scr[...] = jnp.zeros(l_scr.shape, F32)
        acc_scr[...] = jnp.zeros(acc_scr.shape, F32)
        kp = kp_ref[0]
        kn = kn_ref[0]
        pos_p = lax.broadcasted_iota(I32, kp.shape, 1)
        pos_n = past + lax.broadcasted_iota(I32, kn.shape, 1)

        def count_where(pred):
            return (jnp.sum(pred(kp, pos_p).astype(F32), axis=1, keepdims=True)
                    + jnp.sum(pred(kn, pos_n).astype(F32), axis=1, keepdims=True))

        t = _kth_largest_key(lambda cand: count_where(lambda kk, pos: kk >= cand), float(n_sel), (rows, 1))
        thr_scr[...] = t
        lim_scr[...] = jnp.full((rows, 1), INT_MAX, I32)
        n_gt = count_where(lambda kk, pos: kk > t)
        n_ge = count_where(lambda kk, pos: kk >= t)

        @pl.when(jnp.max(n_ge) > float(n_sel))
        def _():
            lim_scr[...] = _tie_index_limit(
                lambda idx: count_where(lambda kk, pos: jnp.logical_and(kk == t, pos <= idx)),
                float(n_sel) - n_gt, past + LANES, (rows, 1))

    thr = thr_scr[...]
    lim = lim_scr[...]

    def update(g, qg, kc, vc, sel):
        sel = jnp.concatenate([sel] * KV_GROUP, axis=0)
        s = jnp.where(sel, _dot_nt(qg, kc), NEG_BIG)
        m_old = m_scr[g]
        m_new = jnp.maximum(m_old, jnp.max(s, axis=1, keepdims=True))
        p = jnp.where(sel, jnp.exp(s - m_new), 0.0)
        alpha = jnp.exp(m_old - m_new)
        l_scr[g] = alpha * l_scr[g] + jnp.sum(p, axis=1, keepdims=True)
        acc_scr[g] = alpha * acc_scr[g] + _dot(p.astype(BF16), vc)
        m_scr[g] = m_new

    def select(kk, pos):
        return jnp.logical_or(kk > thr, jnp.logical_and(kk == thr, pos <= lim))

    def page_cat(page_refs, g):
        return jnp.concatenate([r[0, 0, :, g, :] for r in page_refs], axis=0).astype(BF16)

    off = pl.multiple_of(c * span, span)
    kk = kp_ref[0, :, pl.ds(off, span)]
    sel_past = select(kk, off + lax.broadcasted_iota(I32, kk.shape, 1))
    q_groups = []
    for g in range(N_KV_HEADS):
        qg = jnp.concatenate(
            [q_ref[:, (g * KV_GROUP + r) * HEAD_DIM:(g * KV_GROUP + r + 1) * HEAD_DIM] for r in range(KV_GROUP)],
            axis=0)
        q_groups.append(qg)
        update(g, qg, page_cat(k_pages, g), page_cat(v_pages, g), sel_past)

    @pl.when(c == pl.num_programs(1) - 1)
    def _():
        kn = kn_ref[0]
        lane = lax.broadcasted_iota(I32, kn.shape, 1)
        sel_new = jnp.logical_and(select(kn, past + lane), lane < rows)
        pad = jnp.zeros((LANES - rows, KV_WIDTH), BF16)
        k_new = jnp.concatenate([knew_ref[...], pad], axis=0)
        v_new = jnp.concatenate([vnew_ref[...], pad], axis=0)
        for g in range(N_KV_HEADS):
            sl = slice(g * HEAD_DIM, (g + 1) * HEAD_DIM)
            update(g, q_groups[g], k_new[:, sl], v_new[:, sl], sel_new)
            o = acc_scr[g] / l_scr[g]
            for r in range(KV_GROUP):
                h = g * KV_GROUP + r
                o_ref[:, h * HEAD_DIM:(h + 1) * HEAD_DIM] = o[r * rows:(r + 1) * rows].astype(BF16)


def _sample_attend(page_table, keys_past, keys_new, q_bf, k_bf, v_bf, cache_k, cache_v, layer, pages, n_sel):
    bs, n_pages = page_table.shape
    past = n_pages * PAGE_SIZE
    page_spec = lambda j: pl.BlockSpec((1, 1, PAGE_SIZE, N_KV_HEADS, HEAD_DIM),
                                       lambda b, c, pt: (layer, pt[b, c * pages + j], 0, 0, 0))
    row = lambda w: pl.BlockSpec((SAMPLE_ROWS, w), lambda b, c, pt: (b, 0))
    grid_spec = pltpu.PrefetchScalarGridSpec(
        num_scalar_prefetch=1,
        grid=(bs, n_pages // pages),
        in_specs=[pl.BlockSpec((1, SAMPLE_ROWS, past), lambda b, c, pt: (b, 0, 0)),
                  pl.BlockSpec((1, SAMPLE_ROWS, LANES), lambda b, c, pt: (b, 0, 0)),
                  row(ATTN_WIDTH), row(KV_WIDTH), row(KV_WIDTH)]
                 + [page_spec(j) for j in range(pages)] * 2,
        out_specs=row(ATTN_WIDTH),
        scratch_shapes=[pltpu.VMEM((SAMPLE_ROWS, 1), I32),
                        pltpu.VMEM((SAMPLE_ROWS, 1), I32),
                        pltpu.VMEM((N_KV_HEADS, KV_GROUP * SAMPLE_ROWS, 1), F32),
                        pltpu.VMEM((N_KV_HEADS, KV_GROUP * SAMPLE_ROWS, 1), F32),
                        pltpu.VMEM((N_KV_HEADS, KV_GROUP * SAMPLE_ROWS, HEAD_DIM), F32)],
    )
    return pl.pallas_call(
        functools.partial(_sample_attend_kernel, pages=pages, n_sel=n_sel, past=past),
        name="sample_attend",
        grid_spec=grid_spec,
        out_shape=jax.ShapeDtypeStruct((bs * SAMPLE_ROWS, ATTN_WIDTH), BF16),
        compiler_params=_cparams(("parallel", "arbitrary")),
    )(page_table, keys_past, keys_new, q_bf, k_bf, v_bf, *([cache_k] * pages), *([cache_v] * pages))


def _delta_prep_kernel(x_ref, halo_ref, prev_ref, misc_ref, cw_ref, al_ref, dt_ref,
                       qn_ref, kn_ref, vv_ref, bg_ref, xp_scr, *, tiles_per_seq, t_valid, tt):
    i = pl.program_id(0)
    tile_in_seq = i % tiles_per_seq
    halo = jnp.where(tile_in_seq == 0, prev_ref[0], halo_ref[...])
    xp_scr[0:SUBLANES, :] = halo
    xp_scr[SUBLANES:SUBLANES + tt, :] = x_ref[...]
    base = SUBLANES - (CONV_WIDTH - 1)
    outs = (qn_ref, kn_ref, vv_ref)
    for sec in range(3):
        for h in range(N_DELTA_HEADS):
            col = sec * DELTA_WIDTH + h * HEAD_DIM
            sl = slice(col, col + HEAD_DIM)
            y = xp_scr[base:base + tt, sl] * cw_ref[0:1, sl]
            for j in range(1, CONV_WIDTH):
                y = y + xp_scr[base + j:base + j + tt, sl] * cw_ref[j:j + 1, sl]
            y = _silu(y)
            if sec < 2:
                y = y * lax.rsqrt(jnp.sum(y * y, axis=-1, keepdims=True) + EPS)
            if sec == 0:
                y = y * (HEAD_DIM ** -0.5)
            outs[sec][:, h * HEAD_DIM:(h + 1) * HEAD_DIM] = y
    m = misc_ref[...]
    lane = lax.broadcasted_iota(I32, m.shape, 1)
    row = tile_in_seq * tt + lax.broadcasted_iota(I32, m.shape, 0)
    beta = _sigmoid(m)
    g = -jnp.exp(al_ref[...]) * _softplus(m + dt_ref[...])
    is_b = jnp.logical_and(lane >= M_BD, lane < M_BD + N_DELTA_HEADS)
    is_g = jnp.logical_and(lane >= M_AD, lane < M_AD + N_DELTA_HEADS)
    comb = jnp.where(is_b, beta, jnp.where(is_g, g, 0.0))
    comb = jnp.where(row < t_valid, comb, 0.0)
    bg_ref[...] = pltpu.roll(comb, LANES - M_BD, 1)


def _delta_prep(proj, prev8, conv_w, a_log, dt_bias, b, t, tt, t_valid):
    n = proj.shape[0]
    tiles_per_seq = t // tt
    pad_vec = lambda v: jnp.zeros((1, LANES), F32).at[0, M_AD:M_AD + N_DELTA_HEADS].set(v)
    halo_blocks = tt // SUBLANES
    return pl.pallas_call(
        functools.partial(_delta_prep_kernel, tiles_per_seq=tiles_per_seq, t_valid=t_valid, tt=tt),
        name="delta_prep",
        grid=(n // tt,),
        in_specs=[pl.BlockSpec((tt, CONV_CHANNELS), lambda i: (i, C_CONV // CONV_CHANNELS)),
                  pl.BlockSpec((SUBLANES, CONV_CHANNELS),
                               lambda i: (jnp.maximum(i * halo_blocks - 1, 0), C_CONV // CONV_CHANNELS)),
                  pl.BlockSpec((1, SUBLANES, CONV_CHANNELS), lambda i: (i // tiles_per_seq, 0, 0)),
                  pl.BlockSpec((tt, LANES), lambda i: (i, C_MISC // LANES)),
                  pl.BlockSpec((CONV_WIDTH, CONV_CHANNELS), lambda i: (0, 0)),
                  pl.BlockSpec((1, LANES), lambda i: (0, 0)),
                  pl.BlockSpec((1, LANES), lambda i: (0, 0))],
        out_specs=[pl.BlockSpec((tt, DELTA_WIDTH), lambda i: (i, 0))] * 3
                  + [pl.BlockSpec((tt, LANES), lambda i: (i, 0))],
        out_shape=[jax.ShapeDtypeStruct((n, DELTA_WIDTH), F32)] * 3 + [jax.ShapeDtypeStruct((n, LANES), F32)],
        scratch_shapes=[pltpu.VMEM((SUBLANES + tt, CONV_CHANNELS), F32)],
        compiler_params=_cparams(("parallel",)),
    )(proj, proj, prev8, proj, conv_w, pad_vec(a_log), pad_vec(dt_bias))


def _mm(a, b):
    return _dot(a.astype(BF16), b.astype(BF16))


def _mm_nt(a, b):
    return _dot_nt(a.astype(BF16), b.astype(BF16))


DELTA_INV_BLOCK = 16
DELTA_STACK = 4


def _delta_chunk_kernel(qn_ref, kn_ref, vv_ref, bg_ref, z_ref, s0_ref, ow_ref, od_ref, so_ref, s_scr):
    c = pl.program_id(1)
    cs = DELTA_CHUNK

    @pl.when(c == 0)
    def _():
        s_scr[...] = s0_ref[0]

    bg = bg_ref[...]
    ltri = (lax.broadcasted_iota(I32, (cs, cs), 0) >= lax.broadcasted_iota(I32, (cs, cs), 1)).astype(BF16)
    g1 = bg.astype(BF16)
    r1 = bg - g1.astype(F32)
    g2 = r1.astype(BF16)
    g3 = (r1 - g2.astype(F32)).astype(BF16)
    gc = _dot(ltri, g1) + _dot(ltri, g2) + _dot(ltri, g3)
    gct = gc.T

    gh = DELTA_STACK
    rows = gh * cs
    rr = lax.broadcasted_iota(I32, (rows, rows), 0)
    cc = lax.broadcasted_iota(I32, (rows, rows), 1)
    same = (rr // cs) == (cc // cs)
    causal = jnp.logical_and(same, rr >= cc)
    strict = jnp.logical_and(same, rr > cc)
    eye = (rr == cc).astype(F32)
    row_head = lax.broadcasted_iota(I32, (rows, 1), 0) // cs
    for grp in range(N_DELTA_HEADS // gh):
        heads = [grp * gh + j for j in range(gh)]
        stack = lambda ref: jnp.concatenate([ref[:, h * HEAD_DIM:(h + 1) * HEAD_DIM] for h in heads], axis=0)
        col = lambda a, lane0: jnp.concatenate([a[:, lane0 + h:lane0 + h + 1] for h in heads], axis=0)
        k = stack(kn_ref)
        q = stack(qn_ref)
        v = stack(vv_ref)
        bcol = col(bg, 0)
        gcc = col(gc, N_DELTA_HEADS)
        gcr = jnp.concatenate([gct[N_DELTA_HEADS + h:N_DELTA_HEADS + h + 1, :] for h in heads], axis=1)
        g_last = [gc[cs - 1:cs, N_DELTA_HEADS + h:N_DELTA_HEADS + h + 1] for h in heads]
        glc = jnp.concatenate([jnp.broadcast_to(gl, (cs, 1)) for gl in g_last], axis=0)
        decay = jnp.exp(jnp.where(causal, gcc - gcr, -jnp.inf))
        kb = k * bcol
        eg = jnp.exp(gcc)
        kq = _mm_nt(jnp.concatenate([kb, q], axis=0), k)
        a = jnp.where(strict, kq[0:rows] * decay, 0.0)
        intra = jnp.where(causal, kq[rows:2 * rows] * decay, 0.0)
        x = -a
        nb = DELTA_INV_BLOCK
        y = jnp.where((rr // nb) == (cc // nb), x, 0.0)
        p = eye + y
        y = _mm(y, y)
        n_sq = max(1, int(nb - 1).bit_length())
        for lvl in range(1, n_sq):
            if lvl < n_sq - 1:
                py = _mm(jnp.concatenate([p, y], axis=0), y)
                p = p + py[0:rows]
                y = py[rows:2 * rows]
            else:
                p = p + _mm(p, y)
        size = 2 * nb
        while size <= cs:
            off = jnp.where(jnp.logical_and((rr // size) == (cc // size), (rr // (size // 2)) != (cc // (size // 2))),
                            x, 0.0)
            p = p + _mm(_mm(p, off), p)
            size *= 2
        sol = _mm(p, jnp.concatenate([v * bcol, kb * eg], axis=1))
        u = sol[:, 0:HEAD_DIM]
        w = sol[:, HEAD_DIM:2 * HEAD_DIM]
        lanes_g = slice(grp * gh * HEAD_DIM, (grp + 1) * gh * HEAD_DIM)
        s_g = s_scr[:, lanes_g]
        wq_s = _mm(jnp.concatenate([w, q * eg], axis=0), s_g)
        own = lambda m, r0: jnp.concatenate(
            [m[r0 + j * cs:r0 + (j + 1) * cs, j * HEAD_DIM:(j + 1) * HEAD_DIM] for j in range(gh)], axis=0)
        v_new = u - own(wq_s, 0)
        o = own(wq_s, rows) + _mm(intra, v_new)
        kg_t = (k * jnp.exp(glc - gcc)).T
        vn_blocks = jnp.concatenate([jnp.where(row_head == j, v_new, 0.0) for j in range(gh)], axis=1)
        s_decay = jnp.concatenate([jnp.broadcast_to(jnp.exp(gl), (1, HEAD_DIM)) for gl in g_last], axis=1)
        s_scr[:, lanes_g] = s_g * s_decay + _mm(kg_t, vn_blocks)
        on = o * lax.rsqrt(jnp.mean(o * o, axis=-1, keepdims=True) + EPS) * ow_ref[...]
        for j, h in enumerate(heads):
            sl = slice(h * HEAD_DIM, (h + 1) * HEAD_DIM)
            od_ref[:, sl] = (on[j * cs:(j + 1) * cs] * _silu(z_ref[:, sl])).astype(BF16)

    so_ref[0] = s_scr[...]


def _delta_chunks(qn, kn, vv, bg, zsrc, z_col_block, state0, o_norm_w, b, t):
    n = b * t
    nc = t // DELTA_CHUNK
    sw = N_DELTA_HEADS * HEAD_DIM
    row = lambda w, cb=0: pl.BlockSpec((DELTA_CHUNK, w), lambda bb, c: (bb * nc + c, cb))
    st = pl.BlockSpec((1, HEAD_DIM, sw), lambda bb, c: (bb, 0, 0))
    s_in = state0.transpose(0, 2, 1, 3).reshape(b, HEAD_DIM, sw)
    od, s_out = pl.pallas_call(
        _delta_chunk_kernel,
        name="delta_chunks",
        grid=(b, nc),
        in_specs=[row(DELTA_WIDTH), row(DELTA_WIDTH), row(DELTA_WIDTH), row(LANES),
                  row(DELTA_WIDTH, z_col_block), st, pl.BlockSpec((1, LANES), lambda bb, c: (0, 0))],
        out_specs=[row(DELTA_WIDTH), st],
        out_shape=[jax.ShapeDtypeStruct((n, DELTA_WIDTH), BF16),
                   jax.ShapeDtypeStruct((b, HEAD_DIM, sw), F32)],
        scratch_shapes=[pltpu.VMEM((HEAD_DIM, sw), F32)],
        compiler_params=_cparams(("parallel", "arbitrary")),
    )(qn, kn, vv, bg, zsrc, s_in, o_norm_w.reshape(1, LANES))
    return od, s_out.reshape(b, HEAD_DIM, N_DELTA_HEADS, HEAD_DIM).transpose(0, 2, 1, 3)


def _outproj_kernel(oa_ref, od_ref, x_ref, g1_ref, sh_ref, sc_ref, nw_ref, wo_ref, wrh_ref, wrl_ref, br_ref,
                    x1_ref, h2_ref, lg_ref):
    mix = _dot(oa_ref[...], wo_ref[0:ATTN_WIDTH, :]) + _dot(od_ref[...], wo_ref[ATTN_WIDTH:ATTN_WIDTH + DELTA_WIDTH, :])
    x1 = x_ref[...] + g1_ref[0] * mix
    x1_ref[...] = x1
    y = x1 * lax.rsqrt(jnp.mean(x1 * x1, axis=-1, keepdims=True) + EPS) * nw_ref[...]
    h2 = y * (1.0 + sc_ref[0]) + sh_ref[0]
    h2_ref[...] = h2
    hb = h2.astype(BF16)
    lo = (h2 - hb.astype(F32)).astype(BF16)
    lg_ref[...] = _dot(hb, wrh_ref[...]) + _dot(lo, wrh_ref[...]) + _dot(hb, wrl_ref[...]) + br_ref[...]


def _out_projection(o_attn, o_delta, x2d, gate1, shift2, scale2, norm2_w, w_out_bf, wr_hi, wr_lo, b_rt,
                    tm, rows_per_mod_block):
    n, d = x2d.shape
    r = gate1.shape[1]
    tiles_per_mod = rows_per_mod_block // tm
    mod_spec = pl.BlockSpec((1, r, d), lambda i: (i // tiles_per_mod, 0, 0))
    row = lambda w: pl.BlockSpec((tm, w), lambda i: (i, 0))
    full = lambda a: pl.BlockSpec(a.shape, lambda i: (0, 0))
    return pl.pallas_call(
        _outproj_kernel,
        name="out_proj",
        grid=(n // tm,),
        in_specs=[row(ATTN_WIDTH), row(DELTA_WIDTH), row(d), mod_spec, mod_spec, mod_spec,
                  pl.BlockSpec((1, d), lambda i: (0, 0)), full(w_out_bf), full(wr_hi), full(wr_lo), full(b_rt)],
        out_specs=[row(d), row(d), row(LANES)],
        out_shape=[jax.ShapeDtypeStruct((n, d), F32), jax.ShapeDtypeStruct((n, d), F32),
                   jax.ShapeDtypeStruct((n, LANES), F32)],
        compiler_params=_cparams(("parallel",)),
    )(o_attn, o_delta, x2d, gate1, shift2, scale2, norm2_w.reshape(1, d), w_out_bf, wr_hi, wr_lo, b_rt)


def _route_kernel(lg_ref, eid_ref, gate_ref):
    x = lg_ref[...]
    lane = lax.broadcasted_iota(I32, x.shape, 1)
    gl = jnp.where(lane < N_GROUPS, x, -jnp.inf)
    ge = jnp.exp(gl - jnp.max(gl, axis=1, keepdims=True))
    p = ge / jnp.sum(ge, axis=1, keepdims=True)
    p_max = jnp.max(p, axis=1, keepdims=True)
    grp = jnp.min(jnp.where(p == p_max, lane, LANES), axis=1, keepdims=True)
    e_lane = lane - N_GROUPS
    in_grp = jnp.logical_and(jnp.logical_and(e_lane >= 0, e_lane < N_EXPERTS),
                             (e_lane >> 3) == grp)
    rl = jnp.where(in_grp, x, -jnp.inf)
    v1 = jnp.max(rl, axis=1, keepdims=True)
    i1 = jnp.min(jnp.where(rl == v1, lane, LANES), axis=1, keepdims=True)
    rl2 = jnp.where(lane == i1, -jnp.inf, rl)
    v2 = jnp.max(rl2, axis=1, keepdims=True)
    i2 = jnp.min(jnp.where(rl2 == v2, lane, LANES), axis=1, keepdims=True)
    t = jnp.exp(v2 - v1)
    den = 1.0 + t
    eid_ref[...] = jnp.where(lane == 0, i1 - N_GROUPS, jnp.where(lane == 1, i2 - N_GROUPS, 0))
    gate_ref[...] = jnp.where(lane == 0, (1.0 / den) * p_max, jnp.where(lane == 1, (t / den) * p_max, 0.0))


def _route(logits, tm):
    n = logits.shape[0]
    spec = pl.BlockSpec((tm, LANES), lambda i: (i, 0))
    return pl.pallas_call(
        _route_kernel,
        name="route",
        grid=(n // tm,),
        in_specs=[spec],
        out_specs=[spec, spec],
        out_shape=[jax.ShapeDtypeStruct((n, LANES), I32), jax.ShapeDtypeStruct((n, LANES), F32)],
        compiler_params=_cparams(("parallel",)),
    )(logits)


def _row_gather(idx_ref, base, n_rows, src_hbm, dst, sem):
    def body(r, carry):
        pltpu.make_async_copy(src_hbm.at[pl.ds(idx_ref[base + r], 1), :], dst.at[pl.ds(r, 1), :], sem).start()
        return carry

    lax.fori_loop(0, n_rows, body, 0, unroll=8)


def _row_gather_wait(n_rows, src_hbm, dst, sem):
    pltpu.make_async_copy(src_hbm.at[pl.ds(0, n_rows), :], dst, sem).wait()


def _moe_kernel(tok_ref, j0_ref, be_ref, na_ref, h_hbm, wg_ref, wu_ref, wd_ref, o_ref,
                x_buf, sem, wg_scr, wu_scr, wd_scr, *, bm):
    i = pl.program_id(0)
    n_act = na_ref[0]

    @pl.when(jnp.logical_and(i == 0, n_act > 0))
    def _():
        _row_gather(tok_ref, j0_ref[0], bm, h_hbm, x_buf.at[0], sem.at[0])

    @pl.when(i < n_act)
    def _():
        slot = i % 2

        @pl.when(i + 1 < n_act)
        def _():
            _row_gather(tok_ref, j0_ref[i + 1], bm, h_hbm, x_buf.at[1 - slot], sem.at[1 - slot])

        changed = jnp.logical_or(i == 0, be_ref[i] != be_ref[jnp.maximum(i - 1, 0)])

        @pl.when(changed)
        def _():
            wg_scr[...] = wg_ref[0].astype(BF16)
            wu_scr[...] = wu_ref[0].astype(BF16)
            wd_scr[...] = wd_ref[0].astype(BF16)

        _row_gather_wait(bm, h_hbm, x_buf.at[slot], sem.at[slot])
        x = x_buf[slot].astype(BF16)
        hid = _silu(_dot(x, wg_scr[...])) * _dot(x, wu_scr[...])
        o_ref[...] = _dot(hid.astype(BF16), wd_scr[...])

    @pl.when(i >= n_act)
    def _():
        o_ref[...] = jnp.zeros(o_ref.shape, F32)


def _moe_experts(tok_sorted, block_j0, block_exp, n_active, h2, w_gate, w_up, w_down, bm):
    ns = block_exp.shape[0] * bm
    d = h2.shape[1]
    f = w_gate.shape[2]
    grid_spec = pltpu.PrefetchScalarGridSpec(
        num_scalar_prefetch=4,
        grid=(ns // bm,),
        in_specs=[pl.BlockSpec(memory_space=pl.ANY),
                  pl.BlockSpec((1, d, f), lambda i, tok, j0, be, na: (be[i], 0, 0)),
                  pl.BlockSpec((1, d, f), lambda i, tok, j0, be, na: (be[i], 0, 0)),
                  pl.BlockSpec((1, f, d), lambda i, tok, j0, be, na: (be[i], 0, 0))],
        out_specs=pl.BlockSpec((bm, d), lambda i, tok, j0, be, na: (i, 0)),
        scratch_shapes=[pltpu.VMEM((2, bm, d), F32), pltpu.SemaphoreType.DMA((2,)),
                        pltpu.VMEM((d, f), BF16), pltpu.VMEM((d, f), BF16), pltpu.VMEM((f, d), BF16)],
    )
    return pl.pallas_call(
        functools.partial(_moe_kernel, bm=bm),
        name="moe_experts",
        grid_spec=grid_spec,
        out_shape=jax.ShapeDtypeStruct((ns, d), F32),
        compiler_params=_cparams(("arbitrary",)),
    )(tok_sorted, block_j0, block_exp, n_active, h2, w_gate, w_up, w_down)


def _combine_kernel(dest_ref, x1_ref, gt_ref, g2_ref, y_hbm, o_ref, y_buf, sem, *, tm):
    i = pl.program_id(0)
    n = pl.num_programs(0)
    slot = i % 2

    @pl.when(i == 0)
    def _():
        _row_gather(dest_ref, 0, 2 * tm, y_hbm, y_buf.at[0], sem.at[0])

    @pl.when(i + 1 < n)
    def _():
        _row_gather(dest_ref, (i + 1) * 2 * tm, 2 * tm, y_hbm, y_buf.at[1 - slot], sem.at[1 - slot])

    _row_gather_wait(2 * tm, y_hbm, y_buf.at[slot], sem.at[slot])
    gt = gt_ref[...]
    y = y_buf[slot, 0:tm, :] * gt[:, 0:1] + y_buf[slot, tm:2 * tm, :] * gt[:, 1:2]
    o_ref[...] = x1_ref[...] + g2_ref[0] * y


def _combine(x1, dest, y_rows, gates, gate2, tm, rows_per_mod_block):
    n, d = x1.shape
    r = gate2.shape[1]
    tiles_per_mod = rows_per_mod_block // tm
    dest_tiles = dest.reshape(n // tm, tm, 2).transpose(0, 2, 1).reshape(-1)
    row = lambda w: pl.BlockSpec((tm, w), lambda i, dst: (i, 0))
    grid_spec = pltpu.PrefetchScalarGridSpec(
        num_scalar_prefetch=1,
        grid=(n // tm,),
        in_specs=[row(d), row(LANES),
                  pl.BlockSpec((1, r, d), lambda i, dst: (i // tiles_per_mod, 0, 0)),
                  pl.BlockSpec(memory_space=pl.ANY)],
        out_specs=row(d),
        scratch_shapes=[pltpu.VMEM((2, 2 * tm, d), F32), pltpu.SemaphoreType.DMA((2,))],
    )
    return pl.pallas_call(
        functools.partial(_combine_kernel, tm=tm),
        name="moe_combine",
        grid_spec=grid_spec,
        out_shape=jax.ShapeDtypeStruct((n, d), F32),
        compiler_params=_cparams(("arbitrary",)),
    )(dest_tiles, x1, gates, gate2, y_rows)


def _pick_tile(n, pref, mult=16):
    t = min(pref, n)
    while n % t or t % mult:
        t -= 1
    return t


def _pack_w_in(w_in):
    d = w_in.shape[0]
    bounds = np.cumsum(PROJ_SIZES)[:-1].tolist()
    qa, ka, va, qi, ki, wi, qd, kd, vd, zd, bd, ad = jnp.split(w_in, bounds, axis=1)
    used = IDX_DIM + N_IDX_HEADS + 2 * N_DELTA_HEADS
    misc = jnp.concatenate([ki, wi, bd, ad, jnp.zeros((d, LANES - used), w_in.dtype)], axis=1)
    cols = [qa, qi, zd, qd, kd, vd, ka, va, misc]
    width = sum(c.shape[1] for c in cols)
    cols.append(jnp.zeros((d, PROJ_PACKED - width), w_in.dtype))
    return jnp.concatenate(cols, axis=1).astype(BF16)


def _route_and_sort(eid, bm):
    n = eid.shape[0]
    nk = 2 * n
    flat_e = eid.reshape(-1)
    order = jnp.argsort(flat_e, stable=True).astype(I32)
    inv = jnp.argsort(order).astype(I32)
    onehot = flat_e[:, None] == jnp.arange(N_EXPERTS, dtype=I32)[None, :]
    counts = jnp.sum(onehot.astype(I32), axis=0)
    padded = (counts + bm - 1) // bm * bm
    pad_end = jnp.cumsum(padded)
    shift = (pad_end - padded) - (jnp.cumsum(counts) - counts)
    dest = inv + jnp.sum(jnp.where(onehot, shift[None, :], 0), axis=1)
    n_blocks = -(-nk // bm) + N_EXPERTS
    block_exp = jnp.minimum(jnp.searchsorted(pad_end, jnp.arange(n_blocks, dtype=I32) * bm, side='right'),
                            N_EXPERTS - 1).astype(I32)
    n_active = (pad_end[-1] // bm).astype(I32).reshape(1)
    block_j0 = jnp.clip(jnp.arange(n_blocks, dtype=I32) * bm - shift[block_exp], 0, nk)
    tok_sorted = jnp.concatenate([order // 2, jnp.zeros((bm,), I32)])
    return tok_sorted, block_j0, dest.astype(I32).reshape(n, 2), block_exp, n_active


def _layer(layer, yp, ys, cache_k, cache_v, cache_idx, state_ssm, state_conv, page_table, c_prompt, c_sample,
           w_in, w_out, conv_w, a_log, dt_bias, q_norm_w, k_norm_w, idx_k_norm_w, o_norm_w, norm1_w, norm2_w,
           w_ada, b_ada, w_group, b_group, w_router, b_router, w_gate, w_up, w_down):
    bp, tp, d = yp.shape
    bs, ts, _ = ys.shape
    past = page_table.shape[1] * PAGE_SIZE
    rows = SAMPLE_ROWS
    assert CONV_WIDTH - 1 <= ts <= rows and tp % SEL_SPAN == 0 and tp % DELTA_CHUNK == 0

    n_c = bp + bs
    n_c_pad = -(-n_c // SUBLANES) * SUBLANES
    c_all = jnp.concatenate([c_prompt, c_sample, jnp.zeros((n_c_pad - n_c, d), F32)], axis=0)
    mod = _ada_modulation(c_all, w_ada, b_ada)
    mods = jnp.split(mod, N_MOD, axis=1)
    mp = [m[:bp].reshape(bp, 1, d) for m in mods]
    ms = [jnp.repeat(m[bp:bp + bs], rows, axis=0).reshape(1, bs * rows, d) for m in mods]

    w_packed = _pack_w_in(w_in)
    w_out_bf = w_out.astype(BF16)
    w_rt = jnp.concatenate([w_group, w_router, jnp.zeros((d, LANES - N_GROUPS - N_EXPERTS), F32)], axis=1)
    wr_hi = w_rt.astype(BF16)
    wr_lo = (w_rt - wr_hi.astype(F32)).astype(BF16)
    b_rt = jnp.concatenate([b_group, b_router, jnp.zeros((LANES - N_GROUPS - N_EXPERTS,), F32)]).reshape(1, LANES)

    np_ = bp * tp
    xp2 = yp.reshape(np_, d)
    tm_p = _pick_tile(tp, 256)
    proj_p = _in_projection(xp2, mp[0], mp[1], norm1_w, w_packed, tm_p, tp)
    tq_p = _pick_tile(tp, 256)
    q_p, kf_p, kb_p, vt_p, qi_p, kif_p, kx_p = _attention_prep(
        proj_p, jnp.arange(tp), tq_p, q_norm_w, k_norm_w, idx_k_norm_w, True)
    oa_p = _dsa_prompt(q_p, qi_p, proj_p, kb_p, vt_p, kx_p, bp, tp)
    tt_p = _pick_tile(tp, 256)
    qn_p, kn_p, vv_p, bg_p = _delta_prep(proj_p, jnp.zeros((bp, SUBLANES, CONV_CHANNELS), F32), conv_w,
                                         a_log, dt_bias, bp, tp, tt_p, tp)
    od_p, ssm_p = _delta_chunks(qn_p, kn_p, vv_p, bg_p, proj_p, C_ZD // DELTA_WIDTH,
                                jnp.zeros((bp, N_DELTA_HEADS, HEAD_DIM, HEAD_DIM), F32), o_norm_w, bp, tp)
    tm_o = _pick_tile(tp, 256)
    x1_p, h2_p, lg_p = _out_projection(oa_p, od_p, xp2, mp[2], mp[3], mp[4], norm2_w, w_out_bf, wr_hi, wr_lo, b_rt,
                                       tm_o, tp)

    ns_ = bs * rows
    xs2 = jnp.pad(ys, ((0, 0), (0, rows - ts), (0, 0))).reshape(ns_, d)
    tm_s = _pick_tile(ns_, 256)
    proj_s = _in_projection(xs2, ms[0].reshape(ns_ // tm_s, tm_s, d), ms[1].reshape(ns_ // tm_s, tm_s, d),
                            norm1_w, w_packed, tm_s, tm_s)
    q_s, kf_s, kb_s, vb_s, qi_s, kif_s, _ = _attention_prep(
        proj_s, past + jnp.arange(rows), rows, q_norm_w, k_norm_w, idx_k_norm_w, False)
    q_t = qi_s.reshape(bs, rows, N_IDX_HEADS, IDX_DIM).transpose(0, 2, 1, 3).reshape(bs, N_IDX_HEADS * rows, IDX_DIM)
    w_col = proj_s[:, C_MISC + M_WI:C_MISC + M_WI + N_IDX_HEADS].reshape(bs, rows, N_IDX_HEADS)
    w_col = w_col.transpose(0, 2, 1).reshape(bs, N_IDX_HEADS * rows, 1)
    pages = _pick_tile(page_table.shape[1], 8, 1)
    keys_past, keys_new = _sample_scores(page_table, q_t, w_col, kif_s, cache_idx, layer, pages, ts)
    n_sel_s = min(TOPK_MAX, (past + ts) // 4)
    oa_s = _sample_attend(page_table, keys_past, keys_new, q_s, kb_s, vb_s, cache_k, cache_v, layer, pages, n_sel_s)
    prev8 = jnp.pad(state_conv, ((0, 0), (SUBLANES - (CONV_WIDTH - 1), 0), (0, 0)))
    qn_s, kn_s, vv_s, bg_s = _delta_prep(proj_s, prev8, conv_w, a_log, dt_bias, bs, rows, rows, ts)
    to_chunk = lambda a: jnp.pad(a.reshape(bs, rows, -1), ((0, 0), (0, DELTA_CHUNK - rows), (0, 0))).reshape(
        bs * DELTA_CHUNK, -1)
    z_s = proj_s[:, C_ZD:C_ZD + DELTA_WIDTH]
    od_s, ssm_s = _delta_chunks(to_chunk(qn_s), to_chunk(kn_s), to_chunk(vv_s), to_chunk(bg_s), to_chunk(z_s), 0,
                                state_ssm, o_norm_w, bs, DELTA_CHUNK)
    od_s = od_s.reshape(bs, DELTA_CHUNK, DELTA_WIDTH)[:, :rows].reshape(ns_, DELTA_WIDTH)
    x1_s, h2_s, lg_s = _out_projection(oa_s, od_s, xs2, ms[2], ms[3], ms[4], norm2_w, w_out_bf, wr_hi, wr_lo, b_rt,
                                       ns_, ns_)

    n_all = np_ + ns_
    h2_all = jnp.concatenate([h2_p, h2_s], axis=0)
    lg_all = jnp.concatenate([lg_p, lg_s], axis=0)
    eid, gates = _route(lg_all, _pick_tile(n_all, 512, SUBLANES))
    bm = 256
    tok_sorted, block_j0, dest, block_exp, n_active = _route_and_sort(eid[:, 0:2], bm)
    yb = _moe_experts(tok_sorted, block_j0, block_exp, n_active, h2_all, w_gate, w_up, w_down, bm)
    out_p = _combine(x1_p, dest[:np_], yb, gates[:np_], mp[5], tm_o, tp)
    out_s = _combine(x1_s, dest[np_:], yb, gates[np_:], ms[5], ns_, ns_)

    valid = lambda a: a.reshape(bs, rows, -1)[:, :ts]
    conv_p = proj_p.reshape(bp, tp, PROJ_PACKED)[:, tp - (CONV_WIDTH - 1):, C_CONV:C_CONV + CONV_CHANNELS]
    conv_s = proj_s.reshape(bs, rows, PROJ_PACKED)[:, ts - (CONV_WIDTH - 1):ts, C_CONV:C_CONV + CONV_CHANNELS]
    return (out_p.reshape(bp, tp, d), valid(out_s),
            kf_p.reshape(bp, tp, N_KV_HEADS, HEAD_DIM),
            proj_p[:, C_VA:C_VA + KV_WIDTH].reshape(bp, tp, N_KV_HEADS, HEAD_DIM),
            kif_p.reshape(bp, tp, IDX_DIM), ssm_p, conv_p,
            valid(kf_s).reshape(bs, ts, N_KV_HEADS, HEAD_DIM),
            valid(proj_s[:, C_VA:C_VA + KV_WIDTH]).reshape(bs, ts, N_KV_HEADS, HEAD_DIM),
            valid(kif_s), ssm_s, conv_s)


def kernel(x_prompt, x_sample, cache_k, cache_v, cache_idx_k, state_ssm, state_conv, page_table, c_prompt, c_sample,
           w_in, w_out, conv_w, a_log, dt_bias, q_norm_w, k_norm_w, idx_k_norm_w, o_norm_w, norm1_w, norm2_w,
           w_ada, b_ada, w_group, b_group, w_router, b_router, w_gate, w_up, w_down):
    depth = w_in.shape[0]
    yp, ys = x_prompt, x_sample
    per_layer = []
    for l in range(depth):
        res = _layer(l, yp, ys, cache_k, cache_v, cache_idx_k, state_ssm[l], state_conv[l], page_table,
                     c_prompt, c_sample, w_in[l], w_out[l], conv_w[l], a_log[l], dt_bias[l], q_norm_w[l],
                     k_norm_w[l], idx_k_norm_w[l], o_norm_w[l], norm1_w[l], norm2_w[l], w_ada[l], b_ada[l],
                     w_group[l], b_group[l], w_router[l], b_router[l], w_gate[l], w_up[l], w_down[l])
        yp, ys = res[0], res[1]
        per_layer.append(res[2:])
    stacked = tuple(jnp.stack([pl_[j] for pl_ in per_layer]) for j in range(10))
    return (yp, ys) + stacked
```

```python
import functools

import jax
import jax.numpy as jnp
import numpy as np
from jax import lax
from jax.experimental import pallas as pl
from jax.experimental.pallas import tpu as pltpu

F32 = jnp.float32
BF16 = jnp.bfloat16
I32 = jnp.int32

HEAD_DIM = 128
N_ATTN_HEADS = 8
N_KV_HEADS = 2
KV_GROUP = N_ATTN_HEADS // N_KV_HEADS
N_DELTA_HEADS = 8
N_IDX_HEADS = 16
IDX_DIM = 64
ATTN_WIDTH = N_ATTN_HEADS * HEAD_DIM
KV_WIDTH = N_KV_HEADS * HEAD_DIM
DELTA_WIDTH = N_DELTA_HEADS * HEAD_DIM
IDX_WIDTH = N_IDX_HEADS * IDX_DIM
CONV_CHANNELS = 3 * DELTA_WIDTH
TOPK_MAX = 256
ROPE_THETA = 500000.0
ROPE_FRACTION = 4
CONV_WIDTH = 4
DELTA_CHUNK = 64
N_GROUPS = 8
EXPERTS_PER_GROUP = 8
N_EXPERTS = N_GROUPS * EXPERTS_PER_GROUP
N_MOD = 6
EPS = 1e-6
PAGE_SIZE = 128
PROJ_SIZES = (ATTN_WIDTH, KV_WIDTH, KV_WIDTH, IDX_WIDTH, IDX_DIM, N_IDX_HEADS,
              DELTA_WIDTH, DELTA_WIDTH, DELTA_WIDTH, DELTA_WIDTH, N_DELTA_HEADS, N_DELTA_HEADS)

LANES = 128
SUBLANES = 8
VMEM_LIMIT = 56 * 1024 * 1024

C_QA = 0
C_QI = 1024
C_ZD = 2048
C_CONV = 3072
C_KA = 6144
C_VA = 6400
C_MISC = 6656
PROJ_PACKED = 6912
M_KI = 0
M_WI = 64
M_BD = 80
M_AD = 88

Q_TILE = 128
KEY_CHUNK = 256
SEL_SPAN = 512
NEG_INF_KEY = -2139095041
SAMPLE_ROWS = 16
NEG_BIG = -1e30
INT_MIN = -2147483648
INT_MAX = 2147483647


def _cparams(sem):
    return pltpu.CompilerParams(dimension_semantics=sem, vmem_limit_bytes=VMEM_LIMIT)


def _dot(a, b):
    return jnp.dot(a, b, preferred_element_type=F32)


def _dot_nt(a, b):
    return lax.dot_general(a, b, (((1,), (1,)), ((), ())), preferred_element_type=F32)


def _dot_tn(a, b):
    return lax.dot_general(a, b, (((0,), (0,)), ((), ())), preferred_element_type=F32)


def _sigmoid(x):
    return 1.0 / (1.0 + jnp.exp(-x))


def _silu(x):
    return x * _sigmoid(x)


def _softplus(x):
    return jnp.maximum(x, 0.0) + jnp.log(1.0 + jnp.exp(-jnp.abs(x)))


def _ada_kernel(c_ref, w_ref, b_ref, o_ref):
    s = _silu(c_ref[...]).astype(BF16)
    o_ref[...] = _dot(s, w_ref[...].astype(BF16)) + b_ref[...]


def _ada_modulation(c, w_ada, b_ada):
    r, d = c.shape
    n = w_ada.shape[1]
    tn = 1024 if n % 1024 == 0 else n
    return pl.pallas_call(
        _ada_kernel,
        name="ada_mod",
        grid=(n // tn,),
        in_specs=[pl.BlockSpec((r, d), lambda j: (0, 0)),
                  pl.BlockSpec((d, tn), lambda j: (0, j)),
                  pl.BlockSpec((1, tn), lambda j: (0, j))],
        out_specs=pl.BlockSpec((r, tn), lambda j: (0, j)),
        out_shape=jax.ShapeDtypeStruct((r, n), F32),
        compiler_params=_cparams(("parallel",)),
    )(c, w_ada, b_ada.reshape(1, n))


INPROJ_COLS = 1152


def _inproj_kernel(x_ref, sh_ref, sc_ref, nw_ref, w_ref, o_ref):
    x = x_ref[...]
    y = x * lax.rsqrt(jnp.mean(x * x, axis=-1, keepdims=True) + EPS) * nw_ref[...]
    h = (y * (1.0 + sc_ref[0]) + sh_ref[0]).astype(BF16)
    for c0 in range(0, o_ref.shape[1], INPROJ_COLS):
        o_ref[:, c0:c0 + INPROJ_COLS] = _dot(h, w_ref[:, c0:c0 + INPROJ_COLS])


def _in_projection(x2d, shift, scale, norm_w, w_packed, tm, rows_per_mod_block):
    n, d = x2d.shape
    np_ = w_packed.shape[1]
    r = shift.shape[1]
    tiles_per_mod = rows_per_mod_block // tm
    mod_spec = pl.BlockSpec((1, r, d), lambda i: (i // tiles_per_mod, 0, 0))
    return pl.pallas_call(
        _inproj_kernel,
        name="in_proj",
        grid=(n // tm,),
        in_specs=[pl.BlockSpec((tm, d), lambda i: (i, 0)),
                  mod_spec, mod_spec,
                  pl.BlockSpec((1, d), lambda i: (0, 0)),
                  pl.BlockSpec((d, np_), lambda i: (0, 0), pipeline_mode=pl.Buffered(1))],
        out_specs=pl.BlockSpec((tm, np_), lambda i: (i, 0)),
        out_shape=jax.ShapeDtypeStruct((n, np_), F32),
        compiler_params=_cparams(("parallel",)),
    )(x2d, shift, scale, norm_w.reshape(1, d), w_packed)


def _rope(x, tab, rot):
    c = tab[:, 0:LANES]
    s1 = tab[:, LANES:2 * LANES]
    s2 = tab[:, 2 * LANES:3 * LANES]
    return x * c + pltpu.roll(x, LANES - rot, 1) * s1 + pltpu.roll(x, rot, 1) * s2


def _rms_head(x, w):
    return x * lax.rsqrt(jnp.mean(x * x, axis=-1, keepdims=True) + EPS) * w


def _prep_kernel(qa_ref, qi_ref, ka_ref, va_ref, misc_ref, tabm_ref, tabi_ref, qw_ref, kw_ref, iw_ref,
                 q_ref, kf_ref, kb_ref, vb_ref, qib_ref, kif_ref, kib_ref, *, transpose_v):
    tabm = tabm_ref[0]
    tabi = tabi_ref[0]
    half_main = HEAD_DIM // ROPE_FRACTION // 2
    half_idx = IDX_DIM // ROPE_FRACTION // 2
    for h in range(N_ATTN_HEADS):
        sl = slice(h * HEAD_DIM, (h + 1) * HEAD_DIM)
        y = _rope(_rms_head(qa_ref[:, sl], qw_ref[...]), tabm, half_main)
        q_ref[:, sl] = (y * (HEAD_DIM ** -0.5)).astype(BF16)
    for h in range(N_KV_HEADS):
        sl = slice(h * HEAD_DIM, (h + 1) * HEAD_DIM)
        y = _rope(_rms_head(ka_ref[:, sl], kw_ref[...]), tabm, half_main)
        kf_ref[:, sl] = y
        kb_ref[:, sl] = y.astype(BF16)
    if transpose_v:
        vb_ref[...] = va_ref[...].T.astype(BF16)
    else:
        vb_ref[...] = va_ref[...].astype(BF16)
    for p in range(IDX_WIDTH // LANES):
        sl = slice(p * LANES, (p + 1) * LANES)
        qib_ref[:, sl] = _rope(qi_ref[:, sl], tabi, half_idx).astype(BF16)
    m = misc_ref[...]
    lane = lax.broadcasted_iota(I32, m.shape, 1)
    ki = jnp.where(lane < IDX_DIM, m, 0.0)
    ms = jnp.sum(ki * ki, axis=-1, keepdims=True) * (1.0 / IDX_DIM)
    y = _rope(ki * lax.rsqrt(ms + EPS) * iw_ref[...], tabi, half_idx)
    kif_ref[...] = y[:, 0:IDX_DIM]
    kib_ref[...] = (y + pltpu.roll(y, IDX_DIM, 1)).astype(BF16)


def _rope_tables(pos, head_dim, group):
    d_rot = head_dim // ROPE_FRACTION
    half = d_rot // 2
    inv_freq = jnp.power(ROPE_THETA, -(jnp.arange(half, dtype=F32) * 2.0 / d_rot))
    ang = pos.astype(F32)[:, None] * inv_freq[None, :]
    cos = jnp.cos(ang)
    sin = jnp.sin(ang)
    t = pos.shape[0]
    z = jnp.zeros((t, group - d_rot), F32)
    c = jnp.concatenate([cos, cos, jnp.ones((t, group - d_rot), F32)], axis=1)
    s1 = jnp.concatenate([-sin, jnp.zeros((t, half), F32), z], axis=1)
    s2 = jnp.concatenate([jnp.zeros((t, half), F32), sin, z], axis=1)
    rep = LANES // group
    return jnp.concatenate([jnp.tile(c, (1, rep)), jnp.tile(s1, (1, rep)), jnp.tile(s2, (1, rep))], axis=1)


def _attention_prep(proj, pos, tq, q_norm_w, k_norm_w, idx_k_norm_w, transpose_v):
    n = proj.shape[0]
    p = pos.shape[0]
    g = p // tq
    tabm = _rope_tables(pos, HEAD_DIM, LANES).reshape(g, tq, 3 * LANES)
    tabi = _rope_tables(pos, IDX_DIM, IDX_DIM).reshape(g, tq, 3 * LANES)
    iw = jnp.concatenate([idx_k_norm_w, jnp.zeros((LANES - IDX_DIM,), F32)]).reshape(1, LANES)
    row = lambda w, c: pl.BlockSpec((tq, w), lambda i: (i, c // w))
    tab_spec = pl.BlockSpec((1, tq, 3 * LANES), lambda i: (i % g, 0, 0))
    vec_spec = pl.BlockSpec((1, LANES), lambda i: (0, 0))
    out_row = lambda w: pl.BlockSpec((tq, w), lambda i: (i, 0))
    v_spec = pl.BlockSpec((KV_WIDTH, tq), lambda i: (0, i)) if transpose_v else out_row(KV_WIDTH)
    v_shape = (KV_WIDTH, n) if transpose_v else (n, KV_WIDTH)
    return pl.pallas_call(
        functools.partial(_prep_kernel, transpose_v=transpose_v),
        name="attn_prep",
        grid=(n // tq,),
        in_specs=[row(ATTN_WIDTH, C_QA), row(IDX_WIDTH, C_QI), row(KV_WIDTH, C_KA), row(KV_WIDTH, C_VA),
                  row(LANES, C_MISC), tab_spec, tab_spec, vec_spec, vec_spec, vec_spec],
        out_specs=[out_row(ATTN_WIDTH), out_row(KV_WIDTH), out_row(KV_WIDTH), v_spec,
                   out_row(IDX_WIDTH), out_row(IDX_DIM), out_row(LANES)],
        out_shape=[jax.ShapeDtypeStruct((n, ATTN_WIDTH), BF16),
                   jax.ShapeDtypeStruct((n, KV_WIDTH), F32),
                   jax.ShapeDtypeStruct((n, KV_WIDTH), BF16),
                   jax.ShapeDtypeStruct(v_shape, BF16),
                   jax.ShapeDtypeStruct((n, IDX_WIDTH), BF16),
                   jax.ShapeDtypeStruct((n, IDX_DIM), F32),
                   jax.ShapeDtypeStruct((n, LANES), BF16)],
        compiler_params=_cparams(("parallel",)),
    )(proj, proj, proj, proj, proj, tabm, tabi,
      q_norm_w.reshape(1, LANES), k_norm_w.reshape(1, LANES), iw)


def _sort_key(x):
    b = pltpu.bitcast(x + 0.0, I32)
    return b ^ ((b >> 31) & INT_MAX)


def _kth_largest_key(count_ge, k, shape):
    def body(it, ans_u):
        bit = jnp.left_shift(jnp.int32(1), 31 - it)
        cand_u = ans_u | bit
        cnt = count_ge(cand_u ^ INT_MIN)
        return jnp.where(cnt >= k, cand_u, ans_u)

    ans_u = lax.fori_loop(0, 32, body, jnp.zeros(shape, I32))
    return ans_u ^ INT_MIN


def _tie_index_limit(count_eq_le, need, n_keys, shape):
    nbits = max(1, int(n_keys - 1).bit_length())

    def body(it, lo):
        bit = jnp.left_shift(jnp.int32(1), nbits - 1 - it)
        cand = lo | bit
        cnt = count_eq_le(cand - 1)
        return jnp.where(cnt >= need, lo, cand)

    return lax.fori_loop(0, nbits, body, jnp.zeros(shape, I32))


def _dsa_prompt_kernel(q_ref, qi_ref, misc_ref, k_ref, vt_ref, kx_ref, o_ref,
                       key_scr, qsel_scr, qg_scr, thr_scr, lim_scr, m_scr, l_scr, acc_scr, *, n_sel):
    i = pl.program_id(1)
    tq = Q_TILE
    ck = KEY_CHUNK
    n_ch = (i * tq + tq + ck - 1) // ck
    q_pos = i * tq + lax.broadcasted_iota(I32, (1, tq), 1)
    row_k = lax.broadcasted_iota(I32, (ck, 1), 0)

    lo_half = lax.broadcasted_iota(I32, (tq, LANES), 1) < IDX_DIM
    zero = jnp.zeros((), BF16)
    for p in range(IDX_WIDTH // LANES):
        slab = qi_ref[:, p * LANES:(p + 1) * LANES]
        qsel_scr[(2 * p) * tq:(2 * p + 1) * tq, :] = jnp.where(lo_half, slab, zero)
        qsel_scr[(2 * p + 1) * tq:(2 * p + 2) * tq, :] = jnp.where(lo_half, zero, slab)
    w_t = misc_ref[...].T

    def score_chunk(c, carry):
        off = pl.multiple_of(c * ck, ck)
        s = _dot_nt(kx_ref[pl.ds(off, ck), :], qsel_scr[...])
        acc = jnp.zeros((ck, tq), F32)
        for h in range(N_IDX_HEADS):
            acc = acc + w_t[M_WI + h:M_WI + h + 1, :] * jnp.maximum(s[:, h * tq:(h + 1) * tq], 0.0)
        acc = jnp.where(off + row_k <= q_pos, acc, -jnp.inf)
        key_scr[pl.ds(off, ck), :] = _sort_key(acc)
        return carry

    lax.fori_loop(0, n_ch, score_chunk, 0)

    spc = SEL_SPAN // ck
    n_span = (n_ch + spc - 1) // spc
    neg_key = jnp.full((ck, tq), NEG_INF_KEY, I32)

    def pad_chunk(c, carry):
        key_scr[pl.ds(pl.multiple_of(c * ck, ck), ck), :] = neg_key
        return carry

    lax.fori_loop(n_ch, n_span * spc, pad_chunk, 0)

    thr_scr[...] = jnp.full((1, tq), INT_MIN, I32)
    lim_scr[...] = jnp.full((1, tq), INT_MAX, I32)

    def select_threshold(n_keys):
        def count_where(pred):
            tot = jnp.zeros((SUBLANES, tq), F32)
            for c0 in range(0, n_keys, ck):
                hit = pred(key_scr[c0:c0 + ck, :], c0 + row_k).astype(F32)
                tot = tot + jnp.sum(hit.reshape(ck // SUBLANES, SUBLANES, tq), axis=0)
            return jnp.sum(tot, axis=0, keepdims=True)

        t = _kth_largest_key(lambda cand: count_where(lambda kk, pos: kk >= cand), float(n_sel), (1, tq))
        thr_scr[...] = t
        n_gt = count_where(lambda kk, pos: kk > t)
        n_ge = count_where(lambda kk, pos: kk >= t)

        @pl.when(jnp.max(n_ge) > float(n_sel))
        def _():
            lim_scr[...] = _tie_index_limit(
                lambda idx: count_where(lambda kk, pos: jnp.logical_and(kk == t, pos <= idx)),
                float(n_sel) - n_gt, k_ref.shape[0], (1, tq))

    for spans in range(1, k_ref.shape[0] // SEL_SPAN + 1):
        if spans * SEL_SPAN > n_sel:
            pl.when(jnp.logical_and(n_span == spans, (i + 1) * tq > n_sel))(
                functools.partial(select_threshold, spans * SEL_SPAN))

    thr = thr_scr[...]
    lim = lim_scr[...]

    for g in range(N_KV_HEADS):
        for r in range(KV_GROUP):
            h = g * KV_GROUP + r
            qg_scr[g, r * tq:(r + 1) * tq, :] = q_ref[:, h * HEAD_DIM:(h + 1) * HEAD_DIM]
    m_scr[...] = jnp.full(m_scr.shape, NEG_BIG, F32)
    l_scr[...] = jnp.zeros(l_scr.shape, F32)
    acc_scr[...] = jnp.zeros(acc_scr.shape, F32)

    def attend_chunk(c, carry):
        off = pl.multiple_of(c * ck, ck)
        kk = key_scr[pl.ds(off, ck), :]
        pos = off + row_k
        sel = jnp.logical_or(kk > thr, jnp.logical_and(kk == thr, pos <= lim))
        sel = jnp.logical_and(sel, pos <= q_pos)
        for g in range(N_KV_HEADS):
            kc = k_ref[pl.ds(off, ck), g * HEAD_DIM:(g + 1) * HEAD_DIM]
            vt = vt_ref[g * HEAD_DIM:(g + 1) * HEAD_DIM, pl.ds(off, ck)]
            qk = _dot_nt(kc, qg_scr[g])
            s = jnp.concatenate([jnp.where(sel, qk[:, r * tq:(r + 1) * tq], NEG_BIG) for r in range(KV_GROUP)],
                                axis=1)
            m_old = m_scr[g]
            m_new = jnp.maximum(m_old, jnp.max(s, axis=0, keepdims=True))
            p = jnp.exp(s - m_new)
            alpha = jnp.exp(m_old - m_new)
            l_scr[g] = alpha * l_scr[g] + jnp.sum(p, axis=0, keepdims=True)
            acc_scr[g] = alpha * acc_scr[g] + _dot(vt, p.astype(BF16))
            m_scr[g] = m_new
        return carry

    lax.fori_loop(0, n_ch, attend_chunk, 0)
    for g in range(N_KV_HEADS):
        o_t = acc_scr[g] / l_scr[g]
        for r in range(KV_GROUP):
            h = g * KV_GROUP + r
            o_ref[:, h * HEAD_DIM:(h + 1) * HEAD_DIM] = o_t[:, r * tq:(r + 1) * tq].T.astype(BF16)


def _dsa_prompt(q_bf, qi_bf, proj, k_bf, vt_bf, kx_bf, b, t):
    n = b * t
    nq = t // Q_TILE
    n_sel = min(TOPK_MAX, t // 4)
    qrow = lambda w: pl.BlockSpec((Q_TILE, w), lambda bb, i: (bb * nq + i, 0))
    seq = lambda w: pl.BlockSpec((t, w), lambda bb, i: (bb, 0))
    return pl.pallas_call(
        functools.partial(_dsa_prompt_kernel, n_sel=n_sel),
        name="dsa_prompt",
        grid=(b, nq),
        in_specs=[qrow(ATTN_WIDTH), qrow(IDX_WIDTH),
                  pl.BlockSpec((Q_TILE, LANES), lambda bb, i: (bb * nq + i, C_MISC // LANES)),
                  seq(KV_WIDTH), pl.BlockSpec((KV_WIDTH, t), lambda bb, i: (0, bb)), seq(LANES)],
        out_specs=qrow(ATTN_WIDTH),
        out_shape=jax.ShapeDtypeStruct((n, ATTN_WIDTH), BF16),
        scratch_shapes=[pltpu.VMEM((t, Q_TILE), I32),
                        pltpu.VMEM((N_IDX_HEADS * Q_TILE, LANES), BF16),
                        pltpu.VMEM((N_KV_HEADS, KV_GROUP * Q_TILE, HEAD_DIM), BF16),
                        pltpu.VMEM((1, Q_TILE), I32),
                        pltpu.VMEM((1, Q_TILE), I32),
                        pltpu.VMEM((N_KV_HEADS, 1, KV_GROUP * Q_TILE), F32),
                        pltpu.VMEM((N_KV_HEADS, 1, KV_GROUP * Q_TILE), F32),
                        pltpu.VMEM((N_KV_HEADS, HEAD_DIM, KV_GROUP * Q_TILE), F32)],
        compiler_params=_cparams(("parallel", "arbitrary")),
    )(q_bf, qi_bf, proj, k_bf, vt_bf, kx_bf)


def _sample_score_kernel(pt_ref, q_ref, w_ref, kn_ref, *refs, pages, t_valid):
    page_refs = refs[:pages]
    past_ref, new_ref = refs[pages], refs[pages + 1]
    rows = SAMPLE_ROWS
    q = q_ref[0]
    w = w_ref[0]

    def head_sum(s):
        s = w * jnp.maximum(s, 0.0)
        acc = s[0:rows]
        for h in range(1, N_IDX_HEADS):
            acc = acc + s[h * rows:(h + 1) * rows]
        return acc

    for j in range(pages):
        kp = page_refs[j][0, 0].astype(BF16)
        past_ref[0, :, j * PAGE_SIZE:(j + 1) * PAGE_SIZE] = _sort_key(head_sum(_dot_nt(q, kp)))

    @pl.when(pl.program_id(1) == 0)
    def _():
        kn = jnp.concatenate([kn_ref[...], jnp.zeros((LANES - rows, IDX_DIM), F32)], axis=0).astype(BF16)
        sc = head_sum(_dot_nt(q, kn))
        t = lax.broadcasted_iota(I32, sc.shape, 0)
        s = lax.broadcasted_iota(I32, sc.shape, 1)
        ok = jnp.logical_and(s <= t, s < t_valid)
        new_ref[0] = _sort_key(jnp.where(ok, sc, -jnp.inf))


def _sample_scores(page_table, q_t, w_col, kif, cache_idx, layer, pages, t_valid):
    bs, n_pages = page_table.shape
    past = n_pages * PAGE_SIZE
    hr = N_IDX_HEADS * SAMPLE_ROWS
    page_spec = lambda j: pl.BlockSpec((1, 1, PAGE_SIZE, IDX_DIM),
                                       lambda b, c, pt: (layer, pt[b, c * pages + j], 0, 0))
    grid_spec = pltpu.PrefetchScalarGridSpec(
        num_scalar_prefetch=1,
        grid=(bs, n_pages // pages),
        in_specs=[pl.BlockSpec((1, hr, IDX_DIM), lambda b, c, pt: (b, 0, 0)),
                  pl.BlockSpec((1, hr, 1), lambda b, c, pt: (b, 0, 0)),
                  pl.BlockSpec((SAMPLE_ROWS, IDX_DIM), lambda b, c, pt: (b, 0))]
                 + [page_spec(j) for j in range(pages)],
        out_specs=[pl.BlockSpec((1, SAMPLE_ROWS, pages * PAGE_SIZE), lambda b, c, pt: (b, 0, c)),
                   pl.BlockSpec((1, SAMPLE_ROWS, LANES), lambda b, c, pt: (b, 0, 0))],
    )
    return pl.pallas_call(
        functools.partial(_sample_score_kernel, pages=pages, t_valid=t_valid),
        name="sample_scores",
        grid_spec=grid_spec,
        out_shape=[jax.ShapeDtypeStruct((bs, SAMPLE_ROWS, past), I32),
                   jax.ShapeDtypeStruct((bs, SAMPLE_ROWS, LANES), I32)],
        compiler_params=_cparams(("parallel", "arbitrary")),
    )(page_table, q_t, w_col, kif, *([cache_idx] * pages))


def _sample_attend_kernel(pt_ref, kp_ref, kn_ref, q_ref, knew_ref, vnew_ref, *refs, pages, n_sel, past):
    k_pages = refs[:pages]
    v_pages = refs[pages:2 * pages]
    o_ref = refs[2 * pages]
    thr_scr, lim_scr, m_scr, l_scr, acc_scr = refs[2 * pages + 1:]
    c = pl.program_id(1)
    rows = SAMPLE_ROWS
    span = pages * PAGE_SIZE

    @pl.when(c == 0)
    def _():
        m_scr[...] = jnp.full(m_scr.shape, NEG_BIG, F32)
        l_scr[...] = jnp.zeros(l_scr.shape, F32)
        acc_scr[...] = jnp.zeros(acc_scr.shape, F32)
        kp = kp_ref[0]
        kn = kn_ref[0]
        pos_p = lax.broadcasted_iota(I32, kp.shape, 1)
        pos_n = past + lax.broadcasted_iota(I32, kn.shape, 1)

        def count_where(pred):
            return (jnp.sum(pred(kp, pos_p).astype(F32), axis=1, keepdims=True)
                    + jnp.sum(pred(kn, pos_n).astype(F32), axis=1, keepdims=True))

        t = _kth_largest_key(lambda cand: count_where(lambda kk, pos: kk >= cand), float(n_sel), (rows, 1))
        thr_scr[...] = t
        lim_scr[...] = jnp.full((rows, 1), INT_MAX, I32)
        n_gt = count_where(lambda kk, pos: kk > t)
        n_ge = count_where(lambda kk, pos: kk >= t)

        @pl.when(jnp.max(n_ge) > float(n_sel))
        def _():
            lim_scr[...] = _tie_index_limit(
                lambda idx: count_where(lambda kk, pos: jnp.logical_and(kk == t, pos <= idx)),
                float(n_sel) - n_gt, past + LANES, (rows, 1))

    thr = thr_scr[...]
    lim = lim_scr[...]

    def update(g, qg, kc, vc, sel):
        sel = jnp.concatenate([sel] * KV_GROUP, axis=0)
        s = jnp.where(sel, _dot_nt(qg, kc), NEG_BIG)
        m_old = m_scr[g]
        m_new = jnp.maximum(m_old, jnp.max(s, axis=1, keepdims=True))
        p = jnp.where(sel, jnp.exp(s - m_new), 0.0)
        alpha = jnp.exp(m_old - m_new)
        l_scr[g] = alpha * l_scr[g] + jnp.sum(p, axis=1, keepdims=True)
        acc_scr[g] = alpha * acc_scr[g] + _dot(p.astype(BF16), vc)
        m_scr[g] = m_new

    def select(kk, pos):
        return jnp.logical_or(kk > thr, jnp.logical_and(kk == thr, pos <= lim))

    def page_cat(page_refs, g):
        return jnp.concatenate([r[0, 0, :, g, :] for r in page_refs], axis=0).astype(BF16)

    off = pl.multiple_of(c * span, span)
    kk = kp_ref[0, :, pl.ds(off, span)]
    sel_past = select(kk, off + lax.broadcasted_iota(I32, kk.shape, 1))
    q_groups = []
    for g in range(N_KV_HEADS):
        qg = jnp.concatenate(
            [q_ref[:, (g * KV_GROUP + r) * HEAD_DIM:(g * KV_GROUP + r + 1) * HEAD_DIM] for r in range(KV_GROUP)],
            axis=0)
        q_groups.append(qg)
        update(g, qg, page_cat(k_pages, g), page_cat(v_pages, g), sel_past)

    @pl.when(c == pl.num_programs(1) - 1)
    def _():
        kn = kn_ref[0]
        lane = lax.broadcasted_iota(I32, kn.shape, 1)
        sel_new = jnp.logical_and(select(kn, past + lane), lane < rows)
        pad = jnp.zeros((LANES - rows, KV_WIDTH), BF16)
        k_new = jnp.concatenate([knew_ref[...], pad], axis=0)
        v_new = jnp.concatenate([vnew_ref[...], pad], axis=0)
        for g in range(N_KV_HEADS):
            sl = slice(g * HEAD_DIM, (g + 1) * HEAD_DIM)
            update(g, q_groups[g], k_new[:, sl], v_new[:, sl], sel_new)
            o = acc_scr[g] / l_scr[g]
            for r in range(KV_GROUP):
                h = g * KV_GROUP + r
                o_ref[:, h * HEAD_DIM:(h + 1) * HEAD_DIM] = o[r * rows:(r + 1) * rows].astype(BF16)


def _sample_attend(page_table, keys_past, keys_new, q_bf, k_bf, v_bf, cache_k, cache_v, layer, pages, n_sel):
    bs, n_pages = page_table.shape
    past = n_pages * PAGE_SIZE
    page_spec = lambda j: pl.BlockSpec((1, 1, PAGE_SIZE, N_KV_HEADS, HEAD_DIM),
                                       lambda b, c, pt: (layer, pt[b, c * pages + j], 0, 0, 0))
    row = lambda w: pl.BlockSpec((SAMPLE_ROWS, w), lambda b, c, pt: (b, 0))
    grid_spec = pltpu.PrefetchScalarGridSpec(
        num_scalar_prefetch=1,
        grid=(bs, n_pages // pages),
        in_specs=[pl.BlockSpec((1, SAMPLE_ROWS, past), lambda b, c, pt: (b, 0, 0)),
                  pl.BlockSpec((1, SAMPLE_ROWS, LANES), lambda b, c, pt: (b, 0, 0)),
                  row(ATTN_WIDTH), row(KV_WIDTH), row(KV_WIDTH)]
                 + [page_spec(j) for j in range(pages)] * 2,
        out_specs=row(ATTN_WIDTH),
        scratch_shapes=[pltpu.VMEM((SAMPLE_ROWS, 1), I32),
                        pltpu.VMEM((SAMPLE_ROWS, 1), I32),
                        pltpu.VMEM((N_KV_HEADS, KV_GROUP * SAMPLE_ROWS, 1), F32),
                        pltpu.VMEM((N_KV_HEADS, KV_GROUP * SAMPLE_ROWS, 1), F32),
                        pltpu.VMEM((N_KV_HEADS, KV_GROUP * SAMPLE_ROWS, HEAD_DIM), F32)],
    )
    return pl.pallas_call(
        functools.partial(_sample_attend_kernel, pages=pages, n_sel=n_sel, past=past),
        name="sample_attend",
        grid_spec=grid_spec,
        out_shape=jax.ShapeDtypeStruct((bs * SAMPLE_ROWS, ATTN_WIDTH), BF16),
        compiler_params=_cparams(("parallel", "arbitrary")),
    )(page_table, keys_past, keys_new, q_bf, k_bf, v_bf, *([cache_k] * pages), *([cache_v] * pages))


def _delta_prep_kernel(x_ref, halo_ref, prev_ref, misc_ref, cw_ref, al_ref, dt_ref,
                       qn_ref, kn_ref, vv_ref, bg_ref, xp_scr, *, tiles_per_seq, t_valid, tt):
    i = pl.program_id(0)
    tile_in_seq = i % tiles_per_seq
    halo = jnp.where(tile_in_seq == 0, prev_ref[0], halo_ref[...])
    xp_scr[0:SUBLANES, :] = halo
    xp_scr[SUBLANES:SUBLANES + tt, :] = x_ref[...]
    base = SUBLANES - (CONV_WIDTH - 1)
    outs = (qn_ref, kn_ref, vv_ref)
    for sec in range(3):
        for h in range(N_DELTA_HEADS):
            col = sec * DELTA_WIDTH + h * HEAD_DIM
            sl = slice(col, col + HEAD_DIM)
            y = xp_scr[base:base + tt, sl] * cw_ref[0:1, sl]
            for j in range(1, CONV_WIDTH):
                y = y + xp_scr[base + j:base + j + tt, sl] * cw_ref[j:j + 1, sl]
            y = _silu(y)
            if sec < 2:
                y = y * lax.rsqrt(jnp.sum(y * y, axis=-1, keepdims=True) + EPS)
            if sec == 0:
                y = y * (HEAD_DIM ** -0.5)
            outs[sec][:, h * HEAD_DIM:(h + 1) * HEAD_DIM] = y
    m = misc_ref[...]
    lane = lax.broadcasted_iota(I32, m.shape, 1)
    row = tile_in_seq * tt + lax.broadcasted_iota(I32, m.shape, 0)
    beta = _sigmoid(m)
    g = -jnp.exp(al_ref[...]) * _softplus(m + dt_ref[...])
    is_b = jnp.logical_and(lane >= M_BD, lane < M_BD + N_DELTA_HEADS)
    is_g = jnp.logical_and(lane >= M_AD, lane < M_AD + N_DELTA_HEADS)
    comb = jnp.where(is_b, beta, jnp.where(is_g, g, 0.0))
    comb = jnp.where(row < t_valid, comb, 0.0)
    bg_ref[...] = pltpu.roll(comb, LANES - M_BD, 1)


def _delta_prep(proj, prev8, conv_w, a_log, dt_bias, b, t, tt, t_valid):
    n = proj.shape[0]
    tiles_per_seq = t // tt
    pad_vec = lambda v: jnp.zeros((1, LANES), F32).at[0, M_AD:M_AD + N_DELTA_HEADS].set(v)
    halo_blocks = tt // SUBLANES
    return pl.pallas_call(
        functools.partial(_delta_prep_kernel, tiles_per_seq=tiles_per_seq, t_valid=t_valid, tt=tt),
        name="delta_prep",
        grid=(n // tt,),
        in_specs=[pl.BlockSpec((tt, CONV_CHANNELS), lambda i: (i, C_CONV // CONV_CHANNELS)),
                  pl.BlockSpec((SUBLANES, CONV_CHANNELS),
                               lambda i: (jnp.maximum(i * halo_blocks - 1, 0), C_CONV // CONV_CHANNELS)),
                  pl.BlockSpec((1, SUBLANES, CONV_CHANNELS), lambda i: (i // tiles_per_seq, 0, 0)),
                  pl.BlockSpec((tt, LANES), lambda i: (i, C_MISC // LANES)),
                  pl.BlockSpec((CONV_WIDTH, CONV_CHANNELS), lambda i: (0, 0)),
                  pl.BlockSpec((1, LANES), lambda i: (0, 0)),
                  pl.BlockSpec((1, LANES), lambda i: (0, 0))],
        out_specs=[pl.BlockSpec((tt, DELTA_WIDTH), lambda i: (i, 0))] * 3
                  + [pl.BlockSpec((tt, LANES), lambda i: (i, 0))],
        out_shape=[jax.ShapeDtypeStruct((n, DELTA_WIDTH), F32)] * 3 + [jax.ShapeDtypeStruct((n, LANES), F32)],
        scratch_shapes=[pltpu.VMEM((SUBLANES + tt, CONV_CHANNELS), F32)],
        compiler_params=_cparams(("parallel",)),
    )(proj, proj, prev8, proj, conv_w, pad_vec(a_log), pad_vec(dt_bias))


def _mm(a, b):
    return _dot(a.astype(BF16), b.astype(BF16))


def _mm_nt(a, b):
    return _dot_nt(a.astype(BF16), b.astype(BF16))


DELTA_INV_BLOCK = 16
DELTA_STACK = 4
DELTA_CHUNKS_PER_STEP = 2


def _delta_chunk_kernel(qn_ref, kn_ref, vv_ref, bg_ref, z_ref, s0_ref, ow_ref, od_ref, so_ref, s_scr, *, n_chunks):
    c = pl.program_id(1)
    cs = DELTA_CHUNK

    @pl.when(c == 0)
    def _():
        s_scr[...] = s0_ref[0]

    ltri = (lax.broadcasted_iota(I32, (cs, cs), 0) >= lax.broadcasted_iota(I32, (cs, cs), 1)).astype(BF16)

    def chunk_gates(ch):
        bg = bg_ref[ch * cs:(ch + 1) * cs, :]
        g1 = bg.astype(BF16)
        r1 = bg - g1.astype(F32)
        g2 = r1.astype(BF16)
        g3 = (r1 - g2.astype(F32)).astype(BF16)
        gc = _dot(ltri, g1) + _dot(ltri, g2) + _dot(ltri, g3)
        return bg, gc, gc.T

    gates = [chunk_gates(ch) for ch in range(n_chunks)]
    state_ready = {}

    gh = DELTA_STACK
    rows = gh * cs
    rr = lax.broadcasted_iota(I32, (rows, rows), 0)
    cc = lax.broadcasted_iota(I32, (rows, rows), 1)
    same = (rr // cs) == (cc // cs)
    causal = jnp.logical_and(same, rr >= cc)
    strict = jnp.logical_and(same, rr > cc)
    eye = (rr == cc).astype(F32)
    row_head = lax.broadcasted_iota(I32, (rows, 1), 0) // cs
    def group_steps(ch, grp):
        heads = [grp * gh + j for j in range(gh)]
        bg, gc, gct = gates[ch]
        r0 = ch * cs
        stack = lambda ref: jnp.concatenate([ref[r0:r0 + cs, h * HEAD_DIM:(h + 1) * HEAD_DIM] for h in heads],
                                            axis=0)
        col = lambda a, lane0: jnp.concatenate([a[:, lane0 + h:lane0 + h + 1] for h in heads], axis=0)
        k = stack(kn_ref)
        q = stack(qn_ref)
        v = stack(vv_ref)
        bcol = col(bg, 0)
        gcc = col(gc, N_DELTA_HEADS)
        gcr = jnp.concatenate([gct[N_DELTA_HEADS + h:N_DELTA_HEADS + h + 1, :] for h in heads], axis=1)
        g_last = [gc[cs - 1:cs, N_DELTA_HEADS + h:N_DELTA_HEADS + h + 1] for h in heads]
        glc = jnp.concatenate([jnp.broadcast_to(gl, (cs, 1)) for gl in g_last], axis=0)
        decay = jnp.exp(jnp.where(causal, gcc - gcr, -jnp.inf))
        kb = k * bcol
        eg = jnp.exp(gcc)
        kq = _mm_nt(jnp.concatenate([kb, q], axis=0), k)
        yield
        a = jnp.where(strict, kq[0:rows] * decay, 0.0)
        intra = jnp.where(causal, kq[rows:2 * rows] * decay, 0.0)
        x = -a
        nb = DELTA_INV_BLOCK
        y = jnp.where((rr // nb) == (cc // nb), x, 0.0)
        p = eye + y
        y = _mm(y, y)
        yield
        n_sq = max(1, int(nb - 1).bit_length())
        for lvl in range(1, n_sq):
            if lvl < n_sq - 1:
                py = _mm(jnp.concatenate([p, y], axis=0), y)
                p = p + py[0:rows]
                y = py[rows:2 * rows]
            else:
                p = p + _mm(p, y)
            yield
        size = 2 * nb
        while size <= cs:
            off = jnp.where(jnp.logical_and((rr // size) == (cc // size), (rr // (size // 2)) != (cc // (size // 2))),
                            x, 0.0)
            po = _mm(p, off)
            yield
            p = p + _mm(po, p)
            yield
            size *= 2
        sol = _mm(p, jnp.concatenate([v * bcol, kb * eg], axis=1))
        yield
        u = sol[:, 0:HEAD_DIM]
        w = sol[:, HEAD_DIM:2 * HEAD_DIM]
        lanes_g = slice(grp * gh * HEAD_DIM, (grp + 1) * gh * HEAD_DIM)
        while ch > 0 and not state_ready.get((ch - 1, grp)):
            yield
        s_g = s_scr[:, lanes_g]
        wq_s = _mm(jnp.concatenate([w, q * eg], axis=0), s_g)
        yield
        own = lambda m, r0: jnp.concatenate(
            [m[r0 + j * cs:r0 + (j + 1) * cs, j * HEAD_DIM:(j + 1) * HEAD_DIM] for j in range(gh)], axis=0)
        v_new = u - own(wq_s, 0)
        o = own(wq_s, rows) + _mm(intra, v_new)
        yield
        kg_t = (k * jnp.exp(glc - gcc)).T
        vn_blocks = jnp.concatenate([jnp.where(row_head == j, v_new, 0.0) for j in range(gh)], axis=1)
        s_decay = jnp.concatenate([jnp.broadcast_to(jnp.exp(gl), (1, HEAD_DIM)) for gl in g_last], axis=1)
        s_scr[:, lanes_g] = s_g * s_decay + _mm(kg_t, vn_blocks)
        state_ready[(ch, grp)] = True
        yield
        on = o * lax.rsqrt(jnp.mean(o * o, axis=-1, keepdims=True) + EPS) * ow_ref[...]
        for j, h in enumerate(heads):
            sl = slice(h * HEAD_DIM, (h + 1) * HEAD_DIM)
            od_ref[r0:r0 + cs, sl] = (on[j * cs:(j + 1) * cs] * _silu(z_ref[r0:r0 + cs, sl])).astype(BF16)

    chains = [group_steps(ch, grp) for ch in range(n_chunks) for grp in range(N_DELTA_HEADS // gh)]
    while chains:
        chains = [g for g in chains if next(g, "done") != "done"]

    so_ref[0] = s_scr[...]


def _delta_chunks(qn, kn, vv, bg, zsrc, z_col_block, state0, o_norm_w, b, t):
    n = b * t
    per_step = DELTA_CHUNKS_PER_STEP if (t // DELTA_CHUNK) % DELTA_CHUNKS_PER_STEP == 0 else 1
    nc = t // (DELTA_CHUNK * per_step)
    sw = N_DELTA_HEADS * HEAD_DIM
    row = lambda w, cb=0: pl.BlockSpec((DELTA_CHUNK * per_step, w), lambda bb, c: (bb * nc + c, cb))
    st = pl.BlockSpec((1, HEAD_DIM, sw), lambda bb, c: (bb, 0, 0))
    s_in = state0.transpose(0, 2, 1, 3).reshape(b, HEAD_DIM, sw)
    od, s_out = pl.pallas_call(
        functools.partial(_delta_chunk_kernel, n_chunks=per_step),
        name="delta_chunks",
        grid=(b, nc),
        in_specs=[row(DELTA_WIDTH), row(DELTA_WIDTH), row(DELTA_WIDTH), row(LANES),
                  row(DELTA_WIDTH, z_col_block), st, pl.BlockSpec((1, LANES), lambda bb, c: (0, 0))],
        out_specs=[row(DELTA_WIDTH), st],
        out_shape=[jax.ShapeDtypeStruct((n, DELTA_WIDTH), BF16),
                   jax.ShapeDtypeStruct((b, HEAD_DIM, sw), F32)],
        scratch_shapes=[pltpu.VMEM((HEAD_DIM, sw), F32)],
        compiler_params=_cparams(("parallel", "arbitrary")),
    )(qn, kn, vv, bg, zsrc, s_in, o_norm_w.reshape(1, LANES))
    return od, s_out.reshape(b, HEAD_DIM, N_DELTA_HEADS, HEAD_DIM).transpose(0, 2, 1, 3)


def _outproj_kernel(oa_ref, od_ref, x_ref, g1_ref, sh_ref, sc_ref, nw_ref, wo_ref, wrh_ref, wrl_ref, br_ref,
                    x1_ref, h2_ref, lg_ref):
    mix = _dot(oa_ref[...], wo_ref[0:ATTN_WIDTH, :]) + _dot(od_ref[...], wo_ref[ATTN_WIDTH:ATTN_WIDTH + DELTA_WIDTH, :])
    x1 = x_ref[...] + g1_ref[0] * mix
    x1_ref[...] = x1
    y = x1 * lax.rsqrt(jnp.mean(x1 * x1, axis=-1, keepdims=True) + EPS) * nw_ref[...]
    h2 = y * (1.0 + sc_ref[0]) + sh_ref[0]
    h2_ref[...] = h2
    hb = h2.astype(BF16)
    lo = (h2 - hb.astype(F32)).astype(BF16)
    lg_ref[...] = _dot(hb, wrh_ref[...]) + _dot(lo, wrh_ref[...]) + _dot(hb, wrl_ref[...]) + br_ref[...]


def _out_projection(o_attn, o_delta, x2d, gate1, shift2, scale2, norm2_w, w_out_bf, wr_hi, wr_lo, b_rt,
                    tm, rows_per_mod_block):
    n, d = x2d.shape
    r = gate1.shape[1]
    tiles_per_mod = rows_per_mod_block // tm
    mod_spec = pl.BlockSpec((1, r, d), lambda i: (i // tiles_per_mod, 0, 0))
    row = lambda w: pl.BlockSpec((tm, w), lambda i: (i, 0))
    full = lambda a: pl.BlockSpec(a.shape, lambda i: (0, 0))
    return pl.pallas_call(
        _outproj_kernel,
        name="out_proj",
        grid=(n // tm,),
        in_specs=[row(ATTN_WIDTH), row(DELTA_WIDTH), row(d), mod_spec, mod_spec, mod_spec,
                  pl.BlockSpec((1, d), lambda i: (0, 0)), full(w_out_bf), full(wr_hi), full(wr_lo), full(b_rt)],
        out_specs=[row(d), row(d), row(LANES)],
        out_shape=[jax.ShapeDtypeStruct((n, d), F32), jax.ShapeDtypeStruct((n, d), F32),
                   jax.ShapeDtypeStruct((n, LANES), F32)],
        compiler_params=_cparams(("parallel",)),
    )(o_attn, o_delta, x2d, gate1, shift2, scale2, norm2_w.reshape(1, d), w_out_bf, wr_hi, wr_lo, b_rt)


def _route_kernel(lg_ref, eid_ref, gate_ref):
    x = lg_ref[...]
    lane = lax.broadcasted_iota(I32, x.shape, 1)
    gl = jnp.where(lane < N_GROUPS, x, -jnp.inf)
    ge = jnp.exp(gl - jnp.max(gl, axis=1, keepdims=True))
    p = ge / jnp.sum(ge, axis=1, keepdims=True)
    p_max = jnp.max(p, axis=1, keepdims=True)
    grp = jnp.min(jnp.where(p == p_max, lane, LANES), axis=1, keepdims=True)
    e_lane = lane - N_GROUPS
    in_grp = jnp.logical_and(jnp.logical_and(e_lane >= 0, e_lane < N_EXPERTS),
                             (e_lane >> 3) == grp)
    rl = jnp.where(in_grp, x, -jnp.inf)
    v1 = jnp.max(rl, axis=1, keepdims=True)
    i1 = jnp.min(jnp.where(rl == v1, lane, LANES), axis=1, keepdims=True)
    rl2 = jnp.where(lane == i1, -jnp.inf, rl)
    v2 = jnp.max(rl2, axis=1, keepdims=True)
    i2 = jnp.min(jnp.where(rl2 == v2, lane, LANES), axis=1, keepdims=True)
    t = jnp.exp(v2 - v1)
    den = 1.0 + t
    eid_ref[...] = jnp.where(lane == 0, i1 - N_GROUPS, jnp.where(lane == 1, i2 - N_GROUPS, 0))
    gate_ref[...] = jnp.where(lane == 0, (1.0 / den) * p_max, jnp.where(lane == 1, (t / den) * p_max, 0.0))


def _route(logits, tm):
    n = logits.shape[0]
    spec = pl.BlockSpec((tm, LANES), lambda i: (i, 0))
    return pl.pallas_call(
        _route_kernel,
        name="route",
        grid=(n // tm,),
        in_specs=[spec],
        out_specs=[spec, spec],
        out_shape=[jax.ShapeDtypeStruct((n, LANES), I32), jax.ShapeDtypeStruct((n, LANES), F32)],
        compiler_params=_cparams(("parallel",)),
    )(logits)


def _row_gather(idx_ref, base, n_rows, src_hbm, dst, sem):
    def body(r, carry):
        pltpu.make_async_copy(src_hbm.at[pl.ds(idx_ref[base + r], 1), :], dst.at[pl.ds(r, 1), :], sem).start()
        return carry

    lax.fori_loop(0, n_rows, body, 0, unroll=8)


def _row_gather_wait(n_rows, src_hbm, dst, sem):
    pltpu.make_async_copy(src_hbm.at[pl.ds(0, n_rows), :], dst, sem).wait()


def _moe_kernel(tok_ref, j0_ref, be_ref, na_ref, h_hbm, wg_ref, wu_ref, wd_ref, o_ref,
                x_even, x_odd, sem, wg_scr, wu_scr, wd_scr, *, bm):
    i = pl.program_id(0)
    n_act = na_ref[0]
    bufs = ((x_even, sem.at[0]), (x_odd, sem.at[1]))

    @pl.when(i == 0)
    def _():
        _row_gather(tok_ref, j0_ref[0], bm, h_hbm, x_even, sem.at[0])

    changed = jnp.logical_or(i == 0, be_ref[i] != be_ref[jnp.maximum(i - 1, 0)])

    @pl.when(jnp.logical_and(i < n_act, changed))
    def _():
        wg_scr[...] = wg_ref[0].astype(BF16)
        wu_scr[...] = wu_ref[0].astype(BF16)
        wd_scr[...] = wd_ref[0].astype(BF16)

    for parity in range(2):
        cur, cur_sem = bufs[parity]
        nxt, nxt_sem = bufs[1 - parity]

        @pl.when(jnp.logical_and(i < n_act, i % 2 == parity))
        def _():
            _row_gather_wait(bm, h_hbm, cur, cur_sem)
            base = j0_ref[i + 1]
            for r in range(bm):
                pltpu.make_async_copy(h_hbm.at[pl.ds(tok_ref[base + r], 1), :], nxt.at[pl.ds(r, 1), :],
                                      nxt_sem).start()
            x = cur[...].astype(BF16)
            hid = _silu(_dot(x, wg_scr[...])) * _dot(x, wu_scr[...])
            o_ref[...] = _dot(hid.astype(BF16), wd_scr[...])

        @pl.when(jnp.logical_and(i == n_act, i % 2 == parity))
        def _():
            _row_gather_wait(bm, h_hbm, cur, cur_sem)

    @pl.when(i >= n_act)
    def _():
        o_ref[...] = jnp.zeros(o_ref.shape, F32)


def _moe_experts(tok_sorted, block_j0, block_exp, n_active, h2, w_gate, w_up, w_down, bm):
    ns = block_exp.shape[0] * bm
    d = h2.shape[1]
    f = w_gate.shape[2]
    grid_spec = pltpu.PrefetchScalarGridSpec(
        num_scalar_prefetch=4,
        grid=(ns // bm,),
        in_specs=[pl.BlockSpec(memory_space=pl.ANY),
                  pl.BlockSpec((1, d, f), lambda i, tok, j0, be, na: (be[i], 0, 0)),
                  pl.BlockSpec((1, d, f), lambda i, tok, j0, be, na: (be[i], 0, 0)),
                  pl.BlockSpec((1, f, d), lambda i, tok, j0, be, na: (be[i], 0, 0))],
        out_specs=pl.BlockSpec((bm, d), lambda i, tok, j0, be, na: (i, 0)),
        scratch_shapes=[pltpu.VMEM((bm, d), F32), pltpu.VMEM((bm, d), F32), pltpu.SemaphoreType.DMA((2,)),
                        pltpu.VMEM((d, f), BF16), pltpu.VMEM((d, f), BF16), pltpu.VMEM((f, d), BF16)],
    )
    return pl.pallas_call(
        functools.partial(_moe_kernel, bm=bm),
        name="moe_experts",
        grid_spec=grid_spec,
        out_shape=jax.ShapeDtypeStruct((ns, d), F32),
        compiler_params=_cparams(("arbitrary",)),
    )(tok_sorted, block_j0, block_exp, n_active, h2, w_gate, w_up, w_down)


def _combine_kernel(dest_ref, x1_ref, gt_ref, g2_ref, y_hbm, o_ref, y_buf, sem, *, tm):
    i = pl.program_id(0)
    n = pl.num_programs(0)
    slot = i % 2

    @pl.when(i == 0)
    def _():
        _row_gather(dest_ref, 0, 2 * tm, y_hbm, y_buf.at[0], sem.at[0])

    @pl.when(i + 1 < n)
    def _():
        _row_gather(dest_ref, (i + 1) * 2 * tm, 2 * tm, y_hbm, y_buf.at[1 - slot], sem.at[1 - slot])

    _row_gather_wait(2 * tm, y_hbm, y_buf.at[slot], sem.at[slot])
    gt = gt_ref[...]
    y = y_buf[slot, 0:tm, :] * gt[:, 0:1] + y_buf[slot, tm:2 * tm, :] * gt[:, 1:2]
    o_ref[...] = x1_ref[...] + g2_ref[0] * y


def _combine(x1, dest, y_rows, gates, gate2, tm, rows_per_mod_block):
    n, d = x1.shape
    r = gate2.shape[1]
    tiles_per_mod = rows_per_mod_block // tm
    dest_tiles = dest.reshape(n // tm, tm, 2).transpose(0, 2, 1).reshape(-1)
    row = lambda w: pl.BlockSpec((tm, w), lambda i, dst: (i, 0))
    grid_spec = pltpu.PrefetchScalarGridSpec(
        num_scalar_prefetch=1,
        grid=(n // tm,),
        in_specs=[row(d), row(LANES),
                  pl.BlockSpec((1, r, d), lambda i, dst: (i // tiles_per_mod, 0, 0)),
                  pl.BlockSpec(memory_space=pl.ANY)],
        out_specs=row(d),
        scratch_shapes=[pltpu.VMEM((2, 2 * tm, d), F32), pltpu.SemaphoreType.DMA((2,))],
    )
    return pl.pallas_call(
        functools.partial(_combine_kernel, tm=tm),
        name="moe_combine",
        grid_spec=grid_spec,
        out_shape=jax.ShapeDtypeStruct((n, d), F32),
        compiler_params=_cparams(("arbitrary",)),
    )(dest_tiles, x1, gates, gate2, y_rows)


def _pick_tile(n, pref, mult=16):
    t = min(pref, n)
    while n % t or t % mult:
        t -= 1
    return t


def _pack_w_in(w_in):
    d = w_in.shape[0]
    bounds = np.cumsum(PROJ_SIZES)[:-1].tolist()
    qa, ka, va, qi, ki, wi, qd, kd, vd, zd, bd, ad = jnp.split(w_in, bounds, axis=1)
    used = IDX_DIM + N_IDX_HEADS + 2 * N_DELTA_HEADS
    misc = jnp.concatenate([ki, wi, bd, ad, jnp.zeros((d, LANES - used), w_in.dtype)], axis=1)
    cols = [qa, qi, zd, qd, kd, vd, ka, va, misc]
    width = sum(c.shape[1] for c in cols)
    cols.append(jnp.zeros((d, PROJ_PACKED - width), w_in.dtype))
    return jnp.concatenate(cols, axis=1).astype(BF16)


def _route_and_sort(eid, bm):
    n = eid.shape[0]
    nk = 2 * n
    flat_e = eid.reshape(-1)
    order = jnp.argsort(flat_e, stable=True).astype(I32)
    inv = jnp.argsort(order).astype(I32)
    onehot = flat_e[:, None] == jnp.arange(N_EXPERTS, dtype=I32)[None, :]
    counts = jnp.sum(onehot.astype(I32), axis=0)
    padded = (counts + bm - 1) // bm * bm
    pad_end = jnp.cumsum(padded)
    shift = (pad_end - padded) - (jnp.cumsum(counts) - counts)
    dest = inv + jnp.sum(jnp.where(onehot, shift[None, :], 0), axis=1)
    n_blocks = -(-nk // bm) + N_EXPERTS + 1
    block_exp = jnp.minimum(jnp.searchsorted(pad_end, jnp.arange(n_blocks, dtype=I32) * bm, side='right'),
                            N_EXPERTS - 1).astype(I32)
    n_active = (pad_end[-1] // bm).astype(I32).reshape(1)
    block_j0 = jnp.clip(jnp.arange(n_blocks, dtype=I32) * bm - shift[block_exp], 0, nk)
    tok_sorted = jnp.concatenate([order // 2, jnp.zeros((bm,), I32)])
    return tok_sorted, block_j0, dest.astype(I32).reshape(n, 2), block_exp, n_active


def _layer(layer, yp, ys, cache_k, cache_v, cache_idx, state_ssm, state_conv, page_table, c_prompt, c_sample,
           w_in, w_out, conv_w, a_log, dt_bias, q_norm_w, k_norm_w, idx_k_norm_w, o_norm_w, norm1_w, norm2_w,
           w_ada, b_ada, w_group, b_group, w_router, b_router, w_gate, w_up, w_down):
    bp, tp, d = yp.shape
    bs, ts, _ = ys.shape
    past = page_table.shape[1] * PAGE_SIZE
    rows = SAMPLE_ROWS
    assert CONV_WIDTH - 1 <= ts <= rows and tp % SEL_SPAN == 0 and tp % DELTA_CHUNK == 0

    n_c = bp + bs
    n_c_pad = -(-n_c // SUBLANES) * SUBLANES
    c_all = jnp.concatenate([c_prompt, c_sample, jnp.zeros((n_c_pad - n_c, d), F32)], axis=0)
    mod = _ada_modulation(c_all, w_ada, b_ada)
    mods = jnp.split(mod, N_MOD, axis=1)
    mp = [m[:bp].reshape(bp, 1, d) for m in mods]
    ms = [jnp.repeat(m[bp:bp + bs], rows, axis=0).reshape(1, bs * rows, d) for m in mods]

    w_packed = _pack_w_in(w_in)
    w_out_bf = w_out.astype(BF16)
    w_rt = jnp.concatenate([w_group, w_router, jnp.zeros((d, LANES - N_GROUPS - N_EXPERTS), F32)], axis=1)
    wr_hi = w_rt.astype(BF16)
    wr_lo = (w_rt - wr_hi.astype(F32)).astype(BF16)
    b_rt = jnp.concatenate([b_group, b_router, jnp.zeros((LANES - N_GROUPS - N_EXPERTS,), F32)]).reshape(1, LANES)

    np_ = bp * tp
    xp2 = yp.reshape(np_, d)
    tm_p = _pick_tile(tp, 256)
    proj_p = _in_projection(xp2, mp[0], mp[1], norm1_w, w_packed, tm_p, tp)
    tq_p = _pick_tile(tp, 256)
    q_p, kf_p, kb_p, vt_p, qi_p, kif_p, kx_p = _attention_prep(
        proj_p, jnp.arange(tp), tq_p, q_norm_w, k_norm_w, idx_k_norm_w, True)
    oa_p = _dsa_prompt(q_p, qi_p, proj_p, kb_p, vt_p, kx_p, bp, tp)
    tt_p = _pick_tile(tp, 256)
    qn_p, kn_p, vv_p, bg_p = _delta_prep(proj_p, jnp.zeros((bp, SUBLANES, CONV_CHANNELS), F32), conv_w,
                                         a_log, dt_bias, bp, tp, tt_p, tp)
    od_p, ssm_p = _delta_chunks(qn_p, kn_p, vv_p, bg_p, proj_p, C_ZD // DELTA_WIDTH,
                                jnp.zeros((bp, N_DELTA_HEADS, HEAD_DIM, HEAD_DIM), F32), o_norm_w, bp, tp)
    tm_o = _pick_tile(tp, 256)
    x1_p, h2_p, lg_p = _out_projection(oa_p, od_p, xp2, mp[2], mp[3], mp[4], norm2_w, w_out_bf, wr_hi, wr_lo, b_rt,
                                       tm_o, tp)

    ns_ = bs * rows
    xs2 = jnp.pad(ys, ((0, 0), (0, rows - ts), (0, 0))).reshape(ns_, d)
    tm_s = _pick_tile(ns_, 256)
    proj_s = _in_projection(xs2, ms[0].reshape(ns_ // tm_s, tm_s, d), ms[1].reshape(ns_ // tm_s, tm_s, d),
                            norm1_w, w_packed, tm_s, tm_s)
    q_s, kf_s, kb_s, vb_s, qi_s, kif_s, _ = _attention_prep(
        proj_s, past + jnp.arange(rows), rows, q_norm_w, k_norm_w, idx_k_norm_w, False)
    q_t = qi_s.reshape(bs, rows, N_IDX_HEADS, IDX_DIM).transpose(0, 2, 1, 3).reshape(bs, N_IDX_HEADS * rows, IDX_DIM)
    w_col = proj_s[:, C_MISC + M_WI:C_MISC + M_WI + N_IDX_HEADS].reshape(bs, rows, N_IDX_HEADS)
    w_col = w_col.transpose(0, 2, 1).reshape(bs, N_IDX_HEADS * rows, 1)
    pages = _pick_tile(page_table.shape[1], 8, 1)
    keys_past, keys_new = _sample_scores(page_table, q_t, w_col, kif_s, cache_idx, layer, pages, ts)
    n_sel_s = min(TOPK_MAX, (past + ts) // 4)
    oa_s = _sample_attend(page_table, keys_past, keys_new, q_s, kb_s, vb_s, cache_k, cache_v, layer, pages, n_sel_s)
    prev8 = jnp.pad(state_conv, ((0, 0), (SUBLANES - (CONV_WIDTH - 1), 0), (0, 0)))
    qn_s, kn_s, vv_s, bg_s = _delta_prep(proj_s, prev8, conv_w, a_log, dt_bias, bs, rows, rows, ts)
    to_chunk = lambda a: jnp.pad(a.reshape(bs, rows, -1), ((0, 0), (0, DELTA_CHUNK - rows), (0, 0))).reshape(
        bs * DELTA_CHUNK, -1)
    z_s = proj_s[:, C_ZD:C_ZD + DELTA_WIDTH]
    od_s, ssm_s = _delta_chunks(to_chunk(qn_s), to_chunk(kn_s), to_chunk(vv_s), to_chunk(bg_s), to_chunk(z_s), 0,
                                state_ssm, o_norm_w, bs, DELTA_CHUNK)
    od_s = od_s.reshape(bs, DELTA_CHUNK, DELTA_WIDTH)[:, :rows].reshape(ns_, DELTA_WIDTH)
    x1_s, h2_s, lg_s = _out_projection(oa_s, od_s, xs2, ms[2], ms[3], ms[4], norm2_w, w_out_bf, wr_hi, wr_lo, b_rt,
                                       ns_, ns_)

    n_all = np_ + ns_
    h2_all = jnp.concatenate([h2_p, h2_s], axis=0)
    lg_all = jnp.concatenate([lg_p, lg_s], axis=0)
    eid, gates = _route(lg_all, _pick_tile(n_all, 512, SUBLANES))
    bm = 256
    tok_sorted, block_j0, dest, block_exp, n_active = _route_and_sort(eid[:, 0:2], bm)
    yb = _moe_experts(tok_sorted, block_j0, block_exp, n_active, h2_all, w_gate, w_up, w_down, bm)
    out_p = _combine(x1_p, dest[:np_], yb, gates[:np_], mp[5], tm_o, tp)
    out_s = _combine(x1_s, dest[np_:], yb, gates[np_:], ms[5], ns_, ns_)

    valid = lambda a: a.reshape(bs, rows, -1)[:, :ts]
    conv_p = proj_p.reshape(bp, tp, PROJ_PACKED)[:, tp - (CONV_WIDTH - 1):, C_CONV:C_CONV + CONV_CHANNELS]
    conv_s = proj_s.reshape(bs, rows, PROJ_PACKED)[:, ts - (CONV_WIDTH - 1):ts, C_CONV:C_CONV + CONV_CHANNELS]
    return (out_p.reshape(bp, tp, d), valid(out_s),
            kf_p.reshape(bp, tp, N_KV_HEADS, HEAD_DIM),
            proj_p[:, C_VA:C_VA + KV_WIDTH].reshape(bp, tp, N_KV_HEADS, HEAD_DIM),
            kif_p.reshape(bp, tp, IDX_DIM), ssm_p, conv_p,
            valid(kf_s).reshape(bs, ts, N_KV_HEADS, HEAD_DIM),
            valid(proj_s[:, C_VA:C_VA + KV_WIDTH]).reshape(bs, ts, N_KV_HEADS, HEAD_DIM),
            valid(kif_s), ssm_s, conv_s)


def kernel(x_prompt, x_sample, cache_k, cache_v, cache_idx_k, state_ssm, state_conv, page_table, c_prompt, c_sample,
           w_in, w_out, conv_w, a_log, dt_bias, q_norm_w, k_norm_w, idx_k_norm_w, o_norm_w, norm1_w, norm2_w,
           w_ada, b_ada, w_group, b_group, w_router, b_router, w_gate, w_up, w_down):
    depth = w_in.shape[0]
    yp, ys = x_prompt, x_sample
    per_layer = []
    for l in range(depth):
        res = _layer(l, yp, ys, cache_k, cache_v, cache_idx_k, state_ssm[l], state_conv[l], page_table,
                     c_prompt, c_sample, w_in[l], w_out[l], conv_w[l], a_log[l], dt_bias[l], q_norm_w[l],
                     k_norm_w[l], idx_k_norm_w[l], o_norm_w[l], norm1_w[l], norm2_w[l], w_ada[l], b_ada[l],
                     w_group[l], b_group[l], w_router[l], b_router[l], w_gate[l], w_up[l], w_down[l])
        yp, ys = res[0], res[1]
        per_layer.append(res[2:])
    stacked = tuple(jnp.stack([pl_[j] for pl_ in per_layer]) for j in range(10))
    return (yp, ys) + stacked
```

```python
import functools

import jax
import jax.numpy as jnp
import numpy as np
from jax import lax
from jax.experimental import pallas as pl
from jax.experimental.pallas import tpu as pltpu

F32 = jnp.float32
BF16 = jnp.bfloat16
I32 = jnp.int32

HEAD_DIM = 128
N_ATTN_HEADS = 8
N_KV_HEADS = 2
KV_GROUP = N_ATTN_HEADS // N_KV_HEADS
N_DELTA_HEADS = 8
N_IDX_HEADS = 16
IDX_DIM = 64
ATTN_WIDTH = N_ATTN_HEADS * HEAD_DIM
KV_WIDTH = N_KV_HEADS * HEAD_DIM
DELTA_WIDTH = N_DELTA_HEADS * HEAD_DIM
IDX_WIDTH = N_IDX_HEADS * IDX_DIM
CONV_CHANNELS = 3 * DELTA_WIDTH
TOPK_MAX = 256
ROPE_THETA = 500000.0
ROPE_FRACTION = 4
CONV_WIDTH = 4
DELTA_CHUNK = 64
N_GROUPS = 8
EXPERTS_PER_GROUP = 8
N_EXPERTS = N_GROUPS * EXPERTS_PER_GROUP
N_MOD = 6
EPS = 1e-6
PAGE_SIZE = 128
PROJ_SIZES = (ATTN_WIDTH, KV_WIDTH, KV_WIDTH, IDX_WIDTH, IDX_DIM, N_IDX_HEADS,
              DELTA_WIDTH, DELTA_WIDTH, DELTA_WIDTH, DELTA_WIDTH, N_DELTA_HEADS, N_DELTA_HEADS)

LANES = 128
SUBLANES = 8
VMEM_LIMIT = 56 * 1024 * 1024

C_QA = 0
C_QI = 1024
C_ZD = 2048
C_CONV = 3072
C_KA = 6144
C_VA = 6400
C_MISC = 6656
PROJ_PACKED = 6912
M_KI = 0
M_WI = 64
M_BD = 80
M_AD = 88

Q_TILE = 128
KEY_CHUNK = 256
SEL_SPAN = 512
NEG_INF_KEY = -2139095041
SAMPLE_ROWS = 16
NEG_BIG = -1e30
INT_MIN = -2147483648
INT_MAX = 2147483647


def _cparams(sem):
    return pltpu.CompilerParams(dimension_semantics=sem, vmem_limit_bytes=VMEM_LIMIT)


def _dot(a, b):
    return jnp.dot(a, b, preferred_element_type=F32)


def _dot_nt(a, b):
    return lax.dot_general(a, b, (((1,), (1,)), ((), ())), preferred_element_type=F32)


def _dot_tn(a, b):
    return lax.dot_general(a, b, (((0,), (0,)), ((), ())), preferred_element_type=F32)


def _sigmoid(x):
    return 1.0 / (1.0 + jnp.exp(-x))


def _silu(x):
    return x * _sigmoid(x)


def _softplus(x):
    return jnp.maximum(x, 0.0) + jnp.log(1.0 + jnp.exp(-jnp.abs(x)))


def _ada_kernel(c_ref, w_ref, b_ref, o_ref):
    s = _silu(c_ref[...]).astype(BF16)
    o_ref[...] = _dot(s, w_ref[...].astype(BF16)) + b_ref[...]


def _ada_modulation(c, w_ada, b_ada):
    r, d = c.shape
    n = w_ada.shape[1]
    tn = 1024 if n % 1024 == 0 else n
    return pl.pallas_call(
        _ada_kernel,
        name="ada_mod",
        grid=(n // tn,),
        in_specs=[pl.BlockSpec((r, d), lambda j: (0, 0)),
                  pl.BlockSpec((d, tn), lambda j: (0, j)),
                  pl.BlockSpec((1, tn), lambda j: (0, j))],
        out_specs=pl.BlockSpec((r, tn), lambda j: (0, j)),
        out_shape=jax.ShapeDtypeStruct((r, n), F32),
        compiler_params=_cparams(("parallel",)),
    )(c, w_ada, b_ada.reshape(1, n))


INPROJ_COLS = 1152


def _inproj_kernel(x_ref, sh_ref, sc_ref, nw_ref, w_ref, o_ref):
    x = x_ref[...]
    y = x * lax.rsqrt(jnp.mean(x * x, axis=-1, keepdims=True) + EPS) * nw_ref[...]
    h = (y * (1.0 + sc_ref[0]) + sh_ref[0]).astype(BF16)
    for c0 in range(0, o_ref.shape[1], INPROJ_COLS):
        o_ref[:, c0:c0 + INPROJ_COLS] = _dot(h, w_ref[:, c0:c0 + INPROJ_COLS])


def _in_projection(x2d, shift, scale, norm_w, w_packed, tm, rows_per_mod_block):
    n, d = x2d.shape
    np_ = w_packed.shape[1]
    r = shift.shape[1]
    tiles_per_mod = rows_per_mod_block // tm
    mod_spec = pl.BlockSpec((1, r, d), lambda i: (i // tiles_per_mod, 0, 0))
    return pl.pallas_call(
        _inproj_kernel,
        name="in_proj",
        grid=(n // tm,),
        in_specs=[pl.BlockSpec((tm, d), lambda i: (i, 0)),
                  mod_spec, mod_spec,
                  pl.BlockSpec((1, d), lambda i: (0, 0)),
                  pl.BlockSpec((d, np_), lambda i: (0, 0), pipeline_mode=pl.Buffered(1))],
        out_specs=pl.BlockSpec((tm, np_), lambda i: (i, 0)),
        out_shape=jax.ShapeDtypeStruct((n, np_), F32),
        compiler_params=_cparams(("parallel",)),
    )(x2d, shift, scale, norm_w.reshape(1, d), w_packed)


def _rope(x, tab, rot):
    c = tab[:, 0:LANES]
    s1 = tab[:, LANES:2 * LANES]
    s2 = tab[:, 2 * LANES:3 * LANES]
    return x * c + pltpu.roll(x, LANES - rot, 1) * s1 + pltpu.roll(x, rot, 1) * s2


def _rms_head(x, w):
    return x * lax.rsqrt(jnp.mean(x * x, axis=-1, keepdims=True) + EPS) * w


def _prep_kernel(qa_ref, qi_ref, ka_ref, va_ref, misc_ref, tabm_ref, tabi_ref, qw_ref, kw_ref, iw_ref,
                 q_ref, kf_ref, kb_ref, vb_ref, qib_ref, kif_ref, kib_ref, *, transpose_v):
    tabm = tabm_ref[0]
    tabi = tabi_ref[0]
    half_main = HEAD_DIM // ROPE_FRACTION // 2
    half_idx = IDX_DIM // ROPE_FRACTION // 2
    for h in range(N_ATTN_HEADS):
        sl = slice(h * HEAD_DIM, (h + 1) * HEAD_DIM)
        y = _rope(_rms_head(qa_ref[:, sl], qw_ref[...]), tabm, half_main)
        q_ref[:, sl] = (y * (HEAD_DIM ** -0.5)).astype(BF16)
    for h in range(N_KV_HEADS):
        sl = slice(h * HEAD_DIM, (h + 1) * HEAD_DIM)
        y = _rope(_rms_head(ka_ref[:, sl], kw_ref[...]), tabm, half_main)
        kf_ref[:, sl] = y
        kb_ref[:, sl] = y.astype(BF16)
    if transpose_v:
        vb_ref[...] = va_ref[...].T.astype(BF16)
    else:
        vb_ref[...] = va_ref[...].astype(BF16)
    for p in range(IDX_WIDTH // LANES):
        sl = slice(p * LANES, (p + 1) * LANES)
        qib_ref[:, sl] = _rope(qi_ref[:, sl], tabi, half_idx).astype(BF16)
    m = misc_ref[...]
    lane = lax.broadcasted_iota(I32, m.shape, 1)
    ki = jnp.where(lane < IDX_DIM, m, 0.0)
    ms = jnp.sum(ki * ki, axis=-1, keepdims=True) * (1.0 / IDX_DIM)
    y = _rope(ki * lax.rsqrt(ms + EPS) * iw_ref[...], tabi, half_idx)
    kif_ref[...] = y[:, 0:IDX_DIM]
    kib_ref[...] = (y + pltpu.roll(y, IDX_DIM, 1)).astype(BF16)


def _rope_tables(pos, head_dim, group):
    d_rot = head_dim // ROPE_FRACTION
    half = d_rot // 2
    inv_freq = jnp.power(ROPE_THETA, -(jnp.arange(half, dtype=F32) * 2.0 / d_rot))
    ang = pos.astype(F32)[:, None] * inv_freq[None, :]
    cos = jnp.cos(ang)
    sin = jnp.sin(ang)
    t = pos.shape[0]
    z = jnp.zeros((t, group - d_rot), F32)
    c = jnp.concatenate([cos, cos, jnp.ones((t, group - d_rot), F32)], axis=1)
    s1 = jnp.concatenate([-sin, jnp.zeros((t, half), F32), z], axis=1)
    s2 = jnp.concatenate([jnp.zeros((t, half), F32), sin, z], axis=1)
    rep = LANES // group
    return jnp.concatenate([jnp.tile(c, (1, rep)), jnp.tile(s1, (1, rep)), jnp.tile(s2, (1, rep))], axis=1)


def _attention_prep(proj, pos, tq, q_norm_w, k_norm_w, idx_k_norm_w, transpose_v):
    n = proj.shape[0]
    p = pos.shape[0]
    g = p // tq
    tabm = _rope_tables(pos, HEAD_DIM, LANES).reshape(g, tq, 3 * LANES)
    tabi = _rope_tables(pos, IDX_DIM, IDX_DIM).reshape(g, tq, 3 * LANES)
    iw = jnp.concatenate([idx_k_norm_w, jnp.zeros((LANES - IDX_DIM,), F32)]).reshape(1, LANES)
    row = lambda w, c: pl.BlockSpec((tq, w), lambda i: (i, c // w))
    tab_spec = pl.BlockSpec((1, tq, 3 * LANES), lambda i: (i % g, 0, 0))
    vec_spec = pl.BlockSpec((1, LANES), lambda i: (0, 0))
    out_row = lambda w: pl.BlockSpec((tq, w), lambda i: (i, 0))
    v_spec = pl.BlockSpec((KV_WIDTH, tq), lambda i: (0, i)) if transpose_v else out_row(KV_WIDTH)
    v_shape = (KV_WIDTH, n) if transpose_v else (n, KV_WIDTH)
    return pl.pallas_call(
        functools.partial(_prep_kernel, transpose_v=transpose_v),
        name="attn_prep",
        grid=(n // tq,),
        in_specs=[row(ATTN_WIDTH, C_QA), row(IDX_WIDTH, C_QI), row(KV_WIDTH, C_KA), row(KV_WIDTH, C_VA),
                  row(LANES, C_MISC), tab_spec, tab_spec, vec_spec, vec_spec, vec_spec],
        out_specs=[out_row(ATTN_WIDTH), out_row(KV_WIDTH), out_row(KV_WIDTH), v_spec,
                   out_row(IDX_WIDTH), out_row(IDX_DIM), out_row(LANES)],
        out_shape=[jax.ShapeDtypeStruct((n, ATTN_WIDTH), BF16),
                   jax.ShapeDtypeStruct((n, KV_WIDTH), F32),
                   jax.ShapeDtypeStruct((n, KV_WIDTH), BF16),
                   jax.ShapeDtypeStruct(v_shape, BF16),
                   jax.ShapeDtypeStruct((n, IDX_WIDTH), BF16),
                   jax.ShapeDtypeStruct((n, IDX_DIM), F32),
                   jax.ShapeDtypeStruct((n, LANES), BF16)],
        compiler_params=_cparams(("parallel",)),
    )(proj, proj, proj, proj, proj, tabm, tabi,
      q_norm_w.reshape(1, LANES), k_norm_w.reshape(1, LANES), iw)


def _sort_key(x):
    b = pltpu.bitcast(x + 0.0, I32)
    return b ^ ((b >> 31) & INT_MAX)


def _kth_largest_key(count_ge, k, shape):
    def body(it, ans_u):
        bit = jnp.left_shift(jnp.int32(1), 31 - it)
        cand_u = ans_u | bit
        cnt = count_ge(cand_u ^ INT_MIN)
        return jnp.where(cnt >= k, cand_u, ans_u)

    ans_u = lax.fori_loop(0, 32, body, jnp.zeros(shape, I32))
    return ans_u ^ INT_MIN


def _tie_index_limit(count_eq_le, need, n_keys, shape):
    nbits = max(1, int(n_keys - 1).bit_length())

    def body(it, lo):
        bit = jnp.left_shift(jnp.int32(1), nbits - 1 - it)
        cand = lo | bit
        cnt = count_eq_le(cand - 1)
        return jnp.where(cnt >= need, lo, cand)

    return lax.fori_loop(0, nbits, body, jnp.zeros(shape, I32))


def _dsa_prompt_kernel(q_ref, qi_ref, misc_ref, k_ref, vt_ref, kx_ref, o_ref,
                       key_scr, qsel_scr, qg_scr, thr_scr, lim_scr, m_scr, l_scr, acc_scr, *, n_sel):
    i = pl.program_id(1)
    tq = Q_TILE
    ck = KEY_CHUNK
    n_ch = (i * tq + tq + ck - 1) // ck
    q_pos = i * tq + lax.broadcasted_iota(I32, (1, tq), 1)
    row_k = lax.broadcasted_iota(I32, (ck, 1), 0)

    lo_half = lax.broadcasted_iota(I32, (tq, LANES), 1) < IDX_DIM
    zero = jnp.zeros((), BF16)
    for p in range(IDX_WIDTH // LANES):
        slab = qi_ref[:, p * LANES:(p + 1) * LANES]
        qsel_scr[(2 * p) * tq:(2 * p + 1) * tq, :] = jnp.where(lo_half, slab, zero)
        qsel_scr[(2 * p + 1) * tq:(2 * p + 2) * tq, :] = jnp.where(lo_half, zero, slab)
    w_t = misc_ref[...].T

    def score_chunk(c, carry):
        off = pl.multiple_of(c * ck, ck)
        s = _dot_nt(kx_ref[pl.ds(off, ck), :], qsel_scr[...])
        acc = jnp.zeros((ck, tq), F32)
        for h in range(N_IDX_HEADS):
            acc = acc + w_t[M_WI + h:M_WI + h + 1, :] * jnp.maximum(s[:, h * tq:(h + 1) * tq], 0.0)
        acc = jnp.where(off + row_k <= q_pos, acc, -jnp.inf)
        key_scr[pl.ds(off, ck), :] = _sort_key(acc)
        return carry

    lax.fori_loop(0, n_ch, score_chunk, 0)

    spc = SEL_SPAN // ck
    n_span = (n_ch + spc - 1) // spc
    neg_key = jnp.full((ck, tq), NEG_INF_KEY, I32)

    def pad_chunk(c, carry):
        key_scr[pl.ds(pl.multiple_of(c * ck, ck), ck), :] = neg_key
        return carry

    lax.fori_loop(n_ch, n_span * spc, pad_chunk, 0)

    thr_scr[...] = jnp.full((1, tq), INT_MIN, I32)
    lim_scr[...] = jnp.full((1, tq), INT_MAX, I32)

    def select_threshold(n_keys):
        def count_where(pred):
            tot = jnp.zeros((SUBLANES, tq), F32)
            for c0 in range(0, n_keys, ck):
                hit = pred(key_scr[c0:c0 + ck, :], c0 + row_k).astype(F32)
                tot = tot + jnp.sum(hit.reshape(ck // SUBLANES, SUBLANES, tq), axis=0)
            return jnp.sum(tot, axis=0, keepdims=True)

        t = _kth_largest_key(lambda cand: count_where(lambda kk, pos: kk >= cand), float(n_sel), (1, tq))
        thr_scr[...] = t
        n_gt = count_where(lambda kk, pos: kk > t)
        n_ge = count_where(lambda kk, pos: kk >= t)

        @pl.when(jnp.max(n_ge) > float(n_sel))
        def _():
            lim_scr[...] = _tie_index_limit(
                lambda idx: count_where(lambda kk, pos: jnp.logical_and(kk == t, pos <= idx)),
                float(n_sel) - n_gt, k_ref.shape[0], (1, tq))

    for spans in range(1, k_ref.shape[0] // SEL_SPAN + 1):
        if spans * SEL_SPAN > n_sel:
            pl.when(jnp.logical_and(n_span == spans, (i + 1) * tq > n_sel))(
                functools.partial(select_threshold, spans * SEL_SPAN))

    thr = thr_scr[...]
    lim = lim_scr[...]

    for g in range(N_KV_HEADS):
        for r in range(KV_GROUP):
            h = g * KV_GROUP + r
            qg_scr[g, r * tq:(r + 1) * tq, :] = q_ref[:, h * HEAD_DIM:(h + 1) * HEAD_DIM]
    m_scr[...] = jnp.full(m_scr.shape, NEG_BIG, F32)
    l_scr[...] = jnp.zeros(l_scr.shape, F32)
    acc_scr[...] = jnp.zeros(acc_scr.shape, F32)

    def attend_chunk(c, carry):
        off = pl.multiple_of(c * ck, ck)
        kk = key_scr[pl.ds(off, ck), :]
        pos = off + row_k
        sel = jnp.logical_or(kk > thr, jnp.logical_and(kk == thr, pos <= lim))
        sel = jnp.logical_and(sel, pos <= q_pos)
        def group_steps(g):
            kc = k_ref[pl.ds(off, ck), g * HEAD_DIM:(g + 1) * HEAD_DIM]
            vt = vt_ref[g * HEAD_DIM:(g + 1) * HEAD_DIM, pl.ds(off, ck)]
            qk = _dot_nt(kc, qg_scr[g])
            yield
            s = jnp.concatenate([jnp.where(sel, qk[:, r * tq:(r + 1) * tq], NEG_BIG) for r in range(KV_GROUP)],
                                axis=1)
            m_old = m_scr[g]
            m_new = jnp.maximum(m_old, jnp.max(s, axis=0, keepdims=True))
            yield
            p = jnp.exp(s - m_new)
            alpha = jnp.exp(m_old - m_new)
            l_scr[g] = alpha * l_scr[g] + jnp.sum(p, axis=0, keepdims=True)
            yield
            acc_scr[g] = alpha * acc_scr[g] + _dot(vt, p.astype(BF16))
            m_scr[g] = m_new

        chains = [group_steps(g) for g in range(N_KV_HEADS)]
        while chains:
            chains = [ch for ch in chains if next(ch, "done") != "done"]
        return carry

    lax.fori_loop(0, n_ch, attend_chunk, 0)
    for g in range(N_KV_HEADS):
        o_t = acc_scr[g] / l_scr[g]
        for r in range(KV_GROUP):
            h = g * KV_GROUP + r
            o_ref[:, h * HEAD_DIM:(h + 1) * HEAD_DIM] = o_t[:, r * tq:(r + 1) * tq].T.astype(BF16)


def _dsa_prompt(q_bf, qi_bf, proj, k_bf, vt_bf, kx_bf, b, t):
    n = b * t
    nq = t // Q_TILE
    n_sel = min(TOPK_MAX, t // 4)
    qrow = lambda w: pl.BlockSpec((Q_TILE, w), lambda bb, i: (bb * nq + i, 0))
    seq = lambda w: pl.BlockSpec((t, w), lambda bb, i: (bb, 0))
    return pl.pallas_call(
        functools.partial(_dsa_prompt_kernel, n_sel=n_sel),
        name="dsa_prompt",
        grid=(b, nq),
        in_specs=[qrow(ATTN_WIDTH), qrow(IDX_WIDTH),
                  pl.BlockSpec((Q_TILE, LANES), lambda bb, i: (bb * nq + i, C_MISC // LANES)),
                  seq(KV_WIDTH), pl.BlockSpec((KV_WIDTH, t), lambda bb, i: (0, bb)), seq(LANES)],
        out_specs=qrow(ATTN_WIDTH),
        out_shape=jax.ShapeDtypeStruct((n, ATTN_WIDTH), BF16),
        scratch_shapes=[pltpu.VMEM((t, Q_TILE), I32),
                        pltpu.VMEM((N_IDX_HEADS * Q_TILE, LANES), BF16),
                        pltpu.VMEM((N_KV_HEADS, KV_GROUP * Q_TILE, HEAD_DIM), BF16),
                        pltpu.VMEM((1, Q_TILE), I32),
                        pltpu.VMEM((1, Q_TILE), I32),
                        pltpu.VMEM((N_KV_HEADS, 1, KV_GROUP * Q_TILE), F32),
                        pltpu.VMEM((N_KV_HEADS, 1, KV_GROUP * Q_TILE), F32),
                        pltpu.VMEM((N_KV_HEADS, HEAD_DIM, KV_GROUP * Q_TILE), F32)],
        compiler_params=_cparams(("parallel", "arbitrary")),
    )(q_bf, qi_bf, proj, k_bf, vt_bf, kx_bf)


def _sample_score_kernel(pt_ref, q_ref, w_ref, kn_ref, *refs, pages, t_valid):
    page_refs = refs[:pages]
    past_ref, new_ref = refs[pages], refs[pages + 1]
    rows = SAMPLE_ROWS
    q = q_ref[0]
    w = w_ref[0]

    def head_sum(s):
        s = w * jnp.maximum(s, 0.0)
        acc = s[0:rows]
        for h in range(1, N_IDX_HEADS):
            acc = acc + s[h * rows:(h + 1) * rows]
        return acc

    for j in range(pages):
        kp = page_refs[j][0, 0].astype(BF16)
        past_ref[0, :, j * PAGE_SIZE:(j + 1) * PAGE_SIZE] = _sort_key(head_sum(_dot_nt(q, kp)))

    @pl.when(pl.program_id(1) == 0)
    def _():
        kn = jnp.concatenate([kn_ref[...], jnp.zeros((LANES - rows, IDX_DIM), F32)], axis=0).astype(BF16)
        sc = head_sum(_dot_nt(q, kn))
        t = lax.broadcasted_iota(I32, sc.shape, 0)
        s = lax.broadcasted_iota(I32, sc.shape, 1)
        ok = jnp.logical_and(s <= t, s < t_valid)
        new_ref[0] = _sort_key(jnp.where(ok, sc, -jnp.inf))


def _sample_scores(page_table, q_t, w_col, kif, cache_idx, layer, pages, t_valid):
    bs, n_pages = page_table.shape
    past = n_pages * PAGE_SIZE
    hr = N_IDX_HEADS * SAMPLE_ROWS
    page_spec = lambda j: pl.BlockSpec((1, 1, PAGE_SIZE, IDX_DIM),
                                       lambda b, c, pt: (layer, pt[b, c * pages + j], 0, 0))
    grid_spec = pltpu.PrefetchScalarGridSpec(
        num_scalar_prefetch=1,
        grid=(bs, n_pages // pages),
        in_specs=[pl.BlockSpec((1, hr, IDX_DIM), lambda b, c, pt: (b, 0, 0)),
                  pl.BlockSpec((1, hr, 1), lambda b, c, pt: (b, 0, 0)),
                  pl.BlockSpec((SAMPLE_ROWS, IDX_DIM), lambda b, c, pt: (b, 0))]
                 + [page_spec(j) for j in range(pages)],
        out_specs=[pl.BlockSpec((1, SAMPLE_ROWS, pages * PAGE_SIZE), lambda b, c, pt: (b, 0, c)),
                   pl.BlockSpec((1, SAMPLE_ROWS, LANES), lambda b, c, pt: (b, 0, 0))],
    )
    return pl.pallas_call(
        functools.partial(_sample_score_kernel, pages=pages, t_valid=t_valid),
        name="sample_scores",
        grid_spec=grid_spec,
        out_shape=[jax.ShapeDtypeStruct((bs, SAMPLE_ROWS, past), I32),
                   jax.ShapeDtypeStruct((bs, SAMPLE_ROWS, LANES), I32)],
        compiler_params=_cparams(("parallel", "arbitrary")),
    )(page_table, q_t, w_col, kif, *([cache_idx] * pages))


def _sample_attend_kernel(pt_ref, kp_ref, kn_ref, q_ref, knew_ref, vnew_ref, *refs, pages, n_sel, past):
    k_pages = refs[:pages]
    v_pages = refs[pages:2 * pages]
    o_ref = refs[2 * pages]
    thr_scr, lim_scr, m_scr, l_scr, acc_scr = refs[2 * pages + 1:]
    c = pl.program_id(1)
    rows = SAMPLE_ROWS
    span = pages * PAGE_SIZE

    @pl.when(c == 0)
    def _():
        m_scr[...] = jnp.full(m_scr.shape, NEG_BIG, F32)
        l_scr[...] = jnp.zeros(l_scr.shape, F32)
        acc_scr[...] = jnp.zeros(acc_scr.shape, F32)
        kp = kp_ref[0]
        kn = kn_ref[0]
        pos_p = lax.broadcasted_iota(I32, kp.shape, 1)
        pos_n = past + lax.broadcasted_iota(I32, kn.shape, 1)

        def count_where(pred):
            return (jnp.sum(pred(kp, pos_p).astype(F32), axis=1, keepdims=True)
                    + jnp.sum(pred(kn, pos_n).astype(F32), axis=1, keepdims=True))

        t = _kth_largest_key(lambda cand: count_where(lambda kk, pos: kk >= cand), float(n_sel), (rows, 1))
        thr_scr[...] = t
        lim_scr[...] = jnp.full((rows, 1), INT_MAX, I32)
        n_gt = count_where(lambda kk, pos: kk > t)
        n_ge = count_where(lambda kk, pos: kk >= t)

        @pl.when(jnp.max(n_ge) > float(n_sel))
        def _():
            lim_scr[...] = _tie_index_limit(
                lambda idx: count_where(lambda kk, pos: jnp.logical_and(kk == t, pos <= idx)),
                float(n_sel) - n_gt, past + LANES, (rows, 1))

    thr = thr_scr[...]
    lim = lim_scr[...]

    def update(g, qg, kc, vc, sel):
        sel = jnp.concatenate([sel] * KV_GROUP, axis=0)
        s = jnp.where(sel, _dot_nt(qg, kc), NEG_BIG)
        m_old = m_scr[g]
        m_new = jnp.maximum(m_old, jnp.max(s, axis=1, keepdims=True))
        p = jnp.where(sel, jnp.exp(s - m_new), 0.0)
        alpha = jnp.exp(m_old - m_new)
        l_scr[g] = alpha * l_scr[g] + jnp.sum(p, axis=1, keepdims=True)
        acc_scr[g] = alpha * acc_scr[g] + _dot(p.astype(BF16), vc)
        m_scr[g] = m_new

    def select(kk, pos):
        return jnp.logical_or(kk > thr, jnp.logical_and(kk == thr, pos <= lim))

    def page_cat(page_refs, g):
        return jnp.concatenate([r[0, 0, pl.ds(g, PAGE_SIZE, stride=N_KV_HEADS), :] for r in page_refs],
                               axis=0).astype(BF16)

    off = pl.multiple_of(c * span, span)
    kk = kp_ref[0, :, pl.ds(off, span)]
    sel_past = select(kk, off + lax.broadcasted_iota(I32, kk.shape, 1))
    q_groups = []
    for g in range(N_KV_HEADS):
        qg = jnp.concatenate(
            [q_ref[:, (g * KV_GROUP + r) * HEAD_DIM:(g * KV_GROUP + r + 1) * HEAD_DIM] for r in range(KV_GROUP)],
            axis=0)
        q_groups.append(qg)
        update(g, qg, page_cat(k_pages, g), page_cat(v_pages, g), sel_past)

    @pl.when(c == pl.num_programs(1) - 1)
    def _():
        kn = kn_ref[0]
        lane = lax.broadcasted_iota(I32, kn.shape, 1)
        sel_new = jnp.logical_and(select(kn, past + lane), lane < rows)
        pad = jnp.zeros((LANES - rows, KV_WIDTH), BF16)
        k_new = jnp.concatenate([knew_ref[...], pad], axis=0)
        v_new = jnp.concatenate([vnew_ref[...], pad], axis=0)
        for g in range(N_KV_HEADS):
            sl = slice(g * HEAD_DIM, (g + 1) * HEAD_DIM)
            update(g, q_groups[g], k_new[:, sl], v_new[:, sl], sel_new)
            o = acc_scr[g] / l_scr[g]
            for r in range(KV_GROUP):
                h = g * KV_GROUP + r
                o_ref[:, h * HEAD_DIM:(h + 1) * HEAD_DIM] = o[r * rows:(r + 1) * rows].astype(BF16)


def _sample_attend(page_table, keys_past, keys_new, q_bf, k_bf, v_bf, cache_k, cache_v, layer, pages, n_sel):
    bs, n_pages = page_table.shape
    past = n_pages * PAGE_SIZE
    depth, pool = cache_k.shape[0], cache_k.shape[1]
    cache_k = cache_k.reshape(depth, pool, PAGE_SIZE * N_KV_HEADS, HEAD_DIM)
    cache_v = cache_v.reshape(depth, pool, PAGE_SIZE * N_KV_HEADS, HEAD_DIM)
    page_spec = lambda j: pl.BlockSpec((1, 1, PAGE_SIZE * N_KV_HEADS, HEAD_DIM),
                                       lambda b, c, pt: (layer, pt[b, c * pages + j], 0, 0))
    row = lambda w: pl.BlockSpec((SAMPLE_ROWS, w), lambda b, c, pt: (b, 0))
    grid_spec = pltpu.PrefetchScalarGridSpec(
        num_scalar_prefetch=1,
        grid=(bs, n_pages // pages),
        in_specs=[pl.BlockSpec((1, SAMPLE_ROWS, past), lambda b, c, pt: (b, 0, 0)),
                  pl.BlockSpec((1, SAMPLE_ROWS, LANES), lambda b, c, pt: (b, 0, 0)),
                  row(ATTN_WIDTH), row(KV_WIDTH), row(KV_WIDTH)]
                 + [page_spec(j) for j in range(pages)] * 2,
        out_specs=row(ATTN_WIDTH),
        scratch_shapes=[pltpu.VMEM((SAMPLE_ROWS, 1), I32),
                        pltpu.VMEM((SAMPLE_ROWS, 1), I32),
                        pltpu.VMEM((N_KV_HEADS, KV_GROUP * SAMPLE_ROWS, 1), F32),
                        pltpu.VMEM((N_KV_HEADS, KV_GROUP * SAMPLE_ROWS, 1), F32),
                        pltpu.VMEM((N_KV_HEADS, KV_GROUP * SAMPLE_ROWS, HEAD_DIM), F32)],
    )
    return pl.pallas_call(
        functools.partial(_sample_attend_kernel, pages=pages, n_sel=n_sel, past=past),
        name="sample_attend",
        grid_spec=grid_spec,
        out_shape=jax.ShapeDtypeStruct((bs * SAMPLE_ROWS, ATTN_WIDTH), BF16),
        compiler_params=_cparams(("parallel", "arbitrary")),
    )(page_table, keys_past, keys_new, q_bf, k_bf, v_bf, *([cache_k] * pages), *([cache_v] * pages))


def _delta_prep_kernel(x_ref, halo_ref, prev_ref, misc_ref, cw_ref, al_ref, dt_ref,
                       qn_ref, kn_ref, vv_ref, bg_ref, xp_scr, *, tiles_per_seq, t_valid, tt):
    i = pl.program_id(0)
    tile_in_seq = i % tiles_per_seq
    halo = jnp.where(tile_in_seq == 0, prev_ref[0], halo_ref[...])
    xp_scr[0:SUBLANES, :] = halo
    xp_scr[SUBLANES:SUBLANES + tt, :] = x_ref[...]
    base = SUBLANES - (CONV_WIDTH - 1)
    outs = (qn_ref, kn_ref, vv_ref)
    for sec in range(3):
        for h in range(N_DELTA_HEADS):
            col = sec * DELTA_WIDTH + h * HEAD_DIM
            sl = slice(col, col + HEAD_DIM)
            y = xp_scr[base:base + tt, sl] * cw_ref[0:1, sl]
            for j in range(1, CONV_WIDTH):
                y = y + xp_scr[base + j:base + j + tt, sl] * cw_ref[j:j + 1, sl]
            y = _silu(y)
            if sec < 2:
                y = y * lax.rsqrt(jnp.sum(y * y, axis=-1, keepdims=True) + EPS)
            if sec == 0:
                y = y * (HEAD_DIM ** -0.5)
            outs[sec][:, h * HEAD_DIM:(h + 1) * HEAD_DIM] = y
    m = misc_ref[...]
    lane = lax.broadcasted_iota(I32, m.shape, 1)
    row = tile_in_seq * tt + lax.broadcasted_iota(I32, m.shape, 0)
    beta = _sigmoid(m)
    g = -jnp.exp(al_ref[...]) * _softplus(m + dt_ref[...])
    is_b = jnp.logical_and(lane >= M_BD, lane < M_BD + N_DELTA_HEADS)
    is_g = jnp.logical_and(lane >= M_AD, lane < M_AD + N_DELTA_HEADS)
    comb = jnp.where(is_b, beta, jnp.where(is_g, g, 0.0))
    comb = jnp.where(row < t_valid, comb, 0.0)
    bg_ref[...] = pltpu.roll(comb, LANES - M_BD, 1)


def _delta_prep(proj, prev8, conv_w, a_log, dt_bias, b, t, tt, t_valid):
    n = proj.shape[0]
    tiles_per_seq = t // tt
    pad_vec = lambda v: jnp.zeros((1, LANES), F32).at[0, M_AD:M_AD + N_DELTA_HEADS].set(v)
    halo_blocks = tt // SUBLANES
    return pl.pallas_call(
        functools.partial(_delta_prep_kernel, tiles_per_seq=tiles_per_seq, t_valid=t_valid, tt=tt),
        name="delta_prep",
        grid=(n // tt,),
        in_specs=[pl.BlockSpec((tt, CONV_CHANNELS), lambda i: (i, C_CONV // CONV_CHANNELS)),
                  pl.BlockSpec((SUBLANES, CONV_CHANNELS),
                               lambda i: (jnp.maximum(i * halo_blocks - 1, 0), C_CONV // CONV_CHANNELS)),
                  pl.BlockSpec((1, SUBLANES, CONV_CHANNELS), lambda i: (i // tiles_per_seq, 0, 0)),
                  pl.BlockSpec((tt, LANES), lambda i: (i, C_MISC // LANES)),
                  pl.BlockSpec((CONV_WIDTH, CONV_CHANNELS), lambda i: (0, 0)),
                  pl.BlockSpec((1, LANES), lambda i: (0, 0)),
                  pl.BlockSpec((1, LANES), lambda i: (0, 0))],
        out_specs=[pl.BlockSpec((tt, DELTA_WIDTH), lambda i: (i, 0))] * 3
                  + [pl.BlockSpec((tt, LANES), lambda i: (i, 0))],
        out_shape=[jax.ShapeDtypeStruct((n, DELTA_WIDTH), F32)] * 3 + [jax.ShapeDtypeStruct((n, LANES), F32)],
        scratch_shapes=[pltpu.VMEM((SUBLANES + tt, CONV_CHANNELS), F32)],
        compiler_params=_cparams(("parallel",)),
    )(proj, proj, prev8, proj, conv_w, pad_vec(a_log), pad_vec(dt_bias))


def _mm(a, b):
    return _dot(a.astype(BF16), b.astype(BF16))


def _mm_nt(a, b):
    return _dot_nt(a.astype(BF16), b.astype(BF16))


DELTA_INV_BLOCK = 16
DELTA_STACK = 4
DELTA_CHUNKS_PER_STEP = 2


def _delta_chunk_kernel(qn_ref, kn_ref, vv_ref, bg_ref, z_ref, s0_ref, ow_ref, od_ref, so_ref, s_scr, *, n_chunks):
    c = pl.program_id(1)
    cs = DELTA_CHUNK

    @pl.when(c == 0)
    def _():
        s_scr[...] = s0_ref[0]

    ltri = (lax.broadcasted_iota(I32, (cs, cs), 0) >= lax.broadcasted_iota(I32, (cs, cs), 1)).astype(BF16)

    def chunk_gates(ch):
        bg = bg_ref[ch * cs:(ch + 1) * cs, :]
        g1 = bg.astype(BF16)
        r1 = bg - g1.astype(F32)
        g2 = r1.astype(BF16)
        g3 = (r1 - g2.astype(F32)).astype(BF16)
        gc = _dot(ltri, g1) + _dot(ltri, g2) + _dot(ltri, g3)
        return bg, gc, gc.T

    gates = [chunk_gates(ch) for ch in range(n_chunks)]
    state_ready = {}

    gh = DELTA_STACK
    rows = gh * cs
    rr = lax.broadcasted_iota(I32, (rows, rows), 0)
    cc = lax.broadcasted_iota(I32, (rows, rows), 1)
    same = (rr // cs) == (cc // cs)
    causal = jnp.logical_and(same, rr >= cc)
    strict = jnp.logical_and(same, rr > cc)
    eye = (rr == cc).astype(F32)
    row_head = lax.broadcasted_iota(I32, (rows, 1), 0) // cs
    def group_steps(ch, grp):
        heads = [grp * gh + j for j in range(gh)]
        bg, gc, gct = gates[ch]
        r0 = ch * cs
        stack = lambda ref: jnp.concatenate([ref[r0:r0 + cs, h * HEAD_DIM:(h + 1) * HEAD_DIM] for h in heads],
                                            axis=0)
        col = lambda a, lane0: jnp.concatenate([a[:, lane0 + h:lane0 + h + 1] for h in heads], axis=0)
        k = stack(kn_ref)
        q = stack(qn_ref)
        v = stack(vv_ref)
        bcol = col(bg, 0)
        gcc = col(gc, N_DELTA_HEADS)
        gcr = jnp.concatenate([gct[N_DELTA_HEADS + h:N_DELTA_HEADS + h + 1, :] for h in heads], axis=1)
        g_last = [gc[cs - 1:cs, N_DELTA_HEADS + h:N_DELTA_HEADS + h + 1] for h in heads]
        glc = jnp.concatenate([jnp.broadcast_to(gl, (cs, 1)) for gl in g_last], axis=0)
        decay = jnp.exp(jnp.where(causal, gcc - gcr, -jnp.inf))
        kb = k * bcol
        eg = jnp.exp(gcc)
        kq = _mm_nt(jnp.concatenate([kb, q], axis=0), k)
        yield
        a = jnp.where(strict, kq[0:rows] * decay, 0.0)
        intra = jnp.where(causal, kq[rows:2 * rows] * decay, 0.0)
        x = -a
        nb = DELTA_INV_BLOCK
        y = jnp.where((rr // nb) == (cc // nb), x, 0.0)
        p = eye + y
        y = _mm(y, y)
        yield
        n_sq = max(1, int(nb - 1).bit_length())
        for lvl in range(1, n_sq):
            if lvl < n_sq - 1:
                py = _mm(jnp.concatenate([p, y], axis=0), y)
                p = p + py[0:rows]
                y = py[rows:2 * rows]
            else:
                p = p + _mm(p, y)
            yield
        size = 2 * nb
        while size <= cs:
            off = jnp.where(jnp.logical_and((rr // size) == (cc // size), (rr // (size // 2)) != (cc // (size // 2))),
                            x, 0.0)
            po = _mm(p, off)
            yield
            p = p + _mm(po, p)
            yield
            size *= 2
        sol = _mm(p, jnp.concatenate([v * bcol, kb * eg], axis=1))
        yield
        u = sol[:, 0:HEAD_DIM]
        w = sol[:, HEAD_DIM:2 * HEAD_DIM]
        lanes_g = slice(grp * gh * HEAD_DIM, (grp + 1) * gh * HEAD_DIM)
        while ch > 0 and not state_ready.get((ch - 1, grp)):
            yield
        s_g = s_scr[:, lanes_g]
        wq_s = _mm(jnp.concatenate([w, q * eg], axis=0), s_g)
        yield
        own = lambda m, r0: jnp.concatenate(
            [m[r0 + j * cs:r0 + (j + 1) * cs, j * HEAD_DIM:(j + 1) * HEAD_DIM] for j in range(gh)], axis=0)
        v_new = u - own(wq_s, 0)
        o = own(wq_s, rows) + _mm(intra, v_new)
        yield
        kg_t = (k * jnp.exp(glc - gcc)).T
        vn_blocks = jnp.concatenate([jnp.where(row_head == j, v_new, 0.0) for j in range(gh)], axis=1)
        s_decay = jnp.concatenate([jnp.broadcast_to(jnp.exp(gl), (1, HEAD_DIM)) for gl in g_last], axis=1)
        s_scr[:, lanes_g] = s_g * s_decay + _mm(kg_t, vn_blocks)
        state_ready[(ch, grp)] = True
        yield
        on = o * lax.rsqrt(jnp.mean(o * o, axis=-1, keepdims=True) + EPS) * ow_ref[...]
        for j, h in enumerate(heads):
            sl = slice(h * HEAD_DIM, (h + 1) * HEAD_DIM)
            od_ref[r0:r0 + cs, sl] = (on[j * cs:(j + 1) * cs] * _silu(z_ref[r0:r0 + cs, sl])).astype(BF16)

    chains = [group_steps(ch, grp) for ch in range(n_chunks) for grp in range(N_DELTA_HEADS // gh)]
    while chains:
        chains = [g for g in chains if next(g, "done") != "done"]

    so_ref[0] = s_scr[...]


def _delta_chunks(qn, kn, vv, bg, zsrc, z_col_block, state0, o_norm_w, b, t):
    n = b * t
    per_step = DELTA_CHUNKS_PER_STEP if (t // DELTA_CHUNK) % DELTA_CHUNKS_PER_STEP == 0 else 1
    nc = t // (DELTA_CHUNK * per_step)
    sw = N_DELTA_HEADS * HEAD_DIM
    row = lambda w, cb=0: pl.BlockSpec((DELTA_CHUNK * per_step, w), lambda bb, c: (bb * nc + c, cb))
    st = pl.BlockSpec((1, HEAD_DIM, sw), lambda bb, c: (bb, 0, 0))
    s_in = state0.transpose(0, 2, 1, 3).reshape(b, HEAD_DIM, sw)
    od, s_out = pl.pallas_call(
        functools.partial(_delta_chunk_kernel, n_chunks=per_step),
        name="delta_chunks",
        grid=(b, nc),
        in_specs=[row(DELTA_WIDTH), row(DELTA_WIDTH), row(DELTA_WIDTH), row(LANES),
                  row(DELTA_WIDTH, z_col_block), st, pl.BlockSpec((1, LANES), lambda bb, c: (0, 0))],
        out_specs=[row(DELTA_WIDTH), st],
        out_shape=[jax.ShapeDtypeStruct((n, DELTA_WIDTH), BF16),
                   jax.ShapeDtypeStruct((b, HEAD_DIM, sw), F32)],
        scratch_shapes=[pltpu.VMEM((HEAD_DIM, sw), F32)],
        compiler_params=_cparams(("parallel", "arbitrary")),
    )(qn, kn, vv, bg, zsrc, s_in, o_norm_w.reshape(1, LANES))
    return od, s_out.reshape(b, HEAD_DIM, N_DELTA_HEADS, HEAD_DIM).transpose(0, 2, 1, 3)


def _outproj_kernel(oa_ref, od_ref, x_ref, g1_ref, sh_ref, sc_ref, nw_ref, wo_ref, wrh_ref, wrl_ref, br_ref,
                    x1_ref, h2_ref, lg_ref):
    mix = _dot(oa_ref[...], wo_ref[0:ATTN_WIDTH, :]) + _dot(od_ref[...], wo_ref[ATTN_WIDTH:ATTN_WIDTH + DELTA_WIDTH, :])
    x1 = x_ref[...] + g1_ref[0] * mix
    x1_ref[...] = x1
    y = x1 * lax.rsqrt(jnp.mean(x1 * x1, axis=-1, keepdims=True) + EPS) * nw_ref[...]
    h2 = y * (1.0 + sc_ref[0]) + sh_ref[0]
    h2_ref[...] = h2
    hb = h2.astype(BF16)
    lo = (h2 - hb.astype(F32)).astype(BF16)
    lg_ref[...] = _dot(hb, wrh_ref[...]) + _dot(lo, wrh_ref[...]) + _dot(hb, wrl_ref[...]) + br_ref[...]


def _out_projection(o_attn, o_delta, x2d, gate1, shift2, scale2, norm2_w, w_out_bf, wr_hi, wr_lo, b_rt,
                    tm, rows_per_mod_block):
    n, d = x2d.shape
    r = gate1.shape[1]
    tiles_per_mod = rows_per_mod_block // tm
    mod_spec = pl.BlockSpec((1, r, d), lambda i: (i // tiles_per_mod, 0, 0))
    row = lambda w: pl.BlockSpec((tm, w), lambda i: (i, 0))
    full = lambda a: pl.BlockSpec(a.shape, lambda i: (0, 0))
    return pl.pallas_call(
        _outproj_kernel,
        name="out_proj",
        grid=(n // tm,),
        in_specs=[row(ATTN_WIDTH), row(DELTA_WIDTH), row(d), mod_spec, mod_spec, mod_spec,
                  pl.BlockSpec((1, d), lambda i: (0, 0)), full(w_out_bf), full(wr_hi), full(wr_lo), full(b_rt)],
        out_specs=[row(d), row(d), row(LANES)],
        out_shape=[jax.ShapeDtypeStruct((n, d), F32), jax.ShapeDtypeStruct((n, d), F32),
                   jax.ShapeDtypeStruct((n, LANES), F32)],
        compiler_params=_cparams(("parallel",)),
    )(o_attn, o_delta, x2d, gate1, shift2, scale2, norm2_w.reshape(1, d), w_out_bf, wr_hi, wr_lo, b_rt)


def _route_kernel(lg_ref, eid_ref, gate_ref):
    x = lg_ref[...]
    lane = lax.broadcasted_iota(I32, x.shape, 1)
    gl = jnp.where(lane < N_GROUPS, x, -jnp.inf)
    ge = jnp.exp(gl - jnp.max(gl, axis=1, keepdims=True))
    p = ge / jnp.sum(ge, axis=1, keepdims=True)
    p_max = jnp.max(p, axis=1, keepdims=True)
    grp = jnp.min(jnp.where(p == p_max, lane, LANES), axis=1, keepdims=True)
    e_lane = lane - N_GROUPS
    in_grp = jnp.logical_and(jnp.logical_and(e_lane >= 0, e_lane < N_EXPERTS),
                             (e_lane >> 3) == grp)
    rl = jnp.where(in_grp, x, -jnp.inf)
    v1 = jnp.max(rl, axis=1, keepdims=True)
    i1 = jnp.min(jnp.where(rl == v1, lane, LANES), axis=1, keepdims=True)
    rl2 = jnp.where(lane == i1, -jnp.inf, rl)
    v2 = jnp.max(rl2, axis=1, keepdims=True)
    i2 = jnp.min(jnp.where(rl2 == v2, lane, LANES), axis=1, keepdims=True)
    t = jnp.exp(v2 - v1)
    den = 1.0 + t
    eid_ref[...] = jnp.where(lane == 0, i1 - N_GROUPS, jnp.where(lane == 1, i2 - N_GROUPS, 0))
    gate_ref[...] = jnp.where(lane == 0, (1.0 / den) * p_max, jnp.where(lane == 1, (t / den) * p_max, 0.0))


def _route(logits, tm):
    n = logits.shape[0]
    spec = pl.BlockSpec((tm, LANES), lambda i: (i, 0))
    return pl.pallas_call(
        _route_kernel,
        name="route",
        grid=(n // tm,),
        in_specs=[spec],
        out_specs=[spec, spec],
        out_shape=[jax.ShapeDtypeStruct((n, LANES), I32), jax.ShapeDtypeStruct((n, LANES), F32)],
        compiler_params=_cparams(("parallel",)),
    )(logits)


def _row_gather(idx_ref, base, n_rows, src_hbm, dst, sem):
    def body(r, carry):
        pltpu.make_async_copy(src_hbm.at[pl.ds(idx_ref[base + r], 1), :], dst.at[pl.ds(r, 1), :], sem).start()
        return carry

    lax.fori_loop(0, n_rows, body, 0, unroll=8)


def _row_gather_wait(n_rows, src_hbm, dst, sem):
    pltpu.make_async_copy(src_hbm.at[pl.ds(0, n_rows), :], dst, sem).wait()


def _moe_kernel(tok_ref, j0_ref, be_ref, na_ref, h_hbm, wg_ref, wu_ref, wd_ref, o_ref,
                x_even, x_odd, sem, wg_scr, wu_scr, wd_scr, *, bm):
    i = pl.program_id(0)
    n_act = na_ref[0]
    bufs = ((x_even, sem.at[0]), (x_odd, sem.at[1]))

    @pl.when(i == 0)
    def _():
        _row_gather(tok_ref, j0_ref[0], bm, h_hbm, x_even, sem.at[0])

    changed = jnp.logical_or(i == 0, be_ref[i] != be_ref[jnp.maximum(i - 1, 0)])

    @pl.when(jnp.logical_and(i < n_act, changed))
    def _():
        wg_scr[...] = wg_ref[0].astype(BF16)
        wu_scr[...] = wu_ref[0].astype(BF16)
        wd_scr[...] = wd_ref[0].astype(BF16)

    for parity in range(2):
        cur, cur_sem = bufs[parity]
        nxt, nxt_sem = bufs[1 - parity]

        @pl.when(jnp.logical_and(i < n_act, i % 2 == parity))
        def _():
            _row_gather_wait(bm, h_hbm, cur, cur_sem)
            base = j0_ref[i + 1]
            for r in range(bm):
                pltpu.make_async_copy(h_hbm.at[pl.ds(tok_ref[base + r], 1), :], nxt.at[pl.ds(r, 1), :],
                                      nxt_sem).start()
            x = cur[...].astype(BF16)
            hid = _silu(_dot(x, wg_scr[...])) * _dot(x, wu_scr[...])
            o_ref[...] = _dot(hid.astype(BF16), wd_scr[...])

        @pl.when(jnp.logical_and(i == n_act, i % 2 == parity))
        def _():
            _row_gather_wait(bm, h_hbm, cur, cur_sem)

    @pl.when(i >= n_act)
    def _():
        o_ref[...] = jnp.zeros(o_ref.shape, F32)


def _moe_experts(tok_sorted, block_j0, block_exp, n_active, h2, w_gate, w_up, w_down, bm):
    ns = block_exp.shape[0] * bm
    d = h2.shape[1]
    f = w_gate.shape[2]
    grid_spec = pltpu.PrefetchScalarGridSpec(
        num_scalar_prefetch=4,
        grid=(ns // bm,),
        in_specs=[pl.BlockSpec(memory_space=pl.ANY),
                  pl.BlockSpec((1, d, f), lambda i, tok, j0, be, na: (be[i], 0, 0)),
                  pl.BlockSpec((1, d, f), lambda i, tok, j0, be, na: (be[i], 0, 0)),
                  pl.BlockSpec((1, f, d), lambda i, tok, j0, be, na: (be[i], 0, 0))],
        out_specs=pl.BlockSpec((bm, d), lambda i, tok, j0, be, na: (i, 0)),
        scratch_shapes=[pltpu.VMEM((bm, d), F32), pltpu.VMEM((bm, d), F32), pltpu.SemaphoreType.DMA((2,)),
                        pltpu.VMEM((d, f), BF16), pltpu.VMEM((d, f), BF16), pltpu.VMEM((f, d), BF16)],
    )
    return pl.pallas_call(
        functools.partial(_moe_kernel, bm=bm),
        name="moe_experts",
        grid_spec=grid_spec,
        out_shape=jax.ShapeDtypeStruct((ns, d), F32),
        compiler_params=_cparams(("arbitrary",)),
    )(tok_sorted, block_j0, block_exp, n_active, h2, w_gate, w_up, w_down)


def _combine_kernel(dest_ref, x1_ref, gt_ref, g2_ref, y_hbm, o_ref, y_buf, sem, *, tm):
    i = pl.program_id(0)
    n = pl.num_programs(0)
    slot = i % 2

    @pl.when(i == 0)
    def _():
        _row_gather(dest_ref, 0, 2 * tm, y_hbm, y_buf.at[0], sem.at[0])

    @pl.when(i + 1 < n)
    def _():
        _row_gather(dest_ref, (i + 1) * 2 * tm, 2 * tm, y_hbm, y_buf.at[1 - slot], sem.at[1 - slot])

    _row_gather_wait(2 * tm, y_hbm, y_buf.at[slot], sem.at[slot])
    gt = gt_ref[...]
    y = y_buf[slot, 0:tm, :] * gt[:, 0:1] + y_buf[slot, tm:2 * tm, :] * gt[:, 1:2]
    o_ref[...] = x1_ref[...] + g2_ref[0] * y


def _combine(x1, dest, y_rows, gates, gate2, tm, rows_per_mod_block):
    n, d = x1.shape
    r = gate2.shape[1]
    tiles_per_mod = rows_per_mod_block // tm
    dest_tiles = dest.reshape(n // tm, tm, 2).transpose(0, 2, 1).reshape(-1)
    row = lambda w: pl.BlockSpec((tm, w), lambda i, dst: (i, 0))
    grid_spec = pltpu.PrefetchScalarGridSpec(
        num_scalar_prefetch=1,
        grid=(n // tm,),
        in_specs=[row(d), row(LANES),
                  pl.BlockSpec((1, r, d), lambda i, dst: (i // tiles_per_mod, 0, 0)),
                  pl.BlockSpec(memory_space=pl.ANY)],
        out_specs=row(d),
        scratch_shapes=[pltpu.VMEM((2, 2 * tm, d), F32), pltpu.SemaphoreType.DMA((2,))],
    )
    return pl.pallas_call(
        functools.partial(_combine_kernel, tm=tm),
        name="moe_combine",
        grid_spec=grid_spec,
        out_shape=jax.ShapeDtypeStruct((n, d), F32),
        compiler_params=_cparams(("arbitrary",)),
    )(dest_tiles, x1, gates, gate2, y_rows)


def _pick_tile(n, pref, mult=16):
    t = min(pref, n)
    while n % t or t % mult:
        t -= 1
    return t


def _pack_w_in(w_in):
    d = w_in.shape[0]
    bounds = np.cumsum(PROJ_SIZES)[:-1].tolist()
    qa, ka, va, qi, ki, wi, qd, kd, vd, zd, bd, ad = jnp.split(w_in, bounds, axis=1)
    used = IDX_DIM + N_IDX_HEADS + 2 * N_DELTA_HEADS
    misc = jnp.concatenate([ki, wi, bd, ad, jnp.zeros((d, LANES - used), w_in.dtype)], axis=1)
    cols = [qa, qi, zd, qd, kd, vd, ka, va, misc]
    width = sum(c.shape[1] for c in cols)
    cols.append(jnp.zeros((d, PROJ_PACKED - width), w_in.dtype))
    return jnp.concatenate(cols, axis=1).astype(BF16)


def _route_and_sort(eid, bm):
    n = eid.shape[0]
    nk = 2 * n
    flat_e = eid.reshape(-1)
    order = jnp.argsort(flat_e, stable=True).astype(I32)
    inv = jnp.argsort(order).astype(I32)
    onehot = flat_e[:, None] == jnp.arange(N_EXPERTS, dtype=I32)[None, :]
    counts = jnp.sum(onehot.astype(I32), axis=0)
    padded = (counts + bm - 1) // bm * bm
    pad_end = jnp.cumsum(padded)
    shift = (pad_end - padded) - (jnp.cumsum(counts) - counts)
    dest = inv + jnp.sum(jnp.where(onehot, shift[None, :], 0), axis=1)
    n_blocks = -(-nk // bm) + N_EXPERTS + 1
    block_exp = jnp.minimum(jnp.searchsorted(pad_end, jnp.arange(n_blocks, dtype=I32) * bm, side='right'),
                            N_EXPERTS - 1).astype(I32)
    n_active = (pad_end[-1] // bm).astype(I32).reshape(1)
    block_j0 = jnp.clip(jnp.arange(n_blocks, dtype=I32) * bm - shift[block_exp], 0, nk)
    tok_sorted = jnp.concatenate([order // 2, jnp.zeros((bm,), I32)])
    return tok_sorted, block_j0, dest.astype(I32).reshape(n, 2), block_exp, n_active


def _layer(layer, yp, ys, cache_k, cache_v, cache_idx, state_ssm, state_conv, page_table, c_prompt, c_sample,
           w_in, w_out, conv_w, a_log, dt_bias, q_norm_w, k_norm_w, idx_k_norm_w, o_norm_w, norm1_w, norm2_w,
           w_ada, b_ada, w_group, b_group, w_router, b_router, w_gate, w_up, w_down):
    bp, tp, d = yp.shape
    bs, ts, _ = ys.shape
    past = page_table.shape[1] * PAGE_SIZE
    rows = SAMPLE_ROWS
    assert CONV_WIDTH - 1 <= ts <= rows and tp % SEL_SPAN == 0 and tp % DELTA_CHUNK == 0

    n_c = bp + bs
    n_c_pad = -(-n_c // SUBLANES) * SUBLANES
    c_all = jnp.concatenate([c_prompt, c_sample, jnp.zeros((n_c_pad - n_c, d), F32)], axis=0)
    mod = _ada_modulation(c_all, w_ada, b_ada)
    mods = jnp.split(mod, N_MOD, axis=1)
    mp = [m[:bp].reshape(bp, 1, d) for m in mods]
    ms = [jnp.repeat(m[bp:bp + bs], rows, axis=0).reshape(1, bs * rows, d) for m in mods]

    w_packed = _pack_w_in(w_in)
    w_out_bf = w_out.astype(BF16)
    w_rt = jnp.concatenate([w_group, w_router, jnp.zeros((d, LANES - N_GROUPS - N_EXPERTS), F32)], axis=1)
    wr_hi = w_rt.astype(BF16)
    wr_lo = (w_rt - wr_hi.astype(F32)).astype(BF16)
    b_rt = jnp.concatenate([b_group, b_router, jnp.zeros((LANES - N_GROUPS - N_EXPERTS,), F32)]).reshape(1, LANES)

    np_ = bp * tp
    xp2 = yp.reshape(np_, d)
    tm_p = _pick_tile(tp, 256)
    proj_p = _in_projection(xp2, mp[0], mp[1], norm1_w, w_packed, tm_p, tp)
    tq_p = _pick_tile(tp, 256)
    q_p, kf_p, kb_p, vt_p, qi_p, kif_p, kx_p = _attention_prep(
        proj_p, jnp.arange(tp), tq_p, q_norm_w, k_norm_w, idx_k_norm_w, True)
    oa_p = _dsa_prompt(q_p, qi_p, proj_p, kb_p, vt_p, kx_p, bp, tp)
    tt_p = _pick_tile(tp, 256)
    qn_p, kn_p, vv_p, bg_p = _delta_prep(proj_p, jnp.zeros((bp, SUBLANES, CONV_CHANNELS), F32), conv_w,
                                         a_log, dt_bias, bp, tp, tt_p, tp)
    od_p, ssm_p = _delta_chunks(qn_p, kn_p, vv_p, bg_p, proj_p, C_ZD // DELTA_WIDTH,
                                jnp.zeros((bp, N_DELTA_HEADS, HEAD_DIM, HEAD_DIM), F32), o_norm_w, bp, tp)
    tm_o = _pick_tile(tp, 256)
    x1_p, h2_p, lg_p = _out_projection(oa_p, od_p, xp2, mp[2], mp[3], mp[4], norm2_w, w_out_bf, wr_hi, wr_lo, b_rt,
                                       tm_o, tp)

    ns_ = bs * rows
    xs2 = jnp.pad(ys, ((0, 0), (0, rows - ts), (0, 0))).reshape(ns_, d)
    tm_s = _pick_tile(ns_, 256)
    proj_s = _in_projection(xs2, ms[0].reshape(ns_ // tm_s, tm_s, d), ms[1].reshape(ns_ // tm_s, tm_s, d),
                            norm1_w, w_packed, tm_s, tm_s)
    q_s, kf_s, kb_s, vb_s, qi_s, kif_s, _ = _attention_prep(
        proj_s, past + jnp.arange(rows), rows, q_norm_w, k_norm_w, idx_k_norm_w, False)
    q_t = qi_s.reshape(bs, rows, N_IDX_HEADS, IDX_DIM).transpose(0, 2, 1, 3).reshape(bs, N_IDX_HEADS * rows, IDX_DIM)
    w_col = proj_s[:, C_MISC + M_WI:C_MISC + M_WI + N_IDX_HEADS].reshape(bs, rows, N_IDX_HEADS)
    w_col = w_col.transpose(0, 2, 1).reshape(bs, N_IDX_HEADS * rows, 1)
    pages = _pick_tile(page_table.shape[1], 8, 1)
    keys_past, keys_new = _sample_scores(page_table, q_t, w_col, kif_s, cache_idx, layer, pages, ts)
    n_sel_s = min(TOPK_MAX, (past + ts) // 4)
    oa_s = _sample_attend(page_table, keys_past, keys_new, q_s, kb_s, vb_s, cache_k, cache_v, layer, pages, n_sel_s)
    prev8 = jnp.pad(state_conv, ((0, 0), (SUBLANES - (CONV_WIDTH - 1), 0), (0, 0)))
    qn_s, kn_s, vv_s, bg_s = _delta_prep(proj_s, prev8, conv_w, a_log, dt_bias, bs, rows, rows, ts)
    to_chunk = lambda a: jnp.pad(a.reshape(bs, rows, -1), ((0, 0), (0, DELTA_CHUNK - rows), (0, 0))).reshape(
        bs * DELTA_CHUNK, -1)
    z_s = proj_s[:, C_ZD:C_ZD + DELTA_WIDTH]
    od_s, ssm_s = _delta_chunks(to_chunk(qn_s), to_chunk(kn_s), to_chunk(vv_s), to_chunk(bg_s), to_chunk(z_s), 0,
                                state_ssm, o_norm_w, bs, DELTA_CHUNK)
    od_s = od_s.reshape(bs, DELTA_CHUNK, DELTA_WIDTH)[:, :rows].reshape(ns_, DELTA_WIDTH)
    x1_s, h2_s, lg_s = _out_projection(oa_s, od_s, xs2, ms[2], ms[3], ms[4], norm2_w, w_out_bf, wr_hi, wr_lo, b_rt,
                                       ns_, ns_)

    n_all = np_ + ns_
    h2_all = jnp.concatenate([h2_p, h2_s], axis=0)
    lg_all = jnp.concatenate([lg_p, lg_s], axis=0)
    eid, gates = _route(lg_all, _pick_tile(n_all, 512, SUBLANES))
    bm = 256
    tok_sorted, block_j0, dest, block_exp, n_active = _route_and_sort(eid[:, 0:2], bm)
    yb = _moe_experts(tok_sorted, block_j0, block_exp, n_active, h2_all, w_gate, w_up, w_down, bm)
    out_p = _combine(x1_p, dest[:np_], yb, gates[:np_], mp[5], tm_o, tp)
    out_s = _combine(x1_s, dest[np_:], yb, gates[np_:], ms[5], ns_, ns_)

    valid = lambda a: a.reshape(bs, rows, -1)[:, :ts]
    conv_p = proj_p.reshape(bp, tp, PROJ_PACKED)[:, tp - (CONV_WIDTH - 1):, C_CONV:C_CONV + CONV_CHANNELS]
    conv_s = proj_s.reshape(bs, rows, PROJ_PACKED)[:, ts - (CONV_WIDTH - 1):ts, C_CONV:C_CONV + CONV_CHANNELS]
    return (out_p.reshape(bp, tp, d), valid(out_s),
            kf_p.reshape(bp, tp, N_KV_HEADS, HEAD_DIM),
            proj_p[:, C_VA:C_VA + KV_WIDTH].reshape(bp, tp, N_KV_HEADS, HEAD_DIM),
            kif_p.reshape(bp, tp, IDX_DIM), ssm_p, conv_p,
            valid(kf_s).reshape(bs, ts, N_KV_HEADS, HEAD_DIM),
            valid(proj_s[:, C_VA:C_VA + KV_WIDTH]).reshape(bs, ts, N_KV_HEADS, HEAD_DIM),
            valid(kif_s), ssm_s, conv_s)


def kernel(x_prompt, x_sample, cache_k, cache_v, cache_idx_k, state_ssm, state_conv, page_table, c_prompt, c_sample,
           w_in, w_out, conv_w, a_log, dt_bias, q_norm_w, k_norm_w, idx_k_norm_w, o_norm_w, norm1_w, norm2_w,
           w_ada, b_ada, w_group, b_group, w_router, b_router, w_gate, w_up, w_down):
    depth = w_in.shape[0]
    yp, ys = x_prompt, x_sample
    per_layer = []
    for l in range(depth):
        res = _layer(l, yp, ys, cache_k, cache_v, cache_idx_k, state_ssm[l], state_conv[l], page_table,
                     c_prompt, c_sample, w_in[l], w_out[l], conv_w[l], a_log[l], dt_bias[l], q_norm_w[l],
                     k_norm_w[l], idx_k_norm_w[l], o_norm_w[l], norm1_w[l], norm2_w[l], w_ada[l], b_ada[l],
                     w_group[l], b_group[l], w_router[l], b_router[l], w_gate[l], w_up[l], w_down[l])
        yp, ys = res[0], res[1]
        per_layer.append(res[2:])
    stacked = tuple(jnp.stack([pl_[j] for pl_ in per_layer]) for j in range(10))
    return (yp, ys) + stacked
```

```python
import functools

import jax
import jax.numpy as jnp
import numpy as np
from jax import lax
from jax.experimental import pallas as pl
from jax.experimental.pallas import tpu as pltpu

F32 = jnp.float32
BF16 = jnp.bfloat16
I32 = jnp.int32

HEAD_DIM = 128
N_ATTN_HEADS = 8
N_KV_HEADS = 2
KV_GROUP = N_ATTN_HEADS // N_KV_HEADS
N_DELTA_HEADS = 8
N_IDX_HEADS = 16
IDX_DIM = 64
ATTN_WIDTH = N_ATTN_HEADS * HEAD_DIM
KV_WIDTH = N_KV_HEADS * HEAD_DIM
DELTA_WIDTH = N_DELTA_HEADS * HEAD_DIM
IDX_WIDTH = N_IDX_HEADS * IDX_DIM
CONV_CHANNELS = 3 * DELTA_WIDTH
TOPK_MAX = 256
ROPE_THETA = 500000.0
ROPE_FRACTION = 4
CONV_WIDTH = 4
DELTA_CHUNK = 64
N_GROUPS = 8
EXPERTS_PER_GROUP = 8
N_EXPERTS = N_GROUPS * EXPERTS_PER_GROUP
N_MOD = 6
EPS = 1e-6
PAGE_SIZE = 128
PROJ_SIZES = (ATTN_WIDTH, KV_WIDTH, KV_WIDTH, IDX_WIDTH, IDX_DIM, N_IDX_HEADS,
              DELTA_WIDTH, DELTA_WIDTH, DELTA_WIDTH, DELTA_WIDTH, N_DELTA_HEADS, N_DELTA_HEADS)

LANES = 128
SUBLANES = 8
VMEM_LIMIT = 56 * 1024 * 1024

C_QA = 0
C_QI = 1024
C_ZD = 2048
C_CONV = 3072
C_KA = 6144
C_VA = 6400
C_MISC = 6656
PROJ_PACKED = 6912
M_KI = 0
M_WI = 64
M_BD = 80
M_AD = 88

Q_TILE = 128
KEY_CHUNK = 256
SEL_SPAN = 512
NEG_INF_KEY = -2139095041
SAMPLE_ROWS = 16
SAMPLE_PAGES_PER_STEP = 16
NEG_BIG = -1e30
INT_MIN = -2147483648
INT_MAX = 2147483647


def _cparams(sem):
    return pltpu.CompilerParams(dimension_semantics=sem, vmem_limit_bytes=VMEM_LIMIT)


def _dot(a, b):
    return jnp.dot(a, b, preferred_element_type=F32)


def _dot_nt(a, b):
    return lax.dot_general(a, b, (((1,), (1,)), ((), ())), preferred_element_type=F32)


def _dot_tn(a, b):
    return lax.dot_general(a, b, (((0,), (0,)), ((), ())), preferred_element_type=F32)


def _sigmoid(x):
    return 1.0 / (1.0 + jnp.exp(-x))


def _silu(x):
    return x * _sigmoid(x)


def _softplus(x):
    return jnp.maximum(x, 0.0) + jnp.log(1.0 + jnp.exp(-jnp.abs(x)))


def _ada_kernel(c_ref, w_ref, b_ref, o_ref):
    s = _silu(c_ref[...]).astype(BF16)
    o_ref[...] = _dot(s, w_ref[...].astype(BF16)) + b_ref[...]


def _ada_modulation(c, w_ada, b_ada):
    r, d = c.shape
    n = w_ada.shape[1]
    tn = 1024 if n % 1024 == 0 else n
    return pl.pallas_call(
        _ada_kernel,
        name="ada_mod",
        grid=(n // tn,),
        in_specs=[pl.BlockSpec((r, d), lambda j: (0, 0)),
                  pl.BlockSpec((d, tn), lambda j: (0, j)),
                  pl.BlockSpec((1, tn), lambda j: (0, j))],
        out_specs=pl.BlockSpec((r, tn), lambda j: (0, j)),
        out_shape=jax.ShapeDtypeStruct((r, n), F32),
        compiler_params=_cparams(("parallel",)),
    )(c, w_ada, b_ada.reshape(1, n))


INPROJ_COLS = 1152


def _inproj_kernel(x_ref, sh_ref, sc_ref, nw_ref, w_ref, o_ref):
    x = x_ref[...]
    y = x * lax.rsqrt(jnp.mean(x * x, axis=-1, keepdims=True) + EPS) * nw_ref[...]
    h = (y * (1.0 + sc_ref[0]) + sh_ref[0]).astype(BF16)
    for c0 in range(0, o_ref.shape[1], INPROJ_COLS):
        o_ref[:, c0:c0 + INPROJ_COLS] = _dot(h, w_ref[:, c0:c0 + INPROJ_COLS])


def _in_projection(x2d, shift, scale, norm_w, w_packed, tm, rows_per_mod_block):
    n, d = x2d.shape
    np_ = w_packed.shape[1]
    r = shift.shape[1]
    tiles_per_mod = rows_per_mod_block // tm
    mod_spec = pl.BlockSpec((1, r, d), lambda i: (i // tiles_per_mod, 0, 0))
    return pl.pallas_call(
        _inproj_kernel,
        name="in_proj",
        grid=(n // tm,),
        in_specs=[pl.BlockSpec((tm, d), lambda i: (i, 0)),
                  mod_spec, mod_spec,
                  pl.BlockSpec((1, d), lambda i: (0, 0)),
                  pl.BlockSpec((d, np_), lambda i: (0, 0), pipeline_mode=pl.Buffered(1))],
        out_specs=pl.BlockSpec((tm, np_), lambda i: (i, 0)),
        out_shape=jax.ShapeDtypeStruct((n, np_), F32),
        compiler_params=_cparams(("parallel",)),
    )(x2d, shift, scale, norm_w.reshape(1, d), w_packed)


def _rope(x, tab, rot):
    c = tab[:, 0:LANES]
    s1 = tab[:, LANES:2 * LANES]
    s2 = tab[:, 2 * LANES:3 * LANES]
    return x * c + pltpu.roll(x, LANES - rot, 1) * s1 + pltpu.roll(x, rot, 1) * s2


def _rms_head(x, w):
    return x * lax.rsqrt(jnp.mean(x * x, axis=-1, keepdims=True) + EPS) * w


def _prep_kernel(qa_ref, qi_ref, ka_ref, va_ref, misc_ref, tabm_ref, tabi_ref, qw_ref, kw_ref, iw_ref,
                 q_ref, kf_ref, kb_ref, vb_ref, qib_ref, kif_ref, kib_ref, *, transpose_v):
    tabm = tabm_ref[0]
    tabi = tabi_ref[0]
    half_main = HEAD_DIM // ROPE_FRACTION // 2
    half_idx = IDX_DIM // ROPE_FRACTION // 2
    for h in range(N_ATTN_HEADS):
        sl = slice(h * HEAD_DIM, (h + 1) * HEAD_DIM)
        y = _rope(_rms_head(qa_ref[:, sl], qw_ref[...]), tabm, half_main)
        q_ref[:, sl] = (y * (HEAD_DIM ** -0.5)).astype(BF16)
    for h in range(N_KV_HEADS):
        sl = slice(h * HEAD_DIM, (h + 1) * HEAD_DIM)
        y = _rope(_rms_head(ka_ref[:, sl], kw_ref[...]), tabm, half_main)
        kf_ref[:, sl] = y
        kb_ref[:, sl] = y.astype(BF16)
    if transpose_v:
        vb_ref[...] = va_ref[...].T.astype(BF16)
    else:
        vb_ref[...] = va_ref[...].astype(BF16)
    for p in range(IDX_WIDTH // LANES):
        sl = slice(p * LANES, (p + 1) * LANES)
        qib_ref[:, sl] = _rope(qi_ref[:, sl], tabi, half_idx).astype(BF16)
    m = misc_ref[...]
    lane = lax.broadcasted_iota(I32, m.shape, 1)
    ki = jnp.where(lane < IDX_DIM, m, 0.0)
    ms = jnp.sum(ki * ki, axis=-1, keepdims=True) * (1.0 / IDX_DIM)
    y = _rope(ki * lax.rsqrt(ms + EPS) * iw_ref[...], tabi, half_idx)
    kif_ref[...] = y[:, 0:IDX_DIM]
    kib_ref[...] = (y + pltpu.roll(y, IDX_DIM, 1)).astype(BF16)


def _rope_tables(pos, head_dim, group):
    d_rot = head_dim // ROPE_FRACTION
    half = d_rot // 2
    inv_freq = jnp.power(ROPE_THETA, -(jnp.arange(half, dtype=F32) * 2.0 / d_rot))
    ang = pos.astype(F32)[:, None] * inv_freq[None, :]
    cos = jnp.cos(ang)
    sin = jnp.sin(ang)
    t = pos.shape[0]
    z = jnp.zeros((t, group - d_rot), F32)
    c = jnp.concatenate([cos, cos, jnp.ones((t, group - d_rot), F32)], axis=1)
    s1 = jnp.concatenate([-sin, jnp.zeros((t, half), F32), z], axis=1)
    s2 = jnp.concatenate([jnp.zeros((t, half), F32), sin, z], axis=1)
    rep = LANES // group
    return jnp.concatenate([jnp.tile(c, (1, rep)), jnp.tile(s1, (1, rep)), jnp.tile(s2, (1, rep))], axis=1)


def _attention_prep(proj, pos, tq, q_norm_w, k_norm_w, idx_k_norm_w, transpose_v):
    n = proj.shape[0]
    p = pos.shape[0]
    g = p // tq
    tabm = _rope_tables(pos, HEAD_DIM, LANES).reshape(g, tq, 3 * LANES)
    tabi = _rope_tables(pos, IDX_DIM, IDX_DIM).reshape(g, tq, 3 * LANES)
    iw = jnp.concatenate([idx_k_norm_w, jnp.zeros((LANES - IDX_DIM,), F32)]).reshape(1, LANES)
    row = lambda w, c: pl.BlockSpec((tq, w), lambda i: (i, c // w))
    tab_spec = pl.BlockSpec((1, tq, 3 * LANES), lambda i: (i % g, 0, 0))
    vec_spec = pl.BlockSpec((1, LANES), lambda i: (0, 0))
    out_row = lambda w: pl.BlockSpec((tq, w), lambda i: (i, 0))
    v_spec = pl.BlockSpec((KV_WIDTH, tq), lambda i: (0, i)) if transpose_v else out_row(KV_WIDTH)
    v_shape = (KV_WIDTH, n) if transpose_v else (n, KV_WIDTH)
    return pl.pallas_call(
        functools.partial(_prep_kernel, transpose_v=transpose_v),
        name="attn_prep",
        grid=(n // tq,),
        in_specs=[row(ATTN_WIDTH, C_QA), row(IDX_WIDTH, C_QI), row(KV_WIDTH, C_KA), row(KV_WIDTH, C_VA),
                  row(LANES, C_MISC), tab_spec, tab_spec, vec_spec, vec_spec, vec_spec],
        out_specs=[out_row(ATTN_WIDTH), out_row(KV_WIDTH), out_row(KV_WIDTH), v_spec,
                   out_row(IDX_WIDTH), out_row(IDX_DIM), out_row(LANES)],
        out_shape=[jax.ShapeDtypeStruct((n, ATTN_WIDTH), BF16),
                   jax.ShapeDtypeStruct((n, KV_WIDTH), F32),
                   jax.ShapeDtypeStruct((n, KV_WIDTH), BF16),
                   jax.ShapeDtypeStruct(v_shape, BF16),
                   jax.ShapeDtypeStruct((n, IDX_WIDTH), BF16),
                   jax.ShapeDtypeStruct((n, IDX_DIM), F32),
                   jax.ShapeDtypeStruct((n, LANES), BF16)],
        compiler_params=_cparams(("parallel",)),
    )(proj, proj, proj, proj, proj, tabm, tabi,
      q_norm_w.reshape(1, LANES), k_norm_w.reshape(1, LANES), iw)


def _sort_key(x):
    b = pltpu.bitcast(x + 0.0, I32)
    return b ^ ((b >> 31) & INT_MAX)


def _kth_largest_key(count_ge, k, shape):
    def body(it, ans_u):
        bit = jnp.left_shift(jnp.int32(1), 31 - it)
        cand_u = ans_u | bit
        cnt = count_ge(cand_u ^ INT_MIN)
        return jnp.where(cnt >= k, cand_u, ans_u)

    ans_u = lax.fori_loop(0, 32, body, jnp.zeros(shape, I32))
    return ans_u ^ INT_MIN


def _tie_index_limit(count_eq_le, need, n_keys, shape):
    nbits = max(1, int(n_keys - 1).bit_length())

    def body(it, lo):
        bit = jnp.left_shift(jnp.int32(1), nbits - 1 - it)
        cand = lo | bit
        cnt = count_eq_le(cand - 1)
        return jnp.where(cnt >= need, lo, cand)

    return lax.fori_loop(0, nbits, body, jnp.zeros(shape, I32))


def _dsa_prompt_kernel(q_ref, qi_ref, misc_ref, k_ref, vt_ref, kx_ref, o_ref,
                       key_scr, qsel_scr, qg_scr, thr_scr, lim_scr, m_scr, l_scr, acc_scr, *, n_sel):
    i = pl.program_id(1)
    tq = Q_TILE
    ck = KEY_CHUNK
    n_ch = (i * tq + tq + ck - 1) // ck
    q_pos = i * tq + lax.broadcasted_iota(I32, (1, tq), 1)
    row_k = lax.broadcasted_iota(I32, (ck, 1), 0)

    lo_half = lax.broadcasted_iota(I32, (tq, LANES), 1) < IDX_DIM
    zero = jnp.zeros((), BF16)
    for p in range(IDX_WIDTH // LANES):
        slab = qi_ref[:, p * LANES:(p + 1) * LANES]
        qsel_scr[(2 * p) * tq:(2 * p + 1) * tq, :] = jnp.where(lo_half, slab, zero)
        qsel_scr[(2 * p + 1) * tq:(2 * p + 2) * tq, :] = jnp.where(lo_half, zero, slab)
    w_t = misc_ref[...].T

    def score_chunk(c, carry):
        off = pl.multiple_of(c * ck, ck)
        s = _dot_nt(kx_ref[pl.ds(off, ck), :], qsel_scr[...])
        acc = jnp.zeros((ck, tq), F32)
        for h in range(N_IDX_HEADS):
            acc = acc + w_t[M_WI + h:M_WI + h + 1, :] * jnp.maximum(s[:, h * tq:(h + 1) * tq], 0.0)
        acc = jnp.where(off + row_k <= q_pos, acc, -jnp.inf)
        key_scr[pl.ds(off, ck), :] = _sort_key(acc)
        return carry

    lax.fori_loop(0, n_ch, score_chunk, 0)

    spc = SEL_SPAN // ck
    n_span = (n_ch + spc - 1) // spc
    neg_key = jnp.full((ck, tq), NEG_INF_KEY, I32)

    def pad_chunk(c, carry):
        key_scr[pl.ds(pl.multiple_of(c * ck, ck), ck), :] = neg_key
        return carry

    lax.fori_loop(n_ch, n_span * spc, pad_chunk, 0)

    thr_scr[...] = jnp.full((1, tq), INT_MIN, I32)
    lim_scr[...] = jnp.full((1, tq), INT_MAX, I32)

    def select_threshold(n_keys):
        def count_where(pred):
            tot = jnp.zeros((SUBLANES, tq), F32)
            for c0 in range(0, n_keys, ck):
                hit = pred(key_scr[c0:c0 + ck, :], c0 + row_k).astype(F32)
                tot = tot + jnp.sum(hit.reshape(ck // SUBLANES, SUBLANES, tq), axis=0)
            return jnp.sum(tot, axis=0, keepdims=True)

        t = _kth_largest_key(lambda cand: count_where(lambda kk, pos: kk >= cand), float(n_sel), (1, tq))
        thr_scr[...] = t
        n_gt = count_where(lambda kk, pos: kk > t)
        n_ge = count_where(lambda kk, pos: kk >= t)

        @pl.when(jnp.max(n_ge) > float(n_sel))
        def _():
            lim_scr[...] = _tie_index_limit(
                lambda idx: count_where(lambda kk, pos: jnp.logical_and(kk == t, pos <= idx)),
                float(n_sel) - n_gt, k_ref.shape[0], (1, tq))

    for spans in range(1, k_ref.shape[0] // SEL_SPAN + 1):
        if spans * SEL_SPAN > n_sel:
            pl.when(jnp.logical_and(n_span == spans, (i + 1) * tq > n_sel))(
                functools.partial(select_threshold, spans * SEL_SPAN))

    thr = thr_scr[...]
    lim = lim_scr[...]

    for g in range(N_KV_HEADS):
        for r in range(KV_GROUP):
            h = g * KV_GROUP + r
            qg_scr[g, r * tq:(r + 1) * tq, :] = q_ref[:, h * HEAD_DIM:(h + 1) * HEAD_DIM]
    m_scr[...] = jnp.full(m_scr.shape, NEG_BIG, F32)
    l_scr[...] = jnp.zeros(l_scr.shape, F32)
    acc_scr[...] = jnp.zeros(acc_scr.shape, F32)

    def attend_chunk(c, carry):
        off = pl.multiple_of(c * ck, ck)
        kk = key_scr[pl.ds(off, ck), :]
        pos = off + row_k
        sel = jnp.logical_or(kk > thr, jnp.logical_and(kk == thr, pos <= lim))
        sel = jnp.logical_and(sel, pos <= q_pos)
        def group_steps(g):
            kc = k_ref[pl.ds(off, ck), g * HEAD_DIM:(g + 1) * HEAD_DIM]
            vt = vt_ref[g * HEAD_DIM:(g + 1) * HEAD_DIM, pl.ds(off, ck)]
            qk = _dot_nt(kc, qg_scr[g])
            yield
            s = jnp.concatenate([jnp.where(sel, qk[:, r * tq:(r + 1) * tq], NEG_BIG) for r in range(KV_GROUP)],
                                axis=1)
            m_old = m_scr[g]
            m_new = jnp.maximum(m_old, jnp.max(s, axis=0, keepdims=True))
            yield
            p = jnp.exp(s - m_new)
            alpha = jnp.exp(m_old - m_new)
            l_scr[g] = alpha * l_scr[g] + jnp.sum(p, axis=0, keepdims=True)
            yield
            acc_scr[g] = alpha * acc_scr[g] + _dot(vt, p.astype(BF16))
            m_scr[g] = m_new

        chains = [group_steps(g) for g in range(N_KV_HEADS)]
        while chains:
            chains = [ch for ch in chains if next(ch, "done") != "done"]
        return carry

    lax.fori_loop(0, n_ch, attend_chunk, 0)
    for g in range(N_KV_HEADS):
        o_t = acc_scr[g] / l_scr[g]
        for r in range(KV_GROUP):
            h = g * KV_GROUP + r
            o_ref[:, h * HEAD_DIM:(h + 1) * HEAD_DIM] = o_t[:, r * tq:(r + 1) * tq].T.astype(BF16)


def _dsa_prompt(q_bf, qi_bf, proj, k_bf, vt_bf, kx_bf, b, t):
    n = b * t
    nq = t // Q_TILE
    n_sel = min(TOPK_MAX, t // 4)
    qrow = lambda w: pl.BlockSpec((Q_TILE, w), lambda bb, i: (bb * nq + i, 0))
    seq = lambda w: pl.BlockSpec((t, w), lambda bb, i: (bb, 0))
    return pl.pallas_call(
        functools.partial(_dsa_prompt_kernel, n_sel=n_sel),
        name="dsa_prompt",
        grid=(b, nq),
        in_specs=[qrow(ATTN_WIDTH), qrow(IDX_WIDTH),
                  pl.BlockSpec((Q_TILE, LANES), lambda bb, i: (bb * nq + i, C_MISC // LANES)),
                  seq(KV_WIDTH), pl.BlockSpec((KV_WIDTH, t), lambda bb, i: (0, bb)), seq(LANES)],
        out_specs=qrow(ATTN_WIDTH),
        out_shape=jax.ShapeDtypeStruct((n, ATTN_WIDTH), BF16),
        scratch_shapes=[pltpu.VMEM((t, Q_TILE), I32),
                        pltpu.VMEM((N_IDX_HEADS * Q_TILE, LANES), BF16),
                        pltpu.VMEM((N_KV_HEADS, KV_GROUP * Q_TILE, HEAD_DIM), BF16),
                        pltpu.VMEM((1, Q_TILE), I32),
                        pltpu.VMEM((1, Q_TILE), I32),
                        pltpu.VMEM((N_KV_HEADS, 1, KV_GROUP * Q_TILE), F32),
                        pltpu.VMEM((N_KV_HEADS, 1, KV_GROUP * Q_TILE), F32),
                        pltpu.VMEM((N_KV_HEADS, HEAD_DIM, KV_GROUP * Q_TILE), F32)],
        compiler_params=_cparams(("parallel", "arbitrary")),
    )(q_bf, qi_bf, proj, k_bf, vt_bf, kx_bf)


def _sample_score_kernel(pt_ref, q_ref, w_ref, kn_ref, *refs, pages, t_valid):
    page_refs = refs[:pages]
    past_ref, new_ref = refs[pages], refs[pages + 1]
    rows = SAMPLE_ROWS
    hr = N_IDX_HEADS * rows
    half = PAGE_SIZE // 2
    q2 = q_ref[0]
    w = w_ref[0]

    def head_sum(s):
        s = w * jnp.maximum(s, 0.0)
        acc = s[0:rows]
        for h in range(1, N_IDX_HEADS):
            acc = acc + s[h * rows:(h + 1) * rows]
        return acc

    for j in range(pages):
        s = _dot_nt(q2, page_refs[j][0, 0].astype(BF16))
        past_ref[0, :, j * PAGE_SIZE:j * PAGE_SIZE + half] = _sort_key(head_sum(s[0:hr]))
        past_ref[0, :, j * PAGE_SIZE + half:(j + 1) * PAGE_SIZE] = _sort_key(head_sum(s[hr:2 * hr]))

    @pl.when(pl.program_id(1) == 0)
    def _():
        kn = jnp.concatenate([kn_ref[...], jnp.zeros((LANES - rows, IDX_DIM), F32)], axis=0).astype(BF16)
        sc = head_sum(_dot_nt(q2[0:hr, 0:IDX_DIM], kn))
        t = lax.broadcasted_iota(I32, sc.shape, 0)
        s = lax.broadcasted_iota(I32, sc.shape, 1)
        ok = jnp.logical_and(s <= t, s < t_valid)
        new_ref[0] = _sort_key(jnp.where(ok, sc, -jnp.inf))


def _page_order_pos(lane_pos):
    p = lane_pos & (PAGE_SIZE - 1)
    half = PAGE_SIZE // 2
    return (lane_pos - p) + 2 * (p & (half - 1)) + (p // half)


def _sample_scores(page_table, q_t, w_col, kif, cache_idx, layer, pages, t_valid):
    bs, n_pages = page_table.shape
    past = n_pages * PAGE_SIZE
    hr = N_IDX_HEADS * SAMPLE_ROWS
    depth, pool = cache_idx.shape[0], cache_idx.shape[1]
    cache_idx = cache_idx.reshape(depth, pool, PAGE_SIZE // 2, 2 * IDX_DIM)
    zq = jnp.zeros_like(q_t)
    q_t = jnp.concatenate([jnp.concatenate([q_t, zq], axis=2), jnp.concatenate([zq, q_t], axis=2)], axis=1)
    page_spec = lambda j: pl.BlockSpec((1, 1, PAGE_SIZE // 2, 2 * IDX_DIM),
                                       lambda b, c, pt: (layer, pt[b, c * pages + j], 0, 0))
    grid_spec = pltpu.PrefetchScalarGridSpec(
        num_scalar_prefetch=1,
        grid=(bs, n_pages // pages),
        in_specs=[pl.BlockSpec((1, 2 * hr, 2 * IDX_DIM), lambda b, c, pt: (b, 0, 0)),
                  pl.BlockSpec((1, hr, 1), lambda b, c, pt: (b, 0, 0)),
                  pl.BlockSpec((SAMPLE_ROWS, IDX_DIM), lambda b, c, pt: (b, 0))]
                 + [page_spec(j) for j in range(pages)],
        out_specs=[pl.BlockSpec((1, SAMPLE_ROWS, pages * PAGE_SIZE), lambda b, c, pt: (b, 0, c)),
                   pl.BlockSpec((1, SAMPLE_ROWS, LANES), lambda b, c, pt: (b, 0, 0))],
    )
    return pl.pallas_call(
        functools.partial(_sample_score_kernel, pages=pages, t_valid=t_valid),
        name="sample_scores",
        grid_spec=grid_spec,
        out_shape=[jax.ShapeDtypeStruct((bs, SAMPLE_ROWS, past), I32),
                   jax.ShapeDtypeStruct((bs, SAMPLE_ROWS, LANES), I32)],
        compiler_params=_cparams(("parallel", "arbitrary")),
    )(page_table, q_t, w_col, kif, *([cache_idx] * pages))


def _sample_attend_kernel(pt_ref, kp_ref, kn_ref, q_ref, knew_ref, vnew_ref, *refs, pages, n_sel, past):
    k_pages = refs[:pages]
    v_pages = refs[pages:2 * pages]
    o_ref = refs[2 * pages]
    thr_scr, lim_scr, m_scr, l_scr, acc_scr = refs[2 * pages + 1:]
    c = pl.program_id(1)
    rows = SAMPLE_ROWS
    span = pages * PAGE_SIZE

    @pl.when(c == 0)
    def _():
        m_scr[...] = jnp.full(m_scr.shape, NEG_BIG, F32)
        l_scr[...] = jnp.zeros(l_scr.shape, F32)
        acc_scr[...] = jnp.zeros(acc_scr.shape, F32)
        kp = kp_ref[0]
        kn = kn_ref[0]
        pos_p = _page_order_pos(lax.broadcasted_iota(I32, kp.shape, 1))
        pos_n = past + lax.broadcasted_iota(I32, kn.shape, 1)

        def count_where(pred):
            return (jnp.sum(pred(kp, pos_p).astype(F32), axis=1, keepdims=True)
                    + jnp.sum(pred(kn, pos_n).astype(F32), axis=1, keepdims=True))

        t = _kth_largest_key(lambda cand: count_where(lambda kk, pos: kk >= cand), float(n_sel), (rows, 1))
        thr_scr[...] = t
        lim_scr[...] = jnp.full((rows, 1), INT_MAX, I32)
        n_gt = count_where(lambda kk, pos: kk > t)
        n_ge = count_where(lambda kk, pos: kk >= t)

        @pl.when(jnp.max(n_ge) > float(n_sel))
        def _():
            lim_scr[...] = _tie_index_limit(
                lambda idx: count_where(lambda kk, pos: jnp.logical_and(kk == t, pos <= idx)),
                float(n_sel) - n_gt, past + LANES, (rows, 1))

    thr = thr_scr[...]
    lim = lim_scr[...]

    def update_steps(g, qg, k_fn, v_fn, sel):
        sel = jnp.concatenate([sel] * KV_GROUP, axis=0)
        qk = _dot_nt(qg, k_fn())
        yield
        s = jnp.where(sel, qk, NEG_BIG)
        m_old = m_scr[g]
        m_new = jnp.maximum(m_old, jnp.max(s, axis=1, keepdims=True))
        yield
        p = jnp.where(sel, jnp.exp(s - m_new), 0.0)
        alpha = jnp.exp(m_old - m_new)
        l_scr[g] = alpha * l_scr[g] + jnp.sum(p, axis=1, keepdims=True)
        yield
        acc_scr[g] = alpha * acc_scr[g] + _dot(p.astype(BF16), v_fn())
        m_scr[g] = m_new

    def run_lockstep(chains):
        while chains:
            chains = [ch for ch in chains if next(ch, "done") != "done"]

    def select(kk, pos):
        return jnp.logical_or(kk > thr, jnp.logical_and(kk == thr, pos <= lim))

    def page_cat(page_refs, g):
        half = PAGE_SIZE // 2
        parts = [r[0, 0, pl.ds(par * N_KV_HEADS + g, half, stride=2 * N_KV_HEADS), :]
                 for r in page_refs for par in range(2)]
        return jnp.concatenate(parts, axis=0).astype(BF16)

    off = pl.multiple_of(c * span, span)
    kk = kp_ref[0, :, pl.ds(off, span)]
    sel_past = select(kk, _page_order_pos(off + lax.broadcasted_iota(I32, kk.shape, 1)))
    q_groups = []
    for g in range(N_KV_HEADS):
        qg = jnp.concatenate(
            [q_ref[:, (g * KV_GROUP + r) * HEAD_DIM:(g * KV_GROUP + r + 1) * HEAD_DIM] for r in range(KV_GROUP)],
            axis=0)
        q_groups.append(qg)
    run_lockstep([update_steps(g, q_groups[g], functools.partial(page_cat, k_pages, g),
                               functools.partial(page_cat, v_pages, g), sel_past) for g in range(N_KV_HEADS)])

    @pl.when(c == pl.num_programs(1) - 1)
    def _():
        kn = kn_ref[0]
        lane = lax.broadcasted_iota(I32, kn.shape, 1)
        sel_new = jnp.logical_and(select(kn, past + lane), lane < rows)
        pad = jnp.zeros((LANES - rows, KV_WIDTH), BF16)
        k_new = jnp.concatenate([knew_ref[...], pad], axis=0)
        v_new = jnp.concatenate([vnew_ref[...], pad], axis=0)
        head = lambda a, g: (lambda: a[:, g * HEAD_DIM:(g + 1) * HEAD_DIM])
        run_lockstep([update_steps(g, q_groups[g], head(k_new, g), head(v_new, g), sel_new)
                      for g in range(N_KV_HEADS)])
        for g in range(N_KV_HEADS):
            o = acc_scr[g] / l_scr[g]
            for r in range(KV_GROUP):
                h = g * KV_GROUP + r
                o_ref[:, h * HEAD_DIM:(h + 1) * HEAD_DIM] = o[r * rows:(r + 1) * rows].astype(BF16)


def _sample_attend(page_table, keys_past, keys_new, q_bf, k_bf, v_bf, cache_k, cache_v, layer, pages, n_sel):
    bs, n_pages = page_table.shape
    past = n_pages * PAGE_SIZE
    depth, pool = cache_k.shape[0], cache_k.shape[1]
    cache_k = cache_k.reshape(depth, pool, PAGE_SIZE * N_KV_HEADS, HEAD_DIM)
    cache_v = cache_v.reshape(depth, pool, PAGE_SIZE * N_KV_HEADS, HEAD_DIM)
    page_spec = lambda j: pl.BlockSpec((1, 1, PAGE_SIZE * N_KV_HEADS, HEAD_DIM),
                                       lambda b, c, pt: (layer, pt[b, c * pages + j], 0, 0))
    row = lambda w: pl.BlockSpec((SAMPLE_ROWS, w), lambda b, c, pt: (b, 0))
    grid_spec = pltpu.PrefetchScalarGridSpec(
        num_scalar_prefetch=1,
        grid=(bs, n_pages // pages),
        in_specs=[pl.BlockSpec((1, SAMPLE_ROWS, past), lambda b, c, pt: (b, 0, 0)),
                  pl.BlockSpec((1, SAMPLE_ROWS, LANES), lambda b, c, pt: (b, 0, 0)),
                  row(ATTN_WIDTH), row(KV_WIDTH), row(KV_WIDTH)]
                 + [page_spec(j) for j in range(pages)] * 2,
        out_specs=row(ATTN_WIDTH),
        scratch_shapes=[pltpu.VMEM((SAMPLE_ROWS, 1), I32),
                        pltpu.VMEM((SAMPLE_ROWS, 1), I32),
                        pltpu.VMEM((N_KV_HEADS, KV_GROUP * SAMPLE_ROWS, 1), F32),
                        pltpu.VMEM((N_KV_HEADS, KV_GROUP * SAMPLE_ROWS, 1), F32),
                        pltpu.VMEM((N_KV_HEADS, KV_GROUP * SAMPLE_ROWS, HEAD_DIM), F32)],
    )
    return pl.pallas_call(
        functools.partial(_sample_attend_kernel, pages=pages, n_sel=n_sel, past=past),
        name="sample_attend",
        grid_spec=grid_spec,
        out_shape=jax.ShapeDtypeStruct((bs * SAMPLE_ROWS, ATTN_WIDTH), BF16),
        compiler_params=_cparams(("parallel", "arbitrary")),
    )(page_table, keys_past, keys_new, q_bf, k_bf, v_bf, *([cache_k] * pages), *([cache_v] * pages))


def _delta_prep_kernel(x_ref, halo_ref, prev_ref, misc_ref, cw_ref, al_ref, dt_ref,
                       qn_ref, kn_ref, vv_ref, bg_ref, xp_scr, *, tiles_per_seq, t_valid, tt):
    i = pl.program_id(0)
    tile_in_seq = i % tiles_per_seq
    halo = jnp.where(tile_in_seq == 0, prev_ref[0], halo_ref[...])
    xp_scr[0:SUBLANES, :] = halo
    xp_scr[SUBLANES:SUBLANES + tt, :] = x_ref[...]
    base = SUBLANES - (CONV_WIDTH - 1)
    outs = (qn_ref, kn_ref, vv_ref)
    for sec in range(3):
        for h in range(N_DELTA_HEADS):
            col = sec * DELTA_WIDTH + h * HEAD_DIM
            sl = slice(col, col + HEAD_DIM)
            y = xp_scr[base:base + tt, sl] * cw_ref[0:1, sl]
            for j in range(1, CONV_WIDTH):
                y = y + xp_scr[base + j:base + j + tt, sl] * cw_ref[j:j + 1, sl]
            y = _silu(y)
            if sec < 2:
                y = y * lax.rsqrt(jnp.sum(y * y, axis=-1, keepdims=True) + EPS)
            if sec == 0:
                y = y * (HEAD_DIM ** -0.5)
            outs[sec][:, h * HEAD_DIM:(h + 1) * HEAD_DIM] = y
    m = misc_ref[...]
    lane = lax.broadcasted_iota(I32, m.shape, 1)
    row = tile_in_seq * tt + lax.broadcasted_iota(I32, m.shape, 0)
    beta = _sigmoid(m)
    g = -jnp.exp(al_ref[...]) * _softplus(m + dt_ref[...])
    is_b = jnp.logical_and(lane >= M_BD, lane < M_BD + N_DELTA_HEADS)
    is_g = jnp.logical_and(lane >= M_AD, lane < M_AD + N_DELTA_HEADS)
    comb = jnp.where(is_b, beta, jnp.where(is_g, g, 0.0))
    comb = jnp.where(row < t_valid, comb, 0.0)
    bg_ref[...] = pltpu.roll(comb, LANES - M_BD, 1)


def _delta_prep(proj, prev8, conv_w, a_log, dt_bias, b, t, tt, t_valid):
    n = proj.shape[0]
    tiles_per_seq = t // tt
    pad_vec = lambda v: jnp.zeros((1, LANES), F32).at[0, M_AD:M_AD + N_DELTA_HEADS].set(v)
    halo_blocks = tt // SUBLANES
    return pl.pallas_call(
        functools.partial(_delta_prep_kernel, tiles_per_seq=tiles_per_seq, t_valid=t_valid, tt=tt),
        name="delta_prep",
        grid=(n // tt,),
        in_specs=[pl.BlockSpec((tt, CONV_CHANNELS), lambda i: (i, C_CONV // CONV_CHANNELS)),
                  pl.BlockSpec((SUBLANES, CONV_CHANNELS),
                               lambda i: (jnp.maximum(i * halo_blocks - 1, 0), C_CONV // CONV_CHANNELS)),
                  pl.BlockSpec((1, SUBLANES, CONV_CHANNELS), lambda i: (i // tiles_per_seq, 0, 0)),
                  pl.BlockSpec((tt, LANES), lambda i: (i, C_MISC // LANES)),
                  pl.BlockSpec((CONV_WIDTH, CONV_CHANNELS), lambda i: (0, 0)),
                  pl.BlockSpec((1, LANES), lambda i: (0, 0)),
                  pl.BlockSpec((1, LANES), lambda i: (0, 0))],
        out_specs=[pl.BlockSpec((tt, DELTA_WIDTH), lambda i: (i, 0))] * 3
                  + [pl.BlockSpec((tt, LANES), lambda i: (i, 0))],
        out_shape=[jax.ShapeDtypeStruct((n, DELTA_WIDTH), F32)] * 3 + [jax.ShapeDtypeStruct((n, LANES), F32)],
        scratch_shapes=[pltpu.VMEM((SUBLANES + tt, CONV_CHANNELS), F32)],
        compiler_params=_cparams(("parallel",)),
    )(proj, proj, prev8, proj, conv_w, pad_vec(a_log), pad_vec(dt_bias))


def _mm(a, b):
    return _dot(a.astype(BF16), b.astype(BF16))


def _mm_nt(a, b):
    return _dot_nt(a.astype(BF16), b.astype(BF16))


DELTA_INV_BLOCK = 16
DELTA_STACK = 4
DELTA_CHUNKS_PER_STEP = 2


def _delta_chunk_kernel(qn_ref, kn_ref, vv_ref, bg_ref, z_ref, s0_ref, ow_ref, od_ref, so_ref, s_scr, *, n_chunks):
    c = pl.program_id(1)
    cs = DELTA_CHUNK

    @pl.when(c == 0)
    def _():
        s_scr[...] = s0_ref[0]

    ltri = (lax.broadcasted_iota(I32, (cs, cs), 0) >= lax.broadcasted_iota(I32, (cs, cs), 1)).astype(BF16)

    def chunk_gates(ch):
        bg = bg_ref[ch * cs:(ch + 1) * cs, :]
        g1 = bg.astype(BF16)
        r1 = bg - g1.astype(F32)
        g2 = r1.astype(BF16)
        g3 = (r1 - g2.astype(F32)).astype(BF16)
        gc = _dot(ltri, g1) + _dot(ltri, g2) + _dot(ltri, g3)
        return bg, gc, gc.T

    gates = [chunk_gates(ch) for ch in range(n_chunks)]
    state_ready = {}

    gh = DELTA_STACK
    rows = gh * cs
    rr = lax.broadcasted_iota(I32, (rows, rows), 0)
    cc = lax.broadcasted_iota(I32, (rows, rows), 1)
    same = (rr // cs) == (cc // cs)
    causal = jnp.logical_and(same, rr >= cc)
    strict = jnp.logical_and(same, rr > cc)
    eye = (rr == cc).astype(F32)
    row_head = lax.broadcasted_iota(I32, (rows, 1), 0) // cs
    def group_steps(ch, grp):
        heads = [grp * gh + j for j in range(gh)]
        bg, gc, gct = gates[ch]
        r0 = ch * cs
        stack = lambda ref: jnp.concatenate([ref[r0:r0 + cs, h * HEAD_DIM:(h + 1) * HEAD_DIM] for h in heads],
                                            axis=0)
        col = lambda a, lane0: jnp.concatenate([a[:, lane0 + h:lane0 + h + 1] for h in heads], axis=0)
        k = stack(kn_ref)
        q = stack(qn_ref)
        v = stack(vv_ref)
        bcol = col(bg, 0)
        gcc = col(gc, N_DELTA_HEADS)
        gcr = jnp.concatenate([gct[N_DELTA_HEADS + h:N_DELTA_HEADS + h + 1, :] for h in heads], axis=1)
        g_last = [gc[cs - 1:cs, N_DELTA_HEADS + h:N_DELTA_HEADS + h + 1] for h in heads]
        glc = jnp.concatenate([jnp.broadcast_to(gl, (cs, 1)) for gl in g_last], axis=0)
        decay = jnp.exp(jnp.where(causal, gcc - gcr, -jnp.inf))
        kb = k * bcol
        eg = jnp.exp(gcc)
        kq = _mm_nt(jnp.concatenate([kb, q], axis=0), k)
        yield
        a = jnp.where(strict, kq[0:rows] * decay, 0.0)
        intra = jnp.where(causal, kq[rows:2 * rows] * decay, 0.0)
        x = -a
        nb = DELTA_INV_BLOCK
        y = jnp.where((rr // nb) == (cc // nb), x, 0.0)
        p = eye + y
        y = _mm(y, y)
        yield
        n_sq = max(1, int(nb - 1).bit_length())
        for lvl in range(1, n_sq):
            if lvl < n_sq - 1:
                py = _mm(jnp.concatenate([p, y], axis=0), y)
                p = p + py[0:rows]
                y = py[rows:2 * rows]
            else:
                p = p + _mm(p, y)
            yield
        size = 2 * nb
        while size <= cs:
            off = jnp.where(jnp.logical_and((rr // size) == (cc // size), (rr // (size // 2)) != (cc // (size // 2))),
                            x, 0.0)
            po = _mm(p, off)
            yield
            p = p + _mm(po, p)
            yield
            size *= 2
        sol = _mm(p, jnp.concatenate([v * bcol, kb * eg], axis=1))
        yield
        u = sol[:, 0:HEAD_DIM]
        w = sol[:, HEAD_DIM:2 * HEAD_DIM]
        lanes_g = slice(grp * gh * HEAD_DIM, (grp + 1) * gh * HEAD_DIM)
        while ch > 0 and not state_ready.get((ch - 1, grp)):
            yield
        s_g = s_scr[:, lanes_g]
        wq_s = _mm(jnp.concatenate([w, q * eg], axis=0), s_g)
        yield
        own = lambda m, r0: jnp.concatenate(
            [m[r0 + j * cs:r0 + (j + 1) * cs, j * HEAD_DIM:(j + 1) * HEAD_DIM] for j in range(gh)], axis=0)
        v_new = u - own(wq_s, 0)
        o = own(wq_s, rows) + _mm(intra, v_new)
        yield
        kg_t = (k * jnp.exp(glc - gcc)).T
        vn_blocks = jnp.concatenate([jnp.where(row_head == j, v_new, 0.0) for j in range(gh)], axis=1)
        s_decay = jnp.concatenate([jnp.broadcast_to(jnp.exp(gl), (1, HEAD_DIM)) for gl in g_last], axis=1)
        s_scr[:, lanes_g] = s_g * s_decay + _mm(kg_t, vn_blocks)
        state_ready[(ch, grp)] = True
        yield
        on = o * lax.rsqrt(jnp.mean(o * o, axis=-1, keepdims=True) + EPS) * ow_ref[...]
        for j, h in enumerate(heads):
            sl = slice(h * HEAD_DIM, (h + 1) * HEAD_DIM)
            od_ref[r0:r0 + cs, sl] = (on[j * cs:(j + 1) * cs] * _silu(z_ref[r0:r0 + cs, sl])).astype(BF16)

    chains = [group_steps(ch, grp) for ch in range(n_chunks) for grp in range(N_DELTA_HEADS // gh)]
    while chains:
        chains = [g for g in chains if next(g, "done") != "done"]

    so_ref[0] = s_scr[...]


def _delta_chunks(qn, kn, vv, bg, zsrc, z_col_block, state0, o_norm_w, b, t):
    n = b * t
    per_step = DELTA_CHUNKS_PER_STEP if (t // DELTA_CHUNK) % DELTA_CHUNKS_PER_STEP == 0 else 1
    nc = t // (DELTA_CHUNK * per_step)
    sw = N_DELTA_HEADS * HEAD_DIM
    row = lambda w, cb=0: pl.BlockSpec((DELTA_CHUNK * per_step, w), lambda bb, c: (bb * nc + c, cb))
    st = pl.BlockSpec((1, HEAD_DIM, sw), lambda bb, c: (bb, 0, 0))
    s_in = state0.transpose(0, 2, 1, 3).reshape(b, HEAD_DIM, sw)
    od, s_out = pl.pallas_call(
        functools.partial(_delta_chunk_kernel, n_chunks=per_step),
        name="delta_chunks",
        grid=(b, nc),
        in_specs=[row(DELTA_WIDTH), row(DELTA_WIDTH), row(DELTA_WIDTH), row(LANES),
                  row(DELTA_WIDTH, z_col_block), st, pl.BlockSpec((1, LANES), lambda bb, c: (0, 0))],
        out_specs=[row(DELTA_WIDTH), st],
        out_shape=[jax.ShapeDtypeStruct((n, DELTA_WIDTH), BF16),
                   jax.ShapeDtypeStruct((b, HEAD_DIM, sw), F32)],
        scratch_shapes=[pltpu.VMEM((HEAD_DIM, sw), F32)],
        compiler_params=_cparams(("parallel", "arbitrary")),
    )(qn, kn, vv, bg, zsrc, s_in, o_norm_w.reshape(1, LANES))
    return od, s_out.reshape(b, HEAD_DIM, N_DELTA_HEADS, HEAD_DIM).transpose(0, 2, 1, 3)


def _outproj_kernel(oa_ref, od_ref, x_ref, g1_ref, sh_ref, sc_ref, nw_ref, wo_ref, wrh_ref, wrl_ref, br_ref,
                    x1_ref, h2_ref, lg_ref):
    mix = _dot(oa_ref[...], wo_ref[0:ATTN_WIDTH, :]) + _dot(od_ref[...], wo_ref[ATTN_WIDTH:ATTN_WIDTH + DELTA_WIDTH, :])
    x1 = x_ref[...] + g1_ref[0] * mix
    x1_ref[...] = x1
    y = x1 * lax.rsqrt(jnp.mean(x1 * x1, axis=-1, keepdims=True) + EPS) * nw_ref[...]
    h2 = y * (1.0 + sc_ref[0]) + sh_ref[0]
    h2_ref[...] = h2
    hb = h2.astype(BF16)
    lo = (h2 - hb.astype(F32)).astype(BF16)
    lg_ref[...] = _dot(hb, wrh_ref[...]) + _dot(lo, wrh_ref[...]) + _dot(hb, wrl_ref[...]) + br_ref[...]


def _out_projection(o_attn, o_delta, x2d, gate1, shift2, scale2, norm2_w, w_out_bf, wr_hi, wr_lo, b_rt,
                    tm, rows_per_mod_block):
    n, d = x2d.shape
    r = gate1.shape[1]
    tiles_per_mod = rows_per_mod_block // tm
    mod_spec = pl.BlockSpec((1, r, d), lambda i: (i // tiles_per_mod, 0, 0))
    row = lambda w: pl.BlockSpec((tm, w), lambda i: (i, 0))
    full = lambda a: pl.BlockSpec(a.shape, lambda i: (0, 0))
    return pl.pallas_call(
        _outproj_kernel,
        name="out_proj",
        grid=(n // tm,),
        in_specs=[row(ATTN_WIDTH), row(DELTA_WIDTH), row(d), mod_spec, mod_spec, mod_spec,
                  pl.BlockSpec((1, d), lambda i: (0, 0)), full(w_out_bf), full(wr_hi), full(wr_lo), full(b_rt)],
        out_specs=[row(d), row(d), row(LANES)],
        out_shape=[jax.ShapeDtypeStruct((n, d), F32), jax.ShapeDtypeStruct((n, d), F32),
                   jax.ShapeDtypeStruct((n, LANES), F32)],
        compiler_params=_cparams(("parallel",)),
    )(o_attn, o_delta, x2d, gate1, shift2, scale2, norm2_w.reshape(1, d), w_out_bf, wr_hi, wr_lo, b_rt)


def _route_kernel(lg_ref, eid_ref, gate_ref):
    x = lg_ref[...]
    lane = lax.broadcasted_iota(I32, x.shape, 1)
    gl = jnp.where(lane < N_GROUPS, x, -jnp.inf)
    ge = jnp.exp(gl - jnp.max(gl, axis=1, keepdims=True))
    p = ge / jnp.sum(ge, axis=1, keepdims=True)
    p_max = jnp.max(p, axis=1, keepdims=True)
    grp = jnp.min(jnp.where(p == p_max, lane, LANES), axis=1, keepdims=True)
    e_lane = lane - N_GROUPS
    in_grp = jnp.logical_and(jnp.logical_and(e_lane >= 0, e_lane < N_EXPERTS),
                             (e_lane >> 3) == grp)
    rl = jnp.where(in_grp, x, -jnp.inf)
    v1 = jnp.max(rl, axis=1, keepdims=True)
    i1 = jnp.min(jnp.where(rl == v1, lane, LANES), axis=1, keepdims=True)
    rl2 = jnp.where(lane == i1, -jnp.inf, rl)
    v2 = jnp.max(rl2, axis=1, keepdims=True)
    i2 = jnp.min(jnp.where(rl2 == v2, lane, LANES), axis=1, keepdims=True)
    t = jnp.exp(v2 - v1)
    den = 1.0 + t
    eid_ref[...] = jnp.where(lane == 0, i1 - N_GROUPS, jnp.where(lane == 1, i2 - N_GROUPS, 0))
    gate_ref[...] = jnp.where(lane == 0, (1.0 / den) * p_max, jnp.where(lane == 1, (t / den) * p_max, 0.0))


def _route(logits, tm):
    n = logits.shape[0]
    spec = pl.BlockSpec((tm, LANES), lambda i: (i, 0))
    return pl.pallas_call(
        _route_kernel,
        name="route",
        grid=(n // tm,),
        in_specs=[spec],
        out_specs=[spec, spec],
        out_shape=[jax.ShapeDtypeStruct((n, LANES), I32), jax.ShapeDtypeStruct((n, LANES), F32)],
        compiler_params=_cparams(("parallel",)),
    )(logits)


def _row_gather(idx_ref, base, n_rows, src_hbm, dst, sem):
    def body(r, carry):
        pltpu.make_async_copy(src_hbm.at[pl.ds(idx_ref[base + r], 1), :], dst.at[pl.ds(r, 1), :], sem).start()
        return carry

    lax.fori_loop(0, n_rows, body, 0, unroll=8)


def _row_gather_wait(n_rows, src_hbm, dst, sem):
    pltpu.make_async_copy(src_hbm.at[pl.ds(0, n_rows), :], dst, sem).wait()


def _moe_kernel(tok_ref, j0_ref, be_ref, na_ref, h_hbm, wg_ref, wu_ref, wd_ref, o_ref,
                x_even, x_odd, sem, wg_scr, wu_scr, wd_scr, *, bm):
    i = pl.program_id(0)
    n_act = na_ref[0]
    bufs = ((x_even, sem.at[0]), (x_odd, sem.at[1]))

    @pl.when(i == 0)
    def _():
        _row_gather(tok_ref, j0_ref[0], bm, h_hbm, x_even, sem.at[0])

    changed = jnp.logical_or(i == 0, be_ref[i] != be_ref[jnp.maximum(i - 1, 0)])

    @pl.when(jnp.logical_and(i < n_act, changed))
    def _():
        wg_scr[...] = wg_ref[0].astype(BF16)
        wu_scr[...] = wu_ref[0].astype(BF16)
        wd_scr[...] = wd_ref[0].astype(BF16)

    for parity in range(2):
        cur, cur_sem = bufs[parity]
        nxt, nxt_sem = bufs[1 - parity]

        @pl.when(jnp.logical_and(i < n_act, i % 2 == parity))
        def _():
            _row_gather_wait(bm, h_hbm, cur, cur_sem)
            base = j0_ref[i + 1]
            for r in range(bm):
                pltpu.make_async_copy(h_hbm.at[pl.ds(tok_ref[base + r], 1), :], nxt.at[pl.ds(r, 1), :],
                                      nxt_sem).start()
            x = cur[...].astype(BF16)
            hid = _silu(_dot(x, wg_scr[...])) * _dot(x, wu_scr[...])
            o_ref[...] = _dot(hid.astype(BF16), wd_scr[...])

        @pl.when(jnp.logical_and(i == n_act, i % 2 == parity))
        def _():
            _row_gather_wait(bm, h_hbm, cur, cur_sem)

    @pl.when(i >= n_act)
    def _():
        o_ref[...] = jnp.zeros(o_ref.shape, F32)


def _moe_experts(tok_sorted, block_j0, block_exp, n_active, h2, w_gate, w_up, w_down, bm):
    ns = block_exp.shape[0] * bm
    d = h2.shape[1]
    f = w_gate.shape[2]
    grid_spec = pltpu.PrefetchScalarGridSpec(
        num_scalar_prefetch=4,
        grid=(ns // bm,),
        in_specs=[pl.BlockSpec(memory_space=pl.ANY),
                  pl.BlockSpec((1, d, f), lambda i, tok, j0, be, na: (be[i], 0, 0)),
                  pl.BlockSpec((1, d, f), lambda i, tok, j0, be, na: (be[i], 0, 0)),
                  pl.BlockSpec((1, f, d), lambda i, tok, j0, be, na: (be[i], 0, 0))],
        out_specs=pl.BlockSpec((bm, d), lambda i, tok, j0, be, na: (i, 0)),
        scratch_shapes=[pltpu.VMEM((bm, d), F32), pltpu.VMEM((bm, d), F32), pltpu.SemaphoreType.DMA((2,)),
                        pltpu.VMEM((d, f), BF16), pltpu.VMEM((d, f), BF16), pltpu.VMEM((f, d), BF16)],
    )
    return pl.pallas_call(
        functools.partial(_moe_kernel, bm=bm),
        name="moe_experts",
        grid_spec=grid_spec,
        out_shape=jax.ShapeDtypeStruct((ns, d), F32),
        compiler_params=_cparams(("arbitrary",)),
    )(tok_sorted, block_j0, block_exp, n_active, h2, w_gate, w_up, w_down)


def _combine_kernel(dest_ref, x1_ref, gt_ref, g2_ref, y_hbm, o_ref, y_buf, sem, *, tm):
    i = pl.program_id(0)
    n = pl.num_programs(0)
    slot = i % 2

    @pl.when(i == 0)
    def _():
        _row_gather(dest_ref, 0, 2 * tm, y_hbm, y_buf.at[0], sem.at[0])

    @pl.when(i + 1 < n)
    def _():
        _row_gather(dest_ref, (i + 1) * 2 * tm, 2 * tm, y_hbm, y_buf.at[1 - slot], sem.at[1 - slot])

    _row_gather_wait(2 * tm, y_hbm, y_buf.at[slot], sem.at[slot])
    gt = gt_ref[...]
    y = y_buf[slot, 0:tm, :] * gt[:, 0:1] + y_buf[slot, tm:2 * tm, :] * gt[:, 1:2]
    o_ref[...] = x1_ref[...] + g2_ref[0] * y


def _combine(x1, dest, y_rows, gates, gate2, tm, rows_per_mod_block):
    n, d = x1.shape
    r = gate2.shape[1]
    tiles_per_mod = rows_per_mod_block // tm
    dest_tiles = dest.reshape(n // tm, tm, 2).transpose(0, 2, 1).reshape(-1)
    row = lambda w: pl.BlockSpec((tm, w), lambda i, dst: (i, 0))
    grid_spec = pltpu.PrefetchScalarGridSpec(
        num_scalar_prefetch=1,
        grid=(n // tm,),
        in_specs=[row(d), row(LANES),
                  pl.BlockSpec((1, r, d), lambda i, dst: (i // tiles_per_mod, 0, 0)),
                  pl.BlockSpec(memory_space=pl.ANY)],
        out_specs=row(d),
        scratch_shapes=[pltpu.VMEM((2, 2 * tm, d), F32), pltpu.SemaphoreType.DMA((2,))],
    )
    return pl.pallas_call(
        functools.partial(_combine_kernel, tm=tm),
        name="moe_combine",
        grid_spec=grid_spec,
        out_shape=jax.ShapeDtypeStruct((n, d), F32),
        compiler_params=_cparams(("arbitrary",)),
    )(dest_tiles, x1, gates, gate2, y_rows)


def _pick_tile(n, pref, mult=16):
    t = min(pref, n)
    while n % t or t % mult:
        t -= 1
    return t


def _pack_w_in(w_in):
    d = w_in.shape[0]
    bounds = np.cumsum(PROJ_SIZES)[:-1].tolist()
    qa, ka, va, qi, ki, wi, qd, kd, vd, zd, bd, ad = jnp.split(w_in, bounds, axis=1)
    used = IDX_DIM + N_IDX_HEADS + 2 * N_DELTA_HEADS
    misc = jnp.concatenate([ki, wi, bd, ad, jnp.zeros((d, LANES - used), w_in.dtype)], axis=1)
    cols = [qa, qi, zd, qd, kd, vd, ka, va, misc]
    width = sum(c.shape[1] for c in cols)
    cols.append(jnp.zeros((d, PROJ_PACKED - width), w_in.dtype))
    return jnp.concatenate(cols, axis=1).astype(BF16)


def _route_and_sort(eid, bm):
    n = eid.shape[0]
    nk = 2 * n
    flat_e = eid.reshape(-1)
    order = jnp.argsort(flat_e, stable=True).astype(I32)
    inv = jnp.argsort(order).astype(I32)
    onehot = flat_e[:, None] == jnp.arange(N_EXPERTS, dtype=I32)[None, :]
    counts = jnp.sum(onehot.astype(I32), axis=0)
    padded = (counts + bm - 1) // bm * bm
    pad_end = jnp.cumsum(padded)
    shift = (pad_end - padded) - (jnp.cumsum(counts) - counts)
    dest = inv + jnp.sum(jnp.where(onehot, shift[None, :], 0), axis=1)
    n_blocks = -(-nk // bm) + N_EXPERTS + 1
    block_exp = jnp.minimum(jnp.searchsorted(pad_end, jnp.arange(n_blocks, dtype=I32) * bm, side='right'),
                            N_EXPERTS - 1).astype(I32)
    n_active = (pad_end[-1] // bm).astype(I32).reshape(1)
    block_j0 = jnp.clip(jnp.arange(n_blocks, dtype=I32) * bm - shift[block_exp], 0, nk)
    tok_sorted = jnp.concatenate([order // 2, jnp.zeros((bm,), I32)])
    return tok_sorted, block_j0, dest.astype(I32).reshape(n, 2), block_exp, n_active


def _layer(layer, yp, ys, cache_k, cache_v, cache_idx, state_ssm, state_conv, page_table, c_prompt, c_sample,
           w_in, w_out, conv_w, a_log, dt_bias, q_norm_w, k_norm_w, idx_k_norm_w, o_norm_w, norm1_w, norm2_w,
           w_ada, b_ada, w_group, b_group, w_router, b_router, w_gate, w_up, w_down):
    bp, tp, d = yp.shape
    bs, ts, _ = ys.shape
    past = page_table.shape[1] * PAGE_SIZE
    rows = SAMPLE_ROWS
    assert CONV_WIDTH - 1 <= ts <= rows and tp % SEL_SPAN == 0 and tp % DELTA_CHUNK == 0

    n_c = bp + bs
    n_c_pad = -(-n_c // SUBLANES) * SUBLANES
    c_all = jnp.concatenate([c_prompt, c_sample, jnp.zeros((n_c_pad - n_c, d), F32)], axis=0)
    mod = _ada_modulation(c_all, w_ada, b_ada)
    mods = jnp.split(mod, N_MOD, axis=1)
    mp = [m[:bp].reshape(bp, 1, d) for m in mods]
    ms = [jnp.repeat(m[bp:bp + bs], rows, axis=0).reshape(1, bs * rows, d) for m in mods]

    w_packed = _pack_w_in(w_in)
    w_out_bf = w_out.astype(BF16)
    w_rt = jnp.concatenate([w_group, w_router, jnp.zeros((d, LANES - N_GROUPS - N_EXPERTS), F32)], axis=1)
    wr_hi = w_rt.astype(BF16)
    wr_lo = (w_rt - wr_hi.astype(F32)).astype(BF16)
    b_rt = jnp.concatenate([b_group, b_router, jnp.zeros((LANES - N_GROUPS - N_EXPERTS,), F32)]).reshape(1, LANES)

    np_ = bp * tp
    xp2 = yp.reshape(np_, d)
    tm_p = _pick_tile(tp, 256)
    proj_p = _in_projection(xp2, mp[0], mp[1], norm1_w, w_packed, tm_p, tp)
    tq_p = _pick_tile(tp, 256)
    q_p, kf_p, kb_p, vt_p, qi_p, kif_p, kx_p = _attention_prep(
        proj_p, jnp.arange(tp), tq_p, q_norm_w, k_norm_w, idx_k_norm_w, True)
    oa_p = _dsa_prompt(q_p, qi_p, proj_p, kb_p, vt_p, kx_p, bp, tp)
    tt_p = _pick_tile(tp, 256)
    qn_p, kn_p, vv_p, bg_p = _delta_prep(proj_p, jnp.zeros((bp, SUBLANES, CONV_CHANNELS), F32), conv_w,
                                         a_log, dt_bias, bp, tp, tt_p, tp)
    od_p, ssm_p = _delta_chunks(qn_p, kn_p, vv_p, bg_p, proj_p, C_ZD // DELTA_WIDTH,
                                jnp.zeros((bp, N_DELTA_HEADS, HEAD_DIM, HEAD_DIM), F32), o_norm_w, bp, tp)
    tm_o = _pick_tile(tp, 256)
    x1_p, h2_p, lg_p = _out_projection(oa_p, od_p, xp2, mp[2], mp[3], mp[4], norm2_w, w_out_bf, wr_hi, wr_lo, b_rt,
                                       tm_o, tp)

    ns_ = bs * rows
    xs2 = jnp.pad(ys, ((0, 0), (0, rows - ts), (0, 0))).reshape(ns_, d)
    tm_s = _pick_tile(ns_, 256)
    proj_s = _in_projection(xs2, ms[0].reshape(ns_ // tm_s, tm_s, d), ms[1].reshape(ns_ // tm_s, tm_s, d),
                            norm1_w, w_packed, tm_s, tm_s)
    q_s, kf_s, kb_s, vb_s, qi_s, kif_s, _ = _attention_prep(
        proj_s, past + jnp.arange(rows), rows, q_norm_w, k_norm_w, idx_k_norm_w, False)
    q_t = qi_s.reshape(bs, rows, N_IDX_HEADS, IDX_DIM).transpose(0, 2, 1, 3).reshape(bs, N_IDX_HEADS * rows, IDX_DIM)
    w_col = proj_s[:, C_MISC + M_WI:C_MISC + M_WI + N_IDX_HEADS].reshape(bs, rows, N_IDX_HEADS)
    w_col = w_col.transpose(0, 2, 1).reshape(bs, N_IDX_HEADS * rows, 1)
    pages = _pick_tile(page_table.shape[1], SAMPLE_PAGES_PER_STEP, 1)
    keys_past, keys_new = _sample_scores(page_table, q_t, w_col, kif_s, cache_idx, layer, pages, ts)
    n_sel_s = min(TOPK_MAX, (past + ts) // 4)
    oa_s = _sample_attend(page_table, keys_past, keys_new, q_s, kb_s, vb_s, cache_k, cache_v, layer, pages, n_sel_s)
    prev8 = jnp.pad(state_conv, ((0, 0), (SUBLANES - (CONV_WIDTH - 1), 0), (0, 0)))
    qn_s, kn_s, vv_s, bg_s = _delta_prep(proj_s, prev8, conv_w, a_log, dt_bias, bs, rows, rows, ts)
    to_chunk = lambda a: jnp.pad(a.reshape(bs, rows, -1), ((0, 0), (0, DELTA_CHUNK - rows), (0, 0))).reshape(
        bs * DELTA_CHUNK, -1)
    z_s = proj_s[:, C_ZD:C_ZD + DELTA_WIDTH]
    od_s, ssm_s = _delta_chunks(to_chunk(qn_s), to_chunk(kn_s), to_chunk(vv_s), to_chunk(bg_s), to_chunk(z_s), 0,
                                state_ssm, o_norm_w, bs, DELTA_CHUNK)
    od_s = od_s.reshape(bs, DELTA_CHUNK, DELTA_WIDTH)[:, :rows].reshape(ns_, DELTA_WIDTH)
    x1_s, h2_s, lg_s = _out_projection(oa_s, od_s, xs2, ms[2], ms[3], ms[4], norm2_w, w_out_bf, wr_hi, wr_lo, b_rt,
                                       ns_, ns_)

    n_all = np_ + ns_
    h2_all = jnp.concatenate([h2_p, h2_s], axis=0)
    lg_all = jnp.concatenate([lg_p, lg_s], axis=0)
    eid, gates = _route(lg_all, _pick_tile(n_all, 512, SUBLANES))
    bm = 256
    tok_sorted, block_j0, dest, block_exp, n_active = _route_and_sort(eid[:, 0:2], bm)
    yb = _moe_experts(tok_sorted, block_j0, block_exp, n_active, h2_all, w_gate, w_up, w_down, bm)
    out_p = _combine(x1_p, dest[:np_], yb, gates[:np_], mp[5], tm_o, tp)
    out_s = _combine(x1_s, dest[np_:], yb, gates[np_:], ms[5], ns_, ns_)

    valid = lambda a: a.reshape(bs, rows, -1)[:, :ts]
    conv_p = proj_p.reshape(bp, tp, PROJ_PACKED)[:, tp - (CONV_WIDTH - 1):, C_CONV:C_CONV + CONV_CHANNELS]
    conv_s = proj_s.reshape(bs, rows, PROJ_PACKED)[:, ts - (CONV_WIDTH - 1):ts, C_CONV:C_CONV + CONV_CHANNELS]
    return (out_p.reshape(bp, tp, d), valid(out_s),
            kf_p.reshape(bp, tp, N_KV_HEADS, HEAD_DIM),
            proj_p[:, C_VA:C_VA + KV_WIDTH].reshape(bp, tp, N_KV_HEADS, HEAD_DIM),
            kif_p.reshape(bp, tp, IDX_DIM), ssm_p, conv_p,
            valid(kf_s).reshape(bs, ts, N_KV_HEADS, HEAD_DIM),
            valid(proj_s[:, C_VA:C_VA + KV_WIDTH]).reshape(bs, ts, N_KV_HEADS, HEAD_DIM),
            valid(kif_s), ssm_s, conv_s)


def kernel(x_prompt, x_sample, cache_k, cache_v, cache_idx_k, state_ssm, state_conv, page_table, c_prompt, c_sample,
           w_in, w_out, conv_w, a_log, dt_bias, q_norm_w, k_norm_w, idx_k_norm_w, o_norm_w, norm1_w, norm2_w,
           w_ada, b_ada, w_group, b_group, w_router, b_router, w_gate, w_up, w_down):
    depth = w_in.shape[0]
    yp, ys = x_prompt, x_sample
    per_layer = []
    for l in range(depth):
        res = _layer(l, yp, ys, cache_k, cache_v, cache_idx_k, state_ssm[l], state_conv[l], page_table,
                     c_prompt, c_sample, w_in[l], w_out[l], conv_w[l], a_log[l], dt_bias[l], q_norm_w[l],
                     k_norm_w[l], idx_k_norm_w[l], o_norm_w[l], norm1_w[l], norm2_w[l], w_ada[l], b_ada[l],
                     w_group[l], b_group[l], w_router[l], b_router[l], w_gate[l], w_up[l], w_down[l])
        yp, ys = res[0], res[1]
        per_layer.append(res[2:])
    stacked = tuple(jnp.stack([pl_[j] for pl_ in per_layer]) for j in range(10))
    return (yp, ys) + stacked
```

```python
import functools

import jax
import jax.numpy as jnp
import numpy as np
from jax import lax
from jax.experimental import pallas as pl
from jax.experimental.pallas import tpu as pltpu

F32 = jnp.float32
BF16 = jnp.bfloat16
I32 = jnp.int32

HEAD_DIM = 128
N_ATTN_HEADS = 8
N_KV_HEADS = 2
KV_GROUP = N_ATTN_HEADS // N_KV_HEADS
N_DELTA_HEADS = 8
N_IDX_HEADS = 16
IDX_DIM = 64
ATTN_WIDTH = N_ATTN_HEADS * HEAD_DIM
KV_WIDTH = N_KV_HEADS * HEAD_DIM
DELTA_WIDTH = N_DELTA_HEADS * HEAD_DIM
IDX_WIDTH = N_IDX_HEADS * IDX_DIM
CONV_CHANNELS = 3 * DELTA_WIDTH
TOPK_MAX = 256
ROPE_THETA = 500000.0
ROPE_FRACTION = 4
CONV_WIDTH = 4
DELTA_CHUNK = 64
N_GROUPS = 8
EXPERTS_PER_GROUP = 8
N_EXPERTS = N_GROUPS * EXPERTS_PER_GROUP
N_MOD = 6
EPS = 1e-6
PAGE_SIZE = 128
PROJ_SIZES = (ATTN_WIDTH, KV_WIDTH, KV_WIDTH, IDX_WIDTH, IDX_DIM, N_IDX_HEADS,
              DELTA_WIDTH, DELTA_WIDTH, DELTA_WIDTH, DELTA_WIDTH, N_DELTA_HEADS, N_DELTA_HEADS)

LANES = 128
SUBLANES = 8
VMEM_LIMIT = 56 * 1024 * 1024

C_QA = 0
C_QI = 1024
C_ZD = 2048
C_CONV = 3072
C_KA = 6144
C_VA = 6400
C_MISC = 6656
PROJ_PACKED = 6912
M_KI = 0
M_WI = 64
M_BD = 80
M_AD = 88

Q_TILE = 128
KEY_CHUNK = 256
SEL_SPAN = 512
NEG_INF_KEY = -2139095041
SAMPLE_ROWS = 16
SAMPLE_PAGES_PER_STEP = 16
NEG_BIG = -1e30
INT_MIN = -2147483648
INT_MAX = 2147483647


def _cparams(sem):
    return pltpu.CompilerParams(dimension_semantics=sem, vmem_limit_bytes=VMEM_LIMIT)


def _dot(a, b):
    return jnp.dot(a, b, preferred_element_type=F32)


def _dot_nt(a, b):
    return lax.dot_general(a, b, (((1,), (1,)), ((), ())), preferred_element_type=F32)


def _dot_tn(a, b):
    return lax.dot_general(a, b, (((0,), (0,)), ((), ())), preferred_element_type=F32)


def _sigmoid(x):
    return 0.5 * jnp.tanh(0.5 * x) + 0.5


def _silu(x):
    return x * _sigmoid(x)


def _softplus(x):
    return jnp.maximum(x, 0.0) + jnp.log(1.0 + jnp.exp(-jnp.abs(x)))


def _ada_kernel(c_ref, w_ref, b_ref, o_ref):
    s = _silu(c_ref[...]).astype(BF16)
    o_ref[...] = _dot(s, w_ref[...].astype(BF16)) + b_ref[...]


def _ada_modulation(c, w_ada, b_ada):
    r, d = c.shape
    n = w_ada.shape[1]
    tn = 1024 if n % 1024 == 0 else n
    return pl.pallas_call(
        _ada_kernel,
        name="ada_mod",
        grid=(n // tn,),
        in_specs=[pl.BlockSpec((r, d), lambda j: (0, 0)),
                  pl.BlockSpec((d, tn), lambda j: (0, j)),
                  pl.BlockSpec((1, tn), lambda j: (0, j))],
        out_specs=pl.BlockSpec((r, tn), lambda j: (0, j)),
        out_shape=jax.ShapeDtypeStruct((r, n), F32),
        compiler_params=_cparams(("parallel",)),
    )(c, w_ada, b_ada.reshape(1, n))


INPROJ_COLS = 1152


def _inproj_kernel(x_ref, sh_ref, sc_ref, nw_ref, w_ref, o_ref):
    x = x_ref[...]
    y = x * lax.rsqrt(jnp.mean(x * x, axis=-1, keepdims=True) + EPS) * nw_ref[...]
    h = (y * (1.0 + sc_ref[0]) + sh_ref[0]).astype(BF16)
    for c0 in range(0, o_ref.shape[1], INPROJ_COLS):
        o_ref[:, c0:c0 + INPROJ_COLS] = _dot(h, w_ref[:, c0:c0 + INPROJ_COLS])


def _in_projection(x2d, shift, scale, norm_w, w_packed, tm, rows_per_mod_block):
    n, d = x2d.shape
    np_ = w_packed.shape[1]
    r = shift.shape[1]
    tiles_per_mod = rows_per_mod_block // tm
    mod_spec = pl.BlockSpec((1, r, d), lambda i: (i // tiles_per_mod, 0, 0))
    return pl.pallas_call(
        _inproj_kernel,
        name="in_proj",
        grid=(n // tm,),
        in_specs=[pl.BlockSpec((tm, d), lambda i: (i, 0)),
                  mod_spec, mod_spec,
                  pl.BlockSpec((1, d), lambda i: (0, 0)),
                  pl.BlockSpec((d, np_), lambda i: (0, 0), pipeline_mode=pl.Buffered(1))],
        out_specs=pl.BlockSpec((tm, np_), lambda i: (i, 0)),
        out_shape=jax.ShapeDtypeStruct((n, np_), F32),
        compiler_params=_cparams(("parallel",)),
    )(x2d, shift, scale, norm_w.reshape(1, d), w_packed)


def _rope(x, tab, rot):
    c = tab[:, 0:LANES]
    s1 = tab[:, LANES:2 * LANES]
    s2 = tab[:, 2 * LANES:3 * LANES]
    return x * c + pltpu.roll(x, LANES - rot, 1) * s1 + pltpu.roll(x, rot, 1) * s2


def _rms_head(x, w):
    return x * lax.rsqrt(jnp.mean(x * x, axis=-1, keepdims=True) + EPS) * w


def _prep_kernel(qa_ref, qi_ref, ka_ref, va_ref, misc_ref, tabm_ref, tabi_ref, qw_ref, kw_ref, iw_ref,
                 q_ref, kf_ref, kb_ref, vb_ref, qib_ref, kif_ref, kib_ref, *, transpose_v):
    tabm = tabm_ref[0]
    tabi = tabi_ref[0]
    half_main = HEAD_DIM // ROPE_FRACTION // 2
    half_idx = IDX_DIM // ROPE_FRACTION // 2
    for h in range(N_ATTN_HEADS):
        sl = slice(h * HEAD_DIM, (h + 1) * HEAD_DIM)
        y = _rope(_rms_head(qa_ref[:, sl], qw_ref[...]), tabm, half_main)
        q_ref[:, sl] = (y * (HEAD_DIM ** -0.5)).astype(BF16)
    for h in range(N_KV_HEADS):
        sl = slice(h * HEAD_DIM, (h + 1) * HEAD_DIM)
        y = _rope(_rms_head(ka_ref[:, sl], kw_ref[...]), tabm, half_main)
        kf_ref[:, sl] = y
        kb_ref[:, sl] = y.astype(BF16)
    if transpose_v:
        vb_ref[...] = va_ref[...].T.astype(BF16)
    else:
        vb_ref[...] = va_ref[...].astype(BF16)
    for p in range(IDX_WIDTH // LANES):
        sl = slice(p * LANES, (p + 1) * LANES)
        qib_ref[:, sl] = _rope(qi_ref[:, sl], tabi, half_idx).astype(BF16)
    m = misc_ref[...]
    lane = lax.broadcasted_iota(I32, m.shape, 1)
    ki = jnp.where(lane < IDX_DIM, m, 0.0)
    ms = jnp.sum(ki * ki, axis=-1, keepdims=True) * (1.0 / IDX_DIM)
    y = _rope(ki * lax.rsqrt(ms + EPS) * iw_ref[...], tabi, half_idx)
    kif_ref[...] = y[:, 0:IDX_DIM]
    kib_ref[...] = (y + pltpu.roll(y, IDX_DIM, 1)).astype(BF16)


def _rope_tables(pos, head_dim, group):
    d_rot = head_dim // ROPE_FRACTION
    half = d_rot // 2
    inv_freq = jnp.power(ROPE_THETA, -(jnp.arange(half, dtype=F32) * 2.0 / d_rot))
    ang = pos.astype(F32)[:, None] * inv_freq[None, :]
    cos = jnp.cos(ang)
    sin = jnp.sin(ang)
    t = pos.shape[0]
    z = jnp.zeros((t, group - d_rot), F32)
    c = jnp.concatenate([cos, cos, jnp.ones((t, group - d_rot), F32)], axis=1)
    s1 = jnp.concatenate([-sin, jnp.zeros((t, half), F32), z], axis=1)
    s2 = jnp.concatenate([jnp.zeros((t, half), F32), sin, z], axis=1)
    rep = LANES // group
    return jnp.concatenate([jnp.tile(c, (1, rep)), jnp.tile(s1, (1, rep)), jnp.tile(s2, (1, rep))], axis=1)


def _attention_prep(proj, pos, tq, q_norm_w, k_norm_w, idx_k_norm_w, transpose_v):
    n = proj.shape[0]
    p = pos.shape[0]
    g = p // tq
    tabm = _rope_tables(pos, HEAD_DIM, LANES).reshape(g, tq, 3 * LANES)
    tabi = _rope_tables(pos, IDX_DIM, IDX_DIM).reshape(g, tq, 3 * LANES)
    iw = jnp.concatenate([idx_k_norm_w, jnp.zeros((LANES - IDX_DIM,), F32)]).reshape(1, LANES)
    row = lambda w, c: pl.BlockSpec((tq, w), lambda i: (i, c // w))
    tab_spec = pl.BlockSpec((1, tq, 3 * LANES), lambda i: (i % g, 0, 0))
    vec_spec = pl.BlockSpec((1, LANES), lambda i: (0, 0))
    out_row = lambda w: pl.BlockSpec((tq, w), lambda i: (i, 0))
    v_spec = pl.BlockSpec((KV_WIDTH, tq), lambda i: (0, i)) if transpose_v else out_row(KV_WIDTH)
    v_shape = (KV_WIDTH, n) if transpose_v else (n, KV_WIDTH)
    return pl.pallas_call(
        functools.partial(_prep_kernel, transpose_v=transpose_v),
        name="attn_prep",
        grid=(n // tq,),
        in_specs=[row(ATTN_WIDTH, C_QA), row(IDX_WIDTH, C_QI), row(KV_WIDTH, C_KA), row(KV_WIDTH, C_VA),
                  row(LANES, C_MISC), tab_spec, tab_spec, vec_spec, vec_spec, vec_spec],
        out_specs=[out_row(ATTN_WIDTH), out_row(KV_WIDTH), out_row(KV_WIDTH), v_spec,
                   out_row(IDX_WIDTH), out_row(IDX_DIM), out_row(LANES)],
        out_shape=[jax.ShapeDtypeStruct((n, ATTN_WIDTH), BF16),
                   jax.ShapeDtypeStruct((n, KV_WIDTH), F32),
                   jax.ShapeDtypeStruct((n, KV_WIDTH), BF16),
                   jax.ShapeDtypeStruct(v_shape, BF16),
                   jax.ShapeDtypeStruct((n, IDX_WIDTH), BF16),
                   jax.ShapeDtypeStruct((n, IDX_DIM), F32),
                   jax.ShapeDtypeStruct((n, LANES), BF16)],
        compiler_params=_cparams(("parallel",)),
    )(proj, proj, proj, proj, proj, tabm, tabi,
      q_norm_w.reshape(1, LANES), k_norm_w.reshape(1, LANES), iw)


def _sort_key(x):
    b = pltpu.bitcast(x + 0.0, I32)
    return b ^ ((b >> 31) & INT_MAX)


def _kth_largest_key(count_ge, k, shape, n_total):
    def body(it, carry):
        ans_u, n_ge = carry
        bit = jnp.left_shift(jnp.int32(1), 31 - it)
        cand_u = ans_u | bit
        cnt = count_ge(cand_u ^ INT_MIN)
        ok = cnt >= k
        return jnp.where(ok, cand_u, ans_u), jnp.where(ok, cnt, n_ge)

    ans_u, n_ge = lax.fori_loop(0, 32, body, (jnp.zeros(shape, I32), jnp.full(shape, float(n_total), F32)))
    return ans_u ^ INT_MIN, n_ge


def _tie_index_limit(count_eq_le, need, n_keys, shape):
    nbits = max(1, int(n_keys - 1).bit_length())

    def body(it, lo):
        bit = jnp.left_shift(jnp.int32(1), nbits - 1 - it)
        cand = lo | bit
        cnt = count_eq_le(cand - 1)
        return jnp.where(cnt >= need, lo, cand)

    return lax.fori_loop(0, nbits, body, jnp.zeros(shape, I32))


def _dsa_prompt_kernel(q_ref, qi_ref, misc_ref, k_ref, vt_ref, kx_ref, o_ref,
                       key_scr, qsel_scr, qg_scr, thr_scr, lim_scr, m_scr, l_scr, acc_scr, *, n_sel):
    i = pl.program_id(1)
    tq = Q_TILE
    ck = KEY_CHUNK
    n_ch = (i * tq + tq + ck - 1) // ck
    q_pos = i * tq + lax.broadcasted_iota(I32, (1, tq), 1)
    row_k = lax.broadcasted_iota(I32, (ck, 1), 0)

    lo_half = lax.broadcasted_iota(I32, (tq, LANES), 1) < IDX_DIM
    zero = jnp.zeros((), BF16)
    for p in range(IDX_WIDTH // LANES):
        slab = qi_ref[:, p * LANES:(p + 1) * LANES]
        qsel_scr[(2 * p) * tq:(2 * p + 1) * tq, :] = jnp.where(lo_half, slab, zero)
        qsel_scr[(2 * p + 1) * tq:(2 * p + 2) * tq, :] = jnp.where(lo_half, zero, slab)
    w_t = misc_ref[...].T

    def score_chunk(c, carry):
        off = pl.multiple_of(c * ck, ck)
        s = _dot_nt(kx_ref[pl.ds(off, ck), :], qsel_scr[...])
        acc = jnp.zeros((ck, tq), F32)
        for h in range(N_IDX_HEADS):
            acc = acc + w_t[M_WI + h:M_WI + h + 1, :] * jnp.maximum(s[:, h * tq:(h + 1) * tq], 0.0)
        acc = jnp.where(off + row_k <= q_pos, acc, -jnp.inf)
        key_scr[pl.ds(off, ck), :] = _sort_key(acc)
        return carry

    lax.fori_loop(0, n_ch, score_chunk, 0)

    spc = SEL_SPAN // ck
    n_span = (n_ch + spc - 1) // spc
    neg_key = jnp.full((ck, tq), NEG_INF_KEY, I32)

    def pad_chunk(c, carry):
        key_scr[pl.ds(pl.multiple_of(c * ck, ck), ck), :] = neg_key
        return carry

    lax.fori_loop(n_ch, n_span * spc, pad_chunk, 0)

    thr_scr[...] = jnp.full((1, tq), INT_MIN, I32)
    lim_scr[...] = jnp.full((1, tq), INT_MAX, I32)

    def select_threshold(n_keys):
        def count_where(pred):
            tot = jnp.zeros((SUBLANES, tq), F32)
            for c0 in range(0, n_keys, ck):
                hit = pred(key_scr[c0:c0 + ck, :], c0 + row_k).astype(F32)
                tot = tot + jnp.sum(hit.reshape(ck // SUBLANES, SUBLANES, tq), axis=0)
            return jnp.sum(tot, axis=0, keepdims=True)

        t, n_ge = _kth_largest_key(lambda cand: count_where(lambda kk, pos: kk >= cand), float(n_sel), (1, tq),
                                   n_keys)
        thr_scr[...] = t

        @pl.when(jnp.max(n_ge) > float(n_sel))
        def _():
            n_gt = count_where(lambda kk, pos: kk > t)
            lim_scr[...] = _tie_index_limit(
                lambda idx: count_where(lambda kk, pos: jnp.logical_and(kk == t, pos <= idx)),
                float(n_sel) - n_gt, k_ref.shape[0], (1, tq))

    for spans in range(1, k_ref.shape[0] // SEL_SPAN + 1):
        if spans * SEL_SPAN > n_sel:
            pl.when(jnp.logical_and(n_span == spans, (i + 1) * tq > n_sel))(
                functools.partial(select_threshold, spans * SEL_SPAN))

    thr = thr_scr[...]
    lim = lim_scr[...]

    for g in range(N_KV_HEADS):
        for r in range(KV_GROUP):
            h = g * KV_GROUP + r
            qg_scr[g, r * tq:(r + 1) * tq, :] = q_ref[:, h * HEAD_DIM:(h + 1) * HEAD_DIM]
    m_scr[...] = jnp.full(m_scr.shape, NEG_BIG, F32)
    l_scr[...] = jnp.zeros(l_scr.shape, F32)
    acc_scr[...] = jnp.zeros(acc_scr.shape, F32)

    def attend_chunk(c, carry):
        off = pl.multiple_of(c * ck, ck)
        kk = key_scr[pl.ds(off, ck), :]
        pos = off + row_k
        sel = jnp.logical_or(kk > thr, jnp.logical_and(kk == thr, pos <= lim))
        sel = jnp.logical_and(sel, pos <= q_pos)
        def group_steps(g):
            kc = k_ref[pl.ds(off, ck), g * HEAD_DIM:(g + 1) * HEAD_DIM]
            vt = vt_ref[g * HEAD_DIM:(g + 1) * HEAD_DIM, pl.ds(off, ck)]
            qk = _dot_nt(kc, qg_scr[g])
            yield
            s = jnp.concatenate([jnp.where(sel, qk[:, r * tq:(r + 1) * tq], NEG_BIG) for r in range(KV_GROUP)],
                                axis=1)
            m_old = m_scr[g]
            m_new = jnp.maximum(m_old, jnp.max(s, axis=0, keepdims=True))
            yield
            p = jnp.exp(s - m_new)
            alpha = jnp.exp(m_old - m_new)
            l_scr[g] = alpha * l_scr[g] + jnp.sum(p, axis=0, keepdims=True)
            yield
            acc_scr[g] = alpha * acc_scr[g] + _dot(vt, p.astype(BF16))
            m_scr[g] = m_new

        chains = [group_steps(g) for g in range(N_KV_HEADS)]
        while chains:
            chains = [ch for ch in chains if next(ch, "done") != "done"]
        return carry

    lax.fori_loop(0, n_ch, attend_chunk, 0)
    for g in range(N_KV_HEADS):
        o_t = acc_scr[g] / l_scr[g]
        for r in range(KV_GROUP):
            h = g * KV_GROUP + r
            o_ref[:, h * HEAD_DIM:(h + 1) * HEAD_DIM] = o_t[:, r * tq:(r + 1) * tq].T.astype(BF16)


def _dsa_prompt(q_bf, qi_bf, proj, k_bf, vt_bf, kx_bf, b, t):
    n = b * t
    nq = t // Q_TILE
    n_sel = min(TOPK_MAX, t // 4)
    qrow = lambda w: pl.BlockSpec((Q_TILE, w), lambda bb, i: (bb * nq + i, 0))
    seq = lambda w: pl.BlockSpec((t, w), lambda bb, i: (bb, 0))
    return pl.pallas_call(
        functools.partial(_dsa_prompt_kernel, n_sel=n_sel),
        name="dsa_prompt",
        grid=(b, nq),
        in_specs=[qrow(ATTN_WIDTH), qrow(IDX_WIDTH),
                  pl.BlockSpec((Q_TILE, LANES), lambda bb, i: (bb * nq + i, C_MISC // LANES)),
                  seq(KV_WIDTH), pl.BlockSpec((KV_WIDTH, t), lambda bb, i: (0, bb)), seq(LANES)],
        out_specs=qrow(ATTN_WIDTH),
        out_shape=jax.ShapeDtypeStruct((n, ATTN_WIDTH), BF16),
        scratch_shapes=[pltpu.VMEM((t, Q_TILE), I32),
                        pltpu.VMEM((N_IDX_HEADS * Q_TILE, LANES), BF16),
                        pltpu.VMEM((N_KV_HEADS, KV_GROUP * Q_TILE, HEAD_DIM), BF16),
                        pltpu.VMEM((1, Q_TILE), I32),
                        pltpu.VMEM((1, Q_TILE), I32),
                        pltpu.VMEM((N_KV_HEADS, 1, KV_GROUP * Q_TILE), F32),
                        pltpu.VMEM((N_KV_HEADS, 1, KV_GROUP * Q_TILE), F32),
                        pltpu.VMEM((N_KV_HEADS, HEAD_DIM, KV_GROUP * Q_TILE), F32)],
        compiler_params=_cparams(("parallel", "arbitrary")),
    )(q_bf, qi_bf, proj, k_bf, vt_bf, kx_bf)


def _sample_score_kernel(pt_ref, q_ref, w_ref, kn_ref, *refs, pages, t_valid):
    page_refs = refs[:pages]
    past_ref, new_ref = refs[pages], refs[pages + 1]
    rows = SAMPLE_ROWS
    hr = N_IDX_HEADS * rows
    half = PAGE_SIZE // 2
    q2 = q_ref[0]
    w = w_ref[0]

    def head_sum(s):
        s = w * jnp.maximum(s, 0.0)
        acc = s[0:rows]
        for h in range(1, N_IDX_HEADS):
            acc = acc + s[h * rows:(h + 1) * rows]
        return acc

    for j in range(pages):
        s = _dot_nt(q2, page_refs[j][0, 0].astype(BF16))
        past_ref[0, :, j * PAGE_SIZE:j * PAGE_SIZE + half] = _sort_key(head_sum(s[0:hr]))
        past_ref[0, :, j * PAGE_SIZE + half:(j + 1) * PAGE_SIZE] = _sort_key(head_sum(s[hr:2 * hr]))

    @pl.when(pl.program_id(1) == 0)
    def _():
        kn = jnp.concatenate([kn_ref[...], jnp.zeros((LANES - rows, IDX_DIM), F32)], axis=0).astype(BF16)
        sc = head_sum(_dot_nt(q2[0:hr, 0:IDX_DIM], kn))
        t = lax.broadcasted_iota(I32, sc.shape, 0)
        s = lax.broadcasted_iota(I32, sc.shape, 1)
        ok = jnp.logical_and(s <= t, s < t_valid)
        new_ref[0] = _sort_key(jnp.where(ok, sc, -jnp.inf))


def _page_order_pos(lane_pos):
    p = lane_pos & (PAGE_SIZE - 1)
    half = PAGE_SIZE // 2
    return (lane_pos - p) + 2 * (p & (half - 1)) + (p // half)


def _sample_scores(page_table, q_t, w_col, kif, cache_idx, layer, pages, t_valid):
    bs, n_pages = page_table.shape
    past = n_pages * PAGE_SIZE
    hr = N_IDX_HEADS * SAMPLE_ROWS
    depth, pool = cache_idx.shape[0], cache_idx.shape[1]
    cache_idx = cache_idx.reshape(depth, pool, PAGE_SIZE // 2, 2 * IDX_DIM)
    zq = jnp.zeros_like(q_t)
    q_t = jnp.concatenate([jnp.concatenate([q_t, zq], axis=2), jnp.concatenate([zq, q_t], axis=2)], axis=1)
    page_spec = lambda j: pl.BlockSpec((1, 1, PAGE_SIZE // 2, 2 * IDX_DIM),
                                       lambda b, c, pt: (layer, pt[b, c * pages + j], 0, 0))
    grid_spec = pltpu.PrefetchScalarGridSpec(
        num_scalar_prefetch=1,
        grid=(bs, n_pages // pages),
        in_specs=[pl.BlockSpec((1, 2 * hr, 2 * IDX_DIM), lambda b, c, pt: (b, 0, 0)),
                  pl.BlockSpec((1, hr, 1), lambda b, c, pt: (b, 0, 0)),
                  pl.BlockSpec((SAMPLE_ROWS, IDX_DIM), lambda b, c, pt: (b, 0))]
                 + [page_spec(j) for j in range(pages)],
        out_specs=[pl.BlockSpec((1, SAMPLE_ROWS, pages * PAGE_SIZE), lambda b, c, pt: (b, 0, c)),
                   pl.BlockSpec((1, SAMPLE_ROWS, LANES), lambda b, c, pt: (b, 0, 0))],
    )
    return pl.pallas_call(
        functools.partial(_sample_score_kernel, pages=pages, t_valid=t_valid),
        name="sample_scores",
        grid_spec=grid_spec,
        out_shape=[jax.ShapeDtypeStruct((bs, SAMPLE_ROWS, past), I32),
                   jax.ShapeDtypeStruct((bs, SAMPLE_ROWS, LANES), I32)],
        compiler_params=_cparams(("parallel", "arbitrary")),
    )(page_table, q_t, w_col, kif, *([cache_idx] * pages))


def _sample_attend_kernel(pt_ref, kp_ref, kn_ref, q_ref, knew_ref, vnew_ref, *refs, pages, n_sel, past):
    k_pages = refs[:pages]
    v_pages = refs[pages:2 * pages]
    o_ref = refs[2 * pages]
    thr_scr, lim_scr, m_scr, l_scr, acc_scr = refs[2 * pages + 1:]
    c = pl.program_id(1)
    rows = SAMPLE_ROWS
    span = pages * PAGE_SIZE

    @pl.when(c == 0)
    def _():
        m_scr[...] = jnp.full(m_scr.shape, NEG_BIG, F32)
        l_scr[...] = jnp.zeros(l_scr.shape, F32)
        acc_scr[...] = jnp.zeros(acc_scr.shape, F32)
        kp = kp_ref[0]
        kn = kn_ref[0]
        pos_p = _page_order_pos(lax.broadcasted_iota(I32, kp.shape, 1))
        pos_n = past + lax.broadcasted_iota(I32, kn.shape, 1)

        def count_where(pred):
            return (jnp.sum(pred(kp, pos_p).astype(F32), axis=1, keepdims=True)
                    + jnp.sum(pred(kn, pos_n).astype(F32), axis=1, keepdims=True))

        t, n_ge = _kth_largest_key(lambda cand: count_where(lambda kk, pos: kk >= cand), float(n_sel), (rows, 1),
                                   past + LANES)
        thr_scr[...] = t
        lim_scr[...] = jnp.full((rows, 1), INT_MAX, I32)

        @pl.when(jnp.max(n_ge) > float(n_sel))
        def _():
            n_gt = count_where(lambda kk, pos: kk > t)
            lim_scr[...] = _tie_index_limit(
                lambda idx: count_where(lambda kk, pos: jnp.logical_and(kk == t, pos <= idx)),
                float(n_sel) - n_gt, past + LANES, (rows, 1))

    thr = thr_scr[...]
    lim = lim_scr[...]

    def update_steps(g, qg, k_fn, v_fn, sel):
        sel = jnp.concatenate([sel] * KV_GROUP, axis=0)
        qk = _dot_nt(qg, k_fn())
        yield
        s = jnp.where(sel, qk, NEG_BIG)
        m_old = m_scr[g]
        m_new = jnp.maximum(m_old, jnp.max(s, axis=1, keepdims=True))
        yield
        p = jnp.where(sel, jnp.exp(s - m_new), 0.0)
        alpha = jnp.exp(m_old - m_new)
        l_scr[g] = alpha * l_scr[g] + jnp.sum(p, axis=1, keepdims=True)
        yield
        acc_scr[g] = alpha * acc_scr[g] + _dot(p.astype(BF16), v_fn())
        m_scr[g] = m_new

    def run_lockstep(chains):
        while chains:
            chains = [ch for ch in chains if next(ch, "done") != "done"]

    def select(kk, pos):
        return jnp.logical_or(kk > thr, jnp.logical_and(kk == thr, pos <= lim))

    def page_cat(page_refs, g):
        half = PAGE_SIZE // 2
        parts = [r[0, 0, pl.ds(par * N_KV_HEADS + g, half, stride=2 * N_KV_HEADS), :]
                 for r in page_refs for par in range(2)]
        return jnp.concatenate(parts, axis=0).astype(BF16)

    off = pl.multiple_of(c * span, span)
    kk = kp_ref[0, :, pl.ds(off, span)]
    sel_past = select(kk, _page_order_pos(off + lax.broadcasted_iota(I32, kk.shape, 1)))
    q_groups = []
    for g in range(N_KV_HEADS):
        qg = jnp.concatenate(
            [q_ref[:, (g * KV_GROUP + r) * HEAD_DIM:(g * KV_GROUP + r + 1) * HEAD_DIM] for r in range(KV_GROUP)],
            axis=0)
        q_groups.append(qg)
    run_lockstep([update_steps(g, q_groups[g], functools.partial(page_cat, k_pages, g),
                               functools.partial(page_cat, v_pages, g), sel_past) for g in range(N_KV_HEADS)])

    @pl.when(c == pl.num_programs(1) - 1)
    def _():
        kn = kn_ref[0]
        lane = lax.broadcasted_iota(I32, kn.shape, 1)
        sel_new = jnp.logical_and(select(kn, past + lane), lane < rows)
        pad = jnp.zeros((LANES - rows, KV_WIDTH), BF16)
        k_new = jnp.concatenate([knew_ref[...], pad], axis=0)
        v_new = jnp.concatenate([vnew_ref[...], pad], axis=0)
        head = lambda a, g: (lambda: a[:, g * HEAD_DIM:(g + 1) * HEAD_DIM])
        run_lockstep([update_steps(g, q_groups[g], head(k_new, g), head(v_new, g), sel_new)
                      for g in range(N_KV_HEADS)])
        for g in range(N_KV_HEADS):
            o = acc_scr[g] / l_scr[g]
            for r in range(KV_GROUP):
                h = g * KV_GROUP + r
                o_ref[:, h * HEAD_DIM:(h + 1) * HEAD_DIM] = o[r * rows:(r + 1) * rows].astype(BF16)


def _sample_attend(page_table, keys_past, keys_new, q_bf, k_bf, v_bf, cache_k, cache_v, layer, pages, n_sel):
    bs, n_pages = page_table.shape
    past = n_pages * PAGE_SIZE
    depth, pool = cache_k.shape[0], cache_k.shape[1]
    cache_k = cache_k.reshape(depth, pool, PAGE_SIZE * N_KV_HEADS, HEAD_DIM)
    cache_v = cache_v.reshape(depth, pool, PAGE_SIZE * N_KV_HEADS, HEAD_DIM)
    page_spec = lambda j: pl.BlockSpec((1, 1, PAGE_SIZE * N_KV_HEADS, HEAD_DIM),
                                       lambda b, c, pt: (layer, pt[b, c * pages + j], 0, 0))
    row = lambda w: pl.BlockSpec((SAMPLE_ROWS, w), lambda b, c, pt: (b, 0))
    grid_spec = pltpu.PrefetchScalarGridSpec(
        num_scalar_prefetch=1,
        grid=(bs, n_pages // pages),
        in_specs=[pl.BlockSpec((1, SAMPLE_ROWS, past), lambda b, c, pt: (b, 0, 0)),
                  pl.BlockSpec((1, SAMPLE_ROWS, LANES), lambda b, c, pt: (b, 0, 0)),
                  row(ATTN_WIDTH), row(KV_WIDTH), row(KV_WIDTH)]
                 + [page_spec(j) for j in range(pages)] * 2,
        out_specs=row(ATTN_WIDTH),
        scratch_shapes=[pltpu.VMEM((SAMPLE_ROWS, 1), I32),
                        pltpu.VMEM((SAMPLE_ROWS, 1), I32),
                        pltpu.VMEM((N_KV_HEADS, KV_GROUP * SAMPLE_ROWS, 1), F32),
                        pltpu.VMEM((N_KV_HEADS, KV_GROUP * SAMPLE_ROWS, 1), F32),
                        pltpu.VMEM((N_KV_HEADS, KV_GROUP * SAMPLE_ROWS, HEAD_DIM), F32)],
    )
    return pl.pallas_call(
        functools.partial(_sample_attend_kernel, pages=pages, n_sel=n_sel, past=past),
        name="sample_attend",
        grid_spec=grid_spec,
        out_shape=jax.ShapeDtypeStruct((bs * SAMPLE_ROWS, ATTN_WIDTH), BF16),
        compiler_params=_cparams(("parallel", "arbitrary")),
    )(page_table, keys_past, keys_new, q_bf, k_bf, v_bf, *([cache_k] * pages), *([cache_v] * pages))


def _delta_prep_kernel(x_ref, halo_ref, prev_ref, misc_ref, cw_ref, al_ref, dt_ref,
                       qn_ref, kn_ref, vv_ref, bg_ref, xp_scr, *, tiles_per_seq, t_valid, tt):
    i = pl.program_id(0)
    tile_in_seq = i % tiles_per_seq
    halo = jnp.where(tile_in_seq == 0, prev_ref[0], halo_ref[...])
    xp_scr[0:SUBLANES, :] = halo
    xp_scr[SUBLANES:SUBLANES + tt, :] = x_ref[...]
    base = SUBLANES - (CONV_WIDTH - 1)
    outs = (qn_ref, kn_ref, vv_ref)
    for sec in range(3):
        for h in range(N_DELTA_HEADS):
            col = sec * DELTA_WIDTH + h * HEAD_DIM
            sl = slice(col, col + HEAD_DIM)
            y = xp_scr[base:base + tt, sl] * cw_ref[0:1, sl]
            for j in range(1, CONV_WIDTH):
                y = y + xp_scr[base + j:base + j + tt, sl] * cw_ref[j:j + 1, sl]
            y = _silu(y)
            if sec < 2:
                y = y * lax.rsqrt(jnp.sum(y * y, axis=-1, keepdims=True) + EPS)
            if sec == 0:
                y = y * (HEAD_DIM ** -0.5)
            outs[sec][:, h * HEAD_DIM:(h + 1) * HEAD_DIM] = y
    m = misc_ref[...]
    lane = lax.broadcasted_iota(I32, m.shape, 1)
    row = tile_in_seq * tt + lax.broadcasted_iota(I32, m.shape, 0)
    beta = _sigmoid(m)
    g = -jnp.exp(al_ref[...]) * _softplus(m + dt_ref[...])
    is_b = jnp.logical_and(lane >= M_BD, lane < M_BD + N_DELTA_HEADS)
    is_g = jnp.logical_and(lane >= M_AD, lane < M_AD + N_DELTA_HEADS)
    comb = jnp.where(is_b, beta, jnp.where(is_g, g, 0.0))
    comb = jnp.where(row < t_valid, comb, 0.0)
    bg_ref[...] = pltpu.roll(comb, LANES - M_BD, 1)


def _delta_prep(proj, prev8, conv_w, a_log, dt_bias, b, t, tt, t_valid):
    n = proj.shape[0]
    tiles_per_seq = t // tt
    pad_vec = lambda v: jnp.zeros((1, LANES), F32).at[0, M_AD:M_AD + N_DELTA_HEADS].set(v)
    halo_blocks = tt // SUBLANES
    return pl.pallas_call(
        functools.partial(_delta_prep_kernel, tiles_per_seq=tiles_per_seq, t_valid=t_valid, tt=tt),
        name="delta_prep",
        grid=(n // tt,),
        in_specs=[pl.BlockSpec((tt, CONV_CHANNELS), lambda i: (i, C_CONV // CONV_CHANNELS)),
                  pl.BlockSpec((SUBLANES, CONV_CHANNELS),
                               lambda i: (jnp.maximum(i * halo_blocks - 1, 0), C_CONV // CONV_CHANNELS)),
                  pl.BlockSpec((1, SUBLANES, CONV_CHANNELS), lambda i: (i // tiles_per_seq, 0, 0)),
                  pl.BlockSpec((tt, LANES), lambda i: (i, C_MISC // LANES)),
                  pl.BlockSpec((CONV_WIDTH, CONV_CHANNELS), lambda i: (0, 0)),
                  pl.BlockSpec((1, LANES), lambda i: (0, 0)),
                  pl.BlockSpec((1, LANES), lambda i: (0, 0))],
        out_specs=[pl.BlockSpec((tt, DELTA_WIDTH), lambda i: (i, 0))] * 3
                  + [pl.BlockSpec((tt, LANES), lambda i: (i, 0))],
        out_shape=[jax.ShapeDtypeStruct((n, DELTA_WIDTH), F32)] * 3 + [jax.ShapeDtypeStruct((n, LANES), F32)],
        scratch_shapes=[pltpu.VMEM((SUBLANES + tt, CONV_CHANNELS), F32)],
        compiler_params=_cparams(("parallel",)),
    )(proj, proj, prev8, proj, conv_w, pad_vec(a_log), pad_vec(dt_bias))


def _mm(a, b):
    return _dot(a.astype(BF16), b.astype(BF16))


def _mm_nt(a, b):
    return _dot_nt(a.astype(BF16), b.astype(BF16))


DELTA_INV_BLOCK = 16
DELTA_STACK = 4
DELTA_CHUNKS_PER_STEP = 2


def _delta_chunk_kernel(qn_ref, kn_ref, vv_ref, bg_ref, z_ref, s0_ref, ow_ref, od_ref, so_ref, s_scr, *, n_chunks):
    c = pl.program_id(1)
    cs = DELTA_CHUNK

    @pl.when(c == 0)
    def _():
        s_scr[...] = s0_ref[0]

    ltri = (lax.broadcasted_iota(I32, (cs, cs), 0) >= lax.broadcasted_iota(I32, (cs, cs), 1)).astype(BF16)

    def chunk_gates(ch):
        bg = bg_ref[ch * cs:(ch + 1) * cs, :]
        g1 = bg.astype(BF16)
        r1 = bg - g1.astype(F32)
        g2 = r1.astype(BF16)
        g3 = (r1 - g2.astype(F32)).astype(BF16)
        gc = _dot(ltri, g1) + _dot(ltri, g2) + _dot(ltri, g3)
        return bg, gc, gc.T

    gates = [chunk_gates(ch) for ch in range(n_chunks)]
    state_ready = {}

    gh = DELTA_STACK
    rows = gh * cs
    rr = lax.broadcasted_iota(I32, (rows, rows), 0)
    cc = lax.broadcasted_iota(I32, (rows, rows), 1)
    same = (rr // cs) == (cc // cs)
    causal = jnp.logical_and(same, rr >= cc)
    strict = jnp.logical_and(same, rr > cc)
    eye = (rr == cc).astype(F32)
    row_head = lax.broadcasted_iota(I32, (rows, 1), 0) // cs
    def group_steps(ch, grp):
        heads = [grp * gh + j for j in range(gh)]
        bg, gc, gct = gates[ch]
        r0 = ch * cs
        stack = lambda ref: jnp.concatenate([ref[r0:r0 + cs, h * HEAD_DIM:(h + 1) * HEAD_DIM] for h in heads],
                                            axis=0)
        col = lambda a, lane0: jnp.concatenate([a[:, lane0 + h:lane0 + h + 1] for h in heads], axis=0)
        k = stack(kn_ref)
        q = stack(qn_ref)
        v = stack(vv_ref)
        bcol = col(bg, 0)
        gcc = col(gc, N_DELTA_HEADS)
        gcr = jnp.concatenate([gct[N_DELTA_HEADS + h:N_DELTA_HEADS + h + 1, :] for h in heads], axis=1)
        g_last = [gc[cs - 1:cs, N_DELTA_HEADS + h:N_DELTA_HEADS + h + 1] for h in heads]
        glc = jnp.concatenate([jnp.broadcast_to(gl, (cs, 1)) for gl in g_last], axis=0)
        decay = jnp.exp(jnp.where(causal, gcc - gcr, -jnp.inf))
        kb = k * bcol
        eg = jnp.exp(gcc)
        kq = _mm_nt(jnp.concatenate([kb, q], axis=0), k)
        yield
        a = jnp.where(strict, kq[0:rows] * decay, 0.0)
        intra = jnp.where(causal, kq[rows:2 * rows] * decay, 0.0)
        x = -a
        nb = DELTA_INV_BLOCK
        y = jnp.where((rr // nb) == (cc // nb), x, 0.0)
        p = eye + y
        y = _mm(y, y)
        yield
        n_sq = max(1, int(nb - 1).bit_length())
        for lvl in range(1, n_sq):
            if lvl < n_sq - 1:
                py = _mm(jnp.concatenate([p, y], axis=0), y)
                p = p + py[0:rows]
                y = py[rows:2 * rows]
            else:
                p = p + _mm(p, y)
            yield
        size = 2 * nb
        while size <= cs:
            off = jnp.where(jnp.logical_and((rr // size) == (cc // size), (rr // (size // 2)) != (cc // (size // 2))),
                            x, 0.0)
            po = _mm(p, off)
            yield
            p = p + _mm(po, p)
            yield
            size *= 2
        sol = _mm(p, jnp.concatenate([v * bcol, kb * eg], axis=1))
        yield
        u = sol[:, 0:HEAD_DIM]
        w = sol[:, HEAD_DIM:2 * HEAD_DIM]
        lanes_g = slice(grp * gh * HEAD_DIM, (grp + 1) * gh * HEAD_DIM)
        while ch > 0 and not state_ready.get((ch - 1, grp)):
            yield
        s_g = s_scr[:, lanes_g]
        wq_s = _mm(jnp.concatenate([w, q * eg], axis=0), s_g)
        yield
        own = lambda m, r0: jnp.concatenate(
            [m[r0 + j * cs:r0 + (j + 1) * cs, j * HEAD_DIM:(j + 1) * HEAD_DIM] for j in range(gh)], axis=0)
        v_new = u - own(wq_s, 0)
        o = own(wq_s, rows) + _mm(intra, v_new)
        yield
        kg_t = (k * jnp.exp(glc - gcc)).T
        vn_blocks = jnp.concatenate([jnp.where(row_head == j, v_new, 0.0) for j in range(gh)], axis=1)
        s_decay = jnp.concatenate([jnp.broadcast_to(jnp.exp(gl), (1, HEAD_DIM)) for gl in g_last], axis=1)
        s_scr[:, lanes_g] = s_g * s_decay + _mm(kg_t, vn_blocks)
        state_ready[(ch, grp)] = True
        yield
        on = o * lax.rsqrt(jnp.mean(o * o, axis=-1, keepdims=True) + EPS) * ow_ref[...]
        for j, h in enumerate(heads):
            sl = slice(h * HEAD_DIM, (h + 1) * HEAD_DIM)
            od_ref[r0:r0 + cs, sl] = (on[j * cs:(j + 1) * cs] * _silu(z_ref[r0:r0 + cs, sl])).astype(BF16)

    chains = [group_steps(ch, grp) for ch in range(n_chunks) for grp in range(N_DELTA_HEADS // gh)]
    while chains:
        chains = [g for g in chains if next(g, "done") != "done"]

    so_ref[0] = s_scr[...]


def _delta_chunks(qn, kn, vv, bg, zsrc, z_col_block, state0, o_norm_w, b, t):
    n = b * t
    per_step = DELTA_CHUNKS_PER_STEP if (t // DELTA_CHUNK) % DELTA_CHUNKS_PER_STEP == 0 else 1
    nc = t // (DELTA_CHUNK * per_step)
    sw = N_DELTA_HEADS * HEAD_DIM
    row = lambda w, cb=0: pl.BlockSpec((DELTA_CHUNK * per_step, w), lambda bb, c: (bb * nc + c, cb))
    st = pl.BlockSpec((1, HEAD_DIM, sw), lambda bb, c: (bb, 0, 0))
    s_in = state0.transpose(0, 2, 1, 3).reshape(b, HEAD_DIM, sw)
    od, s_out = pl.pallas_call(
        functools.partial(_delta_chunk_kernel, n_chunks=per_step),
        name="delta_chunks",
        grid=(b, nc),
        in_specs=[row(DELTA_WIDTH), row(DELTA_WIDTH), row(DELTA_WIDTH), row(LANES),
                  row(DELTA_WIDTH, z_col_block), st, pl.BlockSpec((1, LANES), lambda bb, c: (0, 0))],
        out_specs=[row(DELTA_WIDTH), st],
        out_shape=[jax.ShapeDtypeStruct((n, DELTA_WIDTH), BF16),
                   jax.ShapeDtypeStruct((b, HEAD_DIM, sw), F32)],
        scratch_shapes=[pltpu.VMEM((HEAD_DIM, sw), F32)],
        compiler_params=_cparams(("parallel", "arbitrary")),
    )(qn, kn, vv, bg, zsrc, s_in, o_norm_w.reshape(1, LANES))
    return od, s_out.reshape(b, HEAD_DIM, N_DELTA_HEADS, HEAD_DIM).transpose(0, 2, 1, 3)


def _outproj_kernel(oa_ref, od_ref, x_ref, g1_ref, sh_ref, sc_ref, nw_ref, wo_ref, wrh_ref, br_ref,
                    x1_ref, h2_ref, lg_ref):
    mix = _dot(oa_ref[...], wo_ref[0:ATTN_WIDTH, :]) + _dot(od_ref[...], wo_ref[ATTN_WIDTH:ATTN_WIDTH + DELTA_WIDTH, :])
    x1 = x_ref[...] + g1_ref[0] * mix
    x1_ref[...] = x1
    y = x1 * lax.rsqrt(jnp.mean(x1 * x1, axis=-1, keepdims=True) + EPS) * nw_ref[...]
    h2 = y * (1.0 + sc_ref[0]) + sh_ref[0]
    h2_ref[...] = h2
    hb = h2.astype(BF16)
    lo = (h2 - hb.astype(F32)).astype(BF16)
    wr = wrh_ref[...]
    both = _dot(hb, wr)
    lg_ref[...] = both[:, 0:LANES] + both[:, LANES:2 * LANES] + _dot(lo, wr[:, 0:LANES]) + br_ref[...]


def _out_projection(o_attn, o_delta, x2d, gate1, shift2, scale2, norm2_w, w_out_bf, wr_both, b_rt,
                    tm, rows_per_mod_block):
    n, d = x2d.shape
    r = gate1.shape[1]
    tiles_per_mod = rows_per_mod_block // tm
    mod_spec = pl.BlockSpec((1, r, d), lambda i: (i // tiles_per_mod, 0, 0))
    row = lambda w: pl.BlockSpec((tm, w), lambda i: (i, 0))
    full = lambda a: pl.BlockSpec(a.shape, lambda i: (0, 0))
    return pl.pallas_call(
        _outproj_kernel,
        name="out_proj",
        grid=(n // tm,),
        in_specs=[row(ATTN_WIDTH), row(DELTA_WIDTH), row(d), mod_spec, mod_spec, mod_spec,
                  pl.BlockSpec((1, d), lambda i: (0, 0)), full(w_out_bf), full(wr_both), full(b_rt)],
        out_specs=[row(d), row(d), row(LANES)],
        out_shape=[jax.ShapeDtypeStruct((n, d), F32), jax.ShapeDtypeStruct((n, d), F32),
                   jax.ShapeDtypeStruct((n, LANES), F32)],
        compiler_params=_cparams(("parallel",)),
    )(o_attn, o_delta, x2d, gate1, shift2, scale2, norm2_w.reshape(1, d), w_out_bf, wr_both, b_rt)


def _route_kernel(lg_ref, eid_ref, gate_ref):
    x = lg_ref[...]
    lane = lax.broadcasted_iota(I32, x.shape, 1)
    gl = jnp.where(lane < N_GROUPS, x, -jnp.inf)
    ge = jnp.exp(gl - jnp.max(gl, axis=1, keepdims=True))
    p = ge / jnp.sum(ge, axis=1, keepdims=True)
    p_max = jnp.max(p, axis=1, keepdims=True)
    grp = jnp.min(jnp.where(p == p_max, lane, LANES), axis=1, keepdims=True)
    e_lane = lane - N_GROUPS
    in_grp = jnp.logical_and(jnp.logical_and(e_lane >= 0, e_lane < N_EXPERTS),
                             (e_lane >> 3) == grp)
    rl = jnp.where(in_grp, x, -jnp.inf)
    v1 = jnp.max(rl, axis=1, keepdims=True)
    i1 = jnp.min(jnp.where(rl == v1, lane, LANES), axis=1, keepdims=True)
    rl2 = jnp.where(lane == i1, -jnp.inf, rl)
    v2 = jnp.max(rl2, axis=1, keepdims=True)
    i2 = jnp.min(jnp.where(rl2 == v2, lane, LANES), axis=1, keepdims=True)
    t = jnp.exp(v2 - v1)
    den = 1.0 + t
    eid_ref[...] = jnp.where(lane == 0, i1 - N_GROUPS, jnp.where(lane == 1, i2 - N_GROUPS, 0))
    gate_ref[...] = jnp.where(lane == 0, (1.0 / den) * p_max, jnp.where(lane == 1, (t / den) * p_max, 0.0))


def _route(logits, tm):
    n = logits.shape[0]
    spec = pl.BlockSpec((tm, LANES), lambda i: (i, 0))
    return pl.pallas_call(
        _route_kernel,
        name="route",
        grid=(n // tm,),
        in_specs=[spec],
        out_specs=[spec, spec],
        out_shape=[jax.ShapeDtypeStruct((n, LANES), I32), jax.ShapeDtypeStruct((n, LANES), F32)],
        compiler_params=_cparams(("parallel",)),
    )(logits)


def _row_gather(idx_ref, base, n_rows, src_hbm, dst, sem):
    def body(r, carry):
        pltpu.make_async_copy(src_hbm.at[pl.ds(idx_ref[base + r], 1), :], dst.at[pl.ds(r, 1), :], sem).start()
        return carry

    lax.fori_loop(0, n_rows, body, 0, unroll=8)


def _row_gather_wait(n_rows, src_hbm, dst, sem):
    pltpu.make_async_copy(src_hbm.at[pl.ds(0, n_rows), :], dst, sem).wait()


def _moe_kernel(tok_ref, j0_ref, be_ref, na_ref, h_hbm, wg_ref, wu_ref, wd_ref, o_ref,
                x_even, x_odd, sem, wg_scr, wu_scr, wd_scr, *, bm):
    i = pl.program_id(0)
    n_act = na_ref[0]
    bufs = ((x_even, sem.at[0]), (x_odd, sem.at[1]))

    @pl.when(i == 0)
    def _():
        _row_gather(tok_ref, j0_ref[0], bm, h_hbm, x_even, sem.at[0])

    changed = jnp.logical_or(i == 0, be_ref[i] != be_ref[jnp.maximum(i - 1, 0)])

    @pl.when(jnp.logical_and(i < n_act, changed))
    def _():
        wg_scr[...] = wg_ref[0].astype(BF16)
        wu_scr[...] = wu_ref[0].astype(BF16)
        wd_scr[...] = wd_ref[0].astype(BF16)

    for parity in range(2):
        cur, cur_sem = bufs[parity]
        nxt, nxt_sem = bufs[1 - parity]

        @pl.when(jnp.logical_and(i < n_act, i % 2 == parity))
        def _():
            _row_gather_wait(bm, h_hbm, cur, cur_sem)
            base = j0_ref[i + 1]
            for r in range(bm):
                pltpu.make_async_copy(h_hbm.at[pl.ds(tok_ref[base + r], 1), :], nxt.at[pl.ds(r, 1), :],
                                      nxt_sem).start()
            x = cur[...].astype(BF16)
            hid = _silu(_dot(x, wg_scr[...])) * _dot(x, wu_scr[...])
            o_ref[...] = _dot(hid.astype(BF16), wd_scr[...])

        @pl.when(jnp.logical_and(i == n_act, i % 2 == parity))
        def _():
            _row_gather_wait(bm, h_hbm, cur, cur_sem)

    @pl.when(i >= n_act)
    def _():
        o_ref[...] = jnp.zeros(o_ref.shape, F32)


def _moe_experts(tok_sorted, block_j0, block_exp, n_active, h2, w_gate, w_up, w_down, bm):
    ns = block_exp.shape[0] * bm
    d = h2.shape[1]
    f = w_gate.shape[2]
    grid_spec = pltpu.PrefetchScalarGridSpec(
        num_scalar_prefetch=4,
        grid=(ns // bm,),
        in_specs=[pl.BlockSpec(memory_space=pl.ANY),
                  pl.BlockSpec((1, d, f), lambda i, tok, j0, be, na: (be[i], 0, 0)),
                  pl.BlockSpec((1, d, f), lambda i, tok, j0, be, na: (be[i], 0, 0)),
                  pl.BlockSpec((1, f, d), lambda i, tok, j0, be, na: (be[i], 0, 0))],
        out_specs=pl.BlockSpec((bm, d), lambda i, tok, j0, be, na: (i, 0)),
        scratch_shapes=[pltpu.VMEM((bm, d), F32), pltpu.VMEM((bm, d), F32), pltpu.SemaphoreType.DMA((2,)),
                        pltpu.VMEM((d, f), BF16), pltpu.VMEM((d, f), BF16), pltpu.VMEM((f, d), BF16)],
    )
    return pl.pallas_call(
        functools.partial(_moe_kernel, bm=bm),
        name="moe_experts",
        grid_spec=grid_spec,
        out_shape=jax.ShapeDtypeStruct((ns, d), F32),
        compiler_params=_cparams(("arbitrary",)),
    )(tok_sorted, block_j0, block_exp, n_active, h2, w_gate, w_up, w_down)


def _combine_kernel(dest_ref, x1_ref, gt_ref, g2_ref, y_hbm, o_ref, y_buf, sem, *, tm):
    i = pl.program_id(0)
    n = pl.num_programs(0)
    slot = i % 2

    @pl.when(i == 0)
    def _():
        _row_gather(dest_ref, 0, 2 * tm, y_hbm, y_buf.at[0], sem.at[0])

    for parity in range(2):
        @pl.when(jnp.logical_and(i + 1 < n, slot == parity))
        def _():
            base = (i + 1) * 2 * tm
            for r in range(2 * tm):
                pltpu.make_async_copy(y_hbm.at[pl.ds(dest_ref[base + r], 1), :],
                                      y_buf.at[1 - parity, pl.ds(r, 1), :], sem.at[1 - parity]).start()

    _row_gather_wait(2 * tm, y_hbm, y_buf.at[slot], sem.at[slot])
    gt = gt_ref[...]
    y = y_buf[slot, 0:tm, :] * gt[:, 0:1] + y_buf[slot, tm:2 * tm, :] * gt[:, 1:2]
    o_ref[...] = x1_ref[...] + g2_ref[0] * y


def _combine(x1, dest, y_rows, gates, gate2, tm, rows_per_mod_block):
    n, d = x1.shape
    r = gate2.shape[1]
    tiles_per_mod = rows_per_mod_block // tm
    dest_tiles = dest.reshape(n // tm, tm, 2).transpose(0, 2, 1).reshape(-1)
    row = lambda w: pl.BlockSpec((tm, w), lambda i, dst: (i, 0))
    grid_spec = pltpu.PrefetchScalarGridSpec(
        num_scalar_prefetch=1,
        grid=(n // tm,),
        in_specs=[row(d), row(LANES),
                  pl.BlockSpec((1, r, d), lambda i, dst: (i // tiles_per_mod, 0, 0)),
                  pl.BlockSpec(memory_space=pl.ANY)],
        out_specs=row(d),
        scratch_shapes=[pltpu.VMEM((2, 2 * tm, d), F32), pltpu.SemaphoreType.DMA((2,))],
    )
    return pl.pallas_call(
        functools.partial(_combine_kernel, tm=tm),
        name="moe_combine",
        grid_spec=grid_spec,
        out_shape=jax.ShapeDtypeStruct((n, d), F32),
        compiler_params=_cparams(("arbitrary",)),
    )(dest_tiles, x1, gates, gate2, y_rows)


def _pick_tile(n, pref, mult=16):
    t = min(pref, n)
    while n % t or t % mult:
        t -= 1
    return t


def _pack_w_in(w_in):
    d = w_in.shape[0]
    bounds = np.cumsum(PROJ_SIZES)[:-1].tolist()
    qa, ka, va, qi, ki, wi, qd, kd, vd, zd, bd, ad = jnp.split(w_in, bounds, axis=1)
    used = IDX_DIM + N_IDX_HEADS + 2 * N_DELTA_HEADS
    misc = jnp.concatenate([ki, wi, bd, ad, jnp.zeros((d, LANES - used), w_in.dtype)], axis=1)
    cols = [qa, qi, zd, qd, kd, vd, ka, va, misc]
    width = sum(c.shape[1] for c in cols)
    cols.append(jnp.zeros((d, PROJ_PACKED - width), w_in.dtype))
    return jnp.concatenate(cols, axis=1).astype(BF16)


def _route_and_sort(eid, bm):
    n = eid.shape[0]
    nk = 2 * n
    flat_e = eid.reshape(-1)
    order = jnp.argsort(flat_e, stable=True).astype(I32)
    inv = jnp.argsort(order).astype(I32)
    onehot = flat_e[:, None] == jnp.arange(N_EXPERTS, dtype=I32)[None, :]
    counts = jnp.sum(onehot.astype(I32), axis=0)
    padded = (counts + bm - 1) // bm * bm
    pad_end = jnp.cumsum(padded)
    shift = (pad_end - padded) - (jnp.cumsum(counts) - counts)
    dest = inv + jnp.sum(jnp.where(onehot, shift[None, :], 0), axis=1)
    n_blocks = -(-nk // bm) + N_EXPERTS + 1
    block_exp = jnp.minimum(jnp.searchsorted(pad_end, jnp.arange(n_blocks, dtype=I32) * bm, side='right'),
                            N_EXPERTS - 1).astype(I32)
    n_active = (pad_end[-1] // bm).astype(I32).reshape(1)
    block_j0 = jnp.clip(jnp.arange(n_blocks, dtype=I32) * bm - shift[block_exp], 0, nk)
    tok_sorted = jnp.concatenate([order // 2, jnp.zeros((bm,), I32)])
    return tok_sorted, block_j0, dest.astype(I32).reshape(n, 2), block_exp, n_active


def _layer(layer, yp, ys, cache_k, cache_v, cache_idx, state_ssm, state_conv, page_table, c_prompt, c_sample,
           w_in, w_out, conv_w, a_log, dt_bias, q_norm_w, k_norm_w, idx_k_norm_w, o_norm_w, norm1_w, norm2_w,
           w_ada, b_ada, w_group, b_group, w_router, b_router, w_gate, w_up, w_down):
    bp, tp, d = yp.shape
    bs, ts, _ = ys.shape
    past = page_table.shape[1] * PAGE_SIZE
    rows = SAMPLE_ROWS
    assert CONV_WIDTH - 1 <= ts <= rows and tp % SEL_SPAN == 0 and tp % DELTA_CHUNK == 0

    n_c = bp + bs
    n_c_pad = -(-n_c // SUBLANES) * SUBLANES
    c_all = jnp.concatenate([c_prompt, c_sample, jnp.zeros((n_c_pad - n_c, d), F32)], axis=0)
    mod = _ada_modulation(c_all, w_ada, b_ada)
    mods = jnp.split(mod, N_MOD, axis=1)
    mp = [m[:bp].reshape(bp, 1, d) for m in mods]
    ms = [jnp.repeat(m[bp:bp + bs], rows, axis=0).reshape(1, bs * rows, d) for m in mods]

    w_packed = _pack_w_in(w_in)
    w_out_bf = w_out.astype(BF16)
    w_rt = jnp.concatenate([w_group, w_router, jnp.zeros((d, LANES - N_GROUPS - N_EXPERTS), F32)], axis=1)
    wr_hi = w_rt.astype(BF16)
    wr_both = jnp.concatenate([wr_hi, (w_rt - wr_hi.astype(F32)).astype(BF16)], axis=1)
    b_rt = jnp.concatenate([b_group, b_router, jnp.zeros((LANES - N_GROUPS - N_EXPERTS,), F32)]).reshape(1, LANES)

    np_ = bp * tp
    xp2 = yp.reshape(np_, d)
    tm_p = _pick_tile(tp, 256)
    proj_p = _in_projection(xp2, mp[0], mp[1], norm1_w, w_packed, tm_p, tp)
    tq_p = _pick_tile(tp, 256)
    q_p, kf_p, kb_p, vt_p, qi_p, kif_p, kx_p = _attention_prep(
        proj_p, jnp.arange(tp), tq_p, q_norm_w, k_norm_w, idx_k_norm_w, True)
    oa_p = _dsa_prompt(q_p, qi_p, proj_p, kb_p, vt_p, kx_p, bp, tp)
    tt_p = _pick_tile(tp, 256)
    qn_p, kn_p, vv_p, bg_p = _delta_prep(proj_p, jnp.zeros((bp, SUBLANES, CONV_CHANNELS), F32), conv_w,
                                         a_log, dt_bias, bp, tp, tt_p, tp)
    od_p, ssm_p = _delta_chunks(qn_p, kn_p, vv_p, bg_p, proj_p, C_ZD // DELTA_WIDTH,
                                jnp.zeros((bp, N_DELTA_HEADS, HEAD_DIM, HEAD_DIM), F32), o_norm_w, bp, tp)
    tm_o = _pick_tile(tp, 256)
    x1_p, h2_p, lg_p = _out_projection(oa_p, od_p, xp2, mp[2], mp[3], mp[4], norm2_w, w_out_bf, wr_both, b_rt,
                                       tm_o, tp)

    ns_ = bs * rows
    xs2 = jnp.pad(ys, ((0, 0), (0, rows - ts), (0, 0))).reshape(ns_, d)
    tm_s = _pick_tile(ns_, 256)
    proj_s = _in_projection(xs2, ms[0].reshape(ns_ // tm_s, tm_s, d), ms[1].reshape(ns_ // tm_s, tm_s, d),
                            norm1_w, w_packed, tm_s, tm_s)
    q_s, kf_s, kb_s, vb_s, qi_s, kif_s, _ = _attention_prep(
        proj_s, past + jnp.arange(rows), rows, q_norm_w, k_norm_w, idx_k_norm_w, False)
    q_t = qi_s.reshape(bs, rows, N_IDX_HEADS, IDX_DIM).transpose(0, 2, 1, 3).reshape(bs, N_IDX_HEADS * rows, IDX_DIM)
    w_col = proj_s[:, C_MISC + M_WI:C_MISC + M_WI + N_IDX_HEADS].reshape(bs, rows, N_IDX_HEADS)
    w_col = w_col.transpose(0, 2, 1).reshape(bs, N_IDX_HEADS * rows, 1)
    pages = _pick_tile(page_table.shape[1], SAMPLE_PAGES_PER_STEP, 1)
    keys_past, keys_new = _sample_scores(page_table, q_t, w_col, kif_s, cache_idx, layer, pages, ts)
    n_sel_s = min(TOPK_MAX, (past + ts) // 4)
    oa_s = _sample_attend(page_table, keys_past, keys_new, q_s, kb_s, vb_s, cache_k, cache_v, layer, pages, n_sel_s)
    prev8 = jnp.pad(state_conv, ((0, 0), (SUBLANES - (CONV_WIDTH - 1), 0), (0, 0)))
    qn_s, kn_s, vv_s, bg_s = _delta_prep(proj_s, prev8, conv_w, a_log, dt_bias, bs, rows, rows, ts)
    to_chunk = lambda a: jnp.pad(a.reshape(bs, rows, -1), ((0, 0), (0, DELTA_CHUNK - rows), (0, 0))).reshape(
        bs * DELTA_CHUNK, -1)
    z_s = proj_s[:, C_ZD:C_ZD + DELTA_WIDTH]
    od_s, ssm_s = _delta_chunks(to_chunk(qn_s), to_chunk(kn_s), to_chunk(vv_s), to_chunk(bg_s), to_chunk(z_s), 0,
                                state_ssm, o_norm_w, bs, DELTA_CHUNK)
    od_s = od_s.reshape(bs, DELTA_CHUNK, DELTA_WIDTH)[:, :rows].reshape(ns_, DELTA_WIDTH)
    x1_s, h2_s, lg_s = _out_projection(oa_s, od_s, xs2, ms[2], ms[3], ms[4], norm2_w, w_out_bf, wr_both, b_rt,
                                       ns_, ns_)

    n_all = np_ + ns_
    h2_all = jnp.concatenate([h2_p, h2_s], axis=0)
    lg_all = jnp.concatenate([lg_p, lg_s], axis=0)
    eid, gates = _route(lg_all, _pick_tile(n_all, 512, SUBLANES))
    bm = 256
    tok_sorted, block_j0, dest, block_exp, n_active = _route_and_sort(eid[:, 0:2], bm)
    yb = _moe_experts(tok_sorted, block_j0, block_exp, n_active, h2_all, w_gate, w_up, w_down, bm)
    out_p = _combine(x1_p, dest[:np_], yb, gates[:np_], mp[5], tm_o, tp)
    out_s = _combine(x1_s, dest[np_:], yb, gates[np_:], ms[5].reshape(ns_ // tm_s, tm_s, d), tm_s, tm_s)

    valid = lambda a: a.reshape(bs, rows, -1)[:, :ts]
    conv_p = proj_p.reshape(bp, tp, PROJ_PACKED)[:, tp - (CONV_WIDTH - 1):, C_CONV:C_CONV + CONV_CHANNELS]
    conv_s = proj_s.reshape(bs, rows, PROJ_PACKED)[:, ts - (CONV_WIDTH - 1):ts, C_CONV:C_CONV + CONV_CHANNELS]
    return (out_p.reshape(bp, tp, d), valid(out_s),
            kf_p.reshape(bp, tp, N_KV_HEADS, HEAD_DIM),
            proj_p[:, C_VA:C_VA + KV_WIDTH].reshape(bp, tp, N_KV_HEADS, HEAD_DIM),
            kif_p.reshape(bp, tp, IDX_DIM), ssm_p, conv_p,
            valid(kf_s).reshape(bs, ts, N_KV_HEADS, HEAD_DIM),
            valid(proj_s[:, C_VA:C_VA + KV_WIDTH]).reshape(bs, ts, N_KV_HEADS, HEAD_DIM),
            valid(kif_s), ssm_s, conv_s)


def kernel(x_prompt, x_sample, cache_k, cache_v, cache_idx_k, state_ssm, state_conv, page_table, c_prompt, c_sample,
           w_in, w_out, conv_w, a_log, dt_bias, q_norm_w, k_norm_w, idx_k_norm_w, o_norm_w, norm1_w, norm2_w,
           w_ada, b_ada, w_group, b_group, w_router, b_router, w_gate, w_up, w_down):
    depth = w_in.shape[0]
    yp, ys = x_prompt, x_sample
    per_layer = []
    for l in range(depth):
        res = _layer(l, yp, ys, cache_k, cache_v, cache_idx_k, state_ssm[l], state_conv[l], page_table,
                     c_prompt, c_sample, w_in[l], w_out[l], conv_w[l], a_log[l], dt_bias[l], q_norm_w[l],
                     k_norm_w[l], idx_k_norm_w[l], o_norm_w[l], norm1_w[l], norm2_w[l], w_ada[l], b_ada[l],
                     w_group[l], b_group[l], w_router[l], b_router[l], w_gate[l], w_up[l], w_down[l])
        yp, ys = res[0], res[1]
        per_layer.append(res[2:])
    stacked = tuple(jnp.stack([pl_[j] for pl_ in per_layer]) for j in range(10))
    return (yp, ys) + stacked
```

```python
import functools

import jax
import jax.numpy as jnp
import numpy as np
from jax import lax
from jax.experimental import pallas as pl
from jax.experimental.pallas import tpu as pltpu

F32 = jnp.float32
BF16 = jnp.bfloat16
I32 = jnp.int32

HEAD_DIM = 128
N_ATTN_HEADS = 8
N_KV_HEADS = 2
KV_GROUP = N_ATTN_HEADS // N_KV_HEADS
N_DELTA_HEADS = 8
N_IDX_HEADS = 16
IDX_DIM = 64
ATTN_WIDTH = N_ATTN_HEADS * HEAD_DIM
KV_WIDTH = N_KV_HEADS * HEAD_DIM
DELTA_WIDTH = N_DELTA_HEADS * HEAD_DIM
IDX_WIDTH = N_IDX_HEADS * IDX_DIM
CONV_CHANNELS = 3 * DELTA_WIDTH
TOPK_MAX = 256
ROPE_THETA = 500000.0
ROPE_FRACTION = 4
CONV_WIDTH = 4
DELTA_CHUNK = 64
N_GROUPS = 8
EXPERTS_PER_GROUP = 8
N_EXPERTS = N_GROUPS * EXPERTS_PER_GROUP
N_MOD = 6
EPS = 1e-6
PAGE_SIZE = 128
PROJ_SIZES = (ATTN_WIDTH, KV_WIDTH, KV_WIDTH, IDX_WIDTH, IDX_DIM, N_IDX_HEADS,
              DELTA_WIDTH, DELTA_WIDTH, DELTA_WIDTH, DELTA_WIDTH, N_DELTA_HEADS, N_DELTA_HEADS)

LANES = 128
SUBLANES = 8
VMEM_LIMIT = 56 * 1024 * 1024
FRONT_VMEM_LIMIT = 60 * 1024 * 1024

C_QA = 0
C_QI = 1024
C_ZD = 2048
C_CONV = 3072
C_KA = 6144
C_VA = 6400
C_MISC = 6656
PROJ_PACKED = 6912
M_KI = 0
M_WI = 64
M_BD = 80
M_AD = 88

Q_TILE = 128
KEY_CHUNK = 256
SEL_SPAN = 512
NEG_INF_KEY = -2139095041
SAMPLE_ROWS = 16
SAMPLE_PAGES_PER_STEP = 16
NEG_BIG = -1e30
INT_MIN = -2147483648
INT_MAX = 2147483647


def _cparams(sem):
    return pltpu.CompilerParams(dimension_semantics=sem, vmem_limit_bytes=VMEM_LIMIT)


def _dot(a, b):
    return jnp.dot(a, b, preferred_element_type=F32)


def _dot_nt(a, b):
    return lax.dot_general(a, b, (((1,), (1,)), ((), ())), preferred_element_type=F32)


def _dot_tn(a, b):
    return lax.dot_general(a, b, (((0,), (0,)), ((), ())), preferred_element_type=F32)


def _sigmoid(x):
    return 0.5 * jnp.tanh(0.5 * x) + 0.5


def _silu(x):
    return x * _sigmoid(x)


def _softplus(x):
    return jnp.maximum(x, 0.0) + jnp.log(1.0 + jnp.exp(-jnp.abs(x)))


def _ada_kernel(c_ref, w_ref, b_ref, o_ref):
    s = _silu(c_ref[...]).astype(BF16)
    o_ref[...] = _dot(s, w_ref[...].astype(BF16)) + b_ref[...]


def _ada_modulation(c, w_ada, b_ada):
    r, d = c.shape
    n = w_ada.shape[1]
    tn = 1024 if n % 1024 == 0 else n
    return pl.pallas_call(
        _ada_kernel,
        name="ada_mod",
        grid=(n // tn,),
        in_specs=[pl.BlockSpec((r, d), lambda j: (0, 0)),
                  pl.BlockSpec((d, tn), lambda j: (0, j)),
                  pl.BlockSpec((1, tn), lambda j: (0, j))],
        out_specs=pl.BlockSpec((r, tn), lambda j: (0, j)),
        out_shape=jax.ShapeDtypeStruct((r, n), F32),
        compiler_params=_cparams(("parallel",)),
    )(c, w_ada, b_ada.reshape(1, n))


INPROJ_COLS = 1152


def _inproj_kernel(x_ref, sh_ref, sc_ref, nw_ref, w_ref, o_ref):
    x = x_ref[...]
    y = x * lax.rsqrt(jnp.mean(x * x, axis=-1, keepdims=True) + EPS) * nw_ref[...]
    h = (y * (1.0 + sc_ref[0]) + sh_ref[0]).astype(BF16)
    for c0 in range(0, o_ref.shape[1], INPROJ_COLS):
        o_ref[:, c0:c0 + INPROJ_COLS] = _dot(h, w_ref[:, c0:c0 + INPROJ_COLS])


def _in_projection(x2d, shift, scale, norm_w, w_packed, tm, rows_per_mod_block):
    n, d = x2d.shape
    np_ = w_packed.shape[1]
    r = shift.shape[1]
    tiles_per_mod = rows_per_mod_block // tm
    mod_spec = pl.BlockSpec((1, r, d), lambda i: (i // tiles_per_mod, 0, 0))
    return pl.pallas_call(
        _inproj_kernel,
        name="in_proj",
        grid=(n // tm,),
        in_specs=[pl.BlockSpec((tm, d), lambda i: (i, 0)),
                  mod_spec, mod_spec,
                  pl.BlockSpec((1, d), lambda i: (0, 0)),
                  pl.BlockSpec((d, np_), lambda i: (0, 0), pipeline_mode=pl.Buffered(1))],
        out_specs=pl.BlockSpec((tm, np_), lambda i: (i, 0)),
        out_shape=jax.ShapeDtypeStruct((n, np_), F32),
        compiler_params=_cparams(("parallel",)),
    )(x2d, shift, scale, norm_w.reshape(1, d), w_packed)


def _rope(x, tab, rot):
    c = tab[:, 0:LANES]
    s1 = tab[:, LANES:2 * LANES]
    s2 = tab[:, 2 * LANES:3 * LANES]
    return x * c + pltpu.roll(x, LANES - rot, 1) * s1 + pltpu.roll(x, rot, 1) * s2


def _rms_head(x, w):
    return x * lax.rsqrt(jnp.mean(x * x, axis=-1, keepdims=True) + EPS) * w


def _prep_kernel(qa_ref, qi_ref, ka_ref, va_ref, misc_ref, tabm_ref, tabi_ref, qw_ref, kw_ref, iw_ref,
                 q_ref, kf_ref, kb_ref, vb_ref, qib_ref, kif_ref, kib_ref, *, transpose_v):
    tabm = tabm_ref[0]
    tabi = tabi_ref[0]
    half_main = HEAD_DIM // ROPE_FRACTION // 2
    half_idx = IDX_DIM // ROPE_FRACTION // 2
    for h in range(N_ATTN_HEADS):
        sl = slice(h * HEAD_DIM, (h + 1) * HEAD_DIM)
        y = _rope(_rms_head(qa_ref[:, sl], qw_ref[...]), tabm, half_main)
        q_ref[:, sl] = (y * (HEAD_DIM ** -0.5)).astype(BF16)
    for h in range(N_KV_HEADS):
        sl = slice(h * HEAD_DIM, (h + 1) * HEAD_DIM)
        y = _rope(_rms_head(ka_ref[:, sl], kw_ref[...]), tabm, half_main)
        kf_ref[:, sl] = y
        kb_ref[:, sl] = y.astype(BF16)
    if transpose_v:
        vb_ref[...] = va_ref[...].T.astype(BF16)
    else:
        vb_ref[...] = va_ref[...].astype(BF16)
    for p in range(IDX_WIDTH // LANES):
        sl = slice(p * LANES, (p + 1) * LANES)
        qib_ref[:, sl] = _rope(qi_ref[:, sl], tabi, half_idx).astype(BF16)
    m = misc_ref[...]
    lane = lax.broadcasted_iota(I32, m.shape, 1)
    ki = jnp.where(lane < IDX_DIM, m, 0.0)
    ms = jnp.sum(ki * ki, axis=-1, keepdims=True) * (1.0 / IDX_DIM)
    y = _rope(ki * lax.rsqrt(ms + EPS) * iw_ref[...], tabi, half_idx)
    kif_ref[...] = y[:, 0:IDX_DIM]
    kib_ref[...] = (y + pltpu.roll(y, IDX_DIM, 1)).astype(BF16)


def _rope_tables(pos, head_dim, group):
    d_rot = head_dim // ROPE_FRACTION
    half = d_rot // 2
    inv_freq = jnp.power(ROPE_THETA, -(jnp.arange(half, dtype=F32) * 2.0 / d_rot))
    ang = pos.astype(F32)[:, None] * inv_freq[None, :]
    cos = jnp.cos(ang)
    sin = jnp.sin(ang)
    t = pos.shape[0]
    z = jnp.zeros((t, group - d_rot), F32)
    c = jnp.concatenate([cos, cos, jnp.ones((t, group - d_rot), F32)], axis=1)
    s1 = jnp.concatenate([-sin, jnp.zeros((t, half), F32), z], axis=1)
    s2 = jnp.concatenate([jnp.zeros((t, half), F32), sin, z], axis=1)
    rep = LANES // group
    return jnp.concatenate([jnp.tile(c, (1, rep)), jnp.tile(s1, (1, rep)), jnp.tile(s2, (1, rep))], axis=1)


def _attention_prep(proj, pos, tq, q_norm_w, k_norm_w, idx_k_norm_w, transpose_v):
    n = proj.shape[0]
    p = pos.shape[0]
    g = p // tq
    tabm = _rope_tables(pos, HEAD_DIM, LANES).reshape(g, tq, 3 * LANES)
    tabi = _rope_tables(pos, IDX_DIM, IDX_DIM).reshape(g, tq, 3 * LANES)
    iw = jnp.concatenate([idx_k_norm_w, jnp.zeros((LANES - IDX_DIM,), F32)]).reshape(1, LANES)
    row = lambda w, c: pl.BlockSpec((tq, w), lambda i: (i, c // w))
    tab_spec = pl.BlockSpec((1, tq, 3 * LANES), lambda i: (i % g, 0, 0))
    vec_spec = pl.BlockSpec((1, LANES), lambda i: (0, 0))
    out_row = lambda w: pl.BlockSpec((tq, w), lambda i: (i, 0))
    v_spec = pl.BlockSpec((KV_WIDTH, tq), lambda i: (0, i)) if transpose_v else out_row(KV_WIDTH)
    v_shape = (KV_WIDTH, n) if transpose_v else (n, KV_WIDTH)
    return pl.pallas_call(
        functools.partial(_prep_kernel, transpose_v=transpose_v),
        name="attn_prep",
        grid=(n // tq,),
        in_specs=[row(ATTN_WIDTH, C_QA), row(IDX_WIDTH, C_QI), row(KV_WIDTH, C_KA), row(KV_WIDTH, C_VA),
                  row(LANES, C_MISC), tab_spec, tab_spec, vec_spec, vec_spec, vec_spec],
        out_specs=[out_row(ATTN_WIDTH), out_row(KV_WIDTH), out_row(KV_WIDTH), v_spec,
                   out_row(IDX_WIDTH), out_row(IDX_DIM), out_row(LANES)],
        out_shape=[jax.ShapeDtypeStruct((n, ATTN_WIDTH), BF16),
                   jax.ShapeDtypeStruct((n, KV_WIDTH), F32),
                   jax.ShapeDtypeStruct((n, KV_WIDTH), BF16),
                   jax.ShapeDtypeStruct(v_shape, BF16),
                   jax.ShapeDtypeStruct((n, IDX_WIDTH), BF16),
                   jax.ShapeDtypeStruct((n, IDX_DIM), F32),
                   jax.ShapeDtypeStruct((n, LANES), BF16)],
        compiler_params=_cparams(("parallel",)),
    )(proj, proj, proj, proj, proj, tabm, tabi,
      q_norm_w.reshape(1, LANES), k_norm_w.reshape(1, LANES), iw)


def _sort_key(x):
    b = pltpu.bitcast(x + 0.0, I32)
    return b ^ ((b >> 31) & INT_MAX)


def _kth_largest_key(count_ge, k, shape, n_total):
    def body(it, carry):
        ans_u, n_ge = carry
        bit = jnp.left_shift(jnp.int32(1), 31 - it)
        cand_u = ans_u | bit
        cnt = count_ge(cand_u ^ INT_MIN)
        ok = cnt >= k
        return jnp.where(ok, cand_u, ans_u), jnp.where(ok, cnt, n_ge)

    ans_u, n_ge = lax.fori_loop(0, 32, body, (jnp.zeros(shape, I32), jnp.full(shape, float(n_total), F32)))
    return ans_u ^ INT_MIN, n_ge


def _tie_index_limit(count_eq_le, need, n_keys, shape):
    nbits = max(1, int(n_keys - 1).bit_length())

    def body(it, lo):
        bit = jnp.left_shift(jnp.int32(1), nbits - 1 - it)
        cand = lo | bit
        cnt = count_eq_le(cand - 1)
        return jnp.where(cnt >= need, lo, cand)

    return lax.fori_loop(0, nbits, body, jnp.zeros(shape, I32))


def _dsa_prompt_kernel(q_ref, qi_ref, misc_ref, k_ref, vt_ref, kx_ref, o_ref,
                       key_scr, qsel_scr, qg_scr, thr_scr, lim_scr, m_scr, l_scr, acc_scr, *, n_sel):
    i = pl.program_id(1)
    tq = Q_TILE
    ck = KEY_CHUNK
    n_ch = (i * tq + tq + ck - 1) // ck
    q_pos = i * tq + lax.broadcasted_iota(I32, (1, tq), 1)
    row_k = lax.broadcasted_iota(I32, (ck, 1), 0)

    lo_half = lax.broadcasted_iota(I32, (tq, LANES), 1) < IDX_DIM
    zero = jnp.zeros((), BF16)
    for p in range(IDX_WIDTH // LANES):
        slab = qi_ref[:, p * LANES:(p + 1) * LANES]
        qsel_scr[(2 * p) * tq:(2 * p + 1) * tq, :] = jnp.where(lo_half, slab, zero)
        qsel_scr[(2 * p + 1) * tq:(2 * p + 2) * tq, :] = jnp.where(lo_half, zero, slab)
    w_t = misc_ref[...].T

    def score_chunk(c, carry):
        off = pl.multiple_of(c * ck, ck)
        s = _dot_nt(kx_ref[pl.ds(off, ck), :], qsel_scr[...])
        acc = jnp.zeros((ck, tq), F32)
        for h in range(N_IDX_HEADS):
            acc = acc + w_t[M_WI + h:M_WI + h + 1, :] * jnp.maximum(s[:, h * tq:(h + 1) * tq], 0.0)
        acc = jnp.where(off + row_k <= q_pos, acc, -jnp.inf)
        key_scr[pl.ds(off, ck), :] = _sort_key(acc)
        return carry

    lax.fori_loop(0, n_ch, score_chunk, 0)

    spc = SEL_SPAN // ck
    n_span = (n_ch + spc - 1) // spc
    neg_key = jnp.full((ck, tq), NEG_INF_KEY, I32)

    def pad_chunk(c, carry):
        key_scr[pl.ds(pl.multiple_of(c * ck, ck), ck), :] = neg_key
        return carry

    lax.fori_loop(n_ch, n_span * spc, pad_chunk, 0)

    thr_scr[...] = jnp.full((1, tq), INT_MIN, I32)
    lim_scr[...] = jnp.full((1, tq), INT_MAX, I32)

    def select_threshold(n_keys):
        def count_where(pred):
            tot = jnp.zeros((SUBLANES, tq), F32)
            for c0 in range(0, n_keys, ck):
                hit = pred(key_scr[c0:c0 + ck, :], c0 + row_k).astype(F32)
                tot = tot + jnp.sum(hit.reshape(ck // SUBLANES, SUBLANES, tq), axis=0)
            return jnp.sum(tot, axis=0, keepdims=True)

        t, n_ge = _kth_largest_key(lambda cand: count_where(lambda kk, pos: kk >= cand), float(n_sel), (1, tq),
                                   n_keys)
        thr_scr[...] = t

        @pl.when(jnp.max(n_ge) > float(n_sel))
        def _():
            n_gt = count_where(lambda kk, pos: kk > t)
            lim_scr[...] = _tie_index_limit(
                lambda idx: count_where(lambda kk, pos: jnp.logical_and(kk == t, pos <= idx)),
                float(n_sel) - n_gt, k_ref.shape[0], (1, tq))

    for spans in range(1, k_ref.shape[0] // SEL_SPAN + 1):
        if spans * SEL_SPAN > n_sel:
            pl.when(jnp.logical_and(n_span == spans, (i + 1) * tq > n_sel))(
                functools.partial(select_threshold, spans * SEL_SPAN))

    thr = thr_scr[...]
    lim = lim_scr[...]

    for g in range(N_KV_HEADS):
        for r in range(KV_GROUP):
            h = g * KV_GROUP + r
            qg_scr[g, r * tq:(r + 1) * tq, :] = q_ref[:, h * HEAD_DIM:(h + 1) * HEAD_DIM]
    m_scr[...] = jnp.full(m_scr.shape, NEG_BIG, F32)
    l_scr[...] = jnp.zeros(l_scr.shape, F32)
    acc_scr[...] = jnp.zeros(acc_scr.shape, F32)

    def attend_chunk(c, carry):
        off = pl.multiple_of(c * ck, ck)
        kk = key_scr[pl.ds(off, ck), :]
        pos = off + row_k
        sel = jnp.logical_or(kk > thr, jnp.logical_and(kk == thr, pos <= lim))
        sel = jnp.logical_and(sel, pos <= q_pos)
        def group_steps(g):
            kc = k_ref[pl.ds(off, ck), g * HEAD_DIM:(g + 1) * HEAD_DIM]
            vt = vt_ref[g * HEAD_DIM:(g + 1) * HEAD_DIM, pl.ds(off, ck)]
            qk = _dot_nt(kc, qg_scr[g])
            yield
            s = jnp.concatenate([jnp.where(sel, qk[:, r * tq:(r + 1) * tq], NEG_BIG) for r in range(KV_GROUP)],
                                axis=1)
            m_old = m_scr[g]
            m_new = jnp.maximum(m_old, jnp.max(s, axis=0, keepdims=True))
            yield
            p = jnp.exp(s - m_new)
            alpha = jnp.exp(m_old - m_new)
            l_scr[g] = alpha * l_scr[g] + jnp.sum(p, axis=0, keepdims=True)
            yield
            acc_scr[g] = alpha * acc_scr[g] + _dot(vt, p.astype(BF16))
            m_scr[g] = m_new

        chains = [group_steps(g) for g in range(N_KV_HEADS)]
        while chains:
            chains = [ch for ch in chains if next(ch, "done") != "done"]
        return carry

    lax.fori_loop(0, n_ch, attend_chunk, 0)
    for g in range(N_KV_HEADS):
        o_t = acc_scr[g] / l_scr[g]
        for r in range(KV_GROUP):
            h = g * KV_GROUP + r
            o_ref[:, h * HEAD_DIM:(h + 1) * HEAD_DIM] = o_t[:, r * tq:(r + 1) * tq].T.astype(BF16)


def _dsa_prompt(q_bf, qi_bf, misc, k_bf, vt_bf, kx_bf, b, t):
    n = b * t
    nq = t // Q_TILE
    n_sel = min(TOPK_MAX, t // 4)
    qrow = lambda w: pl.BlockSpec((Q_TILE, w), lambda bb, i: (bb * nq + i, 0))
    seq = lambda w: pl.BlockSpec((t, w), lambda bb, i: (bb, 0))
    return pl.pallas_call(
        functools.partial(_dsa_prompt_kernel, n_sel=n_sel),
        name="dsa_prompt",
        grid=(b, nq),
        in_specs=[qrow(ATTN_WIDTH), qrow(IDX_WIDTH),
                  qrow(LANES),
                  seq(KV_WIDTH), pl.BlockSpec((KV_WIDTH, t), lambda bb, i: (0, bb)), seq(LANES)],
        out_specs=qrow(ATTN_WIDTH),
        out_shape=jax.ShapeDtypeStruct((n, ATTN_WIDTH), BF16),
        scratch_shapes=[pltpu.VMEM((t, Q_TILE), I32),
                        pltpu.VMEM((N_IDX_HEADS * Q_TILE, LANES), BF16),
                        pltpu.VMEM((N_KV_HEADS, KV_GROUP * Q_TILE, HEAD_DIM), BF16),
                        pltpu.VMEM((1, Q_TILE), I32),
                        pltpu.VMEM((1, Q_TILE), I32),
                        pltpu.VMEM((N_KV_HEADS, 1, KV_GROUP * Q_TILE), F32),
                        pltpu.VMEM((N_KV_HEADS, 1, KV_GROUP * Q_TILE), F32),
                        pltpu.VMEM((N_KV_HEADS, HEAD_DIM, KV_GROUP * Q_TILE), F32)],
        compiler_params=_cparams(("parallel", "arbitrary")),
    )(q_bf, qi_bf, misc, k_bf, vt_bf, kx_bf)


def _sample_score_kernel(pt_ref, q_ref, w_ref, kn_ref, *refs, pages, t_valid):
    page_refs = refs[:pages]
    past_ref, new_ref = refs[pages], refs[pages + 1]
    rows = SAMPLE_ROWS
    hr = N_IDX_HEADS * rows
    half = PAGE_SIZE // 2
    q2 = q_ref[0]
    w = w_ref[0]

    def head_sum(s):
        s = w * jnp.maximum(s, 0.0)
        acc = s[0:rows]
        for h in range(1, N_IDX_HEADS):
            acc = acc + s[h * rows:(h + 1) * rows]
        return acc

    for j in range(pages):
        s = _dot_nt(q2, page_refs[j][0, 0].astype(BF16))
        past_ref[0, :, j * PAGE_SIZE:j * PAGE_SIZE + half] = _sort_key(head_sum(s[0:hr]))
        past_ref[0, :, j * PAGE_SIZE + half:(j + 1) * PAGE_SIZE] = _sort_key(head_sum(s[hr:2 * hr]))

    @pl.when(pl.program_id(1) == 0)
    def _():
        kn = jnp.concatenate([kn_ref[...], jnp.zeros((LANES - rows, IDX_DIM), F32)], axis=0).astype(BF16)
        sc = head_sum(_dot_nt(q2[0:hr, 0:IDX_DIM], kn))
        t = lax.broadcasted_iota(I32, sc.shape, 0)
        s = lax.broadcasted_iota(I32, sc.shape, 1)
        ok = jnp.logical_and(s <= t, s < t_valid)
        new_ref[0] = _sort_key(jnp.where(ok, sc, -jnp.inf))


def _page_order_pos(lane_pos):
    p = lane_pos & (PAGE_SIZE - 1)
    half = PAGE_SIZE // 2
    return (lane_pos - p) + 2 * (p & (half - 1)) + (p // half)


def _sample_scores(page_table, q_t, w_col, kif, cache_idx, layer, pages, t_valid):
    bs, n_pages = page_table.shape
    past = n_pages * PAGE_SIZE
    hr = N_IDX_HEADS * SAMPLE_ROWS
    depth, pool = cache_idx.shape[0], cache_idx.shape[1]
    cache_idx = cache_idx.reshape(depth, pool, PAGE_SIZE // 2, 2 * IDX_DIM)
    zq = jnp.zeros_like(q_t)
    q_t = jnp.concatenate([jnp.concatenate([q_t, zq], axis=2), jnp.concatenate([zq, q_t], axis=2)], axis=1)
    page_spec = lambda j: pl.BlockSpec((1, 1, PAGE_SIZE // 2, 2 * IDX_DIM),
                                       lambda b, c, pt: (layer, pt[b, c * pages + j], 0, 0))
    grid_spec = pltpu.PrefetchScalarGridSpec(
        num_scalar_prefetch=1,
        grid=(bs, n_pages // pages),
        in_specs=[pl.BlockSpec((1, 2 * hr, 2 * IDX_DIM), lambda b, c, pt: (b, 0, 0)),
                  pl.BlockSpec((1, hr, 1), lambda b, c, pt: (b, 0, 0)),
                  pl.BlockSpec((SAMPLE_ROWS, IDX_DIM), lambda b, c, pt: (b, 0))]
                 + [page_spec(j) for j in range(pages)],
        out_specs=[pl.BlockSpec((1, SAMPLE_ROWS, pages * PAGE_SIZE), lambda b, c, pt: (b, 0, c)),
                   pl.BlockSpec((1, SAMPLE_ROWS, LANES), lambda b, c, pt: (b, 0, 0))],
    )
    return pl.pallas_call(
        functools.partial(_sample_score_kernel, pages=pages, t_valid=t_valid),
        name="sample_scores",
        grid_spec=grid_spec,
        out_shape=[jax.ShapeDtypeStruct((bs, SAMPLE_ROWS, past), I32),
                   jax.ShapeDtypeStruct((bs, SAMPLE_ROWS, LANES), I32)],
        compiler_params=_cparams(("parallel", "arbitrary")),
    )(page_table, q_t, w_col, kif, *([cache_idx] * pages))


def _sample_attend_kernel(pt_ref, kp_ref, kn_ref, q_ref, knew_ref, vnew_ref, *refs, pages, n_sel, past):
    k_pages = refs[:pages]
    v_pages = refs[pages:2 * pages]
    o_ref = refs[2 * pages]
    thr_scr, lim_scr, m_scr, l_scr, acc_scr = refs[2 * pages + 1:]
    c = pl.program_id(1)
    rows = SAMPLE_ROWS
    span = pages * PAGE_SIZE

    @pl.when(c == 0)
    def _():
        m_scr[...] = jnp.full(m_scr.shape, NEG_BIG, F32)
        l_scr[...] = jnp.zeros(l_scr.shape, F32)
        acc_scr[...] = jnp.zeros(acc_scr.shape, F32)
        kp = kp_ref[0]
        kn = kn_ref[0]
        pos_p = _page_order_pos(lax.broadcasted_iota(I32, kp.shape, 1))
        pos_n = past + lax.broadcasted_iota(I32, kn.shape, 1)

        def count_where(pred):
            return (jnp.sum(pred(kp, pos_p).astype(F32), axis=1, keepdims=True)
                    + jnp.sum(pred(kn, pos_n).astype(F32), axis=1, keepdims=True))

        t, n_ge = _kth_largest_key(lambda cand: count_where(lambda kk, pos: kk >= cand), float(n_sel), (rows, 1),
                                   past + LANES)
        thr_scr[...] = t
        lim_scr[...] = jnp.full((rows, 1), INT_MAX, I32)

        @pl.when(jnp.max(n_ge) > float(n_sel))
        def _():
            n_gt = count_where(lambda kk, pos: kk > t)
            lim_scr[...] = _tie_index_limit(
                lambda idx: count_where(lambda kk, pos: jnp.logical_and(kk == t, pos <= idx)),
                float(n_sel) - n_gt, past + LANES, (rows, 1))

    thr = thr_scr[...]
    lim = lim_scr[...]

    def update_steps(g, qg, k_fn, v_fn, sel):
        sel = jnp.concatenate([sel] * KV_GROUP, axis=0)
        qk = _dot_nt(qg, k_fn())
        yield
        s = jnp.where(sel, qk, NEG_BIG)
        m_old = m_scr[g]
        m_new = jnp.maximum(m_old, jnp.max(s, axis=1, keepdims=True))
        yield
        p = jnp.where(sel, jnp.exp(s - m_new), 0.0)
        alpha = jnp.exp(m_old - m_new)
        l_scr[g] = alpha * l_scr[g] + jnp.sum(p, axis=1, keepdims=True)
        yield
        acc_scr[g] = alpha * acc_scr[g] + _dot(p.astype(BF16), v_fn())
        m_scr[g] = m_new

    def run_lockstep(chains):
        while chains:
            chains = [ch for ch in chains if next(ch, "done") != "done"]

    def select(kk, pos):
        return jnp.logical_or(kk > thr, jnp.logical_and(kk == thr, pos <= lim))

    def page_cat(page_refs, g):
        half = PAGE_SIZE // 2
        parts = [r[0, 0, pl.ds(par * N_KV_HEADS + g, half, stride=2 * N_KV_HEADS), :]
                 for r in page_refs for par in range(2)]
        return jnp.concatenate(parts, axis=0).astype(BF16)

    off = pl.multiple_of(c * span, span)
    kk = kp_ref[0, :, pl.ds(off, span)]
    sel_past = select(kk, _page_order_pos(off + lax.broadcasted_iota(I32, kk.shape, 1)))
    q_groups = []
    for g in range(N_KV_HEADS):
        qg = jnp.concatenate(
            [q_ref[:, (g * KV_GROUP + r) * HEAD_DIM:(g * KV_GROUP + r + 1) * HEAD_DIM] for r in range(KV_GROUP)],
            axis=0)
        q_groups.append(qg)
    run_lockstep([update_steps(g, q_groups[g], functools.partial(page_cat, k_pages, g),
                               functools.partial(page_cat, v_pages, g), sel_past) for g in range(N_KV_HEADS)])

    @pl.when(c == pl.num_programs(1) - 1)
    def _():
        kn = kn_ref[0]
        lane = lax.broadcasted_iota(I32, kn.shape, 1)
        sel_new = jnp.logical_and(select(kn, past + lane), lane < rows)
        pad = jnp.zeros((LANES - rows, KV_WIDTH), BF16)
        k_new = jnp.concatenate([knew_ref[...], pad], axis=0)
        v_new = jnp.concatenate([vnew_ref[...], pad], axis=0)
        head = lambda a, g: (lambda: a[:, g * HEAD_DIM:(g + 1) * HEAD_DIM])
        run_lockstep([update_steps(g, q_groups[g], head(k_new, g), head(v_new, g), sel_new)
                      for g in range(N_KV_HEADS)])
        for g in range(N_KV_HEADS):
            o = acc_scr[g] / l_scr[g]
            for r in range(KV_GROUP):
                h = g * KV_GROUP + r
                o_ref[:, h * HEAD_DIM:(h + 1) * HEAD_DIM] = o[r * rows:(r + 1) * rows].astype(BF16)


def _sample_attend(page_table, keys_past, keys_new, q_bf, k_bf, v_bf, cache_k, cache_v, layer, pages, n_sel):
    bs, n_pages = page_table.shape
    past = n_pages * PAGE_SIZE
    depth, pool = cache_k.shape[0], cache_k.shape[1]
    cache_k = cache_k.reshape(depth, pool, PAGE_SIZE * N_KV_HEADS, HEAD_DIM)
    cache_v = cache_v.reshape(depth, pool, PAGE_SIZE * N_KV_HEADS, HEAD_DIM)
    page_spec = lambda j: pl.BlockSpec((1, 1, PAGE_SIZE * N_KV_HEADS, HEAD_DIM),
                                       lambda b, c, pt: (layer, pt[b, c * pages + j], 0, 0))
    row = lambda w: pl.BlockSpec((SAMPLE_ROWS, w), lambda b, c, pt: (b, 0))
    grid_spec = pltpu.PrefetchScalarGridSpec(
        num_scalar_prefetch=1,
        grid=(bs, n_pages // pages),
        in_specs=[pl.BlockSpec((1, SAMPLE_ROWS, past), lambda b, c, pt: (b, 0, 0)),
                  pl.BlockSpec((1, SAMPLE_ROWS, LANES), lambda b, c, pt: (b, 0, 0)),
                  row(ATTN_WIDTH), row(KV_WIDTH), row(KV_WIDTH)]
                 + [page_spec(j) for j in range(pages)] * 2,
        out_specs=row(ATTN_WIDTH),
        scratch_shapes=[pltpu.VMEM((SAMPLE_ROWS, 1), I32),
                        pltpu.VMEM((SAMPLE_ROWS, 1), I32),
                        pltpu.VMEM((N_KV_HEADS, KV_GROUP * SAMPLE_ROWS, 1), F32),
                        pltpu.VMEM((N_KV_HEADS, KV_GROUP * SAMPLE_ROWS, 1), F32),
                        pltpu.VMEM((N_KV_HEADS, KV_GROUP * SAMPLE_ROWS, HEAD_DIM), F32)],
    )
    return pl.pallas_call(
        functools.partial(_sample_attend_kernel, pages=pages, n_sel=n_sel, past=past),
        name="sample_attend",
        grid_spec=grid_spec,
        out_shape=jax.ShapeDtypeStruct((bs * SAMPLE_ROWS, ATTN_WIDTH), BF16),
        compiler_params=_cparams(("parallel", "arbitrary")),
    )(page_table, keys_past, keys_new, q_bf, k_bf, v_bf, *([cache_k] * pages), *([cache_v] * pages))


def _delta_prep_kernel(x_ref, halo_ref, prev_ref, misc_ref, cw_ref, al_ref, dt_ref,
                       qn_ref, kn_ref, vv_ref, bg_ref, xp_scr, *, tiles_per_seq, t_valid, tt):
    i = pl.program_id(0)
    tile_in_seq = i % tiles_per_seq
    halo = jnp.where(tile_in_seq == 0, prev_ref[0], halo_ref[...])
    xp_scr[0:SUBLANES, :] = halo
    xp_scr[SUBLANES:SUBLANES + tt, :] = x_ref[...]
    base = SUBLANES - (CONV_WIDTH - 1)
    outs = (qn_ref, kn_ref, vv_ref)
    for sec in range(3):
        for h in range(N_DELTA_HEADS):
            col = sec * DELTA_WIDTH + h * HEAD_DIM
            sl = slice(col, col + HEAD_DIM)
            y = xp_scr[base:base + tt, sl] * cw_ref[0:1, sl]
            for j in range(1, CONV_WIDTH):
                y = y + xp_scr[base + j:base + j + tt, sl] * cw_ref[j:j + 1, sl]
            y = _silu(y)
            if sec < 2:
                y = y * lax.rsqrt(jnp.sum(y * y, axis=-1, keepdims=True) + EPS)
            if sec == 0:
                y = y * (HEAD_DIM ** -0.5)
            outs[sec][:, h * HEAD_DIM:(h + 1) * HEAD_DIM] = y
    m = misc_ref[...]
    lane = lax.broadcasted_iota(I32, m.shape, 1)
    row = tile_in_seq * tt + lax.broadcasted_iota(I32, m.shape, 0)
    beta = _sigmoid(m)
    g = -jnp.exp(al_ref[...]) * _softplus(m + dt_ref[...])
    is_b = jnp.logical_and(lane >= M_BD, lane < M_BD + N_DELTA_HEADS)
    is_g = jnp.logical_and(lane >= M_AD, lane < M_AD + N_DELTA_HEADS)
    comb = jnp.where(is_b, beta, jnp.where(is_g, g, 0.0))
    comb = jnp.where(row < t_valid, comb, 0.0)
    bg_ref[...] = pltpu.roll(comb, LANES - M_BD, 1)


def _delta_prep(proj, prev8, conv_w, a_log, dt_bias, b, t, tt, t_valid):
    n = proj.shape[0]
    tiles_per_seq = t // tt
    pad_vec = lambda v: jnp.zeros((1, LANES), F32).at[0, M_AD:M_AD + N_DELTA_HEADS].set(v)
    halo_blocks = tt // SUBLANES
    return pl.pallas_call(
        functools.partial(_delta_prep_kernel, tiles_per_seq=tiles_per_seq, t_valid=t_valid, tt=tt),
        name="delta_prep",
        grid=(n // tt,),
        in_specs=[pl.BlockSpec((tt, CONV_CHANNELS), lambda i: (i, C_CONV // CONV_CHANNELS)),
                  pl.BlockSpec((SUBLANES, CONV_CHANNELS),
                               lambda i: (jnp.maximum(i * halo_blocks - 1, 0), C_CONV // CONV_CHANNELS)),
                  pl.BlockSpec((1, SUBLANES, CONV_CHANNELS), lambda i: (i // tiles_per_seq, 0, 0)),
                  pl.BlockSpec((tt, LANES), lambda i: (i, C_MISC // LANES)),
                  pl.BlockSpec((CONV_WIDTH, CONV_CHANNELS), lambda i: (0, 0)),
                  pl.BlockSpec((1, LANES), lambda i: (0, 0)),
                  pl.BlockSpec((1, LANES), lambda i: (0, 0))],
        out_specs=[pl.BlockSpec((tt, DELTA_WIDTH), lambda i: (i, 0))] * 3
                  + [pl.BlockSpec((tt, LANES), lambda i: (i, 0))],
        out_shape=[jax.ShapeDtypeStruct((n, DELTA_WIDTH), F32)] * 3 + [jax.ShapeDtypeStruct((n, LANES), F32)],
        scratch_shapes=[pltpu.VMEM((SUBLANES + tt, CONV_CHANNELS), F32)],
        compiler_params=_cparams(("parallel",)),
    )(proj, proj, prev8, proj, conv_w, pad_vec(a_log), pad_vec(dt_bias))


def _mm(a, b):
    return _dot(a.astype(BF16), b.astype(BF16))


def _mm_nt(a, b):
    return _dot_nt(a.astype(BF16), b.astype(BF16))


DELTA_INV_BLOCK = 16
DELTA_STACK = 4
DELTA_CHUNKS_PER_STEP = 2


def _delta_chunk_kernel(qn_ref, kn_ref, vv_ref, bg_ref, z_ref, s0_ref, ow_ref, od_ref, so_ref, s_scr, *, n_chunks):
    c = pl.program_id(1)
    cs = DELTA_CHUNK

    @pl.when(c == 0)
    def _():
        s_scr[...] = s0_ref[0]

    ltri = (lax.broadcasted_iota(I32, (cs, cs), 0) >= lax.broadcasted_iota(I32, (cs, cs), 1)).astype(BF16)

    def chunk_gates(ch):
        bg = bg_ref[ch * cs:(ch + 1) * cs, :]
        g1 = bg.astype(BF16)
        r1 = bg - g1.astype(F32)
        g2 = r1.astype(BF16)
        g3 = (r1 - g2.astype(F32)).astype(BF16)
        gc = _dot(ltri, g1) + _dot(ltri, g2) + _dot(ltri, g3)
        return bg, gc, gc.T

    gates = [chunk_gates(ch) for ch in range(n_chunks)]
    state_ready = {}

    gh = DELTA_STACK
    rows = gh * cs
    rr = lax.broadcasted_iota(I32, (rows, rows), 0)
    cc = lax.broadcasted_iota(I32, (rows, rows), 1)
    same = (rr // cs) == (cc // cs)
    causal = jnp.logical_and(same, rr >= cc)
    strict = jnp.logical_and(same, rr > cc)
    eye = (rr == cc).astype(F32)
    row_head = lax.broadcasted_iota(I32, (rows, 1), 0) // cs
    def group_steps(ch, grp):
        heads = [grp * gh + j for j in range(gh)]
        bg, gc, gct = gates[ch]
        r0 = ch * cs
        stack = lambda ref: jnp.concatenate([ref[r0:r0 + cs, h * HEAD_DIM:(h + 1) * HEAD_DIM] for h in heads],
                                            axis=0)
        col = lambda a, lane0: jnp.concatenate([a[:, lane0 + h:lane0 + h + 1] for h in heads], axis=0)
        k = stack(kn_ref)
        q = stack(qn_ref)
        v = stack(vv_ref)
        bcol = col(bg, 0)
        gcc = col(gc, N_DELTA_HEADS)
        gcr = jnp.concatenate([gct[N_DELTA_HEADS + h:N_DELTA_HEADS + h + 1, :] for h in heads], axis=1)
        g_last = [gc[cs - 1:cs, N_DELTA_HEADS + h:N_DELTA_HEADS + h + 1] for h in heads]
        glc = jnp.concatenate([jnp.broadcast_to(gl, (cs, 1)) for gl in g_last], axis=0)
        decay = jnp.exp(jnp.where(causal, gcc - gcr, -jnp.inf))
        kb = k * bcol
        eg = jnp.exp(gcc)
        kq = _mm_nt(jnp.concatenate([kb, q], axis=0), k)
        yield
        a = jnp.where(strict, kq[0:rows] * decay, 0.0)
        intra = jnp.where(causal, kq[rows:2 * rows] * decay, 0.0)
        x = -a
        nb = DELTA_INV_BLOCK
        y = jnp.where((rr // nb) == (cc // nb), x, 0.0)
        p = eye + y
        y = _mm(y, y)
        yield
        n_sq = max(1, int(nb - 1).bit_length())
        for lvl in range(1, n_sq):
            if lvl < n_sq - 1:
                py = _mm(jnp.concatenate([p, y], axis=0), y)
                p = p + py[0:rows]
                y = py[rows:2 * rows]
            else:
                p = p + _mm(p, y)
            yield
        size = 2 * nb
        while size <= cs:
            off = jnp.where(jnp.logical_and((rr // size) == (cc // size), (rr // (size // 2)) != (cc // (size // 2))),
                            x, 0.0)
            po = _mm(p, off)
            yield
            p = p + _mm(po, p)
            yield
            size *= 2
        sol = _mm(p, jnp.concatenate([v * bcol, kb * eg], axis=1))
        yield
        u = sol[:, 0:HEAD_DIM]
        w = sol[:, HEAD_DIM:2 * HEAD_DIM]
        lanes_g = slice(grp * gh * HEAD_DIM, (grp + 1) * gh * HEAD_DIM)
        while ch > 0 and not state_ready.get((ch - 1, grp)):
            yield
        s_g = s_scr[:, lanes_g]
        wq_s = _mm(jnp.concatenate([w, q * eg], axis=0), s_g)
        yield
        own = lambda m, r0: jnp.concatenate(
            [m[r0 + j * cs:r0 + (j + 1) * cs, j * HEAD_DIM:(j + 1) * HEAD_DIM] for j in range(gh)], axis=0)
        v_new = u - own(wq_s, 0)
        o = own(wq_s, rows) + _mm(intra, v_new)
        yield
        kg_t = (k * jnp.exp(glc - gcc)).T
        vn_blocks = jnp.concatenate([jnp.where(row_head == j, v_new, 0.0) for j in range(gh)], axis=1)
        s_decay = jnp.concatenate([jnp.broadcast_to(jnp.exp(gl), (1, HEAD_DIM)) for gl in g_last], axis=1)
        s_scr[:, lanes_g] = s_g * s_decay + _mm(kg_t, vn_blocks)
        state_ready[(ch, grp)] = True
        yield
        on = o * lax.rsqrt(jnp.mean(o * o, axis=-1, keepdims=True) + EPS) * ow_ref[...]
        for j, h in enumerate(heads):
            sl = slice(h * HEAD_DIM, (h + 1) * HEAD_DIM)
            od_ref[r0:r0 + cs, sl] = (on[j * cs:(j + 1) * cs] * _silu(z_ref[r0:r0 + cs, sl])).astype(BF16)

    chains = [group_steps(ch, grp) for ch in range(n_chunks) for grp in range(N_DELTA_HEADS // gh)]
    while chains:
        chains = [g for g in chains if next(g, "done") != "done"]

    so_ref[0] = s_scr[...]


def _delta_chunks(qn, kn, vv, bg, zsrc, z_col_block, state0, o_norm_w, b, t):
    n = b * t
    per_step = DELTA_CHUNKS_PER_STEP if (t // DELTA_CHUNK) % DELTA_CHUNKS_PER_STEP == 0 else 1
    nc = t // (DELTA_CHUNK * per_step)
    sw = N_DELTA_HEADS * HEAD_DIM
    row = lambda w, cb=0: pl.BlockSpec((DELTA_CHUNK * per_step, w), lambda bb, c: (bb * nc + c, cb))
    st = pl.BlockSpec((1, HEAD_DIM, sw), lambda bb, c: (bb, 0, 0))
    s_in = state0.transpose(0, 2, 1, 3).reshape(b, HEAD_DIM, sw)
    od, s_out = pl.pallas_call(
        functools.partial(_delta_chunk_kernel, n_chunks=per_step),
        name="delta_chunks",
        grid=(b, nc),
        in_specs=[row(DELTA_WIDTH), row(DELTA_WIDTH), row(DELTA_WIDTH), row(LANES),
                  row(DELTA_WIDTH, z_col_block), st, pl.BlockSpec((1, LANES), lambda bb, c: (0, 0))],
        out_specs=[row(DELTA_WIDTH), st],
        out_shape=[jax.ShapeDtypeStruct((n, DELTA_WIDTH), BF16),
                   jax.ShapeDtypeStruct((b, HEAD_DIM, sw), F32)],
        scratch_shapes=[pltpu.VMEM((HEAD_DIM, sw), F32)],
        compiler_params=_cparams(("parallel", "arbitrary")),
    )(qn, kn, vv, bg, zsrc, s_in, o_norm_w.reshape(1, LANES))
    return od, s_out.reshape(b, HEAD_DIM, N_DELTA_HEADS, HEAD_DIM).transpose(0, 2, 1, 3)


def _outproj_kernel(oa_ref, od_ref, x_ref, g1_ref, sh_ref, sc_ref, nw_ref, wo_ref, wrh_ref, br_ref,
                    x1_ref, h2_ref, lg_ref):
    mix = _dot(oa_ref[...], wo_ref[0:ATTN_WIDTH, :]) + _dot(od_ref[...], wo_ref[ATTN_WIDTH:ATTN_WIDTH + DELTA_WIDTH, :])
    x1 = x_ref[...] + g1_ref[0] * mix
    x1_ref[...] = x1
    y = x1 * lax.rsqrt(jnp.mean(x1 * x1, axis=-1, keepdims=True) + EPS) * nw_ref[...]
    h2 = y * (1.0 + sc_ref[0]) + sh_ref[0]
    h2_ref[...] = h2
    hb = h2.astype(BF16)
    lo = (h2 - hb.astype(F32)).astype(BF16)
    wr = wrh_ref[...]
    both = _dot(hb, wr)
    lg_ref[...] = both[:, 0:LANES] + both[:, LANES:2 * LANES] + _dot(lo, wr[:, 0:LANES]) + br_ref[...]


def _out_projection(o_attn, o_delta, x2d, gate1, shift2, scale2, norm2_w, w_out_bf, wr_both, b_rt,
                    tm, rows_per_mod_block):
    n, d = x2d.shape
    r = gate1.shape[1]
    tiles_per_mod = rows_per_mod_block // tm
    mod_spec = pl.BlockSpec((1, r, d), lambda i: (i // tiles_per_mod, 0, 0))
    row = lambda w: pl.BlockSpec((tm, w), lambda i: (i, 0))
    full = lambda a: pl.BlockSpec(a.shape, lambda i: (0, 0))
    return pl.pallas_call(
        _outproj_kernel,
        name="out_proj",
        grid=(n // tm,),
        in_specs=[row(ATTN_WIDTH), row(DELTA_WIDTH), row(d), mod_spec, mod_spec, mod_spec,
                  pl.BlockSpec((1, d), lambda i: (0, 0)), full(w_out_bf), full(wr_both), full(b_rt)],
        out_specs=[row(d), row(d), row(LANES)],
        out_shape=[jax.ShapeDtypeStruct((n, d), F32), jax.ShapeDtypeStruct((n, d), F32),
                   jax.ShapeDtypeStruct((n, LANES), F32)],
        compiler_params=_cparams(("parallel",)),
    )(o_attn, o_delta, x2d, gate1, shift2, scale2, norm2_w.reshape(1, d), w_out_bf, wr_both, b_rt)


def _route_kernel(lg_ref, eid_ref, gate_ref):
    x = lg_ref[...]
    lane = lax.broadcasted_iota(I32, x.shape, 1)
    gl = jnp.where(lane < N_GROUPS, x, -jnp.inf)
    ge = jnp.exp(gl - jnp.max(gl, axis=1, keepdims=True))
    p = ge / jnp.sum(ge, axis=1, keepdims=True)
    p_max = jnp.max(p, axis=1, keepdims=True)
    grp = jnp.min(jnp.where(p == p_max, lane, LANES), axis=1, keepdims=True)
    e_lane = lane - N_GROUPS
    in_grp = jnp.logical_and(jnp.logical_and(e_lane >= 0, e_lane < N_EXPERTS),
                             (e_lane >> 3) == grp)
    rl = jnp.where(in_grp, x, -jnp.inf)
    v1 = jnp.max(rl, axis=1, keepdims=True)
    i1 = jnp.min(jnp.where(rl == v1, lane, LANES), axis=1, keepdims=True)
    rl2 = jnp.where(lane == i1, -jnp.inf, rl)
    v2 = jnp.max(rl2, axis=1, keepdims=True)
    i2 = jnp.min(jnp.where(rl2 == v2, lane, LANES), axis=1, keepdims=True)
    t = jnp.exp(v2 - v1)
    den = 1.0 + t
    eid_ref[...] = jnp.where(lane == 0, i1 - N_GROUPS, jnp.where(lane == 1, i2 - N_GROUPS, 0))
    gate_ref[...] = jnp.where(lane == 0, (1.0 / den) * p_max, jnp.where(lane == 1, (t / den) * p_max, 0.0))


def _route(logits, tm):
    n = logits.shape[0]
    spec = pl.BlockSpec((tm, LANES), lambda i: (i, 0))
    return pl.pallas_call(
        _route_kernel,
        name="route",
        grid=(n // tm,),
        in_specs=[spec],
        out_specs=[spec, spec],
        out_shape=[jax.ShapeDtypeStruct((n, LANES), I32), jax.ShapeDtypeStruct((n, LANES), F32)],
        compiler_params=_cparams(("parallel",)),
    )(logits)


def _row_gather(idx_ref, base, n_rows, src_hbm, dst, sem):
    def body(r, carry):
        pltpu.make_async_copy(src_hbm.at[pl.ds(idx_ref[base + r], 1), :], dst.at[pl.ds(r, 1), :], sem).start()
        return carry

    lax.fori_loop(0, n_rows, body, 0, unroll=8)


def _row_gather_wait(n_rows, src_hbm, dst, sem):
    pltpu.make_async_copy(src_hbm.at[pl.ds(0, n_rows), :], dst, sem).wait()


def _moe_kernel(tok_ref, j0_ref, be_ref, na_ref, h_hbm, wg_ref, wu_ref, wd_ref, o_ref,
                x_even, x_odd, sem, wg_scr, wu_scr, wd_scr, *, bm):
    i = pl.program_id(0)
    n_act = na_ref[0]
    bufs = ((x_even, sem.at[0]), (x_odd, sem.at[1]))

    @pl.when(i == 0)
    def _():
        _row_gather(tok_ref, j0_ref[0], bm, h_hbm, x_even, sem.at[0])

    changed = jnp.logical_or(i == 0, be_ref[i] != be_ref[jnp.maximum(i - 1, 0)])

    @pl.when(jnp.logical_and(i < n_act, changed))
    def _():
        wg_scr[...] = wg_ref[0].astype(BF16)
        wu_scr[...] = wu_ref[0].astype(BF16)
        wd_scr[...] = wd_ref[0].astype(BF16)

    for parity in range(2):
        cur, cur_sem = bufs[parity]
        nxt, nxt_sem = bufs[1 - parity]

        @pl.when(jnp.logical_and(i < n_act, i % 2 == parity))
        def _():
            _row_gather_wait(bm, h_hbm, cur, cur_sem)
            base = j0_ref[i + 1]
            for r in range(bm):
                pltpu.make_async_copy(h_hbm.at[pl.ds(tok_ref[base + r], 1), :], nxt.at[pl.ds(r, 1), :],
                                      nxt_sem).start()
            x = cur[...].astype(BF16)
            hid = _silu(_dot(x, wg_scr[...])) * _dot(x, wu_scr[...])
            o_ref[...] = _dot(hid.astype(BF16), wd_scr[...])

        @pl.when(jnp.logical_and(i == n_act, i % 2 == parity))
        def _():
            _row_gather_wait(bm, h_hbm, cur, cur_sem)

    @pl.when(i >= n_act)
    def _():
        o_ref[...] = jnp.zeros(o_ref.shape, F32)


def _moe_experts(tok_sorted, block_j0, block_exp, n_active, h2, w_gate, w_up, w_down, bm):
    ns = block_exp.shape[0] * bm
    d = h2.shape[1]
    f = w_gate.shape[2]
    grid_spec = pltpu.PrefetchScalarGridSpec(
        num_scalar_prefetch=4,
        grid=(ns // bm,),
        in_specs=[pl.BlockSpec(memory_space=pl.ANY),
                  pl.BlockSpec((1, d, f), lambda i, tok, j0, be, na: (be[i], 0, 0)),
                  pl.BlockSpec((1, d, f), lambda i, tok, j0, be, na: (be[i], 0, 0)),
                  pl.BlockSpec((1, f, d), lambda i, tok, j0, be, na: (be[i], 0, 0))],
        out_specs=pl.BlockSpec((bm, d), lambda i, tok, j0, be, na: (i, 0)),
        scratch_shapes=[pltpu.VMEM((bm, d), F32), pltpu.VMEM((bm, d), F32), pltpu.SemaphoreType.DMA((2,)),
                        pltpu.VMEM((d, f), BF16), pltpu.VMEM((d, f), BF16), pltpu.VMEM((f, d), BF16)],
    )
    return pl.pallas_call(
        functools.partial(_moe_kernel, bm=bm),
        name="moe_experts",
        grid_spec=grid_spec,
        out_shape=jax.ShapeDtypeStruct((ns, d), F32),
        compiler_params=_cparams(("arbitrary",)),
    )(tok_sorted, block_j0, block_exp, n_active, h2, w_gate, w_up, w_down)


def _combine_kernel(dest_ref, x1_ref, gt_ref, g2_ref, y_hbm, o_ref, y_buf, sem, *, tm):
    i = pl.program_id(0)
    n = pl.num_programs(0)
    slot = i % 2

    @pl.when(i == 0)
    def _():
        _row_gather(dest_ref, 0, 2 * tm, y_hbm, y_buf.at[0], sem.at[0])

    for parity in range(2):
        @pl.when(jnp.logical_and(i + 1 < n, slot == parity))
        def _():
            base = (i + 1) * 2 * tm
            for r in range(2 * tm):
                pltpu.make_async_copy(y_hbm.at[pl.ds(dest_ref[base + r], 1), :],
                                      y_buf.at[1 - parity, pl.ds(r, 1), :], sem.at[1 - parity]).start()

    _row_gather_wait(2 * tm, y_hbm, y_buf.at[slot], sem.at[slot])
    gt = gt_ref[...]
    y = y_buf[slot, 0:tm, :] * gt[:, 0:1] + y_buf[slot, tm:2 * tm, :] * gt[:, 1:2]
    o_ref[...] = x1_ref[...] + g2_ref[0] * y


def _combine(x1, dest, y_rows, gates, gate2, tm, rows_per_mod_block):
    n, d = x1.shape
    r = gate2.shape[1]
    tiles_per_mod = rows_per_mod_block // tm
    dest_tiles = dest.reshape(n // tm, tm, 2).transpose(0, 2, 1).reshape(-1)
    row = lambda w: pl.BlockSpec((tm, w), lambda i, dst: (i, 0))
    grid_spec = pltpu.PrefetchScalarGridSpec(
        num_scalar_prefetch=1,
        grid=(n // tm,),
        in_specs=[row(d), row(LANES),
                  pl.BlockSpec((1, r, d), lambda i, dst: (i // tiles_per_mod, 0, 0)),
                  pl.BlockSpec(memory_space=pl.ANY)],
        out_specs=row(d),
        scratch_shapes=[pltpu.VMEM((2, 2 * tm, d), F32), pltpu.SemaphoreType.DMA((2,))],
    )
    return pl.pallas_call(
        functools.partial(_combine_kernel, tm=tm),
        name="moe_combine",
        grid_spec=grid_spec,
        out_shape=jax.ShapeDtypeStruct((n, d), F32),
        compiler_params=_cparams(("arbitrary",)),
    )(dest_tiles, x1, gates, gate2, y_rows)


FRONT_SECTIONS = ((C_QA, ATTN_WIDTH), (C_QI, IDX_WIDTH), (C_ZD, DELTA_WIDTH),
                  (C_CONV, DELTA_WIDTH), (C_CONV + DELTA_WIDTH, DELTA_WIDTH), (C_CONV + 2 * DELTA_WIDTH, DELTA_WIDTH),
                  (C_KA, PROJ_PACKED - C_KA))


def _prompt_front_kernel(x_ref, sh_ref, sc_ref, nw_ref, w_ref, tabm_ref, tabi_ref, qw_ref, kw_ref, iw_ref,
                         prev_ref, cw_ref, al_ref, dt_ref,
                         q_ref, qi_ref, z_ref, qn_ref, kn_ref, vv_ref, kf_ref, kb_ref, vf_ref, vt_ref,
                         kif_ref, kx_ref, bg_ref, misc_ref, tail_ref,
                         carry_scr, xp_scr, *, tiles_per_seq, tm):
    i = pl.program_id(0)
    tile_in_seq = i % tiles_per_seq
    x = x_ref[...]
    y = x * lax.rsqrt(jnp.mean(x * x, axis=-1, keepdims=True) + EPS) * nw_ref[...]
    h = (y * (1.0 + sc_ref[0]) + sh_ref[0]).astype(BF16)
    tabm = tabm_ref[0]
    tabi = tabi_ref[0]
    half_main = HEAD_DIM // ROPE_FRACTION // 2
    half_idx = IDX_DIM // ROPE_FRACTION // 2

    @pl.when(tile_in_seq == 0)
    def _():
        carry_scr[...] = prev_ref[0]

    def queries(pj):
        for hh in range(N_ATTN_HEADS):
            sl = slice(hh * HEAD_DIM, (hh + 1) * HEAD_DIM)
            v = _rope(_rms_head(pj[:, sl], qw_ref[...]), tabm, half_main)
            q_ref[:, sl] = (v * (HEAD_DIM ** -0.5)).astype(BF16)

    def index_queries(pj):
        for p in range(IDX_WIDTH // LANES):
            sl = slice(p * LANES, (p + 1) * LANES)
            qi_ref[:, sl] = _rope(pj[:, sl], tabi, half_idx).astype(BF16)

    def gate_z(pj):
        z_ref[...] = pj

    def conv_section(sec, pj):
        cols = slice(sec * DELTA_WIDTH, (sec + 1) * DELTA_WIDTH)
        xp_scr[sec, 0:SUBLANES, :] = carry_scr[:, cols]
        xp_scr[sec, SUBLANES:SUBLANES + tm, :] = pj
        carry_scr[:, cols] = pj[tm - SUBLANES:tm, :]
        base = SUBLANES - (CONV_WIDTH - 1)
        out = (qn_ref, kn_ref, vv_ref)[sec]
        for hh in range(N_DELTA_HEADS):
            sl = slice(hh * HEAD_DIM, (hh + 1) * HEAD_DIM)
            wsl = slice(sec * DELTA_WIDTH + hh * HEAD_DIM, sec * DELTA_WIDTH + (hh + 1) * HEAD_DIM)
            v = xp_scr[sec, base:base + tm, sl] * cw_ref[0:1, wsl]
            for j in range(1, CONV_WIDTH):
                v = v + xp_scr[sec, base + j:base + j + tm, sl] * cw_ref[j:j + 1, wsl]
            v = _silu(v)
            if sec < 2:
                v = v * lax.rsqrt(jnp.sum(v * v, axis=-1, keepdims=True) + EPS)
            if sec == 0:
                v = v * (HEAD_DIM ** -0.5)
            out[:, sl] = v
        if sec == 2:
            tail_ref[0] = carry_scr[...]

    def keys_values_misc(pj):
        for hk in range(N_KV_HEADS):
            sl = slice(hk * HEAD_DIM, (hk + 1) * HEAD_DIM)
            v = _rope(_rms_head(pj[:, sl], kw_ref[...]), tabm, half_main)
            kf_ref[:, sl] = v
            kb_ref[:, sl] = v.astype(BF16)
        va = pj[:, KV_WIDTH:2 * KV_WIDTH]
        vf_ref[...] = va
        vt_ref[...] = va.T.astype(BF16)
        m = pj[:, 2 * KV_WIDTH:2 * KV_WIDTH + LANES]
        misc_ref[...] = m
        lane = lax.broadcasted_iota(I32, m.shape, 1)
        ki = jnp.where(lane < IDX_DIM, m, 0.0)
        ms = jnp.sum(ki * ki, axis=-1, keepdims=True) * (1.0 / IDX_DIM)
        v = _rope(ki * lax.rsqrt(ms + EPS) * iw_ref[...], tabi, half_idx)
        kif_ref[...] = v[:, 0:IDX_DIM]
        kx_ref[...] = (v + pltpu.roll(v, IDX_DIM, 1)).astype(BF16)
        beta = _sigmoid(m)
        g = -jnp.exp(al_ref[...]) * _softplus(m + dt_ref[...])
        is_b = jnp.logical_and(lane >= M_BD, lane < M_BD + N_DELTA_HEADS)
        is_g = jnp.logical_and(lane >= M_AD, lane < M_AD + N_DELTA_HEADS)
        comb = jnp.where(is_b, beta, jnp.where(is_g, g, 0.0))
        bg_ref[...] = pltpu.roll(comb, LANES - M_BD, 1)

    epilogues = (queries, index_queries, gate_z,
                 functools.partial(conv_section, 0), functools.partial(conv_section, 1),
                 functools.partial(conv_section, 2), keys_values_misc)
    project = lambda k: _dot(h, w_ref[:, FRONT_SECTIONS[k][0]:FRONT_SECTIONS[k][0] + FRONT_SECTIONS[k][1]])
    pj_next = project(0)
    for k, epilogue in enumerate(epilogues):
        pj = pj_next
        if k + 1 < len(epilogues):
            pj_next = project(k + 1)
        epilogue(pj)


def _prompt_front(x2d, shift, scale, norm_w, w_packed, pos, q_norm_w, k_norm_w, idx_k_norm_w, prev8, conv_w,
                  a_log, dt_bias, b, t, tm):
    n, d = x2d.shape
    np_ = w_packed.shape[1]
    tiles_per_seq = t // tm
    tabm = _rope_tables(pos, HEAD_DIM, LANES).reshape(tiles_per_seq, tm, 3 * LANES)
    tabi = _rope_tables(pos, IDX_DIM, IDX_DIM).reshape(tiles_per_seq, tm, 3 * LANES)
    iw = jnp.concatenate([idx_k_norm_w, jnp.zeros((LANES - IDX_DIM,), F32)]).reshape(1, LANES)
    pad_vec = lambda v: jnp.zeros((1, LANES), F32).at[0, M_AD:M_AD + N_DELTA_HEADS].set(v)
    mod_spec = pl.BlockSpec((1, 1, d), lambda i: (i // tiles_per_seq, 0, 0))
    tab_spec = pl.BlockSpec((1, tm, 3 * LANES), lambda i: (i % tiles_per_seq, 0, 0))
    vec = lambda w: pl.BlockSpec((1, w), lambda i: (0, 0))
    row = lambda w: pl.BlockSpec((tm, w), lambda i: (i, 0))
    seq_state = pl.BlockSpec((1, SUBLANES, CONV_CHANNELS), lambda i: (i // tiles_per_seq, 0, 0))
    widths = [(ATTN_WIDTH, BF16), (IDX_WIDTH, BF16), (DELTA_WIDTH, F32), (DELTA_WIDTH, F32), (DELTA_WIDTH, F32),
              (DELTA_WIDTH, F32), (KV_WIDTH, F32), (KV_WIDTH, BF16), (KV_WIDTH, F32)]
    outs = pl.pallas_call(
        functools.partial(_prompt_front_kernel, tiles_per_seq=tiles_per_seq, tm=tm),
        name="prompt_front",
        grid=(n // tm,),
        in_specs=[row(d), mod_spec, mod_spec, vec(d),
                  pl.BlockSpec((d, np_), lambda i: (0, 0), pipeline_mode=pl.Buffered(1)),
                  tab_spec, tab_spec, vec(LANES), vec(LANES), vec(LANES), seq_state,
                  pl.BlockSpec((CONV_WIDTH, CONV_CHANNELS), lambda i: (0, 0)), vec(LANES), vec(LANES)],
        out_specs=[row(w) for w, _ in widths]
                  + [pl.BlockSpec((KV_WIDTH, tm), lambda i: (0, i)), row(IDX_DIM), row(LANES), row(LANES), row(LANES),
                     seq_state],
        out_shape=[jax.ShapeDtypeStruct((n, w), dt) for w, dt in widths]
                  + [jax.ShapeDtypeStruct((KV_WIDTH, n), BF16), jax.ShapeDtypeStruct((n, IDX_DIM), F32),
                     jax.ShapeDtypeStruct((n, LANES), BF16), jax.ShapeDtypeStruct((n, LANES), F32),
                     jax.ShapeDtypeStruct((n, LANES), F32),
                     jax.ShapeDtypeStruct((b, SUBLANES, CONV_CHANNELS), F32)],
        scratch_shapes=[pltpu.VMEM((SUBLANES, CONV_CHANNELS), F32),
                        pltpu.VMEM((3, SUBLANES + tm, DELTA_WIDTH), F32)],
        compiler_params=pltpu.CompilerParams(dimension_semantics=("arbitrary",),
                                             vmem_limit_bytes=FRONT_VMEM_LIMIT),
    )(x2d, shift, scale, norm_w.reshape(1, d), w_packed, tabm, tabi,
      q_norm_w.reshape(1, LANES), k_norm_w.reshape(1, LANES), iw, prev8, conv_w, pad_vec(a_log), pad_vec(dt_bias))
    names = ("q", "qi", "z", "qn", "kn", "vv", "kf", "kb", "vf", "vt", "kif", "kx", "bg", "misc", "tail")
    return dict(zip(names, outs))


def _pick_tile(n, pref, mult=16):
    t = min(pref, n)
    while n % t or t % mult:
        t -= 1
    return t


def _pack_w_in(w_in):
    d = w_in.shape[0]
    bounds = np.cumsum(PROJ_SIZES)[:-1].tolist()
    qa, ka, va, qi, ki, wi, qd, kd, vd, zd, bd, ad = jnp.split(w_in, bounds, axis=1)
    used = IDX_DIM + N_IDX_HEADS + 2 * N_DELTA_HEADS
    misc = jnp.concatenate([ki, wi, bd, ad, jnp.zeros((d, LANES - used), w_in.dtype)], axis=1)
    cols = [qa, qi, zd, qd, kd, vd, ka, va, misc]
    width = sum(c.shape[1] for c in cols)
    cols.append(jnp.zeros((d, PROJ_PACKED - width), w_in.dtype))
    return jnp.concatenate(cols, axis=1).astype(BF16)


def _route_and_sort(eid, bm):
    n = eid.shape[0]
    nk = 2 * n
    flat_e = eid.reshape(-1)
    order = jnp.argsort(flat_e, stable=True).astype(I32)
    inv = jnp.argsort(order).astype(I32)
    onehot = flat_e[:, None] == jnp.arange(N_EXPERTS, dtype=I32)[None, :]
    counts = jnp.sum(onehot.astype(I32), axis=0)
    padded = (counts + bm - 1) // bm * bm
    pad_end = jnp.cumsum(padded)
    shift = (pad_end - padded) - (jnp.cumsum(counts) - counts)
    dest = inv + jnp.sum(jnp.where(onehot, shift[None, :], 0), axis=1)
    n_blocks = -(-nk // bm) + N_EXPERTS + 1
    block_exp = jnp.minimum(jnp.searchsorted(pad_end, jnp.arange(n_blocks, dtype=I32) * bm, side='right'),
                            N_EXPERTS - 1).astype(I32)
    n_active = (pad_end[-1] // bm).astype(I32).reshape(1)
    block_j0 = jnp.clip(jnp.arange(n_blocks, dtype=I32) * bm - shift[block_exp], 0, nk)
    tok_sorted = jnp.concatenate([order // 2, jnp.zeros((bm,), I32)])
    return tok_sorted, block_j0, dest.astype(I32).reshape(n, 2), block_exp, n_active


def _layer(layer, yp, ys, cache_k, cache_v, cache_idx, state_ssm, state_conv, page_table, c_prompt, c_sample,
           w_in, w_out, conv_w, a_log, dt_bias, q_norm_w, k_norm_w, idx_k_norm_w, o_norm_w, norm1_w, norm2_w,
           w_ada, b_ada, w_group, b_group, w_router, b_router, w_gate, w_up, w_down):
    bp, tp, d = yp.shape
    bs, ts, _ = ys.shape
    past = page_table.shape[1] * PAGE_SIZE
    rows = SAMPLE_ROWS
    assert CONV_WIDTH - 1 <= ts <= rows and tp % SEL_SPAN == 0 and tp % DELTA_CHUNK == 0

    n_c = bp + bs
    n_c_pad = -(-n_c // SUBLANES) * SUBLANES
    c_all = jnp.concatenate([c_prompt, c_sample, jnp.zeros((n_c_pad - n_c, d), F32)], axis=0)
    mod = _ada_modulation(c_all, w_ada, b_ada)
    mods = jnp.split(mod, N_MOD, axis=1)
    mp = [m[:bp].reshape(bp, 1, d) for m in mods]
    ms = [jnp.repeat(m[bp:bp + bs], rows, axis=0).reshape(1, bs * rows, d) for m in mods]

    w_packed = _pack_w_in(w_in)
    w_out_bf = w_out.astype(BF16)
    w_rt = jnp.concatenate([w_group, w_router, jnp.zeros((d, LANES - N_GROUPS - N_EXPERTS), F32)], axis=1)
    wr_hi = w_rt.astype(BF16)
    wr_both = jnp.concatenate([wr_hi, (w_rt - wr_hi.astype(F32)).astype(BF16)], axis=1)
    b_rt = jnp.concatenate([b_group, b_router, jnp.zeros((LANES - N_GROUPS - N_EXPERTS,), F32)]).reshape(1, LANES)

    np_ = bp * tp
    xp2 = yp.reshape(np_, d)
    tm_p = _pick_tile(tp, 256)
    fr = _prompt_front(xp2, mp[0], mp[1], norm1_w, w_packed, jnp.arange(tp), q_norm_w, k_norm_w, idx_k_norm_w,
                       jnp.zeros((bp, SUBLANES, CONV_CHANNELS), F32), conv_w, a_log, dt_bias, bp, tp, tm_p)
    kf_p, kif_p = fr["kf"], fr["kif"]
    oa_p = _dsa_prompt(fr["q"], fr["qi"], fr["misc"], fr["kb"], fr["vt"], fr["kx"], bp, tp)
    od_p, ssm_p = _delta_chunks(fr["qn"], fr["kn"], fr["vv"], fr["bg"], fr["z"], 0,
                                jnp.zeros((bp, N_DELTA_HEADS, HEAD_DIM, HEAD_DIM), F32), o_norm_w, bp, tp)
    tm_o = _pick_tile(tp, 256)
    x1_p, h2_p, lg_p = _out_projection(oa_p, od_p, xp2, mp[2], mp[3], mp[4], norm2_w, w_out_bf, wr_both, b_rt,
                                       tm_o, tp)

    ns_ = bs * rows
    xs2 = jnp.pad(ys, ((0, 0), (0, rows - ts), (0, 0))).reshape(ns_, d)
    tm_s = _pick_tile(ns_, 256)
    proj_s = _in_projection(xs2, ms[0].reshape(ns_ // tm_s, tm_s, d), ms[1].reshape(ns_ // tm_s, tm_s, d),
                            norm1_w, w_packed, tm_s, tm_s)
    q_s, kf_s, kb_s, vb_s, qi_s, kif_s, _ = _attention_prep(
        proj_s, past + jnp.arange(rows), rows, q_norm_w, k_norm_w, idx_k_norm_w, False)
    q_t = qi_s.reshape(bs, rows, N_IDX_HEADS, IDX_DIM).transpose(0, 2, 1, 3).reshape(bs, N_IDX_HEADS * rows, IDX_DIM)
    w_col = proj_s[:, C_MISC + M_WI:C_MISC + M_WI + N_IDX_HEADS].reshape(bs, rows, N_IDX_HEADS)
    w_col = w_col.transpose(0, 2, 1).reshape(bs, N_IDX_HEADS * rows, 1)
    pages = _pick_tile(page_table.shape[1], SAMPLE_PAGES_PER_STEP, 1)
    keys_past, keys_new = _sample_scores(page_table, q_t, w_col, kif_s, cache_idx, layer, pages, ts)
    n_sel_s = min(TOPK_MAX, (past + ts) // 4)
    oa_s = _sample_attend(page_table, keys_past, keys_new, q_s, kb_s, vb_s, cache_k, cache_v, layer, pages, n_sel_s)
    prev8 = jnp.pad(state_conv, ((0, 0), (SUBLANES - (CONV_WIDTH - 1), 0), (0, 0)))
    qn_s, kn_s, vv_s, bg_s = _delta_prep(proj_s, prev8, conv_w, a_log, dt_bias, bs, rows, rows, ts)
    to_chunk = lambda a: jnp.pad(a.reshape(bs, rows, -1), ((0, 0), (0, DELTA_CHUNK - rows), (0, 0))).reshape(
        bs * DELTA_CHUNK, -1)
    z_s = proj_s[:, C_ZD:C_ZD + DELTA_WIDTH]
    od_s, ssm_s = _delta_chunks(to_chunk(qn_s), to_chunk(kn_s), to_chunk(vv_s), to_chunk(bg_s), to_chunk(z_s), 0,
                                state_ssm, o_norm_w, bs, DELTA_CHUNK)
    od_s = od_s.reshape(bs, DELTA_CHUNK, DELTA_WIDTH)[:, :rows].reshape(ns_, DELTA_WIDTH)
    x1_s, h2_s, lg_s = _out_projection(oa_s, od_s, xs2, ms[2], ms[3], ms[4], norm2_w, w_out_bf, wr_both, b_rt,
                                       ns_, ns_)

    n_all = np_ + ns_
    h2_all = jnp.concatenate([h2_p, h2_s], axis=0)
    lg_all = jnp.concatenate([lg_p, lg_s], axis=0)
    eid, gates = _route(lg_all, _pick_tile(n_all, 512, SUBLANES))
    bm = 256
    tok_sorted, block_j0, dest, block_exp, n_active = _route_and_sort(eid[:, 0:2], bm)
    yb = _moe_experts(tok_sorted, block_j0, block_exp, n_active, h2_all, w_gate, w_up, w_down, bm)
    out_p = _combine(x1_p, dest[:np_], yb, gates[:np_], mp[5], tm_o, tp)
    out_s = _combine(x1_s, dest[np_:], yb, gates[np_:], ms[5].reshape(ns_ // tm_s, tm_s, d), tm_s, tm_s)

    valid = lambda a: a.reshape(bs, rows, -1)[:, :ts]
    conv_p = fr["tail"][:, SUBLANES - (CONV_WIDTH - 1):]
    conv_s = proj_s.reshape(bs, rows, PROJ_PACKED)[:, ts - (CONV_WIDTH - 1):ts, C_CONV:C_CONV + CONV_CHANNELS]
    return (out_p.reshape(bp, tp, d), valid(out_s),
            kf_p.reshape(bp, tp, N_KV_HEADS, HEAD_DIM),
            fr["vf"].reshape(bp, tp, N_KV_HEADS, HEAD_DIM),
            kif_p.reshape(bp, tp, IDX_DIM), ssm_p, conv_p,
            valid(kf_s).reshape(bs, ts, N_KV_HEADS, HEAD_DIM),
            valid(proj_s[:, C_VA:C_VA + KV_WIDTH]).reshape(bs, ts, N_KV_HEADS, HEAD_DIM),
            valid(kif_s), ssm_s, conv_s)


def kernel(x_prompt, x_sample, cache_k, cache_v, cache_idx_k, state_ssm, state_conv, page_table, c_prompt, c_sample,
           w_in, w_out, conv_w, a_log, dt_bias, q_norm_w, k_norm_w, idx_k_norm_w, o_norm_w, norm1_w, norm2_w,
           w_ada, b_ada, w_group, b_group, w_router, b_router, w_gate, w_up, w_down):
    depth = w_in.shape[0]
    yp, ys = x_prompt, x_sample
    per_layer = []
    for l in range(depth):
        res = _layer(l, yp, ys, cache_k, cache_v, cache_idx_k, state_ssm[l], state_conv[l], page_table,
                     c_prompt, c_sample, w_in[l], w_out[l], conv_w[l], a_log[l], dt_bias[l], q_norm_w[l],
                     k_norm_w[l], idx_k_norm_w[l], o_norm_w[l], norm1_w[l], norm2_w[l], w_ada[l], b_ada[l],
                     w_group[l], b_group[l], w_router[l], b_router[l], w_gate[l], w_up[l], w_down[l])
        yp, ys = res[0], res[1]
        per_layer.append(res[2:])
    stacked = tuple(jnp.stack([pl_[j] for pl_ in per_layer]) for j in range(10))
    return (yp, ys) + stacked
```

```python
import functools

import jax
import jax.numpy as jnp
import numpy as np
from jax import lax
from jax.experimental import pallas as pl
from jax.experimental.pallas import tpu as pltpu

F32 = jnp.float32
BF16 = jnp.bfloat16
I32 = jnp.int32

HEAD_DIM = 128
N_ATTN_HEADS = 8
N_KV_HEADS = 2
KV_GROUP = N_ATTN_HEADS // N_KV_HEADS
N_DELTA_HEADS = 8
N_IDX_HEADS = 16
IDX_DIM = 64
ATTN_WIDTH = N_ATTN_HEADS * HEAD_DIM
KV_WIDTH = N_KV_HEADS * HEAD_DIM
DELTA_WIDTH = N_DELTA_HEADS * HEAD_DIM
IDX_WIDTH = N_IDX_HEADS * IDX_DIM
CONV_CHANNELS = 3 * DELTA_WIDTH
TOPK_MAX = 256
ROPE_THETA = 500000.0
ROPE_FRACTION = 4
CONV_WIDTH = 4
DELTA_CHUNK = 64
N_GROUPS = 8
EXPERTS_PER_GROUP = 8
N_EXPERTS = N_GROUPS * EXPERTS_PER_GROUP
N_MOD = 6
EPS = 1e-6
PAGE_SIZE = 128
PROJ_SIZES = (ATTN_WIDTH, KV_WIDTH, KV_WIDTH, IDX_WIDTH, IDX_DIM, N_IDX_HEADS,
              DELTA_WIDTH, DELTA_WIDTH, DELTA_WIDTH, DELTA_WIDTH, N_DELTA_HEADS, N_DELTA_HEADS)

LANES = 128
SUBLANES = 8
VMEM_LIMIT = 56 * 1024 * 1024
FRONT_VMEM_LIMIT = 60 * 1024 * 1024

C_QA = 0
C_QI = 1024
C_ZD = 2048
C_CONV = 3072
C_KA = 6144
C_VA = 6400
C_MISC = 6656
PROJ_PACKED = 6912
M_KI = 0
M_WI = 64
M_BD = 80
M_AD = 88

Q_TILE = 128
KEY_CHUNK = 256
SEL_SPAN = 512
NEG_INF_KEY = -2139095041
SAMPLE_ROWS = 16
SAMPLE_PAGES_PER_STEP = 16
NEG_BIG = -1e30
INT_MIN = -2147483648
INT_MAX = 2147483647


def _cparams(sem):
    return pltpu.CompilerParams(dimension_semantics=sem, vmem_limit_bytes=VMEM_LIMIT)


def _dot(a, b):
    return jnp.dot(a, b, preferred_element_type=F32)


def _dot_nt(a, b):
    return lax.dot_general(a, b, (((1,), (1,)), ((), ())), preferred_element_type=F32)


def _dot_tn(a, b):
    return lax.dot_general(a, b, (((0,), (0,)), ((), ())), preferred_element_type=F32)


def _sigmoid(x):
    return 0.5 * jnp.tanh(0.5 * x) + 0.5


def _silu(x):
    return x * _sigmoid(x)


def _softplus(x):
    return jnp.maximum(x, 0.0) + jnp.log(1.0 + jnp.exp(-jnp.abs(x)))


def _ada_kernel(c_ref, w_ref, b_ref, o_ref):
    s = _silu(c_ref[...]).astype(BF16)
    o_ref[...] = _dot(s, w_ref[...].astype(BF16)) + b_ref[...]


def _ada_modulation(c, w_ada, b_ada):
    r, d = c.shape
    n = w_ada.shape[1]
    tn = 1024 if n % 1024 == 0 else n
    return pl.pallas_call(
        _ada_kernel,
        name="ada_mod",
        grid=(n // tn,),
        in_specs=[pl.BlockSpec((r, d), lambda j: (0, 0)),
                  pl.BlockSpec((d, tn), lambda j: (0, j)),
                  pl.BlockSpec((1, tn), lambda j: (0, j))],
        out_specs=pl.BlockSpec((r, tn), lambda j: (0, j)),
        out_shape=jax.ShapeDtypeStruct((r, n), F32),
        compiler_params=_cparams(("parallel",)),
    )(c, w_ada, b_ada.reshape(1, n))


INPROJ_COLS = 1152


def _inproj_kernel(x_ref, sh_ref, sc_ref, nw_ref, w_ref, o_ref):
    x = x_ref[...]
    y = x * lax.rsqrt(jnp.mean(x * x, axis=-1, keepdims=True) + EPS) * nw_ref[...]
    h = (y * (1.0 + sc_ref[0]) + sh_ref[0]).astype(BF16)
    for c0 in range(0, o_ref.shape[1], INPROJ_COLS):
        o_ref[:, c0:c0 + INPROJ_COLS] = _dot(h, w_ref[:, c0:c0 + INPROJ_COLS])


def _in_projection(x2d, shift, scale, norm_w, w_packed, tm, rows_per_mod_block):
    n, d = x2d.shape
    np_ = w_packed.shape[1]
    r = shift.shape[1]
    tiles_per_mod = rows_per_mod_block // tm
    mod_spec = pl.BlockSpec((1, r, d), lambda i: (i // tiles_per_mod, 0, 0))
    return pl.pallas_call(
        _inproj_kernel,
        name="in_proj",
        grid=(n // tm,),
        in_specs=[pl.BlockSpec((tm, d), lambda i: (i, 0)),
                  mod_spec, mod_spec,
                  pl.BlockSpec((1, d), lambda i: (0, 0)),
                  pl.BlockSpec((d, np_), lambda i: (0, 0), pipeline_mode=pl.Buffered(1))],
        out_specs=pl.BlockSpec((tm, np_), lambda i: (i, 0)),
        out_shape=jax.ShapeDtypeStruct((n, np_), F32),
        compiler_params=_cparams(("parallel",)),
    )(x2d, shift, scale, norm_w.reshape(1, d), w_packed)


def _rope(x, tab, rot):
    c = tab[:, 0:LANES]
    s1 = tab[:, LANES:2 * LANES]
    s2 = tab[:, 2 * LANES:3 * LANES]
    return x * c + pltpu.roll(x, LANES - rot, 1) * s1 + pltpu.roll(x, rot, 1) * s2


def _rms_head(x, w):
    return x * lax.rsqrt(jnp.mean(x * x, axis=-1, keepdims=True) + EPS) * w


def _prep_kernel(qa_ref, qi_ref, ka_ref, va_ref, misc_ref, tabm_ref, tabi_ref, qw_ref, kw_ref, iw_ref,
                 q_ref, kf_ref, kb_ref, vb_ref, qib_ref, kif_ref, kib_ref, *, transpose_v):
    tabm = tabm_ref[0]
    tabi = tabi_ref[0]
    half_main = HEAD_DIM // ROPE_FRACTION // 2
    half_idx = IDX_DIM // ROPE_FRACTION // 2
    for h in range(N_ATTN_HEADS):
        sl = slice(h * HEAD_DIM, (h + 1) * HEAD_DIM)
        y = _rope(_rms_head(qa_ref[:, sl], qw_ref[...]), tabm, half_main)
        q_ref[:, sl] = (y * (HEAD_DIM ** -0.5)).astype(BF16)
    for h in range(N_KV_HEADS):
        sl = slice(h * HEAD_DIM, (h + 1) * HEAD_DIM)
        y = _rope(_rms_head(ka_ref[:, sl], kw_ref[...]), tabm, half_main)
        kf_ref[:, sl] = y
        kb_ref[:, sl] = y.astype(BF16)
    if transpose_v:
        vb_ref[...] = va_ref[...].T.astype(BF16)
    else:
        vb_ref[...] = va_ref[...].astype(BF16)
    for p in range(IDX_WIDTH // LANES):
        sl = slice(p * LANES, (p + 1) * LANES)
        qib_ref[:, sl] = _rope(qi_ref[:, sl], tabi, half_idx).astype(BF16)
    m = misc_ref[...]
    lane = lax.broadcasted_iota(I32, m.shape, 1)
    ki = jnp.where(lane < IDX_DIM, m, 0.0)
    ms = jnp.sum(ki * ki, axis=-1, keepdims=True) * (1.0 / IDX_DIM)
    y = _rope(ki * lax.rsqrt(ms + EPS) * iw_ref[...], tabi, half_idx)
    kif_ref[...] = y[:, 0:IDX_DIM]
    kib_ref[...] = (y + pltpu.roll(y, IDX_DIM, 1)).astype(BF16)


def _rope_tables(pos, head_dim, group):
    d_rot = head_dim // ROPE_FRACTION
    half = d_rot // 2
    inv_freq = jnp.power(ROPE_THETA, -(jnp.arange(half, dtype=F32) * 2.0 / d_rot))
    ang = pos.astype(F32)[:, None] * inv_freq[None, :]
    cos = jnp.cos(ang)
    sin = jnp.sin(ang)
    t = pos.shape[0]
    z = jnp.zeros((t, group - d_rot), F32)
    c = jnp.concatenate([cos, cos, jnp.ones((t, group - d_rot), F32)], axis=1)
    s1 = jnp.concatenate([-sin, jnp.zeros((t, half), F32), z], axis=1)
    s2 = jnp.concatenate([jnp.zeros((t, half), F32), sin, z], axis=1)
    rep = LANES // group
    return jnp.concatenate([jnp.tile(c, (1, rep)), jnp.tile(s1, (1, rep)), jnp.tile(s2, (1, rep))], axis=1)


def _attention_prep(proj, pos, tq, q_norm_w, k_norm_w, idx_k_norm_w, transpose_v):
    n = proj.shape[0]
    p = pos.shape[0]
    g = p // tq
    tabm = _rope_tables(pos, HEAD_DIM, LANES).reshape(g, tq, 3 * LANES)
    tabi = _rope_tables(pos, IDX_DIM, IDX_DIM).reshape(g, tq, 3 * LANES)
    iw = jnp.concatenate([idx_k_norm_w, jnp.zeros((LANES - IDX_DIM,), F32)]).reshape(1, LANES)
    row = lambda w, c: pl.BlockSpec((tq, w), lambda i: (i, c // w))
    tab_spec = pl.BlockSpec((1, tq, 3 * LANES), lambda i: (i % g, 0, 0))
    vec_spec = pl.BlockSpec((1, LANES), lambda i: (0, 0))
    out_row = lambda w: pl.BlockSpec((tq, w), lambda i: (i, 0))
    v_spec = pl.BlockSpec((KV_WIDTH, tq), lambda i: (0, i)) if transpose_v else out_row(KV_WIDTH)
    v_shape = (KV_WIDTH, n) if transpose_v else (n, KV_WIDTH)
    return pl.pallas_call(
        functools.partial(_prep_kernel, transpose_v=transpose_v),
        name="attn_prep",
        grid=(n // tq,),
        in_specs=[row(ATTN_WIDTH, C_QA), row(IDX_WIDTH, C_QI), row(KV_WIDTH, C_KA), row(KV_WIDTH, C_VA),
                  row(LANES, C_MISC), tab_spec, tab_spec, vec_spec, vec_spec, vec_spec],
        out_specs=[out_row(ATTN_WIDTH), out_row(KV_WIDTH), out_row(KV_WIDTH), v_spec,
                   out_row(IDX_WIDTH), out_row(IDX_DIM), out_row(LANES)],
        out_shape=[jax.ShapeDtypeStruct((n, ATTN_WIDTH), BF16),
                   jax.ShapeDtypeStruct((n, KV_WIDTH), F32),
                   jax.ShapeDtypeStruct((n, KV_WIDTH), BF16),
                   jax.ShapeDtypeStruct(v_shape, BF16),
                   jax.ShapeDtypeStruct((n, IDX_WIDTH), BF16),
                   jax.ShapeDtypeStruct((n, IDX_DIM), F32),
                   jax.ShapeDtypeStruct((n, LANES), BF16)],
        compiler_params=_cparams(("parallel",)),
    )(proj, proj, proj, proj, proj, tabm, tabi,
      q_norm_w.reshape(1, LANES), k_norm_w.reshape(1, LANES), iw)


def _sort_key(x):
    b = pltpu.bitcast(x + 0.0, I32)
    return b ^ ((b >> 31) & INT_MAX)


def _kth_largest_key(count_ge, k, shape, n_total):
    def body(it, carry):
        ans_u, n_ge = carry
        bit = jnp.left_shift(jnp.int32(1), 31 - it)
        cand_u = ans_u | bit
        cnt = count_ge(cand_u ^ INT_MIN)
        ok = cnt >= k
        return jnp.where(ok, cand_u, ans_u), jnp.where(ok, cnt, n_ge)

    ans_u, n_ge = lax.fori_loop(0, 32, body, (jnp.zeros(shape, I32), jnp.full(shape, float(n_total), F32)))
    return ans_u ^ INT_MIN, n_ge


def _tie_index_limit(count_eq_le, need, n_keys, shape):
    nbits = max(1, int(n_keys - 1).bit_length())

    def body(it, lo):
        bit = jnp.left_shift(jnp.int32(1), nbits - 1 - it)
        cand = lo | bit
        cnt = count_eq_le(cand - 1)
        return jnp.where(cnt >= need, lo, cand)

    return lax.fori_loop(0, nbits, body, jnp.zeros(shape, I32))


def _dsa_prompt_kernel(q_ref, qi_ref, misc_ref, k_ref, vt_ref, kx_ref, o_ref,
                       key_scr, qsel_scr, qg_scr, thr_scr, lim_scr, m_scr, l_scr, acc_scr, *, n_sel):
    i = pl.program_id(1)
    tq = Q_TILE
    ck = KEY_CHUNK
    n_ch = (i * tq + tq + ck - 1) // ck
    q_pos = i * tq + lax.broadcasted_iota(I32, (1, tq), 1)
    row_k = lax.broadcasted_iota(I32, (ck, 1), 0)

    lo_half = lax.broadcasted_iota(I32, (tq, LANES), 1) < IDX_DIM
    zero = jnp.zeros((), BF16)
    for p in range(IDX_WIDTH // LANES):
        slab = qi_ref[:, p * LANES:(p + 1) * LANES]
        qsel_scr[(2 * p) * tq:(2 * p + 1) * tq, :] = jnp.where(lo_half, slab, zero)
        qsel_scr[(2 * p + 1) * tq:(2 * p + 2) * tq, :] = jnp.where(lo_half, zero, slab)
    w_t = misc_ref[...].T

    def score_chunk(c, carry):
        off = pl.multiple_of(c * ck, ck)
        s = _dot_nt(kx_ref[pl.ds(off, ck), :], qsel_scr[...])
        acc = jnp.zeros((ck, tq), F32)
        for h in range(N_IDX_HEADS):
            acc = acc + w_t[M_WI + h:M_WI + h + 1, :] * jnp.maximum(s[:, h * tq:(h + 1) * tq], 0.0)
        acc = jnp.where(off + row_k <= q_pos, acc, -jnp.inf)
        key_scr[pl.ds(off, ck), :] = _sort_key(acc)
        return carry

    lax.fori_loop(0, n_ch, score_chunk, 0)

    spc = SEL_SPAN // ck
    n_span = (n_ch + spc - 1) // spc
    neg_key = jnp.full((ck, tq), NEG_INF_KEY, I32)

    def pad_chunk(c, carry):
        key_scr[pl.ds(pl.multiple_of(c * ck, ck), ck), :] = neg_key
        return carry

    lax.fori_loop(n_ch, n_span * spc, pad_chunk, 0)

    thr_scr[...] = jnp.full((1, tq), INT_MIN, I32)
    lim_scr[...] = jnp.full((1, tq), INT_MAX, I32)

    def select_threshold(n_keys):
        def count_where(pred):
            tot = jnp.zeros((SUBLANES, tq), F32)
            for c0 in range(0, n_keys, ck):
                hit = pred(key_scr[c0:c0 + ck, :], c0 + row_k).astype(F32)
                tot = tot + jnp.sum(hit.reshape(ck // SUBLANES, SUBLANES, tq), axis=0)
            return jnp.sum(tot, axis=0, keepdims=True)

        t, n_ge = _kth_largest_key(lambda cand: count_where(lambda kk, pos: kk >= cand), float(n_sel), (1, tq),
                                   n_keys)
        thr_scr[...] = t

        @pl.when(jnp.max(n_ge) > float(n_sel))
        def _():
            n_gt = count_where(lambda kk, pos: kk > t)
            lim_scr[...] = _tie_index_limit(
                lambda idx: count_where(lambda kk, pos: jnp.logical_and(kk == t, pos <= idx)),
                float(n_sel) - n_gt, k_ref.shape[0], (1, tq))

    for spans in range(1, k_ref.shape[0] // SEL_SPAN + 1):
        if spans * SEL_SPAN > n_sel:
            pl.when(jnp.logical_and(n_span == spans, (i + 1) * tq > n_sel))(
                functools.partial(select_threshold, spans * SEL_SPAN))

    thr = thr_scr[...]
    lim = lim_scr[...]

    for g in range(N_KV_HEADS):
        for r in range(KV_GROUP):
            h = g * KV_GROUP + r
            qg_scr[g, r * tq:(r + 1) * tq, :] = q_ref[:, h * HEAD_DIM:(h + 1) * HEAD_DIM]
    m_scr[...] = jnp.full(m_scr.shape, NEG_BIG, F32)
    l_scr[...] = jnp.zeros(l_scr.shape, F32)
    acc_scr[...] = jnp.zeros(acc_scr.shape, F32)

    def attend_chunk(c, carry):
        off = pl.multiple_of(c * ck, ck)
        kk = key_scr[pl.ds(off, ck), :]
        pos = off + row_k
        sel = jnp.logical_or(kk > thr, jnp.logical_and(kk == thr, pos <= lim))
        sel = jnp.logical_and(sel, pos <= q_pos)
        def group_steps(g):
            kc = k_ref[pl.ds(off, ck), g * HEAD_DIM:(g + 1) * HEAD_DIM]
            vt = vt_ref[g * HEAD_DIM:(g + 1) * HEAD_DIM, pl.ds(off, ck)]
            qk = _dot_nt(kc, qg_scr[g])
            yield
            s = jnp.concatenate([jnp.where(sel, qk[:, r * tq:(r + 1) * tq], NEG_BIG) for r in range(KV_GROUP)],
                                axis=1)
            m_old = m_scr[g]
            m_new = jnp.maximum(m_old, jnp.max(s, axis=0, keepdims=True))
            yield
            p = jnp.exp(s - m_new)
            alpha = jnp.exp(m_old - m_new)
            l_scr[g] = alpha * l_scr[g] + jnp.sum(p, axis=0, keepdims=True)
            yield
            acc_scr[g] = alpha * acc_scr[g] + _dot(vt, p.astype(BF16))
            m_scr[g] = m_new

        chains = [group_steps(g) for g in range(N_KV_HEADS)]
        while chains:
            chains = [ch for ch in chains if next(ch, "done") != "done"]
        return carry

    lax.fori_loop(0, n_ch, attend_chunk, 0)
    for g in range(N_KV_HEADS):
        o_t = acc_scr[g] / l_scr[g]
        for r in range(KV_GROUP):
            h = g * KV_GROUP + r
            o_ref[:, h * HEAD_DIM:(h + 1) * HEAD_DIM] = o_t[:, r * tq:(r + 1) * tq].T.astype(BF16)


def _dsa_prompt(q_bf, qi_bf, misc, k_bf, vt_bf, kx_bf, b, t):
    n = b * t
    nq = t // Q_TILE
    n_sel = min(TOPK_MAX, t // 4)
    qrow = lambda w: pl.BlockSpec((Q_TILE, w), lambda bb, i: (bb * nq + i, 0))
    seq = lambda w: pl.BlockSpec((t, w), lambda bb, i: (bb, 0))
    return pl.pallas_call(
        functools.partial(_dsa_prompt_kernel, n_sel=n_sel),
        name="dsa_prompt",
        grid=(b, nq),
        in_specs=[qrow(ATTN_WIDTH), qrow(IDX_WIDTH),
                  qrow(LANES),
                  seq(KV_WIDTH), pl.BlockSpec((KV_WIDTH, t), lambda bb, i: (0, bb)), seq(LANES)],
        out_specs=qrow(ATTN_WIDTH),
        out_shape=jax.ShapeDtypeStruct((n, ATTN_WIDTH), BF16),
        scratch_shapes=[pltpu.VMEM((t, Q_TILE), I32),
                        pltpu.VMEM((N_IDX_HEADS * Q_TILE, LANES), BF16),
                        pltpu.VMEM((N_KV_HEADS, KV_GROUP * Q_TILE, HEAD_DIM), BF16),
                        pltpu.VMEM((1, Q_TILE), I32),
                        pltpu.VMEM((1, Q_TILE), I32),
                        pltpu.VMEM((N_KV_HEADS, 1, KV_GROUP * Q_TILE), F32),
                        pltpu.VMEM((N_KV_HEADS, 1, KV_GROUP * Q_TILE), F32),
                        pltpu.VMEM((N_KV_HEADS, HEAD_DIM, KV_GROUP * Q_TILE), F32)],
        compiler_params=_cparams(("parallel", "arbitrary")),
    )(q_bf, qi_bf, misc, k_bf, vt_bf, kx_bf)


def _sample_score_kernel(pt_ref, q_ref, w_ref, kn_ref, *refs, pages, t_valid):
    page_refs = refs[:pages]
    past_ref, new_ref = refs[pages], refs[pages + 1]
    rows = SAMPLE_ROWS
    hr = N_IDX_HEADS * rows
    half = PAGE_SIZE // 2
    q2 = q_ref[0]
    w = w_ref[0]

    def head_sum(s):
        s = w * jnp.maximum(s, 0.0)
        acc = s[0:rows]
        for h in range(1, N_IDX_HEADS):
            acc = acc + s[h * rows:(h + 1) * rows]
        return acc

    for j in range(pages):
        s = _dot_nt(q2, page_refs[j][0, 0].astype(BF16))
        past_ref[0, :, j * PAGE_SIZE:j * PAGE_SIZE + half] = _sort_key(head_sum(s[0:hr]))
        past_ref[0, :, j * PAGE_SIZE + half:(j + 1) * PAGE_SIZE] = _sort_key(head_sum(s[hr:2 * hr]))

    @pl.when(pl.program_id(1) == 0)
    def _():
        kn = jnp.concatenate([kn_ref[...], jnp.zeros((LANES - rows, IDX_DIM), F32)], axis=0).astype(BF16)
        sc = head_sum(_dot_nt(q2[0:hr, 0:IDX_DIM], kn))
        t = lax.broadcasted_iota(I32, sc.shape, 0)
        s = lax.broadcasted_iota(I32, sc.shape, 1)
        ok = jnp.logical_and(s <= t, s < t_valid)
        new_ref[0] = _sort_key(jnp.where(ok, sc, -jnp.inf))


def _page_order_pos(lane_pos):
    p = lane_pos & (PAGE_SIZE - 1)
    half = PAGE_SIZE // 2
    return (lane_pos - p) + 2 * (p & (half - 1)) + (p // half)


def _sample_scores(page_table, q_t, w_col, kif, cache_idx, layer, pages, t_valid):
    bs, n_pages = page_table.shape
    past = n_pages * PAGE_SIZE
    hr = N_IDX_HEADS * SAMPLE_ROWS
    depth, pool = cache_idx.shape[0], cache_idx.shape[1]
    cache_idx = cache_idx.reshape(depth, pool, PAGE_SIZE // 2, 2 * IDX_DIM)
    zq = jnp.zeros_like(q_t)
    q_t = jnp.concatenate([jnp.concatenate([q_t, zq], axis=2), jnp.concatenate([zq, q_t], axis=2)], axis=1)
    page_spec = lambda j: pl.BlockSpec((1, 1, PAGE_SIZE // 2, 2 * IDX_DIM),
                                       lambda b, c, pt: (layer, pt[b, c * pages + j], 0, 0))
    grid_spec = pltpu.PrefetchScalarGridSpec(
        num_scalar_prefetch=1,
        grid=(bs, n_pages // pages),
        in_specs=[pl.BlockSpec((1, 2 * hr, 2 * IDX_DIM), lambda b, c, pt: (b, 0, 0)),
                  pl.BlockSpec((1, hr, 1), lambda b, c, pt: (b, 0, 0)),
                  pl.BlockSpec((SAMPLE_ROWS, IDX_DIM), lambda b, c, pt: (b, 0))]
                 + [page_spec(j) for j in range(pages)],
        out_specs=[pl.BlockSpec((1, SAMPLE_ROWS, pages * PAGE_SIZE), lambda b, c, pt: (b, 0, c)),
                   pl.BlockSpec((1, SAMPLE_ROWS, LANES), lambda b, c, pt: (b, 0, 0))],
    )
    return pl.pallas_call(
        functools.partial(_sample_score_kernel, pages=pages, t_valid=t_valid),
        name="sample_scores",
        grid_spec=grid_spec,
        out_shape=[jax.ShapeDtypeStruct((bs, SAMPLE_ROWS, past), I32),
                   jax.ShapeDtypeStruct((bs, SAMPLE_ROWS, LANES), I32)],
        compiler_params=_cparams(("parallel", "arbitrary")),
    )(page_table, q_t, w_col, kif, *([cache_idx] * pages))


def _sample_attend_kernel(pt_ref, kp_ref, kn_ref, q_ref, knew_ref, vnew_ref, *refs, pages, n_sel, past):
    k_pages = refs[:pages]
    v_pages = refs[pages:2 * pages]
    o_ref = refs[2 * pages]
    thr_scr, lim_scr, m_scr, l_scr, acc_scr = refs[2 * pages + 1:]
    c = pl.program_id(1)
    rows = SAMPLE_ROWS
    span = pages * PAGE_SIZE

    @pl.when(c == 0)
    def _():
        m_scr[...] = jnp.full(m_scr.shape, NEG_BIG, F32)
        l_scr[...] = jnp.zeros(l_scr.shape, F32)
        acc_scr[...] = jnp.zeros(acc_scr.shape, F32)
        kp = kp_ref[0]
        kn = kn_ref[0]
        pos_p = _page_order_pos(lax.broadcasted_iota(I32, kp.shape, 1))
        pos_n = past + lax.broadcasted_iota(I32, kn.shape, 1)

        def count_where(pred):
            return (jnp.sum(pred(kp, pos_p).astype(F32), axis=1, keepdims=True)
                    + jnp.sum(pred(kn, pos_n).astype(F32), axis=1, keepdims=True))

        t, n_ge = _kth_largest_key(lambda cand: count_where(lambda kk, pos: kk >= cand), float(n_sel), (rows, 1),
                                   past + LANES)
        thr_scr[...] = t
        lim_scr[...] = jnp.full((rows, 1), INT_MAX, I32)

        @pl.when(jnp.max(n_ge) > float(n_sel))
        def _():
            n_gt = count_where(lambda kk, pos: kk > t)
            lim_scr[...] = _tie_index_limit(
                lambda idx: count_where(lambda kk, pos: jnp.logical_and(kk == t, pos <= idx)),
                float(n_sel) - n_gt, past + LANES, (rows, 1))

    thr = thr_scr[...]
    lim = lim_scr[...]

    def update_steps(g, qg, k_fn, v_fn, sel):
        sel = jnp.concatenate([sel] * KV_GROUP, axis=0)
        qk = _dot_nt(qg, k_fn())
        yield
        s = jnp.where(sel, qk, NEG_BIG)
        m_old = m_scr[g]
        m_new = jnp.maximum(m_old, jnp.max(s, axis=1, keepdims=True))
        yield
        p = jnp.where(sel, jnp.exp(s - m_new), 0.0)
        alpha = jnp.exp(m_old - m_new)
        l_scr[g] = alpha * l_scr[g] + jnp.sum(p, axis=1, keepdims=True)
        yield
        acc_scr[g] = alpha * acc_scr[g] + _dot(p.astype(BF16), v_fn())
        m_scr[g] = m_new

    def run_lockstep(chains):
        while chains:
            chains = [ch for ch in chains if next(ch, "done") != "done"]

    def select(kk, pos):
        return jnp.logical_or(kk > thr, jnp.logical_and(kk == thr, pos <= lim))

    def page_cat(page_refs, g):
        half = PAGE_SIZE // 2
        parts = [r[0, 0, pl.ds(par * N_KV_HEADS + g, half, stride=2 * N_KV_HEADS), :]
                 for r in page_refs for par in range(2)]
        return jnp.concatenate(parts, axis=0).astype(BF16)

    off = pl.multiple_of(c * span, span)
    kk = kp_ref[0, :, pl.ds(off, span)]
    sel_past = select(kk, _page_order_pos(off + lax.broadcasted_iota(I32, kk.shape, 1)))
    q_groups = []
    for g in range(N_KV_HEADS):
        qg = jnp.concatenate(
            [q_ref[:, (g * KV_GROUP + r) * HEAD_DIM:(g * KV_GROUP + r + 1) * HEAD_DIM] for r in range(KV_GROUP)],
            axis=0)
        q_groups.append(qg)
    run_lockstep([update_steps(g, q_groups[g], functools.partial(page_cat, k_pages, g),
                               functools.partial(page_cat, v_pages, g), sel_past) for g in range(N_KV_HEADS)])

    @pl.when(c == pl.num_programs(1) - 1)
    def _():
        kn = kn_ref[0]
        lane = lax.broadcasted_iota(I32, kn.shape, 1)
        sel_new = jnp.logical_and(select(kn, past + lane), lane < rows)
        pad = jnp.zeros((LANES - rows, KV_WIDTH), BF16)
        k_new = jnp.concatenate([knew_ref[...], pad], axis=0)
        v_new = jnp.concatenate([vnew_ref[...], pad], axis=0)
        head = lambda a, g: (lambda: a[:, g * HEAD_DIM:(g + 1) * HEAD_DIM])
        run_lockstep([update_steps(g, q_groups[g], head(k_new, g), head(v_new, g), sel_new)
                      for g in range(N_KV_HEADS)])
        for g in range(N_KV_HEADS):
            o = acc_scr[g] / l_scr[g]
            for r in range(KV_GROUP):
                h = g * KV_GROUP + r
                o_ref[:, h * HEAD_DIM:(h + 1) * HEAD_DIM] = o[r * rows:(r + 1) * rows].astype(BF16)


def _sample_attend(page_table, keys_past, keys_new, q_bf, k_bf, v_bf, cache_k, cache_v, layer, pages, n_sel):
    bs, n_pages = page_table.shape
    past = n_pages * PAGE_SIZE
    depth, pool = cache_k.shape[0], cache_k.shape[1]
    cache_k = cache_k.reshape(depth, pool, PAGE_SIZE * N_KV_HEADS, HEAD_DIM)
    cache_v = cache_v.reshape(depth, pool, PAGE_SIZE * N_KV_HEADS, HEAD_DIM)
    page_spec = lambda j: pl.BlockSpec((1, 1, PAGE_SIZE * N_KV_HEADS, HEAD_DIM),
                                       lambda b, c, pt: (layer, pt[b, c * pages + j], 0, 0))
    row = lambda w: pl.BlockSpec((SAMPLE_ROWS, w), lambda b, c, pt: (b, 0))
    grid_spec = pltpu.PrefetchScalarGridSpec(
        num_scalar_prefetch=1,
        grid=(bs, n_pages // pages),
        in_specs=[pl.BlockSpec((1, SAMPLE_ROWS, past), lambda b, c, pt: (b, 0, 0)),
                  pl.BlockSpec((1, SAMPLE_ROWS, LANES), lambda b, c, pt: (b, 0, 0)),
                  row(ATTN_WIDTH), row(KV_WIDTH), row(KV_WIDTH)]
                 + [page_spec(j) for j in range(pages)] * 2,
        out_specs=row(ATTN_WIDTH),
        scratch_shapes=[pltpu.VMEM((SAMPLE_ROWS, 1), I32),
                        pltpu.VMEM((SAMPLE_ROWS, 1), I32),
                        pltpu.VMEM((N_KV_HEADS, KV_GROUP * SAMPLE_ROWS, 1), F32),
                        pltpu.VMEM((N_KV_HEADS, KV_GROUP * SAMPLE_ROWS, 1), F32),
                        pltpu.VMEM((N_KV_HEADS, KV_GROUP * SAMPLE_ROWS, HEAD_DIM), F32)],
    )
    return pl.pallas_call(
        functools.partial(_sample_attend_kernel, pages=pages, n_sel=n_sel, past=past),
        name="sample_attend",
        grid_spec=grid_spec,
        out_shape=jax.ShapeDtypeStruct((bs * SAMPLE_ROWS, ATTN_WIDTH), BF16),
        compiler_params=_cparams(("parallel", "arbitrary")),
    )(page_table, keys_past, keys_new, q_bf, k_bf, v_bf, *([cache_k] * pages), *([cache_v] * pages))


def _delta_prep_kernel(x_ref, halo_ref, prev_ref, misc_ref, cw_ref, al_ref, dt_ref,
                       qn_ref, kn_ref, vv_ref, bg_ref, xp_scr, *, tiles_per_seq, t_valid, tt):
    i = pl.program_id(0)
    tile_in_seq = i % tiles_per_seq
    halo = jnp.where(tile_in_seq == 0, prev_ref[0], halo_ref[...])
    xp_scr[0:SUBLANES, :] = halo
    xp_scr[SUBLANES:SUBLANES + tt, :] = x_ref[...]
    base = SUBLANES - (CONV_WIDTH - 1)
    outs = (qn_ref, kn_ref, vv_ref)
    for sec in range(3):
        for h in range(N_DELTA_HEADS):
            col = sec * DELTA_WIDTH + h * HEAD_DIM
            sl = slice(col, col + HEAD_DIM)
            y = xp_scr[base:base + tt, sl] * cw_ref[0:1, sl]
            for j in range(1, CONV_WIDTH):
                y = y + xp_scr[base + j:base + j + tt, sl] * cw_ref[j:j + 1, sl]
            y = _silu(y)
            if sec < 2:
                y = y * lax.rsqrt(jnp.sum(y * y, axis=-1, keepdims=True) + EPS)
            if sec == 0:
                y = y * (HEAD_DIM ** -0.5)
            outs[sec][:, h * HEAD_DIM:(h + 1) * HEAD_DIM] = y
    m = misc_ref[...]
    lane = lax.broadcasted_iota(I32, m.shape, 1)
    row = tile_in_seq * tt + lax.broadcasted_iota(I32, m.shape, 0)
    beta = _sigmoid(m)
    g = -jnp.exp(al_ref[...]) * _softplus(m + dt_ref[...])
    is_b = jnp.logical_and(lane >= M_BD, lane < M_BD + N_DELTA_HEADS)
    is_g = jnp.logical_and(lane >= M_AD, lane < M_AD + N_DELTA_HEADS)
    comb = jnp.where(is_b, beta, jnp.where(is_g, g, 0.0))
    comb = jnp.where(row < t_valid, comb, 0.0)
    bg_ref[...] = pltpu.roll(comb, LANES - M_BD, 1)


def _delta_prep(proj, prev8, conv_w, a_log, dt_bias, b, t, tt, t_valid):
    n = proj.shape[0]
    tiles_per_seq = t // tt
    pad_vec = lambda v: jnp.zeros((1, LANES), F32).at[0, M_AD:M_AD + N_DELTA_HEADS].set(v)
    halo_blocks = tt // SUBLANES
    return pl.pallas_call(
        functools.partial(_delta_prep_kernel, tiles_per_seq=tiles_per_seq, t_valid=t_valid, tt=tt),
        name="delta_prep",
        grid=(n // tt,),
        in_specs=[pl.BlockSpec((tt, CONV_CHANNELS), lambda i: (i, C_CONV // CONV_CHANNELS)),
                  pl.BlockSpec((SUBLANES, CONV_CHANNELS),
                               lambda i: (jnp.maximum(i * halo_blocks - 1, 0), C_CONV // CONV_CHANNELS)),
                  pl.BlockSpec((1, SUBLANES, CONV_CHANNELS), lambda i: (i // tiles_per_seq, 0, 0)),
                  pl.BlockSpec((tt, LANES), lambda i: (i, C_MISC // LANES)),
                  pl.BlockSpec((CONV_WIDTH, CONV_CHANNELS), lambda i: (0, 0)),
                  pl.BlockSpec((1, LANES), lambda i: (0, 0)),
                  pl.BlockSpec((1, LANES), lambda i: (0, 0))],
        out_specs=[pl.BlockSpec((tt, DELTA_WIDTH), lambda i: (i, 0))] * 3
                  + [pl.BlockSpec((tt, LANES), lambda i: (i, 0))],
        out_shape=[jax.ShapeDtypeStruct((n, DELTA_WIDTH), F32)] * 3 + [jax.ShapeDtypeStruct((n, LANES), F32)],
        scratch_shapes=[pltpu.VMEM((SUBLANES + tt, CONV_CHANNELS), F32)],
        compiler_params=_cparams(("parallel",)),
    )(proj, proj, prev8, proj, conv_w, pad_vec(a_log), pad_vec(dt_bias))


def _mm(a, b):
    return _dot(a.astype(BF16), b.astype(BF16))


def _mm_nt(a, b):
    return _dot_nt(a.astype(BF16), b.astype(BF16))


DELTA_INV_BLOCK = 16
DELTA_STACK = 4
DELTA_CHUNKS_PER_STEP = 2


def _delta_chunk_kernel(qn_ref, kn_ref, vv_ref, bg_ref, z_ref, s0_ref, ow_ref, od_ref, so_ref, s_scr, *, n_chunks):
    c = pl.program_id(1)
    cs = DELTA_CHUNK

    @pl.when(c == 0)
    def _():
        s_scr[...] = s0_ref[0]

    ltri = (lax.broadcasted_iota(I32, (cs, cs), 0) >= lax.broadcasted_iota(I32, (cs, cs), 1)).astype(BF16)

    def chunk_gates(ch):
        bg = bg_ref[ch * cs:(ch + 1) * cs, :]
        g1 = bg.astype(BF16)
        r1 = bg - g1.astype(F32)
        g2 = r1.astype(BF16)
        g3 = (r1 - g2.astype(F32)).astype(BF16)
        gc = _dot(ltri, g1) + _dot(ltri, g2) + _dot(ltri, g3)
        return bg, gc, gc.T

    gates = [chunk_gates(ch) for ch in range(n_chunks)]
    state_ready = {}

    gh = DELTA_STACK
    rows = gh * cs
    rr = lax.broadcasted_iota(I32, (rows, rows), 0)
    cc = lax.broadcasted_iota(I32, (rows, rows), 1)
    same = (rr // cs) == (cc // cs)
    causal = jnp.logical_and(same, rr >= cc)
    strict = jnp.logical_and(same, rr > cc)
    eye = (rr == cc).astype(F32)
    row_head = lax.broadcasted_iota(I32, (rows, 1), 0) // cs
    def group_steps(ch, grp):
        heads = [grp * gh + j for j in range(gh)]
        bg, gc, gct = gates[ch]
        r0 = ch * cs
        stack = lambda ref: jnp.concatenate([ref[r0:r0 + cs, h * HEAD_DIM:(h + 1) * HEAD_DIM] for h in heads],
                                            axis=0)
        col = lambda a, lane0: jnp.concatenate([a[:, lane0 + h:lane0 + h + 1] for h in heads], axis=0)
        k = stack(kn_ref)
        q = stack(qn_ref)
        v = stack(vv_ref)
        bcol = col(bg, 0)
        gcc = col(gc, N_DELTA_HEADS)
        gcr = jnp.concatenate([gct[N_DELTA_HEADS + h:N_DELTA_HEADS + h + 1, :] for h in heads], axis=1)
        g_last = [gc[cs - 1:cs, N_DELTA_HEADS + h:N_DELTA_HEADS + h + 1] for h in heads]
        glc = jnp.concatenate([jnp.broadcast_to(gl, (cs, 1)) for gl in g_last], axis=0)
        decay = jnp.exp(jnp.where(causal, gcc - gcr, -jnp.inf))
        kb = k * bcol
        eg = jnp.exp(gcc)
        kq = _mm_nt(jnp.concatenate([kb, q], axis=0), k)
        yield
        a = jnp.where(strict, kq[0:rows] * decay, 0.0)
        intra = jnp.where(causal, kq[rows:2 * rows] * decay, 0.0)
        x = -a
        nb = DELTA_INV_BLOCK
        y = jnp.where((rr // nb) == (cc // nb), x, 0.0)
        p = eye + y
        y = _mm(y, y)
        yield
        n_sq = max(1, int(nb - 1).bit_length())
        for lvl in range(1, n_sq):
            if lvl < n_sq - 1:
                py = _mm(jnp.concatenate([p, y], axis=0), y)
                p = p + py[0:rows]
                y = py[rows:2 * rows]
            else:
                p = p + _mm(p, y)
            yield
        size = 2 * nb
        while size <= cs:
            off = jnp.where(jnp.logical_and((rr // size) == (cc // size), (rr // (size // 2)) != (cc // (size // 2))),
                            x, 0.0)
            po = _mm(p, off)
            yield
            p = p + _mm(po, p)
            yield
            size *= 2
        sol = _mm(p, jnp.concatenate([v * bcol, kb * eg], axis=1))
        yield
        u = sol[:, 0:HEAD_DIM]
        w = sol[:, HEAD_DIM:2 * HEAD_DIM]
        lanes_g = slice(grp * gh * HEAD_DIM, (grp + 1) * gh * HEAD_DIM)
        while ch > 0 and not state_ready.get((ch - 1, grp)):
            yield
        s_g = s_scr[:, lanes_g]
        wq_s = _mm(jnp.concatenate([w, q * eg], axis=0), s_g)
        yield
        own = lambda m, r0: jnp.concatenate(
            [m[r0 + j * cs:r0 + (j + 1) * cs, j * HEAD_DIM:(j + 1) * HEAD_DIM] for j in range(gh)], axis=0)
        v_new = u - own(wq_s, 0)
        o = own(wq_s, rows) + _mm(intra, v_new)
        yield
        kg_t = (k * jnp.exp(glc - gcc)).T
        vn_blocks = jnp.concatenate([jnp.where(row_head == j, v_new, 0.0) for j in range(gh)], axis=1)
        s_decay = jnp.concatenate([jnp.broadcast_to(jnp.exp(gl), (1, HEAD_DIM)) for gl in g_last], axis=1)
        s_scr[:, lanes_g] = s_g * s_decay + _mm(kg_t, vn_blocks)
        state_ready[(ch, grp)] = True
        yield
        on = o * lax.rsqrt(jnp.mean(o * o, axis=-1, keepdims=True) + EPS) * ow_ref[...]
        for j, h in enumerate(heads):
            sl = slice(h * HEAD_DIM, (h + 1) * HEAD_DIM)
            od_ref[r0:r0 + cs, sl] = (on[j * cs:(j + 1) * cs] * _silu(z_ref[r0:r0 + cs, sl])).astype(BF16)

    chains = [group_steps(ch, grp) for ch in range(n_chunks) for grp in range(N_DELTA_HEADS // gh)]
    while chains:
        chains = [g for g in chains if next(g, "done") != "done"]

    so_ref[0] = s_scr[...]


def _delta_chunks(qn, kn, vv, bg, zsrc, z_col_block, state0, o_norm_w, b, t):
    n = b * t
    per_step = DELTA_CHUNKS_PER_STEP if (t // DELTA_CHUNK) % DELTA_CHUNKS_PER_STEP == 0 else 1
    nc = t // (DELTA_CHUNK * per_step)
    sw = N_DELTA_HEADS * HEAD_DIM
    row = lambda w, cb=0: pl.BlockSpec((DELTA_CHUNK * per_step, w), lambda bb, c: (bb * nc + c, cb))
    st = pl.BlockSpec((1, HEAD_DIM, sw), lambda bb, c: (bb, 0, 0))
    s_in = state0.transpose(0, 2, 1, 3).reshape(b, HEAD_DIM, sw)
    od, s_out = pl.pallas_call(
        functools.partial(_delta_chunk_kernel, n_chunks=per_step),
        name="delta_chunks",
        grid=(b, nc),
        in_specs=[row(DELTA_WIDTH), row(DELTA_WIDTH), row(DELTA_WIDTH), row(LANES),
                  row(DELTA_WIDTH, z_col_block), st, pl.BlockSpec((1, LANES), lambda bb, c: (0, 0))],
        out_specs=[row(DELTA_WIDTH), st],
        out_shape=[jax.ShapeDtypeStruct((n, DELTA_WIDTH), BF16),
                   jax.ShapeDtypeStruct((b, HEAD_DIM, sw), F32)],
        scratch_shapes=[pltpu.VMEM((HEAD_DIM, sw), F32)],
        compiler_params=_cparams(("parallel", "arbitrary")),
    )(qn, kn, vv, bg, zsrc, s_in, o_norm_w.reshape(1, LANES))
    return od, s_out.reshape(b, HEAD_DIM, N_DELTA_HEADS, HEAD_DIM).transpose(0, 2, 1, 3)


def _outproj_kernel(oap_ref, odp_ref, xp_ref, g1p_ref, shp_ref, scp_ref,
                    oas_ref, ods_ref, xs_ref, g1s_ref, shs_ref, scs_ref,
                    nw_ref, wo_ref, wrh_ref, br_ref, x1_ref, h2_ref, lg_ref, *, prompt_tiles):
    def body(oa_ref, od_ref, x_ref, g1_ref, sh_ref, sc_ref):
        mix = (_dot(oa_ref[...], wo_ref[0:ATTN_WIDTH, :])
               + _dot(od_ref[...], wo_ref[ATTN_WIDTH:ATTN_WIDTH + DELTA_WIDTH, :]))
        x1 = x_ref[...] + g1_ref[0] * mix
        x1_ref[...] = x1
        y = x1 * lax.rsqrt(jnp.mean(x1 * x1, axis=-1, keepdims=True) + EPS) * nw_ref[...]
        h2 = y * (1.0 + sc_ref[0]) + sh_ref[0]
        h2_ref[...] = h2
        hb = h2.astype(BF16)
        lo = (h2 - hb.astype(F32)).astype(BF16)
        wr = wrh_ref[...]
        both = _dot(hb, wr)
        lg_ref[...] = both[:, 0:LANES] + both[:, LANES:2 * LANES] + _dot(lo, wr[:, 0:LANES]) + br_ref[...]

    i = pl.program_id(0)
    pl.when(i < prompt_tiles)(functools.partial(body, oap_ref, odp_ref, xp_ref, g1p_ref, shp_ref, scp_ref))
    pl.when(i >= prompt_tiles)(functools.partial(body, oas_ref, ods_ref, xs_ref, g1s_ref, shs_ref, scs_ref))


def _out_projection(prompt, sample, norm2_w, w_out_bf, wr_both, b_rt, tm, prompt_seq_len):
    n_p, d = prompt[2].shape
    n_s = sample[2].shape[0]
    pt = n_p // tm
    tiles_per_seq = prompt_seq_len // tm
    p_row = lambda w: pl.BlockSpec((tm, w), lambda i: (jnp.minimum(i, pt - 1), 0))
    s_row = lambda w: pl.BlockSpec((tm, w), lambda i: (jnp.maximum(i - pt, 0), 0))
    p_mod = pl.BlockSpec((1, 1, d), lambda i: (jnp.minimum(i, pt - 1) // tiles_per_seq, 0, 0))
    s_mod = pl.BlockSpec((1, tm, d), lambda i: (jnp.maximum(i - pt, 0), 0, 0))
    row = lambda w: pl.BlockSpec((tm, w), lambda i: (i, 0))
    full = lambda a: pl.BlockSpec(a.shape, lambda i: (0, 0))
    n = n_p + n_s
    return pl.pallas_call(
        functools.partial(_outproj_kernel, prompt_tiles=pt),
        name="out_proj",
        grid=(n // tm,),
        in_specs=[p_row(ATTN_WIDTH), p_row(DELTA_WIDTH), p_row(d), p_mod, p_mod, p_mod,
                  s_row(ATTN_WIDTH), s_row(DELTA_WIDTH), s_row(d), s_mod, s_mod, s_mod,
                  pl.BlockSpec((1, d), lambda i: (0, 0)), full(w_out_bf), full(wr_both), full(b_rt)],
        out_specs=[row(d), row(d), row(LANES)],
        out_shape=[jax.ShapeDtypeStruct((n, d), F32), jax.ShapeDtypeStruct((n, d), F32),
                   jax.ShapeDtypeStruct((n, LANES), F32)],
        compiler_params=_cparams(("parallel",)),
    )(*prompt, *sample, norm2_w.reshape(1, d), w_out_bf, wr_both, b_rt)


def _route_kernel(lg_ref, eid_ref, gate_ref):
    x = lg_ref[...]
    lane = lax.broadcasted_iota(I32, x.shape, 1)
    gl = jnp.where(lane < N_GROUPS, x, -jnp.inf)
    ge = jnp.exp(gl - jnp.max(gl, axis=1, keepdims=True))
    p = ge / jnp.sum(ge, axis=1, keepdims=True)
    p_max = jnp.max(p, axis=1, keepdims=True)
    grp = jnp.min(jnp.where(p == p_max, lane, LANES), axis=1, keepdims=True)
    e_lane = lane - N_GROUPS
    in_grp = jnp.logical_and(jnp.logical_and(e_lane >= 0, e_lane < N_EXPERTS),
                             (e_lane >> 3) == grp)
    rl = jnp.where(in_grp, x, -jnp.inf)
    v1 = jnp.max(rl, axis=1, keepdims=True)
    i1 = jnp.min(jnp.where(rl == v1, lane, LANES), axis=1, keepdims=True)
    rl2 = jnp.where(lane == i1, -jnp.inf, rl)
    v2 = jnp.max(rl2, axis=1, keepdims=True)
    i2 = jnp.min(jnp.where(rl2 == v2, lane, LANES), axis=1, keepdims=True)
    t = jnp.exp(v2 - v1)
    den = 1.0 + t
    eid_ref[...] = jnp.where(lane == 0, i1 - N_GROUPS, jnp.where(lane == 1, i2 - N_GROUPS, 0))
    gate_ref[...] = jnp.where(lane == 0, (1.0 / den) * p_max, jnp.where(lane == 1, (t / den) * p_max, 0.0))


def _route(logits, tm):
    n = logits.shape[0]
    spec = pl.BlockSpec((tm, LANES), lambda i: (i, 0))
    return pl.pallas_call(
        _route_kernel,
        name="route",
        grid=(n // tm,),
        in_specs=[spec],
        out_specs=[spec, spec],
        out_shape=[jax.ShapeDtypeStruct((n, LANES), I32), jax.ShapeDtypeStruct((n, LANES), F32)],
        compiler_params=_cparams(("parallel",)),
    )(logits)


def _row_gather(idx_ref, base, n_rows, src_hbm, dst, sem):
    def body(r, carry):
        pltpu.make_async_copy(src_hbm.at[pl.ds(idx_ref[base + r], 1), :], dst.at[pl.ds(r, 1), :], sem).start()
        return carry

    lax.fori_loop(0, n_rows, body, 0, unroll=8)


def _row_gather_wait(n_rows, src_hbm, dst, sem):
    pltpu.make_async_copy(src_hbm.at[pl.ds(0, n_rows), :], dst, sem).wait()


def _moe_kernel(tok_ref, j0_ref, be_ref, na_ref, h_hbm, wg_ref, wu_ref, wd_ref, o_ref,
                x_even, x_odd, sem, wg_scr, wu_scr, wd_scr, *, bm):
    i = pl.program_id(0)
    n_act = na_ref[0]
    bufs = ((x_even, sem.at[0]), (x_odd, sem.at[1]))

    @pl.when(i == 0)
    def _():
        _row_gather(tok_ref, j0_ref[0], bm, h_hbm, x_even, sem.at[0])

    changed = jnp.logical_or(i == 0, be_ref[i] != be_ref[jnp.maximum(i - 1, 0)])

    @pl.when(jnp.logical_and(i < n_act, changed))
    def _():
        wg_scr[...] = wg_ref[0].astype(BF16)
        wu_scr[...] = wu_ref[0].astype(BF16)
        wd_scr[...] = wd_ref[0].astype(BF16)

    for parity in range(2):
        cur, cur_sem = bufs[parity]
        nxt, nxt_sem = bufs[1 - parity]

        @pl.when(jnp.logical_and(i < n_act, i % 2 == parity))
        def _():
            _row_gather_wait(bm, h_hbm, cur, cur_sem)
            base = j0_ref[i + 1]
            for r in range(bm):
                pltpu.make_async_copy(h_hbm.at[pl.ds(tok_ref[base + r], 1), :], nxt.at[pl.ds(r, 1), :],
                                      nxt_sem).start()
            x = cur[...].astype(BF16)
            hid = _silu(_dot(x, wg_scr[...])) * _dot(x, wu_scr[...])
            o_ref[...] = _dot(hid.astype(BF16), wd_scr[...])

        @pl.when(jnp.logical_and(i == n_act, i % 2 == parity))
        def _():
            _row_gather_wait(bm, h_hbm, cur, cur_sem)

    @pl.when(i >= n_act)
    def _():
        o_ref[...] = jnp.zeros(o_ref.shape, F32)


def _moe_experts(tok_sorted, block_j0, block_exp, n_active, h2, w_gate, w_up, w_down, bm):
    ns = block_exp.shape[0] * bm
    d = h2.shape[1]
    f = w_gate.shape[2]
    grid_spec = pltpu.PrefetchScalarGridSpec(
        num_scalar_prefetch=4,
        grid=(ns // bm,),
        in_specs=[pl.BlockSpec(memory_space=pl.ANY),
                  pl.BlockSpec((1, d, f), lambda i, tok, j0, be, na: (be[i], 0, 0)),
                  pl.BlockSpec((1, d, f), lambda i, tok, j0, be, na: (be[i], 0, 0)),
                  pl.BlockSpec((1, f, d), lambda i, tok, j0, be, na: (be[i], 0, 0))],
        out_specs=pl.BlockSpec((bm, d), lambda i, tok, j0, be, na: (i, 0)),
        scratch_shapes=[pltpu.VMEM((bm, d), F32), pltpu.VMEM((bm, d), F32), pltpu.SemaphoreType.DMA((2,)),
                        pltpu.VMEM((d, f), BF16), pltpu.VMEM((d, f), BF16), pltpu.VMEM((f, d), BF16)],
    )
    return pl.pallas_call(
        functools.partial(_moe_kernel, bm=bm),
        name="moe_experts",
        grid_spec=grid_spec,
        out_shape=jax.ShapeDtypeStruct((ns, d), F32),
        compiler_params=_cparams(("arbitrary",)),
    )(tok_sorted, block_j0, block_exp, n_active, h2, w_gate, w_up, w_down)


def _combine_kernel(dest_ref, x1_ref, gt_ref, g2p_ref, g2s_ref, y_hbm, op_ref, os_ref, y_buf, sem, *,
                    tm, prompt_tiles):
    i = pl.program_id(0)
    n = pl.num_programs(0)
    slot = i % 2

    @pl.when(i == 0)
    def _():
        _row_gather(dest_ref, 0, 2 * tm, y_hbm, y_buf.at[0], sem.at[0])

    for parity in range(2):
        @pl.when(jnp.logical_and(i + 1 < n, slot == parity))
        def _():
            base = (i + 1) * 2 * tm
            for r in range(2 * tm):
                pltpu.make_async_copy(y_hbm.at[pl.ds(dest_ref[base + r], 1), :],
                                      y_buf.at[1 - parity, pl.ds(r, 1), :], sem.at[1 - parity]).start()

    _row_gather_wait(2 * tm, y_hbm, y_buf.at[slot], sem.at[slot])
    gt = gt_ref[...]
    y = y_buf[slot, 0:tm, :] * gt[:, 0:1] + y_buf[slot, tm:2 * tm, :] * gt[:, 1:2]

    @pl.when(i < prompt_tiles)
    def _():
        op_ref[...] = x1_ref[...] + g2p_ref[0] * y

    @pl.when(i >= prompt_tiles)
    def _():
        os_ref[...] = x1_ref[...] + g2s_ref[0] * y


def _combine(x1, dest, y_rows, gates, gate2_p, gate2_s, tm, n_prompt, prompt_seq_len):
    n, d = x1.shape
    pt = n_prompt // tm
    tiles_per_seq = prompt_seq_len // tm
    dest_tiles = dest.reshape(n // tm, tm, 2).transpose(0, 2, 1).reshape(-1)
    row = lambda w: pl.BlockSpec((tm, w), lambda i, dst: (i, 0))
    grid_spec = pltpu.PrefetchScalarGridSpec(
        num_scalar_prefetch=1,
        grid=(n // tm,),
        in_specs=[row(d), row(LANES),
                  pl.BlockSpec((1, 1, d), lambda i, dst: (jnp.minimum(i, pt - 1) // tiles_per_seq, 0, 0)),
                  pl.BlockSpec((1, tm, d), lambda i, dst: (jnp.maximum(i - pt, 0), 0, 0)),
                  pl.BlockSpec(memory_space=pl.ANY)],
        out_specs=[pl.BlockSpec((tm, d), lambda i, dst: (jnp.minimum(i, pt - 1), 0)),
                   pl.BlockSpec((tm, d), lambda i, dst: (jnp.maximum(i - pt, 0), 0))],
        scratch_shapes=[pltpu.VMEM((2, 2 * tm, d), F32), pltpu.SemaphoreType.DMA((2,))],
    )
    return pl.pallas_call(
        functools.partial(_combine_kernel, tm=tm, prompt_tiles=pt),
        name="moe_combine",
        grid_spec=grid_spec,
        out_shape=[jax.ShapeDtypeStruct((n_prompt, d), F32), jax.ShapeDtypeStruct((n - n_prompt, d), F32)],
        compiler_params=_cparams(("arbitrary",)),
    )(dest_tiles, x1, gates, gate2_p, gate2_s, y_rows)


FRONT_SECTIONS = ((C_QA, ATTN_WIDTH), (C_QI, IDX_WIDTH),
                  (C_CONV, DELTA_WIDTH), (C_CONV + DELTA_WIDTH, DELTA_WIDTH), (C_CONV + 2 * DELTA_WIDTH, DELTA_WIDTH),
                  (C_KA, PROJ_PACKED - C_KA), (C_ZD, DELTA_WIDTH))


def _prompt_front_kernel(x_ref, sh_ref, sc_ref, nw_ref, w_ref, tabm_ref, tabi_ref, qw_ref, kw_ref, iw_ref,
                         prev_ref, cw_ref, al_ref, dt_ref,
                         q_ref, qi_ref, z_ref, qn_ref, kn_ref, vv_ref, kf_ref, kb_ref, vf_ref, vt_ref,
                         kif_ref, kx_ref, bg_ref, misc_ref, tail_ref,
                         carry_scr, xp_scr, *, tiles_per_seq, tm):
    i = pl.program_id(0)
    tile_in_seq = i % tiles_per_seq
    x = x_ref[...]
    y = x * lax.rsqrt(jnp.mean(x * x, axis=-1, keepdims=True) + EPS) * nw_ref[...]
    h = (y * (1.0 + sc_ref[0]) + sh_ref[0]).astype(BF16)
    tabm = tabm_ref[0]
    tabi = tabi_ref[0]
    half_main = HEAD_DIM // ROPE_FRACTION // 2
    half_idx = IDX_DIM // ROPE_FRACTION // 2

    @pl.when(tile_in_seq == 0)
    def _():
        carry_scr[...] = prev_ref[0]

    def queries(pj):
        for hh in range(N_ATTN_HEADS):
            sl = slice(hh * HEAD_DIM, (hh + 1) * HEAD_DIM)
            v = _rope(_rms_head(pj[:, sl], qw_ref[...]), tabm, half_main)
            q_ref[:, sl] = (v * (HEAD_DIM ** -0.5)).astype(BF16)

    def index_queries(pj):
        for p in range(IDX_WIDTH // LANES):
            sl = slice(p * LANES, (p + 1) * LANES)
            qi_ref[:, sl] = _rope(pj[:, sl], tabi, half_idx).astype(BF16)

    def gate_z(pj):
        z_ref[...] = pj

    def conv_section(sec, pj):
        cols = slice(sec * DELTA_WIDTH, (sec + 1) * DELTA_WIDTH)
        xp_scr[sec, 0:SUBLANES, :] = carry_scr[:, cols]
        xp_scr[sec, SUBLANES:SUBLANES + tm, :] = pj
        carry_scr[:, cols] = pj[tm - SUBLANES:tm, :]
        base = SUBLANES - (CONV_WIDTH - 1)
        out = (qn_ref, kn_ref, vv_ref)[sec]
        for hh in range(N_DELTA_HEADS):
            sl = slice(hh * HEAD_DIM, (hh + 1) * HEAD_DIM)
            wsl = slice(sec * DELTA_WIDTH + hh * HEAD_DIM, sec * DELTA_WIDTH + (hh + 1) * HEAD_DIM)
            v = xp_scr[sec, base:base + tm, sl] * cw_ref[0:1, wsl]
            for j in range(1, CONV_WIDTH):
                v = v + xp_scr[sec, base + j:base + j + tm, sl] * cw_ref[j:j + 1, wsl]
            v = _silu(v)
            if sec < 2:
                v = v * lax.rsqrt(jnp.sum(v * v, axis=-1, keepdims=True) + EPS)
            if sec == 0:
                v = v * (HEAD_DIM ** -0.5)
            out[:, sl] = v
        if sec == 2:
            tail_ref[0] = carry_scr[...]

    def keys_values_misc(pj):
        for hk in range(N_KV_HEADS):
            sl = slice(hk * HEAD_DIM, (hk + 1) * HEAD_DIM)
            v = _rope(_rms_head(pj[:, sl], kw_ref[...]), tabm, half_main)
            kf_ref[:, sl] = v
            kb_ref[:, sl] = v.astype(BF16)
        va = pj[:, KV_WIDTH:2 * KV_WIDTH]
        vf_ref[...] = va
        vt_ref[...] = va.T.astype(BF16)
        m = pj[:, 2 * KV_WIDTH:2 * KV_WIDTH + LANES]
        misc_ref[...] = m
        lane = lax.broadcasted_iota(I32, m.shape, 1)
        ki = jnp.where(lane < IDX_DIM, m, 0.0)
        ms = jnp.sum(ki * ki, axis=-1, keepdims=True) * (1.0 / IDX_DIM)
        v = _rope(ki * lax.rsqrt(ms + EPS) * iw_ref[...], tabi, half_idx)
        kif_ref[...] = v[:, 0:IDX_DIM]
        kx_ref[...] = (v + pltpu.roll(v, IDX_DIM, 1)).astype(BF16)
        beta = _sigmoid(m)
        g = -jnp.exp(al_ref[...]) * _softplus(m + dt_ref[...])
        is_b = jnp.logical_and(lane >= M_BD, lane < M_BD + N_DELTA_HEADS)
        is_g = jnp.logical_and(lane >= M_AD, lane < M_AD + N_DELTA_HEADS)
        comb = jnp.where(is_b, beta, jnp.where(is_g, g, 0.0))
        bg_ref[...] = pltpu.roll(comb, LANES - M_BD, 1)

    epilogues = (queries, index_queries,
                 functools.partial(conv_section, 0), functools.partial(conv_section, 1),
                 functools.partial(conv_section, 2), keys_values_misc, gate_z)
    project = lambda k: _dot(h, w_ref[:, FRONT_SECTIONS[k][0]:FRONT_SECTIONS[k][0] + FRONT_SECTIONS[k][1]])
    pj_next = project(0)
    for k, epilogue in enumerate(epilogues):
        pj = pj_next
        if k + 1 < len(epilogues):
            pj_next = project(k + 1)
        epilogue(pj)


def _prompt_front(x2d, shift, scale, norm_w, w_packed, pos, q_norm_w, k_norm_w, idx_k_norm_w, prev8, conv_w,
                  a_log, dt_bias, b, t, tm):
    n, d = x2d.shape
    np_ = w_packed.shape[1]
    tiles_per_seq = t // tm
    tabm = _rope_tables(pos, HEAD_DIM, LANES).reshape(tiles_per_seq, tm, 3 * LANES)
    tabi = _rope_tables(pos, IDX_DIM, IDX_DIM).reshape(tiles_per_seq, tm, 3 * LANES)
    iw = jnp.concatenate([idx_k_norm_w, jnp.zeros((LANES - IDX_DIM,), F32)]).reshape(1, LANES)
    pad_vec = lambda v: jnp.zeros((1, LANES), F32).at[0, M_AD:M_AD + N_DELTA_HEADS].set(v)
    mod_spec = pl.BlockSpec((1, 1, d), lambda i: (i // tiles_per_seq, 0, 0))
    tab_spec = pl.BlockSpec((1, tm, 3 * LANES), lambda i: (i % tiles_per_seq, 0, 0))
    vec = lambda w: pl.BlockSpec((1, w), lambda i: (0, 0))
    row = lambda w: pl.BlockSpec((tm, w), lambda i: (i, 0))
    seq_state = pl.BlockSpec((1, SUBLANES, CONV_CHANNELS), lambda i: (i // tiles_per_seq, 0, 0))
    widths = [(ATTN_WIDTH, BF16), (IDX_WIDTH, BF16), (DELTA_WIDTH, F32), (DELTA_WIDTH, F32), (DELTA_WIDTH, F32),
              (DELTA_WIDTH, F32), (KV_WIDTH, F32), (KV_WIDTH, BF16), (KV_WIDTH, F32)]
    outs = pl.pallas_call(
        functools.partial(_prompt_front_kernel, tiles_per_seq=tiles_per_seq, tm=tm),
        name="prompt_front",
        grid=(n // tm,),
        in_specs=[row(d), mod_spec, mod_spec, vec(d),
                  pl.BlockSpec((d, np_), lambda i: (0, 0), pipeline_mode=pl.Buffered(1)),
                  tab_spec, tab_spec, vec(LANES), vec(LANES), vec(LANES), seq_state,
                  pl.BlockSpec((CONV_WIDTH, CONV_CHANNELS), lambda i: (0, 0)), vec(LANES), vec(LANES)],
        out_specs=[row(w) for w, _ in widths]
                  + [pl.BlockSpec((KV_WIDTH, tm), lambda i: (0, i)), row(IDX_DIM), row(LANES), row(LANES), row(LANES),
                     seq_state],
        out_shape=[jax.ShapeDtypeStruct((n, w), dt) for w, dt in widths]
                  + [jax.ShapeDtypeStruct((KV_WIDTH, n), BF16), jax.ShapeDtypeStruct((n, IDX_DIM), F32),
                     jax.ShapeDtypeStruct((n, LANES), BF16), jax.ShapeDtypeStruct((n, LANES), F32),
                     jax.ShapeDtypeStruct((n, LANES), F32),
                     jax.ShapeDtypeStruct((b, SUBLANES, CONV_CHANNELS), F32)],
        scratch_shapes=[pltpu.VMEM((SUBLANES, CONV_CHANNELS), F32),
                        pltpu.VMEM((3, SUBLANES + tm, DELTA_WIDTH), F32)],
        compiler_params=pltpu.CompilerParams(dimension_semantics=("arbitrary",),
                                             vmem_limit_bytes=FRONT_VMEM_LIMIT),
    )(x2d, shift, scale, norm_w.reshape(1, d), w_packed, tabm, tabi,
      q_norm_w.reshape(1, LANES), k_norm_w.reshape(1, LANES), iw, prev8, conv_w, pad_vec(a_log), pad_vec(dt_bias))
    names = ("q", "qi", "z", "qn", "kn", "vv", "kf", "kb", "vf", "vt", "kif", "kx", "bg", "misc", "tail")
    return dict(zip(names, outs))


def _pick_tile(n, pref, mult=16):
    t = min(pref, n)
    while n % t or t % mult:
        t -= 1
    return t


def _pack_w_in(w_in):
    d = w_in.shape[0]
    bounds = np.cumsum(PROJ_SIZES)[:-1].tolist()
    qa, ka, va, qi, ki, wi, qd, kd, vd, zd, bd, ad = jnp.split(w_in, bounds, axis=1)
    used = IDX_DIM + N_IDX_HEADS + 2 * N_DELTA_HEADS
    misc = jnp.concatenate([ki, wi, bd, ad, jnp.zeros((d, LANES - used), w_in.dtype)], axis=1)
    cols = [qa, qi, zd, qd, kd, vd, ka, va, misc]
    width = sum(c.shape[1] for c in cols)
    cols.append(jnp.zeros((d, PROJ_PACKED - width), w_in.dtype))
    return jnp.concatenate(cols, axis=1).astype(BF16)


def _route_and_sort(eid, bm):
    n = eid.shape[0]
    nk = 2 * n
    flat_e = eid.reshape(-1)
    order = jnp.argsort(flat_e, stable=True).astype(I32)
    inv = jnp.argsort(order).astype(I32)
    onehot = flat_e[:, None] == jnp.arange(N_EXPERTS, dtype=I32)[None, :]
    counts = jnp.sum(onehot.astype(I32), axis=0)
    padded = (counts + bm - 1) // bm * bm
    pad_end = jnp.cumsum(padded)
    shift = (pad_end - padded) - (jnp.cumsum(counts) - counts)
    dest = inv + jnp.sum(jnp.where(onehot, shift[None, :], 0), axis=1)
    n_blocks = -(-nk // bm) + N_EXPERTS + 1
    block_exp = jnp.minimum(jnp.searchsorted(pad_end, jnp.arange(n_blocks, dtype=I32) * bm, side='right'),
                            N_EXPERTS - 1).astype(I32)
    n_active = (pad_end[-1] // bm).astype(I32).reshape(1)
    block_j0 = jnp.clip(jnp.arange(n_blocks, dtype=I32) * bm - shift[block_exp], 0, nk)
    tok_sorted = jnp.concatenate([order // 2, jnp.zeros((bm,), I32)])
    return tok_sorted, block_j0, dest.astype(I32).reshape(n, 2), block_exp, n_active


def _layer(layer, yp, ys, cache_k, cache_v, cache_idx, state_ssm, state_conv, page_table, c_prompt, c_sample,
           w_in, w_out, conv_w, a_log, dt_bias, q_norm_w, k_norm_w, idx_k_norm_w, o_norm_w, norm1_w, norm2_w,
           w_ada, b_ada, w_group, b_group, w_router, b_router, w_gate, w_up, w_down):
    bp, tp, d = yp.shape
    bs, ts, _ = ys.shape
    past = page_table.shape[1] * PAGE_SIZE
    rows = SAMPLE_ROWS
    assert CONV_WIDTH - 1 <= ts <= rows and tp % SEL_SPAN == 0 and tp % DELTA_CHUNK == 0

    n_c = bp + bs
    n_c_pad = -(-n_c // SUBLANES) * SUBLANES
    c_all = jnp.concatenate([c_prompt, c_sample, jnp.zeros((n_c_pad - n_c, d), F32)], axis=0)
    mod = _ada_modulation(c_all, w_ada, b_ada)
    mods = jnp.split(mod, N_MOD, axis=1)
    mp = [m[:bp].reshape(bp, 1, d) for m in mods]
    ms = [jnp.repeat(m[bp:bp + bs], rows, axis=0).reshape(1, bs * rows, d) for m in mods]

    w_packed = _pack_w_in(w_in)
    w_out_bf = w_out.astype(BF16)
    w_rt = jnp.concatenate([w_group, w_router, jnp.zeros((d, LANES - N_GROUPS - N_EXPERTS), F32)], axis=1)
    wr_hi = w_rt.astype(BF16)
    wr_both = jnp.concatenate([wr_hi, (w_rt - wr_hi.astype(F32)).astype(BF16)], axis=1)
    b_rt = jnp.concatenate([b_group, b_router, jnp.zeros((LANES - N_GROUPS - N_EXPERTS,), F32)]).reshape(1, LANES)

    np_ = bp * tp
    xp2 = yp.reshape(np_, d)
    tm_p = _pick_tile(tp, 256)
    fr = _prompt_front(xp2, mp[0], mp[1], norm1_w, w_packed, jnp.arange(tp), q_norm_w, k_norm_w, idx_k_norm_w,
                       jnp.zeros((bp, SUBLANES, CONV_CHANNELS), F32), conv_w, a_log, dt_bias, bp, tp, tm_p)
    kf_p, kif_p = fr["kf"], fr["kif"]
    oa_p = _dsa_prompt(fr["q"], fr["qi"], fr["misc"], fr["kb"], fr["vt"], fr["kx"], bp, tp)
    od_p, ssm_p = _delta_chunks(fr["qn"], fr["kn"], fr["vv"], fr["bg"], fr["z"], 0,
                                jnp.zeros((bp, N_DELTA_HEADS, HEAD_DIM, HEAD_DIM), F32), o_norm_w, bp, tp)

    ns_ = bs * rows
    xs2 = jnp.pad(ys, ((0, 0), (0, rows - ts), (0, 0))).reshape(ns_, d)
    tm_s = _pick_tile(ns_, 256)
    proj_s = _in_projection(xs2, ms[0].reshape(ns_ // tm_s, tm_s, d), ms[1].reshape(ns_ // tm_s, tm_s, d),
                            norm1_w, w_packed, tm_s, tm_s)
    q_s, kf_s, kb_s, vb_s, qi_s, kif_s, _ = _attention_prep(
        proj_s, past + jnp.arange(rows), rows, q_norm_w, k_norm_w, idx_k_norm_w, False)
    q_t = qi_s.reshape(bs, rows, N_IDX_HEADS, IDX_DIM).transpose(0, 2, 1, 3).reshape(bs, N_IDX_HEADS * rows, IDX_DIM)
    w_col = proj_s[:, C_MISC + M_WI:C_MISC + M_WI + N_IDX_HEADS].reshape(bs, rows, N_IDX_HEADS)
    w_col = w_col.transpose(0, 2, 1).reshape(bs, N_IDX_HEADS * rows, 1)
    pages = _pick_tile(page_table.shape[1], SAMPLE_PAGES_PER_STEP, 1)
    keys_past, keys_new = _sample_scores(page_table, q_t, w_col, kif_s, cache_idx, layer, pages, ts)
    n_sel_s = min(TOPK_MAX, (past + ts) // 4)
    oa_s = _sample_attend(page_table, keys_past, keys_new, q_s, kb_s, vb_s, cache_k, cache_v, layer, pages, n_sel_s)
    prev8 = jnp.pad(state_conv, ((0, 0), (SUBLANES - (CONV_WIDTH - 1), 0), (0, 0)))
    qn_s, kn_s, vv_s, bg_s = _delta_prep(proj_s, prev8, conv_w, a_log, dt_bias, bs, rows, rows, ts)
    to_chunk = lambda a: jnp.pad(a.reshape(bs, rows, -1), ((0, 0), (0, DELTA_CHUNK - rows), (0, 0))).reshape(
        bs * DELTA_CHUNK, -1)
    z_s = proj_s[:, C_ZD:C_ZD + DELTA_WIDTH]
    od_s, ssm_s = _delta_chunks(to_chunk(qn_s), to_chunk(kn_s), to_chunk(vv_s), to_chunk(bg_s), to_chunk(z_s), 0,
                                state_ssm, o_norm_w, bs, DELTA_CHUNK)
    od_s = od_s.reshape(bs, DELTA_CHUNK, DELTA_WIDTH)[:, :rows].reshape(ns_, DELTA_WIDTH)

    tm_o = _pick_tile(tp, tm_s)
    assert ns_ % tm_o == 0
    per_tok = lambda m: m.reshape(ns_ // tm_o, tm_o, d)
    n_all = np_ + ns_
    x1_all, h2_all, lg_all = _out_projection(
        (oa_p, od_p, xp2, mp[2], mp[3], mp[4]),
        (oa_s, od_s, xs2, per_tok(ms[2]), per_tok(ms[3]), per_tok(ms[4])),
        norm2_w, w_out_bf, wr_both, b_rt, tm_o, tp)
    eid, gates = _route(lg_all, _pick_tile(n_all, 512, SUBLANES))
    bm = 256
    tok_sorted, block_j0, dest, block_exp, n_active = _route_and_sort(eid[:, 0:2], bm)
    yb = _moe_experts(tok_sorted, block_j0, block_exp, n_active, h2_all, w_gate, w_up, w_down, bm)
    out_p, out_s = _combine(x1_all, dest, yb, gates, mp[5], per_tok(ms[5]), tm_o, np_, tp)

    valid = lambda a: a.reshape(bs, rows, -1)[:, :ts]
    conv_p = fr["tail"][:, SUBLANES - (CONV_WIDTH - 1):]
    conv_s = proj_s.reshape(bs, rows, PROJ_PACKED)[:, ts - (CONV_WIDTH - 1):ts, C_CONV:C_CONV + CONV_CHANNELS]
    return (out_p.reshape(bp, tp, d), valid(out_s),
            kf_p.reshape(bp, tp, N_KV_HEADS, HEAD_DIM),
            fr["vf"].reshape(bp, tp, N_KV_HEADS, HEAD_DIM),
            kif_p.reshape(bp, tp, IDX_DIM), ssm_p, conv_p,
            valid(kf_s).reshape(bs, ts, N_KV_HEADS, HEAD_DIM),
            valid(proj_s[:, C_VA:C_VA + KV_WIDTH]).reshape(bs, ts, N_KV_HEADS, HEAD_DIM),
            valid(kif_s), ssm_s, conv_s)


def kernel(x_prompt, x_sample, cache_k, cache_v, cache_idx_k, state_ssm, state_conv, page_table, c_prompt, c_sample,
           w_in, w_out, conv_w, a_log, dt_bias, q_norm_w, k_norm_w, idx_k_norm_w, o_norm_w, norm1_w, norm2_w,
           w_ada, b_ada, w_group, b_group, w_router, b_router, w_gate, w_up, w_down):
    depth = w_in.shape[0]
    yp, ys = x_prompt, x_sample
    per_layer = []
    for l in range(depth):
        res = _layer(l, yp, ys, cache_k, cache_v, cache_idx_k, state_ssm[l], state_conv[l], page_table,
                     c_prompt, c_sample, w_in[l], w_out[l], conv_w[l], a_log[l], dt_bias[l], q_norm_w[l],
                     k_norm_w[l], idx_k_norm_w[l], o_norm_w[l], norm1_w[l], norm2_w[l], w_ada[l], b_ada[l],
                     w_group[l], b_group[l], w_router[l], b_router[l], w_gate[l], w_up[l], w_down[l])
        yp, ys = res[0], res[1]
        per_layer.append(res[2:])
    stacked = tuple(jnp.stack([pl_[j] for pl_ in per_layer]) for j in range(10))
    return (yp, ys) + stacked
```

```python
import functools

import jax
import jax.numpy as jnp
import numpy as np
from jax import lax
from jax.experimental import pallas as pl
from jax.experimental.pallas import tpu as pltpu

F32 = jnp.float32
BF16 = jnp.bfloat16
I32 = jnp.int32

HEAD_DIM = 128
N_ATTN_HEADS = 8
N_KV_HEADS = 2
KV_GROUP = N_ATTN_HEADS // N_KV_HEADS
N_DELTA_HEADS = 8
N_IDX_HEADS = 16
IDX_DIM = 64
ATTN_WIDTH = N_ATTN_HEADS * HEAD_DIM
KV_WIDTH = N_KV_HEADS * HEAD_DIM
DELTA_WIDTH = N_DELTA_HEADS * HEAD_DIM
IDX_WIDTH = N_IDX_HEADS * IDX_DIM
CONV_CHANNELS = 3 * DELTA_WIDTH
TOPK_MAX = 256
ROPE_THETA = 500000.0
ROPE_FRACTION = 4
CONV_WIDTH = 4
DELTA_CHUNK = 64
N_GROUPS = 8
EXPERTS_PER_GROUP = 8
N_EXPERTS = N_GROUPS * EXPERTS_PER_GROUP
N_MOD = 6
EPS = 1e-6
PAGE_SIZE = 128
PROJ_SIZES = (ATTN_WIDTH, KV_WIDTH, KV_WIDTH, IDX_WIDTH, IDX_DIM, N_IDX_HEADS,
              DELTA_WIDTH, DELTA_WIDTH, DELTA_WIDTH, DELTA_WIDTH, N_DELTA_HEADS, N_DELTA_HEADS)

LANES = 128
SUBLANES = 8
VMEM_LIMIT = 56 * 1024 * 1024
FRONT_VMEM_LIMIT = 60 * 1024 * 1024

C_QA = 0
C_QI = 1024
C_ZD = 2048
C_CONV = 3072
C_KA = 6144
C_VA = 6400
C_MISC = 6656
PROJ_PACKED = 6912
M_KI = 0
M_WI = 64
M_BD = 80
M_AD = 88

Q_TILE = 128
KEY_CHUNK = 256
SEL_SPAN = 512
NEG_INF_KEY = -2139095041
SAMPLE_ROWS = 16
SAMPLE_PAGES_PER_STEP = 16
NEG_BIG = -1e30
INT_MIN = -2147483648
INT_MAX = 2147483647


def _cparams(sem):
    return pltpu.CompilerParams(dimension_semantics=sem, vmem_limit_bytes=VMEM_LIMIT)


def _dot(a, b):
    return jnp.dot(a, b, preferred_element_type=F32)


def _dot_nt(a, b):
    return lax.dot_general(a, b, (((1,), (1,)), ((), ())), preferred_element_type=F32)


def _dot_tn(a, b):
    return lax.dot_general(a, b, (((0,), (0,)), ((), ())), preferred_element_type=F32)


def _sigmoid(x):
    return 0.5 * jnp.tanh(0.5 * x) + 0.5


def _silu(x):
    return x * _sigmoid(x)


def _softplus(x):
    return jnp.maximum(x, 0.0) + jnp.log(1.0 + jnp.exp(-jnp.abs(x)))


def _ada_kernel(c_ref, w_ref, b_ref, o_ref):
    s = _silu(c_ref[...]).astype(BF16)
    o_ref[...] = _dot(s, w_ref[...].astype(BF16)) + b_ref[...]


def _ada_modulation(c, w_ada, b_ada):
    r, d = c.shape
    n = w_ada.shape[1]
    tn = 1024 if n % 1024 == 0 else n
    return pl.pallas_call(
        _ada_kernel,
        name="ada_mod",
        grid=(n // tn,),
        in_specs=[pl.BlockSpec((r, d), lambda j: (0, 0)),
                  pl.BlockSpec((d, tn), lambda j: (0, j)),
                  pl.BlockSpec((1, tn), lambda j: (0, j))],
        out_specs=pl.BlockSpec((r, tn), lambda j: (0, j)),
        out_shape=jax.ShapeDtypeStruct((r, n), F32),
        compiler_params=_cparams(("parallel",)),
    )(c, w_ada, b_ada.reshape(1, n))


INPROJ_COLS = 1152


def _inproj_kernel(x_ref, sh_ref, sc_ref, nw_ref, w_ref, o_ref):
    x = x_ref[...]
    y = x * lax.rsqrt(jnp.mean(x * x, axis=-1, keepdims=True) + EPS) * nw_ref[...]
    h = (y * (1.0 + sc_ref[0]) + sh_ref[0]).astype(BF16)
    for c0 in range(0, o_ref.shape[1], INPROJ_COLS):
        o_ref[:, c0:c0 + INPROJ_COLS] = _dot(h, w_ref[:, c0:c0 + INPROJ_COLS])


def _in_projection(x2d, shift, scale, norm_w, w_packed, tm, rows_per_mod_block):
    n, d = x2d.shape
    np_ = w_packed.shape[1]
    r = shift.shape[1]
    tiles_per_mod = rows_per_mod_block // tm
    mod_spec = pl.BlockSpec((1, r, d), lambda i: (i // tiles_per_mod, 0, 0))
    return pl.pallas_call(
        _inproj_kernel,
        name="in_proj",
        grid=(n // tm,),
        in_specs=[pl.BlockSpec((tm, d), lambda i: (i, 0)),
                  mod_spec, mod_spec,
                  pl.BlockSpec((1, d), lambda i: (0, 0)),
                  pl.BlockSpec((d, np_), lambda i: (0, 0), pipeline_mode=pl.Buffered(1))],
        out_specs=pl.BlockSpec((tm, np_), lambda i: (i, 0)),
        out_shape=jax.ShapeDtypeStruct((n, np_), F32),
        compiler_params=_cparams(("parallel",)),
    )(x2d, shift, scale, norm_w.reshape(1, d), w_packed)


def _rope(x, tab, rot):
    c = tab[:, 0:LANES]
    s1 = tab[:, LANES:2 * LANES]
    s2 = tab[:, 2 * LANES:3 * LANES]
    return x * c + pltpu.roll(x, LANES - rot, 1) * s1 + pltpu.roll(x, rot, 1) * s2


def _rms_head(x, w):
    return x * lax.rsqrt(jnp.mean(x * x, axis=-1, keepdims=True) + EPS) * w


def _prep_kernel(qa_ref, qi_ref, ka_ref, va_ref, misc_ref, tabm_ref, tabi_ref, qw_ref, kw_ref, iw_ref,
                 q_ref, kf_ref, kb_ref, vb_ref, qib_ref, kif_ref, kib_ref, *, transpose_v):
    tabm = tabm_ref[0]
    tabi = tabi_ref[0]
    half_main = HEAD_DIM // ROPE_FRACTION // 2
    half_idx = IDX_DIM // ROPE_FRACTION // 2
    for h in range(N_ATTN_HEADS):
        sl = slice(h * HEAD_DIM, (h + 1) * HEAD_DIM)
        y = _rope(_rms_head(qa_ref[:, sl], qw_ref[...]), tabm, half_main)
        q_ref[:, sl] = (y * (HEAD_DIM ** -0.5)).astype(BF16)
    for h in range(N_KV_HEADS):
        sl = slice(h * HEAD_DIM, (h + 1) * HEAD_DIM)
        y = _rope(_rms_head(ka_ref[:, sl], kw_ref[...]), tabm, half_main)
        kf_ref[:, sl] = y
        kb_ref[:, sl] = y.astype(BF16)
    if transpose_v:
        vb_ref[...] = va_ref[...].T.astype(BF16)
    else:
        vb_ref[...] = va_ref[...].astype(BF16)
    for p in range(IDX_WIDTH // LANES):
        sl = slice(p * LANES, (p + 1) * LANES)
        qib_ref[:, sl] = _rope(qi_ref[:, sl], tabi, half_idx).astype(BF16)
    m = misc_ref[...]
    lane = lax.broadcasted_iota(I32, m.shape, 1)
    ki = jnp.where(lane < IDX_DIM, m, 0.0)
    ms = jnp.sum(ki * ki, axis=-1, keepdims=True) * (1.0 / IDX_DIM)
    y = _rope(ki * lax.rsqrt(ms + EPS) * iw_ref[...], tabi, half_idx)
    kif_ref[...] = y[:, 0:IDX_DIM]
    kib_ref[...] = (y + pltpu.roll(y, IDX_DIM, 1)).astype(BF16)


def _rope_tables(pos, head_dim, group):
    d_rot = head_dim // ROPE_FRACTION
    half = d_rot // 2
    inv_freq = jnp.power(ROPE_THETA, -(jnp.arange(half, dtype=F32) * 2.0 / d_rot))
    ang = pos.astype(F32)[:, None] * inv_freq[None, :]
    cos = jnp.cos(ang)
    sin = jnp.sin(ang)
    t = pos.shape[0]
    z = jnp.zeros((t, group - d_rot), F32)
    c = jnp.concatenate([cos, cos, jnp.ones((t, group - d_rot), F32)], axis=1)
    s1 = jnp.concatenate([-sin, jnp.zeros((t, half), F32), z], axis=1)
    s2 = jnp.concatenate([jnp.zeros((t, half), F32), sin, z], axis=1)
    rep = LANES // group
    return jnp.concatenate([jnp.tile(c, (1, rep)), jnp.tile(s1, (1, rep)), jnp.tile(s2, (1, rep))], axis=1)


def _attention_prep(proj, pos, tq, q_norm_w, k_norm_w, idx_k_norm_w, transpose_v):
    n = proj.shape[0]
    p = pos.shape[0]
    g = p // tq
    tabm = _rope_tables(pos, HEAD_DIM, LANES).reshape(g, tq, 3 * LANES)
    tabi = _rope_tables(pos, IDX_DIM, IDX_DIM).reshape(g, tq, 3 * LANES)
    iw = jnp.concatenate([idx_k_norm_w, jnp.zeros((LANES - IDX_DIM,), F32)]).reshape(1, LANES)
    row = lambda w, c: pl.BlockSpec((tq, w), lambda i: (i, c // w))
    tab_spec = pl.BlockSpec((1, tq, 3 * LANES), lambda i: (i % g, 0, 0))
    vec_spec = pl.BlockSpec((1, LANES), lambda i: (0, 0))
    out_row = lambda w: pl.BlockSpec((tq, w), lambda i: (i, 0))
    v_spec = pl.BlockSpec((KV_WIDTH, tq), lambda i: (0, i)) if transpose_v else out_row(KV_WIDTH)
    v_shape = (KV_WIDTH, n) if transpose_v else (n, KV_WIDTH)
    return pl.pallas_call(
        functools.partial(_prep_kernel, transpose_v=transpose_v),
        name="attn_prep",
        grid=(n // tq,),
        in_specs=[row(ATTN_WIDTH, C_QA), row(IDX_WIDTH, C_QI), row(KV_WIDTH, C_KA), row(KV_WIDTH, C_VA),
                  row(LANES, C_MISC), tab_spec, tab_spec, vec_spec, vec_spec, vec_spec],
        out_specs=[out_row(ATTN_WIDTH), out_row(KV_WIDTH), out_row(KV_WIDTH), v_spec,
                   out_row(IDX_WIDTH), out_row(IDX_DIM), out_row(LANES)],
        out_shape=[jax.ShapeDtypeStruct((n, ATTN_WIDTH), BF16),
                   jax.ShapeDtypeStruct((n, KV_WIDTH), F32),
                   jax.ShapeDtypeStruct((n, KV_WIDTH), BF16),
                   jax.ShapeDtypeStruct(v_shape, BF16),
                   jax.ShapeDtypeStruct((n, IDX_WIDTH), BF16),
                   jax.ShapeDtypeStruct((n, IDX_DIM), F32),
                   jax.ShapeDtypeStruct((n, LANES), BF16)],
        compiler_params=_cparams(("parallel",)),
    )(proj, proj, proj, proj, proj, tabm, tabi,
      q_norm_w.reshape(1, LANES), k_norm_w.reshape(1, LANES), iw)


def _sort_key(x):
    b = pltpu.bitcast(x + 0.0, I32)
    return b ^ ((b >> 31) & INT_MAX)


def _kth_largest_key(count_ge, k, shape, n_total):
    def body(it, carry):
        ans_u, n_ge = carry
        bit = jnp.left_shift(jnp.int32(1), 31 - it)
        cand_u = ans_u | bit
        cnt = count_ge(cand_u ^ INT_MIN)
        ok = cnt >= k
        return jnp.where(ok, cand_u, ans_u), jnp.where(ok, cnt, n_ge)

    ans_u, n_ge = lax.fori_loop(0, 32, body, (jnp.zeros(shape, I32), jnp.full(shape, float(n_total), F32)))
    return ans_u ^ INT_MIN, n_ge


def _tie_index_limit(count_eq_le, need, n_keys, shape):
    nbits = max(1, int(n_keys - 1).bit_length())

    def body(it, lo):
        bit = jnp.left_shift(jnp.int32(1), nbits - 1 - it)
        cand = lo | bit
        cnt = count_eq_le(cand - 1)
        return jnp.where(cnt >= need, lo, cand)

    return lax.fori_loop(0, nbits, body, jnp.zeros(shape, I32))


def _dsa_prompt_kernel(q_ref, qi_ref, misc_ref, k_ref, vt_ref, kx_ref, o_ref,
                       key_scr, qsel_scr, qg_scr, thr_scr, lim_scr, m_scr, l_scr, acc_scr, *, n_sel):
    i = pl.program_id(1)
    tq = Q_TILE
    ck = KEY_CHUNK
    n_ch = (i * tq + tq + ck - 1) // ck
    q_pos = i * tq + lax.broadcasted_iota(I32, (1, tq), 1)
    row_k = lax.broadcasted_iota(I32, (ck, 1), 0)

    lo_half = lax.broadcasted_iota(I32, (tq, LANES), 1) < IDX_DIM
    zero = jnp.zeros((), BF16)
    for p in range(IDX_WIDTH // LANES):
        slab = qi_ref[:, p * LANES:(p + 1) * LANES]
        qsel_scr[(2 * p) * tq:(2 * p + 1) * tq, :] = jnp.where(lo_half, slab, zero)
        qsel_scr[(2 * p + 1) * tq:(2 * p + 2) * tq, :] = jnp.where(lo_half, zero, slab)
    w_t = misc_ref[...].T

    def score_chunk(c, carry):
        off = pl.multiple_of(c * ck, ck)
        s = _dot_nt(kx_ref[pl.ds(off, ck), :], qsel_scr[...])
        acc = jnp.zeros((ck, tq), F32)
        for h in range(N_IDX_HEADS):
            acc = acc + w_t[M_WI + h:M_WI + h + 1, :] * jnp.maximum(s[:, h * tq:(h + 1) * tq], 0.0)
        acc = jnp.where(off + row_k <= q_pos, acc, -jnp.inf)
        key_scr[pl.ds(off, ck), :] = _sort_key(acc)
        return carry

    lax.fori_loop(0, n_ch, score_chunk, 0)

    spc = SEL_SPAN // ck
    n_span = (n_ch + spc - 1) // spc
    neg_key = jnp.full((ck, tq), NEG_INF_KEY, I32)

    def pad_chunk(c, carry):
        key_scr[pl.ds(pl.multiple_of(c * ck, ck), ck), :] = neg_key
        return carry

    lax.fori_loop(n_ch, n_span * spc, pad_chunk, 0)

    thr_scr[...] = jnp.full((1, tq), INT_MIN, I32)
    lim_scr[...] = jnp.full((1, tq), INT_MAX, I32)

    def select_threshold(n_keys):
        def count_where(pred):
            tot = jnp.zeros((SUBLANES, tq), F32)
            for c0 in range(0, n_keys, ck):
                hit = pred(key_scr[c0:c0 + ck, :], c0 + row_k).astype(F32)
                tot = tot + jnp.sum(hit.reshape(ck // SUBLANES, SUBLANES, tq), axis=0)
            return jnp.sum(tot, axis=0, keepdims=True)

        t, n_ge = _kth_largest_key(lambda cand: count_where(lambda kk, pos: kk >= cand), float(n_sel), (1, tq),
                                   n_keys)
        thr_scr[...] = t

        @pl.when(jnp.max(n_ge) > float(n_sel))
        def _():
            n_gt = count_where(lambda kk, pos: kk > t)
            lim_scr[...] = _tie_index_limit(
                lambda idx: count_where(lambda kk, pos: jnp.logical_and(kk == t, pos <= idx)),
                float(n_sel) - n_gt, k_ref.shape[0], (1, tq))

    for spans in range(1, k_ref.shape[0] // SEL_SPAN + 1):
        if spans * SEL_SPAN > n_sel:
            pl.when(jnp.logical_and(n_span == spans, (i + 1) * tq > n_sel))(
                functools.partial(select_threshold, spans * SEL_SPAN))

    thr = thr_scr[...]
    lim = lim_scr[...]

    for g in range(N_KV_HEADS):
        for r in range(KV_GROUP):
            h = g * KV_GROUP + r
            qg_scr[g, r * tq:(r + 1) * tq, :] = q_ref[:, h * HEAD_DIM:(h + 1) * HEAD_DIM]
    m_scr[...] = jnp.full(m_scr.shape, NEG_BIG, F32)
    l_scr[...] = jnp.zeros(l_scr.shape, F32)
    acc_scr[...] = jnp.zeros(acc_scr.shape, F32)

    def attend_chunk(c, carry):
        off = pl.multiple_of(c * ck, ck)
        kk = key_scr[pl.ds(off, ck), :]
        pos = off + row_k
        sel = jnp.logical_or(kk > thr, jnp.logical_and(kk == thr, pos <= lim))
        sel = jnp.logical_and(sel, pos <= q_pos)
        def group_steps(g):
            kc = k_ref[pl.ds(off, ck), g * HEAD_DIM:(g + 1) * HEAD_DIM]
            vt = vt_ref[g * HEAD_DIM:(g + 1) * HEAD_DIM, pl.ds(off, ck)]
            qk = _dot_nt(kc, qg_scr[g])
            yield
            s = jnp.concatenate([jnp.where(sel, qk[:, r * tq:(r + 1) * tq], NEG_BIG) for r in range(KV_GROUP)],
                                axis=1)
            m_old = m_scr[g]
            m_new = jnp.maximum(m_old, jnp.max(s, axis=0, keepdims=True))
            yield
            p = jnp.exp(s - m_new)
            alpha = jnp.exp(m_old - m_new)
            l_scr[g] = alpha * l_scr[g] + jnp.sum(p, axis=0, keepdims=True)
            yield
            acc_scr[g] = alpha * acc_scr[g] + _dot(vt, p.astype(BF16))
            m_scr[g] = m_new

        chains = [group_steps(g) for g in range(N_KV_HEADS)]
        while chains:
            chains = [ch for ch in chains if next(ch, "done") != "done"]
        return carry

    lax.fori_loop(0, n_ch, attend_chunk, 0)
    for g in range(N_KV_HEADS):
        o_t = acc_scr[g] / l_scr[g]
        for r in range(KV_GROUP):
            h = g * KV_GROUP + r
            o_ref[:, h * HEAD_DIM:(h + 1) * HEAD_DIM] = o_t[:, r * tq:(r + 1) * tq].T.astype(BF16)


def _dsa_prompt(q_bf, qi_bf, misc, k_bf, vt_bf, kx_bf, b, t):
    n = b * t
    nq = t // Q_TILE
    n_sel = min(TOPK_MAX, t // 4)
    qrow = lambda w: pl.BlockSpec((Q_TILE, w), lambda bb, i: (bb * nq + i, 0))
    seq = lambda w: pl.BlockSpec((t, w), lambda bb, i: (bb, 0))
    return pl.pallas_call(
        functools.partial(_dsa_prompt_kernel, n_sel=n_sel),
        name="dsa_prompt",
        grid=(b, nq),
        in_specs=[qrow(ATTN_WIDTH), qrow(IDX_WIDTH),
                  qrow(LANES),
                  seq(KV_WIDTH), pl.BlockSpec((KV_WIDTH, t), lambda bb, i: (0, bb)), seq(LANES)],
        out_specs=qrow(ATTN_WIDTH),
        out_shape=jax.ShapeDtypeStruct((n, ATTN_WIDTH), BF16),
        scratch_shapes=[pltpu.VMEM((t, Q_TILE), I32),
                        pltpu.VMEM((N_IDX_HEADS * Q_TILE, LANES), BF16),
                        pltpu.VMEM((N_KV_HEADS, KV_GROUP * Q_TILE, HEAD_DIM), BF16),
                        pltpu.VMEM((1, Q_TILE), I32),
                        pltpu.VMEM((1, Q_TILE), I32),
                        pltpu.VMEM((N_KV_HEADS, 1, KV_GROUP * Q_TILE), F32),
                        pltpu.VMEM((N_KV_HEADS, 1, KV_GROUP * Q_TILE), F32),
                        pltpu.VMEM((N_KV_HEADS, HEAD_DIM, KV_GROUP * Q_TILE), F32)],
        compiler_params=_cparams(("parallel", "arbitrary")),
    )(q_bf, qi_bf, misc, k_bf, vt_bf, kx_bf)


def _sample_score_kernel(pt_ref, q_ref, w_ref, kn_ref, *refs, pages, t_valid):
    page_refs = refs[:pages]
    past_ref, new_ref = refs[pages], refs[pages + 1]
    rows = SAMPLE_ROWS
    hr = N_IDX_HEADS * rows
    half = PAGE_SIZE // 2
    q2 = q_ref[0]
    w = w_ref[0]

    def head_sum(s):
        s = w * jnp.maximum(s, 0.0)
        acc = s[0:rows]
        for h in range(1, N_IDX_HEADS):
            acc = acc + s[h * rows:(h + 1) * rows]
        return acc

    for j in range(pages):
        s = _dot_nt(q2, page_refs[j][0, 0].astype(BF16))
        past_ref[0, :, j * PAGE_SIZE:j * PAGE_SIZE + half] = _sort_key(head_sum(s[0:hr]))
        past_ref[0, :, j * PAGE_SIZE + half:(j + 1) * PAGE_SIZE] = _sort_key(head_sum(s[hr:2 * hr]))

    @pl.when(pl.program_id(1) == 0)
    def _():
        kn = jnp.concatenate([kn_ref[...], jnp.zeros((LANES - rows, IDX_DIM), F32)], axis=0).astype(BF16)
        sc = head_sum(_dot_nt(q2[0:hr, 0:IDX_DIM], kn))
        t = lax.broadcasted_iota(I32, sc.shape, 0)
        s = lax.broadcasted_iota(I32, sc.shape, 1)
        ok = jnp.logical_and(s <= t, s < t_valid)
        new_ref[0] = _sort_key(jnp.where(ok, sc, -jnp.inf))


def _page_order_pos(lane_pos):
    p = lane_pos & (PAGE_SIZE - 1)
    half = PAGE_SIZE // 2
    return (lane_pos - p) + 2 * (p & (half - 1)) + (p // half)


def _sample_scores(page_table, q_t, w_col, kif, cache_idx, layer, pages, t_valid):
    bs, n_pages = page_table.shape
    past = n_pages * PAGE_SIZE
    hr = N_IDX_HEADS * SAMPLE_ROWS
    depth, pool = cache_idx.shape[0], cache_idx.shape[1]
    cache_idx = cache_idx.reshape(depth, pool, PAGE_SIZE // 2, 2 * IDX_DIM)
    zq = jnp.zeros_like(q_t)
    q_t = jnp.concatenate([jnp.concatenate([q_t, zq], axis=2), jnp.concatenate([zq, q_t], axis=2)], axis=1)
    page_spec = lambda j: pl.BlockSpec((1, 1, PAGE_SIZE // 2, 2 * IDX_DIM),
                                       lambda b, c, pt: (layer, pt[b, c * pages + j], 0, 0))
    grid_spec = pltpu.PrefetchScalarGridSpec(
        num_scalar_prefetch=1,
        grid=(bs, n_pages // pages),
        in_specs=[pl.BlockSpec((1, 2 * hr, 2 * IDX_DIM), lambda b, c, pt: (b, 0, 0)),
                  pl.BlockSpec((1, hr, 1), lambda b, c, pt: (b, 0, 0)),
                  pl.BlockSpec((SAMPLE_ROWS, IDX_DIM), lambda b, c, pt: (b, 0))]
                 + [page_spec(j) for j in range(pages)],
        out_specs=[pl.BlockSpec((1, SAMPLE_ROWS, pages * PAGE_SIZE), lambda b, c, pt: (b, 0, c)),
                   pl.BlockSpec((1, SAMPLE_ROWS, LANES), lambda b, c, pt: (b, 0, 0))],
    )
    return pl.pallas_call(
        functools.partial(_sample_score_kernel, pages=pages, t_valid=t_valid),
        name="sample_scores",
        grid_spec=grid_spec,
        out_shape=[jax.ShapeDtypeStruct((bs, SAMPLE_ROWS, past), I32),
                   jax.ShapeDtypeStruct((bs, SAMPLE_ROWS, LANES), I32)],
        compiler_params=_cparams(("parallel", "arbitrary")),
    )(page_table, q_t, w_col, kif, *([cache_idx] * pages))


def _sample_attend_kernel(pt_ref, kp_ref, kn_ref, q_ref, knew_ref, vnew_ref, *refs, pages, n_sel, past):
    k_pages = refs[:pages]
    v_pages = refs[pages:2 * pages]
    o_ref = refs[2 * pages]
    thr_scr, lim_scr, m_scr, l_scr, acc_scr = refs[2 * pages + 1:]
    c = pl.program_id(1)
    rows = SAMPLE_ROWS
    span = pages * PAGE_SIZE

    @pl.when(c == 0)
    def _():
        m_scr[...] = jnp.full(m_scr.shape, NEG_BIG, F32)
        l_scr[...] = jnp.zeros(l_scr.shape, F32)
        acc_scr[...] = jnp.zeros(acc_scr.shape, F32)
        vr = SUBLANES
        tile_pos = lax.broadcasted_iota(I32, (vr, LANES), 1)

        def count_where(pred):
            acc = pred(kn_ref[0, 0:vr, :], past + tile_pos).astype(F32)
            for tix in range(past // LANES):
                acc = acc + pred(kp_ref[0, 0:vr, tix * LANES:(tix + 1) * LANES],
                                 _page_order_pos(tix * LANES + tile_pos)).astype(F32)
            return jnp.sum(acc, axis=1, keepdims=True)

        t, n_ge = _kth_largest_key(lambda cand: count_where(lambda kk, pos: kk >= cand), float(n_sel), (vr, 1),
                                   past + LANES)
        thr_scr[...] = jnp.full((rows, 1), INT_MIN, I32)
        lim_scr[...] = jnp.full((rows, 1), INT_MAX, I32)
        thr_scr[0:vr, :] = t

        @pl.when(jnp.max(n_ge) > float(n_sel))
        def _():
            n_gt = count_where(lambda kk, pos: kk > t)
            lim_scr[0:vr, :] = _tie_index_limit(
                lambda idx: count_where(lambda kk, pos: jnp.logical_and(kk == t, pos <= idx)),
                float(n_sel) - n_gt, past + LANES, (vr, 1))

    thr = thr_scr[...]
    lim = lim_scr[...]

    def update_steps(g, qg, k_fn, v_fn, sel):
        sel = jnp.concatenate([sel] * KV_GROUP, axis=0)
        qk = _dot_nt(qg, k_fn())
        yield
        s = jnp.where(sel, qk, NEG_BIG)
        m_old = m_scr[g]
        m_new = jnp.maximum(m_old, jnp.max(s, axis=1, keepdims=True))
        yield
        p = jnp.where(sel, jnp.exp(s - m_new), 0.0)
        alpha = jnp.exp(m_old - m_new)
        l_scr[g] = alpha * l_scr[g] + jnp.sum(p, axis=1, keepdims=True)
        yield
        acc_scr[g] = alpha * acc_scr[g] + _dot(p.astype(BF16), v_fn())
        m_scr[g] = m_new

    def run_lockstep(chains):
        while chains:
            chains = [ch for ch in chains if next(ch, "done") != "done"]

    def select(kk, pos):
        return jnp.logical_or(kk > thr, jnp.logical_and(kk == thr, pos <= lim))

    def page_cat(page_refs, g):
        half = PAGE_SIZE // 2
        parts = [r[0, 0, pl.ds(par * N_KV_HEADS + g, half, stride=2 * N_KV_HEADS), :]
                 for r in page_refs for par in range(2)]
        return jnp.concatenate(parts, axis=0).astype(BF16)

    off = pl.multiple_of(c * span, span)
    kk = kp_ref[0, :, pl.ds(off, span)]
    sel_past = select(kk, _page_order_pos(off + lax.broadcasted_iota(I32, kk.shape, 1)))
    q_groups = []
    for g in range(N_KV_HEADS):
        qg = jnp.concatenate(
            [q_ref[:, (g * KV_GROUP + r) * HEAD_DIM:(g * KV_GROUP + r + 1) * HEAD_DIM] for r in range(KV_GROUP)],
            axis=0)
        q_groups.append(qg)
    run_lockstep([update_steps(g, q_groups[g], functools.partial(page_cat, k_pages, g),
                               functools.partial(page_cat, v_pages, g), sel_past) for g in range(N_KV_HEADS)])

    @pl.when(c == pl.num_programs(1) - 1)
    def _():
        kn = kn_ref[0]
        lane = lax.broadcasted_iota(I32, kn.shape, 1)
        sel_new = jnp.logical_and(select(kn, past + lane), lane < rows)
        pad = jnp.zeros((LANES - rows, KV_WIDTH), BF16)
        k_new = jnp.concatenate([knew_ref[...], pad], axis=0)
        v_new = jnp.concatenate([vnew_ref[...], pad], axis=0)
        head = lambda a, g: (lambda: a[:, g * HEAD_DIM:(g + 1) * HEAD_DIM])
        run_lockstep([update_steps(g, q_groups[g], head(k_new, g), head(v_new, g), sel_new)
                      for g in range(N_KV_HEADS)])
        for g in range(N_KV_HEADS):
            o = acc_scr[g] / l_scr[g]
            for r in range(KV_GROUP):
                h = g * KV_GROUP + r
                o_ref[:, h * HEAD_DIM:(h + 1) * HEAD_DIM] = o[r * rows:(r + 1) * rows].astype(BF16)


def _sample_attend(page_table, keys_past, keys_new, q_bf, k_bf, v_bf, cache_k, cache_v, layer, pages, n_sel):
    bs, n_pages = page_table.shape
    past = n_pages * PAGE_SIZE
    depth, pool = cache_k.shape[0], cache_k.shape[1]
    cache_k = cache_k.reshape(depth, pool, PAGE_SIZE * N_KV_HEADS, HEAD_DIM)
    cache_v = cache_v.reshape(depth, pool, PAGE_SIZE * N_KV_HEADS, HEAD_DIM)
    page_spec = lambda j: pl.BlockSpec((1, 1, PAGE_SIZE * N_KV_HEADS, HEAD_DIM),
                                       lambda b, c, pt: (layer, pt[b, c * pages + j], 0, 0))
    row = lambda w: pl.BlockSpec((SAMPLE_ROWS, w), lambda b, c, pt: (b, 0))
    grid_spec = pltpu.PrefetchScalarGridSpec(
        num_scalar_prefetch=1,
        grid=(bs, n_pages // pages),
        in_specs=[pl.BlockSpec((1, SAMPLE_ROWS, past), lambda b, c, pt: (b, 0, 0)),
                  pl.BlockSpec((1, SAMPLE_ROWS, LANES), lambda b, c, pt: (b, 0, 0)),
                  row(ATTN_WIDTH), row(KV_WIDTH), row(KV_WIDTH)]
                 + [page_spec(j) for j in range(pages)] * 2,
        out_specs=row(ATTN_WIDTH),
        scratch_shapes=[pltpu.VMEM((SAMPLE_ROWS, 1), I32),
                        pltpu.VMEM((SAMPLE_ROWS, 1), I32),
                        pltpu.VMEM((N_KV_HEADS, KV_GROUP * SAMPLE_ROWS, 1), F32),
                        pltpu.VMEM((N_KV_HEADS, KV_GROUP * SAMPLE_ROWS, 1), F32),
                        pltpu.VMEM((N_KV_HEADS, KV_GROUP * SAMPLE_ROWS, HEAD_DIM), F32)],
    )
    return pl.pallas_call(
        functools.partial(_sample_attend_kernel, pages=pages, n_sel=n_sel, past=past),
        name="sample_attend",
        grid_spec=grid_spec,
        out_shape=jax.ShapeDtypeStruct((bs * SAMPLE_ROWS, ATTN_WIDTH), BF16),
        compiler_params=_cparams(("parallel", "arbitrary")),
    )(page_table, keys_past, keys_new, q_bf, k_bf, v_bf, *([cache_k] * pages), *([cache_v] * pages))


def _delta_prep_kernel(x_ref, halo_ref, prev_ref, misc_ref, cw_ref, al_ref, dt_ref,
                       qn_ref, kn_ref, vv_ref, bg_ref, xp_scr, *, tiles_per_seq, t_valid, tt):
    i = pl.program_id(0)
    tile_in_seq = i % tiles_per_seq
    halo = jnp.where(tile_in_seq == 0, prev_ref[0], halo_ref[...])
    xp_scr[0:SUBLANES, :] = halo
    xp_scr[SUBLANES:SUBLANES + tt, :] = x_ref[...]
    base = SUBLANES - (CONV_WIDTH - 1)
    outs = (qn_ref, kn_ref, vv_ref)
    for sec in range(3):
        for h in range(N_DELTA_HEADS):
            col = sec * DELTA_WIDTH + h * HEAD_DIM
            sl = slice(col, col + HEAD_DIM)
            y = xp_scr[base:base + tt, sl] * cw_ref[0:1, sl]
            for j in range(1, CONV_WIDTH):
                y = y + xp_scr[base + j:base + j + tt, sl] * cw_ref[j:j + 1, sl]
            y = _silu(y)
            if sec < 2:
                y = y * lax.rsqrt(jnp.sum(y * y, axis=-1, keepdims=True) + EPS)
            if sec == 0:
                y = y * (HEAD_DIM ** -0.5)
            outs[sec][:, h * HEAD_DIM:(h + 1) * HEAD_DIM] = y
    m = misc_ref[...]
    lane = lax.broadcasted_iota(I32, m.shape, 1)
    row = tile_in_seq * tt + lax.broadcasted_iota(I32, m.shape, 0)
    beta = _sigmoid(m)
    g = -jnp.exp(al_ref[...]) * _softplus(m + dt_ref[...])
    is_b = jnp.logical_and(lane >= M_BD, lane < M_BD + N_DELTA_HEADS)
    is_g = jnp.logical_and(lane >= M_AD, lane < M_AD + N_DELTA_HEADS)
    comb = jnp.where(is_b, beta, jnp.where(is_g, g, 0.0))
    comb = jnp.where(row < t_valid, comb, 0.0)
    bg_ref[...] = pltpu.roll(comb, LANES - M_BD, 1)


def _delta_prep(proj, prev8, conv_w, a_log, dt_bias, b, t, tt, t_valid):
    n = proj.shape[0]
    tiles_per_seq = t // tt
    pad_vec = lambda v: jnp.zeros((1, LANES), F32).at[0, M_AD:M_AD + N_DELTA_HEADS].set(v)
    halo_blocks = tt // SUBLANES
    return pl.pallas_call(
        functools.partial(_delta_prep_kernel, tiles_per_seq=tiles_per_seq, t_valid=t_valid, tt=tt),
        name="delta_prep",
        grid=(n // tt,),
        in_specs=[pl.BlockSpec((tt, CONV_CHANNELS), lambda i: (i, C_CONV // CONV_CHANNELS)),
                  pl.BlockSpec((SUBLANES, CONV_CHANNELS),
                               lambda i: (jnp.maximum(i * halo_blocks - 1, 0), C_CONV // CONV_CHANNELS)),
                  pl.BlockSpec((1, SUBLANES, CONV_CHANNELS), lambda i: (i // tiles_per_seq, 0, 0)),
                  pl.BlockSpec((tt, LANES), lambda i: (i, C_MISC // LANES)),
                  pl.BlockSpec((CONV_WIDTH, CONV_CHANNELS), lambda i: (0, 0)),
                  pl.BlockSpec((1, LANES), lambda i: (0, 0)),
                  pl.BlockSpec((1, LANES), lambda i: (0, 0))],
        out_specs=[pl.BlockSpec((tt, DELTA_WIDTH), lambda i: (i, 0))] * 3
                  + [pl.BlockSpec((tt, LANES), lambda i: (i, 0))],
        out_shape=[jax.ShapeDtypeStruct((n, DELTA_WIDTH), F32)] * 3 + [jax.ShapeDtypeStruct((n, LANES), F32)],
        scratch_shapes=[pltpu.VMEM((SUBLANES + tt, CONV_CHANNELS), F32)],
        compiler_params=_cparams(("parallel",)),
    )(proj, proj, prev8, proj, conv_w, pad_vec(a_log), pad_vec(dt_bias))


def _mm(a, b):
    return _dot(a.astype(BF16), b.astype(BF16))


def _mm_nt(a, b):
    return _dot_nt(a.astype(BF16), b.astype(BF16))


DELTA_INV_BLOCK = 16
DELTA_STACK = 4
DELTA_CHUNKS_PER_STEP = 2


def _delta_chunk_kernel(qn_ref, kn_ref, vv_ref, bg_ref, z_ref, s0_ref, ow_ref, od_ref, so_ref, s_scr, *, n_chunks):
    c = pl.program_id(1)
    cs = DELTA_CHUNK

    @pl.when(c == 0)
    def _():
        s_scr[...] = s0_ref[0]

    ltri = (lax.broadcasted_iota(I32, (cs, cs), 0) >= lax.broadcasted_iota(I32, (cs, cs), 1)).astype(BF16)

    def chunk_gates(ch):
        bg = bg_ref[ch * cs:(ch + 1) * cs, :]
        g1 = bg.astype(BF16)
        r1 = bg - g1.astype(F32)
        g2 = r1.astype(BF16)
        g3 = (r1 - g2.astype(F32)).astype(BF16)
        gc = _dot(ltri, g1) + _dot(ltri, g2) + _dot(ltri, g3)
        return bg, gc, gc.T

    gates = [chunk_gates(ch) for ch in range(n_chunks)]
    state_ready = {}

    gh = DELTA_STACK
    rows = gh * cs
    rr = lax.broadcasted_iota(I32, (rows, rows), 0)
    cc = lax.broadcasted_iota(I32, (rows, rows), 1)
    same = (rr // cs) == (cc // cs)
    causal = jnp.logical_and(same, rr >= cc)
    strict = jnp.logical_and(same, rr > cc)
    eye = (rr == cc).astype(F32)
    row_head = lax.broadcasted_iota(I32, (rows, 1), 0) // cs
    def group_steps(ch, grp):
        heads = [grp * gh + j for j in range(gh)]
        bg, gc, gct = gates[ch]
        r0 = ch * cs
        stack = lambda ref: jnp.concatenate([ref[r0:r0 + cs, h * HEAD_DIM:(h + 1) * HEAD_DIM] for h in heads],
                                            axis=0)
        col = lambda a, lane0: jnp.concatenate([a[:, lane0 + h:lane0 + h + 1] for h in heads], axis=0)
        k = stack(kn_ref)
        q = stack(qn_ref)
        v = stack(vv_ref)
        bcol = col(bg, 0)
        gcc = col(gc, N_DELTA_HEADS)
        gcr = jnp.concatenate([gct[N_DELTA_HEADS + h:N_DELTA_HEADS + h + 1, :] for h in heads], axis=1)
        g_last = [gc[cs - 1:cs, N_DELTA_HEADS + h:N_DELTA_HEADS + h + 1] for h in heads]
        glc = jnp.concatenate([jnp.broadcast_to(gl, (cs, 1)) for gl in g_last], axis=0)
        decay = jnp.exp(jnp.where(causal, gcc - gcr, -jnp.inf))
        kb = k * bcol
        eg = jnp.exp(gcc)
        kq = _mm_nt(jnp.concatenate([kb, q], axis=0), k)
        yield
        a = jnp.where(strict, kq[0:rows] * decay, 0.0)
        intra = jnp.where(causal, kq[rows:2 * rows] * decay, 0.0)
        x = -a
        nb = DELTA_INV_BLOCK
        y = jnp.where((rr // nb) == (cc // nb), x, 0.0)
        p = eye + y
        y = _mm(y, y)
        yield
        n_sq = max(1, int(nb - 1).bit_length())
        for lvl in range(1, n_sq):
            if lvl < n_sq - 1:
                py = _mm(jnp.concatenate([p, y], axis=0), y)
                p = p + py[0:rows]
                y = py[rows:2 * rows]
            else:
                p = p + _mm(p, y)
            yield
        size = 2 * nb
        while size <= cs:
            off = jnp.where(jnp.logical_and((rr // size) == (cc // size), (rr // (size // 2)) != (cc // (size // 2))),
                            x, 0.0)
            po = _mm(p, off)
            yield
            p = p + _mm(po, p)
            yield
            size *= 2
        sol = _mm(p, jnp.concatenate([v * bcol, kb * eg], axis=1))
        yield
        u = sol[:, 0:HEAD_DIM]
        w = sol[:, HEAD_DIM:2 * HEAD_DIM]
        lanes_g = slice(grp * gh * HEAD_DIM, (grp + 1) * gh * HEAD_DIM)
        while ch > 0 and not state_ready.get((ch - 1, grp)):
            yield
        s_g = s_scr[:, lanes_g]
        wq_s = _mm(jnp.concatenate([w, q * eg], axis=0), s_g)
        yield
        own = lambda m, r0: jnp.concatenate(
            [m[r0 + j * cs:r0 + (j + 1) * cs, j * HEAD_DIM:(j + 1) * HEAD_DIM] for j in range(gh)], axis=0)
        v_new = u - own(wq_s, 0)
        o = own(wq_s, rows) + _mm(intra, v_new)
        yield
        kg_t = (k * jnp.exp(glc - gcc)).T
        vn_blocks = jnp.concatenate([jnp.where(row_head == j, v_new, 0.0) for j in range(gh)], axis=1)
        s_decay = jnp.concatenate([jnp.broadcast_to(jnp.exp(gl), (1, HEAD_DIM)) for gl in g_last], axis=1)
        s_scr[:, lanes_g] = s_g * s_decay + _mm(kg_t, vn_blocks)
        state_ready[(ch, grp)] = True
        yield
        on = o * lax.rsqrt(jnp.mean(o * o, axis=-1, keepdims=True) + EPS) * ow_ref[...]
        for j, h in enumerate(heads):
            sl = slice(h * HEAD_DIM, (h + 1) * HEAD_DIM)
            od_ref[r0:r0 + cs, sl] = (on[j * cs:(j + 1) * cs] * _silu(z_ref[r0:r0 + cs, sl])).astype(BF16)

    chains = [group_steps(ch, grp) for ch in range(n_chunks) for grp in range(N_DELTA_HEADS // gh)]
    while chains:
        chains = [g for g in chains if next(g, "done") != "done"]

    so_ref[0] = s_scr[...]


def _delta_chunks(qn, kn, vv, bg, zsrc, z_col_block, state0, o_norm_w, b, t):
    n = b * t
    per_step = DELTA_CHUNKS_PER_STEP if (t // DELTA_CHUNK) % DELTA_CHUNKS_PER_STEP == 0 else 1
    nc = t // (DELTA_CHUNK * per_step)
    sw = N_DELTA_HEADS * HEAD_DIM
    row = lambda w, cb=0: pl.BlockSpec((DELTA_CHUNK * per_step, w), lambda bb, c: (bb * nc + c, cb))
    st = pl.BlockSpec((1, HEAD_DIM, sw), lambda bb, c: (bb, 0, 0))
    s_in = state0.transpose(0, 2, 1, 3).reshape(b, HEAD_DIM, sw)
    od, s_out = pl.pallas_call(
        functools.partial(_delta_chunk_kernel, n_chunks=per_step),
        name="delta_chunks",
        grid=(b, nc),
        in_specs=[row(DELTA_WIDTH), row(DELTA_WIDTH), row(DELTA_WIDTH), row(LANES),
                  row(DELTA_WIDTH, z_col_block), st, pl.BlockSpec((1, LANES), lambda bb, c: (0, 0))],
        out_specs=[row(DELTA_WIDTH), st],
        out_shape=[jax.ShapeDtypeStruct((n, DELTA_WIDTH), BF16),
                   jax.ShapeDtypeStruct((b, HEAD_DIM, sw), F32)],
        scratch_shapes=[pltpu.VMEM((HEAD_DIM, sw), F32)],
        compiler_params=_cparams(("parallel", "arbitrary")),
    )(qn, kn, vv, bg, zsrc, s_in, o_norm_w.reshape(1, LANES))
    return od, s_out.reshape(b, HEAD_DIM, N_DELTA_HEADS, HEAD_DIM).transpose(0, 2, 1, 3)


def _outproj_kernel(oap_ref, odp_ref, xp_ref, g1p_ref, shp_ref, scp_ref,
                    oas_ref, ods_ref, xs_ref, g1s_ref, shs_ref, scs_ref,
                    nw_ref, wo_ref, wrh_ref, br_ref, x1_ref, h2_ref, eid_ref, gate_ref, *, prompt_tiles):
    def body(oa_ref, od_ref, x_ref, g1_ref, sh_ref, sc_ref):
        mix = (_dot(oa_ref[...], wo_ref[0:ATTN_WIDTH, :])
               + _dot(od_ref[...], wo_ref[ATTN_WIDTH:ATTN_WIDTH + DELTA_WIDTH, :]))
        x1 = x_ref[...] + g1_ref[0] * mix
        x1_ref[...] = x1
        y = x1 * lax.rsqrt(jnp.mean(x1 * x1, axis=-1, keepdims=True) + EPS) * nw_ref[...]
        h2 = y * (1.0 + sc_ref[0]) + sh_ref[0]
        h2_ref[...] = h2
        hb = h2.astype(BF16)
        lo = (h2 - hb.astype(F32)).astype(BF16)
        wr = wrh_ref[...]
        both = _dot(hb, wr)
        logits = both[:, 0:LANES] + both[:, LANES:2 * LANES] + _dot(lo, wr[:, 0:LANES]) + br_ref[...]
        eid_ref[...], gate_ref[...] = _route_tile(logits)

    i = pl.program_id(0)
    pl.when(i < prompt_tiles)(functools.partial(body, oap_ref, odp_ref, xp_ref, g1p_ref, shp_ref, scp_ref))
    pl.when(i >= prompt_tiles)(functools.partial(body, oas_ref, ods_ref, xs_ref, g1s_ref, shs_ref, scs_ref))


def _out_projection(prompt, sample, norm2_w, w_out_bf, wr_both, b_rt, tm, prompt_seq_len):
    n_p, d = prompt[2].shape
    n_s = sample[2].shape[0]
    pt = n_p // tm
    tiles_per_seq = prompt_seq_len // tm
    p_row = lambda w: pl.BlockSpec((tm, w), lambda i: (jnp.minimum(i, pt - 1), 0))
    s_row = lambda w: pl.BlockSpec((tm, w), lambda i: (jnp.maximum(i - pt, 0), 0))
    p_mod = pl.BlockSpec((1, 1, d), lambda i: (jnp.minimum(i, pt - 1) // tiles_per_seq, 0, 0))
    s_mod = pl.BlockSpec((1, tm, d), lambda i: (jnp.maximum(i - pt, 0), 0, 0))
    row = lambda w: pl.BlockSpec((tm, w), lambda i: (i, 0))
    full = lambda a: pl.BlockSpec(a.shape, lambda i: (0, 0))
    n = n_p + n_s
    return pl.pallas_call(
        functools.partial(_outproj_kernel, prompt_tiles=pt),
        name="out_proj",
        grid=(n // tm,),
        in_specs=[p_row(ATTN_WIDTH), p_row(DELTA_WIDTH), p_row(d), p_mod, p_mod, p_mod,
                  s_row(ATTN_WIDTH), s_row(DELTA_WIDTH), s_row(d), s_mod, s_mod, s_mod,
                  pl.BlockSpec((1, d), lambda i: (0, 0)), full(w_out_bf), full(wr_both), full(b_rt)],
        out_specs=[row(d), row(d), row(LANES), row(LANES)],
        out_shape=[jax.ShapeDtypeStruct((n, d), F32), jax.ShapeDtypeStruct((n, d), F32),
                   jax.ShapeDtypeStruct((n, LANES), I32), jax.ShapeDtypeStruct((n, LANES), F32)],
        compiler_params=_cparams(("parallel",)),
    )(*prompt, *sample, norm2_w.reshape(1, d), w_out_bf, wr_both, b_rt)


def _route_tile(x):
    lane = lax.broadcasted_iota(I32, x.shape, 1)
    gl = jnp.where(lane < N_GROUPS, x, -jnp.inf)
    ge = jnp.exp(gl - jnp.max(gl, axis=1, keepdims=True))
    p = ge / jnp.sum(ge, axis=1, keepdims=True)
    p_max = jnp.max(p, axis=1, keepdims=True)
    grp = jnp.min(jnp.where(p == p_max, lane, LANES), axis=1, keepdims=True)
    e_lane = lane - N_GROUPS
    in_grp = jnp.logical_and(jnp.logical_and(e_lane >= 0, e_lane < N_EXPERTS),
                             (e_lane >> 3) == grp)
    rl = jnp.where(in_grp, x, -jnp.inf)
    v1 = jnp.max(rl, axis=1, keepdims=True)
    i1 = jnp.min(jnp.where(rl == v1, lane, LANES), axis=1, keepdims=True)
    rl2 = jnp.where(lane == i1, -jnp.inf, rl)
    v2 = jnp.max(rl2, axis=1, keepdims=True)
    i2 = jnp.min(jnp.where(rl2 == v2, lane, LANES), axis=1, keepdims=True)
    t = jnp.exp(v2 - v1)
    den = 1.0 + t
    eid = jnp.where(lane == 0, i1 - N_GROUPS, jnp.where(lane == 1, i2 - N_GROUPS, 0))
    gate = jnp.where(lane == 0, (1.0 / den) * p_max, jnp.where(lane == 1, (t / den) * p_max, 0.0))
    return eid, gate


def _row_gather(idx_ref, base, n_rows, src_hbm, dst, sem):
    def body(r, carry):
        pltpu.make_async_copy(src_hbm.at[pl.ds(idx_ref[base + r], 1), :], dst.at[pl.ds(r, 1), :], sem).start()
        return carry

    lax.fori_loop(0, n_rows, body, 0, unroll=8)


def _row_gather_wait(n_rows, src_hbm, dst, sem):
    pltpu.make_async_copy(src_hbm.at[pl.ds(0, n_rows), :], dst, sem).wait()


def _moe_kernel(tok_ref, j0_ref, be_ref, na_ref, h_hbm, wg_ref, wu_ref, wd_ref, o_ref,
                x_even, x_odd, sem, wg_scr, wu_scr, wd_scr, *, bm):
    i = pl.program_id(0)
    n_act = na_ref[0]
    bufs = ((x_even, sem.at[0]), (x_odd, sem.at[1]))

    @pl.when(i == 0)
    def _():
        _row_gather(tok_ref, j0_ref[0], bm, h_hbm, x_even, sem.at[0])

    changed = jnp.logical_or(i == 0, be_ref[i] != be_ref[jnp.maximum(i - 1, 0)])

    @pl.when(jnp.logical_and(i < n_act, changed))
    def _():
        wg_scr[...] = wg_ref[0].astype(BF16)
        wu_scr[...] = wu_ref[0].astype(BF16)
        wd_scr[...] = wd_ref[0].astype(BF16)

    for parity in range(2):
        cur, cur_sem = bufs[parity]
        nxt, nxt_sem = bufs[1 - parity]

        @pl.when(jnp.logical_and(i < n_act, i % 2 == parity))
        def _():
            _row_gather_wait(bm, h_hbm, cur, cur_sem)
            base = j0_ref[i + 1]
            for r in range(bm):
                pltpu.make_async_copy(h_hbm.at[pl.ds(tok_ref[base + r], 1), :], nxt.at[pl.ds(r, 1), :],
                                      nxt_sem).start()
            x = cur[...].astype(BF16)
            hid = _silu(_dot(x, wg_scr[...])) * _dot(x, wu_scr[...])
            o_ref[...] = _dot(hid.astype(BF16), wd_scr[...])

        @pl.when(jnp.logical_and(i == n_act, i % 2 == parity))
        def _():
            _row_gather_wait(bm, h_hbm, cur, cur_sem)

    @pl.when(i >= n_act)
    def _():
        o_ref[...] = jnp.zeros(o_ref.shape, F32)


def _moe_experts(tok_sorted, block_j0, block_exp, n_active, h2, w_gate, w_up, w_down, bm):
    ns = block_exp.shape[0] * bm
    d = h2.shape[1]
    f = w_gate.shape[2]
    grid_spec = pltpu.PrefetchScalarGridSpec(
        num_scalar_prefetch=4,
        grid=(ns // bm,),
        in_specs=[pl.BlockSpec(memory_space=pl.ANY),
                  pl.BlockSpec((1, d, f), lambda i, tok, j0, be, na: (be[i], 0, 0)),
                  pl.BlockSpec((1, d, f), lambda i, tok, j0, be, na: (be[i], 0, 0)),
                  pl.BlockSpec((1, f, d), lambda i, tok, j0, be, na: (be[i], 0, 0))],
        out_specs=pl.BlockSpec((bm, d), lambda i, tok, j0, be, na: (i, 0)),
        scratch_shapes=[pltpu.VMEM((bm, d), F32), pltpu.VMEM((bm, d), F32), pltpu.SemaphoreType.DMA((2,)),
                        pltpu.VMEM((d, f), BF16), pltpu.VMEM((d, f), BF16), pltpu.VMEM((f, d), BF16)],
    )
    return pl.pallas_call(
        functools.partial(_moe_kernel, bm=bm),
        name="moe_experts",
        grid_spec=grid_spec,
        out_shape=jax.ShapeDtypeStruct((ns, d), F32),
        compiler_params=_cparams(("arbitrary",)),
    )(tok_sorted, block_j0, block_exp, n_active, h2, w_gate, w_up, w_down)


def _combine_kernel(dest_ref, x1_ref, gt_ref, g2p_ref, g2s_ref, y_hbm, op_ref, os_ref, y_buf, sem, *,
                    tm, prompt_tiles):
    i = pl.program_id(0)
    n = pl.num_programs(0)
    slot = i % 2

    @pl.when(i == 0)
    def _():
        _row_gather(dest_ref, 0, 2 * tm, y_hbm, y_buf.at[0], sem.at[0])

    for parity in range(2):
        @pl.when(jnp.logical_and(i + 1 < n, slot == parity))
        def _():
            base = (i + 1) * 2 * tm
            for r in range(2 * tm):
                pltpu.make_async_copy(y_hbm.at[pl.ds(dest_ref[base + r], 1), :],
                                      y_buf.at[1 - parity, pl.ds(r, 1), :], sem.at[1 - parity]).start()

    _row_gather_wait(2 * tm, y_hbm, y_buf.at[slot], sem.at[slot])
    gt = gt_ref[...]
    y = y_buf[slot, 0:tm, :] * gt[:, 0:1] + y_buf[slot, tm:2 * tm, :] * gt[:, 1:2]

    @pl.when(i < prompt_tiles)
    def _():
        op_ref[...] = x1_ref[...] + g2p_ref[0] * y

    @pl.when(i >= prompt_tiles)
    def _():
        os_ref[...] = x1_ref[...] + g2s_ref[0] * y


def _combine(x1, dest, y_rows, gates, gate2_p, gate2_s, tm, n_prompt, prompt_seq_len):
    n, d = x1.shape
    pt = n_prompt // tm
    tiles_per_seq = prompt_seq_len // tm
    dest_tiles = dest.reshape(n // tm, tm, 2).transpose(0, 2, 1).reshape(-1)
    row = lambda w: pl.BlockSpec((tm, w), lambda i, dst: (i, 0))
    grid_spec = pltpu.PrefetchScalarGridSpec(
        num_scalar_prefetch=1,
        grid=(n // tm,),
        in_specs=[row(d), row(LANES),
                  pl.BlockSpec((1, 1, d), lambda i, dst: (jnp.minimum(i, pt - 1) // tiles_per_seq, 0, 0)),
                  pl.BlockSpec((1, tm, d), lambda i, dst: (jnp.maximum(i - pt, 0), 0, 0)),
                  pl.BlockSpec(memory_space=pl.ANY)],
        out_specs=[pl.BlockSpec((tm, d), lambda i, dst: (jnp.minimum(i, pt - 1), 0)),
                   pl.BlockSpec((tm, d), lambda i, dst: (jnp.maximum(i - pt, 0), 0))],
        scratch_shapes=[pltpu.VMEM((2, 2 * tm, d), F32), pltpu.SemaphoreType.DMA((2,))],
    )
    return pl.pallas_call(
        functools.partial(_combine_kernel, tm=tm, prompt_tiles=pt),
        name="moe_combine",
        grid_spec=grid_spec,
        out_shape=[jax.ShapeDtypeStruct((n_prompt, d), F32), jax.ShapeDtypeStruct((n - n_prompt, d), F32)],
        compiler_params=_cparams(("arbitrary",)),
    )(dest_tiles, x1, gates, gate2_p, gate2_s, y_rows)


FRONT_SECTIONS = ((C_QA, ATTN_WIDTH), (C_QI, IDX_WIDTH),
                  (C_CONV, DELTA_WIDTH), (C_CONV + DELTA_WIDTH, DELTA_WIDTH), (C_CONV + 2 * DELTA_WIDTH, DELTA_WIDTH),
                  (C_KA, PROJ_PACKED - C_KA), (C_ZD, DELTA_WIDTH))


def _prompt_front_kernel(x_ref, sh_ref, sc_ref, nw_ref, w_ref, tabm_ref, tabi_ref, qw_ref, kw_ref, iw_ref,
                         prev_ref, cw_ref, al_ref, dt_ref,
                         q_ref, qi_ref, z_ref, qn_ref, kn_ref, vv_ref, kf_ref, kb_ref, vf_ref, vt_ref,
                         kif_ref, kx_ref, bg_ref, misc_ref, tail_ref,
                         carry_scr, xp_scr, *, tiles_per_seq, tm):
    i = pl.program_id(0)
    tile_in_seq = i % tiles_per_seq
    x = x_ref[...]
    y = x * lax.rsqrt(jnp.mean(x * x, axis=-1, keepdims=True) + EPS) * nw_ref[...]
    h = (y * (1.0 + sc_ref[0]) + sh_ref[0]).astype(BF16)
    tabm = tabm_ref[0]
    tabi = tabi_ref[0]
    half_main = HEAD_DIM // ROPE_FRACTION // 2
    half_idx = IDX_DIM // ROPE_FRACTION // 2

    @pl.when(tile_in_seq == 0)
    def _():
        carry_scr[...] = prev_ref[0]

    def queries(pj):
        for hh in range(N_ATTN_HEADS):
            sl = slice(hh * HEAD_DIM, (hh + 1) * HEAD_DIM)
            v = _rope(_rms_head(pj[:, sl], qw_ref[...]), tabm, half_main)
            q_ref[:, sl] = (v * (HEAD_DIM ** -0.5)).astype(BF16)

    def index_queries(pj):
        for p in range(IDX_WIDTH // LANES):
            sl = slice(p * LANES, (p + 1) * LANES)
            qi_ref[:, sl] = _rope(pj[:, sl], tabi, half_idx).astype(BF16)

    def gate_z(pj):
        z_ref[...] = pj

    def conv_section(sec, pj):
        cols = slice(sec * DELTA_WIDTH, (sec + 1) * DELTA_WIDTH)
        xp_scr[sec, 0:SUBLANES, :] = carry_scr[:, cols]
        xp_scr[sec, SUBLANES:SUBLANES + tm, :] = pj
        carry_scr[:, cols] = pj[tm - SUBLANES:tm, :]
        base = SUBLANES - (CONV_WIDTH - 1)
        out = (qn_ref, kn_ref, vv_ref)[sec]
        for hh in range(N_DELTA_HEADS):
            sl = slice(hh * HEAD_DIM, (hh + 1) * HEAD_DIM)
            wsl = slice(sec * DELTA_WIDTH + hh * HEAD_DIM, sec * DELTA_WIDTH + (hh + 1) * HEAD_DIM)
            v = xp_scr[sec, base:base + tm, sl] * cw_ref[0:1, wsl]
            for j in range(1, CONV_WIDTH):
                v = v + xp_scr[sec, base + j:base + j + tm, sl] * cw_ref[j:j + 1, wsl]
            v = _silu(v)
            if sec < 2:
                v = v * lax.rsqrt(jnp.sum(v * v, axis=-1, keepdims=True) + EPS)
            if sec == 0:
                v = v * (HEAD_DIM ** -0.5)
            out[:, sl] = v
        if sec == 2:
            tail_ref[0] = carry_scr[...]

    def keys_values_misc(pj):
        for hk in range(N_KV_HEADS):
            sl = slice(hk * HEAD_DIM, (hk + 1) * HEAD_DIM)
            v = _rope(_rms_head(pj[:, sl], kw_ref[...]), tabm, half_main)
            kf_ref[:, sl] = v
            kb_ref[:, sl] = v.astype(BF16)
        va = pj[:, KV_WIDTH:2 * KV_WIDTH]
        vf_ref[...] = va
        vt_ref[...] = va.T.astype(BF16)
        m = pj[:, 2 * KV_WIDTH:2 * KV_WIDTH + LANES]
        misc_ref[...] = m
        lane = lax.broadcasted_iota(I32, m.shape, 1)
        ki = jnp.where(lane < IDX_DIM, m, 0.0)
        ms = jnp.sum(ki * ki, axis=-1, keepdims=True) * (1.0 / IDX_DIM)
        v = _rope(ki * lax.rsqrt(ms + EPS) * iw_ref[...], tabi, half_idx)
        kif_ref[...] = v[:, 0:IDX_DIM]
        kx_ref[...] = (v + pltpu.roll(v, IDX_DIM, 1)).astype(BF16)
        beta = _sigmoid(m)
        g = -jnp.exp(al_ref[...]) * _softplus(m + dt_ref[...])
        is_b = jnp.logical_and(lane >= M_BD, lane < M_BD + N_DELTA_HEADS)
        is_g = jnp.logical_and(lane >= M_AD, lane < M_AD + N_DELTA_HEADS)
        comb = jnp.where(is_b, beta, jnp.where(is_g, g, 0.0))
        bg_ref[...] = pltpu.roll(comb, LANES - M_BD, 1)

    epilogues = (queries, index_queries,
                 functools.partial(conv_section, 0), functools.partial(conv_section, 1),
                 functools.partial(conv_section, 2), keys_values_misc, gate_z)
    project = lambda k: _dot(h, w_ref[:, FRONT_SECTIONS[k][0]:FRONT_SECTIONS[k][0] + FRONT_SECTIONS[k][1]])
    pj_next = project(0)
    for k, epilogue in enumerate(epilogues):
        pj = pj_next
        if k + 1 < len(epilogues):
            pj_next = project(k + 1)
        epilogue(pj)


def _prompt_front(x2d, shift, scale, norm_w, w_packed, pos, q_norm_w, k_norm_w, idx_k_norm_w, prev8, conv_w,
                  a_log, dt_bias, b, t, tm):
    n, d = x2d.shape
    np_ = w_packed.shape[1]
    tiles_per_seq = t // tm
    tabm = _rope_tables(pos, HEAD_DIM, LANES).reshape(tiles_per_seq, tm, 3 * LANES)
    tabi = _rope_tables(pos, IDX_DIM, IDX_DIM).reshape(tiles_per_seq, tm, 3 * LANES)
    iw = jnp.concatenate([idx_k_norm_w, jnp.zeros((LANES - IDX_DIM,), F32)]).reshape(1, LANES)
    pad_vec = lambda v: jnp.zeros((1, LANES), F32).at[0, M_AD:M_AD + N_DELTA_HEADS].set(v)
    mod_spec = pl.BlockSpec((1, 1, d), lambda i: (i // tiles_per_seq, 0, 0))
    tab_spec = pl.BlockSpec((1, tm, 3 * LANES), lambda i: (i % tiles_per_seq, 0, 0))
    vec = lambda w: pl.BlockSpec((1, w), lambda i: (0, 0))
    row = lambda w: pl.BlockSpec((tm, w), lambda i: (i, 0))
    seq_state = pl.BlockSpec((1, SUBLANES, CONV_CHANNELS), lambda i: (i // tiles_per_seq, 0, 0))
    widths = [(ATTN_WIDTH, BF16), (IDX_WIDTH, BF16), (DELTA_WIDTH, F32), (DELTA_WIDTH, F32), (DELTA_WIDTH, F32),
              (DELTA_WIDTH, F32), (KV_WIDTH, F32), (KV_WIDTH, BF16), (KV_WIDTH, F32)]
    outs = pl.pallas_call(
        functools.partial(_prompt_front_kernel, tiles_per_seq=tiles_per_seq, tm=tm),
        name="prompt_front",
        grid=(n // tm,),
        in_specs=[row(d), mod_spec, mod_spec, vec(d),
                  pl.BlockSpec((d, np_), lambda i: (0, 0), pipeline_mode=pl.Buffered(1)),
                  tab_spec, tab_spec, vec(LANES), vec(LANES), vec(LANES), seq_state,
                  pl.BlockSpec((CONV_WIDTH, CONV_CHANNELS), lambda i: (0, 0)), vec(LANES), vec(LANES)],
        out_specs=[row(w) for w, _ in widths]
                  + [pl.BlockSpec((KV_WIDTH, tm), lambda i: (0, i)), row(IDX_DIM), row(LANES), row(LANES), row(LANES),
                     seq_state],
        out_shape=[jax.ShapeDtypeStruct((n, w), dt) for w, dt in widths]
                  + [jax.ShapeDtypeStruct((KV_WIDTH, n), BF16), jax.ShapeDtypeStruct((n, IDX_DIM), F32),
                     jax.ShapeDtypeStruct((n, LANES), BF16), jax.ShapeDtypeStruct((n, LANES), F32),
                     jax.ShapeDtypeStruct((n, LANES), F32),
                     jax.ShapeDtypeStruct((b, SUBLANES, CONV_CHANNELS), F32)],
        scratch_shapes=[pltpu.VMEM((SUBLANES, CONV_CHANNELS), F32),
                        pltpu.VMEM((3, SUBLANES + tm, DELTA_WIDTH), F32)],
        compiler_params=pltpu.CompilerParams(dimension_semantics=("arbitrary",),
                                             vmem_limit_bytes=FRONT_VMEM_LIMIT),
    )(x2d, shift, scale, norm_w.reshape(1, d), w_packed, tabm, tabi,
      q_norm_w.reshape(1, LANES), k_norm_w.reshape(1, LANES), iw, prev8, conv_w, pad_vec(a_log), pad_vec(dt_bias))
    names = ("q", "qi", "z", "qn", "kn", "vv", "kf", "kb", "vf", "vt", "kif", "kx", "bg", "misc", "tail")
    return dict(zip(names, outs))


def _pick_tile(n, pref, mult=16):
    t = min(pref, n)
    while n % t or t % mult:
        t -= 1
    return t


def _pack_w_in(w_in):
    d = w_in.shape[0]
    bounds = np.cumsum(PROJ_SIZES)[:-1].tolist()
    qa, ka, va, qi, ki, wi, qd, kd, vd, zd, bd, ad = jnp.split(w_in, bounds, axis=1)
    used = IDX_DIM + N_IDX_HEADS + 2 * N_DELTA_HEADS
    misc = jnp.concatenate([ki, wi, bd, ad, jnp.zeros((d, LANES - used), w_in.dtype)], axis=1)
    cols = [qa, qi, zd, qd, kd, vd, ka, va, misc]
    width = sum(c.shape[1] for c in cols)
    cols.append(jnp.zeros((d, PROJ_PACKED - width), w_in.dtype))
    return jnp.concatenate(cols, axis=1).astype(BF16)


def _route_and_sort(eid, bm):
    n = eid.shape[0]
    nk = 2 * n
    flat_e = eid.reshape(-1)
    order = jnp.argsort(flat_e, stable=True).astype(I32)
    inv = jnp.argsort(order).astype(I32)
    onehot = flat_e[:, None] == jnp.arange(N_EXPERTS, dtype=I32)[None, :]
    counts = jnp.sum(onehot.astype(I32), axis=0)
    padded = (counts + bm - 1) // bm * bm
    pad_end = jnp.cumsum(padded)
    shift = (pad_end - padded) - (jnp.cumsum(counts) - counts)
    dest = inv + jnp.sum(jnp.where(onehot, shift[None, :], 0), axis=1)
    n_blocks = -(-nk // bm) + N_EXPERTS + 1
    block_exp = jnp.minimum(jnp.searchsorted(pad_end, jnp.arange(n_blocks, dtype=I32) * bm, side='right'),
                            N_EXPERTS - 1).astype(I32)
    n_active = (pad_end[-1] // bm).astype(I32).reshape(1)
    block_j0 = jnp.clip(jnp.arange(n_blocks, dtype=I32) * bm - shift[block_exp], 0, nk)
    tok_sorted = jnp.concatenate([order // 2, jnp.zeros((bm,), I32)])
    return tok_sorted, block_j0, dest.astype(I32).reshape(n, 2), block_exp, n_active


def _layer(layer, yp, ys, cache_k, cache_v, cache_idx, state_ssm, state_conv, page_table, c_prompt, c_sample,
           w_in, w_out, conv_w, a_log, dt_bias, q_norm_w, k_norm_w, idx_k_norm_w, o_norm_w, norm1_w, norm2_w,
           w_ada, b_ada, w_group, b_group, w_router, b_router, w_gate, w_up, w_down):
    bp, tp, d = yp.shape
    bs, ts, _ = ys.shape
    past = page_table.shape[1] * PAGE_SIZE
    rows = SAMPLE_ROWS
    assert CONV_WIDTH - 1 <= ts <= SUBLANES <= rows and tp % SEL_SPAN == 0 and tp % DELTA_CHUNK == 0

    n_c = bp + bs
    n_c_pad = -(-n_c // SUBLANES) * SUBLANES
    c_all = jnp.concatenate([c_prompt, c_sample, jnp.zeros((n_c_pad - n_c, d), F32)], axis=0)
    mod = _ada_modulation(c_all, w_ada, b_ada)
    mods = jnp.split(mod, N_MOD, axis=1)
    mp = [m[:bp].reshape(bp, 1, d) for m in mods]
    ms = [jnp.repeat(m[bp:bp + bs], rows, axis=0).reshape(1, bs * rows, d) for m in mods]

    w_packed = _pack_w_in(w_in)
    w_out_bf = w_out.astype(BF16)
    w_rt = jnp.concatenate([w_group, w_router, jnp.zeros((d, LANES - N_GROUPS - N_EXPERTS), F32)], axis=1)
    wr_hi = w_rt.astype(BF16)
    wr_both = jnp.concatenate([wr_hi, (w_rt - wr_hi.astype(F32)).astype(BF16)], axis=1)
    b_rt = jnp.concatenate([b_group, b_router, jnp.zeros((LANES - N_GROUPS - N_EXPERTS,), F32)]).reshape(1, LANES)

    np_ = bp * tp
    xp2 = yp.reshape(np_, d)
    tm_p = _pick_tile(tp, 256)
    fr = _prompt_front(xp2, mp[0], mp[1], norm1_w, w_packed, jnp.arange(tp), q_norm_w, k_norm_w, idx_k_norm_w,
                       jnp.zeros((bp, SUBLANES, CONV_CHANNELS), F32), conv_w, a_log, dt_bias, bp, tp, tm_p)
    kf_p, kif_p = fr["kf"], fr["kif"]
    oa_p = _dsa_prompt(fr["q"], fr["qi"], fr["misc"], fr["kb"], fr["vt"], fr["kx"], bp, tp)
    od_p, ssm_p = _delta_chunks(fr["qn"], fr["kn"], fr["vv"], fr["bg"], fr["z"], 0,
                                jnp.zeros((bp, N_DELTA_HEADS, HEAD_DIM, HEAD_DIM), F32), o_norm_w, bp, tp)

    ns_ = bs * rows
    xs2 = jnp.pad(ys, ((0, 0), (0, rows - ts), (0, 0))).reshape(ns_, d)
    tm_s = _pick_tile(ns_, 256)
    proj_s = _in_projection(xs2, ms[0].reshape(ns_ // tm_s, tm_s, d), ms[1].reshape(ns_ // tm_s, tm_s, d),
                            norm1_w, w_packed, tm_s, tm_s)
    q_s, kf_s, kb_s, vb_s, qi_s, kif_s, _ = _attention_prep(
        proj_s, past + jnp.arange(rows), rows, q_norm_w, k_norm_w, idx_k_norm_w, False)
    q_t = qi_s.reshape(bs, rows, N_IDX_HEADS, IDX_DIM).transpose(0, 2, 1, 3).reshape(bs, N_IDX_HEADS * rows, IDX_DIM)
    w_col = proj_s[:, C_MISC + M_WI:C_MISC + M_WI + N_IDX_HEADS].reshape(bs, rows, N_IDX_HEADS)
    w_col = w_col.transpose(0, 2, 1).reshape(bs, N_IDX_HEADS * rows, 1)
    pages = _pick_tile(page_table.shape[1], SAMPLE_PAGES_PER_STEP, 1)
    keys_past, keys_new = _sample_scores(page_table, q_t, w_col, kif_s, cache_idx, layer, pages, ts)
    n_sel_s = min(TOPK_MAX, (past + ts) // 4)
    oa_s = _sample_attend(page_table, keys_past, keys_new, q_s, kb_s, vb_s, cache_k, cache_v, layer, pages, n_sel_s)
    prev8 = jnp.pad(state_conv, ((0, 0), (SUBLANES - (CONV_WIDTH - 1), 0), (0, 0)))
    qn_s, kn_s, vv_s, bg_s = _delta_prep(proj_s, prev8, conv_w, a_log, dt_bias, bs, rows, rows, ts)
    to_chunk = lambda a: jnp.pad(a.reshape(bs, rows, -1), ((0, 0), (0, DELTA_CHUNK - rows), (0, 0))).reshape(
        bs * DELTA_CHUNK, -1)
    z_s = proj_s[:, C_ZD:C_ZD + DELTA_WIDTH]
    od_s, ssm_s = _delta_chunks(to_chunk(qn_s), to_chunk(kn_s), to_chunk(vv_s), to_chunk(bg_s), to_chunk(z_s), 0,
                                state_ssm, o_norm_w, bs, DELTA_CHUNK)
    od_s = od_s.reshape(bs, DELTA_CHUNK, DELTA_WIDTH)[:, :rows].reshape(ns_, DELTA_WIDTH)

    tm_o = _pick_tile(tp, tm_s)
    assert ns_ % tm_o == 0
    per_tok = lambda m: m.reshape(ns_ // tm_o, tm_o, d)
    n_all = np_ + ns_
    x1_all, h2_all, eid, gates = _out_projection(
        (oa_p, od_p, xp2, mp[2], mp[3], mp[4]),
        (oa_s, od_s, xs2, per_tok(ms[2]), per_tok(ms[3]), per_tok(ms[4])),
        norm2_w, w_out_bf, wr_both, b_rt, tm_o, tp)
    bm = 256
    tok_sorted, block_j0, dest, block_exp, n_active = _route_and_sort(eid[:, 0:2], bm)
    yb = _moe_experts(tok_sorted, block_j0, block_exp, n_active, h2_all, w_gate, w_up, w_down, bm)
    out_p, out_s = _combine(x1_all, dest, yb, gates, mp[5], per_tok(ms[5]), tm_o, np_, tp)

    valid = lambda a: a.reshape(bs, rows, -1)[:, :ts]
    conv_p = fr["tail"][:, SUBLANES - (CONV_WIDTH - 1):]
    conv_s = proj_s.reshape(bs, rows, PROJ_PACKED)[:, ts - (CONV_WIDTH - 1):ts, C_CONV:C_CONV + CONV_CHANNELS]
    return (out_p.reshape(bp, tp, d), valid(out_s),
            kf_p.reshape(bp, tp, N_KV_HEADS, HEAD_DIM),
            fr["vf"].reshape(bp, tp, N_KV_HEADS, HEAD_DIM),
            kif_p.reshape(bp, tp, IDX_DIM), ssm_p, conv_p,
            valid(kf_s).reshape(bs, ts, N_KV_HEADS, HEAD_DIM),
            valid(proj_s[:, C_VA:C_VA + KV_WIDTH]).reshape(bs, ts, N_KV_HEADS, HEAD_DIM),
            valid(kif_s), ssm_s, conv_s)


def kernel(x_prompt, x_sample, cache_k, cache_v, cache_idx_k, state_ssm, state_conv, page_table, c_prompt, c_sample,
           w_in, w_out, conv_w, a_log, dt_bias, q_norm_w, k_norm_w, idx_k_norm_w, o_norm_w, norm1_w, norm2_w,
           w_ada, b_ada, w_group, b_group, w_router, b_router, w_gate, w_up, w_down):
    depth = w_in.shape[0]
    yp, ys = x_prompt, x_sample
    per_layer = []
    for l in range(depth):
        res = _layer(l, yp, ys, cache_k, cache_v, cache_idx_k, state_ssm[l], state_conv[l], page_table,
                     c_prompt, c_sample, w_in[l], w_out[l], conv_w[l], a_log[l], dt_bias[l], q_norm_w[l],
                     k_norm_w[l], idx_k_norm_w[l], o_norm_w[l], norm1_w[l], norm2_w[l], w_ada[l], b_ada[l],
                     w_group[l], b_group[l], w_router[l], b_router[l], w_gate[l], w_up[l], w_down[l])
        yp, ys = res[0], res[1]
        per_layer.append(res[2:])
    stacked = tuple(jnp.stack([pl_[j] for pl_ in per_layer]) for j in range(10))
    return (yp, ys) + stacked
```

```python
import functools

import jax
import jax.numpy as jnp
import numpy as np
from jax import lax
from jax.experimental import pallas as pl
from jax.experimental.pallas import tpu as pltpu

F32 = jnp.float32
BF16 = jnp.bfloat16
I32 = jnp.int32

HEAD_DIM = 128
N_ATTN_HEADS = 8
N_KV_HEADS = 2
KV_GROUP = N_ATTN_HEADS // N_KV_HEADS
N_DELTA_HEADS = 8
N_IDX_HEADS = 16
IDX_DIM = 64
ATTN_WIDTH = N_ATTN_HEADS * HEAD_DIM
KV_WIDTH = N_KV_HEADS * HEAD_DIM
DELTA_WIDTH = N_DELTA_HEADS * HEAD_DIM
IDX_WIDTH = N_IDX_HEADS * IDX_DIM
CONV_CHANNELS = 3 * DELTA_WIDTH
TOPK_MAX = 256
ROPE_THETA = 500000.0
ROPE_FRACTION = 4
CONV_WIDTH = 4
DELTA_CHUNK = 64
N_GROUPS = 8
EXPERTS_PER_GROUP = 8
N_EXPERTS = N_GROUPS * EXPERTS_PER_GROUP
N_MOD = 6
EPS = 1e-6
PAGE_SIZE = 128
PROJ_SIZES = (ATTN_WIDTH, KV_WIDTH, KV_WIDTH, IDX_WIDTH, IDX_DIM, N_IDX_HEADS,
              DELTA_WIDTH, DELTA_WIDTH, DELTA_WIDTH, DELTA_WIDTH, N_DELTA_HEADS, N_DELTA_HEADS)

LANES = 128
SUBLANES = 8
VMEM_LIMIT = 56 * 1024 * 1024
FRONT_VMEM_LIMIT = 60 * 1024 * 1024

C_QA = 0
C_QI = 1024
C_ZD = 2048
C_CONV = 3072
C_KA = 6144
C_VA = 6400
C_MISC = 6656
PROJ_PACKED = 6912
M_KI = 0
M_WI = 64
M_BD = 80
M_AD = 88

Q_TILE = 128
KEY_CHUNK = 256
SEL_SPAN = 512
NEG_INF_KEY = -2139095041
SAMPLE_ROWS = 16
SAMPLE_QROWS = SUBLANES
SAMPLE_PAGES_PER_STEP = 16
NEG_BIG = -1e30
INT_MIN = -2147483648
INT_MAX = 2147483647


def _cparams(sem):
    return pltpu.CompilerParams(dimension_semantics=sem, vmem_limit_bytes=VMEM_LIMIT)


def _dot(a, b):
    return jnp.dot(a, b, preferred_element_type=F32)


def _dot_nt(a, b):
    return lax.dot_general(a, b, (((1,), (1,)), ((), ())), preferred_element_type=F32)


def _dot_tn(a, b):
    return lax.dot_general(a, b, (((0,), (0,)), ((), ())), preferred_element_type=F32)


def _sigmoid(x):
    return 0.5 * jnp.tanh(0.5 * x) + 0.5


def _silu(x):
    return x * _sigmoid(x)


def _softplus(x):
    return jnp.maximum(x, 0.0) + jnp.log(1.0 + jnp.exp(-jnp.abs(x)))


def _ada_kernel(c_ref, w_ref, b_ref, o_ref):
    s = _silu(c_ref[...]).astype(BF16)
    o_ref[...] = _dot(s, w_ref[...].astype(BF16)) + b_ref[...]


def _ada_modulation(c, w_ada, b_ada):
    r, d = c.shape
    n = w_ada.shape[1]
    tn = 1024 if n % 1024 == 0 else n
    return pl.pallas_call(
        _ada_kernel,
        name="ada_mod",
        grid=(n // tn,),
        in_specs=[pl.BlockSpec((r, d), lambda j: (0, 0)),
                  pl.BlockSpec((d, tn), lambda j: (0, j)),
                  pl.BlockSpec((1, tn), lambda j: (0, j))],
        out_specs=pl.BlockSpec((r, tn), lambda j: (0, j)),
        out_shape=jax.ShapeDtypeStruct((r, n), F32),
        compiler_params=_cparams(("parallel",)),
    )(c, w_ada, b_ada.reshape(1, n))


INPROJ_COLS = 1152


def _inproj_kernel(x_ref, sh_ref, sc_ref, nw_ref, w_ref, o_ref):
    x = x_ref[...]
    y = x * lax.rsqrt(jnp.mean(x * x, axis=-1, keepdims=True) + EPS) * nw_ref[...]
    h = (y * (1.0 + sc_ref[0]) + sh_ref[0]).astype(BF16)
    for c0 in range(0, o_ref.shape[1], INPROJ_COLS):
        o_ref[:, c0:c0 + INPROJ_COLS] = _dot(h, w_ref[:, c0:c0 + INPROJ_COLS])


def _in_projection(x2d, shift, scale, norm_w, w_packed, tm, rows_per_mod_block):
    n, d = x2d.shape
    np_ = w_packed.shape[1]
    r = shift.shape[1]
    tiles_per_mod = rows_per_mod_block // tm
    mod_spec = pl.BlockSpec((1, r, d), lambda i: (i // tiles_per_mod, 0, 0))
    return pl.pallas_call(
        _inproj_kernel,
        name="in_proj",
        grid=(n // tm,),
        in_specs=[pl.BlockSpec((tm, d), lambda i: (i, 0)),
                  mod_spec, mod_spec,
                  pl.BlockSpec((1, d), lambda i: (0, 0)),
                  pl.BlockSpec((d, np_), lambda i: (0, 0), pipeline_mode=pl.Buffered(1))],
        out_specs=pl.BlockSpec((tm, np_), lambda i: (i, 0)),
        out_shape=jax.ShapeDtypeStruct((n, np_), F32),
        compiler_params=_cparams(("parallel",)),
    )(x2d, shift, scale, norm_w.reshape(1, d), w_packed)


def _rope(x, tab, rot):
    c = tab[:, 0:LANES]
    s1 = tab[:, LANES:2 * LANES]
    s2 = tab[:, 2 * LANES:3 * LANES]
    return x * c + pltpu.roll(x, LANES - rot, 1) * s1 + pltpu.roll(x, rot, 1) * s2


def _rms_head(x, w):
    return x * lax.rsqrt(jnp.mean(x * x, axis=-1, keepdims=True) + EPS) * w


def _prep_kernel(qa_ref, qi_ref, ka_ref, va_ref, misc_ref, tabm_ref, tabi_ref, qw_ref, kw_ref, iw_ref,
                 q_ref, kf_ref, kb_ref, vb_ref, qib_ref, kif_ref, kib_ref, *, transpose_v):
    tabm = tabm_ref[0]
    tabi = tabi_ref[0]
    half_main = HEAD_DIM // ROPE_FRACTION // 2
    half_idx = IDX_DIM // ROPE_FRACTION // 2
    for h in range(N_ATTN_HEADS):
        sl = slice(h * HEAD_DIM, (h + 1) * HEAD_DIM)
        y = _rope(_rms_head(qa_ref[:, sl], qw_ref[...]), tabm, half_main)
        q_ref[:, sl] = (y * (HEAD_DIM ** -0.5)).astype(BF16)
    for h in range(N_KV_HEADS):
        sl = slice(h * HEAD_DIM, (h + 1) * HEAD_DIM)
        y = _rope(_rms_head(ka_ref[:, sl], kw_ref[...]), tabm, half_main)
        kf_ref[:, sl] = y
        kb_ref[:, sl] = y.astype(BF16)
    if transpose_v:
        vb_ref[...] = va_ref[...].T.astype(BF16)
    else:
        vb_ref[...] = va_ref[...].astype(BF16)
    for p in range(IDX_WIDTH // LANES):
        sl = slice(p * LANES, (p + 1) * LANES)
        qib_ref[:, sl] = _rope(qi_ref[:, sl], tabi, half_idx).astype(BF16)
    m = misc_ref[...]
    lane = lax.broadcasted_iota(I32, m.shape, 1)
    ki = jnp.where(lane < IDX_DIM, m, 0.0)
    ms = jnp.sum(ki * ki, axis=-1, keepdims=True) * (1.0 / IDX_DIM)
    y = _rope(ki * lax.rsqrt(ms + EPS) * iw_ref[...], tabi, half_idx)
    kif_ref[...] = y[:, 0:IDX_DIM]
    kib_ref[...] = (y + pltpu.roll(y, IDX_DIM, 1)).astype(BF16)


def _rope_tables(pos, head_dim, group):
    d_rot = head_dim // ROPE_FRACTION
    half = d_rot // 2
    inv_freq = jnp.power(ROPE_THETA, -(jnp.arange(half, dtype=F32) * 2.0 / d_rot))
    ang = pos.astype(F32)[:, None] * inv_freq[None, :]
    cos = jnp.cos(ang)
    sin = jnp.sin(ang)
    t = pos.shape[0]
    z = jnp.zeros((t, group - d_rot), F32)
    c = jnp.concatenate([cos, cos, jnp.ones((t, group - d_rot), F32)], axis=1)
    s1 = jnp.concatenate([-sin, jnp.zeros((t, half), F32), z], axis=1)
    s2 = jnp.concatenate([jnp.zeros((t, half), F32), sin, z], axis=1)
    rep = LANES // group
    return jnp.concatenate([jnp.tile(c, (1, rep)), jnp.tile(s1, (1, rep)), jnp.tile(s2, (1, rep))], axis=1)


def _attention_prep(proj, pos, tq, q_norm_w, k_norm_w, idx_k_norm_w, transpose_v):
    n = proj.shape[0]
    p = pos.shape[0]
    g = p // tq
    tabm = _rope_tables(pos, HEAD_DIM, LANES).reshape(g, tq, 3 * LANES)
    tabi = _rope_tables(pos, IDX_DIM, IDX_DIM).reshape(g, tq, 3 * LANES)
    iw = jnp.concatenate([idx_k_norm_w, jnp.zeros((LANES - IDX_DIM,), F32)]).reshape(1, LANES)
    row = lambda w, c: pl.BlockSpec((tq, w), lambda i: (i, c // w))
    tab_spec = pl.BlockSpec((1, tq, 3 * LANES), lambda i: (i % g, 0, 0))
    vec_spec = pl.BlockSpec((1, LANES), lambda i: (0, 0))
    out_row = lambda w: pl.BlockSpec((tq, w), lambda i: (i, 0))
    v_spec = pl.BlockSpec((KV_WIDTH, tq), lambda i: (0, i)) if transpose_v else out_row(KV_WIDTH)
    v_shape = (KV_WIDTH, n) if transpose_v else (n, KV_WIDTH)
    return pl.pallas_call(
        functools.partial(_prep_kernel, transpose_v=transpose_v),
        name="attn_prep",
        grid=(n // tq,),
        in_specs=[row(ATTN_WIDTH, C_QA), row(IDX_WIDTH, C_QI), row(KV_WIDTH, C_KA), row(KV_WIDTH, C_VA),
                  row(LANES, C_MISC), tab_spec, tab_spec, vec_spec, vec_spec, vec_spec],
        out_specs=[out_row(ATTN_WIDTH), out_row(KV_WIDTH), out_row(KV_WIDTH), v_spec,
                   out_row(IDX_WIDTH), out_row(IDX_DIM), out_row(LANES)],
        out_shape=[jax.ShapeDtypeStruct((n, ATTN_WIDTH), BF16),
                   jax.ShapeDtypeStruct((n, KV_WIDTH), F32),
                   jax.ShapeDtypeStruct((n, KV_WIDTH), BF16),
                   jax.ShapeDtypeStruct(v_shape, BF16),
                   jax.ShapeDtypeStruct((n, IDX_WIDTH), BF16),
                   jax.ShapeDtypeStruct((n, IDX_DIM), F32),
                   jax.ShapeDtypeStruct((n, LANES), BF16)],
        compiler_params=_cparams(("parallel",)),
    )(proj, proj, proj, proj, proj, tabm, tabi,
      q_norm_w.reshape(1, LANES), k_norm_w.reshape(1, LANES), iw)


def _sort_key(x):
    b = pltpu.bitcast(x + 0.0, I32)
    return b ^ ((b >> 31) & INT_MAX)


def _kth_largest_key(count_ge, k, shape, n_total):
    def body(it, carry):
        ans_u, n_ge = carry
        bit = jnp.left_shift(jnp.int32(1), 31 - it)
        cand_u = ans_u | bit
        cnt = count_ge(cand_u ^ INT_MIN)
        ok = cnt >= k
        return jnp.where(ok, cand_u, ans_u), jnp.where(ok, cnt, n_ge)

    ans_u, n_ge = lax.fori_loop(0, 32, body, (jnp.zeros(shape, I32), jnp.full(shape, float(n_total), F32)))
    return ans_u ^ INT_MIN, n_ge


def _tie_index_limit(count_eq_le, need, n_keys, shape):
    nbits = max(1, int(n_keys - 1).bit_length())

    def body(it, lo):
        bit = jnp.left_shift(jnp.int32(1), nbits - 1 - it)
        cand = lo | bit
        cnt = count_eq_le(cand - 1)
        return jnp.where(cnt >= need, lo, cand)

    return lax.fori_loop(0, nbits, body, jnp.zeros(shape, I32))


def _dsa_prompt_kernel(q_ref, qi_ref, misc_ref, k_ref, vt_ref, kx_ref, o_ref,
                       key_scr, qsel_scr, qg_scr, thr_scr, lim_scr, m_scr, l_scr, acc_scr, *, n_sel):
    i = pl.program_id(1)
    tq = Q_TILE
    ck = KEY_CHUNK
    n_ch = (i * tq + tq + ck - 1) // ck
    q_pos = i * tq + lax.broadcasted_iota(I32, (1, tq), 1)
    row_k = lax.broadcasted_iota(I32, (ck, 1), 0)

    lo_half = lax.broadcasted_iota(I32, (tq, LANES), 1) < IDX_DIM
    zero = jnp.zeros((), BF16)
    for p in range(IDX_WIDTH // LANES):
        slab = qi_ref[:, p * LANES:(p + 1) * LANES]
        qsel_scr[(2 * p) * tq:(2 * p + 1) * tq, :] = jnp.where(lo_half, slab, zero)
        qsel_scr[(2 * p + 1) * tq:(2 * p + 2) * tq, :] = jnp.where(lo_half, zero, slab)
    w_t = misc_ref[...].T

    def score_chunk(c, carry):
        off = pl.multiple_of(c * ck, ck)
        s = _dot_nt(kx_ref[pl.ds(off, ck), :], qsel_scr[...])
        acc = jnp.zeros((ck, tq), F32)
        for h in range(N_IDX_HEADS):
            acc = acc + w_t[M_WI + h:M_WI + h + 1, :] * jnp.maximum(s[:, h * tq:(h + 1) * tq], 0.0)
        acc = jnp.where(off + row_k <= q_pos, acc, -jnp.inf)
        key_scr[pl.ds(off, ck), :] = _sort_key(acc)
        return carry

    lax.fori_loop(0, n_ch, score_chunk, 0)

    spc = SEL_SPAN // ck
    n_span = (n_ch + spc - 1) // spc
    neg_key = jnp.full((ck, tq), NEG_INF_KEY, I32)

    def pad_chunk(c, carry):
        key_scr[pl.ds(pl.multiple_of(c * ck, ck), ck), :] = neg_key
        return carry

    lax.fori_loop(n_ch, n_span * spc, pad_chunk, 0)

    thr_scr[...] = jnp.full((1, tq), INT_MIN, I32)
    lim_scr[...] = jnp.full((1, tq), INT_MAX, I32)

    def select_threshold(n_keys):
        def count_where(pred):
            tot = jnp.zeros((SUBLANES, tq), F32)
            for c0 in range(0, n_keys, ck):
                hit = pred(key_scr[c0:c0 + ck, :], c0 + row_k).astype(F32)
                tot = tot + jnp.sum(hit.reshape(ck // SUBLANES, SUBLANES, tq), axis=0)
            return jnp.sum(tot, axis=0, keepdims=True)

        t, n_ge = _kth_largest_key(lambda cand: count_where(lambda kk, pos: kk >= cand), float(n_sel), (1, tq),
                                   n_keys)
        thr_scr[...] = t

        @pl.when(jnp.max(n_ge) > float(n_sel))
        def _():
            n_gt = count_where(lambda kk, pos: kk > t)
            lim_scr[...] = _tie_index_limit(
                lambda idx: count_where(lambda kk, pos: jnp.logical_and(kk == t, pos <= idx)),
                float(n_sel) - n_gt, k_ref.shape[0], (1, tq))

    for spans in range(1, k_ref.shape[0] // SEL_SPAN + 1):
        if spans * SEL_SPAN > n_sel:
            pl.when(jnp.logical_and(n_span == spans, (i + 1) * tq > n_sel))(
                functools.partial(select_threshold, spans * SEL_SPAN))

    thr = thr_scr[...]
    lim = lim_scr[...]

    for g in range(N_KV_HEADS):
        for r in range(KV_GROUP):
            h = g * KV_GROUP + r
            qg_scr[g, r * tq:(r + 1) * tq, :] = q_ref[:, h * HEAD_DIM:(h + 1) * HEAD_DIM]
    m_scr[...] = jnp.full(m_scr.shape, NEG_BIG, F32)
    l_scr[...] = jnp.zeros(l_scr.shape, F32)
    acc_scr[...] = jnp.zeros(acc_scr.shape, F32)

    def attend_chunk(c, carry):
        off = pl.multiple_of(c * ck, ck)
        kk = key_scr[pl.ds(off, ck), :]
        pos = off + row_k
        sel = jnp.logical_or(kk > thr, jnp.logical_and(kk == thr, pos <= lim))
        sel = jnp.logical_and(sel, pos <= q_pos)
        def group_steps(g):
            kc = k_ref[pl.ds(off, ck), g * HEAD_DIM:(g + 1) * HEAD_DIM]
            vt = vt_ref[g * HEAD_DIM:(g + 1) * HEAD_DIM, pl.ds(off, ck)]
            qk = _dot_nt(kc, qg_scr[g])
            yield
            s = jnp.concatenate([jnp.where(sel, qk[:, r * tq:(r + 1) * tq], NEG_BIG) for r in range(KV_GROUP)],
                                axis=1)
            m_old = m_scr[g]
            m_new = jnp.maximum(m_old, jnp.max(s, axis=0, keepdims=True))
            yield
            p = jnp.exp(s - m_new)
            alpha = jnp.exp(m_old - m_new)
            l_scr[g] = alpha * l_scr[g] + jnp.sum(p, axis=0, keepdims=True)
            yield
            acc_scr[g] = alpha * acc_scr[g] + _dot(vt, p.astype(BF16))
            m_scr[g] = m_new

        chains = [group_steps(g) for g in range(N_KV_HEADS)]
        while chains:
            chains = [ch for ch in chains if next(ch, "done") != "done"]
        return carry

    lax.fori_loop(0, n_ch, attend_chunk, 0)
    for g in range(N_KV_HEADS):
        o_t = acc_scr[g] / l_scr[g]
        for r in range(KV_GROUP):
            h = g * KV_GROUP + r
            o_ref[:, h * HEAD_DIM:(h + 1) * HEAD_DIM] = o_t[:, r * tq:(r + 1) * tq].T.astype(BF16)


def _dsa_prompt(q_bf, qi_bf, misc, k_bf, vt_bf, kx_bf, b, t):
    n = b * t
    nq = t // Q_TILE
    n_sel = min(TOPK_MAX, t // 4)
    qrow = lambda w: pl.BlockSpec((Q_TILE, w), lambda bb, i: (bb * nq + i, 0))
    seq = lambda w: pl.BlockSpec((t, w), lambda bb, i: (bb, 0))
    return pl.pallas_call(
        functools.partial(_dsa_prompt_kernel, n_sel=n_sel),
        name="dsa_prompt",
        grid=(b, nq),
        in_specs=[qrow(ATTN_WIDTH), qrow(IDX_WIDTH),
                  qrow(LANES),
                  seq(KV_WIDTH), pl.BlockSpec((KV_WIDTH, t), lambda bb, i: (0, bb)), seq(LANES)],
        out_specs=qrow(ATTN_WIDTH),
        out_shape=jax.ShapeDtypeStruct((n, ATTN_WIDTH), BF16),
        scratch_shapes=[pltpu.VMEM((t, Q_TILE), I32),
                        pltpu.VMEM((N_IDX_HEADS * Q_TILE, LANES), BF16),
                        pltpu.VMEM((N_KV_HEADS, KV_GROUP * Q_TILE, HEAD_DIM), BF16),
                        pltpu.VMEM((1, Q_TILE), I32),
                        pltpu.VMEM((1, Q_TILE), I32),
                        pltpu.VMEM((N_KV_HEADS, 1, KV_GROUP * Q_TILE), F32),
                        pltpu.VMEM((N_KV_HEADS, 1, KV_GROUP * Q_TILE), F32),
                        pltpu.VMEM((N_KV_HEADS, HEAD_DIM, KV_GROUP * Q_TILE), F32)],
        compiler_params=_cparams(("parallel", "arbitrary")),
    )(q_bf, qi_bf, misc, k_bf, vt_bf, kx_bf)


def _sample_score_kernel(pt_ref, q_ref, w_ref, kn_ref, *refs, pages, t_valid):
    page_refs = refs[:pages]
    past_ref, new_ref = refs[pages], refs[pages + 1]
    rows = SAMPLE_ROWS
    qr = SAMPLE_QROWS
    hr = N_IDX_HEADS * qr
    half = PAGE_SIZE // 2
    q2 = q_ref[0]
    w = w_ref[0]

    def head_sum(s):
        s = w * jnp.maximum(s, 0.0)
        acc = s[0:qr]
        for h in range(1, N_IDX_HEADS):
            acc = acc + s[h * qr:(h + 1) * qr]
        return acc

    for j in range(pages):
        s = _dot_nt(q2, page_refs[j][0, 0].astype(BF16))
        past_ref[0, :, j * PAGE_SIZE:j * PAGE_SIZE + half] = _sort_key(head_sum(s[0:hr]))
        past_ref[0, :, j * PAGE_SIZE + half:(j + 1) * PAGE_SIZE] = _sort_key(head_sum(s[hr:2 * hr]))

    @pl.when(pl.program_id(1) == 0)
    def _():
        kn = jnp.concatenate([kn_ref[...], jnp.zeros((LANES - rows, IDX_DIM), F32)], axis=0).astype(BF16)
        sc = head_sum(_dot_nt(q2[0:hr, 0:IDX_DIM], kn))
        t = lax.broadcasted_iota(I32, sc.shape, 0)
        s = lax.broadcasted_iota(I32, sc.shape, 1)
        ok = jnp.logical_and(s <= t, s < t_valid)
        new_ref[0] = _sort_key(jnp.where(ok, sc, -jnp.inf))


def _page_order_pos(lane_pos):
    p = lane_pos & (PAGE_SIZE - 1)
    half = PAGE_SIZE // 2
    return (lane_pos - p) + 2 * (p & (half - 1)) + (p // half)


def _sample_scores(page_table, q_t, w_col, kif, cache_idx, layer, pages, t_valid):
    bs, n_pages = page_table.shape
    past = n_pages * PAGE_SIZE
    hr = N_IDX_HEADS * SAMPLE_QROWS
    depth, pool = cache_idx.shape[0], cache_idx.shape[1]
    cache_idx = cache_idx.reshape(depth, pool, PAGE_SIZE // 2, 2 * IDX_DIM)
    zq = jnp.zeros_like(q_t)
    q_t = jnp.concatenate([jnp.concatenate([q_t, zq], axis=2), jnp.concatenate([zq, q_t], axis=2)], axis=1)
    page_spec = lambda j: pl.BlockSpec((1, 1, PAGE_SIZE // 2, 2 * IDX_DIM),
                                       lambda b, c, pt: (layer, pt[b, c * pages + j], 0, 0))
    grid_spec = pltpu.PrefetchScalarGridSpec(
        num_scalar_prefetch=1,
        grid=(bs, n_pages // pages),
        in_specs=[pl.BlockSpec((1, 2 * hr, 2 * IDX_DIM), lambda b, c, pt: (b, 0, 0)),
                  pl.BlockSpec((1, hr, 1), lambda b, c, pt: (b, 0, 0)),
                  pl.BlockSpec((SAMPLE_ROWS, IDX_DIM), lambda b, c, pt: (b, 0))]
                 + [page_spec(j) for j in range(pages)],
        out_specs=[pl.BlockSpec((1, SAMPLE_QROWS, pages * PAGE_SIZE), lambda b, c, pt: (b, 0, c)),
                   pl.BlockSpec((1, SAMPLE_QROWS, LANES), lambda b, c, pt: (b, 0, 0))],
    )
    return pl.pallas_call(
        functools.partial(_sample_score_kernel, pages=pages, t_valid=t_valid),
        name="sample_scores",
        grid_spec=grid_spec,
        out_shape=[jax.ShapeDtypeStruct((bs, SAMPLE_QROWS, past), I32),
                   jax.ShapeDtypeStruct((bs, SAMPLE_QROWS, LANES), I32)],
        compiler_params=_cparams(("parallel", "arbitrary")),
    )(page_table, q_t, w_col, kif, *([cache_idx] * pages))


def _sample_attend_kernel(pt_ref, kp_ref, kn_ref, q_ref, knew_ref, vnew_ref, *refs, pages, n_sel, past):
    k_pages = refs[:pages]
    v_pages = refs[pages:2 * pages]
    o_ref = refs[2 * pages]
    thr_scr, lim_scr, m_scr, l_scr, acc_scr = refs[2 * pages + 1:]
    c = pl.program_id(1)
    rows = SAMPLE_ROWS
    span = pages * PAGE_SIZE

    @pl.when(c == 0)
    def _():
        m_scr[...] = jnp.full(m_scr.shape, NEG_BIG, F32)
        l_scr[...] = jnp.zeros(l_scr.shape, F32)
        acc_scr[...] = jnp.zeros(acc_scr.shape, F32)
        vr = SUBLANES
        tile_pos = lax.broadcasted_iota(I32, (vr, LANES), 1)

        def count_where(pred):
            acc = pred(kn_ref[0, 0:vr, :], past + tile_pos).astype(F32)
            for tix in range(past // LANES):
                acc = acc + pred(kp_ref[0, 0:vr, tix * LANES:(tix + 1) * LANES],
                                 _page_order_pos(tix * LANES + tile_pos)).astype(F32)
            return jnp.sum(acc, axis=1, keepdims=True)

        t, n_ge = _kth_largest_key(lambda cand: count_where(lambda kk, pos: kk >= cand), float(n_sel), (vr, 1),
                                   past + LANES)
        thr_scr[...] = jnp.full((rows, 1), INT_MIN, I32)
        lim_scr[...] = jnp.full((rows, 1), INT_MAX, I32)
        thr_scr[0:vr, :] = t

        @pl.when(jnp.max(n_ge) > float(n_sel))
        def _():
            n_gt = count_where(lambda kk, pos: kk > t)
            lim_scr[0:vr, :] = _tie_index_limit(
                lambda idx: count_where(lambda kk, pos: jnp.logical_and(kk == t, pos <= idx)),
                float(n_sel) - n_gt, past + LANES, (vr, 1))

    thr = thr_scr[0:SAMPLE_QROWS, :]
    lim = lim_scr[0:SAMPLE_QROWS, :]
    all_rows = lambda m: jnp.concatenate([m] * (rows // SAMPLE_QROWS), axis=0)

    def update_steps(g, qg, k_fn, v_fn, sel):
        sel = jnp.concatenate([sel] * KV_GROUP, axis=0)
        qk = _dot_nt(qg, k_fn())
        yield
        s = jnp.where(sel, qk, NEG_BIG)
        m_old = m_scr[g]
        m_new = jnp.maximum(m_old, jnp.max(s, axis=1, keepdims=True))
        yield
        p = jnp.where(sel, jnp.exp(s - m_new), 0.0)
        alpha = jnp.exp(m_old - m_new)
        l_scr[g] = alpha * l_scr[g] + jnp.sum(p, axis=1, keepdims=True)
        yield
        acc_scr[g] = alpha * acc_scr[g] + _dot(p.astype(BF16), v_fn())
        m_scr[g] = m_new

    def run_lockstep(chains):
        while chains:
            chains = [ch for ch in chains if next(ch, "done") != "done"]

    def select(kk, pos):
        return jnp.logical_or(kk > thr, jnp.logical_and(kk == thr, pos <= lim))

    def page_cat(page_refs, g):
        half = PAGE_SIZE // 2
        parts = [r[0, 0, pl.ds(par * N_KV_HEADS + g, half, stride=2 * N_KV_HEADS), :]
                 for r in page_refs for par in range(2)]
        return jnp.concatenate(parts, axis=0).astype(BF16)

    off = pl.multiple_of(c * span, span)
    kk = kp_ref[0, :, pl.ds(off, span)]
    sel_past = all_rows(select(kk, _page_order_pos(off + lax.broadcasted_iota(I32, kk.shape, 1))))
    q_groups = []
    for g in range(N_KV_HEADS):
        qg = jnp.concatenate(
            [q_ref[:, (g * KV_GROUP + r) * HEAD_DIM:(g * KV_GROUP + r + 1) * HEAD_DIM] for r in range(KV_GROUP)],
            axis=0)
        q_groups.append(qg)
    run_lockstep([update_steps(g, q_groups[g], functools.partial(page_cat, k_pages, g),
                               functools.partial(page_cat, v_pages, g), sel_past) for g in range(N_KV_HEADS)])

    @pl.when(c == pl.num_programs(1) - 1)
    def _():
        kn = kn_ref[0]
        lane = lax.broadcasted_iota(I32, kn.shape, 1)
        sel_new = all_rows(jnp.logical_and(select(kn, past + lane), lane < rows))
        pad = jnp.zeros((LANES - rows, KV_WIDTH), BF16)
        k_new = jnp.concatenate([knew_ref[...], pad], axis=0)
        v_new = jnp.concatenate([vnew_ref[...], pad], axis=0)
        head = lambda a, g: (lambda: a[:, g * HEAD_DIM:(g + 1) * HEAD_DIM])
        run_lockstep([update_steps(g, q_groups[g], head(k_new, g), head(v_new, g), sel_new)
                      for g in range(N_KV_HEADS)])
        for g in range(N_KV_HEADS):
            o = acc_scr[g] / l_scr[g]
            for r in range(KV_GROUP):
                h = g * KV_GROUP + r
                o_ref[:, h * HEAD_DIM:(h + 1) * HEAD_DIM] = o[r * rows:(r + 1) * rows].astype(BF16)


def _sample_attend(page_table, keys_past, keys_new, q_bf, k_bf, v_bf, cache_k, cache_v, layer, pages, n_sel):
    bs, n_pages = page_table.shape
    past = n_pages * PAGE_SIZE
    depth, pool = cache_k.shape[0], cache_k.shape[1]
    cache_k = cache_k.reshape(depth, pool, PAGE_SIZE * N_KV_HEADS, HEAD_DIM)
    cache_v = cache_v.reshape(depth, pool, PAGE_SIZE * N_KV_HEADS, HEAD_DIM)
    page_spec = lambda j: pl.BlockSpec((1, 1, PAGE_SIZE * N_KV_HEADS, HEAD_DIM),
                                       lambda b, c, pt: (layer, pt[b, c * pages + j], 0, 0))
    row = lambda w: pl.BlockSpec((SAMPLE_ROWS, w), lambda b, c, pt: (b, 0))
    grid_spec = pltpu.PrefetchScalarGridSpec(
        num_scalar_prefetch=1,
        grid=(bs, n_pages // pages),
        in_specs=[pl.BlockSpec((1, SAMPLE_QROWS, past), lambda b, c, pt: (b, 0, 0)),
                  pl.BlockSpec((1, SAMPLE_QROWS, LANES), lambda b, c, pt: (b, 0, 0)),
                  row(ATTN_WIDTH), row(KV_WIDTH), row(KV_WIDTH)]
                 + [page_spec(j) for j in range(pages)] * 2,
        out_specs=row(ATTN_WIDTH),
        scratch_shapes=[pltpu.VMEM((SAMPLE_ROWS, 1), I32),
                        pltpu.VMEM((SAMPLE_ROWS, 1), I32),
                        pltpu.VMEM((N_KV_HEADS, KV_GROUP * SAMPLE_ROWS, 1), F32),
                        pltpu.VMEM((N_KV_HEADS, KV_GROUP * SAMPLE_ROWS, 1), F32),
                        pltpu.VMEM((N_KV_HEADS, KV_GROUP * SAMPLE_ROWS, HEAD_DIM), F32)],
    )
    return pl.pallas_call(
        functools.partial(_sample_attend_kernel, pages=pages, n_sel=n_sel, past=past),
        name="sample_attend",
        grid_spec=grid_spec,
        out_shape=jax.ShapeDtypeStruct((bs * SAMPLE_ROWS, ATTN_WIDTH), BF16),
        compiler_params=_cparams(("parallel", "arbitrary")),
    )(page_table, keys_past, keys_new, q_bf, k_bf, v_bf, *([cache_k] * pages), *([cache_v] * pages))


def _delta_prep_kernel(x_ref, halo_ref, prev_ref, misc_ref, cw_ref, al_ref, dt_ref,
                       qn_ref, kn_ref, vv_ref, bg_ref, xp_scr, *, tiles_per_seq, t_valid, tt):
    i = pl.program_id(0)
    tile_in_seq = i % tiles_per_seq
    halo = jnp.where(tile_in_seq == 0, prev_ref[0], halo_ref[...])
    xp_scr[0:SUBLANES, :] = halo
    xp_scr[SUBLANES:SUBLANES + tt, :] = x_ref[...]
    base = SUBLANES - (CONV_WIDTH - 1)
    outs = (qn_ref, kn_ref, vv_ref)
    for sec in range(3):
        for h in range(N_DELTA_HEADS):
            col = sec * DELTA_WIDTH + h * HEAD_DIM
            sl = slice(col, col + HEAD_DIM)
            y = xp_scr[base:base + tt, sl] * cw_ref[0:1, sl]
            for j in range(1, CONV_WIDTH):
                y = y + xp_scr[base + j:base + j + tt, sl] * cw_ref[j:j + 1, sl]
            y = _silu(y)
            if sec < 2:
                y = y * lax.rsqrt(jnp.sum(y * y, axis=-1, keepdims=True) + EPS)
            if sec == 0:
                y = y * (HEAD_DIM ** -0.5)
            outs[sec][:, h * HEAD_DIM:(h + 1) * HEAD_DIM] = y
    m = misc_ref[...]
    lane = lax.broadcasted_iota(I32, m.shape, 1)
    row = tile_in_seq * tt + lax.broadcasted_iota(I32, m.shape, 0)
    beta = _sigmoid(m)
    g = -jnp.exp(al_ref[...]) * _softplus(m + dt_ref[...])
    is_b = jnp.logical_and(lane >= M_BD, lane < M_BD + N_DELTA_HEADS)
    is_g = jnp.logical_and(lane >= M_AD, lane < M_AD + N_DELTA_HEADS)
    comb = jnp.where(is_b, beta, jnp.where(is_g, g, 0.0))
    comb = jnp.where(row < t_valid, comb, 0.0)
    bg_ref[...] = pltpu.roll(comb, LANES - M_BD, 1)


def _delta_prep(proj, prev8, conv_w, a_log, dt_bias, b, t, tt, t_valid):
    n = proj.shape[0]
    tiles_per_seq = t // tt
    pad_vec = lambda v: jnp.zeros((1, LANES), F32).at[0, M_AD:M_AD + N_DELTA_HEADS].set(v)
    halo_blocks = tt // SUBLANES
    return pl.pallas_call(
        functools.partial(_delta_prep_kernel, tiles_per_seq=tiles_per_seq, t_valid=t_valid, tt=tt),
        name="delta_prep",
        grid=(n // tt,),
        in_specs=[pl.BlockSpec((tt, CONV_CHANNELS), lambda i: (i, C_CONV // CONV_CHANNELS)),
                  pl.BlockSpec((SUBLANES, CONV_CHANNELS),
                               lambda i: (jnp.maximum(i * halo_blocks - 1, 0), C_CONV // CONV_CHANNELS)),
                  pl.BlockSpec((1, SUBLANES, CONV_CHANNELS), lambda i: (i // tiles_per_seq, 0, 0)),
                  pl.BlockSpec((tt, LANES), lambda i: (i, C_MISC // LANES)),
                  pl.BlockSpec((CONV_WIDTH, CONV_CHANNELS), lambda i: (0, 0)),
                  pl.BlockSpec((1, LANES), lambda i: (0, 0)),
                  pl.BlockSpec((1, LANES), lambda i: (0, 0))],
        out_specs=[pl.BlockSpec((tt, DELTA_WIDTH), lambda i: (i, 0))] * 3
                  + [pl.BlockSpec((tt, LANES), lambda i: (i, 0))],
        out_shape=[jax.ShapeDtypeStruct((n, DELTA_WIDTH), F32)] * 3 + [jax.ShapeDtypeStruct((n, LANES), F32)],
        scratch_shapes=[pltpu.VMEM((SUBLANES + tt, CONV_CHANNELS), F32)],
        compiler_params=_cparams(("parallel",)),
    )(proj, proj, prev8, proj, conv_w, pad_vec(a_log), pad_vec(dt_bias))


def _mm(a, b):
    return _dot(a.astype(BF16), b.astype(BF16))


def _mm_nt(a, b):
    return _dot_nt(a.astype(BF16), b.astype(BF16))


DELTA_INV_BLOCK = 16
DELTA_STACK = 4
DELTA_CHUNKS_PER_STEP = 2


def _delta_chunk_kernel(qn_ref, kn_ref, vv_ref, bg_ref, z_ref, s0_ref, ow_ref, od_ref, so_ref, s_scr, *, n_chunks):
    c = pl.program_id(1)
    cs = DELTA_CHUNK

    @pl.when(c == 0)
    def _():
        s_scr[...] = s0_ref[0]

    ltri = (lax.broadcasted_iota(I32, (cs, cs), 0) >= lax.broadcasted_iota(I32, (cs, cs), 1)).astype(BF16)

    def chunk_gates(ch):
        bg = bg_ref[ch * cs:(ch + 1) * cs, :]
        g1 = bg.astype(BF16)
        r1 = bg - g1.astype(F32)
        g2 = r1.astype(BF16)
        g3 = (r1 - g2.astype(F32)).astype(BF16)
        gc = _dot(ltri, g1) + _dot(ltri, g2) + _dot(ltri, g3)
        return bg, gc, gc.T

    gates = [chunk_gates(ch) for ch in range(n_chunks)]
    state_ready = {}

    gh = DELTA_STACK
    rows = gh * cs
    rr = lax.broadcasted_iota(I32, (rows, rows), 0)
    cc = lax.broadcasted_iota(I32, (rows, rows), 1)
    same = (rr // cs) == (cc // cs)
    causal = jnp.logical_and(same, rr >= cc)
    strict = jnp.logical_and(same, rr > cc)
    eye = (rr == cc).astype(F32)
    row_head = lax.broadcasted_iota(I32, (rows, 1), 0) // cs
    def group_steps(ch, grp):
        heads = [grp * gh + j for j in range(gh)]
        bg, gc, gct = gates[ch]
        r0 = ch * cs
        stack = lambda ref: jnp.concatenate([ref[r0:r0 + cs, h * HEAD_DIM:(h + 1) * HEAD_DIM] for h in heads],
                                            axis=0)
        col = lambda a, lane0: jnp.concatenate([a[:, lane0 + h:lane0 + h + 1] for h in heads], axis=0)
        k = stack(kn_ref)
        q = stack(qn_ref)
        v = stack(vv_ref)
        bcol = col(bg, 0)
        gcc = col(gc, N_DELTA_HEADS)
        gcr = jnp.concatenate([gct[N_DELTA_HEADS + h:N_DELTA_HEADS + h + 1, :] for h in heads], axis=1)
        g_last = [gc[cs - 1:cs, N_DELTA_HEADS + h:N_DELTA_HEADS + h + 1] for h in heads]
        glc = jnp.concatenate([jnp.broadcast_to(gl, (cs, 1)) for gl in g_last], axis=0)
        decay = jnp.exp(jnp.where(causal, gcc - gcr, -jnp.inf))
        kb = k * bcol
        eg = jnp.exp(gcc)
        kq = _mm_nt(jnp.concatenate([kb, q], axis=0), k)
        yield
        a = jnp.where(strict, kq[0:rows] * decay, 0.0)
        intra = jnp.where(causal, kq[rows:2 * rows] * decay, 0.0)
        x = -a
        nb = DELTA_INV_BLOCK
        y = jnp.where((rr // nb) == (cc // nb), x, 0.0)
        p = eye + y
        y = _mm(y, y)
        yield
        n_sq = max(1, int(nb - 1).bit_length())
        for lvl in range(1, n_sq):
            if lvl < n_sq - 1:
                py = _mm(jnp.concatenate([p, y], axis=0), y)
                p = p + py[0:rows]
                y = py[rows:2 * rows]
            else:
                p = p + _mm(p, y)
            yield
        size = 2 * nb
        while size <= cs:
            off = jnp.where(jnp.logical_and((rr // size) == (cc // size), (rr // (size // 2)) != (cc // (size // 2))),
                            x, 0.0)
            po = _mm(p, off)
            yield
            p = p + _mm(po, p)
            yield
            size *= 2
        sol = _mm(p, jnp.concatenate([v * bcol, kb * eg], axis=1))
        yield
        u = sol[:, 0:HEAD_DIM]
        w = sol[:, HEAD_DIM:2 * HEAD_DIM]
        lanes_g = slice(grp * gh * HEAD_DIM, (grp + 1) * gh * HEAD_DIM)
        while ch > 0 and not state_ready.get((ch - 1, grp)):
            yield
        s_g = s_scr[:, lanes_g]
        wq_s = _mm(jnp.concatenate([w, q * eg], axis=0), s_g)
        yield
        own = lambda m, r0: jnp.concatenate(
            [m[r0 + j * cs:r0 + (j + 1) * cs, j * HEAD_DIM:(j + 1) * HEAD_DIM] for j in range(gh)], axis=0)
        v_new = u - own(wq_s, 0)
        o = own(wq_s, rows) + _mm(intra, v_new)
        yield
        kg_t = (k * jnp.exp(glc - gcc)).T
        vn_blocks = jnp.concatenate([jnp.where(row_head == j, v_new, 0.0) for j in range(gh)], axis=1)
        s_decay = jnp.concatenate([jnp.broadcast_to(jnp.exp(gl), (1, HEAD_DIM)) for gl in g_last], axis=1)
        s_scr[:, lanes_g] = s_g * s_decay + _mm(kg_t, vn_blocks)
        state_ready[(ch, grp)] = True
        yield
        on = o * lax.rsqrt(jnp.mean(o * o, axis=-1, keepdims=True) + EPS) * ow_ref[...]
        for j, h in enumerate(heads):
            sl = slice(h * HEAD_DIM, (h + 1) * HEAD_DIM)
            od_ref[r0:r0 + cs, sl] = (on[j * cs:(j + 1) * cs] * _silu(z_ref[r0:r0 + cs, sl])).astype(BF16)

    chains = [group_steps(ch, grp) for ch in range(n_chunks) for grp in range(N_DELTA_HEADS // gh)]
    while chains:
        chains = [g for g in chains if next(g, "done") != "done"]

    so_ref[0] = s_scr[...]


def _delta_chunks(qn, kn, vv, bg, zsrc, z_col_block, state0, o_norm_w, b, t):
    n = b * t
    per_step = DELTA_CHUNKS_PER_STEP if (t // DELTA_CHUNK) % DELTA_CHUNKS_PER_STEP == 0 else 1
    nc = t // (DELTA_CHUNK * per_step)
    sw = N_DELTA_HEADS * HEAD_DIM
    row = lambda w, cb=0: pl.BlockSpec((DELTA_CHUNK * per_step, w), lambda bb, c: (bb * nc + c, cb))
    st = pl.BlockSpec((1, HEAD_DIM, sw), lambda bb, c: (bb, 0, 0))
    s_in = state0.transpose(0, 2, 1, 3).reshape(b, HEAD_DIM, sw)
    od, s_out = pl.pallas_call(
        functools.partial(_delta_chunk_kernel, n_chunks=per_step),
        name="delta_chunks",
        grid=(b, nc),
        in_specs=[row(DELTA_WIDTH), row(DELTA_WIDTH), row(DELTA_WIDTH), row(LANES),
                  row(DELTA_WIDTH, z_col_block), st, pl.BlockSpec((1, LANES), lambda bb, c: (0, 0))],
        out_specs=[row(DELTA_WIDTH), st],
        out_shape=[jax.ShapeDtypeStruct((n, DELTA_WIDTH), BF16),
                   jax.ShapeDtypeStruct((b, HEAD_DIM, sw), F32)],
        scratch_shapes=[pltpu.VMEM((HEAD_DIM, sw), F32)],
        compiler_params=_cparams(("parallel", "arbitrary")),
    )(qn, kn, vv, bg, zsrc, s_in, o_norm_w.reshape(1, LANES))
    return od, s_out.reshape(b, HEAD_DIM, N_DELTA_HEADS, HEAD_DIM).transpose(0, 2, 1, 3)


def _outproj_kernel(oap_ref, odp_ref, xp_ref, g1p_ref, shp_ref, scp_ref,
                    oas_ref, ods_ref, xs_ref, g1s_ref, shs_ref, scs_ref,
                    nw_ref, wo_ref, wrh_ref, br_ref, x1_ref, h2_ref, lg_ref, *, prompt_tiles):
    def body(oa_ref, od_ref, x_ref, g1_ref, sh_ref, sc_ref):
        mix = (_dot(oa_ref[...], wo_ref[0:ATTN_WIDTH, :])
               + _dot(od_ref[...], wo_ref[ATTN_WIDTH:ATTN_WIDTH + DELTA_WIDTH, :]))
        x1 = x_ref[...] + g1_ref[0] * mix
        x1_ref[...] = x1
        y = x1 * lax.rsqrt(jnp.mean(x1 * x1, axis=-1, keepdims=True) + EPS) * nw_ref[...]
        h2 = y * (1.0 + sc_ref[0]) + sh_ref[0]
        h2_ref[...] = h2
        hb = h2.astype(BF16)
        lo = (h2 - hb.astype(F32)).astype(BF16)
        wr = wrh_ref[...]
        both = _dot(hb, wr)
        lg_ref[...] = both[:, 0:LANES] + both[:, LANES:2 * LANES] + _dot(lo, wr[:, 0:LANES]) + br_ref[...]

    i = pl.program_id(0)
    pl.when(i < prompt_tiles)(functools.partial(body, oap_ref, odp_ref, xp_ref, g1p_ref, shp_ref, scp_ref))
    pl.when(i >= prompt_tiles)(functools.partial(body, oas_ref, ods_ref, xs_ref, g1s_ref, shs_ref, scs_ref))


def _out_projection(prompt, sample, norm2_w, w_out_bf, wr_both, b_rt, tm, prompt_seq_len):
    n_p, d = prompt[2].shape
    n_s = sample[2].shape[0]
    pt = n_p // tm
    tiles_per_seq = prompt_seq_len // tm
    p_row = lambda w: pl.BlockSpec((tm, w), lambda i: (jnp.minimum(i, pt - 1), 0))
    s_row = lambda w: pl.BlockSpec((tm, w), lambda i: (jnp.maximum(i - pt, 0), 0))
    p_mod = pl.BlockSpec((1, 1, d), lambda i: (jnp.minimum(i, pt - 1) // tiles_per_seq, 0, 0))
    s_mod = pl.BlockSpec((1, tm, d), lambda i: (jnp.maximum(i - pt, 0), 0, 0))
    row = lambda w: pl.BlockSpec((tm, w), lambda i: (i, 0))
    full = lambda a: pl.BlockSpec(a.shape, lambda i: (0, 0))
    n = n_p + n_s
    return pl.pallas_call(
        functools.partial(_outproj_kernel, prompt_tiles=pt),
        name="out_proj",
        grid=(n // tm,),
        in_specs=[p_row(ATTN_WIDTH), p_row(DELTA_WIDTH), p_row(d), p_mod, p_mod, p_mod,
                  s_row(ATTN_WIDTH), s_row(DELTA_WIDTH), s_row(d), s_mod, s_mod, s_mod,
                  pl.BlockSpec((1, d), lambda i: (0, 0)), full(w_out_bf), full(wr_both), full(b_rt)],
        out_specs=[row(d), row(d), row(LANES)],
        out_shape=[jax.ShapeDtypeStruct((n, d), F32), jax.ShapeDtypeStruct((n, d), F32),
                   jax.ShapeDtypeStruct((n, LANES), F32)],
        compiler_params=_cparams(("parallel",)),
    )(*prompt, *sample, norm2_w.reshape(1, d), w_out_bf, wr_both, b_rt)


def _route_tile(x):
    lane = lax.broadcasted_iota(I32, x.shape, 1)
    gl = jnp.where(lane < N_GROUPS, x, -jnp.inf)
    ge = jnp.exp(gl - jnp.max(gl, axis=1, keepdims=True))
    p = ge / jnp.sum(ge, axis=1, keepdims=True)
    p_max = jnp.max(p, axis=1, keepdims=True)
    grp = jnp.min(jnp.where(p == p_max, lane, LANES), axis=1, keepdims=True)
    e_lane = lane - N_GROUPS
    in_grp = jnp.logical_and(jnp.logical_and(e_lane >= 0, e_lane < N_EXPERTS),
                             (e_lane >> 3) == grp)
    rl = jnp.where(in_grp, x, -jnp.inf)
    v1 = jnp.max(rl, axis=1, keepdims=True)
    i1 = jnp.min(jnp.where(rl == v1, lane, LANES), axis=1, keepdims=True)
    rl2 = jnp.where(lane == i1, -jnp.inf, rl)
    v2 = jnp.max(rl2, axis=1, keepdims=True)
    i2 = jnp.min(jnp.where(rl2 == v2, lane, LANES), axis=1, keepdims=True)
    t = jnp.exp(v2 - v1)
    den = 1.0 + t
    eid = jnp.where(lane == 0, i1 - N_GROUPS, jnp.where(lane == 1, i2 - N_GROUPS, 0))
    gate = jnp.where(lane == 0, (1.0 / den) * p_max, jnp.where(lane == 1, (t / den) * p_max, 0.0))
    return eid, gate


def _route_kernel(lg_ref, eid_ref, gate_ref):
    eid_ref[...], gate_ref[...] = _route_tile(lg_ref[...])


def _route(logits, tm):
    n = logits.shape[0]
    spec = pl.BlockSpec((tm, LANES), lambda i: (i, 0))
    return pl.pallas_call(
        _route_kernel,
        name="route",
        grid=(n // tm,),
        in_specs=[spec],
        out_specs=[spec, spec],
        out_shape=[jax.ShapeDtypeStruct((n, LANES), I32), jax.ShapeDtypeStruct((n, LANES), F32)],
        compiler_params=_cparams(("parallel",)),
    )(logits)


def _row_gather(idx_ref, base, n_rows, src_hbm, dst, sem):
    def body(r, carry):
        pltpu.make_async_copy(src_hbm.at[pl.ds(idx_ref[base + r], 1), :], dst.at[pl.ds(r, 1), :], sem).start()
        return carry

    lax.fori_loop(0, n_rows, body, 0, unroll=8)


def _row_gather_wait(n_rows, src_hbm, dst, sem):
    pltpu.make_async_copy(src_hbm.at[pl.ds(0, n_rows), :], dst, sem).wait()


def _moe_kernel(tok_ref, j0_ref, be_ref, na_ref, h_hbm, wg_ref, wu_ref, wd_ref, o_ref,
                x_even, x_odd, sem, wg_scr, wu_scr, wd_scr, *, bm):
    i = pl.program_id(0)
    n_act = na_ref[0]
    bufs = ((x_even, sem.at[0]), (x_odd, sem.at[1]))

    @pl.when(i == 0)
    def _():
        _row_gather(tok_ref, j0_ref[0], bm, h_hbm, x_even, sem.at[0])

    changed = jnp.logical_or(i == 0, be_ref[i] != be_ref[jnp.maximum(i - 1, 0)])

    @pl.when(jnp.logical_and(i < n_act, changed))
    def _():
        wg_scr[...] = wg_ref[0].astype(BF16)
        wu_scr[...] = wu_ref[0].astype(BF16)
        wd_scr[...] = wd_ref[0].astype(BF16)

    for parity in range(2):
        cur, cur_sem = bufs[parity]
        nxt, nxt_sem = bufs[1 - parity]

        @pl.when(jnp.logical_and(i < n_act, i % 2 == parity))
        def _():
            _row_gather_wait(bm, h_hbm, cur, cur_sem)
            base = j0_ref[i + 1]
            for r in range(bm):
                pltpu.make_async_copy(h_hbm.at[pl.ds(tok_ref[base + r], 1), :], nxt.at[pl.ds(r, 1), :],
                                      nxt_sem).start()
            x = cur[...].astype(BF16)
            hid = _silu(_dot(x, wg_scr[...])) * _dot(x, wu_scr[...])
            o_ref[...] = _dot(hid.astype(BF16), wd_scr[...])

        @pl.when(jnp.logical_and(i == n_act, i % 2 == parity))
        def _():
            _row_gather_wait(bm, h_hbm, cur, cur_sem)

    @pl.when(i >= n_act)
    def _():
        o_ref[...] = jnp.zeros(o_ref.shape, F32)


def _moe_experts(tok_sorted, block_j0, block_exp, n_active, h2, w_gate, w_up, w_down, bm):
    ns = block_exp.shape[0] * bm
    d = h2.shape[1]
    f = w_gate.shape[2]
    grid_spec = pltpu.PrefetchScalarGridSpec(
        num_scalar_prefetch=4,
        grid=(ns // bm,),
        in_specs=[pl.BlockSpec(memory_space=pl.ANY),
                  pl.BlockSpec((1, d, f), lambda i, tok, j0, be, na: (be[i], 0, 0)),
                  pl.BlockSpec((1, d, f), lambda i, tok, j0, be, na: (be[i], 0, 0)),
                  pl.BlockSpec((1, f, d), lambda i, tok, j0, be, na: (be[i], 0, 0))],
        out_specs=pl.BlockSpec((bm, d), lambda i, tok, j0, be, na: (i, 0)),
        scratch_shapes=[pltpu.VMEM((bm, d), F32), pltpu.VMEM((bm, d), F32), pltpu.SemaphoreType.DMA((2,)),
                        pltpu.VMEM((d, f), BF16), pltpu.VMEM((d, f), BF16), pltpu.VMEM((f, d), BF16)],
    )
    return pl.pallas_call(
        functools.partial(_moe_kernel, bm=bm),
        name="moe_experts",
        grid_spec=grid_spec,
        out_shape=jax.ShapeDtypeStruct((ns, d), F32),
        compiler_params=_cparams(("arbitrary",)),
    )(tok_sorted, block_j0, block_exp, n_active, h2, w_gate, w_up, w_down)


def _combine_kernel(dest_ref, x1_ref, gt_ref, g2p_ref, g2s_ref, y_hbm, op_ref, os_ref, y_buf, sem, *,
                    tm, prompt_tiles):
    i = pl.program_id(0)
    n = pl.num_programs(0)
    slot = i % 2

    @pl.when(i == 0)
    def _():
        _row_gather(dest_ref, 0, 2 * tm, y_hbm, y_buf.at[0], sem.at[0])

    for parity in range(2):
        @pl.when(jnp.logical_and(i + 1 < n, slot == parity))
        def _():
            base = (i + 1) * 2 * tm
            for r in range(2 * tm):
                pltpu.make_async_copy(y_hbm.at[pl.ds(dest_ref[base + r], 1), :],
                                      y_buf.at[1 - parity, pl.ds(r, 1), :], sem.at[1 - parity]).start()

    _row_gather_wait(2 * tm, y_hbm, y_buf.at[slot], sem.at[slot])
    gt = gt_ref[...]
    y = y_buf[slot, 0:tm, :] * gt[:, 0:1] + y_buf[slot, tm:2 * tm, :] * gt[:, 1:2]

    @pl.when(i < prompt_tiles)
    def _():
        op_ref[...] = x1_ref[...] + g2p_ref[0] * y

    @pl.when(i >= prompt_tiles)
    def _():
        os_ref[...] = x1_ref[...] + g2s_ref[0] * y


def _combine(x1, dest, y_rows, gates, gate2_p, gate2_s, tm, n_prompt, prompt_seq_len):
    n, d = x1.shape
    pt = n_prompt // tm
    tiles_per_seq = prompt_seq_len // tm
    dest_tiles = dest.reshape(n // tm, tm, 2).transpose(0, 2, 1).reshape(-1)
    row = lambda w: pl.BlockSpec((tm, w), lambda i, dst: (i, 0))
    grid_spec = pltpu.PrefetchScalarGridSpec(
        num_scalar_prefetch=1,
        grid=(n // tm,),
        in_specs=[row(d), row(LANES),
                  pl.BlockSpec((1, 1, d), lambda i, dst: (jnp.minimum(i, pt - 1) // tiles_per_seq, 0, 0)),
                  pl.BlockSpec((1, tm, d), lambda i, dst: (jnp.maximum(i - pt, 0), 0, 0)),
                  pl.BlockSpec(memory_space=pl.ANY)],
        out_specs=[pl.BlockSpec((tm, d), lambda i, dst: (jnp.minimum(i, pt - 1), 0)),
                   pl.BlockSpec((tm, d), lambda i, dst: (jnp.maximum(i - pt, 0), 0))],
        scratch_shapes=[pltpu.VMEM((2, 2 * tm, d), F32), pltpu.SemaphoreType.DMA((2,))],
    )
    return pl.pallas_call(
        functools.partial(_combine_kernel, tm=tm, prompt_tiles=pt),
        name="moe_combine",
        grid_spec=grid_spec,
        out_shape=[jax.ShapeDtypeStruct((n_prompt, d), F32), jax.ShapeDtypeStruct((n - n_prompt, d), F32)],
        compiler_params=_cparams(("arbitrary",)),
    )(dest_tiles, x1, gates, gate2_p, gate2_s, y_rows)


FRONT_SECTIONS = ((C_QA, ATTN_WIDTH), (C_QI, IDX_WIDTH),
                  (C_CONV, DELTA_WIDTH), (C_CONV + DELTA_WIDTH, DELTA_WIDTH), (C_CONV + 2 * DELTA_WIDTH, DELTA_WIDTH),
                  (C_KA, PROJ_PACKED - C_KA), (C_ZD, DELTA_WIDTH))


def _prompt_front_kernel(x_ref, sh_ref, sc_ref, nw_ref, w_ref, tabm_ref, tabi_ref, qw_ref, kw_ref, iw_ref,
                         prev_ref, cw_ref, al_ref, dt_ref,
                         q_ref, qi_ref, z_ref, qn_ref, kn_ref, vv_ref, kf_ref, kb_ref, vf_ref, vt_ref,
                         kif_ref, kx_ref, bg_ref, misc_ref, tail_ref,
                         carry_scr, xp_scr, *, tiles_per_seq, tm):
    i = pl.program_id(0)
    tile_in_seq = i % tiles_per_seq
    x = x_ref[...]
    y = x * lax.rsqrt(jnp.mean(x * x, axis=-1, keepdims=True) + EPS) * nw_ref[...]
    h = (y * (1.0 + sc_ref[0]) + sh_ref[0]).astype(BF16)
    tabm = tabm_ref[0]
    tabi = tabi_ref[0]
    half_main = HEAD_DIM // ROPE_FRACTION // 2
    half_idx = IDX_DIM // ROPE_FRACTION // 2

    @pl.when(tile_in_seq == 0)
    def _():
        carry_scr[...] = prev_ref[0]

    def queries(pj):
        for hh in range(N_ATTN_HEADS):
            sl = slice(hh * HEAD_DIM, (hh + 1) * HEAD_DIM)
            v = _rope(_rms_head(pj[:, sl], qw_ref[...]), tabm, half_main)
            q_ref[:, sl] = (v * (HEAD_DIM ** -0.5)).astype(BF16)

    def index_queries(pj):
        for p in range(IDX_WIDTH // LANES):
            sl = slice(p * LANES, (p + 1) * LANES)
            qi_ref[:, sl] = _rope(pj[:, sl], tabi, half_idx).astype(BF16)

    def gate_z(pj):
        z_ref[...] = pj

    def conv_section(sec, pj):
        cols = slice(sec * DELTA_WIDTH, (sec + 1) * DELTA_WIDTH)
        xp_scr[sec, 0:SUBLANES, :] = carry_scr[:, cols]
        xp_scr[sec, SUBLANES:SUBLANES + tm, :] = pj
        carry_scr[:, cols] = pj[tm - SUBLANES:tm, :]
        base = SUBLANES - (CONV_WIDTH - 1)
        out = (qn_ref, kn_ref, vv_ref)[sec]
        for hh in range(N_DELTA_HEADS):
            sl = slice(hh * HEAD_DIM, (hh + 1) * HEAD_DIM)
            wsl = slice(sec * DELTA_WIDTH + hh * HEAD_DIM, sec * DELTA_WIDTH + (hh + 1) * HEAD_DIM)
            v = xp_scr[sec, base:base + tm, sl] * cw_ref[0:1, wsl]
            for j in range(1, CONV_WIDTH):
                v = v + xp_scr[sec, base + j:base + j + tm, sl] * cw_ref[j:j + 1, wsl]
            v = _silu(v)
            if sec < 2:
                v = v * lax.rsqrt(jnp.sum(v * v, axis=-1, keepdims=True) + EPS)
            if sec == 0:
                v = v * (HEAD_DIM ** -0.5)
            out[:, sl] = v
        if sec == 2:
            tail_ref[0] = carry_scr[...]

    def keys_values_misc(pj):
        for hk in range(N_KV_HEADS):
            sl = slice(hk * HEAD_DIM, (hk + 1) * HEAD_DIM)
            v = _rope(_rms_head(pj[:, sl], kw_ref[...]), tabm, half_main)
            kf_ref[:, sl] = v
            kb_ref[:, sl] = v.astype(BF16)
        va = pj[:, KV_WIDTH:2 * KV_WIDTH]
        vf_ref[...] = va
        vt_ref[...] = va.T.astype(BF16)
        m = pj[:, 2 * KV_WIDTH:2 * KV_WIDTH + LANES]
        misc_ref[...] = m
        lane = lax.broadcasted_iota(I32, m.shape, 1)
        ki = jnp.where(lane < IDX_DIM, m, 0.0)
        ms = jnp.sum(ki * ki, axis=-1, keepdims=True) * (1.0 / IDX_DIM)
        v = _rope(ki * lax.rsqrt(ms + EPS) * iw_ref[...], tabi, half_idx)
        kif_ref[...] = v[:, 0:IDX_DIM]
        kx_ref[...] = (v + pltpu.roll(v, IDX_DIM, 1)).astype(BF16)
        beta = _sigmoid(m)
        g = -jnp.exp(al_ref[...]) * _softplus(m + dt_ref[...])
        is_b = jnp.logical_and(lane >= M_BD, lane < M_BD + N_DELTA_HEADS)
        is_g = jnp.logical_and(lane >= M_AD, lane < M_AD + N_DELTA_HEADS)
        comb = jnp.where(is_b, beta, jnp.where(is_g, g, 0.0))
        bg_ref[...] = pltpu.roll(comb, LANES - M_BD, 1)

    epilogues = (queries, index_queries,
                 functools.partial(conv_section, 0), functools.partial(conv_section, 1),
                 functools.partial(conv_section, 2), keys_values_misc, gate_z)
    project = lambda k: _dot(h, w_ref[:, FRONT_SECTIONS[k][0]:FRONT_SECTIONS[k][0] + FRONT_SECTIONS[k][1]])
    pj_next = project(0)
    for k, epilogue in enumerate(epilogues):
        pj = pj_next
        if k + 1 < len(epilogues):
            pj_next = project(k + 1)
        epilogue(pj)


def _prompt_front(x2d, shift, scale, norm_w, w_packed, pos, q_norm_w, k_norm_w, idx_k_norm_w, prev8, conv_w,
                  a_log, dt_bias, b, t, tm):
    n, d = x2d.shape
    np_ = w_packed.shape[1]
    tiles_per_seq = t // tm
    tabm = _rope_tables(pos, HEAD_DIM, LANES).reshape(tiles_per_seq, tm, 3 * LANES)
    tabi = _rope_tables(pos, IDX_DIM, IDX_DIM).reshape(tiles_per_seq, tm, 3 * LANES)
    iw = jnp.concatenate([idx_k_norm_w, jnp.zeros((LANES - IDX_DIM,), F32)]).reshape(1, LANES)
    pad_vec = lambda v: jnp.zeros((1, LANES), F32).at[0, M_AD:M_AD + N_DELTA_HEADS].set(v)
    mod_spec = pl.BlockSpec((1, 1, d), lambda i: (i // tiles_per_seq, 0, 0))
    tab_spec = pl.BlockSpec((1, tm, 3 * LANES), lambda i: (i % tiles_per_seq, 0, 0))
    vec = lambda w: pl.BlockSpec((1, w), lambda i: (0, 0))
    row = lambda w: pl.BlockSpec((tm, w), lambda i: (i, 0))
    seq_state = pl.BlockSpec((1, SUBLANES, CONV_CHANNELS), lambda i: (i // tiles_per_seq, 0, 0))
    widths = [(ATTN_WIDTH, BF16), (IDX_WIDTH, BF16), (DELTA_WIDTH, F32), (DELTA_WIDTH, F32), (DELTA_WIDTH, F32),
              (DELTA_WIDTH, F32), (KV_WIDTH, F32), (KV_WIDTH, BF16), (KV_WIDTH, F32)]
    outs = pl.pallas_call(
        functools.partial(_prompt_front_kernel, tiles_per_seq=tiles_per_seq, tm=tm),
        name="prompt_front",
        grid=(n // tm,),
        in_specs=[row(d), mod_spec, mod_spec, vec(d),
                  pl.BlockSpec((d, np_), lambda i: (0, 0), pipeline_mode=pl.Buffered(1)),
                  tab_spec, tab_spec, vec(LANES), vec(LANES), vec(LANES), seq_state,
                  pl.BlockSpec((CONV_WIDTH, CONV_CHANNELS), lambda i: (0, 0)), vec(LANES), vec(LANES)],
        out_specs=[row(w) for w, _ in widths]
                  + [pl.BlockSpec((KV_WIDTH, tm), lambda i: (0, i)), row(IDX_DIM), row(LANES), row(LANES), row(LANES),
                     seq_state],
        out_shape=[jax.ShapeDtypeStruct((n, w), dt) for w, dt in widths]
                  + [jax.ShapeDtypeStruct((KV_WIDTH, n), BF16), jax.ShapeDtypeStruct((n, IDX_DIM), F32),
                     jax.ShapeDtypeStruct((n, LANES), BF16), jax.ShapeDtypeStruct((n, LANES), F32),
                     jax.ShapeDtypeStruct((n, LANES), F32),
                     jax.ShapeDtypeStruct((b, SUBLANES, CONV_CHANNELS), F32)],
        scratch_shapes=[pltpu.VMEM((SUBLANES, CONV_CHANNELS), F32),
                        pltpu.VMEM((3, SUBLANES + tm, DELTA_WIDTH), F32)],
        compiler_params=pltpu.CompilerParams(dimension_semantics=("arbitrary",),
                                             vmem_limit_bytes=FRONT_VMEM_LIMIT),
    )(x2d, shift, scale, norm_w.reshape(1, d), w_packed, tabm, tabi,
      q_norm_w.reshape(1, LANES), k_norm_w.reshape(1, LANES), iw, prev8, conv_w, pad_vec(a_log), pad_vec(dt_bias))
    names = ("q", "qi", "z", "qn", "kn", "vv", "kf", "kb", "vf", "vt", "kif", "kx", "bg", "misc", "tail")
    return dict(zip(names, outs))


def _pick_tile(n, pref, mult=16):
    t = min(pref, n)
    while n % t or t % mult:
        t -= 1
    return t


def _pack_w_in(w_in):
    d = w_in.shape[0]
    bounds = np.cumsum(PROJ_SIZES)[:-1].tolist()
    qa, ka, va, qi, ki, wi, qd, kd, vd, zd, bd, ad = jnp.split(w_in, bounds, axis=1)
    used = IDX_DIM + N_IDX_HEADS + 2 * N_DELTA_HEADS
    misc = jnp.concatenate([ki, wi, bd, ad, jnp.zeros((d, LANES - used), w_in.dtype)], axis=1)
    cols = [qa, qi, zd, qd, kd, vd, ka, va, misc]
    width = sum(c.shape[1] for c in cols)
    cols.append(jnp.zeros((d, PROJ_PACKED - width), w_in.dtype))
    return jnp.concatenate(cols, axis=1).astype(BF16)


def _route_and_sort(eid, bm):
    n = eid.shape[0]
    nk = 2 * n
    flat_e = eid.reshape(-1)
    order = jnp.argsort(flat_e, stable=True).astype(I32)
    inv = jnp.argsort(order).astype(I32)
    onehot = flat_e[:, None] == jnp.arange(N_EXPERTS, dtype=I32)[None, :]
    counts = jnp.sum(onehot.astype(I32), axis=0)
    padded = (counts + bm - 1) // bm * bm
    pad_end = jnp.cumsum(padded)
    shift = (pad_end - padded) - (jnp.cumsum(counts) - counts)
    dest = inv + jnp.sum(jnp.where(onehot, shift[None, :], 0), axis=1)
    n_blocks = -(-nk // bm) + N_EXPERTS + 1
    block_exp = jnp.minimum(jnp.searchsorted(pad_end, jnp.arange(n_blocks, dtype=I32) * bm, side='right'),
                            N_EXPERTS - 1).astype(I32)
    n_active = (pad_end[-1] // bm).astype(I32).reshape(1)
    block_j0 = jnp.clip(jnp.arange(n_blocks, dtype=I32) * bm - shift[block_exp], 0, nk)
    tok_sorted = jnp.concatenate([order // 2, jnp.zeros((bm,), I32)])
    return tok_sorted, block_j0, dest.astype(I32).reshape(n, 2), block_exp, n_active


def _layer(layer, yp, ys, cache_k, cache_v, cache_idx, state_ssm, state_conv, page_table, c_prompt, c_sample,
           w_in, w_out, conv_w, a_log, dt_bias, q_norm_w, k_norm_w, idx_k_norm_w, o_norm_w, norm1_w, norm2_w,
           w_ada, b_ada, w_group, b_group, w_router, b_router, w_gate, w_up, w_down):
    bp, tp, d = yp.shape
    bs, ts, _ = ys.shape
    past = page_table.shape[1] * PAGE_SIZE
    rows = SAMPLE_ROWS
    assert CONV_WIDTH - 1 <= ts <= SUBLANES <= rows and tp % SEL_SPAN == 0 and tp % DELTA_CHUNK == 0

    n_c = bp + bs
    n_c_pad = -(-n_c // SUBLANES) * SUBLANES
    c_all = jnp.concatenate([c_prompt, c_sample, jnp.zeros((n_c_pad - n_c, d), F32)], axis=0)
    mod = _ada_modulation(c_all, w_ada, b_ada)
    mods = jnp.split(mod, N_MOD, axis=1)
    mp = [m[:bp].reshape(bp, 1, d) for m in mods]
    ms = [jnp.repeat(m[bp:bp + bs], rows, axis=0).reshape(1, bs * rows, d) for m in mods]

    w_packed = _pack_w_in(w_in)
    w_out_bf = w_out.astype(BF16)
    w_rt = jnp.concatenate([w_group, w_router, jnp.zeros((d, LANES - N_GROUPS - N_EXPERTS), F32)], axis=1)
    wr_hi = w_rt.astype(BF16)
    wr_both = jnp.concatenate([wr_hi, (w_rt - wr_hi.astype(F32)).astype(BF16)], axis=1)
    b_rt = jnp.concatenate([b_group, b_router, jnp.zeros((LANES - N_GROUPS - N_EXPERTS,), F32)]).reshape(1, LANES)

    np_ = bp * tp
    xp2 = yp.reshape(np_, d)
    tm_p = _pick_tile(tp, 256)
    fr = _prompt_front(xp2, mp[0], mp[1], norm1_w, w_packed, jnp.arange(tp), q_norm_w, k_norm_w, idx_k_norm_w,
                       jnp.zeros((bp, SUBLANES, CONV_CHANNELS), F32), conv_w, a_log, dt_bias, bp, tp, tm_p)
    kf_p, kif_p = fr["kf"], fr["kif"]
    oa_p = _dsa_prompt(fr["q"], fr["qi"], fr["misc"], fr["kb"], fr["vt"], fr["kx"], bp, tp)
    od_p, ssm_p = _delta_chunks(fr["qn"], fr["kn"], fr["vv"], fr["bg"], fr["z"], 0,
                                jnp.zeros((bp, N_DELTA_HEADS, HEAD_DIM, HEAD_DIM), F32), o_norm_w, bp, tp)

    ns_ = bs * rows
    xs2 = jnp.pad(ys, ((0, 0), (0, rows - ts), (0, 0))).reshape(ns_, d)
    tm_s = _pick_tile(ns_, 256)
    proj_s = _in_projection(xs2, ms[0].reshape(ns_ // tm_s, tm_s, d), ms[1].reshape(ns_ // tm_s, tm_s, d),
                            norm1_w, w_packed, tm_s, tm_s)
    q_s, kf_s, kb_s, vb_s, qi_s, kif_s, _ = _attention_prep(
        proj_s, past + jnp.arange(rows), rows, q_norm_w, k_norm_w, idx_k_norm_w, False)
    qr = SAMPLE_QROWS
    q_t = qi_s.reshape(bs, rows, N_IDX_HEADS, IDX_DIM)[:, :qr].transpose(0, 2, 1, 3)
    q_t = q_t.reshape(bs, N_IDX_HEADS * qr, IDX_DIM)
    w_col = proj_s[:, C_MISC + M_WI:C_MISC + M_WI + N_IDX_HEADS].reshape(bs, rows, N_IDX_HEADS)[:, :qr]
    w_col = w_col.transpose(0, 2, 1).reshape(bs, N_IDX_HEADS * qr, 1)
    pages = _pick_tile(page_table.shape[1], SAMPLE_PAGES_PER_STEP, 1)
    keys_past, keys_new = _sample_scores(page_table, q_t, w_col, kif_s, cache_idx, layer, pages, ts)
    n_sel_s = min(TOPK_MAX, (past + ts) // 4)
    oa_s = _sample_attend(page_table, keys_past, keys_new, q_s, kb_s, vb_s, cache_k, cache_v, layer, pages, n_sel_s)
    prev8 = jnp.pad(state_conv, ((0, 0), (SUBLANES - (CONV_WIDTH - 1), 0), (0, 0)))
    qn_s, kn_s, vv_s, bg_s = _delta_prep(proj_s, prev8, conv_w, a_log, dt_bias, bs, rows, rows, ts)
    to_chunk = lambda a: jnp.pad(a.reshape(bs, rows, -1), ((0, 0), (0, DELTA_CHUNK - rows), (0, 0))).reshape(
        bs * DELTA_CHUNK, -1)
    z_s = proj_s[:, C_ZD:C_ZD + DELTA_WIDTH]
    od_s, ssm_s = _delta_chunks(to_chunk(qn_s), to_chunk(kn_s), to_chunk(vv_s), to_chunk(bg_s), to_chunk(z_s), 0,
                                state_ssm, o_norm_w, bs, DELTA_CHUNK)
    od_s = od_s.reshape(bs, DELTA_CHUNK, DELTA_WIDTH)[:, :rows].reshape(ns_, DELTA_WIDTH)

    tm_o = _pick_tile(tp, tm_s)
    assert ns_ % tm_o == 0
    per_tok = lambda m: m.reshape(ns_ // tm_o, tm_o, d)
    n_all = np_ + ns_
    x1_all, h2_all, lg_all = _out_projection(
        (oa_p, od_p, xp2, mp[2], mp[3], mp[4]),
        (oa_s, od_s, xs2, per_tok(ms[2]), per_tok(ms[3]), per_tok(ms[4])),
        norm2_w, w_out_bf, wr_both, b_rt, tm_o, tp)
    eid, gates = _route(lg_all, _pick_tile(n_all, 512, SUBLANES))
    bm = 256
    tok_sorted, block_j0, dest, block_exp, n_active = _route_and_sort(eid[:, 0:2], bm)
    yb = _moe_experts(tok_sorted, block_j0, block_exp, n_active, h2_all, w_gate, w_up, w_down, bm)
    out_p, out_s = _combine(x1_all, dest, yb, gates, mp[5], per_tok(ms[5]), tm_o, np_, tp)

    valid = lambda a: a.reshape(bs, rows, -1)[:, :ts]
    conv_p = fr["tail"][:, SUBLANES - (CONV_WIDTH - 1):]
    conv_s = proj_s.reshape(bs, rows, PROJ_PACKED)[:, ts - (CONV_WIDTH - 1):ts, C_CONV:C_CONV + CONV_CHANNELS]
    return (out_p.reshape(bp, tp, d), valid(out_s),
            kf_p.reshape(bp, tp, N_KV_HEADS, HEAD_DIM),
            fr["vf"].reshape(bp, tp, N_KV_HEADS, HEAD_DIM),
            kif_p.reshape(bp, tp, IDX_DIM), ssm_p, conv_p,
            valid(kf_s).reshape(bs, ts, N_KV_HEADS, HEAD_DIM),
            valid(proj_s[:, C_VA:C_VA + KV_WIDTH]).reshape(bs, ts, N_KV_HEADS, HEAD_DIM),
            valid(kif_s), ssm_s, conv_s)


def kernel(x_prompt, x_sample, cache_k, cache_v, cache_idx_k, state_ssm, state_conv, page_table, c_prompt, c_sample,
           w_in, w_out, conv_w, a_log, dt_bias, q_norm_w, k_norm_w, idx_k_norm_w, o_norm_w, norm1_w, norm2_w,
           w_ada, b_ada, w_group, b_group, w_router, b_router, w_gate, w_up, w_down):
    depth = w_in.shape[0]
    yp, ys = x_prompt, x_sample
    per_layer = []
    for l in range(depth):
        res = _layer(l, yp, ys, cache_k, cache_v, cache_idx_k, state_ssm[l], state_conv[l], page_table,
                     c_prompt, c_sample, w_in[l], w_out[l], conv_w[l], a_log[l], dt_bias[l], q_norm_w[l],
                     k_norm_w[l], idx_k_norm_w[l], o_norm_w[l], norm1_w[l], norm2_w[l], w_ada[l], b_ada[l],
                     w_group[l], b_group[l], w_router[l], b_router[l], w_gate[l], w_up[l], w_down[l])
        yp, ys = res[0], res[1]
        per_layer.append(res[2:])
    stacked = tuple(jnp.stack([pl_[j] for pl_ in per_layer]) for j in range(10))
    return (yp, ys) + stacked
```

```python
import functools

import jax
import jax.numpy as jnp
import numpy as np
from jax import lax
from jax.experimental import pallas as pl
from jax.experimental.pallas import tpu as pltpu

F32 = jnp.float32
BF16 = jnp.bfloat16
I32 = jnp.int32

HEAD_DIM = 128
N_ATTN_HEADS = 8
N_KV_HEADS = 2
KV_GROUP = N_ATTN_HEADS // N_KV_HEADS
N_DELTA_HEADS = 8
N_IDX_HEADS = 16
IDX_DIM = 64
ATTN_WIDTH = N_ATTN_HEADS * HEAD_DIM
KV_WIDTH = N_KV_HEADS * HEAD_DIM
DELTA_WIDTH = N_DELTA_HEADS * HEAD_DIM
IDX_WIDTH = N_IDX_HEADS * IDX_DIM
CONV_CHANNELS = 3 * DELTA_WIDTH
TOPK_MAX = 256
ROPE_THETA = 500000.0
ROPE_FRACTION = 4
CONV_WIDTH = 4
DELTA_CHUNK = 64
N_GROUPS = 8
EXPERTS_PER_GROUP = 8
N_EXPERTS = N_GROUPS * EXPERTS_PER_GROUP
N_MOD = 6
EPS = 1e-6
PAGE_SIZE = 128
PROJ_SIZES = (ATTN_WIDTH, KV_WIDTH, KV_WIDTH, IDX_WIDTH, IDX_DIM, N_IDX_HEADS,
              DELTA_WIDTH, DELTA_WIDTH, DELTA_WIDTH, DELTA_WIDTH, N_DELTA_HEADS, N_DELTA_HEADS)

LANES = 128
SUBLANES = 8
VMEM_LIMIT = 56 * 1024 * 1024
FRONT_VMEM_LIMIT = 60 * 1024 * 1024

C_QA = 0
C_QI = 1024
C_ZD = 2048
C_CONV = 3072
C_KA = 6144
C_VA = 6400
C_MISC = 6656
PROJ_PACKED = 6912
M_KI = 0
M_WI = 64
M_BD = 80
M_AD = 88

Q_TILE = 128
KEY_CHUNK = 256
SEL_SPAN = 512
NEG_INF_KEY = -2139095041
SAMPLE_ROWS = 16
SAMPLE_QROWS = SUBLANES
SAMPLE_PAGES_PER_STEP = 16
NEG_BIG = -1e30
INT_MIN = -2147483648
INT_MAX = 2147483647


def _cparams(sem):
    return pltpu.CompilerParams(dimension_semantics=sem, vmem_limit_bytes=VMEM_LIMIT)


def _dot(a, b):
    return jnp.dot(a, b, preferred_element_type=F32)


def _dot_nt(a, b):
    return lax.dot_general(a, b, (((1,), (1,)), ((), ())), preferred_element_type=F32)


def _sigmoid(x):
    return 0.5 * jnp.tanh(0.5 * x) + 0.5


def _silu(x):
    return x * _sigmoid(x)


def _softplus(x):
    return jnp.maximum(x, 0.0) + jnp.log(1.0 + jnp.exp(-jnp.abs(x)))


def _ada_kernel(c_ref, w_ref, b_ref, o_ref):
    s = _silu(c_ref[...]).astype(BF16)
    o_ref[...] = _dot(s, w_ref[...].astype(BF16)) + b_ref[...]


def _ada_modulation(c, w_ada, b_ada):
    r, d = c.shape
    n = w_ada.shape[1]
    tn = 1024 if n % 1024 == 0 else n
    return pl.pallas_call(
        _ada_kernel,
        name="ada_mod",
        grid=(n // tn,),
        in_specs=[pl.BlockSpec((r, d), lambda j: (0, 0)),
                  pl.BlockSpec((d, tn), lambda j: (0, j)),
                  pl.BlockSpec((1, tn), lambda j: (0, j))],
        out_specs=pl.BlockSpec((r, tn), lambda j: (0, j)),
        out_shape=jax.ShapeDtypeStruct((r, n), F32),
        compiler_params=_cparams(("parallel",)),
    )(c, w_ada, b_ada.reshape(1, n))


INPROJ_COLS = 1152


def _inproj_kernel(x_ref, sh_ref, sc_ref, nw_ref, w_ref, o_ref):
    x = x_ref[...]
    y = x * lax.rsqrt(jnp.mean(x * x, axis=-1, keepdims=True) + EPS) * nw_ref[...]
    h = (y * (1.0 + sc_ref[0]) + sh_ref[0]).astype(BF16)
    for c0 in range(0, o_ref.shape[1], INPROJ_COLS):
        o_ref[:, c0:c0 + INPROJ_COLS] = _dot(h, w_ref[:, c0:c0 + INPROJ_COLS])


def _in_projection(x2d, shift, scale, norm_w, w_packed, tm, rows_per_mod_block):
    n, d = x2d.shape
    np_ = w_packed.shape[1]
    r = shift.shape[1]
    tiles_per_mod = rows_per_mod_block // tm
    mod_spec = pl.BlockSpec((1, r, d), lambda i: (i // tiles_per_mod, 0, 0))
    return pl.pallas_call(
        _inproj_kernel,
        name="in_proj",
        grid=(n // tm,),
        in_specs=[pl.BlockSpec((tm, d), lambda i: (i, 0)),
                  mod_spec, mod_spec,
                  pl.BlockSpec((1, d), lambda i: (0, 0)),
                  pl.BlockSpec((d, np_), lambda i: (0, 0), pipeline_mode=pl.Buffered(1))],
        out_specs=pl.BlockSpec((tm, np_), lambda i: (i, 0)),
        out_shape=jax.ShapeDtypeStruct((n, np_), F32),
        compiler_params=_cparams(("parallel",)),
    )(x2d, shift, scale, norm_w.reshape(1, d), w_packed)


def _rope(x, tab, rot):
    c = tab[:, 0:LANES]
    s1 = tab[:, LANES:2 * LANES]
    s2 = tab[:, 2 * LANES:3 * LANES]
    return x * c + pltpu.roll(x, LANES - rot, 1) * s1 + pltpu.roll(x, rot, 1) * s2


def _rms_head(x, w):
    return x * lax.rsqrt(jnp.mean(x * x, axis=-1, keepdims=True) + EPS) * w


def _prep_kernel(qa_ref, qi_ref, ka_ref, va_ref, misc_ref, tabm_ref, tabi_ref, qw_ref, kw_ref, iw_ref,
                 q_ref, kf_ref, kb_ref, vb_ref, qib_ref, kif_ref, kib_ref, *, transpose_v):
    tabm = tabm_ref[0]
    tabi = tabi_ref[0]
    half_main = HEAD_DIM // ROPE_FRACTION // 2
    half_idx = IDX_DIM // ROPE_FRACTION // 2
    for h in range(N_ATTN_HEADS):
        sl = slice(h * HEAD_DIM, (h + 1) * HEAD_DIM)
        y = _rope(_rms_head(qa_ref[:, sl], qw_ref[...]), tabm, half_main)
        q_ref[:, sl] = (y * (HEAD_DIM ** -0.5)).astype(BF16)
    for h in range(N_KV_HEADS):
        sl = slice(h * HEAD_DIM, (h + 1) * HEAD_DIM)
        y = _rope(_rms_head(ka_ref[:, sl], kw_ref[...]), tabm, half_main)
        kf_ref[:, sl] = y
        kb_ref[:, sl] = y.astype(BF16)
    if transpose_v:
        vb_ref[...] = va_ref[...].T.astype(BF16)
    else:
        vb_ref[...] = va_ref[...].astype(BF16)
    for p in range(IDX_WIDTH // LANES):
        sl = slice(p * LANES, (p + 1) * LANES)
        qib_ref[:, sl] = _rope(qi_ref[:, sl], tabi, half_idx).astype(BF16)
    m = misc_ref[...]
    lane = lax.broadcasted_iota(I32, m.shape, 1)
    ki = jnp.where(lane < IDX_DIM, m, 0.0)
    ms = jnp.sum(ki * ki, axis=-1, keepdims=True) * (1.0 / IDX_DIM)
    y = _rope(ki * lax.rsqrt(ms + EPS) * iw_ref[...], tabi, half_idx)
    kif_ref[...] = y[:, 0:IDX_DIM]
    kib_ref[...] = (y + pltpu.roll(y, IDX_DIM, 1)).astype(BF16)


def _rope_tables(pos, head_dim, group):
    d_rot = head_dim // ROPE_FRACTION
    half = d_rot // 2
    inv_freq = jnp.power(ROPE_THETA, -(jnp.arange(half, dtype=F32) * 2.0 / d_rot))
    ang = pos.astype(F32)[:, None] * inv_freq[None, :]
    cos = jnp.cos(ang)
    sin = jnp.sin(ang)
    t = pos.shape[0]
    z = jnp.zeros((t, group - d_rot), F32)
    c = jnp.concatenate([cos, cos, jnp.ones((t, group - d_rot), F32)], axis=1)
    s1 = jnp.concatenate([-sin, jnp.zeros((t, half), F32), z], axis=1)
    s2 = jnp.concatenate([jnp.zeros((t, half), F32), sin, z], axis=1)
    rep = LANES // group
    return jnp.concatenate([jnp.tile(c, (1, rep)), jnp.tile(s1, (1, rep)), jnp.tile(s2, (1, rep))], axis=1)


def _attention_prep(proj, pos, tq, q_norm_w, k_norm_w, idx_k_norm_w, transpose_v):
    n = proj.shape[0]
    p = pos.shape[0]
    g = p // tq
    tabm = _rope_tables(pos, HEAD_DIM, LANES).reshape(g, tq, 3 * LANES)
    tabi = _rope_tables(pos, IDX_DIM, IDX_DIM).reshape(g, tq, 3 * LANES)
    iw = jnp.concatenate([idx_k_norm_w, jnp.zeros((LANES - IDX_DIM,), F32)]).reshape(1, LANES)
    row = lambda w, c: pl.BlockSpec((tq, w), lambda i: (i, c // w))
    tab_spec = pl.BlockSpec((1, tq, 3 * LANES), lambda i: (i % g, 0, 0))
    vec_spec = pl.BlockSpec((1, LANES), lambda i: (0, 0))
    out_row = lambda w: pl.BlockSpec((tq, w), lambda i: (i, 0))
    v_spec = pl.BlockSpec((KV_WIDTH, tq), lambda i: (0, i)) if transpose_v else out_row(KV_WIDTH)
    v_shape = (KV_WIDTH, n) if transpose_v else (n, KV_WIDTH)
    return pl.pallas_call(
        functools.partial(_prep_kernel, transpose_v=transpose_v),
        name="attn_prep",
        grid=(n // tq,),
        in_specs=[row(ATTN_WIDTH, C_QA), row(IDX_WIDTH, C_QI), row(KV_WIDTH, C_KA), row(KV_WIDTH, C_VA),
                  row(LANES, C_MISC), tab_spec, tab_spec, vec_spec, vec_spec, vec_spec],
        out_specs=[out_row(ATTN_WIDTH), out_row(KV_WIDTH), out_row(KV_WIDTH), v_spec,
                   out_row(IDX_WIDTH), out_row(IDX_DIM), out_row(LANES)],
        out_shape=[jax.ShapeDtypeStruct((n, ATTN_WIDTH), BF16),
                   jax.ShapeDtypeStruct((n, KV_WIDTH), F32),
                   jax.ShapeDtypeStruct((n, KV_WIDTH), BF16),
                   jax.ShapeDtypeStruct(v_shape, BF16),
                   jax.ShapeDtypeStruct((n, IDX_WIDTH), BF16),
                   jax.ShapeDtypeStruct((n, IDX_DIM), F32),
                   jax.ShapeDtypeStruct((n, LANES), BF16)],
        compiler_params=_cparams(("parallel",)),
    )(proj, proj, proj, proj, proj, tabm, tabi,
      q_norm_w.reshape(1, LANES), k_norm_w.reshape(1, LANES), iw)


def _sort_key(x):
    b = pltpu.bitcast(x + 0.0, I32)
    return b ^ ((b >> 31) & INT_MAX)


def _kth_largest_key(count_ge, k, shape, n_total):
    def body(it, carry):
        ans_u, n_ge = carry
        bit = jnp.left_shift(jnp.int32(1), 31 - it)
        cand_u = ans_u | bit
        cnt = count_ge(cand_u ^ INT_MIN)
        ok = cnt >= k
        return jnp.where(ok, cand_u, ans_u), jnp.where(ok, cnt, n_ge)

    ans_u, n_ge = lax.fori_loop(0, 32, body, (jnp.zeros(shape, I32), jnp.full(shape, float(n_total), F32)))
    return ans_u ^ INT_MIN, n_ge


def _tie_index_limit(count_eq_le, need, n_keys, shape):
    nbits = max(1, int(n_keys - 1).bit_length())

    def body(it, lo):
        bit = jnp.left_shift(jnp.int32(1), nbits - 1 - it)
        cand = lo | bit
        cnt = count_eq_le(cand - 1)
        return jnp.where(cnt >= need, lo, cand)

    return lax.fori_loop(0, nbits, body, jnp.zeros(shape, I32))


def _dsa_prompt_kernel(q_ref, qi_ref, misc_ref, k_ref, vt_ref, kx_ref, o_ref,
                       key_scr, qsel_scr, qg_scr, thr_scr, lim_scr, m_scr, l_scr, acc_scr, *, n_sel):
    i = pl.program_id(1)
    tq = Q_TILE
    ck = KEY_CHUNK
    n_ch = (i * tq + tq + ck - 1) // ck
    q_pos = i * tq + lax.broadcasted_iota(I32, (1, tq), 1)
    row_k = lax.broadcasted_iota(I32, (ck, 1), 0)

    lo_half = lax.broadcasted_iota(I32, (tq, LANES), 1) < IDX_DIM
    zero = jnp.zeros((), BF16)
    for p in range(IDX_WIDTH // LANES):
        slab = qi_ref[:, p * LANES:(p + 1) * LANES]
        qsel_scr[(2 * p) * tq:(2 * p + 1) * tq, :] = jnp.where(lo_half, slab, zero)
        qsel_scr[(2 * p + 1) * tq:(2 * p + 2) * tq, :] = jnp.where(lo_half, zero, slab)
    w_t = misc_ref[...].T

    def score_chunk(c, carry):
        off = pl.multiple_of(c * ck, ck)
        s = _dot_nt(kx_ref[pl.ds(off, ck), :], qsel_scr[...])
        acc = jnp.zeros((ck, tq), F32)
        for h in range(N_IDX_HEADS):
            acc = acc + w_t[M_WI + h:M_WI + h + 1, :] * jnp.maximum(s[:, h * tq:(h + 1) * tq], 0.0)
        acc = jnp.where(off + row_k <= q_pos, acc, -jnp.inf)
        key_scr[pl.ds(off, ck), :] = _sort_key(acc)
        return carry

    lax.fori_loop(0, n_ch, score_chunk, 0)

    spc = SEL_SPAN // ck
    n_span = (n_ch + spc - 1) // spc
    neg_key = jnp.full((ck, tq), NEG_INF_KEY, I32)

    def pad_chunk(c, carry):
        key_scr[pl.ds(pl.multiple_of(c * ck, ck), ck), :] = neg_key
        return carry

    lax.fori_loop(n_ch, n_span * spc, pad_chunk, 0)

    thr_scr[...] = jnp.full((1, tq), INT_MIN, I32)
    lim_scr[...] = jnp.full((1, tq), INT_MAX, I32)

    def select_threshold(n_keys):
        def count_where(pred):
            tot = jnp.zeros((SUBLANES, tq), F32)
            for c0 in range(0, n_keys, ck):
                hit = pred(key_scr[c0:c0 + ck, :], c0 + row_k).astype(F32)
                tot = tot + jnp.sum(hit.reshape(ck // SUBLANES, SUBLANES, tq), axis=0)
            return jnp.sum(tot, axis=0, keepdims=True)

        t, n_ge = _kth_largest_key(lambda cand: count_where(lambda kk, pos: kk >= cand), float(n_sel), (1, tq),
                                   n_keys)
        thr_scr[...] = t

        @pl.when(jnp.max(n_ge) > float(n_sel))
        def _():
            n_gt = count_where(lambda kk, pos: kk > t)
            lim_scr[...] = _tie_index_limit(
                lambda idx: count_where(lambda kk, pos: jnp.logical_and(kk == t, pos <= idx)),
                float(n_sel) - n_gt, k_ref.shape[0], (1, tq))

    for spans in range(1, k_ref.shape[0] // SEL_SPAN + 1):
        if spans * SEL_SPAN > n_sel:
            pl.when(jnp.logical_and(n_span == spans, (i + 1) * tq > n_sel))(
                functools.partial(select_threshold, spans * SEL_SPAN))

    thr = thr_scr[...]
    lim = lim_scr[...]

    for g in range(N_KV_HEADS):
        for r in range(KV_GROUP):
            h = g * KV_GROUP + r
            qg_scr[g, r * tq:(r + 1) * tq, :] = q_ref[:, h * HEAD_DIM:(h + 1) * HEAD_DIM]
    m_scr[...] = jnp.full(m_scr.shape, NEG_BIG, F32)
    l_scr[...] = jnp.zeros(l_scr.shape, F32)
    acc_scr[...] = jnp.zeros(acc_scr.shape, F32)

    def attend_chunk(c, carry):
        off = pl.multiple_of(c * ck, ck)
        kk = key_scr[pl.ds(off, ck), :]
        pos = off + row_k
        sel = jnp.logical_or(kk > thr, jnp.logical_and(kk == thr, pos <= lim))
        sel = jnp.logical_and(sel, pos <= q_pos)
        def group_steps(g):
            kc = k_ref[pl.ds(off, ck), g * HEAD_DIM:(g + 1) * HEAD_DIM]
            vt = vt_ref[g * HEAD_DIM:(g + 1) * HEAD_DIM, pl.ds(off, ck)]
            qk = _dot_nt(kc, qg_scr[g])
            yield
            s = jnp.concatenate([jnp.where(sel, qk[:, r * tq:(r + 1) * tq], NEG_BIG) for r in range(KV_GROUP)],
                                axis=1)
            m_old = m_scr[g]
            m_new = jnp.maximum(m_old, jnp.max(s, axis=0, keepdims=True))
            yield
            p = jnp.exp(s - m_new)
            alpha = jnp.exp(m_old - m_new)
            l_scr[g] = alpha * l_scr[g] + jnp.sum(p, axis=0, keepdims=True)
            yield
            acc_scr[g] = alpha * acc_scr[g] + _dot(vt, p.astype(BF16))
            m_scr[g] = m_new

        chains = [group_steps(g) for g in range(N_KV_HEADS)]
        while chains:
            chains = [ch for ch in chains if next(ch, "done") != "done"]
        return carry

    lax.fori_loop(0, n_ch, attend_chunk, 0)
    for g in range(N_KV_HEADS):
        o_t = acc_scr[g] / l_scr[g]
        for r in range(KV_GROUP):
            h = g * KV_GROUP + r
            o_ref[:, h * HEAD_DIM:(h + 1) * HEAD_DIM] = o_t[:, r * tq:(r + 1) * tq].T.astype(BF16)


def _dsa_prompt(q_bf, qi_bf, misc, k_bf, vt_bf, kx_bf, b, t):
    n = b * t
    nq = t // Q_TILE
    n_sel = min(TOPK_MAX, t // 4)
    qrow = lambda w: pl.BlockSpec((Q_TILE, w), lambda bb, i: (bb * nq + i, 0))
    seq = lambda w: pl.BlockSpec((t, w), lambda bb, i: (bb, 0))
    return pl.pallas_call(
        functools.partial(_dsa_prompt_kernel, n_sel=n_sel),
        name="dsa_prompt",
        grid=(b, nq),
        in_specs=[qrow(ATTN_WIDTH), qrow(IDX_WIDTH),
                  qrow(LANES),
                  seq(KV_WIDTH), pl.BlockSpec((KV_WIDTH, t), lambda bb, i: (0, bb)), seq(LANES)],
        out_specs=qrow(ATTN_WIDTH),
        out_shape=jax.ShapeDtypeStruct((n, ATTN_WIDTH), BF16),
        scratch_shapes=[pltpu.VMEM((t, Q_TILE), I32),
                        pltpu.VMEM((N_IDX_HEADS * Q_TILE, LANES), BF16),
                        pltpu.VMEM((N_KV_HEADS, KV_GROUP * Q_TILE, HEAD_DIM), BF16),
                        pltpu.VMEM((1, Q_TILE), I32),
                        pltpu.VMEM((1, Q_TILE), I32),
                        pltpu.VMEM((N_KV_HEADS, 1, KV_GROUP * Q_TILE), F32),
                        pltpu.VMEM((N_KV_HEADS, 1, KV_GROUP * Q_TILE), F32),
                        pltpu.VMEM((N_KV_HEADS, HEAD_DIM, KV_GROUP * Q_TILE), F32)],
        compiler_params=_cparams(("parallel", "arbitrary")),
    )(q_bf, qi_bf, misc, k_bf, vt_bf, kx_bf)


def _sample_score_kernel(pt_ref, q_ref, w_ref, kn_ref, *refs, pages, t_valid):
    page_refs = refs[:pages]
    past_ref, new_ref = refs[pages], refs[pages + 1]
    rows = SAMPLE_ROWS
    qr = SAMPLE_QROWS
    q = q_ref[0]
    w = w_ref[0]

    def head_sum(s):
        s = w * jnp.maximum(s, 0.0)
        acc = s[0:qr]
        for h in range(1, N_IDX_HEADS):
            acc = acc + s[h * qr:(h + 1) * qr]
        return acc

    for j in range(pages):
        kp = page_refs[j][0, 0].astype(BF16)
        past_ref[0, :, j * PAGE_SIZE:(j + 1) * PAGE_SIZE] = _sort_key(head_sum(_dot_nt(q, kp)))

    @pl.when(pl.program_id(1) == 0)
    def _():
        kn = jnp.concatenate([kn_ref[...], jnp.zeros((LANES - rows, IDX_DIM), F32)], axis=0).astype(BF16)
        sc = head_sum(_dot_nt(q, kn))
        t = lax.broadcasted_iota(I32, sc.shape, 0)
        s = lax.broadcasted_iota(I32, sc.shape, 1)
        ok = jnp.logical_and(s <= t, s < t_valid)
        new_ref[0] = _sort_key(jnp.where(ok, sc, -jnp.inf))


def _sample_scores(page_table, q_t, w_col, kif, cache_idx, layer, pages, t_valid):
    bs, n_pages = page_table.shape
    past = n_pages * PAGE_SIZE
    hr = N_IDX_HEADS * SAMPLE_QROWS
    page_spec = lambda j: pl.BlockSpec((1, 1, PAGE_SIZE, IDX_DIM),
                                       lambda b, c, pt: (layer, pt[b, c * pages + j], 0, 0))
    grid_spec = pltpu.PrefetchScalarGridSpec(
        num_scalar_prefetch=1,
        grid=(bs, n_pages // pages),
        in_specs=[pl.BlockSpec((1, hr, IDX_DIM), lambda b, c, pt: (b, 0, 0)),
                  pl.BlockSpec((1, hr, 1), lambda b, c, pt: (b, 0, 0)),
                  pl.BlockSpec((SAMPLE_ROWS, IDX_DIM), lambda b, c, pt: (b, 0))]
                 + [page_spec(j) for j in range(pages)],
        out_specs=[pl.BlockSpec((1, SAMPLE_QROWS, pages * PAGE_SIZE), lambda b, c, pt: (b, 0, c)),
                   pl.BlockSpec((1, SAMPLE_QROWS, LANES), lambda b, c, pt: (b, 0, 0))],
    )
    return pl.pallas_call(
        functools.partial(_sample_score_kernel, pages=pages, t_valid=t_valid),
        name="sample_scores",
        grid_spec=grid_spec,
        out_shape=[jax.ShapeDtypeStruct((bs, SAMPLE_QROWS, past), I32),
                   jax.ShapeDtypeStruct((bs, SAMPLE_QROWS, LANES), I32)],
        compiler_params=_cparams(("parallel", "arbitrary")),
    )(page_table, q_t, w_col, kif, *([cache_idx] * pages))


def _sample_attend_kernel(pt_ref, kp_ref, kn_ref, q_ref, knew_ref, vnew_ref, *refs, pages, n_sel, past):
    k_pages = refs[:pages]
    v_pages = refs[pages:2 * pages]
    o_ref = refs[2 * pages]
    thr_scr, lim_scr, m_scr, l_scr, acc_scr = refs[2 * pages + 1:]
    c = pl.program_id(1)
    rows = SAMPLE_ROWS
    span = pages * PAGE_SIZE

    @pl.when(c == 0)
    def _():
        m_scr[...] = jnp.full(m_scr.shape, NEG_BIG, F32)
        l_scr[...] = jnp.zeros(l_scr.shape, F32)
        acc_scr[...] = jnp.zeros(acc_scr.shape, F32)
        vr = SUBLANES
        tile_pos = lax.broadcasted_iota(I32, (vr, LANES), 1)

        def count_where(pred):
            acc = pred(kn_ref[0, 0:vr, :], past + tile_pos).astype(F32)
            for tix in range(past // LANES):
                acc = acc + pred(kp_ref[0, 0:vr, tix * LANES:(tix + 1) * LANES],
                                 tix * LANES + tile_pos).astype(F32)
            return jnp.sum(acc, axis=1, keepdims=True)

        t, n_ge = _kth_largest_key(lambda cand: count_where(lambda kk, pos: kk >= cand), float(n_sel), (vr, 1),
                                   past + LANES)
        thr_scr[...] = jnp.full((rows, 1), INT_MIN, I32)
        lim_scr[...] = jnp.full((rows, 1), INT_MAX, I32)
        thr_scr[0:vr, :] = t

        @pl.when(jnp.max(n_ge) > float(n_sel))
        def _():
            n_gt = count_where(lambda kk, pos: kk > t)
            lim_scr[0:vr, :] = _tie_index_limit(
                lambda idx: count_where(lambda kk, pos: jnp.logical_and(kk == t, pos <= idx)),
                float(n_sel) - n_gt, past + LANES, (vr, 1))

    thr = thr_scr[0:SAMPLE_QROWS, :]
    lim = lim_scr[0:SAMPLE_QROWS, :]
    all_rows = lambda m: jnp.concatenate([m] * (rows // SAMPLE_QROWS), axis=0)

    def update_steps(g, qg, k_fn, v_fn, sel):
        sel = jnp.concatenate([sel] * KV_GROUP, axis=0)
        qk = _dot_nt(qg, k_fn())
        yield
        s = jnp.where(sel, qk, NEG_BIG)
        m_old = m_scr[g]
        m_new = jnp.maximum(m_old, jnp.max(s, axis=1, keepdims=True))
        yield
        p = jnp.where(sel, jnp.exp(s - m_new), 0.0)
        alpha = jnp.exp(m_old - m_new)
        l_scr[g] = alpha * l_scr[g] + jnp.sum(p, axis=1, keepdims=True)
        yield
        acc_scr[g] = alpha * acc_scr[g] + _dot(p.astype(BF16), v_fn())
        m_scr[g] = m_new

    def run_lockstep(chains):
        while chains:
            chains = [ch for ch in chains if next(ch, "done") != "done"]

    def select(kk, pos):
        return jnp.logical_or(kk > thr, jnp.logical_and(kk == thr, pos <= lim))

    def page_cat(page_refs, g):
        return jnp.concatenate([r[0, 0, pl.ds(g, PAGE_SIZE, stride=N_KV_HEADS), :] for r in page_refs],
                               axis=0).astype(BF16)

    off = pl.multiple_of(c * span, span)
    kk = kp_ref[0, :, pl.ds(off, span)]
    sel_past = all_rows(select(kk, off + lax.broadcasted_iota(I32, kk.shape, 1)))
    q_groups = []
    for g in range(N_KV_HEADS):
        qg = jnp.concatenate(
            [q_ref[:, (g * KV_GROUP + r) * HEAD_DIM:(g * KV_GROUP + r + 1) * HEAD_DIM] for r in range(KV_GROUP)],
            axis=0)
        q_groups.append(qg)
    run_lockstep([update_steps(g, q_groups[g], functools.partial(page_cat, k_pages, g),
                               functools.partial(page_cat, v_pages, g), sel_past) for g in range(N_KV_HEADS)])

    @pl.when(c == pl.num_programs(1) - 1)
    def _():
        kn = kn_ref[0]
        lane = lax.broadcasted_iota(I32, kn.shape, 1)
        sel_new = all_rows(jnp.logical_and(select(kn, past + lane), lane < rows))
        pad = jnp.zeros((LANES - rows, KV_WIDTH), BF16)
        k_new = jnp.concatenate([knew_ref[...], pad], axis=0)
        v_new = jnp.concatenate([vnew_ref[...], pad], axis=0)
        head = lambda a, g: (lambda: a[:, g * HEAD_DIM:(g + 1) * HEAD_DIM])
        run_lockstep([update_steps(g, q_groups[g], head(k_new, g), head(v_new, g), sel_new)
                      for g in range(N_KV_HEADS)])
        for g in range(N_KV_HEADS):
            o = acc_scr[g] / l_scr[g]
            for r in range(KV_GROUP):
                h = g * KV_GROUP + r
                o_ref[:, h * HEAD_DIM:(h + 1) * HEAD_DIM] = o[r * rows:(r + 1) * rows].astype(BF16)


def _sample_attend(page_table, keys_past, keys_new, q_bf, k_bf, v_bf, cache_k, cache_v, layer, pages, n_sel):
    bs, n_pages = page_table.shape
    past = n_pages * PAGE_SIZE
    depth, pool = cache_k.shape[0], cache_k.shape[1]
    cache_k = cache_k.reshape(depth, pool, PAGE_SIZE * N_KV_HEADS, HEAD_DIM)
    cache_v = cache_v.reshape(depth, pool, PAGE_SIZE * N_KV_HEADS, HEAD_DIM)
    page_spec = lambda j: pl.BlockSpec((1, 1, PAGE_SIZE * N_KV_HEADS, HEAD_DIM),
                                       lambda b, c, pt: (layer, pt[b, c * pages + j], 0, 0))
    row = lambda w: pl.BlockSpec((SAMPLE_ROWS, w), lambda b, c, pt: (b, 0))
    grid_spec = pltpu.PrefetchScalarGridSpec(
        num_scalar_prefetch=1,
        grid=(bs, n_pages // pages),
        in_specs=[pl.BlockSpec((1, SAMPLE_QROWS, past), lambda b, c, pt: (b, 0, 0)),
                  pl.BlockSpec((1, SAMPLE_QROWS, LANES), lambda b, c, pt: (b, 0, 0)),
                  row(ATTN_WIDTH), row(KV_WIDTH), row(KV_WIDTH)]
                 + [page_spec(j) for j in range(pages)] * 2,
        out_specs=row(ATTN_WIDTH),
        scratch_shapes=[pltpu.VMEM((SAMPLE_ROWS, 1), I32),
                        pltpu.VMEM((SAMPLE_ROWS, 1), I32),
                        pltpu.VMEM((N_KV_HEADS, KV_GROUP * SAMPLE_ROWS, 1), F32),
                        pltpu.VMEM((N_KV_HEADS, KV_GROUP * SAMPLE_ROWS, 1), F32),
                        pltpu.VMEM((N_KV_HEADS, KV_GROUP * SAMPLE_ROWS, HEAD_DIM), F32)],
    )
    return pl.pallas_call(
        functools.partial(_sample_attend_kernel, pages=pages, n_sel=n_sel, past=past),
        name="sample_attend",
        grid_spec=grid_spec,
        out_shape=jax.ShapeDtypeStruct((bs * SAMPLE_ROWS, ATTN_WIDTH), BF16),
        compiler_params=_cparams(("parallel", "arbitrary")),
    )(page_table, keys_past, keys_new, q_bf, k_bf, v_bf, *([cache_k] * pages), *([cache_v] * pages))


def _delta_prep_kernel(x_ref, halo_ref, prev_ref, misc_ref, cw_ref, al_ref, dt_ref,
                       qn_ref, kn_ref, vv_ref, bg_ref, xp_scr, *, tiles_per_seq, t_valid, tt):
    i = pl.program_id(0)
    tile_in_seq = i % tiles_per_seq
    halo = jnp.where(tile_in_seq == 0, prev_ref[0], halo_ref[...])
    xp_scr[0:SUBLANES, :] = halo
    xp_scr[SUBLANES:SUBLANES + tt, :] = x_ref[...]
    base = SUBLANES - (CONV_WIDTH - 1)
    outs = (qn_ref, kn_ref, vv_ref)
    for sec in range(3):
        for h in range(N_DELTA_HEADS):
            col = sec * DELTA_WIDTH + h * HEAD_DIM
            sl = slice(col, col + HEAD_DIM)
            y = xp_scr[base:base + tt, sl] * cw_ref[0:1, sl]
            for j in range(1, CONV_WIDTH):
                y = y + xp_scr[base + j:base + j + tt, sl] * cw_ref[j:j + 1, sl]
            y = _silu(y)
            if sec < 2:
                y = y * lax.rsqrt(jnp.sum(y * y, axis=-1, keepdims=True) + EPS)
            if sec == 0:
                y = y * (HEAD_DIM ** -0.5)
            outs[sec][:, h * HEAD_DIM:(h + 1) * HEAD_DIM] = y
    m = misc_ref[...]
    lane = lax.broadcasted_iota(I32, m.shape, 1)
    row = tile_in_seq * tt + lax.broadcasted_iota(I32, m.shape, 0)
    beta = _sigmoid(m)
    g = -jnp.exp(al_ref[...]) * _softplus(m + dt_ref[...])
    is_b = jnp.logical_and(lane >= M_BD, lane < M_BD + N_DELTA_HEADS)
    is_g = jnp.logical_and(lane >= M_AD, lane < M_AD + N_DELTA_HEADS)
    comb = jnp.where(is_b, beta, jnp.where(is_g, g, 0.0))
    comb = jnp.where(row < t_valid, comb, 0.0)
    bg_ref[...] = pltpu.roll(comb, LANES - M_BD, 1)


def _delta_prep(proj, prev8, conv_w, a_log, dt_bias, b, t, tt, t_valid):
    n = proj.shape[0]
    tiles_per_seq = t // tt
    pad_vec = lambda v: jnp.zeros((1, LANES), F32).at[0, M_AD:M_AD + N_DELTA_HEADS].set(v)
    halo_blocks = tt // SUBLANES
    return pl.pallas_call(
        functools.partial(_delta_prep_kernel, tiles_per_seq=tiles_per_seq, t_valid=t_valid, tt=tt),
        name="delta_prep",
        grid=(n // tt,),
        in_specs=[pl.BlockSpec((tt, CONV_CHANNELS), lambda i: (i, C_CONV // CONV_CHANNELS)),
                  pl.BlockSpec((SUBLANES, CONV_CHANNELS),
                               lambda i: (jnp.maximum(i * halo_blocks - 1, 0), C_CONV // CONV_CHANNELS)),
                  pl.BlockSpec((1, SUBLANES, CONV_CHANNELS), lambda i: (i // tiles_per_seq, 0, 0)),
                  pl.BlockSpec((tt, LANES), lambda i: (i, C_MISC // LANES)),
                  pl.BlockSpec((CONV_WIDTH, CONV_CHANNELS), lambda i: (0, 0)),
                  pl.BlockSpec((1, LANES), lambda i: (0, 0)),
                  pl.BlockSpec((1, LANES), lambda i: (0, 0))],
        out_specs=[pl.BlockSpec((tt, DELTA_WIDTH), lambda i: (i, 0))] * 3
                  + [pl.BlockSpec((tt, LANES), lambda i: (i, 0))],
        out_shape=[jax.ShapeDtypeStruct((n, DELTA_WIDTH), F32)] * 3 + [jax.ShapeDtypeStruct((n, LANES), F32)],
        scratch_shapes=[pltpu.VMEM((SUBLANES + tt, CONV_CHANNELS), F32)],
        compiler_params=_cparams(("parallel",)),
    )(proj, proj, prev8, proj, conv_w, pad_vec(a_log), pad_vec(dt_bias))


def _mm(a, b):
    return _dot(a.astype(BF16), b.astype(BF16))


def _mm_nt(a, b):
    return _dot_nt(a.astype(BF16), b.astype(BF16))


DELTA_INV_BLOCK = 16
DELTA_STACK = 4
DELTA_CHUNKS_PER_STEP = 2


def _delta_chunk_kernel(qn_ref, kn_ref, vv_ref, bg_ref, z_ref, s0_ref, ow_ref, od_ref, so_ref, s_scr, *, n_chunks):
    c = pl.program_id(1)
    cs = DELTA_CHUNK

    @pl.when(c == 0)
    def _():
        s_scr[...] = s0_ref[0]

    ltri = (lax.broadcasted_iota(I32, (cs, cs), 0) >= lax.broadcasted_iota(I32, (cs, cs), 1)).astype(BF16)

    def chunk_gates(ch):
        bg = bg_ref[ch * cs:(ch + 1) * cs, :]
        g1 = bg.astype(BF16)
        r1 = bg - g1.astype(F32)
        g2 = r1.astype(BF16)
        g3 = (r1 - g2.astype(F32)).astype(BF16)
        gc = _dot(ltri, g1) + _dot(ltri, g2) + _dot(ltri, g3)
        return bg, gc, gc.T

    gates = [chunk_gates(ch) for ch in range(n_chunks)]
    state_ready = {}

    gh = DELTA_STACK
    rows = gh * cs
    rr = lax.broadcasted_iota(I32, (rows, rows), 0)
    cc = lax.broadcasted_iota(I32, (rows, rows), 1)
    same = (rr // cs) == (cc // cs)
    causal = jnp.logical_and(same, rr >= cc)
    strict = jnp.logical_and(same, rr > cc)
    eye = (rr == cc).astype(F32)
    row_head = lax.broadcasted_iota(I32, (rows, 1), 0) // cs
    def group_steps(ch, grp):
        heads = [grp * gh + j for j in range(gh)]
        bg, gc, gct = gates[ch]
        r0 = ch * cs
        stack = lambda ref: jnp.concatenate([ref[r0:r0 + cs, h * HEAD_DIM:(h + 1) * HEAD_DIM] for h in heads],
                                            axis=0)
        col = lambda a, lane0: jnp.concatenate([a[:, lane0 + h:lane0 + h + 1] for h in heads], axis=0)
        k = stack(kn_ref)
        q = stack(qn_ref)
        v = stack(vv_ref)
        bcol = col(bg, 0)
        gcc = col(gc, N_DELTA_HEADS)
        gcr = jnp.concatenate([gct[N_DELTA_HEADS + h:N_DELTA_HEADS + h + 1, :] for h in heads], axis=1)
        g_last = [gc[cs - 1:cs, N_DELTA_HEADS + h:N_DELTA_HEADS + h + 1] for h in heads]
        glc = jnp.concatenate([jnp.broadcast_to(gl, (cs, 1)) for gl in g_last], axis=0)
        decay = jnp.exp(jnp.where(causal, gcc - gcr, -jnp.inf))
        kb = k * bcol
        eg = jnp.exp(gcc)
        kq = _mm_nt(jnp.concatenate([kb, q], axis=0), k)
        yield
        a = jnp.where(strict, kq[0:rows] * decay, 0.0)
        intra = jnp.where(causal, kq[rows:2 * rows] * decay, 0.0)
        x = -a
        nb = DELTA_INV_BLOCK
        y = jnp.where((rr // nb) == (cc // nb), x, 0.0)
        p = eye + y
        y = _mm(y, y)
        yield
        n_sq = max(1, int(nb - 1).bit_length())
        for lvl in range(1, n_sq):
            if lvl < n_sq - 1:
                py = _mm(jnp.concatenate([p, y], axis=0), y)
                p = p + py[0:rows]
                y = py[rows:2 * rows]
            else:
                p = p + _mm(p, y)
            yield
        size = 2 * nb
        while size <= cs:
            off = jnp.where(jnp.logical_and((rr // size) == (cc // size), (rr // (size // 2)) != (cc // (size // 2))),
                            x, 0.0)
            po = _mm(p, off)
            yield
            p = p + _mm(po, p)
            yield
            size *= 2
        sol = _mm(p, jnp.concatenate([v * bcol, kb * eg], axis=1))
        yield
        u = sol[:, 0:HEAD_DIM]
        w = sol[:, HEAD_DIM:2 * HEAD_DIM]
        lanes_g = slice(grp * gh * HEAD_DIM, (grp + 1) * gh * HEAD_DIM)
        while ch > 0 and not state_ready.get((ch - 1, grp)):
            yield
        s_g = s_scr[:, lanes_g]
        wq_s = _mm(jnp.concatenate([w, q * eg], axis=0), s_g)
        yield
        own = lambda m, r0: jnp.concatenate(
            [m[r0 + j * cs:r0 + (j + 1) * cs, j * HEAD_DIM:(j + 1) * HEAD_DIM] for j in range(gh)], axis=0)
        v_new = u - own(wq_s, 0)
        o = own(wq_s, rows) + _mm(intra, v_new)
        yield
        kg_t = (k * jnp.exp(glc - gcc)).T
        vn_blocks = jnp.concatenate([jnp.where(row_head == j, v_new, 0.0) for j in range(gh)], axis=1)
        s_decay = jnp.concatenate([jnp.broadcast_to(jnp.exp(gl), (1, HEAD_DIM)) for gl in g_last], axis=1)
        s_scr[:, lanes_g] = s_g * s_decay + _mm(kg_t, vn_blocks)
        state_ready[(ch, grp)] = True
        yield
        on = o * lax.rsqrt(jnp.mean(o * o, axis=-1, keepdims=True) + EPS) * ow_ref[...]
        for j, h in enumerate(heads):
            sl = slice(h * HEAD_DIM, (h + 1) * HEAD_DIM)
            od_ref[r0:r0 + cs, sl] = (on[j * cs:(j + 1) * cs] * _silu(z_ref[r0:r0 + cs, sl])).astype(BF16)

    chains = [group_steps(ch, grp) for ch in range(n_chunks) for grp in range(N_DELTA_HEADS // gh)]
    while chains:
        chains = [g for g in chains if next(g, "done") != "done"]

    so_ref[0] = s_scr[...]


def _delta_chunks(qn, kn, vv, bg, zsrc, z_col_block, state0, o_norm_w, b, t):
    n = b * t
    per_step = DELTA_CHUNKS_PER_STEP if (t // DELTA_CHUNK) % DELTA_CHUNKS_PER_STEP == 0 else 1
    nc = t // (DELTA_CHUNK * per_step)
    sw = N_DELTA_HEADS * HEAD_DIM
    row = lambda w, cb=0: pl.BlockSpec((DELTA_CHUNK * per_step, w), lambda bb, c: (bb * nc + c, cb))
    st = pl.BlockSpec((1, HEAD_DIM, sw), lambda bb, c: (bb, 0, 0))
    s_in = state0.transpose(0, 2, 1, 3).reshape(b, HEAD_DIM, sw)
    od, s_out = pl.pallas_call(
        functools.partial(_delta_chunk_kernel, n_chunks=per_step),
        name="delta_chunks",
        grid=(b, nc),
        in_specs=[row(DELTA_WIDTH), row(DELTA_WIDTH), row(DELTA_WIDTH), row(LANES),
                  row(DELTA_WIDTH, z_col_block), st, pl.BlockSpec((1, LANES), lambda bb, c: (0, 0))],
        out_specs=[row(DELTA_WIDTH), st],
        out_shape=[jax.ShapeDtypeStruct((n, DELTA_WIDTH), BF16),
                   jax.ShapeDtypeStruct((b, HEAD_DIM, sw), F32)],
        scratch_shapes=[pltpu.VMEM((HEAD_DIM, sw), F32)],
        compiler_params=_cparams(("parallel", "arbitrary")),
    )(qn, kn, vv, bg, zsrc, s_in, o_norm_w.reshape(1, LANES))
    return od, s_out.reshape(b, HEAD_DIM, N_DELTA_HEADS, HEAD_DIM).transpose(0, 2, 1, 3)


def _outproj_kernel(oap_ref, odp_ref, xp_ref, g1p_ref, shp_ref, scp_ref,
                    oas_ref, ods_ref, xs_ref, g1s_ref, shs_ref, scs_ref,
                    nw_ref, wo_ref, wrh_ref, br_ref, x1_ref, h2_ref, lg_ref, *, prompt_tiles):
    def body(oa_ref, od_ref, x_ref, g1_ref, sh_ref, sc_ref):
        mix = (_dot(oa_ref[...], wo_ref[0:ATTN_WIDTH, :])
               + _dot(od_ref[...], wo_ref[ATTN_WIDTH:ATTN_WIDTH + DELTA_WIDTH, :]))
        x1 = x_ref[...] + g1_ref[0] * mix
        x1_ref[...] = x1
        y = x1 * lax.rsqrt(jnp.mean(x1 * x1, axis=-1, keepdims=True) + EPS) * nw_ref[...]
        h2 = y * (1.0 + sc_ref[0]) + sh_ref[0]
        h2_ref[...] = h2
        hb = h2.astype(BF16)
        lo = (h2 - hb.astype(F32)).astype(BF16)
        wr = wrh_ref[...]
        both = _dot(hb, wr)
        lg_ref[...] = both[:, 0:LANES] + both[:, LANES:2 * LANES] + _dot(lo, wr[:, 0:LANES]) + br_ref[...]

    i = pl.program_id(0)
    pl.when(i < prompt_tiles)(functools.partial(body, oap_ref, odp_ref, xp_ref, g1p_ref, shp_ref, scp_ref))
    pl.when(i >= prompt_tiles)(functools.partial(body, oas_ref, ods_ref, xs_ref, g1s_ref, shs_ref, scs_ref))


def _out_projection(prompt, sample, norm2_w, w_out_bf, wr_both, b_rt, tm, prompt_seq_len):
    n_p, d = prompt[2].shape
    n_s = sample[2].shape[0]
    pt = n_p // tm
    tiles_per_seq = prompt_seq_len // tm
    p_row = lambda w: pl.BlockSpec((tm, w), lambda i: (jnp.minimum(i, pt - 1), 0))
    s_row = lambda w: pl.BlockSpec((tm, w), lambda i: (jnp.maximum(i - pt, 0), 0))
    p_mod = pl.BlockSpec((1, 1, d), lambda i: (jnp.minimum(i, pt - 1) // tiles_per_seq, 0, 0))
    s_mod = pl.BlockSpec((1, tm, d), lambda i: (jnp.maximum(i - pt, 0), 0, 0))
    row = lambda w: pl.BlockSpec((tm, w), lambda i: (i, 0))
    full = lambda a: pl.BlockSpec(a.shape, lambda i: (0, 0))
    n = n_p + n_s
    return pl.pallas_call(
        functools.partial(_outproj_kernel, prompt_tiles=pt),
        name="out_proj",
        grid=(n // tm,),
        in_specs=[p_row(ATTN_WIDTH), p_row(DELTA_WIDTH), p_row(d), p_mod, p_mod, p_mod,
                  s_row(ATTN_WIDTH), s_row(DELTA_WIDTH), s_row(d), s_mod, s_mod, s_mod,
                  pl.BlockSpec((1, d), lambda i: (0, 0)), full(w_out_bf), full(wr_both), full(b_rt)],
        out_specs=[row(d), row(d), row(LANES)],
        out_shape=[jax.ShapeDtypeStruct((n, d), F32), jax.ShapeDtypeStruct((n, d), F32),
                   jax.ShapeDtypeStruct((n, LANES), F32)],
        compiler_params=_cparams(("parallel",)),
    )(*prompt, *sample, norm2_w.reshape(1, d), w_out_bf, wr_both, b_rt)


def _route_tile(x):
    lane = lax.broadcasted_iota(I32, x.shape, 1)
    gl = jnp.where(lane < N_GROUPS, x, -jnp.inf)
    ge = jnp.exp(gl - jnp.max(gl, axis=1, keepdims=True))
    p = ge / jnp.sum(ge, axis=1, keepdims=True)
    p_max = jnp.max(p, axis=1, keepdims=True)
    grp = jnp.min(jnp.where(p == p_max, lane, LANES), axis=1, keepdims=True)
    e_lane = lane - N_GROUPS
    in_grp = jnp.logical_and(jnp.logical_and(e_lane >= 0, e_lane < N_EXPERTS),
                             (e_lane >> 3) == grp)
    rl = jnp.where(in_grp, x, -jnp.inf)
    v1 = jnp.max(rl, axis=1, keepdims=True)
    i1 = jnp.min(jnp.where(rl == v1, lane, LANES), axis=1, keepdims=True)
    rl2 = jnp.where(lane == i1, -jnp.inf, rl)
    v2 = jnp.max(rl2, axis=1, keepdims=True)
    i2 = jnp.min(jnp.where(rl2 == v2, lane, LANES), axis=1, keepdims=True)
    t = jnp.exp(v2 - v1)
    den = 1.0 + t
    eid = jnp.where(lane == 0, i1 - N_GROUPS, jnp.where(lane == 1, i2 - N_GROUPS, 0))
    gate = jnp.where(lane == 0, (1.0 / den) * p_max, jnp.where(lane == 1, (t / den) * p_max, 0.0))
    return eid, gate


def _route_kernel(lg_ref, eid_ref, gate_ref):
    eid_ref[...], gate_ref[...] = _route_tile(lg_ref[...])


def _route(logits, tm):
    n = logits.shape[0]
    spec = pl.BlockSpec((tm, LANES), lambda i: (i, 0))
    return pl.pallas_call(
        _route_kernel,
        name="route",
        grid=(n // tm,),
        in_specs=[spec],
        out_specs=[spec, spec],
        out_shape=[jax.ShapeDtypeStruct((n, LANES), I32), jax.ShapeDtypeStruct((n, LANES), F32)],
        compiler_params=_cparams(("parallel",)),
    )(logits)


def _row_gather(idx_ref, base, n_rows, src_hbm, dst, sem):
    def body(r, carry):
        pltpu.make_async_copy(src_hbm.at[pl.ds(idx_ref[base + r], 1), :], dst.at[pl.ds(r, 1), :], sem).start()
        return carry

    lax.fori_loop(0, n_rows, body, 0, unroll=8)


def _row_gather_wait(n_rows, src_hbm, dst, sem):
    pltpu.make_async_copy(src_hbm.at[pl.ds(0, n_rows), :], dst, sem).wait()


def _moe_kernel(tok_ref, j0_ref, be_ref, na_ref, h_hbm, wg_ref, wu_ref, wd_ref, o_ref,
                x_even, x_odd, sem, wg_scr, wu_scr, wd_scr, *, bm):
    i = pl.program_id(0)
    n_act = na_ref[0]
    bufs = ((x_even, sem.at[0]), (x_odd, sem.at[1]))

    @pl.when(i == 0)
    def _():
        _row_gather(tok_ref, j0_ref[0], bm, h_hbm, x_even, sem.at[0])

    changed = jnp.logical_or(i == 0, be_ref[i] != be_ref[jnp.maximum(i - 1, 0)])

    @pl.when(jnp.logical_and(i < n_act, changed))
    def _():
        wg_scr[...] = wg_ref[0].astype(BF16)
        wu_scr[...] = wu_ref[0].astype(BF16)
        wd_scr[...] = wd_ref[0].astype(BF16)

    for parity in range(2):
        cur, cur_sem = bufs[parity]
        nxt, nxt_sem = bufs[1 - parity]

        @pl.when(jnp.logical_and(i < n_act, i % 2 == parity))
        def _():
            _row_gather_wait(bm, h_hbm, cur, cur_sem)
            base = j0_ref[i + 1]
            for r in range(bm):
                pltpu.make_async_copy(h_hbm.at[pl.ds(tok_ref[base + r], 1), :], nxt.at[pl.ds(r, 1), :],
                                      nxt_sem).start()
            x = cur[...].astype(BF16)
            hid = _silu(_dot(x, wg_scr[...])) * _dot(x, wu_scr[...])
            o_ref[...] = _dot(hid.astype(BF16), wd_scr[...])

        @pl.when(jnp.logical_and(i == n_act, i % 2 == parity))
        def _():
            _row_gather_wait(bm, h_hbm, cur, cur_sem)

    @pl.when(i >= n_act)
    def _():
        o_ref[...] = jnp.zeros(o_ref.shape, F32)


def _moe_experts(tok_sorted, block_j0, block_exp, n_active, h2, w_gate, w_up, w_down, bm):
    ns = block_exp.shape[0] * bm
    d = h2.shape[1]
    f = w_gate.shape[2]
    grid_spec = pltpu.PrefetchScalarGridSpec(
        num_scalar_prefetch=4,
        grid=(ns // bm,),
        in_specs=[pl.BlockSpec(memory_space=pl.ANY),
                  pl.BlockSpec((1, d, f), lambda i, tok, j0, be, na: (be[i], 0, 0)),
                  pl.BlockSpec((1, d, f), lambda i, tok, j0, be, na: (be[i], 0, 0)),
                  pl.BlockSpec((1, f, d), lambda i, tok, j0, be, na: (be[i], 0, 0))],
        out_specs=pl.BlockSpec((bm, d), lambda i, tok, j0, be, na: (i, 0)),
        scratch_shapes=[pltpu.VMEM((bm, d), F32), pltpu.VMEM((bm, d), F32), pltpu.SemaphoreType.DMA((2,)),
                        pltpu.VMEM((d, f), BF16), pltpu.VMEM((d, f), BF16), pltpu.VMEM((f, d), BF16)],
    )
    return pl.pallas_call(
        functools.partial(_moe_kernel, bm=bm),
        name="moe_experts",
        grid_spec=grid_spec,
        out_shape=jax.ShapeDtypeStruct((ns, d), F32),
        compiler_params=_cparams(("arbitrary",)),
    )(tok_sorted, block_j0, block_exp, n_active, h2, w_gate, w_up, w_down)


def _combine_kernel(dest_ref, x1_ref, gt_ref, g2p_ref, g2s_ref, y_hbm, op_ref, os_ref, y_buf, sem, *,
                    tm, prompt_tiles):
    i = pl.program_id(0)
    n = pl.num_programs(0)
    slot = i % 2

    @pl.when(i == 0)
    def _():
        _row_gather(dest_ref, 0, 2 * tm, y_hbm, y_buf.at[0], sem.at[0])

    for parity in range(2):
        @pl.when(jnp.logical_and(i + 1 < n, slot == parity))
        def _():
            base = (i + 1) * 2 * tm
            for r in range(2 * tm):
                pltpu.make_async_copy(y_hbm.at[pl.ds(dest_ref[base + r], 1), :],
                                      y_buf.at[1 - parity, pl.ds(r, 1), :], sem.at[1 - parity]).start()

    _row_gather_wait(2 * tm, y_hbm, y_buf.at[slot], sem.at[slot])
    gt = gt_ref[...]
    y = y_buf[slot, 0:tm, :] * gt[:, 0:1] + y_buf[slot, tm:2 * tm, :] * gt[:, 1:2]

    @pl.when(i < prompt_tiles)
    def _():
        op_ref[...] = x1_ref[...] + g2p_ref[0] * y

    @pl.when(i >= prompt_tiles)
    def _():
        os_ref[...] = x1_ref[...] + g2s_ref[0] * y


def _combine(x1, dest, y_rows, gates, gate2_p, gate2_s, tm, n_prompt, prompt_seq_len):
    n, d = x1.shape
    pt = n_prompt // tm
    tiles_per_seq = prompt_seq_len // tm
    dest_tiles = dest.reshape(n // tm, tm, 2).transpose(0, 2, 1).reshape(-1)
    row = lambda w: pl.BlockSpec((tm, w), lambda i, dst: (i, 0))
    grid_spec = pltpu.PrefetchScalarGridSpec(
        num_scalar_prefetch=1,
        grid=(n // tm,),
        in_specs=[row(d), row(LANES),
                  pl.BlockSpec((1, 1, d), lambda i, dst: (jnp.minimum(i, pt - 1) // tiles_per_seq, 0, 0)),
                  pl.BlockSpec((1, tm, d), lambda i, dst: (jnp.maximum(i - pt, 0), 0, 0)),
                  pl.BlockSpec(memory_space=pl.ANY)],
        out_specs=[pl.BlockSpec((tm, d), lambda i, dst: (jnp.minimum(i, pt - 1), 0)),
                   pl.BlockSpec((tm, d), lambda i, dst: (jnp.maximum(i - pt, 0), 0))],
        scratch_shapes=[pltpu.VMEM((2, 2 * tm, d), F32), pltpu.SemaphoreType.DMA((2,))],
    )
    return pl.pallas_call(
        functools.partial(_combine_kernel, tm=tm, prompt_tiles=pt),
        name="moe_combine",
        grid_spec=grid_spec,
        out_shape=[jax.ShapeDtypeStruct((n_prompt, d), F32), jax.ShapeDtypeStruct((n - n_prompt, d), F32)],
        compiler_params=_cparams(("arbitrary",)),
    )(dest_tiles, x1, gates, gate2_p, gate2_s, y_rows)


FRONT_SECTIONS = ((C_QA, ATTN_WIDTH), (C_QI, IDX_WIDTH),
                  (C_CONV, DELTA_WIDTH), (C_CONV + DELTA_WIDTH, DELTA_WIDTH), (C_CONV + 2 * DELTA_WIDTH, DELTA_WIDTH),
                  (C_KA, PROJ_PACKED - C_KA), (C_ZD, DELTA_WIDTH))


def _prompt_front_kernel(x_ref, sh_ref, sc_ref, nw_ref, w_ref, tabm_ref, tabi_ref, qw_ref, kw_ref, iw_ref,
                         prev_ref, cw_ref, al_ref, dt_ref,
                         q_ref, qi_ref, z_ref, qn_ref, kn_ref, vv_ref, kf_ref, kb_ref, vf_ref, vt_ref,
                         kif_ref, kx_ref, bg_ref, misc_ref, tail_ref,
                         carry_scr, xp_scr, *, tiles_per_seq, tm):
    i = pl.program_id(0)
    tile_in_seq = i % tiles_per_seq
    x = x_ref[...]
    y = x * lax.rsqrt(jnp.mean(x * x, axis=-1, keepdims=True) + EPS) * nw_ref[...]
    h = (y * (1.0 + sc_ref[0]) + sh_ref[0]).astype(BF16)
    tabm = tabm_ref[0]
    tabi = tabi_ref[0]
    half_main = HEAD_DIM // ROPE_FRACTION // 2
    half_idx = IDX_DIM // ROPE_FRACTION // 2

    @pl.when(tile_in_seq == 0)
    def _():
        carry_scr[...] = prev_ref[0]

    def queries(pj):
        for hh in range(N_ATTN_HEADS):
            sl = slice(hh * HEAD_DIM, (hh + 1) * HEAD_DIM)
            v = _rope(_rms_head(pj[:, sl], qw_ref[...]), tabm, half_main)
            q_ref[:, sl] = (v * (HEAD_DIM ** -0.5)).astype(BF16)

    def index_queries(pj):
        for p in range(IDX_WIDTH // LANES):
            sl = slice(p * LANES, (p + 1) * LANES)
            qi_ref[:, sl] = _rope(pj[:, sl], tabi, half_idx).astype(BF16)

    def gate_z(pj):
        z_ref[...] = pj

    def conv_section(sec, pj):
        cols = slice(sec * DELTA_WIDTH, (sec + 1) * DELTA_WIDTH)
        xp_scr[sec, 0:SUBLANES, :] = carry_scr[:, cols]
        xp_scr[sec, SUBLANES:SUBLANES + tm, :] = pj
        carry_scr[:, cols] = pj[tm - SUBLANES:tm, :]
        base = SUBLANES - (CONV_WIDTH - 1)
        out = (qn_ref, kn_ref, vv_ref)[sec]
        for hh in range(N_DELTA_HEADS):
            sl = slice(hh * HEAD_DIM, (hh + 1) * HEAD_DIM)
            wsl = slice(sec * DELTA_WIDTH + hh * HEAD_DIM, sec * DELTA_WIDTH + (hh + 1) * HEAD_DIM)
            v = xp_scr[sec, base:base + tm, sl] * cw_ref[0:1, wsl]
            for j in range(1, CONV_WIDTH):
                v = v + xp_scr[sec, base + j:base + j + tm, sl] * cw_ref[j:j + 1, wsl]
            v = _silu(v)
            if sec < 2:
                v = v * lax.rsqrt(jnp.sum(v * v, axis=-1, keepdims=True) + EPS)
            if sec == 0:
                v = v * (HEAD_DIM ** -0.5)
            out[:, sl] = v
        if sec == 2:
            tail_ref[0] = carry_scr[...]

    def keys_values_misc(pj):
        for hk in range(N_KV_HEADS):
            sl = slice(hk * HEAD_DIM, (hk + 1) * HEAD_DIM)
            v = _rope(_rms_head(pj[:, sl], kw_ref[...]), tabm, half_main)
            kf_ref[:, sl] = v
            kb_ref[:, sl] = v.astype(BF16)
        va = pj[:, KV_WIDTH:2 * KV_WIDTH]
        vf_ref[...] = va
        vt_ref[...] = va.T.astype(BF16)
        m = pj[:, 2 * KV_WIDTH:2 * KV_WIDTH + LANES]
        misc_ref[...] = m
        lane = lax.broadcasted_iota(I32, m.shape, 1)
        ki = jnp.where(lane < IDX_DIM, m, 0.0)
        ms = jnp.sum(ki * ki, axis=-1, keepdims=True) * (1.0 / IDX_DIM)
        v = _rope(ki * lax.rsqrt(ms + EPS) * iw_ref[...], tabi, half_idx)
        kif_ref[...] = v[:, 0:IDX_DIM]
        kx_ref[...] = (v + pltpu.roll(v, IDX_DIM, 1)).astype(BF16)
        beta = _sigmoid(m)
        g = -jnp.exp(al_ref[...]) * _softplus(m + dt_ref[...])
        is_b = jnp.logical_and(lane >= M_BD, lane < M_BD + N_DELTA_HEADS)
        is_g = jnp.logical_and(lane >= M_AD, lane < M_AD + N_DELTA_HEADS)
        comb = jnp.where(is_b, beta, jnp.where(is_g, g, 0.0))
        bg_ref[...] = pltpu.roll(comb, LANES - M_BD, 1)

    epilogues = (queries, index_queries,
                 functools.partial(conv_section, 0), functools.partial(conv_section, 1),
                 functools.partial(conv_section, 2), keys_values_misc, gate_z)
    project = lambda k: _dot(h, w_ref[:, FRONT_SECTIONS[k][0]:FRONT_SECTIONS[k][0] + FRONT_SECTIONS[k][1]])
    pj_next = project(0)
    for k, epilogue in enumerate(epilogues):
        pj = pj_next
        if k + 1 < len(epilogues):
            pj_next = project(k + 1)
        epilogue(pj)


def _prompt_front(x2d, shift, scale, norm_w, w_packed, pos, q_norm_w, k_norm_w, idx_k_norm_w, prev8, conv_w,
                  a_log, dt_bias, b, t, tm):
    n, d = x2d.shape
    np_ = w_packed.shape[1]
    tiles_per_seq = t // tm
    tabm = _rope_tables(pos, HEAD_DIM, LANES).reshape(tiles_per_seq, tm, 3 * LANES)
    tabi = _rope_tables(pos, IDX_DIM, IDX_DIM).reshape(tiles_per_seq, tm, 3 * LANES)
    iw = jnp.concatenate([idx_k_norm_w, jnp.zeros((LANES - IDX_DIM,), F32)]).reshape(1, LANES)
    pad_vec = lambda v: jnp.zeros((1, LANES), F32).at[0, M_AD:M_AD + N_DELTA_HEADS].set(v)
    mod_spec = pl.BlockSpec((1, 1, d), lambda i: (i // tiles_per_seq, 0, 0))
    tab_spec = pl.BlockSpec((1, tm, 3 * LANES), lambda i: (i % tiles_per_seq, 0, 0))
    vec = lambda w: pl.BlockSpec((1, w), lambda i: (0, 0))
    row = lambda w: pl.BlockSpec((tm, w), lambda i: (i, 0))
    seq_state = pl.BlockSpec((1, SUBLANES, CONV_CHANNELS), lambda i: (i // tiles_per_seq, 0, 0))
    widths = [(ATTN_WIDTH, BF16), (IDX_WIDTH, BF16), (DELTA_WIDTH, F32), (DELTA_WIDTH, F32), (DELTA_WIDTH, F32),
              (DELTA_WIDTH, F32), (KV_WIDTH, F32), (KV_WIDTH, BF16), (KV_WIDTH, F32)]
    outs = pl.pallas_call(
        functools.partial(_prompt_front_kernel, tiles_per_seq=tiles_per_seq, tm=tm),
        name="prompt_front",
        grid=(n // tm,),
        in_specs=[row(d), mod_spec, mod_spec, vec(d),
                  pl.BlockSpec((d, np_), lambda i: (0, 0), pipeline_mode=pl.Buffered(1)),
                  tab_spec, tab_spec, vec(LANES), vec(LANES), vec(LANES), seq_state,
                  pl.BlockSpec((CONV_WIDTH, CONV_CHANNELS), lambda i: (0, 0)), vec(LANES), vec(LANES)],
        out_specs=[row(w) for w, _ in widths]
                  + [pl.BlockSpec((KV_WIDTH, tm), lambda i: (0, i)), row(IDX_DIM), row(LANES), row(LANES), row(LANES),
                     seq_state],
        out_shape=[jax.ShapeDtypeStruct((n, w), dt) for w, dt in widths]
                  + [jax.ShapeDtypeStruct((KV_WIDTH, n), BF16), jax.ShapeDtypeStruct((n, IDX_DIM), F32),
                     jax.ShapeDtypeStruct((n, LANES), BF16), jax.ShapeDtypeStruct((n, LANES), F32),
                     jax.ShapeDtypeStruct((n, LANES), F32),
                     jax.ShapeDtypeStruct((b, SUBLANES, CONV_CHANNELS), F32)],
        scratch_shapes=[pltpu.VMEM((SUBLANES, CONV_CHANNELS), F32),
                        pltpu.VMEM((3, SUBLANES + tm, DELTA_WIDTH), F32)],
        compiler_params=pltpu.CompilerParams(dimension_semantics=("arbitrary",),
                                             vmem_limit_bytes=FRONT_VMEM_LIMIT),
    )(x2d, shift, scale, norm_w.reshape(1, d), w_packed, tabm, tabi,
      q_norm_w.reshape(1, LANES), k_norm_w.reshape(1, LANES), iw, prev8, conv_w, pad_vec(a_log), pad_vec(dt_bias))
    names = ("q", "qi", "z", "qn", "kn", "vv", "kf", "kb", "vf", "vt", "kif", "kx", "bg", "misc", "tail")
    return dict(zip(names, outs))


def _pick_tile(n, pref, mult=16):
    t = min(pref, n)
    while n % t or t % mult:
        t -= 1
    return t


def _pack_w_in(w_in):
    d = w_in.shape[0]
    bounds = np.cumsum(PROJ_SIZES)[:-1].tolist()
    qa, ka, va, qi, ki, wi, qd, kd, vd, zd, bd, ad = jnp.split(w_in, bounds, axis=1)
    used = IDX_DIM + N_IDX_HEADS + 2 * N_DELTA_HEADS
    misc = jnp.concatenate([ki, wi, bd, ad, jnp.zeros((d, LANES - used), w_in.dtype)], axis=1)
    cols = [qa, qi, zd, qd, kd, vd, ka, va, misc]
    width = sum(c.shape[1] for c in cols)
    cols.append(jnp.zeros((d, PROJ_PACKED - width), w_in.dtype))
    return jnp.concatenate(cols, axis=1).astype(BF16)


def _route_and_sort(eid, bm):
    n = eid.shape[0]
    nk = 2 * n
    flat_e = eid.reshape(-1)
    order = jnp.argsort(flat_e, stable=True).astype(I32)
    inv = jnp.argsort(order).astype(I32)
    onehot = flat_e[:, None] == jnp.arange(N_EXPERTS, dtype=I32)[None, :]
    counts = jnp.sum(onehot.astype(I32), axis=0)
    padded = (counts + bm - 1) // bm * bm
    pad_end = jnp.cumsum(padded)
    shift = (pad_end - padded) - (jnp.cumsum(counts) - counts)
    dest = inv + jnp.sum(jnp.where(onehot, shift[None, :], 0), axis=1)
    n_blocks = -(-nk // bm) + N_EXPERTS + 1
    block_exp = jnp.minimum(jnp.searchsorted(pad_end, jnp.arange(n_blocks, dtype=I32) * bm, side='right'),
                            N_EXPERTS - 1).astype(I32)
    n_active = (pad_end[-1] // bm).astype(I32).reshape(1)
    block_j0 = jnp.clip(jnp.arange(n_blocks, dtype=I32) * bm - shift[block_exp], 0, nk)
    tok_sorted = jnp.concatenate([order // 2, jnp.zeros((bm,), I32)])
    return tok_sorted, block_j0, dest.astype(I32).reshape(n, 2), block_exp, n_active


def _layer(layer, yp, ys, cache_k, cache_v, cache_idx, state_ssm, state_conv, page_table, c_prompt, c_sample,
           w_in, w_out, conv_w, a_log, dt_bias, q_norm_w, k_norm_w, idx_k_norm_w, o_norm_w, norm1_w, norm2_w,
           w_ada, b_ada, w_group, b_group, w_router, b_router, w_gate, w_up, w_down):
    bp, tp, d = yp.shape
    bs, ts, _ = ys.shape
    past = page_table.shape[1] * PAGE_SIZE
    rows = SAMPLE_ROWS
    assert CONV_WIDTH - 1 <= ts <= SUBLANES <= rows and tp % SEL_SPAN == 0 and tp % DELTA_CHUNK == 0

    n_c = bp + bs
    n_c_pad = -(-n_c // SUBLANES) * SUBLANES
    c_all = jnp.concatenate([c_prompt, c_sample, jnp.zeros((n_c_pad - n_c, d), F32)], axis=0)
    mod = _ada_modulation(c_all, w_ada, b_ada)
    mods = jnp.split(mod, N_MOD, axis=1)
    mp = [m[:bp].reshape(bp, 1, d) for m in mods]
    ms = [jnp.repeat(m[bp:bp + bs], rows, axis=0).reshape(1, bs * rows, d) for m in mods]

    w_packed = _pack_w_in(w_in)
    w_out_bf = w_out.astype(BF16)
    w_rt = jnp.concatenate([w_group, w_router, jnp.zeros((d, LANES - N_GROUPS - N_EXPERTS), F32)], axis=1)
    wr_hi = w_rt.astype(BF16)
    wr_both = jnp.concatenate([wr_hi, (w_rt - wr_hi.astype(F32)).astype(BF16)], axis=1)
    b_rt = jnp.concatenate([b_group, b_router, jnp.zeros((LANES - N_GROUPS - N_EXPERTS,), F32)]).reshape(1, LANES)

    np_ = bp * tp
    xp2 = yp.reshape(np_, d)
    tm_p = _pick_tile(tp, 256)
    fr = _prompt_front(xp2, mp[0], mp[1], norm1_w, w_packed, jnp.arange(tp), q_norm_w, k_norm_w, idx_k_norm_w,
                       jnp.zeros((bp, SUBLANES, CONV_CHANNELS), F32), conv_w, a_log, dt_bias, bp, tp, tm_p)
    kf_p, kif_p = fr["kf"], fr["kif"]
    oa_p = _dsa_prompt(fr["q"], fr["qi"], fr["misc"], fr["kb"], fr["vt"], fr["kx"], bp, tp)
    od_p, ssm_p = _delta_chunks(fr["qn"], fr["kn"], fr["vv"], fr["bg"], fr["z"], 0,
                                jnp.zeros((bp, N_DELTA_HEADS, HEAD_DIM, HEAD_DIM), F32), o_norm_w, bp, tp)

    ns_ = bs * rows
    xs2 = jnp.pad(ys, ((0, 0), (0, rows - ts), (0, 0))).reshape(ns_, d)
    tm_s = _pick_tile(ns_, 256)
    proj_s = _in_projection(xs2, ms[0].reshape(ns_ // tm_s, tm_s, d), ms[1].reshape(ns_ // tm_s, tm_s, d),
                            norm1_w, w_packed, tm_s, tm_s)
    q_s, kf_s, kb_s, vb_s, qi_s, kif_s, _ = _attention_prep(
        proj_s, past + jnp.arange(rows), rows, q_norm_w, k_norm_w, idx_k_norm_w, False)
    qr = SAMPLE_QROWS
    q_t = qi_s.reshape(bs, rows, N_IDX_HEADS, IDX_DIM)[:, :qr].transpose(0, 2, 1, 3)
    q_t = q_t.reshape(bs, N_IDX_HEADS * qr, IDX_DIM)
    w_col = proj_s[:, C_MISC + M_WI:C_MISC + M_WI + N_IDX_HEADS].reshape(bs, rows, N_IDX_HEADS)[:, :qr]
    w_col = w_col.transpose(0, 2, 1).reshape(bs, N_IDX_HEADS * qr, 1)
    pages = _pick_tile(page_table.shape[1], SAMPLE_PAGES_PER_STEP, 1)
    keys_past, keys_new = _sample_scores(page_table, q_t, w_col, kif_s, cache_idx, layer, pages, ts)
    n_sel_s = min(TOPK_MAX, (past + ts) // 4)
    oa_s = _sample_attend(page_table, keys_past, keys_new, q_s, kb_s, vb_s, cache_k, cache_v, layer, pages, n_sel_s)
    prev8 = jnp.pad(state_conv, ((0, 0), (SUBLANES - (CONV_WIDTH - 1), 0), (0, 0)))
    qn_s, kn_s, vv_s, bg_s = _delta_prep(proj_s, prev8, conv_w, a_log, dt_bias, bs, rows, rows, ts)
    to_chunk = lambda a: jnp.pad(a.reshape(bs, rows, -1), ((0, 0), (0, DELTA_CHUNK - rows), (0, 0))).reshape(
        bs * DELTA_CHUNK, -1)
    z_s = proj_s[:, C_ZD:C_ZD + DELTA_WIDTH]
    od_s, ssm_s = _delta_chunks(to_chunk(qn_s), to_chunk(kn_s), to_chunk(vv_s), to_chunk(bg_s), to_chunk(z_s), 0,
                                state_ssm, o_norm_w, bs, DELTA_CHUNK)
    od_s = od_s.reshape(bs, DELTA_CHUNK, DELTA_WIDTH)[:, :rows].reshape(ns_, DELTA_WIDTH)

    tm_o = _pick_tile(tp, tm_s)
    assert ns_ % tm_o == 0
    per_tok = lambda m: m.reshape(ns_ // tm_o, tm_o, d)
    n_all = np_ + ns_
    x1_all, h2_all, lg_all = _out_projection(
        (oa_p, od_p, xp2, mp[2], mp[3], mp[4]),
        (oa_s, od_s, xs2, per_tok(ms[2]), per_tok(ms[3]), per_tok(ms[4])),
        norm2_w, w_out_bf, wr_both, b_rt, tm_o, tp)
    eid, gates = _route(lg_all, _pick_tile(n_all, 512, SUBLANES))
    bm = 256
    tok_sorted, block_j0, dest, block_exp, n_active = _route_and_sort(eid[:, 0:2], bm)
    yb = _moe_experts(tok_sorted, block_j0, block_exp, n_active, h2_all, w_gate, w_up, w_down, bm)
    out_p, out_s = _combine(x1_all, dest, yb, gates, mp[5], per_tok(ms[5]), tm_o, np_, tp)

    valid = lambda a: a.reshape(bs, rows, -1)[:, :ts]
    conv_p = fr["tail"][:, SUBLANES - (CONV_WIDTH - 1):]
    conv_s = proj_s.reshape(bs, rows, PROJ_PACKED)[:, ts - (CONV_WIDTH - 1):ts, C_CONV:C_CONV + CONV_CHANNELS]
    return (out_p.reshape(bp, tp, d), valid(out_s),
            kf_p.reshape(bp, tp, N_KV_HEADS, HEAD_DIM),
            fr["vf"].reshape(bp, tp, N_KV_HEADS, HEAD_DIM),
            kif_p.reshape(bp, tp, IDX_DIM), ssm_p, conv_p,
            valid(kf_s).reshape(bs, ts, N_KV_HEADS, HEAD_DIM),
            valid(proj_s[:, C_VA:C_VA + KV_WIDTH]).reshape(bs, ts, N_KV_HEADS, HEAD_DIM),
            valid(kif_s), ssm_s, conv_s)


def kernel(x_prompt, x_sample, cache_k, cache_v, cache_idx_k, state_ssm, state_conv, page_table, c_prompt, c_sample,
           w_in, w_out, conv_w, a_log, dt_bias, q_norm_w, k_norm_w, idx_k_norm_w, o_norm_w, norm1_w, norm2_w,
           w_ada, b_ada, w_group, b_group, w_router, b_router, w_gate, w_up, w_down):
    depth = w_in.shape[0]
    yp, ys = x_prompt, x_sample
    per_layer = []
    for l in range(depth):
        res = _layer(l, yp, ys, cache_k, cache_v, cache_idx_k, state_ssm[l], state_conv[l], page_table,
                     c_prompt, c_sample, w_in[l], w_out[l], conv_w[l], a_log[l], dt_bias[l], q_norm_w[l],
                     k_norm_w[l], idx_k_norm_w[l], o_norm_w[l], norm1_w[l], norm2_w[l], w_ada[l], b_ada[l],
                     w_group[l], b_group[l], w_router[l], b_router[l], w_gate[l], w_up[l], w_down[l])
        yp, ys = res[0], res[1]
        per_layer.append(res[2:])
    stacked = tuple(jnp.stack([pl_[j] for pl_ in per_layer]) for j in range(10))
    return (yp, ys) + stacked
```

```python
import functools

import jax
import jax.numpy as jnp
import numpy as np
from jax import lax
from jax.experimental import pallas as pl
from jax.experimental.pallas import tpu as pltpu

F32 = jnp.float32
BF16 = jnp.bfloat16
I32 = jnp.int32

HEAD_DIM = 128
N_ATTN_HEADS = 8
N_KV_HEADS = 2
KV_GROUP = N_ATTN_HEADS // N_KV_HEADS
N_DELTA_HEADS = 8
N_IDX_HEADS = 16
IDX_DIM = 64
ATTN_WIDTH = N_ATTN_HEADS * HEAD_DIM
KV_WIDTH = N_KV_HEADS * HEAD_DIM
DELTA_WIDTH = N_DELTA_HEADS * HEAD_DIM
IDX_WIDTH = N_IDX_HEADS * IDX_DIM
CONV_CHANNELS = 3 * DELTA_WIDTH
TOPK_MAX = 256
ROPE_THETA = 500000.0
ROPE_FRACTION = 4
CONV_WIDTH = 4
DELTA_CHUNK = 64
N_GROUPS = 8
EXPERTS_PER_GROUP = 8
N_EXPERTS = N_GROUPS * EXPERTS_PER_GROUP
N_MOD = 6
EPS = 1e-6
PAGE_SIZE = 128
PROJ_SIZES = (ATTN_WIDTH, KV_WIDTH, KV_WIDTH, IDX_WIDTH, IDX_DIM, N_IDX_HEADS,
              DELTA_WIDTH, DELTA_WIDTH, DELTA_WIDTH, DELTA_WIDTH, N_DELTA_HEADS, N_DELTA_HEADS)

LANES = 128
SUBLANES = 8
VMEM_LIMIT = 56 * 1024 * 1024
FRONT_VMEM_LIMIT = 60 * 1024 * 1024

C_QA = 0
C_QI = 1024
C_ZD = 2048
C_CONV = 3072
C_KA = 6144
C_VA = 6400
C_MISC = 6656
PROJ_PACKED = 6912
M_KI = 0
M_WI = 64
M_BD = 80
M_AD = 88

Q_TILE = 128
KEY_CHUNK = 256
SEL_SPAN = 512
NEG_INF_KEY = -2139095041
SAMPLE_ROWS = 16
SAMPLE_QROWS = SUBLANES
SAMPLE_PAGES_PER_STEP = 32
NEG_BIG = -1e30
INT_MIN = -2147483648
INT_MAX = 2147483647


def _cparams(sem):
    return pltpu.CompilerParams(dimension_semantics=sem, vmem_limit_bytes=VMEM_LIMIT)


def _dot(a, b):
    return jnp.dot(a, b, preferred_element_type=F32)


def _dot_nt(a, b):
    return lax.dot_general(a, b, (((1,), (1,)), ((), ())), preferred_element_type=F32)


def _sigmoid(x):
    return 0.5 * jnp.tanh(0.5 * x) + 0.5


def _silu(x):
    return x * _sigmoid(x)


def _softplus(x):
    return jnp.maximum(x, 0.0) + jnp.log(1.0 + jnp.exp(-jnp.abs(x)))


def _ada_kernel(c_ref, w_ref, b_ref, o_ref):
    s = _silu(c_ref[...]).astype(BF16)
    o_ref[...] = _dot(s, w_ref[...].astype(BF16)) + b_ref[...]


def _ada_modulation(c, w_ada, b_ada):
    r, d = c.shape
    n = w_ada.shape[1]
    tn = 1024 if n % 1024 == 0 else n
    return pl.pallas_call(
        _ada_kernel,
        name="ada_mod",
        grid=(n // tn,),
        in_specs=[pl.BlockSpec((r, d), lambda j: (0, 0)),
                  pl.BlockSpec((d, tn), lambda j: (0, j)),
                  pl.BlockSpec((1, tn), lambda j: (0, j))],
        out_specs=pl.BlockSpec((r, tn), lambda j: (0, j)),
        out_shape=jax.ShapeDtypeStruct((r, n), F32),
        compiler_params=_cparams(("parallel",)),
    )(c, w_ada, b_ada.reshape(1, n))


INPROJ_COLS = 1152


def _inproj_kernel(x_ref, sh_ref, sc_ref, nw_ref, w_ref, o_ref):
    x = x_ref[...]
    y = x * lax.rsqrt(jnp.mean(x * x, axis=-1, keepdims=True) + EPS) * nw_ref[...]
    h = (y * (1.0 + sc_ref[0]) + sh_ref[0]).astype(BF16)
    for c0 in range(0, o_ref.shape[1], INPROJ_COLS):
        o_ref[:, c0:c0 + INPROJ_COLS] = _dot(h, w_ref[:, c0:c0 + INPROJ_COLS])


def _in_projection(x2d, shift, scale, norm_w, w_packed, tm, rows_per_mod_block):
    n, d = x2d.shape
    np_ = w_packed.shape[1]
    r = shift.shape[1]
    tiles_per_mod = rows_per_mod_block // tm
    mod_spec = pl.BlockSpec((1, r, d), lambda i: (i // tiles_per_mod, 0, 0))
    return pl.pallas_call(
        _inproj_kernel,
        name="in_proj",
        grid=(n // tm,),
        in_specs=[pl.BlockSpec((tm, d), lambda i: (i, 0)),
                  mod_spec, mod_spec,
                  pl.BlockSpec((1, d), lambda i: (0, 0)),
                  pl.BlockSpec((d, np_), lambda i: (0, 0), pipeline_mode=pl.Buffered(1))],
        out_specs=pl.BlockSpec((tm, np_), lambda i: (i, 0)),
        out_shape=jax.ShapeDtypeStruct((n, np_), F32),
        compiler_params=_cparams(("parallel",)),
    )(x2d, shift, scale, norm_w.reshape(1, d), w_packed)


def _rope(x, tab, rot):
    c = tab[:, 0:LANES]
    s1 = tab[:, LANES:2 * LANES]
    s2 = tab[:, 2 * LANES:3 * LANES]
    return x * c + pltpu.roll(x, LANES - rot, 1) * s1 + pltpu.roll(x, rot, 1) * s2


def _rms_head(x, w):
    return x * lax.rsqrt(jnp.mean(x * x, axis=-1, keepdims=True) + EPS) * w


def _prep_kernel(qa_ref, qi_ref, ka_ref, va_ref, misc_ref, tabm_ref, tabi_ref, qw_ref, kw_ref, iw_ref,
                 q_ref, kf_ref, kb_ref, vb_ref, qib_ref, kif_ref, kib_ref, *, transpose_v):
    tabm = tabm_ref[0]
    tabi = tabi_ref[0]
    half_main = HEAD_DIM // ROPE_FRACTION // 2
    half_idx = IDX_DIM // ROPE_FRACTION // 2
    for h in range(N_ATTN_HEADS):
        sl = slice(h * HEAD_DIM, (h + 1) * HEAD_DIM)
        y = _rope(_rms_head(qa_ref[:, sl], qw_ref[...]), tabm, half_main)
        q_ref[:, sl] = (y * (HEAD_DIM ** -0.5)).astype(BF16)
    for h in range(N_KV_HEADS):
        sl = slice(h * HEAD_DIM, (h + 1) * HEAD_DIM)
        y = _rope(_rms_head(ka_ref[:, sl], kw_ref[...]), tabm, half_main)
        kf_ref[:, sl] = y
        kb_ref[:, sl] = y.astype(BF16)
    if transpose_v:
        vb_ref[...] = va_ref[...].T.astype(BF16)
    else:
        vb_ref[...] = va_ref[...].astype(BF16)
    for p in range(IDX_WIDTH // LANES):
        sl = slice(p * LANES, (p + 1) * LANES)
        qib_ref[:, sl] = _rope(qi_ref[:, sl], tabi, half_idx).astype(BF16)
    m = misc_ref[...]
    lane = lax.broadcasted_iota(I32, m.shape, 1)
    ki = jnp.where(lane < IDX_DIM, m, 0.0)
    ms = jnp.sum(ki * ki, axis=-1, keepdims=True) * (1.0 / IDX_DIM)
    y = _rope(ki * lax.rsqrt(ms + EPS) * iw_ref[...], tabi, half_idx)
    kif_ref[...] = y[:, 0:IDX_DIM]
    kib_ref[...] = (y + pltpu.roll(y, IDX_DIM, 1)).astype(BF16)


def _rope_tables(pos, head_dim, group):
    d_rot = head_dim // ROPE_FRACTION
    half = d_rot // 2
    inv_freq = jnp.power(ROPE_THETA, -(jnp.arange(half, dtype=F32) * 2.0 / d_rot))
    ang = pos.astype(F32)[:, None] * inv_freq[None, :]
    cos = jnp.cos(ang)
    sin = jnp.sin(ang)
    t = pos.shape[0]
    z = jnp.zeros((t, group - d_rot), F32)
    c = jnp.concatenate([cos, cos, jnp.ones((t, group - d_rot), F32)], axis=1)
    s1 = jnp.concatenate([-sin, jnp.zeros((t, half), F32), z], axis=1)
    s2 = jnp.concatenate([jnp.zeros((t, half), F32), sin, z], axis=1)
    rep = LANES // group
    return jnp.concatenate([jnp.tile(c, (1, rep)), jnp.tile(s1, (1, rep)), jnp.tile(s2, (1, rep))], axis=1)


def _attention_prep(proj, pos, tq, q_norm_w, k_norm_w, idx_k_norm_w, transpose_v):
    n = proj.shape[0]
    p = pos.shape[0]
    g = p // tq
    tabm = _rope_tables(pos, HEAD_DIM, LANES).reshape(g, tq, 3 * LANES)
    tabi = _rope_tables(pos, IDX_DIM, IDX_DIM).reshape(g, tq, 3 * LANES)
    iw = jnp.concatenate([idx_k_norm_w, jnp.zeros((LANES - IDX_DIM,), F32)]).reshape(1, LANES)
    row = lambda w, c: pl.BlockSpec((tq, w), lambda i: (i, c // w))
    tab_spec = pl.BlockSpec((1, tq, 3 * LANES), lambda i: (i % g, 0, 0))
    vec_spec = pl.BlockSpec((1, LANES), lambda i: (0, 0))
    out_row = lambda w: pl.BlockSpec((tq, w), lambda i: (i, 0))
    v_spec = pl.BlockSpec((KV_WIDTH, tq), lambda i: (0, i)) if transpose_v else out_row(KV_WIDTH)
    v_shape = (KV_WIDTH, n) if transpose_v else (n, KV_WIDTH)
    return pl.pallas_call(
        functools.partial(_prep_kernel, transpose_v=transpose_v),
        name="attn_prep",
        grid=(n // tq,),
        in_specs=[row(ATTN_WIDTH, C_QA), row(IDX_WIDTH, C_QI), row(KV_WIDTH, C_KA), row(KV_WIDTH, C_VA),
                  row(LANES, C_MISC), tab_spec, tab_spec, vec_spec, vec_spec, vec_spec],
        out_specs=[out_row(ATTN_WIDTH), out_row(KV_WIDTH), out_row(KV_WIDTH), v_spec,
                   out_row(IDX_WIDTH), out_row(IDX_DIM), out_row(LANES)],
        out_shape=[jax.ShapeDtypeStruct((n, ATTN_WIDTH), BF16),
                   jax.ShapeDtypeStruct((n, KV_WIDTH), F32),
                   jax.ShapeDtypeStruct((n, KV_WIDTH), BF16),
                   jax.ShapeDtypeStruct(v_shape, BF16),
                   jax.ShapeDtypeStruct((n, IDX_WIDTH), BF16),
                   jax.ShapeDtypeStruct((n, IDX_DIM), F32),
                   jax.ShapeDtypeStruct((n, LANES), BF16)],
        compiler_params=_cparams(("parallel",)),
    )(proj, proj, proj, proj, proj, tabm, tabi,
      q_norm_w.reshape(1, LANES), k_norm_w.reshape(1, LANES), iw)


def _sort_key(x):
    b = pltpu.bitcast(x + 0.0, I32)
    return b ^ ((b >> 31) & INT_MAX)


def _kth_largest_key(count_ge, k, shape, n_total):
    def body(it, carry):
        ans_u, n_ge = carry
        bit = jnp.left_shift(jnp.int32(1), 31 - it)
        cand_u = ans_u | bit
        cnt = count_ge(cand_u ^ INT_MIN)
        ok = cnt >= k
        return jnp.where(ok, cand_u, ans_u), jnp.where(ok, cnt, n_ge)

    ans_u, n_ge = lax.fori_loop(0, 32, body, (jnp.zeros(shape, I32), jnp.full(shape, float(n_total), F32)))
    return ans_u ^ INT_MIN, n_ge


def _tie_index_limit(count_eq_le, need, n_keys, shape):
    nbits = max(1, int(n_keys - 1).bit_length())

    def body(it, lo):
        bit = jnp.left_shift(jnp.int32(1), nbits - 1 - it)
        cand = lo | bit
        cnt = count_eq_le(cand - 1)
        return jnp.where(cnt >= need, lo, cand)

    return lax.fori_loop(0, nbits, body, jnp.zeros(shape, I32))


def _dsa_prompt_kernel(q_ref, qi_ref, misc_ref, k_ref, vt_ref, kx_ref, o_ref,
                       key_scr, qsel_scr, qg_scr, thr_scr, lim_scr, m_scr, l_scr, acc_scr, *, n_sel):
    i = pl.program_id(1)
    tq = Q_TILE
    ck = KEY_CHUNK
    n_ch = (i * tq + tq + ck - 1) // ck
    q_pos = i * tq + lax.broadcasted_iota(I32, (1, tq), 1)
    row_k = lax.broadcasted_iota(I32, (ck, 1), 0)

    lo_half = lax.broadcasted_iota(I32, (tq, LANES), 1) < IDX_DIM
    zero = jnp.zeros((), BF16)
    for p in range(IDX_WIDTH // LANES):
        slab = qi_ref[:, p * LANES:(p + 1) * LANES]
        qsel_scr[(2 * p) * tq:(2 * p + 1) * tq, :] = jnp.where(lo_half, slab, zero)
        qsel_scr[(2 * p + 1) * tq:(2 * p + 2) * tq, :] = jnp.where(lo_half, zero, slab)
    w_t = misc_ref[...].T

    def score_chunk(c, carry):
        off = pl.multiple_of(c * ck, ck)
        s = _dot_nt(kx_ref[pl.ds(off, ck), :], qsel_scr[...])
        acc = jnp.zeros((ck, tq), F32)
        for h in range(N_IDX_HEADS):
            acc = acc + w_t[M_WI + h:M_WI + h + 1, :] * jnp.maximum(s[:, h * tq:(h + 1) * tq], 0.0)
        acc = jnp.where(off + row_k <= q_pos, acc, -jnp.inf)
        key_scr[pl.ds(off, ck), :] = _sort_key(acc)
        return carry

    lax.fori_loop(0, n_ch, score_chunk, 0)

    spc = SEL_SPAN // ck
    n_span = (n_ch + spc - 1) // spc
    neg_key = jnp.full((ck, tq), NEG_INF_KEY, I32)

    def pad_chunk(c, carry):
        key_scr[pl.ds(pl.multiple_of(c * ck, ck), ck), :] = neg_key
        return carry

    lax.fori_loop(n_ch, n_span * spc, pad_chunk, 0)

    thr_scr[...] = jnp.full((1, tq), INT_MIN, I32)
    lim_scr[...] = jnp.full((1, tq), INT_MAX, I32)

    def select_threshold(n_keys):
        def count_where(pred):
            tot = jnp.zeros((SUBLANES, tq), F32)
            for c0 in range(0, n_keys, ck):
                hit = pred(key_scr[c0:c0 + ck, :], c0 + row_k).astype(F32)
                tot = tot + jnp.sum(hit.reshape(ck // SUBLANES, SUBLANES, tq), axis=0)
            return jnp.sum(tot, axis=0, keepdims=True)

        t, n_ge = _kth_largest_key(lambda cand: count_where(lambda kk, pos: kk >= cand), float(n_sel), (1, tq),
                                   n_keys)
        thr_scr[...] = t

        @pl.when(jnp.max(n_ge) > float(n_sel))
        def _():
            n_gt = count_where(lambda kk, pos: kk > t)
            lim_scr[...] = _tie_index_limit(
                lambda idx: count_where(lambda kk, pos: jnp.logical_and(kk == t, pos <= idx)),
                float(n_sel) - n_gt, k_ref.shape[0], (1, tq))

    for spans in range(1, k_ref.shape[0] // SEL_SPAN + 1):
        if spans * SEL_SPAN > n_sel:
            pl.when(jnp.logical_and(n_span == spans, (i + 1) * tq > n_sel))(
                functools.partial(select_threshold, spans * SEL_SPAN))

    thr = thr_scr[...]
    lim = lim_scr[...]

    for g in range(N_KV_HEADS):
        for r in range(KV_GROUP):
            h = g * KV_GROUP + r
            qg_scr[g, r * tq:(r + 1) * tq, :] = q_ref[:, h * HEAD_DIM:(h + 1) * HEAD_DIM]
    m_scr[...] = jnp.full(m_scr.shape, NEG_BIG, F32)
    l_scr[...] = jnp.zeros(l_scr.shape, F32)
    acc_scr[...] = jnp.zeros(acc_scr.shape, F32)

    def attend_chunk(c, carry):
        off = pl.multiple_of(c * ck, ck)
        kk = key_scr[pl.ds(off, ck), :]
        pos = off + row_k
        sel = jnp.logical_or(kk > thr, jnp.logical_and(kk == thr, pos <= lim))
        sel = jnp.logical_and(sel, pos <= q_pos)
        def group_steps(g):
            kc = k_ref[pl.ds(off, ck), g * HEAD_DIM:(g + 1) * HEAD_DIM]
            vt = vt_ref[g * HEAD_DIM:(g + 1) * HEAD_DIM, pl.ds(off, ck)]
            qk = _dot_nt(kc, qg_scr[g])
            yield
            s = jnp.concatenate([jnp.where(sel, qk[:, r * tq:(r + 1) * tq], NEG_BIG) for r in range(KV_GROUP)],
                                axis=1)
            m_old = m_scr[g]
            m_new = jnp.maximum(m_old, jnp.max(s, axis=0, keepdims=True))
            yield
            p = jnp.exp(s - m_new)
            alpha = jnp.exp(m_old - m_new)
            l_scr[g] = alpha * l_scr[g] + jnp.sum(p, axis=0, keepdims=True)
            yield
            acc_scr[g] = alpha * acc_scr[g] + _dot(vt, p.astype(BF16))
            m_scr[g] = m_new

        chains = [group_steps(g) for g in range(N_KV_HEADS)]
        while chains:
            chains = [ch for ch in chains if next(ch, "done") != "done"]
        return carry

    lax.fori_loop(0, n_ch, attend_chunk, 0)
    for g in range(N_KV_HEADS):
        o_t = acc_scr[g] / l_scr[g]
        for r in range(KV_GROUP):
            h = g * KV_GROUP + r
            o_ref[:, h * HEAD_DIM:(h + 1) * HEAD_DIM] = o_t[:, r * tq:(r + 1) * tq].T.astype(BF16)


def _dsa_prompt(q_bf, qi_bf, misc, k_bf, vt_bf, kx_bf, b, t):
    n = b * t
    nq = t // Q_TILE
    n_sel = min(TOPK_MAX, t // 4)
    qrow = lambda w: pl.BlockSpec((Q_TILE, w), lambda bb, i: (bb * nq + i, 0))
    seq = lambda w: pl.BlockSpec((t, w), lambda bb, i: (bb, 0))
    return pl.pallas_call(
        functools.partial(_dsa_prompt_kernel, n_sel=n_sel),
        name="dsa_prompt",
        grid=(b, nq),
        in_specs=[qrow(ATTN_WIDTH), qrow(IDX_WIDTH),
                  qrow(LANES),
                  seq(KV_WIDTH), pl.BlockSpec((KV_WIDTH, t), lambda bb, i: (0, bb)), seq(LANES)],
        out_specs=qrow(ATTN_WIDTH),
        out_shape=jax.ShapeDtypeStruct((n, ATTN_WIDTH), BF16),
        scratch_shapes=[pltpu.VMEM((t, Q_TILE), I32),
                        pltpu.VMEM((N_IDX_HEADS * Q_TILE, LANES), BF16),
                        pltpu.VMEM((N_KV_HEADS, KV_GROUP * Q_TILE, HEAD_DIM), BF16),
                        pltpu.VMEM((1, Q_TILE), I32),
                        pltpu.VMEM((1, Q_TILE), I32),
                        pltpu.VMEM((N_KV_HEADS, 1, KV_GROUP * Q_TILE), F32),
                        pltpu.VMEM((N_KV_HEADS, 1, KV_GROUP * Q_TILE), F32),
                        pltpu.VMEM((N_KV_HEADS, HEAD_DIM, KV_GROUP * Q_TILE), F32)],
        compiler_params=_cparams(("parallel", "arbitrary")),
    )(q_bf, qi_bf, misc, k_bf, vt_bf, kx_bf)


def _sample_score_kernel(pt_ref, q_ref, w_ref, kn_ref, *refs, pages, t_valid):
    page_refs = refs[:pages]
    past_ref, new_ref = refs[pages], refs[pages + 1]
    rows = SAMPLE_ROWS
    qr = SAMPLE_QROWS
    q = q_ref[0]
    w = w_ref[0]

    def head_sum(s):
        s = w * jnp.maximum(s, 0.0)
        acc = s[0:qr]
        for h in range(1, N_IDX_HEADS):
            acc = acc + s[h * qr:(h + 1) * qr]
        return acc

    for j in range(pages):
        kp = page_refs[j][0, 0].astype(BF16)
        past_ref[0, :, j * PAGE_SIZE:(j + 1) * PAGE_SIZE] = _sort_key(head_sum(_dot_nt(q, kp)))

    @pl.when(pl.program_id(1) == 0)
    def _():
        kn = jnp.concatenate([kn_ref[...], jnp.zeros((LANES - rows, IDX_DIM), F32)], axis=0).astype(BF16)
        sc = head_sum(_dot_nt(q, kn))
        t = lax.broadcasted_iota(I32, sc.shape, 0)
        s = lax.broadcasted_iota(I32, sc.shape, 1)
        ok = jnp.logical_and(s <= t, s < t_valid)
        new_ref[0] = _sort_key(jnp.where(ok, sc, -jnp.inf))


def _sample_scores(page_table, q_t, w_col, kif, cache_idx, layer, pages, t_valid):
    bs, n_pages = page_table.shape
    past = n_pages * PAGE_SIZE
    hr = N_IDX_HEADS * SAMPLE_QROWS
    page_spec = lambda j: pl.BlockSpec((1, 1, PAGE_SIZE, IDX_DIM),
                                       lambda b, c, pt: (layer, pt[b, c * pages + j], 0, 0))
    grid_spec = pltpu.PrefetchScalarGridSpec(
        num_scalar_prefetch=1,
        grid=(bs, n_pages // pages),
        in_specs=[pl.BlockSpec((1, hr, IDX_DIM), lambda b, c, pt: (b, 0, 0)),
                  pl.BlockSpec((1, hr, 1), lambda b, c, pt: (b, 0, 0)),
                  pl.BlockSpec((SAMPLE_ROWS, IDX_DIM), lambda b, c, pt: (b, 0))]
                 + [page_spec(j) for j in range(pages)],
        out_specs=[pl.BlockSpec((1, SAMPLE_QROWS, pages * PAGE_SIZE), lambda b, c, pt: (b, 0, c)),
                   pl.BlockSpec((1, SAMPLE_QROWS, LANES), lambda b, c, pt: (b, 0, 0))],
    )
    return pl.pallas_call(
        functools.partial(_sample_score_kernel, pages=pages, t_valid=t_valid),
        name="sample_scores",
        grid_spec=grid_spec,
        out_shape=[jax.ShapeDtypeStruct((bs, SAMPLE_QROWS, past), I32),
                   jax.ShapeDtypeStruct((bs, SAMPLE_QROWS, LANES), I32)],
        compiler_params=_cparams(("parallel", "arbitrary")),
    )(page_table, q_t, w_col, kif, *([cache_idx] * pages))


def _sample_attend_kernel(pt_ref, kp_ref, kn_ref, q_ref, knew_ref, vnew_ref, *refs, pages, n_sel, past):
    k_pages = refs[:pages]
    v_pages = refs[pages:2 * pages]
    o_ref = refs[2 * pages]
    thr_scr, lim_scr, m_scr, l_scr, acc_scr = refs[2 * pages + 1:]
    c = pl.program_id(1)
    rows = SAMPLE_ROWS
    span = pages * PAGE_SIZE

    @pl.when(c == 0)
    def _():
        m_scr[...] = jnp.full(m_scr.shape, NEG_BIG, F32)
        l_scr[...] = jnp.zeros(l_scr.shape, F32)
        acc_scr[...] = jnp.zeros(acc_scr.shape, F32)
        vr = SUBLANES
        tile_pos = lax.broadcasted_iota(I32, (vr, LANES), 1)

        def count_where(pred):
            acc = pred(kn_ref[0, 0:vr, :], past + tile_pos).astype(F32)
            for tix in range(past // LANES):
                acc = acc + pred(kp_ref[0, 0:vr, tix * LANES:(tix + 1) * LANES],
                                 tix * LANES + tile_pos).astype(F32)
            return jnp.sum(acc, axis=1, keepdims=True)

        t, n_ge = _kth_largest_key(lambda cand: count_where(lambda kk, pos: kk >= cand), float(n_sel), (vr, 1),
                                   past + LANES)
        thr_scr[...] = jnp.full((rows, 1), INT_MIN, I32)
        lim_scr[...] = jnp.full((rows, 1), INT_MAX, I32)
        thr_scr[0:vr, :] = t

        @pl.when(jnp.max(n_ge) > float(n_sel))
        def _():
            n_gt = count_where(lambda kk, pos: kk > t)
            lim_scr[0:vr, :] = _tie_index_limit(
                lambda idx: count_where(lambda kk, pos: jnp.logical_and(kk == t, pos <= idx)),
                float(n_sel) - n_gt, past + LANES, (vr, 1))

    thr = thr_scr[0:SAMPLE_QROWS, :]
    lim = lim_scr[0:SAMPLE_QROWS, :]
    all_rows = lambda m: jnp.concatenate([m] * (rows // SAMPLE_QROWS), axis=0)

    def update_steps(g, qg, k_fn, v_fn, sel):
        sel = jnp.concatenate([sel] * KV_GROUP, axis=0)
        qk = _dot_nt(qg, k_fn())
        yield
        s = jnp.where(sel, qk, NEG_BIG)
        m_old = m_scr[g]
        m_new = jnp.maximum(m_old, jnp.max(s, axis=1, keepdims=True))
        yield
        p = jnp.where(sel, jnp.exp(s - m_new), 0.0)
        alpha = jnp.exp(m_old - m_new)
        l_scr[g] = alpha * l_scr[g] + jnp.sum(p, axis=1, keepdims=True)
        yield
        acc_scr[g] = alpha * acc_scr[g] + _dot(p.astype(BF16), v_fn())
        m_scr[g] = m_new

    def run_lockstep(chains):
        while chains:
            chains = [ch for ch in chains if next(ch, "done") != "done"]

    def select(kk, pos):
        return jnp.logical_or(kk > thr, jnp.logical_and(kk == thr, pos <= lim))

    def page_cat(page_refs, g):
        return jnp.concatenate([r[0, 0, pl.ds(g, PAGE_SIZE, stride=N_KV_HEADS), :] for r in page_refs],
                               axis=0).astype(BF16)

    off = pl.multiple_of(c * span, span)
    kk = kp_ref[0, :, pl.ds(off, span)]
    sel_past = all_rows(select(kk, off + lax.broadcasted_iota(I32, kk.shape, 1)))
    q_groups = []
    for g in range(N_KV_HEADS):
        qg = jnp.concatenate(
            [q_ref[:, (g * KV_GROUP + r) * HEAD_DIM:(g * KV_GROUP + r + 1) * HEAD_DIM] for r in range(KV_GROUP)],
            axis=0)
        q_groups.append(qg)
    run_lockstep([update_steps(g, q_groups[g], functools.partial(page_cat, k_pages, g),
                               functools.partial(page_cat, v_pages, g), sel_past) for g in range(N_KV_HEADS)])

    @pl.when(c == pl.num_programs(1) - 1)
    def _():
        kn = kn_ref[0]
        lane = lax.broadcasted_iota(I32, kn.shape, 1)
        sel_new = all_rows(jnp.logical_and(select(kn, past + lane), lane < rows))
        pad = jnp.zeros((LANES - rows, KV_WIDTH), BF16)
        k_new = jnp.concatenate([knew_ref[...], pad], axis=0)
        v_new = jnp.concatenate([vnew_ref[...], pad], axis=0)
        head = lambda a, g: (lambda: a[:, g * HEAD_DIM:(g + 1) * HEAD_DIM])
        run_lockstep([update_steps(g, q_groups[g], head(k_new, g), head(v_new, g), sel_new)
                      for g in range(N_KV_HEADS)])
        for g in range(N_KV_HEADS):
            o = acc_scr[g] / l_scr[g]
            for r in range(KV_GROUP):
                h = g * KV_GROUP + r
                o_ref[:, h * HEAD_DIM:(h + 1) * HEAD_DIM] = o[r * rows:(r + 1) * rows].astype(BF16)


def _sample_attend(page_table, keys_past, keys_new, q_bf, k_bf, v_bf, cache_k, cache_v, layer, pages, n_sel):
    bs, n_pages = page_table.shape
    past = n_pages * PAGE_SIZE
    depth, pool = cache_k.shape[0], cache_k.shape[1]
    cache_k = cache_k.reshape(depth, pool, PAGE_SIZE * N_KV_HEADS, HEAD_DIM)
    cache_v = cache_v.reshape(depth, pool, PAGE_SIZE * N_KV_HEADS, HEAD_DIM)
    page_spec = lambda j: pl.BlockSpec((1, 1, PAGE_SIZE * N_KV_HEADS, HEAD_DIM),
                                       lambda b, c, pt: (layer, pt[b, c * pages + j], 0, 0))
    row = lambda w: pl.BlockSpec((SAMPLE_ROWS, w), lambda b, c, pt: (b, 0))
    grid_spec = pltpu.PrefetchScalarGridSpec(
        num_scalar_prefetch=1,
        grid=(bs, n_pages // pages),
        in_specs=[pl.BlockSpec((1, SAMPLE_QROWS, past), lambda b, c, pt: (b, 0, 0)),
                  pl.BlockSpec((1, SAMPLE_QROWS, LANES), lambda b, c, pt: (b, 0, 0)),
                  row(ATTN_WIDTH), row(KV_WIDTH), row(KV_WIDTH)]
                 + [page_spec(j) for j in range(pages)] * 2,
        out_specs=row(ATTN_WIDTH),
        scratch_shapes=[pltpu.VMEM((SAMPLE_ROWS, 1), I32),
                        pltpu.VMEM((SAMPLE_ROWS, 1), I32),
                        pltpu.VMEM((N_KV_HEADS, KV_GROUP * SAMPLE_ROWS, 1), F32),
                        pltpu.VMEM((N_KV_HEADS, KV_GROUP * SAMPLE_ROWS, 1), F32),
                        pltpu.VMEM((N_KV_HEADS, KV_GROUP * SAMPLE_ROWS, HEAD_DIM), F32)],
    )
    return pl.pallas_call(
        functools.partial(_sample_attend_kernel, pages=pages, n_sel=n_sel, past=past),
        name="sample_attend",
        grid_spec=grid_spec,
        out_shape=jax.ShapeDtypeStruct((bs * SAMPLE_ROWS, ATTN_WIDTH), BF16),
        compiler_params=_cparams(("parallel", "arbitrary")),
    )(page_table, keys_past, keys_new, q_bf, k_bf, v_bf, *([cache_k] * pages), *([cache_v] * pages))


def _delta_prep_kernel(x_ref, halo_ref, prev_ref, misc_ref, cw_ref, al_ref, dt_ref,
                       qn_ref, kn_ref, vv_ref, bg_ref, xp_scr, *, tiles_per_seq, t_valid, tt):
    i = pl.program_id(0)
    tile_in_seq = i % tiles_per_seq
    halo = jnp.where(tile_in_seq == 0, prev_ref[0], halo_ref[...])
    xp_scr[0:SUBLANES, :] = halo
    xp_scr[SUBLANES:SUBLANES + tt, :] = x_ref[...]
    base = SUBLANES - (CONV_WIDTH - 1)
    outs = (qn_ref, kn_ref, vv_ref)
    for sec in range(3):
        for h in range(N_DELTA_HEADS):
            col = sec * DELTA_WIDTH + h * HEAD_DIM
            sl = slice(col, col + HEAD_DIM)
            y = xp_scr[base:base + tt, sl] * cw_ref[0:1, sl]
            for j in range(1, CONV_WIDTH):
                y = y + xp_scr[base + j:base + j + tt, sl] * cw_ref[j:j + 1, sl]
            y = _silu(y)
            if sec < 2:
                y = y * lax.rsqrt(jnp.sum(y * y, axis=-1, keepdims=True) + EPS)
            if sec == 0:
                y = y * (HEAD_DIM ** -0.5)
            outs[sec][:, h * HEAD_DIM:(h + 1) * HEAD_DIM] = y
    m = misc_ref[...]
    lane = lax.broadcasted_iota(I32, m.shape, 1)
    row = tile_in_seq * tt + lax.broadcasted_iota(I32, m.shape, 0)
    beta = _sigmoid(m)
    g = -jnp.exp(al_ref[...]) * _softplus(m + dt_ref[...])
    is_b = jnp.logical_and(lane >= M_BD, lane < M_BD + N_DELTA_HEADS)
    is_g = jnp.logical_and(lane >= M_AD, lane < M_AD + N_DELTA_HEADS)
    comb = jnp.where(is_b, beta, jnp.where(is_g, g, 0.0))
    comb = jnp.where(row < t_valid, comb, 0.0)
    bg_ref[...] = pltpu.roll(comb, LANES - M_BD, 1)


def _delta_prep(proj, prev8, conv_w, a_log, dt_bias, b, t, tt, t_valid):
    n = proj.shape[0]
    tiles_per_seq = t // tt
    pad_vec = lambda v: jnp.zeros((1, LANES), F32).at[0, M_AD:M_AD + N_DELTA_HEADS].set(v)
    halo_blocks = tt // SUBLANES
    return pl.pallas_call(
        functools.partial(_delta_prep_kernel, tiles_per_seq=tiles_per_seq, t_valid=t_valid, tt=tt),
        name="delta_prep",
        grid=(n // tt,),
        in_specs=[pl.BlockSpec((tt, CONV_CHANNELS), lambda i: (i, C_CONV // CONV_CHANNELS)),
                  pl.BlockSpec((SUBLANES, CONV_CHANNELS),
                               lambda i: (jnp.maximum(i * halo_blocks - 1, 0), C_CONV // CONV_CHANNELS)),
                  pl.BlockSpec((1, SUBLANES, CONV_CHANNELS), lambda i: (i // tiles_per_seq, 0, 0)),
                  pl.BlockSpec((tt, LANES), lambda i: (i, C_MISC // LANES)),
                  pl.BlockSpec((CONV_WIDTH, CONV_CHANNELS), lambda i: (0, 0)),
                  pl.BlockSpec((1, LANES), lambda i: (0, 0)),
                  pl.BlockSpec((1, LANES), lambda i: (0, 0))],
        out_specs=[pl.BlockSpec((tt, DELTA_WIDTH), lambda i: (i, 0))] * 3
                  + [pl.BlockSpec((tt, LANES), lambda i: (i, 0))],
        out_shape=[jax.ShapeDtypeStruct((n, DELTA_WIDTH), F32)] * 3 + [jax.ShapeDtypeStruct((n, LANES), F32)],
        scratch_shapes=[pltpu.VMEM((SUBLANES + tt, CONV_CHANNELS), F32)],
        compiler_params=_cparams(("parallel",)),
    )(proj, proj, prev8, proj, conv_w, pad_vec(a_log), pad_vec(dt_bias))


def _mm(a, b):
    return _dot(a.astype(BF16), b.astype(BF16))


def _mm_nt(a, b):
    return _dot_nt(a.astype(BF16), b.astype(BF16))


DELTA_INV_BLOCK = 16
DELTA_STACK = 4
DELTA_CHUNKS_PER_STEP = 2


def _delta_chunk_kernel(qn_ref, kn_ref, vv_ref, bg_ref, z_ref, s0_ref, ow_ref, od_ref, so_ref, s_scr, *, n_chunks):
    c = pl.program_id(1)
    cs = DELTA_CHUNK

    @pl.when(c == 0)
    def _():
        s_scr[...] = s0_ref[0]

    ltri = (lax.broadcasted_iota(I32, (cs, cs), 0) >= lax.broadcasted_iota(I32, (cs, cs), 1)).astype(BF16)

    def chunk_gates(ch):
        bg = bg_ref[ch * cs:(ch + 1) * cs, :]
        g1 = bg.astype(BF16)
        r1 = bg - g1.astype(F32)
        g2 = r1.astype(BF16)
        g3 = (r1 - g2.astype(F32)).astype(BF16)
        gc = _dot(ltri, g1) + _dot(ltri, g2) + _dot(ltri, g3)
        return bg, gc, gc.T

    gates = [chunk_gates(ch) for ch in range(n_chunks)]
    state_ready = {}

    gh = DELTA_STACK
    rows = gh * cs
    rr = lax.broadcasted_iota(I32, (rows, rows), 0)
    cc = lax.broadcasted_iota(I32, (rows, rows), 1)
    same = (rr // cs) == (cc // cs)
    causal = jnp.logical_and(same, rr >= cc)
    strict = jnp.logical_and(same, rr > cc)
    eye = (rr == cc).astype(F32)
    row_head = lax.broadcasted_iota(I32, (rows, 1), 0) // cs
    def group_steps(ch, grp):
        heads = [grp * gh + j for j in range(gh)]
        bg, gc, gct = gates[ch]
        r0 = ch * cs
        stack = lambda ref: jnp.concatenate([ref[r0:r0 + cs, h * HEAD_DIM:(h + 1) * HEAD_DIM] for h in heads],
                                            axis=0)
        col = lambda a, lane0: jnp.concatenate([a[:, lane0 + h:lane0 + h + 1] for h in heads], axis=0)
        k = stack(kn_ref)
        q = stack(qn_ref)
        v = stack(vv_ref)
        bcol = col(bg, 0)
        gcc = col(gc, N_DELTA_HEADS)
        gcr = jnp.concatenate([gct[N_DELTA_HEADS + h:N_DELTA_HEADS + h + 1, :] for h in heads], axis=1)
        g_last = [gc[cs - 1:cs, N_DELTA_HEADS + h:N_DELTA_HEADS + h + 1] for h in heads]
        glc = jnp.concatenate([jnp.broadcast_to(gl, (cs, 1)) for gl in g_last], axis=0)
        decay = jnp.exp(jnp.where(causal, gcc - gcr, -jnp.inf))
        kb = k * bcol
        eg = jnp.exp(gcc)
        kq = _mm_nt(jnp.concatenate([kb, q], axis=0), k)
        yield
        a = jnp.where(strict, kq[0:rows] * decay, 0.0)
        intra = jnp.where(causal, kq[rows:2 * rows] * decay, 0.0)
        x = -a
        nb = DELTA_INV_BLOCK
        y = jnp.where((rr // nb) == (cc // nb), x, 0.0)
        p = eye + y
        y = _mm(y, y)
        yield
        n_sq = max(1, int(nb - 1).bit_length())
        for lvl in range(1, n_sq):
            if lvl < n_sq - 1:
                py = _mm(jnp.concatenate([p, y], axis=0), y)
                p = p + py[0:rows]
                y = py[rows:2 * rows]
            else:
                p = p + _mm(p, y)
            yield
        size = 2 * nb
        while size <= cs:
            off = jnp.where(jnp.logical_and((rr // size) == (cc // size), (rr // (size // 2)) != (cc // (size // 2))),
                            x, 0.0)
            po = _mm(p, off)
            yield
            p = p + _mm(po, p)
            yield
            size *= 2
        sol = _mm(p, jnp.concatenate([v * bcol, kb * eg], axis=1))
        yield
        u = sol[:, 0:HEAD_DIM]
        w = sol[:, HEAD_DIM:2 * HEAD_DIM]
        lanes_g = slice(grp * gh * HEAD_DIM, (grp + 1) * gh * HEAD_DIM)
        while ch > 0 and not state_ready.get((ch - 1, grp)):
            yield
        s_g = s_scr[:, lanes_g]
        wq_s = _mm(jnp.concatenate([w, q * eg], axis=0), s_g)
        yield
        own = lambda m, r0: jnp.concatenate(
            [m[r0 + j * cs:r0 + (j + 1) * cs, j * HEAD_DIM:(j + 1) * HEAD_DIM] for j in range(gh)], axis=0)
        v_new = u - own(wq_s, 0)
        o = own(wq_s, rows) + _mm(intra, v_new)
        yield
        kg_t = (k * jnp.exp(glc - gcc)).T
        vn_blocks = jnp.concatenate([jnp.where(row_head == j, v_new, 0.0) for j in range(gh)], axis=1)
        s_decay = jnp.concatenate([jnp.broadcast_to(jnp.exp(gl), (1, HEAD_DIM)) for gl in g_last], axis=1)
        s_scr[:, lanes_g] = s_g * s_decay + _mm(kg_t, vn_blocks)
        state_ready[(ch, grp)] = True
        yield
        on = o * lax.rsqrt(jnp.mean(o * o, axis=-1, keepdims=True) + EPS) * ow_ref[...]
        for j, h in enumerate(heads):
            sl = slice(h * HEAD_DIM, (h + 1) * HEAD_DIM)
            od_ref[r0:r0 + cs, sl] = (on[j * cs:(j + 1) * cs] * _silu(z_ref[r0:r0 + cs, sl])).astype(BF16)

    chains = [group_steps(ch, grp) for ch in range(n_chunks) for grp in range(N_DELTA_HEADS // gh)]
    while chains:
        chains = [g for g in chains if next(g, "done") != "done"]

    so_ref[0] = s_scr[...]


def _delta_chunks(qn, kn, vv, bg, zsrc, z_col_block, state0, o_norm_w, b, t):
    n = b * t
    per_step = DELTA_CHUNKS_PER_STEP if (t // DELTA_CHUNK) % DELTA_CHUNKS_PER_STEP == 0 else 1
    nc = t // (DELTA_CHUNK * per_step)
    sw = N_DELTA_HEADS * HEAD_DIM
    row = lambda w, cb=0: pl.BlockSpec((DELTA_CHUNK * per_step, w), lambda bb, c: (bb * nc + c, cb))
    st = pl.BlockSpec((1, HEAD_DIM, sw), lambda bb, c: (bb, 0, 0))
    s_in = state0.transpose(0, 2, 1, 3).reshape(b, HEAD_DIM, sw)
    od, s_out = pl.pallas_call(
        functools.partial(_delta_chunk_kernel, n_chunks=per_step),
        name="delta_chunks",
        grid=(b, nc),
        in_specs=[row(DELTA_WIDTH), row(DELTA_WIDTH), row(DELTA_WIDTH), row(LANES),
                  row(DELTA_WIDTH, z_col_block), st, pl.BlockSpec((1, LANES), lambda bb, c: (0, 0))],
        out_specs=[row(DELTA_WIDTH), st],
        out_shape=[jax.ShapeDtypeStruct((n, DELTA_WIDTH), BF16),
                   jax.ShapeDtypeStruct((b, HEAD_DIM, sw), F32)],
        scratch_shapes=[pltpu.VMEM((HEAD_DIM, sw), F32)],
        compiler_params=_cparams(("parallel", "arbitrary")),
    )(qn, kn, vv, bg, zsrc, s_in, o_norm_w.reshape(1, LANES))
    return od, s_out.reshape(b, HEAD_DIM, N_DELTA_HEADS, HEAD_DIM).transpose(0, 2, 1, 3)


def _outproj_kernel(oap_ref, odp_ref, xp_ref, g1p_ref, shp_ref, scp_ref,
                    oas_ref, ods_ref, xs_ref, g1s_ref, shs_ref, scs_ref,
                    nw_ref, wo_ref, wrh_ref, br_ref, x1_ref, h2_ref, lg_ref, *, prompt_tiles):
    def body(oa_ref, od_ref, x_ref, g1_ref, sh_ref, sc_ref):
        mix = (_dot(oa_ref[...], wo_ref[0:ATTN_WIDTH, :])
               + _dot(od_ref[...], wo_ref[ATTN_WIDTH:ATTN_WIDTH + DELTA_WIDTH, :]))
        x1 = x_ref[...] + g1_ref[0] * mix
        x1_ref[...] = x1
        y = x1 * lax.rsqrt(jnp.mean(x1 * x1, axis=-1, keepdims=True) + EPS) * nw_ref[...]
        h2 = y * (1.0 + sc_ref[0]) + sh_ref[0]
        h2_ref[...] = h2
        hb = h2.astype(BF16)
        lo = (h2 - hb.astype(F32)).astype(BF16)
        wr = wrh_ref[...]
        both = _dot(hb, wr)
        lg_ref[...] = both[:, 0:LANES] + both[:, LANES:2 * LANES] + _dot(lo, wr[:, 0:LANES]) + br_ref[...]

    i = pl.program_id(0)
    pl.when(i < prompt_tiles)(functools.partial(body, oap_ref, odp_ref, xp_ref, g1p_ref, shp_ref, scp_ref))
    pl.when(i >= prompt_tiles)(functools.partial(body, oas_ref, ods_ref, xs_ref, g1s_ref, shs_ref, scs_ref))


def _out_projection(prompt, sample, norm2_w, w_out_bf, wr_both, b_rt, tm, prompt_seq_len):
    n_p, d = prompt[2].shape
    n_s = sample[2].shape[0]
    pt = n_p // tm
    tiles_per_seq = prompt_seq_len // tm
    p_row = lambda w: pl.BlockSpec((tm, w), lambda i: (jnp.minimum(i, pt - 1), 0))
    s_row = lambda w: pl.BlockSpec((tm, w), lambda i: (jnp.maximum(i - pt, 0), 0))
    p_mod = pl.BlockSpec((1, 1, d), lambda i: (jnp.minimum(i, pt - 1) // tiles_per_seq, 0, 0))
    s_mod = pl.BlockSpec((1, tm, d), lambda i: (jnp.maximum(i - pt, 0), 0, 0))
    row = lambda w: pl.BlockSpec((tm, w), lambda i: (i, 0))
    full = lambda a: pl.BlockSpec(a.shape, lambda i: (0, 0))
    n = n_p + n_s
    return pl.pallas_call(
        functools.partial(_outproj_kernel, prompt_tiles=pt),
        name="out_proj",
        grid=(n // tm,),
        in_specs=[p_row(ATTN_WIDTH), p_row(DELTA_WIDTH), p_row(d), p_mod, p_mod, p_mod,
                  s_row(ATTN_WIDTH), s_row(DELTA_WIDTH), s_row(d), s_mod, s_mod, s_mod,
                  pl.BlockSpec((1, d), lambda i: (0, 0)), full(w_out_bf), full(wr_both), full(b_rt)],
        out_specs=[row(d), row(d), row(LANES)],
        out_shape=[jax.ShapeDtypeStruct((n, d), F32), jax.ShapeDtypeStruct((n, d), F32),
                   jax.ShapeDtypeStruct((n, LANES), F32)],
        compiler_params=_cparams(("parallel",)),
    )(*prompt, *sample, norm2_w.reshape(1, d), w_out_bf, wr_both, b_rt)


def _route_tile(x):
    lane = lax.broadcasted_iota(I32, x.shape, 1)
    gl = jnp.where(lane < N_GROUPS, x, -jnp.inf)
    ge = jnp.exp(gl - jnp.max(gl, axis=1, keepdims=True))
    p = ge / jnp.sum(ge, axis=1, keepdims=True)
    p_max = jnp.max(p, axis=1, keepdims=True)
    grp = jnp.min(jnp.where(p == p_max, lane, LANES), axis=1, keepdims=True)
    e_lane = lane - N_GROUPS
    in_grp = jnp.logical_and(jnp.logical_and(e_lane >= 0, e_lane < N_EXPERTS),
                             (e_lane >> 3) == grp)
    rl = jnp.where(in_grp, x, -jnp.inf)
    v1 = jnp.max(rl, axis=1, keepdims=True)
    i1 = jnp.min(jnp.where(rl == v1, lane, LANES), axis=1, keepdims=True)
    rl2 = jnp.where(lane == i1, -jnp.inf, rl)
    v2 = jnp.max(rl2, axis=1, keepdims=True)
    i2 = jnp.min(jnp.where(rl2 == v2, lane, LANES), axis=1, keepdims=True)
    t = jnp.exp(v2 - v1)
    den = 1.0 + t
    eid = jnp.where(lane == 0, i1 - N_GROUPS, jnp.where(lane == 1, i2 - N_GROUPS, 0))
    gate = jnp.where(lane == 0, (1.0 / den) * p_max, jnp.where(lane == 1, (t / den) * p_max, 0.0))
    return eid, gate


def _route_kernel(lg_ref, eid_ref, gate_ref):
    eid_ref[...], gate_ref[...] = _route_tile(lg_ref[...])


def _route(logits, tm):
    n = logits.shape[0]
    spec = pl.BlockSpec((tm, LANES), lambda i: (i, 0))
    return pl.pallas_call(
        _route_kernel,
        name="route",
        grid=(n // tm,),
        in_specs=[spec],
        out_specs=[spec, spec],
        out_shape=[jax.ShapeDtypeStruct((n, LANES), I32), jax.ShapeDtypeStruct((n, LANES), F32)],
        compiler_params=_cparams(("parallel",)),
    )(logits)


def _row_gather(idx_ref, base, n_rows, src_hbm, dst, sem):
    def body(r, carry):
        pltpu.make_async_copy(src_hbm.at[pl.ds(idx_ref[base + r], 1), :], dst.at[pl.ds(r, 1), :], sem).start()
        return carry

    lax.fori_loop(0, n_rows, body, 0, unroll=8)


def _row_gather_wait(n_rows, src_hbm, dst, sem):
    pltpu.make_async_copy(src_hbm.at[pl.ds(0, n_rows), :], dst, sem).wait()


def _moe_kernel(tok_ref, j0_ref, be_ref, na_ref, h_hbm, wg_ref, wu_ref, wd_ref, o_ref,
                x_even, x_odd, sem, wg_scr, wu_scr, wd_scr, *, bm):
    i = pl.program_id(0)
    n_act = na_ref[0]
    bufs = ((x_even, sem.at[0]), (x_odd, sem.at[1]))

    @pl.when(i == 0)
    def _():
        _row_gather(tok_ref, j0_ref[0], bm, h_hbm, x_even, sem.at[0])

    changed = jnp.logical_or(i == 0, be_ref[i] != be_ref[jnp.maximum(i - 1, 0)])

    @pl.when(jnp.logical_and(i < n_act, changed))
    def _():
        wg_scr[...] = wg_ref[0].astype(BF16)
        wu_scr[...] = wu_ref[0].astype(BF16)
        wd_scr[...] = wd_ref[0].astype(BF16)

    for parity in range(2):
        cur, cur_sem = bufs[parity]
        nxt, nxt_sem = bufs[1 - parity]

        @pl.when(jnp.logical_and(i < n_act, i % 2 == parity))
        def _():
            _row_gather_wait(bm, h_hbm, cur, cur_sem)
            base = j0_ref[i + 1]
            for r in range(bm):
                pltpu.make_async_copy(h_hbm.at[pl.ds(tok_ref[base + r], 1), :], nxt.at[pl.ds(r, 1), :],
                                      nxt_sem).start()
            x = cur[...].astype(BF16)
            hid = _silu(_dot(x, wg_scr[...])) * _dot(x, wu_scr[...])
            o_ref[...] = _dot(hid.astype(BF16), wd_scr[...])

        @pl.when(jnp.logical_and(i == n_act, i % 2 == parity))
        def _():
            _row_gather_wait(bm, h_hbm, cur, cur_sem)

    @pl.when(i >= n_act)
    def _():
        o_ref[...] = jnp.zeros(o_ref.shape, F32)


def _moe_experts(tok_sorted, block_j0, block_exp, n_active, h2, w_gate, w_up, w_down, bm):
    ns = block_exp.shape[0] * bm
    d = h2.shape[1]
    f = w_gate.shape[2]
    grid_spec = pltpu.PrefetchScalarGridSpec(
        num_scalar_prefetch=4,
        grid=(ns // bm,),
        in_specs=[pl.BlockSpec(memory_space=pl.ANY),
                  pl.BlockSpec((1, d, f), lambda i, tok, j0, be, na: (be[i], 0, 0)),
                  pl.BlockSpec((1, d, f), lambda i, tok, j0, be, na: (be[i], 0, 0)),
                  pl.BlockSpec((1, f, d), lambda i, tok, j0, be, na: (be[i], 0, 0))],
        out_specs=pl.BlockSpec((bm, d), lambda i, tok, j0, be, na: (i, 0)),
        scratch_shapes=[pltpu.VMEM((bm, d), F32), pltpu.VMEM((bm, d), F32), pltpu.SemaphoreType.DMA((2,)),
                        pltpu.VMEM((d, f), BF16), pltpu.VMEM((d, f), BF16), pltpu.VMEM((f, d), BF16)],
    )
    return pl.pallas_call(
        functools.partial(_moe_kernel, bm=bm),
        name="moe_experts",
        grid_spec=grid_spec,
        out_shape=jax.ShapeDtypeStruct((ns, d), F32),
        compiler_params=_cparams(("arbitrary",)),
    )(tok_sorted, block_j0, block_exp, n_active, h2, w_gate, w_up, w_down)


def _combine_kernel(dest_ref, x1_ref, gt_ref, g2p_ref, g2s_ref, y_hbm, op_ref, os_ref, y_buf, sem, *,
                    tm, prompt_tiles):
    i = pl.program_id(0)
    n = pl.num_programs(0)
    slot = i % 2

    @pl.when(i == 0)
    def _():
        _row_gather(dest_ref, 0, 2 * tm, y_hbm, y_buf.at[0], sem.at[0])

    for parity in range(2):
        @pl.when(jnp.logical_and(i + 1 < n, slot == parity))
        def _():
            base = (i + 1) * 2 * tm
            for r in range(2 * tm):
                pltpu.make_async_copy(y_hbm.at[pl.ds(dest_ref[base + r], 1), :],
                                      y_buf.at[1 - parity, pl.ds(r, 1), :], sem.at[1 - parity]).start()

    _row_gather_wait(2 * tm, y_hbm, y_buf.at[slot], sem.at[slot])
    gt = gt_ref[...]
    y = y_buf[slot, 0:tm, :] * gt[:, 0:1] + y_buf[slot, tm:2 * tm, :] * gt[:, 1:2]

    @pl.when(i < prompt_tiles)
    def _():
        op_ref[...] = x1_ref[...] + g2p_ref[0] * y

    @pl.when(i >= prompt_tiles)
    def _():
        os_ref[...] = x1_ref[...] + g2s_ref[0] * y


def _combine(x1, dest, y_rows, gates, gate2_p, gate2_s, tm, n_prompt, prompt_seq_len):
    n, d = x1.shape
    pt = n_prompt // tm
    tiles_per_seq = prompt_seq_len // tm
    dest_tiles = dest.reshape(n // tm, tm, 2).transpose(0, 2, 1).reshape(-1)
    row = lambda w: pl.BlockSpec((tm, w), lambda i, dst: (i, 0))
    grid_spec = pltpu.PrefetchScalarGridSpec(
        num_scalar_prefetch=1,
        grid=(n // tm,),
        in_specs=[row(d), row(LANES),
                  pl.BlockSpec((1, 1, d), lambda i, dst: (jnp.minimum(i, pt - 1) // tiles_per_seq, 0, 0)),
                  pl.BlockSpec((1, tm, d), lambda i, dst: (jnp.maximum(i - pt, 0), 0, 0)),
                  pl.BlockSpec(memory_space=pl.ANY)],
        out_specs=[pl.BlockSpec((tm, d), lambda i, dst: (jnp.minimum(i, pt - 1), 0)),
                   pl.BlockSpec((tm, d), lambda i, dst: (jnp.maximum(i - pt, 0), 0))],
        scratch_shapes=[pltpu.VMEM((2, 2 * tm, d), F32), pltpu.SemaphoreType.DMA((2,))],
    )
    return pl.pallas_call(
        functools.partial(_combine_kernel, tm=tm, prompt_tiles=pt),
        name="moe_combine",
        grid_spec=grid_spec,
        out_shape=[jax.ShapeDtypeStruct((n_prompt, d), F32), jax.ShapeDtypeStruct((n - n_prompt, d), F32)],
        compiler_params=_cparams(("arbitrary",)),
    )(dest_tiles, x1, gates, gate2_p, gate2_s, y_rows)


FRONT_SECTIONS = ((C_QA, ATTN_WIDTH), (C_QI, IDX_WIDTH),
                  (C_CONV, DELTA_WIDTH), (C_CONV + DELTA_WIDTH, DELTA_WIDTH), (C_CONV + 2 * DELTA_WIDTH, DELTA_WIDTH),
                  (C_KA, PROJ_PACKED - C_KA), (C_ZD, DELTA_WIDTH))


def _prompt_front_kernel(x_ref, sh_ref, sc_ref, nw_ref, w_ref, tabm_ref, tabi_ref, qw_ref, kw_ref, iw_ref,
                         prev_ref, cw_ref, al_ref, dt_ref,
                         q_ref, qi_ref, z_ref, qn_ref, kn_ref, vv_ref, kf_ref, kb_ref, vf_ref, vt_ref,
                         kif_ref, kx_ref, bg_ref, misc_ref, tail_ref,
                         carry_scr, xp_scr, *, tiles_per_seq, tm):
    i = pl.program_id(0)
    tile_in_seq = i % tiles_per_seq
    x = x_ref[...]
    y = x * lax.rsqrt(jnp.mean(x * x, axis=-1, keepdims=True) + EPS) * nw_ref[...]
    h = (y * (1.0 + sc_ref[0]) + sh_ref[0]).astype(BF16)
    tabm = tabm_ref[0]
    tabi = tabi_ref[0]
    half_main = HEAD_DIM // ROPE_FRACTION // 2
    half_idx = IDX_DIM // ROPE_FRACTION // 2

    @pl.when(tile_in_seq == 0)
    def _():
        carry_scr[...] = prev_ref[0]

    def queries(pj):
        for hh in range(N_ATTN_HEADS):
            sl = slice(hh * HEAD_DIM, (hh + 1) * HEAD_DIM)
            v = _rope(_rms_head(pj[:, sl], qw_ref[...]), tabm, half_main)
            q_ref[:, sl] = (v * (HEAD_DIM ** -0.5)).astype(BF16)

    def index_queries(pj):
        for p in range(IDX_WIDTH // LANES):
            sl = slice(p * LANES, (p + 1) * LANES)
            qi_ref[:, sl] = _rope(pj[:, sl], tabi, half_idx).astype(BF16)

    def gate_z(pj):
        z_ref[...] = pj

    def conv_section(sec, pj):
        cols = slice(sec * DELTA_WIDTH, (sec + 1) * DELTA_WIDTH)
        xp_scr[sec, 0:SUBLANES, :] = carry_scr[:, cols]
        xp_scr[sec, SUBLANES:SUBLANES + tm, :] = pj
        carry_scr[:, cols] = pj[tm - SUBLANES:tm, :]
        base = SUBLANES - (CONV_WIDTH - 1)
        out = (qn_ref, kn_ref, vv_ref)[sec]
        for hh in range(N_DELTA_HEADS):
            sl = slice(hh * HEAD_DIM, (hh + 1) * HEAD_DIM)
            wsl = slice(sec * DELTA_WIDTH + hh * HEAD_DIM, sec * DELTA_WIDTH + (hh + 1) * HEAD_DIM)
            v = xp_scr[sec, base:base + tm, sl] * cw_ref[0:1, wsl]
            for j in range(1, CONV_WIDTH):
                v = v + xp_scr[sec, base + j:base + j + tm, sl] * cw_ref[j:j + 1, wsl]
            v = _silu(v)
            if sec < 2:
                v = v * lax.rsqrt(jnp.sum(v * v, axis=-1, keepdims=True) + EPS)
            if sec == 0:
                v = v * (HEAD_DIM ** -0.5)
            out[:, sl] = v
        if sec == 2:
            tail_ref[0] = carry_scr[...]

    def keys_values_misc(pj):
        for hk in range(N_KV_HEADS):
            sl = slice(hk * HEAD_DIM, (hk + 1) * HEAD_DIM)
            v = _rope(_rms_head(pj[:, sl], kw_ref[...]), tabm, half_main)
            kf_ref[:, sl] = v
            kb_ref[:, sl] = v.astype(BF16)
        va = pj[:, KV_WIDTH:2 * KV_WIDTH]
        vf_ref[...] = va
        vt_ref[...] = va.T.astype(BF16)
        m = pj[:, 2 * KV_WIDTH:2 * KV_WIDTH + LANES]
        misc_ref[...] = m
        lane = lax.broadcasted_iota(I32, m.shape, 1)
        ki = jnp.where(lane < IDX_DIM, m, 0.0)
        ms = jnp.sum(ki * ki, axis=-1, keepdims=True) * (1.0 / IDX_DIM)
        v = _rope(ki * lax.rsqrt(ms + EPS) * iw_ref[...], tabi, half_idx)
        kif_ref[...] = v[:, 0:IDX_DIM]
        kx_ref[...] = (v + pltpu.roll(v, IDX_DIM, 1)).astype(BF16)
        beta = _sigmoid(m)
        g = -jnp.exp(al_ref[...]) * _softplus(m + dt_ref[...])
        is_b = jnp.logical_and(lane >= M_BD, lane < M_BD + N_DELTA_HEADS)
        is_g = jnp.logical_and(lane >= M_AD, lane < M_AD + N_DELTA_HEADS)
        comb = jnp.where(is_b, beta, jnp.where(is_g, g, 0.0))
        bg_ref[...] = pltpu.roll(comb, LANES - M_BD, 1)

    epilogues = (queries, index_queries,
                 functools.partial(conv_section, 0), functools.partial(conv_section, 1),
                 functools.partial(conv_section, 2), keys_values_misc, gate_z)
    project = lambda k: _dot(h, w_ref[:, FRONT_SECTIONS[k][0]:FRONT_SECTIONS[k][0] + FRONT_SECTIONS[k][1]])
    pj_next = project(0)
    for k, epilogue in enumerate(epilogues):
        pj = pj_next
        if k + 1 < len(epilogues):
            pj_next = project(k + 1)
        epilogue(pj)


def _prompt_front(x2d, shift, scale, norm_w, w_packed, pos, q_norm_w, k_norm_w, idx_k_norm_w, prev8, conv_w,
                  a_log, dt_bias, b, t, tm):
    n, d = x2d.shape
    np_ = w_packed.shape[1]
    tiles_per_seq = t // tm
    tabm = _rope_tables(pos, HEAD_DIM, LANES).reshape(tiles_per_seq, tm, 3 * LANES)
    tabi = _rope_tables(pos, IDX_DIM, IDX_DIM).reshape(tiles_per_seq, tm, 3 * LANES)
    iw = jnp.concatenate([idx_k_norm_w, jnp.zeros((LANES - IDX_DIM,), F32)]).reshape(1, LANES)
    pad_vec = lambda v: jnp.zeros((1, LANES), F32).at[0, M_AD:M_AD + N_DELTA_HEADS].set(v)
    mod_spec = pl.BlockSpec((1, 1, d), lambda i: (i // tiles_per_seq, 0, 0))
    tab_spec = pl.BlockSpec((1, tm, 3 * LANES), lambda i: (i % tiles_per_seq, 0, 0))
    vec = lambda w: pl.BlockSpec((1, w), lambda i: (0, 0))
    row = lambda w: pl.BlockSpec((tm, w), lambda i: (i, 0))
    seq_state = pl.BlockSpec((1, SUBLANES, CONV_CHANNELS), lambda i: (i // tiles_per_seq, 0, 0))
    widths = [(ATTN_WIDTH, BF16), (IDX_WIDTH, BF16), (DELTA_WIDTH, F32), (DELTA_WIDTH, F32), (DELTA_WIDTH, F32),
              (DELTA_WIDTH, F32), (KV_WIDTH, F32), (KV_WIDTH, BF16), (KV_WIDTH, F32)]
    outs = pl.pallas_call(
        functools.partial(_prompt_front_kernel, tiles_per_seq=tiles_per_seq, tm=tm),
        name="prompt_front",
        grid=(n // tm,),
        in_specs=[row(d), mod_spec, mod_spec, vec(d),
                  pl.BlockSpec((d, np_), lambda i: (0, 0), pipeline_mode=pl.Buffered(1)),
                  tab_spec, tab_spec, vec(LANES), vec(LANES), vec(LANES), seq_state,
                  pl.BlockSpec((CONV_WIDTH, CONV_CHANNELS), lambda i: (0, 0)), vec(LANES), vec(LANES)],
        out_specs=[row(w) for w, _ in widths]
                  + [pl.BlockSpec((KV_WIDTH, tm), lambda i: (0, i)), row(IDX_DIM), row(LANES), row(LANES), row(LANES),
                     seq_state],
        out_shape=[jax.ShapeDtypeStruct((n, w), dt) for w, dt in widths]
                  + [jax.ShapeDtypeStruct((KV_WIDTH, n), BF16), jax.ShapeDtypeStruct((n, IDX_DIM), F32),
                     jax.ShapeDtypeStruct((n, LANES), BF16), jax.ShapeDtypeStruct((n, LANES), F32),
                     jax.ShapeDtypeStruct((n, LANES), F32),
                     jax.ShapeDtypeStruct((b, SUBLANES, CONV_CHANNELS), F32)],
        scratch_shapes=[pltpu.VMEM((SUBLANES, CONV_CHANNELS), F32),
                        pltpu.VMEM((3, SUBLANES + tm, DELTA_WIDTH), F32)],
        compiler_params=pltpu.CompilerParams(dimension_semantics=("arbitrary",),
                                             vmem_limit_bytes=FRONT_VMEM_LIMIT),
    )(x2d, shift, scale, norm_w.reshape(1, d), w_packed, tabm, tabi,
      q_norm_w.reshape(1, LANES), k_norm_w.reshape(1, LANES), iw, prev8, conv_w, pad_vec(a_log), pad_vec(dt_bias))
    names = ("q", "qi", "z", "qn", "kn", "vv", "kf", "kb", "vf", "vt", "kif", "kx", "bg", "misc", "tail")
    return dict(zip(names, outs))


def _pick_tile(n, pref, mult=16):
    t = min(pref, n)
    while n % t or t % mult:
        t -= 1
    return t


def _pack_w_in(w_in):
    d = w_in.shape[0]
    bounds = np.cumsum(PROJ_SIZES)[:-1].tolist()
    qa, ka, va, qi, ki, wi, qd, kd, vd, zd, bd, ad = jnp.split(w_in, bounds, axis=1)
    used = IDX_DIM + N_IDX_HEADS + 2 * N_DELTA_HEADS
    misc = jnp.concatenate([ki, wi, bd, ad, jnp.zeros((d, LANES - used), w_in.dtype)], axis=1)
    cols = [qa, qi, zd, qd, kd, vd, ka, va, misc]
    width = sum(c.shape[1] for c in cols)
    cols.append(jnp.zeros((d, PROJ_PACKED - width), w_in.dtype))
    return jnp.concatenate(cols, axis=1).astype(BF16)


def _route_and_sort(eid, bm):
    n = eid.shape[0]
    nk = 2 * n
    flat_e = eid.reshape(-1)
    order = jnp.argsort(flat_e, stable=True).astype(I32)
    inv = jnp.argsort(order).astype(I32)
    onehot = flat_e[:, None] == jnp.arange(N_EXPERTS, dtype=I32)[None, :]
    counts = jnp.sum(onehot.astype(I32), axis=0)
    padded = (counts + bm - 1) // bm * bm
    pad_end = jnp.cumsum(padded)
    shift = (pad_end - padded) - (jnp.cumsum(counts) - counts)
    dest = inv + jnp.sum(jnp.where(onehot, shift[None, :], 0), axis=1)
    n_blocks = -(-nk // bm) + N_EXPERTS + 1
    block_exp = jnp.minimum(jnp.searchsorted(pad_end, jnp.arange(n_blocks, dtype=I32) * bm, side='right'),
                            N_EXPERTS - 1).astype(I32)
    n_active = (pad_end[-1] // bm).astype(I32).reshape(1)
    block_j0 = jnp.clip(jnp.arange(n_blocks, dtype=I32) * bm - shift[block_exp], 0, nk)
    tok_sorted = jnp.concatenate([order // 2, jnp.zeros((bm,), I32)])
    return tok_sorted, block_j0, dest.astype(I32).reshape(n, 2), block_exp, n_active


def _layer(layer, yp, ys, cache_k, cache_v, cache_idx, state_ssm, state_conv, page_table, c_prompt, c_sample,
           w_in, w_out, conv_w, a_log, dt_bias, q_norm_w, k_norm_w, idx_k_norm_w, o_norm_w, norm1_w, norm2_w,
           w_ada, b_ada, w_group, b_group, w_router, b_router, w_gate, w_up, w_down):
    bp, tp, d = yp.shape
    bs, ts, _ = ys.shape
    past = page_table.shape[1] * PAGE_SIZE
    rows = SAMPLE_ROWS
    assert CONV_WIDTH - 1 <= ts <= SUBLANES <= rows and tp % SEL_SPAN == 0 and tp % DELTA_CHUNK == 0

    n_c = bp + bs
    n_c_pad = -(-n_c // SUBLANES) * SUBLANES
    c_all = jnp.concatenate([c_prompt, c_sample, jnp.zeros((n_c_pad - n_c, d), F32)], axis=0)
    mod = _ada_modulation(c_all, w_ada, b_ada)
    mods = jnp.split(mod, N_MOD, axis=1)
    mp = [m[:bp].reshape(bp, 1, d) for m in mods]
    ms = [jnp.repeat(m[bp:bp + bs], rows, axis=0).reshape(1, bs * rows, d) for m in mods]

    w_packed = _pack_w_in(w_in)
    w_out_bf = w_out.astype(BF16)
    w_rt = jnp.concatenate([w_group, w_router, jnp.zeros((d, LANES - N_GROUPS - N_EXPERTS), F32)], axis=1)
    wr_hi = w_rt.astype(BF16)
    wr_both = jnp.concatenate([wr_hi, (w_rt - wr_hi.astype(F32)).astype(BF16)], axis=1)
    b_rt = jnp.concatenate([b_group, b_router, jnp.zeros((LANES - N_GROUPS - N_EXPERTS,), F32)]).reshape(1, LANES)

    np_ = bp * tp
    xp2 = yp.reshape(np_, d)
    tm_p = _pick_tile(tp, 256)
    fr = _prompt_front(xp2, mp[0], mp[1], norm1_w, w_packed, jnp.arange(tp), q_norm_w, k_norm_w, idx_k_norm_w,
                       jnp.zeros((bp, SUBLANES, CONV_CHANNELS), F32), conv_w, a_log, dt_bias, bp, tp, tm_p)
    kf_p, kif_p = fr["kf"], fr["kif"]
    oa_p = _dsa_prompt(fr["q"], fr["qi"], fr["misc"], fr["kb"], fr["vt"], fr["kx"], bp, tp)
    od_p, ssm_p = _delta_chunks(fr["qn"], fr["kn"], fr["vv"], fr["bg"], fr["z"], 0,
                                jnp.zeros((bp, N_DELTA_HEADS, HEAD_DIM, HEAD_DIM), F32), o_norm_w, bp, tp)

    ns_ = bs * rows
    xs2 = jnp.pad(ys, ((0, 0), (0, rows - ts), (0, 0))).reshape(ns_, d)
    tm_s = _pick_tile(ns_, 256)
    proj_s = _in_projection(xs2, ms[0].reshape(ns_ // tm_s, tm_s, d), ms[1].reshape(ns_ // tm_s, tm_s, d),
                            norm1_w, w_packed, tm_s, tm_s)
    q_s, kf_s, kb_s, vb_s, qi_s, kif_s, _ = _attention_prep(
        proj_s, past + jnp.arange(rows), rows, q_norm_w, k_norm_w, idx_k_norm_w, False)
    qr = SAMPLE_QROWS
    q_t = qi_s.reshape(bs, rows, N_IDX_HEADS, IDX_DIM)[:, :qr].transpose(0, 2, 1, 3)
    q_t = q_t.reshape(bs, N_IDX_HEADS * qr, IDX_DIM)
    w_col = proj_s[:, C_MISC + M_WI:C_MISC + M_WI + N_IDX_HEADS].reshape(bs, rows, N_IDX_HEADS)[:, :qr]
    w_col = w_col.transpose(0, 2, 1).reshape(bs, N_IDX_HEADS * qr, 1)
    pages = _pick_tile(page_table.shape[1], SAMPLE_PAGES_PER_STEP, 1)
    keys_past, keys_new = _sample_scores(page_table, q_t, w_col, kif_s, cache_idx, layer, pages, ts)
    n_sel_s = min(TOPK_MAX, (past + ts) // 4)
    oa_s = _sample_attend(page_table, keys_past, keys_new, q_s, kb_s, vb_s, cache_k, cache_v, layer, pages, n_sel_s)
    prev8 = jnp.pad(state_conv, ((0, 0), (SUBLANES - (CONV_WIDTH - 1), 0), (0, 0)))
    qn_s, kn_s, vv_s, bg_s = _delta_prep(proj_s, prev8, conv_w, a_log, dt_bias, bs, rows, rows, ts)
    to_chunk = lambda a: jnp.pad(a.reshape(bs, rows, -1), ((0, 0), (0, DELTA_CHUNK - rows), (0, 0))).reshape(
        bs * DELTA_CHUNK, -1)
    z_s = proj_s[:, C_ZD:C_ZD + DELTA_WIDTH]
    od_s, ssm_s = _delta_chunks(to_chunk(qn_s), to_chunk(kn_s), to_chunk(vv_s), to_chunk(bg_s), to_chunk(z_s), 0,
                                state_ssm, o_norm_w, bs, DELTA_CHUNK)
    od_s = od_s.reshape(bs, DELTA_CHUNK, DELTA_WIDTH)[:, :rows].reshape(ns_, DELTA_WIDTH)

    tm_o = _pick_tile(tp, tm_s)
    assert ns_ % tm_o == 0
    per_tok = lambda m: m.reshape(ns_ // tm_o, tm_o, d)
    n_all = np_ + ns_
    x1_all, h2_all, lg_all = _out_projection(
        (oa_p, od_p, xp2, mp[2], mp[3], mp[4]),
        (oa_s, od_s, xs2, per_tok(ms[2]), per_tok(ms[3]), per_tok(ms[4])),
        norm2_w, w_out_bf, wr_both, b_rt, tm_o, tp)
    eid, gates = _route(lg_all, _pick_tile(n_all, 512, SUBLANES))
    bm = 256
    tok_sorted, block_j0, dest, block_exp, n_active = _route_and_sort(eid[:, 0:2], bm)
    yb = _moe_experts(tok_sorted, block_j0, block_exp, n_active, h2_all, w_gate, w_up, w_down, bm)
    out_p, out_s = _combine(x1_all, dest, yb, gates, mp[5], per_tok(ms[5]), tm_o, np_, tp)

    valid = lambda a: a.reshape(bs, rows, -1)[:, :ts]
    conv_p = fr["tail"][:, SUBLANES - (CONV_WIDTH - 1):]
    conv_s = proj_s.reshape(bs, rows, PROJ_PACKED)[:, ts - (CONV_WIDTH - 1):ts, C_CONV:C_CONV + CONV_CHANNELS]
    return (out_p.reshape(bp, tp, d), valid(out_s),
            kf_p.reshape(bp, tp, N_KV_HEADS, HEAD_DIM),
            fr["vf"].reshape(bp, tp, N_KV_HEADS, HEAD_DIM),
            kif_p.reshape(bp, tp, IDX_DIM), ssm_p, conv_p,
            valid(kf_s).reshape(bs, ts, N_KV_HEADS, HEAD_DIM),
            valid(proj_s[:, C_VA:C_VA + KV_WIDTH]).reshape(bs, ts, N_KV_HEADS, HEAD_DIM),
            valid(kif_s), ssm_s, conv_s)


def kernel(x_prompt, x_sample, cache_k, cache_v, cache_idx_k, state_ssm, state_conv, page_table, c_prompt, c_sample,
           w_in, w_out, conv_w, a_log, dt_bias, q_norm_w, k_norm_w, idx_k_norm_w, o_norm_w, norm1_w, norm2_w,
           w_ada, b_ada, w_group, b_group, w_router, b_router, w_gate, w_up, w_down):
    depth = w_in.shape[0]
    yp, ys = x_prompt, x_sample
    per_layer = []
    for l in range(depth):
        res = _layer(l, yp, ys, cache_k, cache_v, cache_idx_k, state_ssm[l], state_conv[l], page_table,
                     c_prompt, c_sample, w_in[l], w_out[l], conv_w[l], a_log[l], dt_bias[l], q_norm_w[l],
                     k_norm_w[l], idx_k_norm_w[l], o_norm_w[l], norm1_w[l], norm2_w[l], w_ada[l], b_ada[l],
                     w_group[l], b_group[l], w_router[l], b_router[l], w_gate[l], w_up[l], w_down[l])
        yp, ys = res[0], res[1]
        per_layer.append(res[2:])
    stacked = tuple(jnp.stack([pl_[j] for pl_ in per_layer]) for j in range(10))
    return (yp, ys) + stacked
```

```python
import functools

import jax
import jax.numpy as jnp
import numpy as np
from jax import lax
from jax.experimental import pallas as pl
from jax.experimental.pallas import tpu as pltpu

F32 = jnp.float32
BF16 = jnp.bfloat16
I32 = jnp.int32

HEAD_DIM = 128
N_ATTN_HEADS = 8
N_KV_HEADS = 2
KV_GROUP = N_ATTN_HEADS // N_KV_HEADS
N_DELTA_HEADS = 8
N_IDX_HEADS = 16
IDX_DIM = 64
ATTN_WIDTH = N_ATTN_HEADS * HEAD_DIM
KV_WIDTH = N_KV_HEADS * HEAD_DIM
DELTA_WIDTH = N_DELTA_HEADS * HEAD_DIM
IDX_WIDTH = N_IDX_HEADS * IDX_DIM
CONV_CHANNELS = 3 * DELTA_WIDTH
TOPK_MAX = 256
ROPE_THETA = 500000.0
ROPE_FRACTION = 4
CONV_WIDTH = 4
DELTA_CHUNK = 64
N_GROUPS = 8
EXPERTS_PER_GROUP = 8
N_EXPERTS = N_GROUPS * EXPERTS_PER_GROUP
N_MOD = 6
EPS = 1e-6
PAGE_SIZE = 128
PROJ_SIZES = (ATTN_WIDTH, KV_WIDTH, KV_WIDTH, IDX_WIDTH, IDX_DIM, N_IDX_HEADS,
              DELTA_WIDTH, DELTA_WIDTH, DELTA_WIDTH, DELTA_WIDTH, N_DELTA_HEADS, N_DELTA_HEADS)

LANES = 128
SUBLANES = 8
VMEM_LIMIT = 56 * 1024 * 1024
FRONT_VMEM_LIMIT = 60 * 1024 * 1024

C_QA = 0
C_QI = 1024
C_ZD = 2048
C_CONV = 3072
C_KA = 6144
C_VA = 6400
C_MISC = 6656
PROJ_PACKED = 6912
M_KI = 0
M_WI = 64
M_BD = 80
M_AD = 88

Q_TILE = 128
KEY_CHUNK = 256
SEL_SPAN = 512
NEG_INF_KEY = -2139095041
SAMPLE_ROWS = 16
SAMPLE_QROWS = SUBLANES
SAMPLE_PAGES_PER_STEP = 64
NEG_BIG = -1e30
INT_MIN = -2147483648
INT_MAX = 2147483647


def _cparams(sem):
    return pltpu.CompilerParams(dimension_semantics=sem, vmem_limit_bytes=VMEM_LIMIT)


def _dot(a, b):
    return jnp.dot(a, b, preferred_element_type=F32)


def _dot_nt(a, b):
    return lax.dot_general(a, b, (((1,), (1,)), ((), ())), preferred_element_type=F32)


def _sigmoid(x):
    return 0.5 * jnp.tanh(0.5 * x) + 0.5


def _silu(x):
    return x * _sigmoid(x)


def _softplus(x):
    return jnp.maximum(x, 0.0) + jnp.log(1.0 + jnp.exp(-jnp.abs(x)))


def _ada_kernel(c_ref, w_ref, b_ref, o_ref):
    s = _silu(c_ref[...]).astype(BF16)
    o_ref[...] = _dot(s, w_ref[...].astype(BF16)) + b_ref[...]


def _ada_modulation(c, w_ada, b_ada):
    r, d = c.shape
    n = w_ada.shape[1]
    tn = 1024 if n % 1024 == 0 else n
    return pl.pallas_call(
        _ada_kernel,
        name="ada_mod",
        grid=(n // tn,),
        in_specs=[pl.BlockSpec((r, d), lambda j: (0, 0)),
                  pl.BlockSpec((d, tn), lambda j: (0, j)),
                  pl.BlockSpec((1, tn), lambda j: (0, j))],
        out_specs=pl.BlockSpec((r, tn), lambda j: (0, j)),
        out_shape=jax.ShapeDtypeStruct((r, n), F32),
        compiler_params=_cparams(("parallel",)),
    )(c, w_ada, b_ada.reshape(1, n))


INPROJ_COLS = 1152


def _inproj_kernel(x_ref, sh_ref, sc_ref, nw_ref, w_ref, o_ref):
    x = x_ref[...]
    y = x * lax.rsqrt(jnp.mean(x * x, axis=-1, keepdims=True) + EPS) * nw_ref[...]
    h = (y * (1.0 + sc_ref[0]) + sh_ref[0]).astype(BF16)
    for c0 in range(0, o_ref.shape[1], INPROJ_COLS):
        o_ref[:, c0:c0 + INPROJ_COLS] = _dot(h, w_ref[:, c0:c0 + INPROJ_COLS])


def _in_projection(x2d, shift, scale, norm_w, w_packed, tm, rows_per_mod_block):
    n, d = x2d.shape
    np_ = w_packed.shape[1]
    r = shift.shape[1]
    tiles_per_mod = rows_per_mod_block // tm
    mod_spec = pl.BlockSpec((1, r, d), lambda i: (i // tiles_per_mod, 0, 0))
    return pl.pallas_call(
        _inproj_kernel,
        name="in_proj",
        grid=(n // tm,),
        in_specs=[pl.BlockSpec((tm, d), lambda i: (i, 0)),
                  mod_spec, mod_spec,
                  pl.BlockSpec((1, d), lambda i: (0, 0)),
                  pl.BlockSpec((d, np_), lambda i: (0, 0), pipeline_mode=pl.Buffered(1))],
        out_specs=pl.BlockSpec((tm, np_), lambda i: (i, 0)),
        out_shape=jax.ShapeDtypeStruct((n, np_), F32),
        compiler_params=_cparams(("parallel",)),
    )(x2d, shift, scale, norm_w.reshape(1, d), w_packed)


def _rope(x, tab, rot):
    c = tab[:, 0:LANES]
    s1 = tab[:, LANES:2 * LANES]
    s2 = tab[:, 2 * LANES:3 * LANES]
    return x * c + pltpu.roll(x, LANES - rot, 1) * s1 + pltpu.roll(x, rot, 1) * s2


def _rms_head(x, w):
    return x * lax.rsqrt(jnp.mean(x * x, axis=-1, keepdims=True) + EPS) * w


def _prep_kernel(qa_ref, qi_ref, ka_ref, va_ref, misc_ref, tabm_ref, tabi_ref, qw_ref, kw_ref, iw_ref,
                 q_ref, kf_ref, kb_ref, vb_ref, qib_ref, kif_ref, kib_ref, *, transpose_v):
    tabm = tabm_ref[0]
    tabi = tabi_ref[0]
    half_main = HEAD_DIM // ROPE_FRACTION // 2
    half_idx = IDX_DIM // ROPE_FRACTION // 2
    for h in range(N_ATTN_HEADS):
        sl = slice(h * HEAD_DIM, (h + 1) * HEAD_DIM)
        y = _rope(_rms_head(qa_ref[:, sl], qw_ref[...]), tabm, half_main)
        q_ref[:, sl] = (y * (HEAD_DIM ** -0.5)).astype(BF16)
    for h in range(N_KV_HEADS):
        sl = slice(h * HEAD_DIM, (h + 1) * HEAD_DIM)
        y = _rope(_rms_head(ka_ref[:, sl], kw_ref[...]), tabm, half_main)
        kf_ref[:, sl] = y
        kb_ref[:, sl] = y.astype(BF16)
    if transpose_v:
        vb_ref[...] = va_ref[...].T.astype(BF16)
    else:
        vb_ref[...] = va_ref[...].astype(BF16)
    for p in range(IDX_WIDTH // LANES):
        sl = slice(p * LANES, (p + 1) * LANES)
        qib_ref[:, sl] = _rope(qi_ref[:, sl], tabi, half_idx).astype(BF16)
    m = misc_ref[...]
    lane = lax.broadcasted_iota(I32, m.shape, 1)
    ki = jnp.where(lane < IDX_DIM, m, 0.0)
    ms = jnp.sum(ki * ki, axis=-1, keepdims=True) * (1.0 / IDX_DIM)
    y = _rope(ki * lax.rsqrt(ms + EPS) * iw_ref[...], tabi, half_idx)
    kif_ref[...] = y[:, 0:IDX_DIM]
    kib_ref[...] = (y + pltpu.roll(y, IDX_DIM, 1)).astype(BF16)


def _rope_tables(pos, head_dim, group):
    d_rot = head_dim // ROPE_FRACTION
    half = d_rot // 2
    inv_freq = jnp.power(ROPE_THETA, -(jnp.arange(half, dtype=F32) * 2.0 / d_rot))
    ang = pos.astype(F32)[:, None] * inv_freq[None, :]
    cos = jnp.cos(ang)
    sin = jnp.sin(ang)
    t = pos.shape[0]
    z = jnp.zeros((t, group - d_rot), F32)
    c = jnp.concatenate([cos, cos, jnp.ones((t, group - d_rot), F32)], axis=1)
    s1 = jnp.concatenate([-sin, jnp.zeros((t, half), F32), z], axis=1)
    s2 = jnp.concatenate([jnp.zeros((t, half), F32), sin, z], axis=1)
    rep = LANES // group
    return jnp.concatenate([jnp.tile(c, (1, rep)), jnp.tile(s1, (1, rep)), jnp.tile(s2, (1, rep))], axis=1)


def _attention_prep(proj, pos, tq, q_norm_w, k_norm_w, idx_k_norm_w, transpose_v):
    n = proj.shape[0]
    p = pos.shape[0]
    g = p // tq
    tabm = _rope_tables(pos, HEAD_DIM, LANES).reshape(g, tq, 3 * LANES)
    tabi = _rope_tables(pos, IDX_DIM, IDX_DIM).reshape(g, tq, 3 * LANES)
    iw = jnp.concatenate([idx_k_norm_w, jnp.zeros((LANES - IDX_DIM,), F32)]).reshape(1, LANES)
    row = lambda w, c: pl.BlockSpec((tq, w), lambda i: (i, c // w))
    tab_spec = pl.BlockSpec((1, tq, 3 * LANES), lambda i: (i % g, 0, 0))
    vec_spec = pl.BlockSpec((1, LANES), lambda i: (0, 0))
    out_row = lambda w: pl.BlockSpec((tq, w), lambda i: (i, 0))
    v_spec = pl.BlockSpec((KV_WIDTH, tq), lambda i: (0, i)) if transpose_v else out_row(KV_WIDTH)
    v_shape = (KV_WIDTH, n) if transpose_v else (n, KV_WIDTH)
    return pl.pallas_call(
        functools.partial(_prep_kernel, transpose_v=transpose_v),
        name="attn_prep",
        grid=(n // tq,),
        in_specs=[row(ATTN_WIDTH, C_QA), row(IDX_WIDTH, C_QI), row(KV_WIDTH, C_KA), row(KV_WIDTH, C_VA),
                  row(LANES, C_MISC), tab_spec, tab_spec, vec_spec, vec_spec, vec_spec],
        out_specs=[out_row(ATTN_WIDTH), out_row(KV_WIDTH), out_row(KV_WIDTH), v_spec,
                   out_row(IDX_WIDTH), out_row(IDX_DIM), out_row(LANES)],
        out_shape=[jax.ShapeDtypeStruct((n, ATTN_WIDTH), BF16),
                   jax.ShapeDtypeStruct((n, KV_WIDTH), F32),
                   jax.ShapeDtypeStruct((n, KV_WIDTH), BF16),
                   jax.ShapeDtypeStruct(v_shape, BF16),
                   jax.ShapeDtypeStruct((n, IDX_WIDTH), BF16),
                   jax.ShapeDtypeStruct((n, IDX_DIM), F32),
                   jax.ShapeDtypeStruct((n, LANES), BF16)],
        compiler_params=_cparams(("parallel",)),
    )(proj, proj, proj, proj, proj, tabm, tabi,
      q_norm_w.reshape(1, LANES), k_norm_w.reshape(1, LANES), iw)


def _sort_key(x):
    b = pltpu.bitcast(x + 0.0, I32)
    return b ^ ((b >> 31) & INT_MAX)


def _kth_largest_key(count_ge, k, shape, n_total):
    def body(it, carry):
        ans_u, n_ge = carry
        bit = jnp.left_shift(jnp.int32(1), 31 - it)
        cand_u = ans_u | bit
        cnt = count_ge(cand_u ^ INT_MIN)
        ok = cnt >= k
        return jnp.where(ok, cand_u, ans_u), jnp.where(ok, cnt, n_ge)

    ans_u, n_ge = lax.fori_loop(0, 32, body, (jnp.zeros(shape, I32), jnp.full(shape, float(n_total), F32)))
    return ans_u ^ INT_MIN, n_ge


def _tie_index_limit(count_eq_le, need, n_keys, shape):
    nbits = max(1, int(n_keys - 1).bit_length())

    def body(it, lo):
        bit = jnp.left_shift(jnp.int32(1), nbits - 1 - it)
        cand = lo | bit
        cnt = count_eq_le(cand - 1)
        return jnp.where(cnt >= need, lo, cand)

    return lax.fori_loop(0, nbits, body, jnp.zeros(shape, I32))


def _dsa_prompt_kernel(q_ref, qi_ref, misc_ref, k_ref, vt_ref, kx_ref, o_ref,
                       key_scr, qsel_scr, qg_scr, thr_scr, lim_scr, m_scr, l_scr, acc_scr, *, n_sel):
    i = pl.program_id(1)
    tq = Q_TILE
    ck = KEY_CHUNK
    n_ch = (i * tq + tq + ck - 1) // ck
    q_pos = i * tq + lax.broadcasted_iota(I32, (1, tq), 1)
    row_k = lax.broadcasted_iota(I32, (ck, 1), 0)

    lo_half = lax.broadcasted_iota(I32, (tq, LANES), 1) < IDX_DIM
    zero = jnp.zeros((), BF16)
    for p in range(IDX_WIDTH // LANES):
        slab = qi_ref[:, p * LANES:(p + 1) * LANES]
        qsel_scr[(2 * p) * tq:(2 * p + 1) * tq, :] = jnp.where(lo_half, slab, zero)
        qsel_scr[(2 * p + 1) * tq:(2 * p + 2) * tq, :] = jnp.where(lo_half, zero, slab)
    w_t = misc_ref[...].T

    def score_chunk(c, carry):
        off = pl.multiple_of(c * ck, ck)
        s = _dot_nt(kx_ref[pl.ds(off, ck), :], qsel_scr[...])
        acc = jnp.zeros((ck, tq), F32)
        for h in range(N_IDX_HEADS):
            acc = acc + w_t[M_WI + h:M_WI + h + 1, :] * jnp.maximum(s[:, h * tq:(h + 1) * tq], 0.0)
        acc = jnp.where(off + row_k <= q_pos, acc, -jnp.inf)
        key_scr[pl.ds(off, ck), :] = _sort_key(acc)
        return carry

    lax.fori_loop(0, n_ch, score_chunk, 0)

    spc = SEL_SPAN // ck
    n_span = (n_ch + spc - 1) // spc
    neg_key = jnp.full((ck, tq), NEG_INF_KEY, I32)

    def pad_chunk(c, carry):
        key_scr[pl.ds(pl.multiple_of(c * ck, ck), ck), :] = neg_key
        return carry

    lax.fori_loop(n_ch, n_span * spc, pad_chunk, 0)

    thr_scr[...] = jnp.full((1, tq), INT_MIN, I32)
    lim_scr[...] = jnp.full((1, tq), INT_MAX, I32)

    def select_threshold(n_keys):
        def count_where(pred):
            tot = jnp.zeros((SUBLANES, tq), F32)
            for c0 in range(0, n_keys, ck):
                hit = pred(key_scr[c0:c0 + ck, :], c0 + row_k).astype(F32)
                tot = tot + jnp.sum(hit.reshape(ck // SUBLANES, SUBLANES, tq), axis=0)
            return jnp.sum(tot, axis=0, keepdims=True)

        t, n_ge = _kth_largest_key(lambda cand: count_where(lambda kk, pos: kk >= cand), float(n_sel), (1, tq),
                                   n_keys)
        thr_scr[...] = t

        @pl.when(jnp.max(n_ge) > float(n_sel))
        def _():
            n_gt = count_where(lambda kk, pos: kk > t)
            lim_scr[...] = _tie_index_limit(
                lambda idx: count_where(lambda kk, pos: jnp.logical_and(kk == t, pos <= idx)),
                float(n_sel) - n_gt, k_ref.shape[0], (1, tq))

    for spans in range(1, k_ref.shape[0] // SEL_SPAN + 1):
        if spans * SEL_SPAN > n_sel:
            pl.when(jnp.logical_and(n_span == spans, (i + 1) * tq > n_sel))(
                functools.partial(select_threshold, spans * SEL_SPAN))

    thr = thr_scr[...]
    lim = lim_scr[...]

    for g in range(N_KV_HEADS):
        for r in range(KV_GROUP):
            h = g * KV_GROUP + r
            qg_scr[g, r * tq:(r + 1) * tq, :] = q_ref[:, h * HEAD_DIM:(h + 1) * HEAD_DIM]
    m_scr[...] = jnp.full(m_scr.shape, NEG_BIG, F32)
    l_scr[...] = jnp.zeros(l_scr.shape, F32)
    acc_scr[...] = jnp.zeros(acc_scr.shape, F32)

    def attend_chunk(c, carry):
        off = pl.multiple_of(c * ck, ck)
        kk = key_scr[pl.ds(off, ck), :]
        pos = off + row_k
        sel = jnp.logical_or(kk > thr, jnp.logical_and(kk == thr, pos <= lim))
        sel = jnp.logical_and(sel, pos <= q_pos)
        def group_steps(g):
            kc = k_ref[pl.ds(off, ck), g * HEAD_DIM:(g + 1) * HEAD_DIM]
            vt = vt_ref[g * HEAD_DIM:(g + 1) * HEAD_DIM, pl.ds(off, ck)]
            qk = _dot_nt(kc, qg_scr[g])
            yield
            s = jnp.concatenate([jnp.where(sel, qk[:, r * tq:(r + 1) * tq], NEG_BIG) for r in range(KV_GROUP)],
                                axis=1)
            m_old = m_scr[g]
            m_new = jnp.maximum(m_old, jnp.max(s, axis=0, keepdims=True))
            yield
            p = jnp.exp(s - m_new)
            alpha = jnp.exp(m_old - m_new)
            l_scr[g] = alpha * l_scr[g] + jnp.sum(p, axis=0, keepdims=True)
            yield
            acc_scr[g] = alpha * acc_scr[g] + _dot(vt, p.astype(BF16))
            m_scr[g] = m_new

        chains = [group_steps(g) for g in range(N_KV_HEADS)]
        while chains:
            chains = [ch for ch in chains if next(ch, "done") != "done"]
        return carry

    lax.fori_loop(0, n_ch, attend_chunk, 0)
    for g in range(N_KV_HEADS):
        o_t = acc_scr[g] / l_scr[g]
        for r in range(KV_GROUP):
            h = g * KV_GROUP + r
            o_ref[:, h * HEAD_DIM:(h + 1) * HEAD_DIM] = o_t[:, r * tq:(r + 1) * tq].T.astype(BF16)


def _dsa_prompt(q_bf, qi_bf, misc, k_bf, vt_bf, kx_bf, b, t):
    n = b * t
    nq = t // Q_TILE
    n_sel = min(TOPK_MAX, t // 4)
    qrow = lambda w: pl.BlockSpec((Q_TILE, w), lambda bb, i: (bb * nq + i, 0))
    seq = lambda w: pl.BlockSpec((t, w), lambda bb, i: (bb, 0))
    return pl.pallas_call(
        functools.partial(_dsa_prompt_kernel, n_sel=n_sel),
        name="dsa_prompt",
        grid=(b, nq),
        in_specs=[qrow(ATTN_WIDTH), qrow(IDX_WIDTH),
                  qrow(LANES),
                  seq(KV_WIDTH), pl.BlockSpec((KV_WIDTH, t), lambda bb, i: (0, bb)), seq(LANES)],
        out_specs=qrow(ATTN_WIDTH),
        out_shape=jax.ShapeDtypeStruct((n, ATTN_WIDTH), BF16),
        scratch_shapes=[pltpu.VMEM((t, Q_TILE), I32),
                        pltpu.VMEM((N_IDX_HEADS * Q_TILE, LANES), BF16),
                        pltpu.VMEM((N_KV_HEADS, KV_GROUP * Q_TILE, HEAD_DIM), BF16),
                        pltpu.VMEM((1, Q_TILE), I32),
                        pltpu.VMEM((1, Q_TILE), I32),
                        pltpu.VMEM((N_KV_HEADS, 1, KV_GROUP * Q_TILE), F32),
                        pltpu.VMEM((N_KV_HEADS, 1, KV_GROUP * Q_TILE), F32),
                        pltpu.VMEM((N_KV_HEADS, HEAD_DIM, KV_GROUP * Q_TILE), F32)],
        compiler_params=_cparams(("parallel", "arbitrary")),
    )(q_bf, qi_bf, misc, k_bf, vt_bf, kx_bf)


def _sample_score_kernel(pt_ref, q_ref, w_ref, kn_ref, *refs, pages, t_valid):
    page_refs = refs[:pages]
    past_ref, new_ref = refs[pages], refs[pages + 1]
    rows = SAMPLE_ROWS
    qr = SAMPLE_QROWS
    q = q_ref[0]
    w = w_ref[0]

    def head_sum(s):
        s = w * jnp.maximum(s, 0.0)
        acc = s[0:qr]
        for h in range(1, N_IDX_HEADS):
            acc = acc + s[h * qr:(h + 1) * qr]
        return acc

    for j in range(pages):
        kp = page_refs[j][0, 0].astype(BF16)
        past_ref[0, :, j * PAGE_SIZE:(j + 1) * PAGE_SIZE] = _sort_key(head_sum(_dot_nt(q, kp)))

    @pl.when(pl.program_id(1) == 0)
    def _():
        kn = jnp.concatenate([kn_ref[...], jnp.zeros((LANES - rows, IDX_DIM), F32)], axis=0).astype(BF16)
        sc = head_sum(_dot_nt(q, kn))
        t = lax.broadcasted_iota(I32, sc.shape, 0)
        s = lax.broadcasted_iota(I32, sc.shape, 1)
        ok = jnp.logical_and(s <= t, s < t_valid)
        new_ref[0] = _sort_key(jnp.where(ok, sc, -jnp.inf))


def _sample_scores(page_table, q_t, w_col, kif, cache_idx, layer, pages, t_valid):
    bs, n_pages = page_table.shape
    past = n_pages * PAGE_SIZE
    hr = N_IDX_HEADS * SAMPLE_QROWS
    page_spec = lambda j: pl.BlockSpec((1, 1, PAGE_SIZE, IDX_DIM),
                                       lambda b, c, pt: (layer, pt[b, c * pages + j], 0, 0))
    grid_spec = pltpu.PrefetchScalarGridSpec(
        num_scalar_prefetch=1,
        grid=(bs, n_pages // pages),
        in_specs=[pl.BlockSpec((1, hr, IDX_DIM), lambda b, c, pt: (b, 0, 0)),
                  pl.BlockSpec((1, hr, 1), lambda b, c, pt: (b, 0, 0)),
                  pl.BlockSpec((SAMPLE_ROWS, IDX_DIM), lambda b, c, pt: (b, 0))]
                 + [page_spec(j) for j in range(pages)],
        out_specs=[pl.BlockSpec((1, SAMPLE_QROWS, pages * PAGE_SIZE), lambda b, c, pt: (b, 0, c)),
                   pl.BlockSpec((1, SAMPLE_QROWS, LANES), lambda b, c, pt: (b, 0, 0))],
    )
    return pl.pallas_call(
        functools.partial(_sample_score_kernel, pages=pages, t_valid=t_valid),
        name="sample_scores",
        grid_spec=grid_spec,
        out_shape=[jax.ShapeDtypeStruct((bs, SAMPLE_QROWS, past), I32),
                   jax.ShapeDtypeStruct((bs, SAMPLE_QROWS, LANES), I32)],
        compiler_params=_cparams(("parallel", "arbitrary")),
    )(page_table, q_t, w_col, kif, *([cache_idx] * pages))


def _sample_attend_kernel(pt_ref, kp_ref, kn_ref, q_ref, knew_ref, vnew_ref, *refs, pages, n_sel, past):
    k_pages = refs[:pages]
    v_pages = refs[pages:2 * pages]
    o_ref = refs[2 * pages]
    thr_scr, lim_scr, m_scr, l_scr, acc_scr = refs[2 * pages + 1:]
    c = pl.program_id(1)
    rows = SAMPLE_ROWS
    span = pages * PAGE_SIZE

    @pl.when(c == 0)
    def _():
        m_scr[...] = jnp.full(m_scr.shape, NEG_BIG, F32)
        l_scr[...] = jnp.zeros(l_scr.shape, F32)
        acc_scr[...] = jnp.zeros(acc_scr.shape, F32)
        vr = SUBLANES
        tile_pos = lax.broadcasted_iota(I32, (vr, LANES), 1)

        def count_where(pred):
            acc = pred(kn_ref[0, 0:vr, :], past + tile_pos).astype(F32)
            for tix in range(past // LANES):
                acc = acc + pred(kp_ref[0, 0:vr, tix * LANES:(tix + 1) * LANES],
                                 tix * LANES + tile_pos).astype(F32)
            return jnp.sum(acc, axis=1, keepdims=True)

        t, n_ge = _kth_largest_key(lambda cand: count_where(lambda kk, pos: kk >= cand), float(n_sel), (vr, 1),
                                   past + LANES)
        thr_scr[...] = jnp.full((rows, 1), INT_MIN, I32)
        lim_scr[...] = jnp.full((rows, 1), INT_MAX, I32)
        thr_scr[0:vr, :] = t

        @pl.when(jnp.max(n_ge) > float(n_sel))
        def _():
            n_gt = count_where(lambda kk, pos: kk > t)
            lim_scr[0:vr, :] = _tie_index_limit(
                lambda idx: count_where(lambda kk, pos: jnp.logical_and(kk == t, pos <= idx)),
                float(n_sel) - n_gt, past + LANES, (vr, 1))

    thr = thr_scr[0:SAMPLE_QROWS, :]
    lim = lim_scr[0:SAMPLE_QROWS, :]
    all_rows = lambda m: jnp.concatenate([m] * (rows // SAMPLE_QROWS), axis=0)

    def update_steps(g, qg, k_fn, v_fn, sel):
        sel = jnp.concatenate([sel] * KV_GROUP, axis=0)
        qk = _dot_nt(qg, k_fn())
        yield
        s = jnp.where(sel, qk, NEG_BIG)
        m_old = m_scr[g]
        m_new = jnp.maximum(m_old, jnp.max(s, axis=1, keepdims=True))
        yield
        p = jnp.where(sel, jnp.exp(s - m_new), 0.0)
        alpha = jnp.exp(m_old - m_new)
        l_scr[g] = alpha * l_scr[g] + jnp.sum(p, axis=1, keepdims=True)
        yield
        acc_scr[g] = alpha * acc_scr[g] + _dot(p.astype(BF16), v_fn())
        m_scr[g] = m_new

    def run_lockstep(chains):
        while chains:
            chains = [ch for ch in chains if next(ch, "done") != "done"]

    def select(kk, pos):
        return jnp.logical_or(kk > thr, jnp.logical_and(kk == thr, pos <= lim))

    def page_cat(page_refs, g):
        return jnp.concatenate([r[0, 0, pl.ds(g, PAGE_SIZE, stride=N_KV_HEADS), :] for r in page_refs],
                               axis=0).astype(BF16)

    off = pl.multiple_of(c * span, span)
    kk = kp_ref[0, :, pl.ds(off, span)]
    sel_past = all_rows(select(kk, off + lax.broadcasted_iota(I32, kk.shape, 1)))
    q_groups = []
    for g in range(N_KV_HEADS):
        qg = jnp.concatenate(
            [q_ref[:, (g * KV_GROUP + r) * HEAD_DIM:(g * KV_GROUP + r + 1) * HEAD_DIM] for r in range(KV_GROUP)],
            axis=0)
        q_groups.append(qg)
    run_lockstep([update_steps(g, q_groups[g], functools.partial(page_cat, k_pages, g),
                               functools.partial(page_cat, v_pages, g), sel_past) for g in range(N_KV_HEADS)])

    @pl.when(c == pl.num_programs(1) - 1)
    def _():
        kn = kn_ref[0]
        lane = lax.broadcasted_iota(I32, kn.shape, 1)
        sel_new = all_rows(jnp.logical_and(select(kn, past + lane), lane < rows))
        pad = jnp.zeros((LANES - rows, KV_WIDTH), BF16)
        k_new = jnp.concatenate([knew_ref[...], pad], axis=0)
        v_new = jnp.concatenate([vnew_ref[...], pad], axis=0)
        head = lambda a, g: (lambda: a[:, g * HEAD_DIM:(g + 1) * HEAD_DIM])
        run_lockstep([update_steps(g, q_groups[g], head(k_new, g), head(v_new, g), sel_new)
                      for g in range(N_KV_HEADS)])
        for g in range(N_KV_HEADS):
            o = acc_scr[g] / l_scr[g]
            for r in range(KV_GROUP):
                h = g * KV_GROUP + r
                o_ref[:, h * HEAD_DIM:(h + 1) * HEAD_DIM] = o[r * rows:(r + 1) * rows].astype(BF16)


def _sample_attend(page_table, keys_past, keys_new, q_bf, k_bf, v_bf, cache_k, cache_v, layer, pages, n_sel):
    bs, n_pages = page_table.shape
    past = n_pages * PAGE_SIZE
    depth, pool = cache_k.shape[0], cache_k.shape[1]
    cache_k = cache_k.reshape(depth, pool, PAGE_SIZE * N_KV_HEADS, HEAD_DIM)
    cache_v = cache_v.reshape(depth, pool, PAGE_SIZE * N_KV_HEADS, HEAD_DIM)
    page_spec = lambda j: pl.BlockSpec((1, 1, PAGE_SIZE * N_KV_HEADS, HEAD_DIM),
                                       lambda b, c, pt: (layer, pt[b, c * pages + j], 0, 0))
    row = lambda w: pl.BlockSpec((SAMPLE_ROWS, w), lambda b, c, pt: (b, 0))
    grid_spec = pltpu.PrefetchScalarGridSpec(
        num_scalar_prefetch=1,
        grid=(bs, n_pages // pages),
        in_specs=[pl.BlockSpec((1, SAMPLE_QROWS, past), lambda b, c, pt: (b, 0, 0)),
                  pl.BlockSpec((1, SAMPLE_QROWS, LANES), lambda b, c, pt: (b, 0, 0)),
                  row(ATTN_WIDTH), row(KV_WIDTH), row(KV_WIDTH)]
                 + [page_spec(j) for j in range(pages)] * 2,
        out_specs=row(ATTN_WIDTH),
        scratch_shapes=[pltpu.VMEM((SAMPLE_ROWS, 1), I32),
                        pltpu.VMEM((SAMPLE_ROWS, 1), I32),
                        pltpu.VMEM((N_KV_HEADS, KV_GROUP * SAMPLE_ROWS, 1), F32),
                        pltpu.VMEM((N_KV_HEADS, KV_GROUP * SAMPLE_ROWS, 1), F32),
                        pltpu.VMEM((N_KV_HEADS, KV_GROUP * SAMPLE_ROWS, HEAD_DIM), F32)],
    )
    return pl.pallas_call(
        functools.partial(_sample_attend_kernel, pages=pages, n_sel=n_sel, past=past),
        name="sample_attend",
        grid_spec=grid_spec,
        out_shape=jax.ShapeDtypeStruct((bs * SAMPLE_ROWS, ATTN_WIDTH), BF16),
        compiler_params=_cparams(("parallel", "arbitrary")),
    )(page_table, keys_past, keys_new, q_bf, k_bf, v_bf, *([cache_k] * pages), *([cache_v] * pages))


def _delta_prep_kernel(x_ref, halo_ref, prev_ref, misc_ref, cw_ref, al_ref, dt_ref,
                       qn_ref, kn_ref, vv_ref, bg_ref, xp_scr, *, tiles_per_seq, t_valid, tt):
    i = pl.program_id(0)
    tile_in_seq = i % tiles_per_seq
    halo = jnp.where(tile_in_seq == 0, prev_ref[0], halo_ref[...])
    xp_scr[0:SUBLANES, :] = halo
    xp_scr[SUBLANES:SUBLANES + tt, :] = x_ref[...]
    base = SUBLANES - (CONV_WIDTH - 1)
    outs = (qn_ref, kn_ref, vv_ref)
    for sec in range(3):
        for h in range(N_DELTA_HEADS):
            col = sec * DELTA_WIDTH + h * HEAD_DIM
            sl = slice(col, col + HEAD_DIM)
            y = xp_scr[base:base + tt, sl] * cw_ref[0:1, sl]
            for j in range(1, CONV_WIDTH):
                y = y + xp_scr[base + j:base + j + tt, sl] * cw_ref[j:j + 1, sl]
            y = _silu(y)
            if sec < 2:
                y = y * lax.rsqrt(jnp.sum(y * y, axis=-1, keepdims=True) + EPS)
            if sec == 0:
                y = y * (HEAD_DIM ** -0.5)
            outs[sec][:, h * HEAD_DIM:(h + 1) * HEAD_DIM] = y
    m = misc_ref[...]
    lane = lax.broadcasted_iota(I32, m.shape, 1)
    row = tile_in_seq * tt + lax.broadcasted_iota(I32, m.shape, 0)
    beta = _sigmoid(m)
    g = -jnp.exp(al_ref[...]) * _softplus(m + dt_ref[...])
    is_b = jnp.logical_and(lane >= M_BD, lane < M_BD + N_DELTA_HEADS)
    is_g = jnp.logical_and(lane >= M_AD, lane < M_AD + N_DELTA_HEADS)
    comb = jnp.where(is_b, beta, jnp.where(is_g, g, 0.0))
    comb = jnp.where(row < t_valid, comb, 0.0)
    bg_ref[...] = pltpu.roll(comb, LANES - M_BD, 1)


def _delta_prep(proj, prev8, conv_w, a_log, dt_bias, b, t, tt, t_valid):
    n = proj.shape[0]
    tiles_per_seq = t // tt
    pad_vec = lambda v: jnp.zeros((1, LANES), F32).at[0, M_AD:M_AD + N_DELTA_HEADS].set(v)
    halo_blocks = tt // SUBLANES
    return pl.pallas_call(
        functools.partial(_delta_prep_kernel, tiles_per_seq=tiles_per_seq, t_valid=t_valid, tt=tt),
        name="delta_prep",
        grid=(n // tt,),
        in_specs=[pl.BlockSpec((tt, CONV_CHANNELS), lambda i: (i, C_CONV // CONV_CHANNELS)),
                  pl.BlockSpec((SUBLANES, CONV_CHANNELS),
                               lambda i: (jnp.maximum(i * halo_blocks - 1, 0), C_CONV // CONV_CHANNELS)),
                  pl.BlockSpec((1, SUBLANES, CONV_CHANNELS), lambda i: (i // tiles_per_seq, 0, 0)),
                  pl.BlockSpec((tt, LANES), lambda i: (i, C_MISC // LANES)),
                  pl.BlockSpec((CONV_WIDTH, CONV_CHANNELS), lambda i: (0, 0)),
                  pl.BlockSpec((1, LANES), lambda i: (0, 0)),
                  pl.BlockSpec((1, LANES), lambda i: (0, 0))],
        out_specs=[pl.BlockSpec((tt, DELTA_WIDTH), lambda i: (i, 0))] * 3
                  + [pl.BlockSpec((tt, LANES), lambda i: (i, 0))],
        out_shape=[jax.ShapeDtypeStruct((n, DELTA_WIDTH), F32)] * 3 + [jax.ShapeDtypeStruct((n, LANES), F32)],
        scratch_shapes=[pltpu.VMEM((SUBLANES + tt, CONV_CHANNELS), F32)],
        compiler_params=_cparams(("parallel",)),
    )(proj, proj, prev8, proj, conv_w, pad_vec(a_log), pad_vec(dt_bias))


def _mm(a, b):
    return _dot(a.astype(BF16), b.astype(BF16))


def _mm_nt(a, b):
    return _dot_nt(a.astype(BF16), b.astype(BF16))


DELTA_INV_BLOCK = 16
DELTA_STACK = 4
DELTA_CHUNKS_PER_STEP = 2


def _delta_chunk_kernel(qn_ref, kn_ref, vv_ref, bg_ref, z_ref, s0_ref, ow_ref, od_ref, so_ref, s_scr, *, n_chunks):
    c = pl.program_id(1)
    cs = DELTA_CHUNK

    @pl.when(c == 0)
    def _():
        s_scr[...] = s0_ref[0]

    ltri = (lax.broadcasted_iota(I32, (cs, cs), 0) >= lax.broadcasted_iota(I32, (cs, cs), 1)).astype(BF16)

    def chunk_gates(ch):
        bg = bg_ref[ch * cs:(ch + 1) * cs, :]
        g1 = bg.astype(BF16)
        r1 = bg - g1.astype(F32)
        g2 = r1.astype(BF16)
        g3 = (r1 - g2.astype(F32)).astype(BF16)
        gc = _dot(ltri, g1) + _dot(ltri, g2) + _dot(ltri, g3)
        return bg, gc, gc.T

    gates = [chunk_gates(ch) for ch in range(n_chunks)]
    state_ready = {}

    gh = DELTA_STACK
    rows = gh * cs
    rr = lax.broadcasted_iota(I32, (rows, rows), 0)
    cc = lax.broadcasted_iota(I32, (rows, rows), 1)
    same = (rr // cs) == (cc // cs)
    causal = jnp.logical_and(same, rr >= cc)
    strict = jnp.logical_and(same, rr > cc)
    eye = (rr == cc).astype(F32)
    row_head = lax.broadcasted_iota(I32, (rows, 1), 0) // cs
    def group_steps(ch, grp):
        heads = [grp * gh + j for j in range(gh)]
        bg, gc, gct = gates[ch]
        r0 = ch * cs
        stack = lambda ref: jnp.concatenate([ref[r0:r0 + cs, h * HEAD_DIM:(h + 1) * HEAD_DIM] for h in heads],
                                            axis=0)
        col = lambda a, lane0: jnp.concatenate([a[:, lane0 + h:lane0 + h + 1] for h in heads], axis=0)
        k = stack(kn_ref)
        q = stack(qn_ref)
        v = stack(vv_ref)
        bcol = col(bg, 0)
        gcc = col(gc, N_DELTA_HEADS)
        gcr = jnp.concatenate([gct[N_DELTA_HEADS + h:N_DELTA_HEADS + h + 1, :] for h in heads], axis=1)
        g_last = [gc[cs - 1:cs, N_DELTA_HEADS + h:N_DELTA_HEADS + h + 1] for h in heads]
        glc = jnp.concatenate([jnp.broadcast_to(gl, (cs, 1)) for gl in g_last], axis=0)
        decay = jnp.exp(jnp.where(causal, gcc - gcr, -jnp.inf))
        kb = k * bcol
        eg = jnp.exp(gcc)
        kq = _mm_nt(jnp.concatenate([kb, q], axis=0), k)
        yield
        a = jnp.where(strict, kq[0:rows] * decay, 0.0)
        intra = jnp.where(causal, kq[rows:2 * rows] * decay, 0.0)
        x = -a
        nb = DELTA_INV_BLOCK
        y = jnp.where((rr // nb) == (cc // nb), x, 0.0)
        p = eye + y
        y = _mm(y, y)
        yield
        n_sq = max(1, int(nb - 1).bit_length())
        for lvl in range(1, n_sq):
            if lvl < n_sq - 1:
                py = _mm(jnp.concatenate([p, y], axis=0), y)
                p = p + py[0:rows]
                y = py[rows:2 * rows]
            else:
                p = p + _mm(p, y)
            yield
        size = 2 * nb
        while size <= cs:
            off = jnp.where(jnp.logical_and((rr // size) == (cc // size), (rr // (size // 2)) != (cc // (size // 2))),
                            x, 0.0)
            po = _mm(p, off)
            yield
            p = p + _mm(po, p)
            yield
            size *= 2
        sol = _mm(p, jnp.concatenate([v * bcol, kb * eg], axis=1))
        yield
        u = sol[:, 0:HEAD_DIM]
        w = sol[:, HEAD_DIM:2 * HEAD_DIM]
        lanes_g = slice(grp * gh * HEAD_DIM, (grp + 1) * gh * HEAD_DIM)
        while ch > 0 and not state_ready.get((ch - 1, grp)):
            yield
        s_g = s_scr[:, lanes_g]
        wq_s = _mm(jnp.concatenate([w, q * eg], axis=0), s_g)
        yield
        own = lambda m, r0: jnp.concatenate(
            [m[r0 + j * cs:r0 + (j + 1) * cs, j * HEAD_DIM:(j + 1) * HEAD_DIM] for j in range(gh)], axis=0)
        v_new = u - own(wq_s, 0)
        o = own(wq_s, rows) + _mm(intra, v_new)
        yield
        kg_t = (k * jnp.exp(glc - gcc)).T
        vn_blocks = jnp.concatenate([jnp.where(row_head == j, v_new, 0.0) for j in range(gh)], axis=1)
        s_decay = jnp.concatenate([jnp.broadcast_to(jnp.exp(gl), (1, HEAD_DIM)) for gl in g_last], axis=1)
        s_scr[:, lanes_g] = s_g * s_decay + _mm(kg_t, vn_blocks)
        state_ready[(ch, grp)] = True
        yield
        on = o * lax.rsqrt(jnp.mean(o * o, axis=-1, keepdims=True) + EPS) * ow_ref[...]
        for j, h in enumerate(heads):
            sl = slice(h * HEAD_DIM, (h + 1) * HEAD_DIM)
            od_ref[r0:r0 + cs, sl] = (on[j * cs:(j + 1) * cs] * _silu(z_ref[r0:r0 + cs, sl])).astype(BF16)

    chains = [group_steps(ch, grp) for ch in range(n_chunks) for grp in range(N_DELTA_HEADS // gh)]
    while chains:
        chains = [g for g in chains if next(g, "done") != "done"]

    so_ref[0] = s_scr[...]


def _delta_chunks(qn, kn, vv, bg, zsrc, z_col_block, state0, o_norm_w, b, t):
    n = b * t
    per_step = DELTA_CHUNKS_PER_STEP if (t // DELTA_CHUNK) % DELTA_CHUNKS_PER_STEP == 0 else 1
    nc = t // (DELTA_CHUNK * per_step)
    sw = N_DELTA_HEADS * HEAD_DIM
    row = lambda w, cb=0: pl.BlockSpec((DELTA_CHUNK * per_step, w), lambda bb, c: (bb * nc + c, cb))
    st = pl.BlockSpec((1, HEAD_DIM, sw), lambda bb, c: (bb, 0, 0))
    s_in = state0.transpose(0, 2, 1, 3).reshape(b, HEAD_DIM, sw)
    od, s_out = pl.pallas_call(
        functools.partial(_delta_chunk_kernel, n_chunks=per_step),
        name="delta_chunks",
        grid=(b, nc),
        in_specs=[row(DELTA_WIDTH), row(DELTA_WIDTH), row(DELTA_WIDTH), row(LANES),
                  row(DELTA_WIDTH, z_col_block), st, pl.BlockSpec((1, LANES), lambda bb, c: (0, 0))],
        out_specs=[row(DELTA_WIDTH), st],
        out_shape=[jax.ShapeDtypeStruct((n, DELTA_WIDTH), BF16),
                   jax.ShapeDtypeStruct((b, HEAD_DIM, sw), F32)],
        scratch_shapes=[pltpu.VMEM((HEAD_DIM, sw), F32)],
        compiler_params=_cparams(("parallel", "arbitrary")),
    )(qn, kn, vv, bg, zsrc, s_in, o_norm_w.reshape(1, LANES))
    return od, s_out.reshape(b, HEAD_DIM, N_DELTA_HEADS, HEAD_DIM).transpose(0, 2, 1, 3)


def _outproj_kernel(oap_ref, odp_ref, xp_ref, g1p_ref, shp_ref, scp_ref,
                    oas_ref, ods_ref, xs_ref, g1s_ref, shs_ref, scs_ref,
                    nw_ref, wo_ref, wrh_ref, br_ref, x1_ref, h2_ref, lg_ref, *, prompt_tiles):
    def body(oa_ref, od_ref, x_ref, g1_ref, sh_ref, sc_ref):
        mix = (_dot(oa_ref[...], wo_ref[0:ATTN_WIDTH, :])
               + _dot(od_ref[...], wo_ref[ATTN_WIDTH:ATTN_WIDTH + DELTA_WIDTH, :]))
        x1 = x_ref[...] + g1_ref[0] * mix
        x1_ref[...] = x1
        y = x1 * lax.rsqrt(jnp.mean(x1 * x1, axis=-1, keepdims=True) + EPS) * nw_ref[...]
        h2 = y * (1.0 + sc_ref[0]) + sh_ref[0]
        h2_ref[...] = h2
        hb = h2.astype(BF16)
        lo = (h2 - hb.astype(F32)).astype(BF16)
        wr = wrh_ref[...]
        both = _dot(hb, wr)
        lg_ref[...] = both[:, 0:LANES] + both[:, LANES:2 * LANES] + _dot(lo, wr[:, 0:LANES]) + br_ref[...]

    i = pl.program_id(0)
    pl.when(i < prompt_tiles)(functools.partial(body, oap_ref, odp_ref, xp_ref, g1p_ref, shp_ref, scp_ref))
    pl.when(i >= prompt_tiles)(functools.partial(body, oas_ref, ods_ref, xs_ref, g1s_ref, shs_ref, scs_ref))


def _out_projection(prompt, sample, norm2_w, w_out_bf, wr_both, b_rt, tm, prompt_seq_len):
    n_p, d = prompt[2].shape
    n_s = sample[2].shape[0]
    pt = n_p // tm
    tiles_per_seq = prompt_seq_len // tm
    p_row = lambda w: pl.BlockSpec((tm, w), lambda i: (jnp.minimum(i, pt - 1), 0))
    s_row = lambda w: pl.BlockSpec((tm, w), lambda i: (jnp.maximum(i - pt, 0), 0))
    p_mod = pl.BlockSpec((1, 1, d), lambda i: (jnp.minimum(i, pt - 1) // tiles_per_seq, 0, 0))
    s_mod = pl.BlockSpec((1, tm, d), lambda i: (jnp.maximum(i - pt, 0), 0, 0))
    row = lambda w: pl.BlockSpec((tm, w), lambda i: (i, 0))
    full = lambda a: pl.BlockSpec(a.shape, lambda i: (0, 0))
    n = n_p + n_s
    return pl.pallas_call(
        functools.partial(_outproj_kernel, prompt_tiles=pt),
        name="out_proj",
        grid=(n // tm,),
        in_specs=[p_row(ATTN_WIDTH), p_row(DELTA_WIDTH), p_row(d), p_mod, p_mod, p_mod,
                  s_row(ATTN_WIDTH), s_row(DELTA_WIDTH), s_row(d), s_mod, s_mod, s_mod,
                  pl.BlockSpec((1, d), lambda i: (0, 0)), full(w_out_bf), full(wr_both), full(b_rt)],
        out_specs=[row(d), row(d), row(LANES)],
        out_shape=[jax.ShapeDtypeStruct((n, d), F32), jax.ShapeDtypeStruct((n, d), F32),
                   jax.ShapeDtypeStruct((n, LANES), F32)],
        compiler_params=_cparams(("parallel",)),
    )(*prompt, *sample, norm2_w.reshape(1, d), w_out_bf, wr_both, b_rt)


def _route_tile(x):
    lane = lax.broadcasted_iota(I32, x.shape, 1)
    gl = jnp.where(lane < N_GROUPS, x, -jnp.inf)
    ge = jnp.exp(gl - jnp.max(gl, axis=1, keepdims=True))
    p = ge / jnp.sum(ge, axis=1, keepdims=True)
    p_max = jnp.max(p, axis=1, keepdims=True)
    grp = jnp.min(jnp.where(p == p_max, lane, LANES), axis=1, keepdims=True)
    e_lane = lane - N_GROUPS
    in_grp = jnp.logical_and(jnp.logical_and(e_lane >= 0, e_lane < N_EXPERTS),
                             (e_lane >> 3) == grp)
    rl = jnp.where(in_grp, x, -jnp.inf)
    v1 = jnp.max(rl, axis=1, keepdims=True)
    i1 = jnp.min(jnp.where(rl == v1, lane, LANES), axis=1, keepdims=True)
    rl2 = jnp.where(lane == i1, -jnp.inf, rl)
    v2 = jnp.max(rl2, axis=1, keepdims=True)
    i2 = jnp.min(jnp.where(rl2 == v2, lane, LANES), axis=1, keepdims=True)
    t = jnp.exp(v2 - v1)
    den = 1.0 + t
    eid = jnp.where(lane == 0, i1 - N_GROUPS, jnp.where(lane == 1, i2 - N_GROUPS, 0))
    gate = jnp.where(lane == 0, (1.0 / den) * p_max, jnp.where(lane == 1, (t / den) * p_max, 0.0))
    return eid, gate


def _route_kernel(lg_ref, eid_ref, gate_ref):
    eid_ref[...], gate_ref[...] = _route_tile(lg_ref[...])


def _route(logits, tm):
    n = logits.shape[0]
    spec = pl.BlockSpec((tm, LANES), lambda i: (i, 0))
    return pl.pallas_call(
        _route_kernel,
        name="route",
        grid=(n // tm,),
        in_specs=[spec],
        out_specs=[spec, spec],
        out_shape=[jax.ShapeDtypeStruct((n, LANES), I32), jax.ShapeDtypeStruct((n, LANES), F32)],
        compiler_params=_cparams(("parallel",)),
    )(logits)


def _row_gather(idx_ref, base, n_rows, src_hbm, dst, sem):
    def body(r, carry):
        pltpu.make_async_copy(src_hbm.at[pl.ds(idx_ref[base + r], 1), :], dst.at[pl.ds(r, 1), :], sem).start()
        return carry

    lax.fori_loop(0, n_rows, body, 0, unroll=8)


def _row_gather_wait(n_rows, src_hbm, dst, sem):
    pltpu.make_async_copy(src_hbm.at[pl.ds(0, n_rows), :], dst, sem).wait()


def _moe_kernel(tok_ref, j0_ref, be_ref, na_ref, h_hbm, wg_ref, wu_ref, wd_ref, o_ref,
                x_even, x_odd, sem, wg_scr, wu_scr, wd_scr, *, bm):
    i = pl.program_id(0)
    n_act = na_ref[0]
    bufs = ((x_even, sem.at[0]), (x_odd, sem.at[1]))

    @pl.when(i == 0)
    def _():
        _row_gather(tok_ref, j0_ref[0], bm, h_hbm, x_even, sem.at[0])

    changed = jnp.logical_or(i == 0, be_ref[i] != be_ref[jnp.maximum(i - 1, 0)])

    @pl.when(jnp.logical_and(i < n_act, changed))
    def _():
        wg_scr[...] = wg_ref[0].astype(BF16)
        wu_scr[...] = wu_ref[0].astype(BF16)
        wd_scr[...] = wd_ref[0].astype(BF16)

    for parity in range(2):
        cur, cur_sem = bufs[parity]
        nxt, nxt_sem = bufs[1 - parity]

        @pl.when(jnp.logical_and(i < n_act, i % 2 == parity))
        def _():
            _row_gather_wait(bm, h_hbm, cur, cur_sem)
            base = j0_ref[i + 1]
            for r in range(bm):
                pltpu.make_async_copy(h_hbm.at[pl.ds(tok_ref[base + r], 1), :], nxt.at[pl.ds(r, 1), :],
                                      nxt_sem).start()
            x = cur[...].astype(BF16)
            hid = _silu(_dot(x, wg_scr[...])) * _dot(x, wu_scr[...])
            o_ref[...] = _dot(hid.astype(BF16), wd_scr[...])

        @pl.when(jnp.logical_and(i == n_act, i % 2 == parity))
        def _():
            _row_gather_wait(bm, h_hbm, cur, cur_sem)

    @pl.when(i >= n_act)
    def _():
        o_ref[...] = jnp.zeros(o_ref.shape, F32)


def _moe_experts(tok_sorted, block_j0, block_exp, n_active, h2, w_gate, w_up, w_down, bm):
    ns = block_exp.shape[0] * bm
    d = h2.shape[1]
    f = w_gate.shape[2]
    grid_spec = pltpu.PrefetchScalarGridSpec(
        num_scalar_prefetch=4,
        grid=(ns // bm,),
        in_specs=[pl.BlockSpec(memory_space=pl.ANY),
                  pl.BlockSpec((1, d, f), lambda i, tok, j0, be, na: (be[i], 0, 0)),
                  pl.BlockSpec((1, d, f), lambda i, tok, j0, be, na: (be[i], 0, 0)),
                  pl.BlockSpec((1, f, d), lambda i, tok, j0, be, na: (be[i], 0, 0))],
        out_specs=pl.BlockSpec((bm, d), lambda i, tok, j0, be, na: (i, 0)),
        scratch_shapes=[pltpu.VMEM((bm, d), F32), pltpu.VMEM((bm, d), F32), pltpu.SemaphoreType.DMA((2,)),
                        pltpu.VMEM((d, f), BF16), pltpu.VMEM((d, f), BF16), pltpu.VMEM((f, d), BF16)],
    )
    return pl.pallas_call(
        functools.partial(_moe_kernel, bm=bm),
        name="moe_experts",
        grid_spec=grid_spec,
        out_shape=jax.ShapeDtypeStruct((ns, d), F32),
        compiler_params=_cparams(("arbitrary",)),
    )(tok_sorted, block_j0, block_exp, n_active, h2, w_gate, w_up, w_down)


def _combine_kernel(dest_ref, x1_ref, gt_ref, g2p_ref, g2s_ref, y_hbm, op_ref, os_ref, y_buf, sem, *,
                    tm, prompt_tiles):
    i = pl.program_id(0)
    n = pl.num_programs(0)
    slot = i % 2

    @pl.when(i == 0)
    def _():
        _row_gather(dest_ref, 0, 2 * tm, y_hbm, y_buf.at[0], sem.at[0])

    for parity in range(2):
        @pl.when(jnp.logical_and(i + 1 < n, slot == parity))
        def _():
            base = (i + 1) * 2 * tm
            for r in range(2 * tm):
                pltpu.make_async_copy(y_hbm.at[pl.ds(dest_ref[base + r], 1), :],
                                      y_buf.at[1 - parity, pl.ds(r, 1), :], sem.at[1 - parity]).start()

    _row_gather_wait(2 * tm, y_hbm, y_buf.at[slot], sem.at[slot])
    gt = gt_ref[...]
    y = y_buf[slot, 0:tm, :] * gt[:, 0:1] + y_buf[slot, tm:2 * tm, :] * gt[:, 1:2]

    @pl.when(i < prompt_tiles)
    def _():
        op_ref[...] = x1_ref[...] + g2p_ref[0] * y

    @pl.when(i >= prompt_tiles)
    def _():
        os_ref[...] = x1_ref[...] + g2s_ref[0] * y


def _combine(x1, dest, y_rows, gates, gate2_p, gate2_s, tm, n_prompt, prompt_seq_len):
    n, d = x1.shape
    pt = n_prompt // tm
    tiles_per_seq = prompt_seq_len // tm
    dest_tiles = dest.reshape(n // tm, tm, 2).transpose(0, 2, 1).reshape(-1)
    row = lambda w: pl.BlockSpec((tm, w), lambda i, dst: (i, 0))
    grid_spec = pltpu.PrefetchScalarGridSpec(
        num_scalar_prefetch=1,
        grid=(n // tm,),
        in_specs=[row(d), row(LANES),
                  pl.BlockSpec((1, 1, d), lambda i, dst: (jnp.minimum(i, pt - 1) // tiles_per_seq, 0, 0)),
                  pl.BlockSpec((1, tm, d), lambda i, dst: (jnp.maximum(i - pt, 0), 0, 0)),
                  pl.BlockSpec(memory_space=pl.ANY)],
        out_specs=[pl.BlockSpec((tm, d), lambda i, dst: (jnp.minimum(i, pt - 1), 0)),
                   pl.BlockSpec((tm, d), lambda i, dst: (jnp.maximum(i - pt, 0), 0))],
        scratch_shapes=[pltpu.VMEM((2, 2 * tm, d), F32), pltpu.SemaphoreType.DMA((2,))],
    )
    return pl.pallas_call(
        functools.partial(_combine_kernel, tm=tm, prompt_tiles=pt),
        name="moe_combine",
        grid_spec=grid_spec,
        out_shape=[jax.ShapeDtypeStruct((n_prompt, d), F32), jax.ShapeDtypeStruct((n - n_prompt, d), F32)],
        compiler_params=_cparams(("arbitrary",)),
    )(dest_tiles, x1, gates, gate2_p, gate2_s, y_rows)


FRONT_SECTIONS = ((C_QA, ATTN_WIDTH), (C_QI, IDX_WIDTH),
                  (C_CONV, DELTA_WIDTH), (C_CONV + DELTA_WIDTH, DELTA_WIDTH), (C_CONV + 2 * DELTA_WIDTH, DELTA_WIDTH),
                  (C_KA, PROJ_PACKED - C_KA), (C_ZD, DELTA_WIDTH))


def _prompt_front_kernel(x_ref, sh_ref, sc_ref, nw_ref, w_ref, tabm_ref, tabi_ref, qw_ref, kw_ref, iw_ref,
                         prev_ref, cw_ref, al_ref, dt_ref,
                         q_ref, qi_ref, z_ref, qn_ref, kn_ref, vv_ref, kf_ref, kb_ref, vf_ref, vt_ref,
                         kif_ref, kx_ref, bg_ref, misc_ref, tail_ref,
                         carry_scr, xp_scr, *, tiles_per_seq, tm):
    i = pl.program_id(0)
    tile_in_seq = i % tiles_per_seq
    x = x_ref[...]
    y = x * lax.rsqrt(jnp.mean(x * x, axis=-1, keepdims=True) + EPS) * nw_ref[...]
    h = (y * (1.0 + sc_ref[0]) + sh_ref[0]).astype(BF16)
    tabm = tabm_ref[0]
    tabi = tabi_ref[0]
    half_main = HEAD_DIM // ROPE_FRACTION // 2
    half_idx = IDX_DIM // ROPE_FRACTION // 2

    @pl.when(tile_in_seq == 0)
    def _():
        carry_scr[...] = prev_ref[0]

    def queries(pj):
        for hh in range(N_ATTN_HEADS):
            sl = slice(hh * HEAD_DIM, (hh + 1) * HEAD_DIM)
            v = _rope(_rms_head(pj[:, sl], qw_ref[...]), tabm, half_main)
            q_ref[:, sl] = (v * (HEAD_DIM ** -0.5)).astype(BF16)

    def index_queries(pj):
        for p in range(IDX_WIDTH // LANES):
            sl = slice(p * LANES, (p + 1) * LANES)
            qi_ref[:, sl] = _rope(pj[:, sl], tabi, half_idx).astype(BF16)

    def gate_z(pj):
        z_ref[...] = pj

    def conv_section(sec, pj):
        cols = slice(sec * DELTA_WIDTH, (sec + 1) * DELTA_WIDTH)
        xp_scr[sec, 0:SUBLANES, :] = carry_scr[:, cols]
        xp_scr[sec, SUBLANES:SUBLANES + tm, :] = pj
        carry_scr[:, cols] = pj[tm - SUBLANES:tm, :]
        base = SUBLANES - (CONV_WIDTH - 1)
        out = (qn_ref, kn_ref, vv_ref)[sec]
        for hh in range(N_DELTA_HEADS):
            sl = slice(hh * HEAD_DIM, (hh + 1) * HEAD_DIM)
            wsl = slice(sec * DELTA_WIDTH + hh * HEAD_DIM, sec * DELTA_WIDTH + (hh + 1) * HEAD_DIM)
            v = xp_scr[sec, base:base + tm, sl] * cw_ref[0:1, wsl]
            for j in range(1, CONV_WIDTH):
                v = v + xp_scr[sec, base + j:base + j + tm, sl] * cw_ref[j:j + 1, wsl]
            v = _silu(v)
            if sec < 2:
                v = v * lax.rsqrt(jnp.sum(v * v, axis=-1, keepdims=True) + EPS)
            if sec == 0:
                v = v * (HEAD_DIM ** -0.5)
            out[:, sl] = v
        if sec == 2:
            tail_ref[0] = carry_scr[...]

    def keys_values_misc(pj):
        for hk in range(N_KV_HEADS):
            sl = slice(hk * HEAD_DIM, (hk + 1) * HEAD_DIM)
            v = _rope(_rms_head(pj[:, sl], kw_ref[...]), tabm, half_main)
            kf_ref[:, sl] = v
            kb_ref[:, sl] = v.astype(BF16)
        va = pj[:, KV_WIDTH:2 * KV_WIDTH]
        vf_ref[...] = va
        vt_ref[...] = va.T.astype(BF16)
        m = pj[:, 2 * KV_WIDTH:2 * KV_WIDTH + LANES]
        misc_ref[...] = m
        lane = lax.broadcasted_iota(I32, m.shape, 1)
        ki = jnp.where(lane < IDX_DIM, m, 0.0)
        ms = jnp.sum(ki * ki, axis=-1, keepdims=True) * (1.0 / IDX_DIM)
        v = _rope(ki * lax.rsqrt(ms + EPS) * iw_ref[...], tabi, half_idx)
        kif_ref[...] = v[:, 0:IDX_DIM]
        kx_ref[...] = (v + pltpu.roll(v, IDX_DIM, 1)).astype(BF16)
        beta = _sigmoid(m)
        g = -jnp.exp(al_ref[...]) * _softplus(m + dt_ref[...])
        is_b = jnp.logical_and(lane >= M_BD, lane < M_BD + N_DELTA_HEADS)
        is_g = jnp.logical_and(lane >= M_AD, lane < M_AD + N_DELTA_HEADS)
        comb = jnp.where(is_b, beta, jnp.where(is_g, g, 0.0))
        bg_ref[...] = pltpu.roll(comb, LANES - M_BD, 1)

    epilogues = (queries, index_queries,
                 functools.partial(conv_section, 0), functools.partial(conv_section, 1),
                 functools.partial(conv_section, 2), keys_values_misc, gate_z)
    project = lambda k: _dot(h, w_ref[:, FRONT_SECTIONS[k][0]:FRONT_SECTIONS[k][0] + FRONT_SECTIONS[k][1]])
    pj_next = project(0)
    for k, epilogue in enumerate(epilogues):
        pj = pj_next
        if k + 1 < len(epilogues):
            pj_next = project(k + 1)
        epilogue(pj)


def _prompt_front(x2d, shift, scale, norm_w, w_packed, pos, q_norm_w, k_norm_w, idx_k_norm_w, prev8, conv_w,
                  a_log, dt_bias, b, t, tm):
    n, d = x2d.shape
    np_ = w_packed.shape[1]
    tiles_per_seq = t // tm
    tabm = _rope_tables(pos, HEAD_DIM, LANES).reshape(tiles_per_seq, tm, 3 * LANES)
    tabi = _rope_tables(pos, IDX_DIM, IDX_DIM).reshape(tiles_per_seq, tm, 3 * LANES)
    iw = jnp.concatenate([idx_k_norm_w, jnp.zeros((LANES - IDX_DIM,), F32)]).reshape(1, LANES)
    pad_vec = lambda v: jnp.zeros((1, LANES), F32).at[0, M_AD:M_AD + N_DELTA_HEADS].set(v)
    mod_spec = pl.BlockSpec((1, 1, d), lambda i: (i // tiles_per_seq, 0, 0))
    tab_spec = pl.BlockSpec((1, tm, 3 * LANES), lambda i: (i % tiles_per_seq, 0, 0))
    vec = lambda w: pl.BlockSpec((1, w), lambda i: (0, 0))
    row = lambda w: pl.BlockSpec((tm, w), lambda i: (i, 0))
    seq_state = pl.BlockSpec((1, SUBLANES, CONV_CHANNELS), lambda i: (i // tiles_per_seq, 0, 0))
    widths = [(ATTN_WIDTH, BF16), (IDX_WIDTH, BF16), (DELTA_WIDTH, F32), (DELTA_WIDTH, F32), (DELTA_WIDTH, F32),
              (DELTA_WIDTH, F32), (KV_WIDTH, F32), (KV_WIDTH, BF16), (KV_WIDTH, F32)]
    outs = pl.pallas_call(
        functools.partial(_prompt_front_kernel, tiles_per_seq=tiles_per_seq, tm=tm),
        name="prompt_front",
        grid=(n // tm,),
        in_specs=[row(d), mod_spec, mod_spec, vec(d),
                  pl.BlockSpec((d, np_), lambda i: (0, 0), pipeline_mode=pl.Buffered(1)),
                  tab_spec, tab_spec, vec(LANES), vec(LANES), vec(LANES), seq_state,
                  pl.BlockSpec((CONV_WIDTH, CONV_CHANNELS), lambda i: (0, 0)), vec(LANES), vec(LANES)],
        out_specs=[row(w) for w, _ in widths]
                  + [pl.BlockSpec((KV_WIDTH, tm), lambda i: (0, i)), row(IDX_DIM), row(LANES), row(LANES), row(LANES),
                     seq_state],
        out_shape=[jax.ShapeDtypeStruct((n, w), dt) for w, dt in widths]
                  + [jax.ShapeDtypeStruct((KV_WIDTH, n), BF16), jax.ShapeDtypeStruct((n, IDX_DIM), F32),
                     jax.ShapeDtypeStruct((n, LANES), BF16), jax.ShapeDtypeStruct((n, LANES), F32),
                     jax.ShapeDtypeStruct((n, LANES), F32),
                     jax.ShapeDtypeStruct((b, SUBLANES, CONV_CHANNELS), F32)],
        scratch_shapes=[pltpu.VMEM((SUBLANES, CONV_CHANNELS), F32),
                        pltpu.VMEM((3, SUBLANES + tm, DELTA_WIDTH), F32)],
        compiler_params=pltpu.CompilerParams(dimension_semantics=("arbitrary",),
                                             vmem_limit_bytes=FRONT_VMEM_LIMIT),
    )(x2d, shift, scale, norm_w.reshape(1, d), w_packed, tabm, tabi,
      q_norm_w.reshape(1, LANES), k_norm_w.reshape(1, LANES), iw, prev8, conv_w, pad_vec(a_log), pad_vec(dt_bias))
    names = ("q", "qi", "z", "qn", "kn", "vv", "kf", "kb", "vf", "vt", "kif", "kx", "bg", "misc", "tail")
    return dict(zip(names, outs))


def _pick_tile(n, pref, mult=16):
    t = min(pref, n)
    while n % t or t % mult:
        t -= 1
    return t


def _pack_w_in(w_in):
    d = w_in.shape[0]
    bounds = np.cumsum(PROJ_SIZES)[:-1].tolist()
    qa, ka, va, qi, ki, wi, qd, kd, vd, zd, bd, ad = jnp.split(w_in, bounds, axis=1)
    used = IDX_DIM + N_IDX_HEADS + 2 * N_DELTA_HEADS
    misc = jnp.concatenate([ki, wi, bd, ad, jnp.zeros((d, LANES - used), w_in.dtype)], axis=1)
    cols = [qa, qi, zd, qd, kd, vd, ka, va, misc]
    width = sum(c.shape[1] for c in cols)
    cols.append(jnp.zeros((d, PROJ_PACKED - width), w_in.dtype))
    return jnp.concatenate(cols, axis=1).astype(BF16)


def _route_and_sort(eid, bm):
    n = eid.shape[0]
    nk = 2 * n
    flat_e = eid.reshape(-1)
    order = jnp.argsort(flat_e, stable=True).astype(I32)
    inv = jnp.argsort(order).astype(I32)
    onehot = flat_e[:, None] == jnp.arange(N_EXPERTS, dtype=I32)[None, :]
    counts = jnp.sum(onehot.astype(I32), axis=0)
    padded = (counts + bm - 1) // bm * bm
    pad_end = jnp.cumsum(padded)
    shift = (pad_end - padded) - (jnp.cumsum(counts) - counts)
    dest = inv + jnp.sum(jnp.where(onehot, shift[None, :], 0), axis=1)
    n_blocks = -(-nk // bm) + N_EXPERTS + 1
    block_exp = jnp.minimum(jnp.searchsorted(pad_end, jnp.arange(n_blocks, dtype=I32) * bm, side='right'),
                            N_EXPERTS - 1).astype(I32)
    n_active = (pad_end[-1] // bm).astype(I32).reshape(1)
    block_j0 = jnp.clip(jnp.arange(n_blocks, dtype=I32) * bm - shift[block_exp], 0, nk)
    tok_sorted = jnp.concatenate([order // 2, jnp.zeros((bm,), I32)])
    return tok_sorted, block_j0, dest.astype(I32).reshape(n, 2), block_exp, n_active


def _layer(layer, yp, ys, cache_k, cache_v, cache_idx, state_ssm, state_conv, page_table, c_prompt, c_sample,
           w_in, w_out, conv_w, a_log, dt_bias, q_norm_w, k_norm_w, idx_k_norm_w, o_norm_w, norm1_w, norm2_w,
           w_ada, b_ada, w_group, b_group, w_router, b_router, w_gate, w_up, w_down):
    bp, tp, d = yp.shape
    bs, ts, _ = ys.shape
    past = page_table.shape[1] * PAGE_SIZE
    rows = SAMPLE_ROWS
    assert CONV_WIDTH - 1 <= ts <= SUBLANES <= rows and tp % SEL_SPAN == 0 and tp % DELTA_CHUNK == 0

    n_c = bp + bs
    n_c_pad = -(-n_c // SUBLANES) * SUBLANES
    c_all = jnp.concatenate([c_prompt, c_sample, jnp.zeros((n_c_pad - n_c, d), F32)], axis=0)
    mod = _ada_modulation(c_all, w_ada, b_ada)
    mods = jnp.split(mod, N_MOD, axis=1)
    mp = [m[:bp].reshape(bp, 1, d) for m in mods]
    ms = [jnp.repeat(m[bp:bp + bs], rows, axis=0).reshape(1, bs * rows, d) for m in mods]

    w_packed = _pack_w_in(w_in)
    w_out_bf = w_out.astype(BF16)
    w_rt = jnp.concatenate([w_group, w_router, jnp.zeros((d, LANES - N_GROUPS - N_EXPERTS), F32)], axis=1)
    wr_hi = w_rt.astype(BF16)
    wr_both = jnp.concatenate([wr_hi, (w_rt - wr_hi.astype(F32)).astype(BF16)], axis=1)
    b_rt = jnp.concatenate([b_group, b_router, jnp.zeros((LANES - N_GROUPS - N_EXPERTS,), F32)]).reshape(1, LANES)

    np_ = bp * tp
    xp2 = yp.reshape(np_, d)
    tm_p = _pick_tile(tp, 256)
    fr = _prompt_front(xp2, mp[0], mp[1], norm1_w, w_packed, jnp.arange(tp), q_norm_w, k_norm_w, idx_k_norm_w,
                       jnp.zeros((bp, SUBLANES, CONV_CHANNELS), F32), conv_w, a_log, dt_bias, bp, tp, tm_p)
    kf_p, kif_p = fr["kf"], fr["kif"]
    oa_p = _dsa_prompt(fr["q"], fr["qi"], fr["misc"], fr["kb"], fr["vt"], fr["kx"], bp, tp)
    od_p, ssm_p = _delta_chunks(fr["qn"], fr["kn"], fr["vv"], fr["bg"], fr["z"], 0,
                                jnp.zeros((bp, N_DELTA_HEADS, HEAD_DIM, HEAD_DIM), F32), o_norm_w, bp, tp)

    ns_ = bs * rows
    xs2 = jnp.pad(ys, ((0, 0), (0, rows - ts), (0, 0))).reshape(ns_, d)
    tm_s = _pick_tile(ns_, 256)
    proj_s = _in_projection(xs2, ms[0].reshape(ns_ // tm_s, tm_s, d), ms[1].reshape(ns_ // tm_s, tm_s, d),
                            norm1_w, w_packed, tm_s, tm_s)
    q_s, kf_s, kb_s, vb_s, qi_s, kif_s, _ = _attention_prep(
        proj_s, past + jnp.arange(rows), rows, q_norm_w, k_norm_w, idx_k_norm_w, False)
    qr = SAMPLE_QROWS
    q_t = qi_s.reshape(bs, rows, N_IDX_HEADS, IDX_DIM)[:, :qr].transpose(0, 2, 1, 3)
    q_t = q_t.reshape(bs, N_IDX_HEADS * qr, IDX_DIM)
    w_col = proj_s[:, C_MISC + M_WI:C_MISC + M_WI + N_IDX_HEADS].reshape(bs, rows, N_IDX_HEADS)[:, :qr]
    w_col = w_col.transpose(0, 2, 1).reshape(bs, N_IDX_HEADS * qr, 1)
    pages = _pick_tile(page_table.shape[1], SAMPLE_PAGES_PER_STEP, 1)
    keys_past, keys_new = _sample_scores(page_table, q_t, w_col, kif_s, cache_idx, layer, pages, ts)
    n_sel_s = min(TOPK_MAX, (past + ts) // 4)
    oa_s = _sample_attend(page_table, keys_past, keys_new, q_s, kb_s, vb_s, cache_k, cache_v, layer, pages, n_sel_s)
    prev8 = jnp.pad(state_conv, ((0, 0), (SUBLANES - (CONV_WIDTH - 1), 0), (0, 0)))
    qn_s, kn_s, vv_s, bg_s = _delta_prep(proj_s, prev8, conv_w, a_log, dt_bias, bs, rows, rows, ts)
    to_chunk = lambda a: jnp.pad(a.reshape(bs, rows, -1), ((0, 0), (0, DELTA_CHUNK - rows), (0, 0))).reshape(
        bs * DELTA_CHUNK, -1)
    z_s = proj_s[:, C_ZD:C_ZD + DELTA_WIDTH]
    od_s, ssm_s = _delta_chunks(to_chunk(qn_s), to_chunk(kn_s), to_chunk(vv_s), to_chunk(bg_s), to_chunk(z_s), 0,
                                state_ssm, o_norm_w, bs, DELTA_CHUNK)
    od_s = od_s.reshape(bs, DELTA_CHUNK, DELTA_WIDTH)[:, :rows].reshape(ns_, DELTA_WIDTH)

    tm_o = _pick_tile(tp, tm_s)
    assert ns_ % tm_o == 0
    per_tok = lambda m: m.reshape(ns_ // tm_o, tm_o, d)
    n_all = np_ + ns_
    x1_all, h2_all, lg_all = _out_projection(
        (oa_p, od_p, xp2, mp[2], mp[3], mp[4]),
        (oa_s, od_s, xs2, per_tok(ms[2]), per_tok(ms[3]), per_tok(ms[4])),
        norm2_w, w_out_bf, wr_both, b_rt, tm_o, tp)
    eid, gates = _route(lg_all, _pick_tile(n_all, 512, SUBLANES))
    bm = 256
    tok_sorted, block_j0, dest, block_exp, n_active = _route_and_sort(eid[:, 0:2], bm)
    yb = _moe_experts(tok_sorted, block_j0, block_exp, n_active, h2_all, w_gate, w_up, w_down, bm)
    out_p, out_s = _combine(x1_all, dest, yb, gates, mp[5], per_tok(ms[5]), tm_o, np_, tp)

    valid = lambda a: a.reshape(bs, rows, -1)[:, :ts]
    conv_p = fr["tail"][:, SUBLANES - (CONV_WIDTH - 1):]
    conv_s = proj_s.reshape(bs, rows, PROJ_PACKED)[:, ts - (CONV_WIDTH - 1):ts, C_CONV:C_CONV + CONV_CHANNELS]
    return (out_p.reshape(bp, tp, d), valid(out_s),
            kf_p.reshape(bp, tp, N_KV_HEADS, HEAD_DIM),
            fr["vf"].reshape(bp, tp, N_KV_HEADS, HEAD_DIM),
            kif_p.reshape(bp, tp, IDX_DIM), ssm_p, conv_p,
            valid(kf_s).reshape(bs, ts, N_KV_HEADS, HEAD_DIM),
            valid(proj_s[:, C_VA:C_VA + KV_WIDTH]).reshape(bs, ts, N_KV_HEADS, HEAD_DIM),
            valid(kif_s), ssm_s, conv_s)


def kernel(x_prompt, x_sample, cache_k, cache_v, cache_idx_k, state_ssm, state_conv, page_table, c_prompt, c_sample,
           w_in, w_out, conv_w, a_log, dt_bias, q_norm_w, k_norm_w, idx_k_norm_w, o_norm_w, norm1_w, norm2_w,
           w_ada, b_ada, w_group, b_group, w_router, b_router, w_gate, w_up, w_down):
    depth = w_in.shape[0]
    yp, ys = x_prompt, x_sample
    per_layer = []
    for l in range(depth):
        res = _layer(l, yp, ys, cache_k, cache_v, cache_idx_k, state_ssm[l], state_conv[l], page_table,
                     c_prompt, c_sample, w_in[l], w_out[l], conv_w[l], a_log[l], dt_bias[l], q_norm_w[l],
                     k_norm_w[l], idx_k_norm_w[l], o_norm_w[l], norm1_w[l], norm2_w[l], w_ada[l], b_ada[l],
                     w_group[l], b_group[l], w_router[l], b_router[l], w_gate[l], w_up[l], w_down[l])
        yp, ys = res[0], res[1]
        per_layer.append(res[2:])
    stacked = tuple(jnp.stack([pl_[j] for pl_ in per_layer]) for j in range(10))
    return (yp, ys) + stacked
```
